```python
import jax, jax.numpy as jnp
from jax import lax
import numpy as np

D_MODEL = 1024
BATCH = 2
SEQ = 8192
DEPTH = 2

GRID_W = 64
HEAD_DIM = 64
NA_HEADS = 4
NA_ROWS_MAX = 8
NA_COLS = 16
SWA_Q_HEADS = 8
SWA_KV_HEADS = 2
SWA_WINDOW = 128
SWA_BLOCK = 128
POOL_WINDOWS = (2, 4, 8, 16)
POOL_GROUP_DIM = 64
D_A = NA_HEADS * HEAD_DIM
D_B = SWA_Q_HEADS * HEAD_DIM
D_BKV = SWA_KV_HEADS * HEAD_DIM
D_C = len(POOL_WINDOWS) * POOL_GROUP_DIM
D_MIX = D_A + D_B + D_C
D_IN = 3 * D_A + D_B + 2 * D_BKV + D_C
N_GROUPS = 4
EXPERTS_PER_GROUP = 8
N_EXPERTS = N_GROUPS * EXPERTS_PER_GROUP
TOP_K_INNER = 2
D_EXPERT = 256
RMS_EPS = 1e-6

kernel_name = 'hybrid_natten_swa_pool_hmoe_encoder'


def rmsnorm(x, g):
    xf = x.astype(jnp.float32)
    y = xf * lax.rsqrt(jnp.mean(xf * xf, axis=-1, keepdims=True) + RMS_EPS)
    return (y * g.astype(jnp.float32)).astype(x.dtype)


def alibi_slopes(n):
    return (2.0 ** (-8.0 * np.arange(1, n + 1) / n)).astype(np.float32)


def neighbourhood_attention(q, k, v, rel_bias):
    B, S, H, Dh = q.shape
    rows = S // GRID_W
    kr = min(NA_ROWS_MAX, rows)

    def to_grid(a):
        return a.reshape(B, rows, GRID_W, H, Dh).transpose(0, 3, 1, 2, 4)

    qg, kg, vg = to_grid(q * (Dh ** -0.5)), to_grid(k), to_grid(v)
    cols = np.arange(GRID_W)
    col_start = np.clip(cols - NA_COLS // 2, 0, GRID_W - NA_COLS)
    col_idx = col_start[:, None] + np.arange(NA_COLS)[None, :]
    dc_idx = col_idx - cols[:, None] + (NA_COLS - 1)
    bias_c = rel_bias.astype(jnp.float32)[:, :, dc_idx]

    def row_step(r):
        rs = jnp.clip(r - kr // 2, 0, rows - kr)
        q_r = lax.dynamic_index_in_dim(qg, r, axis=2, keepdims=False)
        k_r = lax.dynamic_slice_in_dim(kg, rs, kr, axis=2)[:, :, :, col_idx]
        v_r = lax.dynamic_slice_in_dim(vg, rs, kr, axis=2)[:, :, :, col_idx]
        dr_idx = rs + jnp.arange(kr) - r + (NA_ROWS_MAX - 1)
        bias = jnp.take(bias_c, dr_idx, axis=1).transpose(0, 2, 1, 3)
        s = jnp.einsum('bhcd,bhrcjd->bhcrj', q_r, k_r).astype(jnp.float32) + bias
        p = jax.nn.softmax(s.reshape(B, H, GRID_W, kr * NA_COLS), axis=-1)
        p = p.reshape(s.shape).astype(v.dtype)
        return jnp.einsum('bhcrj,bhrcjd->bhcd', p, v_r)

    o = lax.map(row_step, jnp.arange(rows))
    return o.transpose(1, 0, 3, 2, 4).reshape(B, S, H * Dh)


def sliding_window_gqa(q, k, v, sink):
    B, S, Hq, Dh = q.shape
    Hkv = k.shape[2]
    rep = Hq // Hkv
    nb = S // SWA_BLOCK
    qb = (q * (Dh ** -0.5)).reshape(B, nb, SWA_BLOCK, Hkv, rep, Dh)
    pad = ((0, 0), (SWA_BLOCK, SWA_BLOCK), (0, 0), (0, 0))
    kp, vp = jnp.pad(k, pad), jnp.pad(v, pad)

    def band(a):
        return jnp.concatenate(
            [a[:, i * SWA_BLOCK: i * SWA_BLOCK + S].reshape(B, nb, SWA_BLOCK, Hkv, Dh) for i in range(3)],
            axis=2)

    kb, vb = band(kp), band(vp)
    qi = np.arange(SWA_BLOCK)
    ki = np.arange(3 * SWA_BLOCK)
    diff = ki[None, :] - qi[:, None] - SWA_BLOCK
    kpos = (np.arange(nb)[:, None] - 1) * SWA_BLOCK + ki[None, :]
    valid = (np.abs(diff) <= SWA_WINDOW)[None] & ((kpos >= 0) & (kpos < S))[:, None, :]
    slopes = jnp.asarray(alibi_slopes(Hq)).reshape(Hkv, rep)
    alibi = -slopes[:, :, None, None] * jnp.asarray(np.abs(diff).astype(np.float32))
    s = jnp.einsum('bnqgrd,bnkgd->bngrqk', qb, kb).astype(jnp.float32)
    s = jnp.where(jnp.asarray(valid)[None, :, None, None], s + alibi, -jnp.inf)
    sink_logit = jnp.broadcast_to(sink.astype(jnp.float32).reshape(Hkv, rep, 1, 1),
                                  (B, nb, Hkv, rep, SWA_BLOCK, 1))
    p = jax.nn.softmax(jnp.concatenate([s, sink_logit], axis=-1), axis=-1)[..., :-1].astype(v.dtype)
    o = jnp.einsum('bngrqk,bnkgd->bnqgrd', p, vb)
    return o.reshape(B, S, Hq * Dh)


def multiscale_pool(u, pool_w, pool_scale):
    B, S, C = u.shape
    uf = u.astype(jnp.float32)
    csum = jnp.concatenate([jnp.zeros((B, 1, C), jnp.float32), jnp.cumsum(uf, axis=1)], axis=1)
    t = np.arange(S)
    outs = []
    for gi, w in enumerate(POOL_WINDOWS):
        sl = slice(gi * POOL_GROUP_DIM, (gi + 1) * POOL_GROUP_DIM)
        lo = np.clip(t - w // 2, 0, S)
        hi = np.clip(t + w // 2, 0, S)
        c = csum[:, :, sl]
        cnt = jnp.asarray((hi - lo).astype(np.float32))[None, :, None]
        d = ((c[:, hi] - c[:, lo]) / cnt - uf[:, :, sl]).astype(u.dtype)
        outs.append(d @ pool_w[gi])
    return jnp.concatenate(outs, axis=-1) * pool_scale


def hierarchical_moe(h, rg_w, rg_b, re_w, re_b, e_gate, e_up, e_down):
    B, S, D = h.shape
    t = h.reshape(-1, D)
    g_prob = jax.nn.softmax((t @ rg_w).astype(jnp.float32) + rg_b.astype(jnp.float32), axis=-1)
    g_top_p, g_top = lax.top_k(g_prob, 1)
    e_logits = ((t @ re_w).astype(jnp.float32) + re_b.astype(jnp.float32)).reshape(-1, N_GROUPS, EXPERTS_PER_GROUP)
    e_sel = jnp.take_along_axis(e_logits, g_top[:, :, None], axis=1)[:, 0]
    e_top_p, e_top = lax.top_k(jax.nn.softmax(e_sel, axis=-1), TOP_K_INNER)
    gates = g_top_p * e_top_p / jnp.sum(e_top_p, axis=-1, keepdims=True)
    expert_id = g_top * EXPERTS_PER_GROUP + e_top
    combine = jnp.sum(jax.nn.one_hot(expert_id, N_EXPERTS, dtype=jnp.float32) * gates[..., None], axis=1)
    combine = combine.astype(t.dtype)
    out = jnp.zeros_like(t)
    for e in range(N_EXPERTS):
        y = (jax.nn.silu(t @ e_gate[e]) * (t @ e_up[e])) @ e_down[e]
        out = out + combine[:, e:e + 1] * y
    return out.reshape(B, S, D)


def hybrid_layer(x, norm1_g, w_in, nat_bias, swa_sink, pool_w, pool_scale, w_out,
                 norm2_g, rg_w, rg_b, re_w, re_b, e_gate, e_up, e_down):
    B, S, _ = x.shape
    h = rmsnorm(x, norm1_g)
    proj = h @ w_in
    splits = [int(v) for v in np.cumsum([D_A, D_A, D_A, D_B, D_BKV, D_BKV])]
    aq, ak, av, bq, bk, bv, cu = jnp.split(proj, splits, axis=-1)
    na = lambda a: a.reshape(B, S, NA_HEADS, HEAD_DIM)
    o_a = neighbourhood_attention(na(aq), na(ak), na(av), nat_bias)
    o_b = sliding_window_gqa(bq.reshape(B, S, SWA_Q_HEADS, HEAD_DIM),
                             bk.reshape(B, S, SWA_KV_HEADS, HEAD_DIM),
                             bv.reshape(B, S, SWA_KV_HEADS, HEAD_DIM), swa_sink)
    o_c = multiscale_pool(cu, pool_w, pool_scale)
    x = x + jnp.concatenate([o_a, o_b, o_c], axis=-1) @ w_out
    x = x + hierarchical_moe(rmsnorm(x, norm2_g), rg_w, rg_b, re_w, re_b, e_gate, e_up, e_down)
    return x


def setup_inputs(seed: int = 0) -> dict:
    key = jax.random.key(seed)
    ks = jax.random.split(key, 17)
    n = jax.random.normal
    f32 = jnp.float32
    return {
        'x': n(ks[0], (BATCH, SEQ, D_MODEL), f32),
        'norm1_g': 1.0 + 0.05 * n(ks[1], (DEPTH, D_MODEL), f32),
        'w_in': n(ks[2], (DEPTH, D_MODEL, D_IN), f32) * D_MODEL ** -0.5,
        'nat_bias': 0.2 * n(ks[3], (DEPTH, NA_HEADS, 2 * NA_ROWS_MAX - 1, 2 * NA_COLS - 1), f32),
        'swa_sink': n(ks[4], (DEPTH, SWA_Q_HEADS), f32),
        'pool_w': n(ks[5], (DEPTH, len(POOL_WINDOWS), POOL_GROUP_DIM, POOL_GROUP_DIM), f32) * POOL_GROUP_DIM ** -0.5,
        'pool_scale': 1.0 + 0.1 * n(ks[6], (DEPTH, D_C), f32),
        'w_out': n(ks[7], (DEPTH, D_MIX, D_MODEL), f32) * D_MIX ** -0.5,
        'norm2_g': 1.0 + 0.05 * n(ks[8], (DEPTH, D_MODEL), f32),
        'router_g_w': n(ks[9], (DEPTH, D_MODEL, N_GROUPS), f32) * D_MODEL ** -0.5,
        'router_g_b': 0.01 * n(ks[10], (DEPTH, N_GROUPS), f32),
        'router_e_w': n(ks[11], (DEPTH, D_MODEL, N_EXPERTS), f32) * D_MODEL ** -0.5,
        'router_e_b': 0.01 * n(ks[12], (DEPTH, N_EXPERTS), f32),
        'expert_w_gate': n(ks[13], (DEPTH, N_EXPERTS, D_MODEL, D_EXPERT), f32) * D_MODEL ** -0.5,
        'expert_w_up': n(ks[14], (DEPTH, N_EXPERTS, D_MODEL, D_EXPERT), f32) * D_MODEL ** -0.5,
        'expert_w_down': n(ks[15], (DEPTH, N_EXPERTS, D_EXPERT, D_MODEL), f32) * D_EXPERT ** -0.5,
        'final_g': 1.0 + 0.05 * n(ks[16], (D_MODEL,), f32),
    }


def reference(x, norm1_g, w_in, nat_bias, swa_sink, pool_w, pool_scale, w_out, norm2_g,
              router_g_w, router_g_b, router_e_w, router_e_b, expert_w_gate, expert_w_up,
              expert_w_down, final_g):
    for l in range(DEPTH):
        x = hybrid_layer(x, norm1_g[l], w_in[l], nat_bias[l], swa_sink[l], pool_w[l], pool_scale[l],
                         w_out[l], norm2_g[l], router_g_w[l], router_g_b[l], router_e_w[l],
                         router_e_b[l], expert_w_gate[l], expert_w_up[l], expert_w_down[l])
    return rmsnorm(x, final_g)
```

```python
import functools

import jax
import jax.numpy as jnp
import numpy as np
from jax import lax
from jax.experimental import pallas as pl
from jax.experimental.pallas import tpu as pltpu

D_MODEL = 1024
GRID_W = 64
HEAD_DIM = 64
NA_HEADS = 4
NA_ROWS = 8
NA_COLS = 16
SWA_Q_HEADS = 8
SWA_KV_HEADS = 2
SWA_REP = SWA_Q_HEADS // SWA_KV_HEADS
SWA_WINDOW = 128
SWA_BLOCK = 128
POOL_WINDOWS = (2, 4, 8, 16)
POOL_GROUP_DIM = 64
D_A = NA_HEADS * HEAD_DIM
D_B = SWA_Q_HEADS * HEAD_DIM
D_BKV = SWA_KV_HEADS * HEAD_DIM
D_C = len(POOL_WINDOWS) * POOL_GROUP_DIM
D_MIX = D_A + D_B + D_C
D_QP = D_A + D_B
D_KV = 2 * D_A + 2 * D_BKV + D_C
D_IN = D_QP + D_KV
N_GROUPS = 4
EXPERTS_PER_GROUP = 8
N_EXPERTS = N_GROUPS * EXPERTS_PER_GROUP
D_EXPERT = 256
RMS_EPS = 1e-6
NEG = -1e30

LANES = 128
SUBLANES = 8
ROW_CHUNKS = D_MODEL // LANES

TOK_TILE = 512
EXP_TILE = 256
ROUTE_LANE0 = 8
HALO = 8
VMEM_LIMIT = 56 * 1024 * 1024


def _rmsnorm_f32(x, g):
    return x * lax.rsqrt(jnp.mean(x * x, axis=-1, keepdims=True) + RMS_EPS) * g


def _norm_proj_kernel(x_ref, g_ref, w_ref, qp_ref, kv_ref):
    xn = _rmsnorm_f32(x_ref[...], g_ref[...]).astype(jnp.bfloat16)
    proj = jnp.dot(xn, w_ref[...], preferred_element_type=jnp.float32)
    qp_ref[...] = proj[:, :D_QP].astype(jnp.bfloat16)
    kv_ref[...] = proj[:, D_QP:].astype(jnp.bfloat16)


def _norm_proj(x2, g, w):
    T = x2.shape[0]
    return pl.pallas_call(
        _norm_proj_kernel,
        grid=(T // TOK_TILE,),
        in_specs=[
            pl.BlockSpec((TOK_TILE, D_MODEL), lambda i: (i, 0)),
            pl.BlockSpec((1, D_MODEL), lambda i: (0, 0)),
            pl.BlockSpec((D_MODEL, D_IN), lambda i: (0, 0)),
        ],
        out_specs=[
            pl.BlockSpec((TOK_TILE, D_QP), lambda i: (i, 0)),
            pl.BlockSpec((TOK_TILE, D_KV), lambda i: (i, 0)),
        ],
        out_shape=[
            jax.ShapeDtypeStruct((T, D_QP), jnp.bfloat16),
            jax.ShapeDtypeStruct((T, D_KV), jnp.bfloat16),
        ],
        compiler_params=pltpu.CompilerParams(
            dimension_semantics=("arbitrary",), vmem_limit_bytes=VMEM_LIMIT),
        name="norm_proj",
    )(x2, g, w)


KW_AK, KW_AV, KW_BK, KW_BV = 0, D_A, 2 * D_A, 2 * D_A + D_BKV
KW_COLS = 2 * D_A + 2 * D_BKV
KV_CU = KW_COLS


def _mixer_kernel(x_ref, qp_ref, kvp_ref, kvc_ref, kvn_ref, nab_ref, swb_ref, sink_ref,
                  poolw_ref, pools_ref, wout_ref, g2_ref, rw_ref, rb_ref, tri_ref,
                  xmid_ref, xg_ref, rinfo_ref, cnt_ref,
                  kwin, uwin, mix, carry, *, seq_len):
    b = pl.program_id(0)
    i = pl.program_id(1)
    nblk = pl.num_programs(1)
    rows_per_tile = TOK_TILE // GRID_W
    grid_rows = seq_len // GRID_W

    kwin[0:TOK_TILE, :] = kvp_ref[:, 0:KW_COLS]
    kwin[TOK_TILE:2 * TOK_TILE, :] = kvc_ref[:, 0:KW_COLS]
    kwin[2 * TOK_TILE:3 * TOK_TILE, :] = kvn_ref[:, 0:KW_COLS]

    lane_a = lax.broadcasted_iota(jnp.int32, (GRID_W, D_A), 1) // HEAD_DIM

    def na_step(rr, c):
        r = i * rows_per_tile + rr
        rs = jnp.clip(r - NA_ROWS // 2, 0, grid_rows - NA_ROWS)
        variant = r - rs
        start = pl.multiple_of((rs - i * rows_per_tile + rows_per_tile) * GRID_W, GRID_W)
        q0 = pl.multiple_of(rr * GRID_W, GRID_W)
        q = qp_ref[pl.ds(q0, GRID_W), 0:D_A] * jnp.bfloat16(HEAD_DIM ** -0.5)
        zero = jnp.zeros_like(q)
        qs = jnp.concatenate([jnp.where(lane_a == h, q, zero) for h in range(NA_HEADS)], axis=0)
        kw = kwin[pl.ds(start, NA_ROWS * GRID_W), KW_AK:KW_AK + D_A]
        vw = kwin[pl.ds(start, NA_ROWS * GRID_W), KW_AV:KW_AV + D_A]
        s = lax.dot_general(qs, kw, (((1,), (1,)), ((), ())), preferred_element_type=jnp.float32)
        s = s + nab_ref[variant]
        m = jnp.max(s, axis=-1, keepdims=True)
        p = jnp.exp(s - m)
        l = jnp.sum(p, axis=-1, keepdims=True)
        pv = jnp.dot(p.astype(jnp.bfloat16), vw, preferred_element_type=jnp.float32)
        pv = pv * (1.0 / l)
        o = jnp.zeros((GRID_W, D_A), jnp.float32)
        for h in range(NA_HEADS):
            o = o + jnp.where(lane_a == h, pv[h * GRID_W:(h + 1) * GRID_W, :], 0.0)
        mix[pl.ds(q0, GRID_W), 0:D_A] = o.astype(jnp.bfloat16)
        return c

    lax.fori_loop(0, rows_per_tile, na_step, 0)

    lane_b = lax.broadcasted_iota(jnp.int32, (SWA_BLOCK, LANES), 1) // HEAD_DIM
    kcol = lax.broadcasted_iota(jnp.int32, (1, 3 * SWA_BLOCK), 1)
    blocks_per_tile = TOK_TILE // SWA_BLOCK
    nblocks = seq_len // SWA_BLOCK

    def swa_step(sb, c):
        n = i * blocks_per_tile + sb
        q0 = pl.multiple_of(sb * SWA_BLOCK, SWA_BLOCK)
        k0 = pl.multiple_of(TOK_TILE - SWA_BLOCK + sb * SWA_BLOCK, SWA_BLOCK)
        pieces = []
        for g in range(SWA_KV_HEADS):
            for t in range(SWA_REP):
                qt = qp_ref[pl.ds(q0, SWA_BLOCK), D_A + t * LANES:D_A + (t + 1) * LANES]
                qt = qt * jnp.bfloat16(HEAD_DIM ** -0.5)
                pieces.append(jnp.where(lane_b == g, qt, jnp.zeros_like(qt)))
        qs = jnp.concatenate(pieces, axis=0)
        kw = kwin[pl.ds(k0, 3 * SWA_BLOCK), KW_BK:KW_BK + D_BKV]
        vw = kwin[pl.ds(k0, 3 * SWA_BLOCK), KW_BV:KW_BV + D_BKV]
        s = lax.dot_general(qs, kw, (((1,), (1,)), ((), ())), preferred_element_type=jnp.float32)
        s = s + swb_ref[...]
        lo = jnp.where(n == 0, SWA_BLOCK, 0)
        hi = jnp.where(n == nblocks - 1, 2 * SWA_BLOCK, 3 * SWA_BLOCK)
        s = jnp.where((kcol >= lo) & (kcol < hi), s, NEG)
        sink = sink_ref[:, 0:1]
        m = jnp.maximum(jnp.max(s, axis=-1, keepdims=True), sink)
        p = jnp.exp(s - m)
        l = jnp.sum(p, axis=-1, keepdims=True) + jnp.exp(sink - m)
        pv = jnp.dot(p.astype(jnp.bfloat16), vw, preferred_element_type=jnp.float32)
        pv = pv * (1.0 / l)
        for t in range(SWA_REP):
            o0 = pv[t * SWA_BLOCK:(t + 1) * SWA_BLOCK, :]
            o1 = pv[(SWA_REP + t) * SWA_BLOCK:(SWA_REP + t + 1) * SWA_BLOCK, :]
            ot = jnp.where(lane_b == 0, o0, o1)
            mix[pl.ds(q0, SWA_BLOCK), D_A + t * LANES:D_A + (t + 1) * LANES] = ot.astype(jnp.bfloat16)
        return c

    lax.fori_loop(0, blocks_per_tile, swa_step, 0)

    u = kvc_ref[:, KV_CU:KV_CU + D_C].astype(jnp.float32)
    prev_ok = (i > 0).astype(jnp.float32)
    next_ok = (i < nblk - 1).astype(jnp.float32)
    uwin[0:HALO, :] = kvp_ref[TOK_TILE - HALO:TOK_TILE, KV_CU:KV_CU + D_C].astype(jnp.float32) * prev_ok
    uwin[HALO:HALO + TOK_TILE, :] = u
    uwin[HALO + TOK_TILE:2 * HALO + TOK_TILE, :] = kvn_ref[0:HALO, KV_CU:KV_CU + D_C].astype(jnp.float32) * next_ok
    n_ext = TOK_TILE + 2 * HALO
    a2 = uwin[0:n_ext - 1, :] + uwin[1:n_ext, :]
    a4 = a2[0:n_ext - 3, :] + a2[2:n_ext - 1, :]
    a8 = a4[0:n_ext - 7, :] + a4[4:n_ext - 3, :]
    a16 = a8[0:n_ext - 15, :] + a8[8:n_ext - 7, :]
    w2 = a2[7:7 + TOK_TILE, :]
    w4 = a4[6:6 + TOK_TILE, :]
    w8 = a8[4:4 + TOK_TILE, :]
    w16 = a16[0:TOK_TILE, :]
    lane_c = lax.broadcasted_iota(jnp.int32, (TOK_TILE, D_C), 1) // POOL_GROUP_DIM
    pooled = jnp.where(lane_c == 0, w2, jnp.where(lane_c == 1, w4, jnp.where(lane_c == 2, w8, w16)))
    half = jnp.where(lane_c == 0, 1, jnp.where(lane_c == 1, 2, jnp.where(lane_c == 2, 4, 8)))
    pos = i * TOK_TILE + lax.broadcasted_iota(jnp.int32, (TOK_TILE, D_C), 0)
    cnt = (jnp.minimum(pos + half, seq_len) - jnp.maximum(pos - half, 0)).astype(jnp.float32)
    d = (pooled / cnt - u).astype(jnp.bfloat16)
    oc = jnp.dot(d, poolw_ref[...], preferred_element_type=jnp.float32) * pools_ref[...]
    mix[:, D_A + D_B:D_MIX] = oc.astype(jnp.bfloat16)

    xm = x_ref[...] + jnp.dot(mix[...], wout_ref[...], preferred_element_type=jnp.float32)
    xmid_ref[...] = xm

    xn = _rmsnorm_f32(xm, g2_ref[...])
    for j in range(ROW_CHUNKS):
        xg_ref[pl.ds(j, TOK_TILE, stride=SUBLANES), :] = xn[:, j * LANES:(j + 1) * LANES]
    logits = jnp.dot(xn.astype(jnp.bfloat16), rw_ref[...], preferred_element_type=jnp.float32) + rb_ref[...]
    lane = lax.broadcasted_iota(jnp.int32, (TOK_TILE, LANES), 1)
    is_g = lane < N_GROUPS
    gl = jnp.where(is_g, logits, NEG)
    gmax = jnp.max(gl, axis=-1, keepdims=True)
    gtop = jnp.min(jnp.where(is_g & (gl == gmax), lane, LANES), axis=-1, keepdims=True)
    gprob = 1.0 / jnp.sum(jnp.exp(gl - gmax), axis=-1, keepdims=True)
    e_lo = ROUTE_LANE0 + gtop * EXPERTS_PER_GROUP
    in_grp = (lane >= e_lo) & (lane < e_lo + EXPERTS_PER_GROUP)
    el = jnp.where(in_grp, logits, NEG)
    m1 = jnp.max(el, axis=-1, keepdims=True)
    i1 = jnp.min(jnp.where(in_grp & (el == m1), lane, LANES), axis=-1, keepdims=True)
    el2 = jnp.where(lane == i1, NEG, el)
    m2 = jnp.max(el2, axis=-1, keepdims=True)
    i2 = jnp.min(jnp.where(in_grp & (lane != i1) & (el2 == m2), lane, LANES), axis=-1, keepdims=True)
    r21 = jnp.exp(m2 - m1)
    gate1 = gprob / (1.0 + r21)
    gate2 = gprob * r21 / (1.0 + r21)

    @pl.when((b == 0) & (i == 0))
    def _():
        carry[...] = jnp.zeros_like(carry)

    oh1 = lane == i1
    oh2 = lane == i2
    oh = (oh1 | oh2).astype(jnp.bfloat16)
    before = jnp.dot(tri_ref[...], oh, preferred_element_type=jnp.float32) + carry[0:1, :]
    rank1 = jnp.sum(jnp.where(oh1, before, 0.0), axis=-1, keepdims=True)
    rank2 = jnp.sum(jnp.where(oh2, before, 0.0), axis=-1, keepdims=True)
    new_carry = carry[0:1, :] + jnp.sum(oh.astype(jnp.float32), axis=0, keepdims=True)
    carry[...] = jnp.broadcast_to(new_carry, carry.shape)
    cnt_ref[...] = jnp.broadcast_to(new_carry, cnt_ref.shape)

    e1 = (i1 - ROUTE_LANE0).astype(jnp.float32)
    e2 = (i2 - ROUTE_LANE0).astype(jnp.float32)
    info = jnp.where(lane == 0, e1, jnp.where(lane == 1, e2, jnp.where(lane == 2, gate1, jnp.where(
        lane == 3, gate2, jnp.where(lane == 4, rank1, jnp.where(lane == 5, rank2, 0.0))))))
    rinfo_ref[...] = info


def _mixer(x2, qp, kv, nab, swb, sinkcol, poolw, pools, wout, g2, rw, rb, tri, *, batch, seq_len):
    T = x2.shape[0]
    nblk = seq_len // TOK_TILE

    def cur(b, i):
        return (b * nblk + i, 0)

    def prev(b, i):
        return (b * nblk + jnp.maximum(i - 1, 0), 0)

    def nxt(b, i):
        return (b * nblk + jnp.minimum(i + 1, nblk - 1), 0)

    def const2(b, i):
        return (0, 0)

    def const3(b, i):
        return (0, 0, 0)

    return pl.pallas_call(
        functools.partial(_mixer_kernel, seq_len=seq_len),
        grid=(batch, nblk),
        in_specs=[
            pl.BlockSpec((TOK_TILE, D_MODEL), cur),
            pl.BlockSpec((TOK_TILE, D_QP), cur),
            pl.BlockSpec((TOK_TILE, D_KV), prev),
            pl.BlockSpec((TOK_TILE, D_KV), cur),
            pl.BlockSpec((TOK_TILE, D_KV), nxt),
            pl.BlockSpec(nab.shape, const3),
            pl.BlockSpec(swb.shape, const2),
            pl.BlockSpec(sinkcol.shape, const2),
            pl.BlockSpec(poolw.shape, const2),
            pl.BlockSpec(pools.shape, const2),
            pl.BlockSpec(wout.shape, const2),
            pl.BlockSpec(g2.shape, const2),
            pl.BlockSpec(rw.shape, const2),
            pl.BlockSpec(rb.shape, const2),
            pl.BlockSpec(tri.shape, const2),
        ],
        out_specs=[
            pl.BlockSpec((TOK_TILE, D_MODEL), cur),
            pl.BlockSpec((TOK_TILE * ROW_CHUNKS, LANES), cur),
            pl.BlockSpec((TOK_TILE, LANES), cur),
            pl.BlockSpec((SUBLANES, LANES), const2),
        ],
        out_shape=[
            jax.ShapeDtypeStruct((T, D_MODEL), jnp.float32),
            jax.ShapeDtypeStruct((T * ROW_CHUNKS, LANES), jnp.float32),
            jax.ShapeDtypeStruct((T, LANES), jnp.float32),
            jax.ShapeDtypeStruct((SUBLANES, LANES), jnp.float32),
        ],
        scratch_shapes=[
            pltpu.VMEM((3 * TOK_TILE, KW_COLS), jnp.bfloat16),
            pltpu.VMEM((TOK_TILE + 2 * HALO, D_C), jnp.float32),
            pltpu.VMEM((TOK_TILE, D_MIX), jnp.bfloat16),
            pltpu.VMEM((SUBLANES, LANES), jnp.float32),
        ],
        compiler_params=pltpu.CompilerParams(
            dimension_semantics=("arbitrary", "arbitrary"), vmem_limit_bytes=VMEM_LIMIT),
        name="mixer",
    )(x2, qp, kv, kv, kv, nab, swb, sinkcol, poolw, pools, wout, g2, rw, rb, tri)


def _row_copy(src_hbm, src_row, dst, dst_row, sem):
    return pltpu.make_async_copy(
        src_hbm.at[pl.ds(pl.multiple_of(src_row * ROW_CHUNKS, ROW_CHUNKS), ROW_CHUNKS)],
        dst.at[pl.ds(pl.multiple_of(dst_row * ROW_CHUNKS, ROW_CHUNKS), ROW_CHUNKS)],
        sem)


def _rows_to_matrix(buf, n_rows):
    return jnp.concatenate(
        [buf[pl.ds(j, n_rows, stride=ROW_CHUNKS), :] for j in range(ROW_CHUNKS)], axis=-1)


def _expert_kernel(te_ref, tok_ref, xg_hbm, wgu_ref, wd_ref, y_ref, buf, sem):
    j = pl.program_id(0)
    nt = pl.num_programs(0)
    slot = j % 2

    def start_gather(tile, s):
        def body(r, c):
            _row_copy(xg_hbm, tok_ref[tile * EXP_TILE + r], buf.at[s], r, sem.at[s]).start()
            return c
        lax.fori_loop(0, EXP_TILE, body, 0)

    @pl.when(j == 0)
    def _():
        start_gather(0, 0)

    @pl.when(j + 1 < nt)
    def _():
        start_gather(j + 1, 1 - slot)

    pltpu.make_async_copy(xg_hbm.at[pl.ds(0, EXP_TILE * ROW_CHUNKS)], buf.at[slot], sem.at[slot]).wait()

    xs = _rows_to_matrix(buf.at[slot], EXP_TILE).astype(jnp.bfloat16)
    h = jnp.dot(xs, wgu_ref[0], preferred_element_type=jnp.float32)
    gate = h[:, :D_EXPERT]
    act = (gate * (1.0 / (1.0 + jnp.exp(-gate))) * h[:, D_EXPERT:]).astype(jnp.bfloat16)
    y = jnp.dot(act, wd_ref[0], preferred_element_type=jnp.float32)
    for c in range(ROW_CHUNKS):
        y_ref[pl.ds(c, EXP_TILE, stride=ROW_CHUNKS), :] = y[:, c * LANES:(c + 1) * LANES]


def _experts(tile_expert, tok_of_row, xg, wgu, wd):
    n_tiles = tile_expert.shape[0]
    return pl.pallas_call(
        _expert_kernel,
        grid_spec=pltpu.PrefetchScalarGridSpec(
            num_scalar_prefetch=2,
            grid=(n_tiles,),
            in_specs=[
                pl.BlockSpec(memory_space=pl.ANY),
                pl.BlockSpec((1, D_MODEL, 2 * D_EXPERT), lambda j, te, tok: (te[j], 0, 0)),
                pl.BlockSpec((1, D_EXPERT, D_MODEL), lambda j, te, tok: (te[j], 0, 0)),
            ],
            out_specs=pl.BlockSpec((EXP_TILE * ROW_CHUNKS, LANES), lambda j, te, tok: (j, 0)),
            scratch_shapes=[
                pltpu.VMEM((2, EXP_TILE * ROW_CHUNKS, LANES), jnp.float32),
                pltpu.SemaphoreType.DMA((2,)),
            ],
        ),
        out_shape=jax.ShapeDtypeStruct((n_tiles * EXP_TILE * ROW_CHUNKS, LANES), jnp.float32),
        compiler_params=pltpu.CompilerParams(
            dimension_semantics=("arbitrary",), vmem_limit_bytes=VMEM_LIMIT),
        name="experts",
    )(tile_expert, tok_of_row, xg, wgu, wd)


def _combine_kernel(pos_ref, xmid_ref, rinfo_ref, g_ref, y_hbm, out_ref, buf, sem, *, final_norm):
    i = pl.program_id(0)
    nt = pl.num_programs(0)
    slot = i % 2
    n_rows = 2 * TOK_TILE

    def start_gather(tile, s):
        def body(r, c):
            _row_copy(y_hbm, pos_ref[tile * n_rows + r], buf.at[s], r, sem.at[s]).start()
            return c
        lax.fori_loop(0, n_rows, body, 0)

    @pl.when(i == 0)
    def _():
        start_gather(0, 0)

    @pl.when(i + 1 < nt)
    def _():
        start_gather(i + 1, 1 - slot)

    pltpu.make_async_copy(y_hbm.at[pl.ds(0, n_rows * ROW_CHUNKS)], buf.at[slot], sem.at[slot]).wait()

    half = TOK_TILE * ROW_CHUNKS
    y1 = _rows_to_matrix(buf.at[slot, pl.ds(0, half)], TOK_TILE)
    y2 = _rows_to_matrix(buf.at[slot, pl.ds(half, half)], TOK_TILE)
    info = rinfo_ref[...]
    out = xmid_ref[...] + info[:, 2:3] * y1 + info[:, 3:4] * y2
    if final_norm:
        out = _rmsnorm_f32(out, g_ref[...])
    out_ref[...] = out


def _combine(pos_rows, xmid, rinfo, g, y, *, final_norm):
    T = xmid.shape[0]
    return pl.pallas_call(
        functools.partial(_combine_kernel, final_norm=final_norm),
        grid_spec=pltpu.PrefetchScalarGridSpec(
            num_scalar_prefetch=1,
            grid=(T // TOK_TILE,),
            in_specs=[
                pl.BlockSpec((TOK_TILE, D_MODEL), lambda i, pos: (i, 0)),
                pl.BlockSpec((TOK_TILE, LANES), lambda i, pos: (i, 0)),
                pl.BlockSpec((1, D_MODEL), lambda i, pos: (0, 0)),
                pl.BlockSpec(memory_space=pl.ANY),
            ],
            out_specs=pl.BlockSpec((TOK_TILE, D_MODEL), lambda i, pos: (i, 0)),
            scratch_shapes=[
                pltpu.VMEM((2, 2 * TOK_TILE * ROW_CHUNKS, LANES), jnp.float32),
                pltpu.SemaphoreType.DMA((2,)),
            ],
        ),
        out_shape=jax.ShapeDtypeStruct((T, D_MODEL), jnp.float32),
        compiler_params=pltpu.CompilerParams(
            dimension_semantics=("arbitrary",), vmem_limit_bytes=VMEM_LIMIT),
        name="combine",
    )(pos_rows, xmid, rinfo, g, y)


def _in_proj_columns():
    off_bq = 3 * D_A
    cols = list(range(0, D_A))
    for t in range(SWA_REP):
        for g in range(SWA_KV_HEADS):
            h = g * SWA_REP + t
            cols += list(range(off_bq + h * HEAD_DIM, off_bq + (h + 1) * HEAD_DIM))
    cols += list(range(D_A, 3 * D_A))
    cols += list(range(off_bq + D_B, off_bq + D_B + 2 * D_BKV + D_C))
    return np.asarray(cols, np.int32)


def _out_proj_rows():
    rows = list(range(0, D_A))
    for t in range(SWA_REP):
        for g in range(SWA_KV_HEADS):
            h = g * SWA_REP + t
            rows += list(range(D_A + h * HEAD_DIM, D_A + (h + 1) * HEAD_DIM))
    rows += list(range(D_A + D_B, D_MIX))
    return np.asarray(rows, np.int32)


def _na_bias_table(rel_bias):
    k = np.arange(NA_ROWS)[:, None]
    j = np.arange(NA_ROWS)[None, :]
    dr = j - k + (NA_ROWS - 1)
    c = np.arange(GRID_W)[:, None]
    cp = np.arange(GRID_W)[None, :]
    cs = np.clip(c - NA_COLS // 2, 0, GRID_W - NA_COLS)
    valid = (cp >= cs) & (cp < cs + NA_COLS)
    dc = np.clip(cp - c + (NA_COLS - 1), 0, 2 * NA_COLS - 2)
    tab = rel_bias.astype(jnp.float32)[:, dr[:, None, :, None], dc[None, :, None, :]]
    tab = jnp.where(jnp.asarray(valid)[None, None, :, None, :], tab, NEG)
    tab = tab.transpose(1, 0, 2, 3, 4)
    return tab.reshape(NA_ROWS, NA_HEADS * GRID_W, NA_ROWS * GRID_W)


def _swa_bias_table():
    slopes = (2.0 ** (-8.0 * np.arange(1, SWA_Q_HEADS + 1) / SWA_Q_HEADS)).astype(np.float32)
    qi = np.arange(SWA_BLOCK)[:, None]
    ki = np.arange(3 * SWA_BLOCK)[None, :]
    dist = np.abs(ki - qi - SWA_BLOCK).astype(np.float32)
    tab = np.where(dist <= SWA_WINDOW, -slopes[:, None, None] * dist[None], np.float32(NEG))
    return jnp.asarray(tab.reshape(SWA_Q_HEADS * SWA_BLOCK, 3 * SWA_BLOCK).astype(np.float32))


def _block_diag(pool_w):
    n = pool_w.shape[0]
    out = jnp.zeros((n * POOL_GROUP_DIM, n * POOL_GROUP_DIM), pool_w.dtype)
    for gi in range(n):
        sl = slice(gi * POOL_GROUP_DIM, (gi + 1) * POOL_GROUP_DIM)
        out = out.at[sl, sl].set(pool_w[gi])
    return out


def _router_weights(rg_w, rg_b, re_w, re_b):
    w = jnp.zeros((D_MODEL, LANES), jnp.float32)
    w = w.at[:, 0:N_GROUPS].set(rg_w).at[:, ROUTE_LANE0:ROUTE_LANE0 + N_EXPERTS].set(re_w)
    bias = jnp.zeros((1, LANES), jnp.float32)
    bias = bias.at[0, 0:N_GROUPS].set(rg_b).at[0, ROUTE_LANE0:ROUTE_LANE0 + N_EXPERTS].set(re_b)
    return w.astype(jnp.bfloat16), bias


def _dispatch_tables(rinfo, cnt, n_tiles):
    T = rinfo.shape[0]
    counts = cnt[0, ROUTE_LANE0:ROUTE_LANE0 + N_EXPERTS].astype(jnp.int32)
    tiles = (counts + EXP_TILE - 1) // EXP_TILE
    tile_end = jnp.cumsum(tiles)
    row_off = (tile_end - tiles) * EXP_TILE
    e1 = rinfo[:, 0].astype(jnp.int32)
    e2 = rinfo[:, 1].astype(jnp.int32)
    pos1 = row_off[e1] + rinfo[:, 4].astype(jnp.int32)
    pos2 = row_off[e2] + rinfo[:, 5].astype(jnp.int32)
    tile_expert = jnp.minimum(
        jnp.searchsorted(tile_end, jnp.arange(n_tiles, dtype=jnp.int32), side="right"),
        N_EXPERTS - 1).astype(jnp.int32)
    tok = jnp.arange(T, dtype=jnp.int32)
    tok_of_row = jnp.zeros((n_tiles * EXP_TILE,), jnp.int32).at[
        jnp.concatenate([pos1, pos2])].set(jnp.concatenate([tok, tok]), unique_indices=True)
    nt = T // TOK_TILE
    pos_rows = jnp.concatenate(
        [pos1.reshape(nt, TOK_TILE), pos2.reshape(nt, TOK_TILE)], axis=1).reshape(-1)
    return tile_expert, tok_of_row, pos_rows


def kernel(x, norm1_g, w_in, nat_bias, swa_sink, pool_w, pool_scale, w_out, norm2_g, router_g_w,
           router_g_b, router_e_w, router_e_b, expert_w_gate, expert_w_up, expert_w_down, final_g):
    batch, seq_len, _ = x.shape
    depth = w_in.shape[0]
    T = batch * seq_len
    assert seq_len % TOK_TILE == 0 and TOK_TILE % SWA_BLOCK == 0 and TOK_TILE % GRID_W == 0
    n_tiles = (2 * T) // EXP_TILE + N_EXPERTS

    in_cols = _in_proj_columns()
    out_rows = _out_proj_rows()
    swb = _swa_bias_table()
    tri = jnp.asarray(np.tril(np.ones((TOK_TILE, TOK_TILE), np.float32), -1)).astype(jnp.bfloat16)

    x2 = x.reshape(T, D_MODEL)
    for l in range(depth):
        w_in_l = w_in[l][:, in_cols].astype(jnp.bfloat16)
        w_out_l = w_out[l][out_rows, :].astype(jnp.bfloat16)
        nab = _na_bias_table(nat_bias[l])
        sinkcol = jnp.broadcast_to(
            jnp.repeat(swa_sink[l].astype(jnp.float32), SWA_BLOCK)[:, None], (SWA_Q_HEADS * SWA_BLOCK, LANES))
        poolw = _block_diag(pool_w[l]).astype(jnp.bfloat16)
        pools = pool_scale[l].reshape(1, D_C).astype(jnp.float32)
        rw, rb = _router_weights(router_g_w[l], router_g_b[l], router_e_w[l], router_e_b[l])
        wgu = jnp.concatenate([expert_w_gate[l], expert_w_up[l]], axis=-1).astype(jnp.bfloat16)
        wd = expert_w_down[l].astype(jnp.bfloat16)

        qp, kv = _norm_proj(x2, norm1_g[l].reshape(1, D_MODEL), w_in_l)
        xmid, xg, rinfo, cnt = _mixer(
            x2, qp, kv, nab, swb, sinkcol, poolw, pools, w_out_l, norm2_g[l].reshape(1, D_MODEL),
            rw, rb, tri, batch=batch, seq_len=seq_len)
        tile_expert, tok_of_row, pos_rows = _dispatch_tables(rinfo, cnt, n_tiles)
        y = _experts(tile_expert, tok_of_row, xg, wgu, wd)
        x2 = _combine(pos_rows, xmid, rinfo, final_g.reshape(1, D_MODEL), y, final_norm=(l == depth - 1))
    return x2.reshape(batch, seq_len, D_MODEL)
```

```python
import functools

import jax
import jax.numpy as jnp
import numpy as np
from jax import lax
from jax.experimental import pallas as pl
from jax.experimental.pallas import tpu as pltpu

D_MODEL = 1024
GRID_W = 64
HEAD_DIM = 64
NA_HEADS = 4
NA_ROWS = 8
NA_COLS = 16
SWA_Q_HEADS = 8
SWA_KV_HEADS = 2
SWA_REP = SWA_Q_HEADS // SWA_KV_HEADS
SWA_WINDOW = 128
SWA_BLOCK = 128
POOL_WINDOWS = (2, 4, 8, 16)
POOL_GROUP_DIM = 64
D_A = NA_HEADS * HEAD_DIM
D_B = SWA_Q_HEADS * HEAD_DIM
D_BKV = SWA_KV_HEADS * HEAD_DIM
D_C = len(POOL_WINDOWS) * POOL_GROUP_DIM
D_MIX = D_A + D_B + D_C
D_QP = D_A + D_B
D_KV = 2 * D_A + 2 * D_BKV + D_C
D_IN = D_QP + D_KV
N_GROUPS = 4
EXPERTS_PER_GROUP = 8
N_EXPERTS = N_GROUPS * EXPERTS_PER_GROUP
D_EXPERT = 256
RMS_EPS = 1e-6
NEG = -1e30

LANES = 128
SUBLANES = 8
ROW_CHUNKS = D_MODEL // LANES

TOK_TILE = 512
EXP_TILE = 256
ROUTE_LANE0 = 8
HALO = 8
VMEM_LIMIT = 56 * 1024 * 1024


def _rmsnorm_f32(x, g):
    return x * lax.rsqrt(jnp.mean(x * x, axis=-1, keepdims=True) + RMS_EPS) * g


def _norm_proj_kernel(x_ref, g_ref, w_ref, qp_ref, kv_ref):
    xn = _rmsnorm_f32(x_ref[...], g_ref[...]).astype(jnp.bfloat16)
    proj = jnp.dot(xn, w_ref[...], preferred_element_type=jnp.float32)
    qp_ref[...] = proj[:, :D_QP].astype(jnp.bfloat16)
    kv_ref[...] = proj[:, D_QP:].astype(jnp.bfloat16)


def _norm_proj(x2, g, w):
    T = x2.shape[0]
    return pl.pallas_call(
        _norm_proj_kernel,
        grid=(T // TOK_TILE,),
        in_specs=[
            pl.BlockSpec((TOK_TILE, D_MODEL), lambda i: (i, 0)),
            pl.BlockSpec((1, D_MODEL), lambda i: (0, 0)),
            pl.BlockSpec((D_MODEL, D_IN), lambda i: (0, 0)),
        ],
        out_specs=[
            pl.BlockSpec((TOK_TILE, D_QP), lambda i: (i, 0)),
            pl.BlockSpec((TOK_TILE, D_KV), lambda i: (i, 0)),
        ],
        out_shape=[
            jax.ShapeDtypeStruct((T, D_QP), jnp.bfloat16),
            jax.ShapeDtypeStruct((T, D_KV), jnp.bfloat16),
        ],
        compiler_params=pltpu.CompilerParams(
            dimension_semantics=("arbitrary",), vmem_limit_bytes=VMEM_LIMIT),
        name="norm_proj",
    )(x2, g, w)


KW_AK, KW_AV, KW_BK, KW_BV = 0, D_A, 2 * D_A, 2 * D_A + D_BKV
KW_COLS = 2 * D_A + 2 * D_BKV
KV_CU = KW_COLS


def _mixer_kernel(x_ref, qp_ref, kvp_ref, kvc_ref, kvn_ref, nab_ref, swb_ref, sink_ref,
                  poolw_ref, pools_ref, wout_ref, g2_ref, rw_ref, rb_ref, tri_ref,
                  xmid_ref, xg_ref, rinfo_ref, cnt_ref,
                  kwin, uwin, mix, carry, *, seq_len):
    b = pl.program_id(0)
    i = pl.program_id(1)
    nblk = pl.num_programs(1)
    rows_per_tile = TOK_TILE // GRID_W
    grid_rows = seq_len // GRID_W

    kwin[0:TOK_TILE, :] = kvp_ref[:, 0:KW_COLS]
    kwin[TOK_TILE:2 * TOK_TILE, :] = kvc_ref[:, 0:KW_COLS]
    kwin[2 * TOK_TILE:3 * TOK_TILE, :] = kvn_ref[:, 0:KW_COLS]

    lane_a = lax.broadcasted_iota(jnp.int32, (GRID_W, D_A), 1) // HEAD_DIM

    def na_step(rr, c):
        r = i * rows_per_tile + rr
        rs = jnp.clip(r - NA_ROWS // 2, 0, grid_rows - NA_ROWS)
        variant = r - rs
        start = pl.multiple_of((rs - i * rows_per_tile + rows_per_tile) * GRID_W, GRID_W)
        q0 = pl.multiple_of(rr * GRID_W, GRID_W)
        q = qp_ref[pl.ds(q0, GRID_W), 0:D_A] * jnp.bfloat16(HEAD_DIM ** -0.5)
        zero = jnp.zeros_like(q)
        qs = jnp.concatenate([jnp.where(lane_a == h, q, zero) for h in range(NA_HEADS)], axis=0)
        kw = kwin[pl.ds(start, NA_ROWS * GRID_W), KW_AK:KW_AK + D_A]
        vw = kwin[pl.ds(start, NA_ROWS * GRID_W), KW_AV:KW_AV + D_A]
        s = lax.dot_general(qs, kw, (((1,), (1,)), ((), ())), preferred_element_type=jnp.float32)
        s = s + nab_ref[variant]
        m = jnp.max(s, axis=-1, keepdims=True)
        p = jnp.exp(s - m)
        l = jnp.sum(p, axis=-1, keepdims=True)
        pv = jnp.dot(p.astype(jnp.bfloat16), vw, preferred_element_type=jnp.float32)
        pv = pv * (1.0 / l)
        o = jnp.zeros((GRID_W, D_A), jnp.float32)
        for h in range(NA_HEADS):
            o = o + jnp.where(lane_a == h, pv[h * GRID_W:(h + 1) * GRID_W, :], 0.0)
        mix[pl.ds(q0, GRID_W), 0:D_A] = o.astype(jnp.bfloat16)
        return c

    lax.fori_loop(0, rows_per_tile, na_step, 0)

    lane_b = lax.broadcasted_iota(jnp.int32, (SWA_BLOCK, LANES), 1) // HEAD_DIM
    kcol = lax.broadcasted_iota(jnp.int32, (1, 3 * SWA_BLOCK), 1)
    blocks_per_tile = TOK_TILE // SWA_BLOCK
    nblocks = seq_len // SWA_BLOCK

    def swa_step(sb, c):
        n = i * blocks_per_tile + sb
        q0 = pl.multiple_of(sb * SWA_BLOCK, SWA_BLOCK)
        k0 = pl.multiple_of(TOK_TILE - SWA_BLOCK + sb * SWA_BLOCK, SWA_BLOCK)
        pieces = []
        for g in range(SWA_KV_HEADS):
            for t in range(SWA_REP):
                qt = qp_ref[pl.ds(q0, SWA_BLOCK), D_A + t * LANES:D_A + (t + 1) * LANES]
                qt = qt * jnp.bfloat16(HEAD_DIM ** -0.5)
                pieces.append(jnp.where(lane_b == g, qt, jnp.zeros_like(qt)))
        qs = jnp.concatenate(pieces, axis=0)
        kw = kwin[pl.ds(k0, 3 * SWA_BLOCK), KW_BK:KW_BK + D_BKV]
        vw = kwin[pl.ds(k0, 3 * SWA_BLOCK), KW_BV:KW_BV + D_BKV]
        s = lax.dot_general(qs, kw, (((1,), (1,)), ((), ())), preferred_element_type=jnp.float32)
        s = s + swb_ref[...]
        lo = jnp.where(n == 0, SWA_BLOCK, 0)
        hi = jnp.where(n == nblocks - 1, 2 * SWA_BLOCK, 3 * SWA_BLOCK)
        s = jnp.where((kcol >= lo) & (kcol < hi), s, NEG)
        sink = sink_ref[:, 0:1]
        m = jnp.maximum(jnp.max(s, axis=-1, keepdims=True), sink)
        p = jnp.exp(s - m)
        l = jnp.sum(p, axis=-1, keepdims=True) + jnp.exp(sink - m)
        pv = jnp.dot(p.astype(jnp.bfloat16), vw, preferred_element_type=jnp.float32)
        pv = pv * (1.0 / l)
        for t in range(SWA_REP):
            o0 = pv[t * SWA_BLOCK:(t + 1) * SWA_BLOCK, :]
            o1 = pv[(SWA_REP + t) * SWA_BLOCK:(SWA_REP + t + 1) * SWA_BLOCK, :]
            ot = jnp.where(lane_b == 0, o0, o1)
            mix[pl.ds(q0, SWA_BLOCK), D_A + t * LANES:D_A + (t + 1) * LANES] = ot.astype(jnp.bfloat16)
        return c

    lax.fori_loop(0, blocks_per_tile, swa_step, 0)

    u = kvc_ref[:, KV_CU:KV_CU + D_C].astype(jnp.float32)
    prev_ok = (i > 0).astype(jnp.float32)
    next_ok = (i < nblk - 1).astype(jnp.float32)
    uwin[0:HALO, :] = kvp_ref[TOK_TILE - HALO:TOK_TILE, KV_CU:KV_CU + D_C].astype(jnp.float32) * prev_ok
    uwin[HALO:HALO + TOK_TILE, :] = u
    uwin[HALO + TOK_TILE:2 * HALO + TOK_TILE, :] = kvn_ref[0:HALO, KV_CU:KV_CU + D_C].astype(jnp.float32) * next_ok
    n_ext = TOK_TILE + 2 * HALO
    a2 = uwin[0:n_ext - 1, :] + uwin[1:n_ext, :]
    a4 = a2[0:n_ext - 3, :] + a2[2:n_ext - 1, :]
    a8 = a4[0:n_ext - 7, :] + a4[4:n_ext - 3, :]
    a16 = a8[0:n_ext - 15, :] + a8[8:n_ext - 7, :]
    w2 = a2[7:7 + TOK_TILE, :]
    w4 = a4[6:6 + TOK_TILE, :]
    w8 = a8[4:4 + TOK_TILE, :]
    w16 = a16[0:TOK_TILE, :]
    lane_c = lax.broadcasted_iota(jnp.int32, (TOK_TILE, D_C), 1) // POOL_GROUP_DIM
    pooled = jnp.where(lane_c == 0, w2, jnp.where(lane_c == 1, w4, jnp.where(lane_c == 2, w8, w16)))
    half = jnp.where(lane_c == 0, 1, jnp.where(lane_c == 1, 2, jnp.where(lane_c == 2, 4, 8)))
    pos = i * TOK_TILE + lax.broadcasted_iota(jnp.int32, (TOK_TILE, D_C), 0)
    cnt = (jnp.minimum(pos + half, seq_len) - jnp.maximum(pos - half, 0)).astype(jnp.float32)
    d = (pooled / cnt - u).astype(jnp.bfloat16)
    oc = jnp.dot(d, poolw_ref[...], preferred_element_type=jnp.float32) * pools_ref[...]
    mix[:, D_A + D_B:D_MIX] = oc.astype(jnp.bfloat16)

    xm = x_ref[...] + jnp.dot(mix[...], wout_ref[...], preferred_element_type=jnp.float32)
    xmid_ref[...] = xm

    xn = _rmsnorm_f32(xm, g2_ref[...])
    for j in range(ROW_CHUNKS):
        xg_ref[pl.ds(j, TOK_TILE, stride=SUBLANES), :] = xn[:, j * LANES:(j + 1) * LANES]
    logits = jnp.dot(xn.astype(jnp.bfloat16), rw_ref[...], preferred_element_type=jnp.float32) + rb_ref[...]
    lane = lax.broadcasted_iota(jnp.int32, (TOK_TILE, LANES), 1)
    is_g = lane < N_GROUPS
    gl = jnp.where(is_g, logits, NEG)
    gmax = jnp.max(gl, axis=-1, keepdims=True)
    gtop = jnp.min(jnp.where(is_g & (gl == gmax), lane, LANES), axis=-1, keepdims=True)
    gprob = 1.0 / jnp.sum(jnp.exp(gl - gmax), axis=-1, keepdims=True)
    e_lo = ROUTE_LANE0 + gtop * EXPERTS_PER_GROUP
    in_grp = (lane >= e_lo) & (lane < e_lo + EXPERTS_PER_GROUP)
    el = jnp.where(in_grp, logits, NEG)
    m1 = jnp.max(el, axis=-1, keepdims=True)
    i1 = jnp.min(jnp.where(in_grp & (el == m1), lane, LANES), axis=-1, keepdims=True)
    el2 = jnp.where(lane == i1, NEG, el)
    m2 = jnp.max(el2, axis=-1, keepdims=True)
    i2 = jnp.min(jnp.where(in_grp & (lane != i1) & (el2 == m2), lane, LANES), axis=-1, keepdims=True)
    r21 = jnp.exp(m2 - m1)
    gate1 = gprob / (1.0 + r21)
    gate2 = gprob * r21 / (1.0 + r21)

    @pl.when((b == 0) & (i == 0))
    def _():
        carry[...] = jnp.zeros_like(carry)

    oh1 = lane == i1
    oh2 = lane == i2
    oh = (oh1 | oh2).astype(jnp.bfloat16)
    before = jnp.dot(tri_ref[...], oh, preferred_element_type=jnp.float32) + carry[0:1, :]
    rank1 = jnp.sum(jnp.where(oh1, before, 0.0), axis=-1, keepdims=True)
    rank2 = jnp.sum(jnp.where(oh2, before, 0.0), axis=-1, keepdims=True)
    new_carry = carry[0:1, :] + jnp.sum(oh.astype(jnp.float32), axis=0, keepdims=True)
    carry[...] = jnp.broadcast_to(new_carry, carry.shape)
    cnt_ref[...] = jnp.broadcast_to(new_carry, cnt_ref.shape)

    e1 = (i1 - ROUTE_LANE0).astype(jnp.float32)
    e2 = (i2 - ROUTE_LANE0).astype(jnp.float32)
    info = jnp.where(lane == 0, e1, jnp.where(lane == 1, e2, jnp.where(lane == 2, gate1, jnp.where(
        lane == 3, gate2, jnp.where(lane == 4, rank1, jnp.where(lane == 5, rank2, 0.0))))))
    rinfo_ref[...] = info


def _mixer(x2, qp, kv, nab, swb, sinkcol, poolw, pools, wout, g2, rw, rb, tri, *, batch, seq_len):
    T = x2.shape[0]
    nblk = seq_len // TOK_TILE

    def cur(b, i):
        return (b * nblk + i, 0)

    def prev(b, i):
        return (b * nblk + jnp.maximum(i - 1, 0), 0)

    def nxt(b, i):
        return (b * nblk + jnp.minimum(i + 1, nblk - 1), 0)

    def const2(b, i):
        return (0, 0)

    def const3(b, i):
        return (0, 0, 0)

    return pl.pallas_call(
        functools.partial(_mixer_kernel, seq_len=seq_len),
        grid=(batch, nblk),
        in_specs=[
            pl.BlockSpec((TOK_TILE, D_MODEL), cur),
            pl.BlockSpec((TOK_TILE, D_QP), cur),
            pl.BlockSpec((TOK_TILE, D_KV), prev),
            pl.BlockSpec((TOK_TILE, D_KV), cur),
            pl.BlockSpec((TOK_TILE, D_KV), nxt),
            pl.BlockSpec(nab.shape, const3),
            pl.BlockSpec(swb.shape, const2),
            pl.BlockSpec(sinkcol.shape, const2),
            pl.BlockSpec(poolw.shape, const2),
            pl.BlockSpec(pools.shape, const2),
            pl.BlockSpec(wout.shape, const2),
            pl.BlockSpec(g2.shape, const2),
            pl.BlockSpec(rw.shape, const2),
            pl.BlockSpec(rb.shape, const2),
            pl.BlockSpec(tri.shape, const2),
        ],
        out_specs=[
            pl.BlockSpec((TOK_TILE, D_MODEL), cur),
            pl.BlockSpec((TOK_TILE * ROW_CHUNKS, LANES), cur),
            pl.BlockSpec((TOK_TILE, LANES), cur),
            pl.BlockSpec((SUBLANES, LANES), const2),
        ],
        out_shape=[
            jax.ShapeDtypeStruct((T, D_MODEL), jnp.float32),
            jax.ShapeDtypeStruct((T * ROW_CHUNKS, LANES), jnp.float32),
            jax.ShapeDtypeStruct((T, LANES), jnp.float32),
            jax.ShapeDtypeStruct((SUBLANES, LANES), jnp.float32),
        ],
        scratch_shapes=[
            pltpu.VMEM((3 * TOK_TILE, KW_COLS), jnp.bfloat16),
            pltpu.VMEM((TOK_TILE + 2 * HALO, D_C), jnp.float32),
            pltpu.VMEM((TOK_TILE, D_MIX), jnp.bfloat16),
            pltpu.VMEM((SUBLANES, LANES), jnp.float32),
        ],
        compiler_params=pltpu.CompilerParams(
            dimension_semantics=("arbitrary", "arbitrary"), vmem_limit_bytes=VMEM_LIMIT),
        name="mixer",
    )(x2, qp, kv, kv, kv, nab, swb, sinkcol, poolw, pools, wout, g2, rw, rb, tri)


def _row_copy(src_hbm, src_row, dst, dst_row, sem):
    return pltpu.make_async_copy(
        src_hbm.at[pl.ds(pl.multiple_of(src_row * ROW_CHUNKS, ROW_CHUNKS), ROW_CHUNKS)],
        dst.at[pl.ds(pl.multiple_of(dst_row * ROW_CHUNKS, ROW_CHUNKS), ROW_CHUNKS)],
        sem)


def _rows_to_matrix(buf, n_rows):
    return jnp.concatenate(
        [buf[pl.ds(j, n_rows, stride=ROW_CHUNKS), :] for j in range(ROW_CHUNKS)], axis=-1)


def _expert_kernel(te_ref, tok_ref, xg_hbm, wg_ref, wu_ref, wd_ref, y_ref, buf, sem):
    j = pl.program_id(0)
    nt = pl.num_programs(0)
    slot = j % 2

    def start_gather(tile, s):
        def body(r, c):
            _row_copy(xg_hbm, tok_ref[tile * EXP_TILE + r], buf.at[s], r, sem.at[s]).start()
            return c
        lax.fori_loop(0, EXP_TILE, body, 0)

    @pl.when(j == 0)
    def _():
        start_gather(0, 0)

    @pl.when(j + 1 < nt)
    def _():
        start_gather(j + 1, 1 - slot)

    pltpu.make_async_copy(xg_hbm.at[pl.ds(0, EXP_TILE * ROW_CHUNKS)], buf.at[slot], sem.at[slot]).wait()

    xs = _rows_to_matrix(buf.at[slot], EXP_TILE).astype(jnp.bfloat16)
    gate = jnp.dot(xs, wg_ref[0].astype(jnp.bfloat16), preferred_element_type=jnp.float32)
    up = jnp.dot(xs, wu_ref[0].astype(jnp.bfloat16), preferred_element_type=jnp.float32)
    act = (gate * (1.0 / (1.0 + jnp.exp(-gate))) * up).astype(jnp.bfloat16)
    y = jnp.dot(act, wd_ref[0].astype(jnp.bfloat16), preferred_element_type=jnp.float32)
    for c in range(ROW_CHUNKS):
        y_ref[pl.ds(c, EXP_TILE, stride=ROW_CHUNKS), :] = y[:, c * LANES:(c + 1) * LANES]


def _experts(tile_expert, tok_of_row, xg, wg, wu, wd):
    n_tiles = tile_expert.shape[0]
    return pl.pallas_call(
        _expert_kernel,
        grid_spec=pltpu.PrefetchScalarGridSpec(
            num_scalar_prefetch=2,
            grid=(n_tiles,),
            in_specs=[
                pl.BlockSpec(memory_space=pl.ANY),
                pl.BlockSpec((1, D_MODEL, D_EXPERT), lambda j, te, tok: (te[j], 0, 0)),
                pl.BlockSpec((1, D_MODEL, D_EXPERT), lambda j, te, tok: (te[j], 0, 0)),
                pl.BlockSpec((1, D_EXPERT, D_MODEL), lambda j, te, tok: (te[j], 0, 0)),
            ],
            out_specs=pl.BlockSpec((EXP_TILE * ROW_CHUNKS, LANES), lambda j, te, tok: (j, 0)),
            scratch_shapes=[
                pltpu.VMEM((2, EXP_TILE * ROW_CHUNKS, LANES), jnp.float32),
                pltpu.SemaphoreType.DMA((2,)),
            ],
        ),
        out_shape=jax.ShapeDtypeStruct((n_tiles * EXP_TILE * ROW_CHUNKS, LANES), jnp.float32),
        compiler_params=pltpu.CompilerParams(
            dimension_semantics=("arbitrary",), vmem_limit_bytes=VMEM_LIMIT),
        name="experts",
    )(tile_expert, tok_of_row, xg, wg, wu, wd)


def _combine_kernel(pos_ref, xmid_ref, rinfo_ref, g_ref, y_hbm, out_ref, buf, sem, *, final_norm):
    i = pl.program_id(0)
    nt = pl.num_programs(0)
    slot = i % 2
    n_rows = 2 * TOK_TILE

    def start_gather(tile, s):
        def body(r, c):
            _row_copy(y_hbm, pos_ref[tile * n_rows + r], buf.at[s], r, sem.at[s]).start()
            return c
        lax.fori_loop(0, n_rows, body, 0)

    @pl.when(i == 0)
    def _():
        start_gather(0, 0)

    @pl.when(i + 1 < nt)
    def _():
        start_gather(i + 1, 1 - slot)

    pltpu.make_async_copy(y_hbm.at[pl.ds(0, n_rows * ROW_CHUNKS)], buf.at[slot], sem.at[slot]).wait()

    half = TOK_TILE * ROW_CHUNKS
    y1 = _rows_to_matrix(buf.at[slot, pl.ds(0, half)], TOK_TILE)
    y2 = _rows_to_matrix(buf.at[slot, pl.ds(half, half)], TOK_TILE)
    info = rinfo_ref[...]
    out = xmid_ref[...] + info[:, 2:3] * y1 + info[:, 3:4] * y2
    if final_norm:
        out = _rmsnorm_f32(out, g_ref[...])
    out_ref[...] = out


def _combine(pos_rows, xmid, rinfo, g, y, *, final_norm):
    T = xmid.shape[0]
    return pl.pallas_call(
        functools.partial(_combine_kernel, final_norm=final_norm),
        grid_spec=pltpu.PrefetchScalarGridSpec(
            num_scalar_prefetch=1,
            grid=(T // TOK_TILE,),
            in_specs=[
                pl.BlockSpec((TOK_TILE, D_MODEL), lambda i, pos: (i, 0)),
                pl.BlockSpec((TOK_TILE, LANES), lambda i, pos: (i, 0)),
                pl.BlockSpec((1, D_MODEL), lambda i, pos: (0, 0)),
                pl.BlockSpec(memory_space=pl.ANY),
            ],
            out_specs=pl.BlockSpec((TOK_TILE, D_MODEL), lambda i, pos: (i, 0)),
            scratch_shapes=[
                pltpu.VMEM((2, 2 * TOK_TILE * ROW_CHUNKS, LANES), jnp.float32),
                pltpu.SemaphoreType.DMA((2,)),
            ],
        ),
        out_shape=jax.ShapeDtypeStruct((T, D_MODEL), jnp.float32),
        compiler_params=pltpu.CompilerParams(
            dimension_semantics=("arbitrary",), vmem_limit_bytes=VMEM_LIMIT),
        name="combine",
    )(pos_rows, xmid, rinfo, g, y)


def _pair_heads(a, axis):
    shape = a.shape
    split = shape[:axis] + (SWA_KV_HEADS, SWA_REP, HEAD_DIM) + shape[axis + 1:]
    return jnp.swapaxes(a.reshape(split), axis, axis + 1).reshape(shape)


def _in_proj_weight(w):
    off_bq = 3 * D_A
    return jnp.concatenate(
        [w[:, 0:D_A], _pair_heads(w[:, off_bq:off_bq + D_B], 1), w[:, D_A:off_bq], w[:, off_bq + D_B:]],
        axis=1).astype(jnp.bfloat16)


def _out_proj_weight(w):
    return jnp.concatenate(
        [w[0:D_A], _pair_heads(w[D_A:D_A + D_B], 0), w[D_A + D_B:]], axis=0).astype(jnp.bfloat16)


def _na_bias_table(rel_bias):
    c = np.arange(GRID_W)[:, None]
    cp = np.arange(GRID_W)[None, :]
    cs = np.clip(c - NA_COLS // 2, 0, GRID_W - NA_COLS)
    valid = (cp >= cs) & (cp < cs + NA_COLS)
    d = np.arange(2 * NA_COLS - 1)[:, None, None]
    onehot = ((cp - c + (NA_COLS - 1))[None] == d) & valid[None]
    full = jnp.einsum("hrd,dcm->hrcm", rel_bias.astype(jnp.float32), jnp.asarray(onehot, jnp.float32),
                      precision=lax.Precision.HIGHEST)
    full = jnp.where(jnp.asarray(valid)[None, None], full, NEG)
    variants = []
    for k in range(NA_ROWS):
        win = full[:, NA_ROWS - 1 - k:2 * NA_ROWS - 1 - k]
        variants.append(win.transpose(0, 2, 1, 3).reshape(NA_HEADS * GRID_W, NA_ROWS * GRID_W))
    return jnp.stack(variants)


def _swa_bias_table():
    slopes = (2.0 ** (-8.0 * np.arange(1, SWA_Q_HEADS + 1) / SWA_Q_HEADS)).astype(np.float32)
    qi = np.arange(SWA_BLOCK)[:, None]
    ki = np.arange(3 * SWA_BLOCK)[None, :]
    dist = np.abs(ki - qi - SWA_BLOCK).astype(np.float32)
    tab = np.where(dist <= SWA_WINDOW, -slopes[:, None, None] * dist[None], np.float32(NEG))
    return jnp.asarray(tab.reshape(SWA_Q_HEADS * SWA_BLOCK, 3 * SWA_BLOCK).astype(np.float32))


def _block_diag(pool_w):
    n = pool_w.shape[0]
    out = jnp.zeros((n * POOL_GROUP_DIM, n * POOL_GROUP_DIM), pool_w.dtype)
    for gi in range(n):
        sl = slice(gi * POOL_GROUP_DIM, (gi + 1) * POOL_GROUP_DIM)
        out = out.at[sl, sl].set(pool_w[gi])
    return out


def _router_weights(rg_w, rg_b, re_w, re_b):
    w = jnp.zeros((D_MODEL, LANES), jnp.float32)
    w = w.at[:, 0:N_GROUPS].set(rg_w).at[:, ROUTE_LANE0:ROUTE_LANE0 + N_EXPERTS].set(re_w)
    bias = jnp.zeros((1, LANES), jnp.float32)
    bias = bias.at[0, 0:N_GROUPS].set(rg_b).at[0, ROUTE_LANE0:ROUTE_LANE0 + N_EXPERTS].set(re_b)
    return w.astype(jnp.bfloat16), bias


def _dispatch_tables(rinfo, cnt, n_tiles):
    T = rinfo.shape[0]
    counts = cnt[0, ROUTE_LANE0:ROUTE_LANE0 + N_EXPERTS].astype(jnp.int32)
    tiles = (counts + EXP_TILE - 1) // EXP_TILE
    tile_end = jnp.cumsum(tiles)
    row_off = (tile_end - tiles) * EXP_TILE
    experts = jnp.arange(N_EXPERTS, dtype=jnp.int32)[None, :]
    e1 = rinfo[:, 0:1].astype(jnp.int32)
    e2 = rinfo[:, 1:2].astype(jnp.int32)
    pos1 = jnp.sum(jnp.where(e1 == experts, row_off[None, :], 0), axis=1) + rinfo[:, 4].astype(jnp.int32)
    pos2 = jnp.sum(jnp.where(e2 == experts, row_off[None, :], 0), axis=1) + rinfo[:, 5].astype(jnp.int32)
    tile_ids = jnp.arange(n_tiles, dtype=jnp.int32)[:, None]
    tile_expert = jnp.minimum(
        jnp.sum((tile_end[None, :] <= tile_ids).astype(jnp.int32), axis=1), N_EXPERTS - 1)
    tok = jnp.arange(T, dtype=jnp.int32)
    tok_of_row = jnp.zeros((n_tiles * EXP_TILE,), jnp.int32).at[
        jnp.concatenate([pos1, pos2])].set(jnp.concatenate([tok, tok]), unique_indices=True)
    nt = T // TOK_TILE
    pos_rows = jnp.concatenate(
        [pos1.reshape(nt, TOK_TILE), pos2.reshape(nt, TOK_TILE)], axis=1).reshape(-1)
    return tile_expert, tok_of_row, pos_rows


def kernel(x, norm1_g, w_in, nat_bias, swa_sink, pool_w, pool_scale, w_out, norm2_g, router_g_w,
           router_g_b, router_e_w, router_e_b, expert_w_gate, expert_w_up, expert_w_down, final_g):
    batch, seq_len, _ = x.shape
    depth = w_in.shape[0]
    T = batch * seq_len
    assert seq_len % TOK_TILE == 0 and TOK_TILE % SWA_BLOCK == 0 and TOK_TILE % GRID_W == 0
    n_tiles = (2 * T) // EXP_TILE + N_EXPERTS

    swb = _swa_bias_table()
    tri = jnp.asarray(np.tril(np.ones((TOK_TILE, TOK_TILE), np.float32), -1)).astype(jnp.bfloat16)

    x2 = x.reshape(T, D_MODEL)
    for l in range(depth):
        w_in_l = _in_proj_weight(w_in[l])
        w_out_l = _out_proj_weight(w_out[l])
        nab = _na_bias_table(nat_bias[l])
        sinkcol = jnp.broadcast_to(
            jnp.repeat(swa_sink[l].astype(jnp.float32), SWA_BLOCK)[:, None], (SWA_Q_HEADS * SWA_BLOCK, LANES))
        poolw = _block_diag(pool_w[l]).astype(jnp.bfloat16)
        pools = pool_scale[l].reshape(1, D_C).astype(jnp.float32)
        rw, rb = _router_weights(router_g_w[l], router_g_b[l], router_e_w[l], router_e_b[l])

        qp, kv = _norm_proj(x2, norm1_g[l].reshape(1, D_MODEL), w_in_l)
        xmid, xg, rinfo, cnt = _mixer(
            x2, qp, kv, nab, swb, sinkcol, poolw, pools, w_out_l, norm2_g[l].reshape(1, D_MODEL),
            rw, rb, tri, batch=batch, seq_len=seq_len)
        tile_expert, tok_of_row, pos_rows = _dispatch_tables(rinfo, cnt, n_tiles)
        y = _experts(tile_expert, tok_of_row, xg, expert_w_gate[l], expert_w_up[l], expert_w_down[l])
        x2 = _combine(pos_rows, xmid, rinfo, final_g.reshape(1, D_MODEL), y, final_norm=(l == depth - 1))
    return x2.reshape(batch, seq_len, D_MODEL)
```

```python
import functools

import jax
import jax.numpy as jnp
import numpy as np
from jax import lax
from jax.experimental import pallas as pl
from jax.experimental.pallas import tpu as pltpu

D_MODEL = 1024
GRID_W = 64
HEAD_DIM = 64
NA_HEADS = 4
NA_ROWS = 8
NA_COLS = 16
SWA_Q_HEADS = 8
SWA_KV_HEADS = 2
SWA_REP = SWA_Q_HEADS // SWA_KV_HEADS
SWA_WINDOW = 128
SWA_BLOCK = 128
POOL_WINDOWS = (2, 4, 8, 16)
POOL_GROUP_DIM = 64
D_A = NA_HEADS * HEAD_DIM
D_B = SWA_Q_HEADS * HEAD_DIM
D_BKV = SWA_KV_HEADS * HEAD_DIM
D_C = len(POOL_WINDOWS) * POOL_GROUP_DIM
D_MIX = D_A + D_B + D_C
D_QP = D_A + D_B
D_KV = 2 * D_A + 2 * D_BKV + D_C
D_IN = D_QP + D_KV
N_GROUPS = 4
EXPERTS_PER_GROUP = 8
N_EXPERTS = N_GROUPS * EXPERTS_PER_GROUP
D_EXPERT = 256
RMS_EPS = 1e-6
NEG = -1e30

LANES = 128
SUBLANES = 8
ROW_CHUNKS = D_MODEL // LANES

TOK_TILE = 512
EXP_TILE = 256
CHUNK = 16
LOCAL_CHUNKS = (2 * TOK_TILE + N_EXPERTS * (CHUNK - 1)) // CHUNK + 2
LOCAL_ROWS = LOCAL_CHUNKS * CHUNK
TILE_CHUNKS = EXP_TILE // CHUNK
ROUTE_LANE0 = 8
HALO = 8
VMEM_LIMIT = 56 * 1024 * 1024


def _rmsnorm_f32(x, g):
    return x * lax.rsqrt(jnp.mean(x * x, axis=-1, keepdims=True) + RMS_EPS) * g


def _norm_proj_kernel(x_ref, g_ref, w_ref, qp_ref, kv_ref):
    xn = _rmsnorm_f32(x_ref[...], g_ref[...]).astype(jnp.bfloat16)
    proj = jnp.dot(xn, w_ref[...], preferred_element_type=jnp.float32)
    qp_ref[...] = proj[:, :D_QP].astype(jnp.bfloat16)
    kv_ref[...] = proj[:, D_QP:].astype(jnp.bfloat16)


def _norm_proj(x2, g, w):
    T = x2.shape[0]
    return pl.pallas_call(
        _norm_proj_kernel,
        grid=(T // TOK_TILE,),
        in_specs=[
            pl.BlockSpec((TOK_TILE, D_MODEL), lambda i: (i, 0)),
            pl.BlockSpec((1, D_MODEL), lambda i: (0, 0)),
            pl.BlockSpec((D_MODEL, D_IN), lambda i: (0, 0)),
        ],
        out_specs=[
            pl.BlockSpec((TOK_TILE, D_QP), lambda i: (i, 0)),
            pl.BlockSpec((TOK_TILE, D_KV), lambda i: (i, 0)),
        ],
        out_shape=[
            jax.ShapeDtypeStruct((T, D_QP), jnp.bfloat16),
            jax.ShapeDtypeStruct((T, D_KV), jnp.bfloat16),
        ],
        compiler_params=pltpu.CompilerParams(
            dimension_semantics=("arbitrary",), vmem_limit_bytes=VMEM_LIMIT),
        name="norm_proj",
    )(x2, g, w)


KW_AK, KW_AV, KW_BK, KW_BV = 0, D_A, 2 * D_A, 2 * D_A + D_BKV
KW_COLS = 2 * D_A + 2 * D_BKV
KV_CU = KW_COLS


def _mixer_kernel(x_ref, qp_ref, kvp_ref, kvc_ref, kvn_ref, nab_ref, swb_ref, sink_ref,
                  poolw_ref, pools_ref, wout_ref, g2_ref, rw_ref, rb_ref, tri_ref, utri_ref,
                  xmid_ref, xs_ref, gloc_ref, rinfo_ref, cnt_ref,
                  kwin, uwin, mix, *, seq_len):
    b = pl.program_id(0)
    i = pl.program_id(1)
    nblk = pl.num_programs(1)
    rows_per_tile = TOK_TILE // GRID_W
    grid_rows = seq_len // GRID_W

    kwin[0:TOK_TILE, :] = kvp_ref[:, 0:KW_COLS]
    kwin[TOK_TILE:2 * TOK_TILE, :] = kvc_ref[:, 0:KW_COLS]
    kwin[2 * TOK_TILE:3 * TOK_TILE, :] = kvn_ref[:, 0:KW_COLS]

    lane_a = lax.broadcasted_iota(jnp.int32, (GRID_W, D_A), 1) // HEAD_DIM

    def na_step(rr, c):
        r = i * rows_per_tile + rr
        rs = jnp.clip(r - NA_ROWS // 2, 0, grid_rows - NA_ROWS)
        variant = r - rs
        start = pl.multiple_of((rs - i * rows_per_tile + rows_per_tile) * GRID_W, GRID_W)
        q0 = pl.multiple_of(rr * GRID_W, GRID_W)
        q = qp_ref[pl.ds(q0, GRID_W), 0:D_A] * jnp.bfloat16(HEAD_DIM ** -0.5)
        zero = jnp.zeros_like(q)
        qs = jnp.concatenate([jnp.where(lane_a == h, q, zero) for h in range(NA_HEADS)], axis=0)
        kw = kwin[pl.ds(start, NA_ROWS * GRID_W), KW_AK:KW_AK + D_A]
        vw = kwin[pl.ds(start, NA_ROWS * GRID_W), KW_AV:KW_AV + D_A]
        s = lax.dot_general(qs, kw, (((1,), (1,)), ((), ())), preferred_element_type=jnp.float32)
        s = s + nab_ref[variant]
        m = jnp.max(s, axis=-1, keepdims=True)
        p = jnp.exp(s - m)
        l = jnp.sum(p, axis=-1, keepdims=True)
        pv = jnp.dot(p.astype(jnp.bfloat16), vw, preferred_element_type=jnp.float32)
        pv = pv * (1.0 / l)
        o = jnp.zeros((GRID_W, D_A), jnp.float32)
        for h in range(NA_HEADS):
            o = o + jnp.where(lane_a == h, pv[h * GRID_W:(h + 1) * GRID_W, :], 0.0)
        mix[pl.ds(q0, GRID_W), 0:D_A] = o.astype(jnp.bfloat16)
        return c

    lax.fori_loop(0, rows_per_tile, na_step, 0)

    lane_b = lax.broadcasted_iota(jnp.int32, (SWA_BLOCK, LANES), 1) // HEAD_DIM
    kcol = lax.broadcasted_iota(jnp.int32, (1, 3 * SWA_BLOCK), 1)
    blocks_per_tile = TOK_TILE // SWA_BLOCK
    nblocks = seq_len // SWA_BLOCK

    def swa_step(sb, c):
        n = i * blocks_per_tile + sb
        q0 = pl.multiple_of(sb * SWA_BLOCK, SWA_BLOCK)
        k0 = pl.multiple_of(TOK_TILE - SWA_BLOCK + sb * SWA_BLOCK, SWA_BLOCK)
        pieces = []
        for g in range(SWA_KV_HEADS):
            for t in range(SWA_REP):
                qt = qp_ref[pl.ds(q0, SWA_BLOCK), D_A + t * LANES:D_A + (t + 1) * LANES]
                qt = qt * jnp.bfloat16(HEAD_DIM ** -0.5)
                pieces.append(jnp.where(lane_b == g, qt, jnp.zeros_like(qt)))
        qs = jnp.concatenate(pieces, axis=0)
        kw = kwin[pl.ds(k0, 3 * SWA_BLOCK), KW_BK:KW_BK + D_BKV]
        vw = kwin[pl.ds(k0, 3 * SWA_BLOCK), KW_BV:KW_BV + D_BKV]
        s = lax.dot_general(qs, kw, (((1,), (1,)), ((), ())), preferred_element_type=jnp.float32)
        s = s + swb_ref[...]
        lo = jnp.where(n == 0, SWA_BLOCK, 0)
        hi = jnp.where(n == nblocks - 1, 2 * SWA_BLOCK, 3 * SWA_BLOCK)
        s = jnp.where((kcol >= lo) & (kcol < hi), s, NEG)
        sink = sink_ref[:, 0:1]
        m = jnp.maximum(jnp.max(s, axis=-1, keepdims=True), sink)
        p = jnp.exp(s - m)
        l = jnp.sum(p, axis=-1, keepdims=True) + jnp.exp(sink - m)
        pv = jnp.dot(p.astype(jnp.bfloat16), vw, preferred_element_type=jnp.float32)
        pv = pv * (1.0 / l)
        for t in range(SWA_REP):
            o0 = pv[t * SWA_BLOCK:(t + 1) * SWA_BLOCK, :]
            o1 = pv[(SWA_REP + t) * SWA_BLOCK:(SWA_REP + t + 1) * SWA_BLOCK, :]
            ot = jnp.where(lane_b == 0, o0, o1)
            mix[pl.ds(q0, SWA_BLOCK), D_A + t * LANES:D_A + (t + 1) * LANES] = ot.astype(jnp.bfloat16)
        return c

    lax.fori_loop(0, blocks_per_tile, swa_step, 0)

    u = kvc_ref[:, KV_CU:KV_CU + D_C].astype(jnp.float32)
    prev_ok = (i > 0).astype(jnp.float32)
    next_ok = (i < nblk - 1).astype(jnp.float32)
    uwin[0:HALO, :] = kvp_ref[TOK_TILE - HALO:TOK_TILE, KV_CU:KV_CU + D_C].astype(jnp.float32) * prev_ok
    uwin[HALO:HALO + TOK_TILE, :] = u
    uwin[HALO + TOK_TILE:2 * HALO + TOK_TILE, :] = kvn_ref[0:HALO, KV_CU:KV_CU + D_C].astype(jnp.float32) * next_ok
    n_ext = TOK_TILE + 2 * HALO
    a2 = uwin[0:n_ext - 1, :] + uwin[1:n_ext, :]
    a4 = a2[0:n_ext - 3, :] + a2[2:n_ext - 1, :]
    a8 = a4[0:n_ext - 7, :] + a4[4:n_ext - 3, :]
    a16 = a8[0:n_ext - 15, :] + a8[8:n_ext - 7, :]
    w2 = a2[7:7 + TOK_TILE, :]
    w4 = a4[6:6 + TOK_TILE, :]
    w8 = a8[4:4 + TOK_TILE, :]
    w16 = a16[0:TOK_TILE, :]
    lane_c = lax.broadcasted_iota(jnp.int32, (TOK_TILE, D_C), 1) // POOL_GROUP_DIM
    pooled = jnp.where(lane_c == 0, w2, jnp.where(lane_c == 1, w4, jnp.where(lane_c == 2, w8, w16)))
    half = jnp.where(lane_c == 0, 1, jnp.where(lane_c == 1, 2, jnp.where(lane_c == 2, 4, 8)))
    pos = i * TOK_TILE + lax.broadcasted_iota(jnp.int32, (TOK_TILE, D_C), 0)
    cnt = (jnp.minimum(pos + half, seq_len) - jnp.maximum(pos - half, 0)).astype(jnp.float32)
    d = (pooled / cnt - u).astype(jnp.bfloat16)
    oc = jnp.dot(d, poolw_ref[...], preferred_element_type=jnp.float32) * pools_ref[...]
    mix[:, D_A + D_B:D_MIX] = oc.astype(jnp.bfloat16)

    xm = x_ref[...] + jnp.dot(mix[...], wout_ref[...], preferred_element_type=jnp.float32)
    xmid_ref[...] = xm

    xn = _rmsnorm_f32(xm, g2_ref[...]).astype(jnp.bfloat16)
    logits = jnp.dot(xn, rw_ref[...], preferred_element_type=jnp.float32) + rb_ref[...]
    lane = lax.broadcasted_iota(jnp.int32, (TOK_TILE, LANES), 1)
    is_g = lane < N_GROUPS
    gl = jnp.where(is_g, logits, NEG)
    gmax = jnp.max(gl, axis=-1, keepdims=True)
    gtop = jnp.min(jnp.where(is_g & (gl == gmax), lane, LANES), axis=-1, keepdims=True)
    gprob = 1.0 / jnp.sum(jnp.exp(gl - gmax), axis=-1, keepdims=True)
    e_lo = ROUTE_LANE0 + gtop * EXPERTS_PER_GROUP
    in_grp = (lane >= e_lo) & (lane < e_lo + EXPERTS_PER_GROUP)
    el = jnp.where(in_grp, logits, NEG)
    m1 = jnp.max(el, axis=-1, keepdims=True)
    i1 = jnp.min(jnp.where(in_grp & (el == m1), lane, LANES), axis=-1, keepdims=True)
    el2 = jnp.where(lane == i1, NEG, el)
    m2 = jnp.max(el2, axis=-1, keepdims=True)
    i2 = jnp.min(jnp.where(in_grp & (lane != i1) & (el2 == m2), lane, LANES), axis=-1, keepdims=True)
    r21 = jnp.exp(m2 - m1)
    gate1 = gprob / (1.0 + r21)
    gate2 = gprob * r21 / (1.0 + r21)

    oh1 = lane == i1
    oh2 = lane == i2
    oh = jnp.where(oh1 | oh2, 1.0, 0.0)
    earlier = jnp.dot(tri_ref[...], oh.astype(jnp.bfloat16), preferred_element_type=jnp.float32)
    n_e = jnp.sum(oh, axis=0, keepdims=True)
    chunks_e = jnp.floor((n_e + (CHUNK - 1)) * (1.0 / CHUNK))
    seg0 = jnp.dot(jnp.broadcast_to(chunks_e, (SUBLANES, LANES)).astype(jnp.bfloat16), utri_ref[...],
                   preferred_element_type=jnp.float32)[0:1, :] * CHUNK
    base = earlier + seg0
    lp1 = jnp.sum(jnp.where(oh1, base, 0.0), axis=-1, keepdims=True)
    lp2 = jnp.sum(jnp.where(oh2, base, 0.0), axis=-1, keepdims=True)
    info = jnp.where(lane == 0, lp1, jnp.where(lane == 1, lp2, 0.0))
    rinfo_ref[...] = info
    cnt_ref[...] = jnp.broadcast_to(n_e, cnt_ref.shape)

    info_t = info.T
    prow = lax.broadcasted_iota(jnp.int32, (LOCAL_ROWS, TOK_TILE), 0).astype(jnp.float32)
    sel1 = jnp.where(prow == info_t[0:1, :], 1.0, 0.0).astype(jnp.bfloat16)
    sel2 = jnp.where(prow == info_t[1:2, :], 1.0, 0.0).astype(jnp.bfloat16)
    xs_ref[...] = jnp.dot(sel1 + sel2, xn, preferred_element_type=jnp.float32).astype(jnp.bfloat16)

    def pieces(g):
        hi = g.astype(jnp.bfloat16).astype(jnp.float32)
        mid = (g - hi).astype(jnp.bfloat16).astype(jnp.float32)
        lo = g - hi - mid
        return jnp.where(lane == 0, hi, jnp.where(lane == 1, mid, jnp.where(lane == 2, lo, 0.0))).astype(jnp.bfloat16)

    gloc_ref[...] = (jnp.dot(sel1, pieces(gate1), preferred_element_type=jnp.float32)
                     + jnp.dot(sel2, pieces(gate2), preferred_element_type=jnp.float32))


def _mixer(x2, qp, kv, nab, swb, sinkcol, poolw, pools, wout, g2, rw, rb, tri, utri, *, batch, seq_len):
    T = x2.shape[0]
    nblk = seq_len // TOK_TILE
    n_tok_tiles = T // TOK_TILE

    def cur(b, i):
        return (b * nblk + i, 0)

    def prev(b, i):
        return (b * nblk + jnp.maximum(i - 1, 0), 0)

    def nxt(b, i):
        return (b * nblk + jnp.minimum(i + 1, nblk - 1), 0)

    def const2(b, i):
        return (0, 0)

    def const3(b, i):
        return (0, 0, 0)

    return pl.pallas_call(
        functools.partial(_mixer_kernel, seq_len=seq_len),
        grid=(batch, nblk),
        in_specs=[
            pl.BlockSpec((TOK_TILE, D_MODEL), cur),
            pl.BlockSpec((TOK_TILE, D_QP), cur),
            pl.BlockSpec((TOK_TILE, D_KV), prev),
            pl.BlockSpec((TOK_TILE, D_KV), cur),
            pl.BlockSpec((TOK_TILE, D_KV), nxt),
            pl.BlockSpec(nab.shape, const3),
            pl.BlockSpec(swb.shape, const2),
            pl.BlockSpec(sinkcol.shape, const2),
            pl.BlockSpec(poolw.shape, const2),
            pl.BlockSpec(pools.shape, const2),
            pl.BlockSpec(wout.shape, const2),
            pl.BlockSpec(g2.shape, const2),
            pl.BlockSpec(rw.shape, const2),
            pl.BlockSpec(rb.shape, const2),
            pl.BlockSpec(tri.shape, const2),
            pl.BlockSpec(utri.shape, const2),
        ],
        out_specs=[
            pl.BlockSpec((TOK_TILE, D_MODEL), cur),
            pl.BlockSpec((LOCAL_ROWS, D_MODEL), cur),
            pl.BlockSpec((LOCAL_ROWS, LANES), cur),
            pl.BlockSpec((TOK_TILE, LANES), cur),
            pl.BlockSpec((SUBLANES, LANES), cur),
        ],
        out_shape=[
            jax.ShapeDtypeStruct((T, D_MODEL), jnp.float32),
            jax.ShapeDtypeStruct((n_tok_tiles * LOCAL_ROWS, D_MODEL), jnp.bfloat16),
            jax.ShapeDtypeStruct((n_tok_tiles * LOCAL_ROWS, LANES), jnp.float32),
            jax.ShapeDtypeStruct((T, LANES), jnp.float32),
            jax.ShapeDtypeStruct((n_tok_tiles * SUBLANES, LANES), jnp.float32),
        ],
        scratch_shapes=[
            pltpu.VMEM((3 * TOK_TILE, KW_COLS), jnp.bfloat16),
            pltpu.VMEM((TOK_TILE + 2 * HALO, D_C), jnp.float32),
            pltpu.VMEM((TOK_TILE, D_MIX), jnp.bfloat16),
        ],
        compiler_params=pltpu.CompilerParams(
            dimension_semantics=("arbitrary", "arbitrary"), vmem_limit_bytes=VMEM_LIMIT),
        name="mixer",
    )(x2, qp, kv, kv, kv, nab, swb, sinkcol, poolw, pools, wout, g2, rw, rb, tri, utri)


def _chunk_copy(src_hbm, src_chunk, dst, dst_chunk, sem):
    return pltpu.make_async_copy(
        src_hbm.at[pl.ds(pl.multiple_of(src_chunk * CHUNK, CHUNK), CHUNK)],
        dst.at[pl.ds(dst_chunk * CHUNK, CHUNK)],
        sem)


def _expert_kernel(te_ref, nused_ref, csrc_ref, xs_hbm, gl_hbm, wg_ref, wu_ref, wd_ref, y_ref,
                   xbuf, gbuf, sem):
    j = pl.program_id(0)
    slot = j % 2
    n_used = nused_ref[0]

    def start_gather(tile, s):
        for c in range(TILE_CHUNKS):
            src = csrc_ref[tile * TILE_CHUNKS + c]
            _chunk_copy(xs_hbm, src, xbuf.at[s], c, sem.at[0, s]).start()
            _chunk_copy(gl_hbm, src, gbuf.at[s], c, sem.at[1, s]).start()

    @pl.when(j == 0)
    def _():
        start_gather(0, 0)

    @pl.when(j + 1 < n_used)
    def _():
        start_gather(j + 1, 1 - slot)

    @pl.when(j < n_used)
    def _():
        pltpu.make_async_copy(xs_hbm.at[pl.ds(0, EXP_TILE)], xbuf.at[slot], sem.at[0, slot]).wait()
        pltpu.make_async_copy(gl_hbm.at[pl.ds(0, EXP_TILE)], gbuf.at[slot], sem.at[1, slot]).wait()
        xs = xbuf[slot]
        gp = gbuf[slot]
        gate_w = gp[:, 0:1] + gp[:, 1:2] + gp[:, 2:3]
        gate = jnp.dot(xs, wg_ref[0, 0].astype(jnp.bfloat16), preferred_element_type=jnp.float32)
        up = jnp.dot(xs, wu_ref[0, 0].astype(jnp.bfloat16), preferred_element_type=jnp.float32)
        act = (gate * (1.0 / (1.0 + jnp.exp(-gate))) * up * gate_w).astype(jnp.bfloat16)
        y = jnp.dot(act, wd_ref[0, 0].astype(jnp.bfloat16), preferred_element_type=jnp.float32)
        y_ref[...] = y.astype(jnp.bfloat16)

    @pl.when(j >= n_used)
    def _():
        y_ref[...] = jnp.zeros_like(y_ref)


def _experts(layer, tile_expert, n_used, chunk_src, xs_local, gate_local, wg, wu, wd):
    n_tiles = tile_expert.shape[0]

    def w_map(j, te, nu, cs):
        return (layer, te[j], 0, 0)

    return pl.pallas_call(
        _expert_kernel,
        grid_spec=pltpu.PrefetchScalarGridSpec(
            num_scalar_prefetch=3,
            grid=(n_tiles,),
            in_specs=[
                pl.BlockSpec(memory_space=pl.ANY),
                pl.BlockSpec(memory_space=pl.ANY),
                pl.BlockSpec((1, 1, D_MODEL, D_EXPERT), w_map),
                pl.BlockSpec((1, 1, D_MODEL, D_EXPERT), w_map),
                pl.BlockSpec((1, 1, D_EXPERT, D_MODEL), w_map),
            ],
            out_specs=pl.BlockSpec((EXP_TILE, D_MODEL), lambda j, te, nu, cs: (j, 0)),
            scratch_shapes=[
                pltpu.VMEM((2, EXP_TILE, D_MODEL), jnp.bfloat16),
                pltpu.VMEM((2, EXP_TILE, LANES), jnp.float32),
                pltpu.SemaphoreType.DMA((2, 2)),
            ],
        ),
        out_shape=jax.ShapeDtypeStruct((n_tiles * EXP_TILE, D_MODEL), jnp.bfloat16),
        compiler_params=pltpu.CompilerParams(
            dimension_semantics=("arbitrary",), vmem_limit_bytes=VMEM_LIMIT),
        name="experts",
    )(tile_expert, n_used, chunk_src, xs_local, gate_local, wg, wu, wd)


def _combine_kernel(ctab_ref, xmid_ref, rinfo_ref, g_ref, ys_hbm, out_ref, ybuf, sem, *, final_norm):
    i = pl.program_id(0)
    nt = pl.num_programs(0)
    slot = i % 2

    def start_gather(tile, s):
        for c in range(LOCAL_CHUNKS):
            _chunk_copy(ys_hbm, ctab_ref[tile * LOCAL_CHUNKS + c], ybuf.at[s], c, sem.at[s]).start()

    @pl.when(i == 0)
    def _():
        start_gather(0, 0)

    @pl.when(i + 1 < nt)
    def _():
        start_gather(i + 1, 1 - slot)

    pltpu.make_async_copy(ys_hbm.at[pl.ds(0, LOCAL_ROWS)], ybuf.at[slot], sem.at[slot]).wait()

    info = rinfo_ref[...]
    pcol = lax.broadcasted_iota(jnp.int32, (TOK_TILE, LOCAL_ROWS), 1).astype(jnp.float32)
    pick = jnp.where((pcol == info[:, 0:1]) | (pcol == info[:, 1:2]), 1.0, 0.0).astype(jnp.bfloat16)
    out = xmid_ref[...] + jnp.dot(pick, ybuf[slot], preferred_element_type=jnp.float32)
    if final_norm:
        out = _rmsnorm_f32(out, g_ref[...])
    out_ref[...] = out


def _combine(chunk_tab, xmid, rinfo, g, ys, *, final_norm):
    T = xmid.shape[0]
    return pl.pallas_call(
        functools.partial(_combine_kernel, final_norm=final_norm),
        grid_spec=pltpu.PrefetchScalarGridSpec(
            num_scalar_prefetch=1,
            grid=(T // TOK_TILE,),
            in_specs=[
                pl.BlockSpec((TOK_TILE, D_MODEL), lambda i, ct: (i, 0)),
                pl.BlockSpec((TOK_TILE, LANES), lambda i, ct: (i, 0)),
                pl.BlockSpec((1, D_MODEL), lambda i, ct: (0, 0)),
                pl.BlockSpec(memory_space=pl.ANY),
            ],
            out_specs=pl.BlockSpec((TOK_TILE, D_MODEL), lambda i, ct: (i, 0)),
            scratch_shapes=[
                pltpu.VMEM((2, LOCAL_ROWS, D_MODEL), jnp.bfloat16),
                pltpu.SemaphoreType.DMA((2,)),
            ],
        ),
        out_shape=jax.ShapeDtypeStruct((T, D_MODEL), jnp.float32),
        compiler_params=pltpu.CompilerParams(
            dimension_semantics=("arbitrary",), vmem_limit_bytes=VMEM_LIMIT),
        name="combine",
    )(chunk_tab, xmid, rinfo, g, ys)


def _pair_heads(a, axis):
    shape = a.shape
    split = shape[:axis] + (SWA_KV_HEADS, SWA_REP, HEAD_DIM) + shape[axis + 1:]
    return jnp.swapaxes(a.reshape(split), axis, axis + 1).reshape(shape)


def _in_proj_weight(w):
    off_bq = 3 * D_A
    return jnp.concatenate(
        [w[:, 0:D_A], _pair_heads(w[:, off_bq:off_bq + D_B], 1), w[:, D_A:off_bq], w[:, off_bq + D_B:]],
        axis=1).astype(jnp.bfloat16)


def _out_proj_weight(w):
    return jnp.concatenate(
        [w[0:D_A], _pair_heads(w[D_A:D_A + D_B], 0), w[D_A + D_B:]], axis=0).astype(jnp.bfloat16)


def _na_bias_table(rel_bias):
    c = np.arange(GRID_W)[:, None]
    cp = np.arange(GRID_W)[None, :]
    cs = np.clip(c - NA_COLS // 2, 0, GRID_W - NA_COLS)
    valid = (cp >= cs) & (cp < cs + NA_COLS)
    d = np.arange(2 * NA_COLS - 1)[:, None, None]
    onehot = ((cp - c + (NA_COLS - 1))[None] == d) & valid[None]
    full = jnp.einsum("hrd,dcm->hrcm", rel_bias.astype(jnp.float32), jnp.asarray(onehot, jnp.float32),
                      precision=lax.Precision.HIGHEST)
    full = jnp.where(jnp.asarray(valid)[None, None], full, NEG)
    variants = []
    for k in range(NA_ROWS):
        win = full[:, NA_ROWS - 1 - k:2 * NA_ROWS - 1 - k]
        variants.append(win.transpose(0, 2, 1, 3).reshape(NA_HEADS * GRID_W, NA_ROWS * GRID_W))
    return jnp.stack(variants)


def _swa_bias_table():
    slopes = (2.0 ** (-8.0 * np.arange(1, SWA_Q_HEADS + 1) / SWA_Q_HEADS)).astype(np.float32)
    qi = np.arange(SWA_BLOCK)[:, None]
    ki = np.arange(3 * SWA_BLOCK)[None, :]
    dist = np.abs(ki - qi - SWA_BLOCK).astype(np.float32)
    tab = np.where(dist <= SWA_WINDOW, -slopes[:, None, None] * dist[None], np.float32(NEG))
    return jnp.asarray(tab.reshape(SWA_Q_HEADS * SWA_BLOCK, 3 * SWA_BLOCK).astype(np.float32))


def _block_diag(pool_w):
    n = pool_w.shape[0]
    out = jnp.zeros((n * POOL_GROUP_DIM, n * POOL_GROUP_DIM), pool_w.dtype)
    for gi in range(n):
        sl = slice(gi * POOL_GROUP_DIM, (gi + 1) * POOL_GROUP_DIM)
        out = out.at[sl, sl].set(pool_w[gi])
    return out


def _router_weights(rg_w, rg_b, re_w, re_b):
    w = jnp.zeros((D_MODEL, LANES), jnp.float32)
    w = w.at[:, 0:N_GROUPS].set(rg_w).at[:, ROUTE_LANE0:ROUTE_LANE0 + N_EXPERTS].set(re_w)
    bias = jnp.zeros((1, LANES), jnp.float32)
    bias = bias.at[0, 0:N_GROUPS].set(rg_b).at[0, ROUTE_LANE0:ROUTE_LANE0 + N_EXPERTS].set(re_b)
    return w.astype(jnp.bfloat16), bias


def _dispatch_tables(cnt, n_tiles):
    nb = cnt.shape[0] // SUBLANES
    n = cnt.reshape(nb, SUBLANES, LANES)[:, 0, ROUTE_LANE0:ROUTE_LANE0 + N_EXPERTS].astype(jnp.int32)
    g = (n + (CHUNK - 1)) // CHUNK
    l_end = jnp.cumsum(g, axis=1)
    l_off = l_end - g
    c_end = jnp.cumsum(g, axis=0)
    c_off = c_end - g
    tot = c_end[-1]
    tiles = (tot + (TILE_CHUNKS - 1)) // TILE_CHUNKS
    t_end = jnp.cumsum(tiles)
    t_off = t_end - tiles
    n_used = t_end[-1:]

    experts = jnp.arange(N_EXPERTS, dtype=jnp.int32)
    tile_ids = jnp.arange(n_tiles, dtype=jnp.int32)
    tile_expert = jnp.minimum(jnp.sum((t_end[None, :] <= tile_ids[:, None]).astype(jnp.int32), axis=1),
                              N_EXPERTS - 1)

    q = jnp.arange(n_tiles * TILE_CHUNKS, dtype=jnp.int32)
    tile_q = q // TILE_CHUNKS
    oh_e = (jnp.repeat(tile_expert, TILE_CHUNKS)[:, None] == experts[None, :]).astype(jnp.int32)
    ro = q - jnp.sum(oh_e * t_off[None, :], axis=1) * TILE_CHUNKS
    valid = (ro < jnp.sum(oh_e * tot[None, :], axis=1)) & (tile_q < n_used[0])
    c_end_q = jnp.sum(oh_e[None, :, :] * c_end[:, None, :], axis=2)
    c_off_q = jnp.sum(oh_e[None, :, :] * c_off[:, None, :], axis=2)
    l_off_q = jnp.sum(oh_e[None, :, :] * l_off[:, None, :], axis=2)
    b_q = jnp.minimum(jnp.sum((c_end_q <= ro[None, :]).astype(jnp.int32), axis=0), nb - 1)
    oh_b = (jnp.arange(nb, dtype=jnp.int32)[:, None] == b_q[None, :]).astype(jnp.int32)
    src = b_q * LOCAL_CHUNKS + jnp.sum(oh_b * (l_off_q + ro[None, :] - c_off_q), axis=0)
    chunk_src = jnp.where(valid, src, LOCAL_CHUNKS - 1)

    c = jnp.arange(LOCAL_CHUNKS, dtype=jnp.int32)
    e_c = jnp.minimum(jnp.sum((l_end[:, None, :] <= c[None, :, None]).astype(jnp.int32), axis=2),
                      N_EXPERTS - 1)
    oh_ec = (e_c[:, :, None] == experts[None, None, :]).astype(jnp.int32)
    pos = (jnp.sum(oh_ec * (t_off[None, None, :] * TILE_CHUNKS + c_off[:, None, :] - l_off[:, None, :]), axis=2)
           + c[None, :])
    chunk_tab = jnp.where(c[None, :] < l_end[:, -1:], pos, 0).reshape(-1)
    return tile_expert, n_used, chunk_src, chunk_tab


def kernel(x, norm1_g, w_in, nat_bias, swa_sink, pool_w, pool_scale, w_out, norm2_g, router_g_w,
           router_g_b, router_e_w, router_e_b, expert_w_gate, expert_w_up, expert_w_down, final_g):
    batch, seq_len, _ = x.shape
    depth = w_in.shape[0]
    T = batch * seq_len
    assert seq_len % TOK_TILE == 0 and TOK_TILE % SWA_BLOCK == 0 and TOK_TILE % GRID_W == 0
    max_chunks = (2 * T) // CHUNK + (T // TOK_TILE) * N_EXPERTS
    n_tiles = max_chunks // TILE_CHUNKS + N_EXPERTS

    swb = _swa_bias_table()
    tri = jnp.asarray(np.tril(np.ones((TOK_TILE, TOK_TILE), np.float32), -1)).astype(jnp.bfloat16)
    utri = jnp.asarray(np.triu(np.ones((LANES, LANES), np.float32), 1)).astype(jnp.bfloat16)

    x2 = x.reshape(T, D_MODEL)
    for l in range(depth):
        w_in_l = _in_proj_weight(w_in[l])
        w_out_l = _out_proj_weight(w_out[l])
        nab = _na_bias_table(nat_bias[l])
        sinkcol = jnp.broadcast_to(
            jnp.repeat(swa_sink[l].astype(jnp.float32), SWA_BLOCK)[:, None], (SWA_Q_HEADS * SWA_BLOCK, LANES))
        poolw = _block_diag(pool_w[l]).astype(jnp.bfloat16)
        pools = pool_scale[l].reshape(1, D_C).astype(jnp.float32)
        rw, rb = _router_weights(router_g_w[l], router_g_b[l], router_e_w[l], router_e_b[l])

        qp, kv = _norm_proj(x2, norm1_g[l].reshape(1, D_MODEL), w_in_l)
        xmid, xs_local, gate_local, rinfo, cnt = _mixer(
            x2, qp, kv, nab, swb, sinkcol, poolw, pools, w_out_l, norm2_g[l].reshape(1, D_MODEL),
            rw, rb, tri, utri, batch=batch, seq_len=seq_len)
        tile_expert, n_used, chunk_src, chunk_tab = _dispatch_tables(cnt, n_tiles)
        ys = _experts(l, tile_expert, n_used, chunk_src, xs_local, gate_local,
                      expert_w_gate, expert_w_up, expert_w_down)
        x2 = _combine(chunk_tab, xmid, rinfo, final_g.reshape(1, D_MODEL), ys, final_norm=(l == depth - 1))
    return x2.reshape(batch, seq_len, D_MODEL)
```

```python
import functools

import jax
import jax.numpy as jnp
import numpy as np
from jax import lax
from jax.experimental import pallas as pl
from jax.experimental.pallas import tpu as pltpu

D_MODEL = 1024
GRID_W = 64
HEAD_DIM = 64
NA_HEADS = 4
NA_ROWS = 8
NA_COLS = 16
SWA_Q_HEADS = 8
SWA_KV_HEADS = 2
SWA_REP = SWA_Q_HEADS // SWA_KV_HEADS
SWA_WINDOW = 128
SWA_BLOCK = 128
POOL_WINDOWS = (2, 4, 8, 16)
POOL_GROUP_DIM = 64
D_A = NA_HEADS * HEAD_DIM
D_B = SWA_Q_HEADS * HEAD_DIM
D_BKV = SWA_KV_HEADS * HEAD_DIM
D_C = len(POOL_WINDOWS) * POOL_GROUP_DIM
D_MIX = D_A + D_B + D_C
D_QP = D_A + D_B
D_KV = 2 * D_A + 2 * D_BKV + D_C
D_IN = D_QP + D_KV
N_GROUPS = 4
EXPERTS_PER_GROUP = 8
N_EXPERTS = N_GROUPS * EXPERTS_PER_GROUP
D_EXPERT = 256
RMS_EPS = 1e-6
NEG = -1e30

LANES = 128
SUBLANES = 8
ROW_CHUNKS = D_MODEL // LANES

TOK_TILE = 512
EXP_TILE = 256
CHUNK = 16
LOCAL_CHUNKS = (2 * TOK_TILE + N_EXPERTS * (CHUNK - 1)) // CHUNK + 2
LOCAL_ROWS = LOCAL_CHUNKS * CHUNK
TILE_CHUNKS = EXP_TILE // CHUNK
ROUTE_LANE0 = 8
NA_ROWS_PER_STEP = 4
HALO = 8
VMEM_LIMIT = 56 * 1024 * 1024


def _rmsnorm_f32(x, g):
    return x * lax.rsqrt(jnp.mean(x * x, axis=-1, keepdims=True) + RMS_EPS) * g


def _norm_proj_kernel(x_ref, g_ref, w_ref, qp_ref, kv_ref):
    xn = _rmsnorm_f32(x_ref[...], g_ref[...]).astype(jnp.bfloat16)
    proj = jnp.dot(xn, w_ref[...], preferred_element_type=jnp.float32)
    qp_ref[...] = proj[:, :D_QP].astype(jnp.bfloat16)
    kv_ref[...] = proj[:, D_QP:].astype(jnp.bfloat16)


def _norm_proj(x2, g, w):
    T = x2.shape[0]
    return pl.pallas_call(
        _norm_proj_kernel,
        grid=(T // TOK_TILE,),
        in_specs=[
            pl.BlockSpec((TOK_TILE, D_MODEL), lambda i: (i, 0)),
            pl.BlockSpec((1, D_MODEL), lambda i: (0, 0)),
            pl.BlockSpec((D_MODEL, D_IN), lambda i: (0, 0)),
        ],
        out_specs=[
            pl.BlockSpec((TOK_TILE, D_QP), lambda i: (i, 0)),
            pl.BlockSpec((TOK_TILE, D_KV), lambda i: (i, 0)),
        ],
        out_shape=[
            jax.ShapeDtypeStruct((T, D_QP), jnp.bfloat16),
            jax.ShapeDtypeStruct((T, D_KV), jnp.bfloat16),
        ],
        compiler_params=pltpu.CompilerParams(
            dimension_semantics=("arbitrary",), vmem_limit_bytes=VMEM_LIMIT),
        name="norm_proj",
    )(x2, g, w)


KW_AK, KW_AV, KW_BK, KW_BV = 0, D_A, 2 * D_A, 2 * D_A + D_BKV
KW_COLS = 2 * D_A + 2 * D_BKV
KV_CU = KW_COLS


def _mixer_kernel(x_ref, qp_ref, kvp_ref, kvc_ref, kvn_ref, nab_ref, swb_ref, sink_ref,
                  poolw_ref, pools_ref, wout_ref, g2_ref, rw_ref, rb_ref, tri_ref, utri_ref,
                  xmid_ref, xs_ref, gloc_ref, rinfo_ref, cnt_ref,
                  kwin, uwin, mix, *, seq_len):
    b = pl.program_id(0)
    i = pl.program_id(1)
    nblk = pl.num_programs(1)
    rows_per_tile = TOK_TILE // GRID_W
    grid_rows = seq_len // GRID_W

    kwin[0:TOK_TILE, :] = kvp_ref[:, 0:KW_COLS]
    kwin[TOK_TILE:2 * TOK_TILE, :] = kvc_ref[:, 0:KW_COLS]
    kwin[2 * TOK_TILE:3 * TOK_TILE, :] = kvn_ref[:, 0:KW_COLS]

    lane_a = lax.broadcasted_iota(jnp.int32, (GRID_W, D_A), 1) // HEAD_DIM

    def na_row(rr):
        r = i * rows_per_tile + rr
        rs = jnp.clip(r - NA_ROWS // 2, 0, grid_rows - NA_ROWS)
        variant = r - rs
        start = pl.multiple_of((rs - i * rows_per_tile + rows_per_tile) * GRID_W, GRID_W)
        q0 = pl.multiple_of(rr * GRID_W, GRID_W)
        q = qp_ref[pl.ds(q0, GRID_W), 0:D_A] * jnp.bfloat16(HEAD_DIM ** -0.5)
        zero = jnp.zeros_like(q)
        qs = jnp.concatenate([jnp.where(lane_a == h, q, zero) for h in range(NA_HEADS)], axis=0)
        kw = kwin[pl.ds(start, NA_ROWS * GRID_W), KW_AK:KW_AK + D_A]
        vw = kwin[pl.ds(start, NA_ROWS * GRID_W), KW_AV:KW_AV + D_A]
        s = lax.dot_general(qs, kw, (((1,), (1,)), ((), ())), preferred_element_type=jnp.float32)
        s = s + nab_ref[variant]
        m = jnp.max(s, axis=-1, keepdims=True)
        p = jnp.exp(s - m)
        l = jnp.sum(p, axis=-1, keepdims=True)
        pv = jnp.dot(p.astype(jnp.bfloat16), vw, preferred_element_type=jnp.float32)
        pv = pv * (1.0 / l)
        o = jnp.zeros((GRID_W, D_A), jnp.float32)
        for h in range(NA_HEADS):
            o = o + jnp.where(lane_a == h, pv[h * GRID_W:(h + 1) * GRID_W, :], 0.0)
        mix[pl.ds(q0, GRID_W), 0:D_A] = o.astype(jnp.bfloat16)

    def na_step(it, c):
        for k in range(NA_ROWS_PER_STEP):
            na_row(it * NA_ROWS_PER_STEP + k)
        return c

    lax.fori_loop(0, rows_per_tile // NA_ROWS_PER_STEP, na_step, 0)

    lane_b = lax.broadcasted_iota(jnp.int32, (SWA_BLOCK, LANES), 1) // HEAD_DIM
    blocks_per_tile = TOK_TILE // SWA_BLOCK
    nblocks = seq_len // SWA_BLOCK

    ones_v = jnp.ones((3 * SWA_BLOCK, LANES), jnp.bfloat16)

    def swa_step(sb, c):
        n = i * blocks_per_tile + sb
        variant = jnp.where(n == 0, 0, jnp.where(n == nblocks - 1, 2, 1))
        q0 = pl.multiple_of(sb * SWA_BLOCK, SWA_BLOCK)
        k0 = pl.multiple_of(TOK_TILE - SWA_BLOCK + sb * SWA_BLOCK, SWA_BLOCK)
        kw = kwin[pl.ds(k0, 3 * SWA_BLOCK), KW_BK:KW_BK + D_BKV]
        vaug = jnp.concatenate([kwin[pl.ds(k0, 3 * SWA_BLOCK), KW_BV:KW_BV + D_BKV], ones_v], axis=1)
        outs = []
        for g in range(SWA_KV_HEADS):
            pieces = []
            for t in range(SWA_REP):
                qt = qp_ref[pl.ds(q0, SWA_BLOCK), D_A + t * LANES:D_A + (t + 1) * LANES]
                qt = qt * jnp.bfloat16(HEAD_DIM ** -0.5)
                pieces.append(jnp.where(lane_b == g, qt, jnp.zeros_like(qt)))
            qs = jnp.concatenate(pieces, axis=0)
            r0 = g * SWA_REP * SWA_BLOCK
            s = lax.dot_general(qs, kw, (((1,), (1,)), ((), ())), preferred_element_type=jnp.float32)
            s = s + swb_ref[variant, r0:r0 + SWA_REP * SWA_BLOCK, :]
            sink = sink_ref[r0:r0 + SWA_REP * SWA_BLOCK, :]
            m = jnp.broadcast_to(jnp.max(s, axis=-1, keepdims=True), sink.shape)
            m = jnp.maximum(m, sink)
            p = jnp.exp(s - jnp.concatenate([m, m, m], axis=1)).astype(jnp.bfloat16)
            pv = jnp.dot(p, vaug, preferred_element_type=jnp.float32)
            l = pv[:, LANES:2 * LANES] + jnp.exp(sink - m)
            outs.append(pv[:, 0:LANES] * (1.0 / l))
        for t in range(SWA_REP):
            o0 = outs[0][t * SWA_BLOCK:(t + 1) * SWA_BLOCK, :]
            o1 = outs[1][t * SWA_BLOCK:(t + 1) * SWA_BLOCK, :]
            ot = jnp.where(lane_b == 0, o0, o1)
            mix[pl.ds(q0, SWA_BLOCK), D_A + t * LANES:D_A + (t + 1) * LANES] = ot.astype(jnp.bfloat16)
        return c

    lax.fori_loop(0, blocks_per_tile, swa_step, 0)

    u = kvc_ref[:, KV_CU:KV_CU + D_C].astype(jnp.float32)
    prev_ok = (i > 0).astype(jnp.float32)
    next_ok = (i < nblk - 1).astype(jnp.float32)
    uwin[0:HALO, :] = kvp_ref[TOK_TILE - HALO:TOK_TILE, KV_CU:KV_CU + D_C].astype(jnp.float32) * prev_ok
    uwin[HALO:HALO + TOK_TILE, :] = u
    uwin[HALO + TOK_TILE:2 * HALO + TOK_TILE, :] = kvn_ref[0:HALO, KV_CU:KV_CU + D_C].astype(jnp.float32) * next_ok
    n_ext = TOK_TILE + 2 * HALO
    a2 = uwin[0:n_ext - 1, :] + uwin[1:n_ext, :]
    a4 = a2[0:n_ext - 3, :] + a2[2:n_ext - 1, :]
    a8 = a4[0:n_ext - 7, :] + a4[4:n_ext - 3, :]
    a16 = a8[0:n_ext - 15, :] + a8[8:n_ext - 7, :]
    w2 = a2[7:7 + TOK_TILE, :]
    w4 = a4[6:6 + TOK_TILE, :]
    w8 = a8[4:4 + TOK_TILE, :]
    w16 = a16[0:TOK_TILE, :]
    lane_c = lax.broadcasted_iota(jnp.int32, (TOK_TILE, D_C), 1) // POOL_GROUP_DIM
    pooled = jnp.where(lane_c == 0, w2, jnp.where(lane_c == 1, w4, jnp.where(lane_c == 2, w8, w16)))
    half = jnp.where(lane_c == 0, 1, jnp.where(lane_c == 1, 2, jnp.where(lane_c == 2, 4, 8)))
    pos = i * TOK_TILE + lax.broadcasted_iota(jnp.int32, (TOK_TILE, D_C), 0)
    cnt = (jnp.minimum(pos + half, seq_len) - jnp.maximum(pos - half, 0)).astype(jnp.float32)
    d = (pooled / cnt - u).astype(jnp.bfloat16)
    oc = jnp.dot(d, poolw_ref[...], preferred_element_type=jnp.float32) * pools_ref[...]
    mix[:, D_A + D_B:D_MIX] = oc.astype(jnp.bfloat16)

    xm = x_ref[...] + jnp.dot(mix[...], wout_ref[...], preferred_element_type=jnp.float32)
    xmid_ref[...] = xm

    xn = _rmsnorm_f32(xm, g2_ref[...]).astype(jnp.bfloat16)
    logits = jnp.dot(xn, rw_ref[...], preferred_element_type=jnp.float32) + rb_ref[...]
    lane = lax.broadcasted_iota(jnp.int32, (TOK_TILE, LANES), 1).astype(jnp.float32)
    is_g = lane < N_GROUPS
    gl = jnp.where(is_g, logits, NEG)
    gmax = jnp.max(gl, axis=-1, keepdims=True)
    gtop = jnp.min(jnp.where(is_g & (gl == gmax), lane, float(LANES)), axis=-1, keepdims=True)
    gprob = 1.0 / jnp.sum(jnp.exp(gl - gmax), axis=-1, keepdims=True)
    e_lo = ROUTE_LANE0 + gtop * EXPERTS_PER_GROUP
    in_grp = (lane >= e_lo) & (lane < e_lo + EXPERTS_PER_GROUP)
    el = jnp.where(in_grp, logits, NEG)
    m1 = jnp.max(el, axis=-1, keepdims=True)
    i1 = jnp.min(jnp.where(in_grp & (el == m1), lane, float(LANES)), axis=-1, keepdims=True)
    el2 = jnp.where(lane == i1, NEG, el)
    m2 = jnp.max(el2, axis=-1, keepdims=True)
    i2 = jnp.min(jnp.where(in_grp & (lane != i1) & (el2 == m2), lane, float(LANES)), axis=-1, keepdims=True)
    r21 = jnp.exp(m2 - m1)
    gate1 = gprob / (1.0 + r21)
    gate2 = gprob * r21 / (1.0 + r21)

    oh1 = lane == i1
    oh2 = lane == i2
    oh = jnp.where(oh1 | oh2, 1.0, 0.0)
    earlier = jnp.dot(tri_ref[...], oh.astype(jnp.bfloat16), preferred_element_type=jnp.float32)
    n_e = jnp.sum(oh, axis=0, keepdims=True)
    chunks_e = jnp.floor((n_e + (CHUNK - 1)) * (1.0 / CHUNK))
    seg0 = jnp.dot(jnp.broadcast_to(chunks_e, (SUBLANES, LANES)).astype(jnp.bfloat16), utri_ref[...],
                   preferred_element_type=jnp.float32)[0:1, :] * CHUNK
    base = earlier + seg0
    lp1 = jnp.sum(jnp.where(oh1, base, 0.0), axis=-1, keepdims=True)
    lp2 = jnp.sum(jnp.where(oh2, base, 0.0), axis=-1, keepdims=True)
    info = jnp.where(lane == 0, lp1, jnp.where(lane == 1, lp2, 0.0))
    rinfo_ref[...] = info
    cnt_ref[...] = jnp.broadcast_to(n_e, cnt_ref.shape)

    def pieces(g):
        hi = g.astype(jnp.bfloat16).astype(jnp.float32)
        mid = (g - hi).astype(jnp.bfloat16).astype(jnp.float32)
        return hi, mid, g - hi - mid

    aux = jnp.zeros((TOK_TILE, LANES), jnp.float32)
    for k, piece in enumerate(pieces(gate1) + pieces(gate2) + (i1,)):
        aux = jnp.where(lane == k, piece, aux)

    info_t = info.T
    prow = lax.broadcasted_iota(jnp.int32, (LOCAL_ROWS, TOK_TILE), 0).astype(jnp.float32)
    sel = jnp.where((prow == info_t[0:1, :]) | (prow == info_t[1:2, :]), 1.0, 0.0).astype(jnp.bfloat16)
    moved = jnp.dot(sel, jnp.concatenate([xn, aux.astype(jnp.bfloat16)], axis=1),
                    preferred_element_type=jnp.float32)
    xs_ref[...] = moved[:, 0:D_MODEL].astype(jnp.bfloat16)
    gloc_ref[...] = moved[:, D_MODEL:]


def _mixer(x2, qp, kv, nab, swb, sinkcol, poolw, pools, wout, g2, rw, rb, tri, utri, *, batch, seq_len):
    T = x2.shape[0]
    nblk = seq_len // TOK_TILE
    n_tok_tiles = T // TOK_TILE

    def cur(b, i):
        return (b * nblk + i, 0)

    def prev(b, i):
        return (b * nblk + jnp.maximum(i - 1, 0), 0)

    def nxt(b, i):
        return (b * nblk + jnp.minimum(i + 1, nblk - 1), 0)

    def resident(a):
        zeros = (0,) * a.ndim
        return pl.BlockSpec(a.shape, lambda b, i: zeros, pipeline_mode=pl.Buffered(1))

    return pl.pallas_call(
        functools.partial(_mixer_kernel, seq_len=seq_len),
        grid=(batch, nblk),
        in_specs=[
            pl.BlockSpec((TOK_TILE, D_MODEL), cur),
            pl.BlockSpec((TOK_TILE, D_QP), cur),
            pl.BlockSpec((TOK_TILE, D_KV), prev),
            pl.BlockSpec((TOK_TILE, D_KV), cur),
            pl.BlockSpec((TOK_TILE, D_KV), nxt),
            resident(nab), resident(swb), resident(sinkcol), resident(poolw), resident(pools),
            resident(wout), resident(g2), resident(rw), resident(rb), resident(tri), resident(utri),
        ],
        out_specs=[
            pl.BlockSpec((TOK_TILE, D_MODEL), cur),
            pl.BlockSpec((LOCAL_ROWS, D_MODEL), cur),
            pl.BlockSpec((LOCAL_ROWS, LANES), cur),
            pl.BlockSpec((TOK_TILE, LANES), cur),
            pl.BlockSpec((SUBLANES, LANES), cur),
        ],
        out_shape=[
            jax.ShapeDtypeStruct((T, D_MODEL), jnp.float32),
            jax.ShapeDtypeStruct((n_tok_tiles * LOCAL_ROWS, D_MODEL), jnp.bfloat16),
            jax.ShapeDtypeStruct((n_tok_tiles * LOCAL_ROWS, LANES), jnp.float32),
            jax.ShapeDtypeStruct((T, LANES), jnp.float32),
            jax.ShapeDtypeStruct((n_tok_tiles * SUBLANES, LANES), jnp.float32),
        ],
        scratch_shapes=[
            pltpu.VMEM((3 * TOK_TILE, KW_COLS), jnp.bfloat16),
            pltpu.VMEM((TOK_TILE + 2 * HALO, D_C), jnp.float32),
            pltpu.VMEM((TOK_TILE, D_MIX), jnp.bfloat16),
        ],
        compiler_params=pltpu.CompilerParams(
            dimension_semantics=("arbitrary", "arbitrary"), vmem_limit_bytes=VMEM_LIMIT),
        name="mixer",
    )(x2, qp, kv, kv, kv, nab, swb, sinkcol, poolw, pools, wout, g2, rw, rb, tri, utri)


def _chunk_copy(src_hbm, src_chunk, dst, dst_chunk, sem):
    return pltpu.make_async_copy(
        src_hbm.at[pl.ds(pl.multiple_of(src_chunk * CHUNK, CHUNK), CHUNK)],
        dst.at[pl.ds(dst_chunk * CHUNK, CHUNK)],
        sem)


def _expert_kernel(te_ref, nused_ref, csrc_ref, xs_hbm, gl_hbm, wg_ref, wu_ref, wd_ref, y_ref,
                   xbuf, gbuf, sem):
    j = pl.program_id(0)
    slot = j % 2
    n_used = nused_ref[0]

    def start_gather(tile, s):
        for c in range(TILE_CHUNKS):
            src = csrc_ref[tile * TILE_CHUNKS + c]
            _chunk_copy(xs_hbm, src, xbuf.at[s], c, sem.at[0, s]).start()
            _chunk_copy(gl_hbm, src, gbuf.at[s], c, sem.at[1, s]).start()

    @pl.when(j == 0)
    def _():
        start_gather(0, 0)

    @pl.when(j + 1 < n_used)
    def _():
        start_gather(j + 1, 1 - slot)

    @pl.when(j < n_used)
    def _():
        pltpu.make_async_copy(xs_hbm.at[pl.ds(0, EXP_TILE)], xbuf.at[slot], sem.at[0, slot]).wait()
        pltpu.make_async_copy(gl_hbm.at[pl.ds(0, EXP_TILE)], gbuf.at[slot], sem.at[1, slot]).wait()
        xs = xbuf[slot]
        gp = gbuf[slot]
        first = gp[:, 6:7] == (te_ref[j] + ROUTE_LANE0).astype(jnp.float32)
        gate_w = jnp.where(first, gp[:, 0:1] + gp[:, 1:2] + gp[:, 2:3], gp[:, 3:4] + gp[:, 4:5] + gp[:, 5:6])
        gate = jnp.dot(xs, wg_ref[0, 0].astype(jnp.bfloat16), preferred_element_type=jnp.float32)
        up = jnp.dot(xs, wu_ref[0, 0].astype(jnp.bfloat16), preferred_element_type=jnp.float32)
        act = (gate * (1.0 / (1.0 + jnp.exp(-gate))) * up * gate_w).astype(jnp.bfloat16)
        y = jnp.dot(act, wd_ref[0, 0].astype(jnp.bfloat16), preferred_element_type=jnp.float32)
        y_ref[...] = y.astype(jnp.bfloat16)

    @pl.when(j >= n_used)
    def _():
        y_ref[...] = jnp.zeros_like(y_ref)


def _experts(layer, tile_expert, n_used, chunk_src, xs_local, gate_local, wg, wu, wd):
    n_tiles = tile_expert.shape[0]

    def w_map(j, te, nu, cs):
        return (layer, te[j], 0, 0)

    return pl.pallas_call(
        _expert_kernel,
        grid_spec=pltpu.PrefetchScalarGridSpec(
            num_scalar_prefetch=3,
            grid=(n_tiles,),
            in_specs=[
                pl.BlockSpec(memory_space=pl.ANY),
                pl.BlockSpec(memory_space=pl.ANY),
                pl.BlockSpec((1, 1, D_MODEL, D_EXPERT), w_map),
                pl.BlockSpec((1, 1, D_MODEL, D_EXPERT), w_map),
                pl.BlockSpec((1, 1, D_EXPERT, D_MODEL), w_map),
            ],
            out_specs=pl.BlockSpec((EXP_TILE, D_MODEL), lambda j, te, nu, cs: (j, 0)),
            scratch_shapes=[
                pltpu.VMEM((2, EXP_TILE, D_MODEL), jnp.bfloat16),
                pltpu.VMEM((2, EXP_TILE, LANES), jnp.float32),
                pltpu.SemaphoreType.DMA((2, 2)),
            ],
        ),
        out_shape=jax.ShapeDtypeStruct((n_tiles * EXP_TILE, D_MODEL), jnp.bfloat16),
        compiler_params=pltpu.CompilerParams(
            dimension_semantics=("arbitrary",), vmem_limit_bytes=VMEM_LIMIT),
        name="experts",
    )(tile_expert, n_used, chunk_src, xs_local, gate_local, wg, wu, wd)


def _combine_kernel(ctab_ref, xmid_ref, rinfo_ref, g_ref, ys_hbm, out_ref, ybuf, sem, *, final_norm):
    i = pl.program_id(0)
    nt = pl.num_programs(0)
    slot = i % 2

    def start_gather(tile, s):
        for c in range(LOCAL_CHUNKS):
            _chunk_copy(ys_hbm, ctab_ref[tile * LOCAL_CHUNKS + c], ybuf.at[s], c, sem.at[s]).start()

    @pl.when(i == 0)
    def _():
        start_gather(0, 0)

    @pl.when(i + 1 < nt)
    def _():
        start_gather(i + 1, 1 - slot)

    pltpu.make_async_copy(ys_hbm.at[pl.ds(0, LOCAL_ROWS)], ybuf.at[slot], sem.at[slot]).wait()

    info = rinfo_ref[...]
    pcol = lax.broadcasted_iota(jnp.int32, (TOK_TILE, LOCAL_ROWS), 1).astype(jnp.float32)
    pick = jnp.where((pcol == info[:, 0:1]) | (pcol == info[:, 1:2]), 1.0, 0.0).astype(jnp.bfloat16)
    out = xmid_ref[...] + jnp.dot(pick, ybuf[slot], preferred_element_type=jnp.float32)
    if final_norm:
        out = _rmsnorm_f32(out, g_ref[...])
    out_ref[...] = out


def _combine(chunk_tab, xmid, rinfo, g, ys, *, final_norm):
    T = xmid.shape[0]
    return pl.pallas_call(
        functools.partial(_combine_kernel, final_norm=final_norm),
        grid_spec=pltpu.PrefetchScalarGridSpec(
            num_scalar_prefetch=1,
            grid=(T // TOK_TILE,),
            in_specs=[
                pl.BlockSpec((TOK_TILE, D_MODEL), lambda i, ct: (i, 0)),
                pl.BlockSpec((TOK_TILE, LANES), lambda i, ct: (i, 0)),
                pl.BlockSpec((1, D_MODEL), lambda i, ct: (0, 0)),
                pl.BlockSpec(memory_space=pl.ANY),
            ],
            out_specs=pl.BlockSpec((TOK_TILE, D_MODEL), lambda i, ct: (i, 0)),
            scratch_shapes=[
                pltpu.VMEM((2, LOCAL_ROWS, D_MODEL), jnp.bfloat16),
                pltpu.SemaphoreType.DMA((2,)),
            ],
        ),
        out_shape=jax.ShapeDtypeStruct((T, D_MODEL), jnp.float32),
        compiler_params=pltpu.CompilerParams(
            dimension_semantics=("arbitrary",), vmem_limit_bytes=VMEM_LIMIT),
        name="combine",
    )(chunk_tab, xmid, rinfo, g, ys)


def _pair_heads(a, axis):
    shape = a.shape
    split = shape[:axis] + (SWA_KV_HEADS, SWA_REP, HEAD_DIM) + shape[axis + 1:]
    return jnp.swapaxes(a.reshape(split), axis, axis + 1).reshape(shape)


def _in_proj_weight(w):
    off_bq = 3 * D_A
    return jnp.concatenate(
        [w[:, 0:D_A], _pair_heads(w[:, off_bq:off_bq + D_B], 1), w[:, D_A:off_bq], w[:, off_bq + D_B:]],
        axis=1).astype(jnp.bfloat16)


def _out_proj_weight(w):
    return jnp.concatenate(
        [w[0:D_A], _pair_heads(w[D_A:D_A + D_B], 0), w[D_A + D_B:]], axis=0).astype(jnp.bfloat16)


def _na_bias_table(rel_bias):
    c = np.arange(GRID_W)[:, None]
    cp = np.arange(GRID_W)[None, :]
    cs = np.clip(c - NA_COLS // 2, 0, GRID_W - NA_COLS)
    valid = (cp >= cs) & (cp < cs + NA_COLS)
    d = np.arange(2 * NA_COLS - 1)[:, None, None]
    onehot = ((cp - c + (NA_COLS - 1))[None] == d) & valid[None]
    full = jnp.einsum("hrd,dcm->hrcm", rel_bias.astype(jnp.float32), jnp.asarray(onehot, jnp.float32),
                      precision=lax.Precision.HIGHEST)
    full = jnp.where(jnp.asarray(valid)[None, None], full, NEG)
    variants = []
    for k in range(NA_ROWS):
        win = full[:, NA_ROWS - 1 - k:2 * NA_ROWS - 1 - k]
        variants.append(win.transpose(0, 2, 1, 3).reshape(NA_HEADS * GRID_W, NA_ROWS * GRID_W))
    return jnp.stack(variants)


def _swa_bias_table():
    slopes = (2.0 ** (-8.0 * np.arange(1, SWA_Q_HEADS + 1) / SWA_Q_HEADS)).astype(np.float32)
    qi = np.arange(SWA_BLOCK)[:, None]
    ki = np.arange(3 * SWA_BLOCK)[None, :]
    dist = np.abs(ki - qi - SWA_BLOCK).astype(np.float32)
    tab = np.where(dist <= SWA_WINDOW, -slopes[:, None, None] * dist[None], np.float32(NEG))
    tab = tab.reshape(SWA_Q_HEADS * SWA_BLOCK, 3 * SWA_BLOCK).astype(np.float32)
    first, last = tab.copy(), tab.copy()
    first[:, :SWA_BLOCK] = NEG
    last[:, 2 * SWA_BLOCK:] = NEG
    return jnp.asarray(np.stack([first, tab, last]))


def _block_diag(pool_w):
    n = pool_w.shape[0]
    out = jnp.zeros((n * POOL_GROUP_DIM, n * POOL_GROUP_DIM), pool_w.dtype)
    for gi in range(n):
        sl = slice(gi * POOL_GROUP_DIM, (gi + 1) * POOL_GROUP_DIM)
        out = out.at[sl, sl].set(pool_w[gi])
    return out


def _router_weights(rg_w, rg_b, re_w, re_b):
    w = jnp.zeros((D_MODEL, LANES), jnp.float32)
    w = w.at[:, 0:N_GROUPS].set(rg_w).at[:, ROUTE_LANE0:ROUTE_LANE0 + N_EXPERTS].set(re_w)
    bias = jnp.zeros((1, LANES), jnp.float32)
    bias = bias.at[0, 0:N_GROUPS].set(rg_b).at[0, ROUTE_LANE0:ROUTE_LANE0 + N_EXPERTS].set(re_b)
    return w.astype(jnp.bfloat16), bias


def _dispatch_tables(cnt, n_tiles):
    nb = cnt.shape[0] // SUBLANES
    n = cnt.reshape(nb, SUBLANES, LANES)[:, 0, ROUTE_LANE0:ROUTE_LANE0 + N_EXPERTS].astype(jnp.int32)
    g = (n + (CHUNK - 1)) // CHUNK
    l_end = jnp.cumsum(g, axis=1)
    l_off = l_end - g
    c_end = jnp.cumsum(g, axis=0)
    c_off = c_end - g
    tot = c_end[-1]
    tiles = (tot + (TILE_CHUNKS - 1)) // TILE_CHUNKS
    t_end = jnp.cumsum(tiles)
    t_off = t_end - tiles
    n_used = t_end[-1:]

    experts = jnp.arange(N_EXPERTS, dtype=jnp.int32)
    tile_ids = jnp.arange(n_tiles, dtype=jnp.int32)
    tile_expert = jnp.minimum(jnp.sum((t_end[None, :] <= tile_ids[:, None]).astype(jnp.int32), axis=1),
                              N_EXPERTS - 1)

    q = jnp.arange(n_tiles * TILE_CHUNKS, dtype=jnp.int32)
    tile_q = q // TILE_CHUNKS
    oh_e = (jnp.repeat(tile_expert, TILE_CHUNKS)[:, None] == experts[None, :]).astype(jnp.int32)
    ro = q - jnp.sum(oh_e * t_off[None, :], axis=1) * TILE_CHUNKS
    valid = (ro < jnp.sum(oh_e * tot[None, :], axis=1)) & (tile_q < n_used[0])
    c_end_q = jnp.sum(oh_e[None, :, :] * c_end[:, None, :], axis=2)
    c_off_q = jnp.sum(oh_e[None, :, :] * c_off[:, None, :], axis=2)
    l_off_q = jnp.sum(oh_e[None, :, :] * l_off[:, None, :], axis=2)
    b_q = jnp.minimum(jnp.sum((c_end_q <= ro[None, :]).astype(jnp.int32), axis=0), nb - 1)
    oh_b = (jnp.arange(nb, dtype=jnp.int32)[:, None] == b_q[None, :]).astype(jnp.int32)
    src = b_q * LOCAL_CHUNKS + jnp.sum(oh_b * (l_off_q + ro[None, :] - c_off_q), axis=0)
    chunk_src = jnp.where(valid, src, LOCAL_CHUNKS - 1)

    c = jnp.arange(LOCAL_CHUNKS, dtype=jnp.int32)
    e_c = jnp.minimum(jnp.sum((l_end[:, None, :] <= c[None, :, None]).astype(jnp.int32), axis=2),
                      N_EXPERTS - 1)
    oh_ec = (e_c[:, :, None] == experts[None, None, :]).astype(jnp.int32)
    pos = (jnp.sum(oh_ec * (t_off[None, None, :] * TILE_CHUNKS + c_off[:, None, :] - l_off[:, None, :]), axis=2)
           + c[None, :])
    chunk_tab = jnp.where(c[None, :] < l_end[:, -1:], pos, 0).reshape(-1)
    return tile_expert, n_used, chunk_src, chunk_tab


def kernel(x, norm1_g, w_in, nat_bias, swa_sink, pool_w, pool_scale, w_out, norm2_g, router_g_w,
           router_g_b, router_e_w, router_e_b, expert_w_gate, expert_w_up, expert_w_down, final_g):
    batch, seq_len, _ = x.shape
    depth = w_in.shape[0]
    T = batch * seq_len
    assert seq_len % TOK_TILE == 0 and TOK_TILE % SWA_BLOCK == 0 and TOK_TILE % GRID_W == 0
    max_chunks = (2 * T) // CHUNK + (T // TOK_TILE) * N_EXPERTS
    n_tiles = max_chunks // TILE_CHUNKS + N_EXPERTS

    swb = _swa_bias_table()
    tri = jnp.asarray(np.tril(np.ones((TOK_TILE, TOK_TILE), np.float32), -1)).astype(jnp.bfloat16)
    utri = jnp.asarray(np.triu(np.ones((LANES, LANES), np.float32), 1)).astype(jnp.bfloat16)

    x2 = x.reshape(T, D_MODEL)
    for l in range(depth):
        w_in_l = _in_proj_weight(w_in[l])
        w_out_l = _out_proj_weight(w_out[l])
        nab = _na_bias_table(nat_bias[l])
        sinkcol = jnp.broadcast_to(
            jnp.repeat(swa_sink[l].astype(jnp.float32), SWA_BLOCK)[:, None], (SWA_Q_HEADS * SWA_BLOCK, LANES))
        poolw = _block_diag(pool_w[l]).astype(jnp.bfloat16)
        pools = pool_scale[l].reshape(1, D_C).astype(jnp.float32)
        rw, rb = _router_weights(router_g_w[l], router_g_b[l], router_e_w[l], router_e_b[l])

        qp, kv = _norm_proj(x2, norm1_g[l].reshape(1, D_MODEL), w_in_l)
        xmid, xs_local, gate_local, rinfo, cnt = _mixer(
            x2, qp, kv, nab, swb, sinkcol, poolw, pools, w_out_l, norm2_g[l].reshape(1, D_MODEL),
            rw, rb, tri, utri, batch=batch, seq_len=seq_len)
        tile_expert, n_used, chunk_src, chunk_tab = _dispatch_tables(cnt, n_tiles)
        ys = _experts(l, tile_expert, n_used, chunk_src, xs_local, gate_local,
                      expert_w_gate, expert_w_up, expert_w_down)
        x2 = _combine(chunk_tab, xmid, rinfo, final_g.reshape(1, D_MODEL), ys, final_norm=(l == depth - 1))
    return x2.reshape(batch, seq_len, D_MODEL)
```

```python
import functools

import jax
import jax.numpy as jnp
import numpy as np
from jax import lax
from jax.experimental import pallas as pl
from jax.experimental.pallas import tpu as pltpu

D_MODEL = 1024
GRID_W = 64
HEAD_DIM = 64
NA_HEADS = 4
NA_ROWS = 8
NA_COLS = 16
SWA_Q_HEADS = 8
SWA_KV_HEADS = 2
SWA_REP = SWA_Q_HEADS // SWA_KV_HEADS
SWA_WINDOW = 128
SWA_BLOCK = 128
POOL_WINDOWS = (2, 4, 8, 16)
POOL_GROUP_DIM = 64
D_A = NA_HEADS * HEAD_DIM
D_B = SWA_Q_HEADS * HEAD_DIM
D_BKV = SWA_KV_HEADS * HEAD_DIM
D_C = len(POOL_WINDOWS) * POOL_GROUP_DIM
D_MIX = D_A + D_B + D_C
D_QP = D_A + D_B
D_KV = 2 * D_A + 2 * D_BKV + D_C
D_IN = D_QP + D_KV
N_GROUPS = 4
EXPERTS_PER_GROUP = 8
N_EXPERTS = N_GROUPS * EXPERTS_PER_GROUP
D_EXPERT = 256
RMS_EPS = 1e-6
NEG = -1e30

LANES = 128
SUBLANES = 8
ROW_CHUNKS = D_MODEL // LANES

TOK_TILE = 512
EXP_SUB = 256
EXP_SUBS = 2
EXP_TILE = EXP_SUB * EXP_SUBS
CHUNK = 16
LOCAL_CHUNKS = (2 * TOK_TILE + N_EXPERTS * (CHUNK - 1)) // CHUNK + 2
LOCAL_ROWS = LOCAL_CHUNKS * CHUNK
SUB_CHUNKS = EXP_SUB // CHUNK
TILE_CHUNKS = EXP_TILE // CHUNK
ROUTE_LANE0 = 8
NA_ROWS_PER_STEP = 4
HALO = 8
VMEM_LIMIT = 56 * 1024 * 1024


def _rmsnorm_f32(x, g):
    return x * lax.rsqrt(jnp.mean(x * x, axis=-1, keepdims=True) + RMS_EPS) * g


def _norm_proj_kernel(x_ref, g_ref, w_ref, qp_ref, kv_ref):
    xn = _rmsnorm_f32(x_ref[...], g_ref[...]).astype(jnp.bfloat16)
    proj = jnp.dot(xn, w_ref[...], preferred_element_type=jnp.float32)
    qp_ref[...] = proj[:, :D_QP].astype(jnp.bfloat16)
    kv_ref[...] = proj[:, D_QP:].astype(jnp.bfloat16)


def _norm_proj(x2, g, w):
    T = x2.shape[0]
    return pl.pallas_call(
        _norm_proj_kernel,
        grid=(T // TOK_TILE,),
        in_specs=[
            pl.BlockSpec((TOK_TILE, D_MODEL), lambda i: (i, 0)),
            pl.BlockSpec((1, D_MODEL), lambda i: (0, 0)),
            pl.BlockSpec((D_MODEL, D_IN), lambda i: (0, 0)),
        ],
        out_specs=[
            pl.BlockSpec((TOK_TILE, D_QP), lambda i: (i, 0)),
            pl.BlockSpec((TOK_TILE, D_KV), lambda i: (i, 0)),
        ],
        out_shape=[
            jax.ShapeDtypeStruct((T, D_QP), jnp.bfloat16),
            jax.ShapeDtypeStruct((T, D_KV), jnp.bfloat16),
        ],
        compiler_params=pltpu.CompilerParams(
            dimension_semantics=("arbitrary",), vmem_limit_bytes=VMEM_LIMIT),
        name="norm_proj",
    )(x2, g, w)


KW_AK, KW_AV, KW_BK, KW_BV = 0, D_A, 2 * D_A, 2 * D_A + D_BKV
KW_COLS = 2 * D_A + 2 * D_BKV
KV_CU = KW_COLS


def _mixer_kernel(x_ref, qp_ref, kvp_ref, kvc_ref, kvn_ref, nab_ref, swb_ref, sink_ref,
                  poolw_ref, pools_ref, wout_ref, g2_ref, rw_ref, rb_ref, tri_ref, utri_ref,
                  xmid_ref, xs_ref, gloc_ref, rinfo_ref, cnt_ref,
                  kwin, uwin, mix, *, seq_len):
    b = pl.program_id(0)
    i = pl.program_id(1)
    nblk = pl.num_programs(1)
    rows_per_tile = TOK_TILE // GRID_W
    grid_rows = seq_len // GRID_W

    kwin[0:TOK_TILE, :] = kvp_ref[:, 0:KW_COLS]
    kwin[TOK_TILE:2 * TOK_TILE, :] = kvc_ref[:, 0:KW_COLS]
    kwin[2 * TOK_TILE:3 * TOK_TILE, :] = kvn_ref[:, 0:KW_COLS]

    lane_a = lax.broadcasted_iota(jnp.int32, (GRID_W, D_A), 1) // HEAD_DIM

    def na_row(rr):
        r = i * rows_per_tile + rr
        rs = jnp.clip(r - NA_ROWS // 2, 0, grid_rows - NA_ROWS)
        variant = r - rs
        start = pl.multiple_of((rs - i * rows_per_tile + rows_per_tile) * GRID_W, GRID_W)
        q0 = pl.multiple_of(rr * GRID_W, GRID_W)
        q = qp_ref[pl.ds(q0, GRID_W), 0:D_A] * jnp.bfloat16(HEAD_DIM ** -0.5)
        zero = jnp.zeros_like(q)
        qs = jnp.concatenate([jnp.where(lane_a == h, q, zero) for h in range(NA_HEADS)], axis=0)
        kw = kwin[pl.ds(start, NA_ROWS * GRID_W), KW_AK:KW_AK + D_A]
        vw = kwin[pl.ds(start, NA_ROWS * GRID_W), KW_AV:KW_AV + D_A]
        s = lax.dot_general(qs, kw, (((1,), (1,)), ((), ())), preferred_element_type=jnp.float32)
        s = s + nab_ref[variant]
        m = jnp.max(s, axis=-1, keepdims=True)
        p = jnp.exp(s - m)
        l = jnp.sum(p, axis=-1, keepdims=True)
        pv = jnp.dot(p.astype(jnp.bfloat16), vw, preferred_element_type=jnp.float32)
        pv = pv * (1.0 / l)
        o = jnp.zeros((GRID_W, D_A), jnp.float32)
        for h in range(NA_HEADS):
            o = o + jnp.where(lane_a == h, pv[h * GRID_W:(h + 1) * GRID_W, :], 0.0)
        mix[pl.ds(q0, GRID_W), 0:D_A] = o.astype(jnp.bfloat16)

    def na_step(it, c):
        for k in range(NA_ROWS_PER_STEP):
            na_row(it * NA_ROWS_PER_STEP + k)
        return c

    lax.fori_loop(0, rows_per_tile // NA_ROWS_PER_STEP, na_step, 0)

    lane_b = lax.broadcasted_iota(jnp.int32, (SWA_BLOCK, LANES), 1) // HEAD_DIM
    blocks_per_tile = TOK_TILE // SWA_BLOCK
    nblocks = seq_len // SWA_BLOCK

    ones_v = jnp.ones((3 * SWA_BLOCK, LANES), jnp.bfloat16)

    def swa_step(sb, c):
        n = i * blocks_per_tile + sb
        variant = jnp.where(n == 0, 0, jnp.where(n == nblocks - 1, 2, 1))
        q0 = pl.multiple_of(sb * SWA_BLOCK, SWA_BLOCK)
        k0 = pl.multiple_of(TOK_TILE - SWA_BLOCK + sb * SWA_BLOCK, SWA_BLOCK)
        kw = kwin[pl.ds(k0, 3 * SWA_BLOCK), KW_BK:KW_BK + D_BKV]
        vaug = jnp.concatenate([kwin[pl.ds(k0, 3 * SWA_BLOCK), KW_BV:KW_BV + D_BKV], ones_v], axis=1)
        outs = []
        for g in range(SWA_KV_HEADS):
            pieces = []
            for t in range(SWA_REP):
                qt = qp_ref[pl.ds(q0, SWA_BLOCK), D_A + t * LANES:D_A + (t + 1) * LANES]
                qt = qt * jnp.bfloat16(HEAD_DIM ** -0.5)
                pieces.append(jnp.where(lane_b == g, qt, jnp.zeros_like(qt)))
            qs = jnp.concatenate(pieces, axis=0)
            r0 = g * SWA_REP * SWA_BLOCK
            s = lax.dot_general(qs, kw, (((1,), (1,)), ((), ())), preferred_element_type=jnp.float32)
            s = s + swb_ref[variant, r0:r0 + SWA_REP * SWA_BLOCK, :]
            sink = sink_ref[r0:r0 + SWA_REP * SWA_BLOCK, :]
            m = jnp.broadcast_to(jnp.max(s, axis=-1, keepdims=True), sink.shape)
            m = jnp.maximum(m, sink)
            p = jnp.exp(s - jnp.concatenate([m, m, m], axis=1)).astype(jnp.bfloat16)
            pv = jnp.dot(p, vaug, preferred_element_type=jnp.float32)
            l = pv[:, LANES:2 * LANES] + jnp.exp(sink - m)
            outs.append(pv[:, 0:LANES] * (1.0 / l))
        for t in range(SWA_REP):
            o0 = outs[0][t * SWA_BLOCK:(t + 1) * SWA_BLOCK, :]
            o1 = outs[1][t * SWA_BLOCK:(t + 1) * SWA_BLOCK, :]
            ot = jnp.where(lane_b == 0, o0, o1)
            mix[pl.ds(q0, SWA_BLOCK), D_A + t * LANES:D_A + (t + 1) * LANES] = ot.astype(jnp.bfloat16)
        return c

    lax.fori_loop(0, blocks_per_tile, swa_step, 0)

    u = kvc_ref[:, KV_CU:KV_CU + D_C].astype(jnp.float32)
    prev_ok = (i > 0).astype(jnp.float32)
    next_ok = (i < nblk - 1).astype(jnp.float32)
    uwin[0:HALO, :] = kvp_ref[TOK_TILE - HALO:TOK_TILE, KV_CU:KV_CU + D_C].astype(jnp.float32) * prev_ok
    uwin[HALO:HALO + TOK_TILE, :] = u
    uwin[HALO + TOK_TILE:2 * HALO + TOK_TILE, :] = kvn_ref[0:HALO, KV_CU:KV_CU + D_C].astype(jnp.float32) * next_ok
    n_ext = TOK_TILE + 2 * HALO
    a2 = uwin[0:n_ext - 1, :] + uwin[1:n_ext, :]
    a4 = a2[0:n_ext - 3, :] + a2[2:n_ext - 1, :]
    a8 = a4[0:n_ext - 7, :] + a4[4:n_ext - 3, :]
    a16 = a8[0:n_ext - 15, :] + a8[8:n_ext - 7, :]
    w2 = a2[7:7 + TOK_TILE, :]
    w4 = a4[6:6 + TOK_TILE, :]
    w8 = a8[4:4 + TOK_TILE, :]
    w16 = a16[0:TOK_TILE, :]
    lane_c = lax.broadcasted_iota(jnp.int32, (TOK_TILE, D_C), 1) // POOL_GROUP_DIM
    pooled = jnp.where(lane_c == 0, w2, jnp.where(lane_c == 1, w4, jnp.where(lane_c == 2, w8, w16)))
    half = jnp.where(lane_c == 0, 1, jnp.where(lane_c == 1, 2, jnp.where(lane_c == 2, 4, 8)))
    pos = i * TOK_TILE + lax.broadcasted_iota(jnp.int32, (TOK_TILE, D_C), 0)
    cnt = (jnp.minimum(pos + half, seq_len) - jnp.maximum(pos - half, 0)).astype(jnp.float32)
    d = (pooled / cnt - u).astype(jnp.bfloat16)
    oc = jnp.dot(d, poolw_ref[...], preferred_element_type=jnp.float32) * pools_ref[...]
    mix[:, D_A + D_B:D_MIX] = oc.astype(jnp.bfloat16)

    xm = x_ref[...] + jnp.dot(mix[...], wout_ref[...], preferred_element_type=jnp.float32)
    xmid_ref[...] = xm

    xn = _rmsnorm_f32(xm, g2_ref[...]).astype(jnp.bfloat16)
    logits = jnp.dot(xn, rw_ref[...], preferred_element_type=jnp.float32) + rb_ref[...]
    lane = lax.broadcasted_iota(jnp.int32, (TOK_TILE, LANES), 1).astype(jnp.float32)
    is_g = lane < N_GROUPS
    gl = jnp.where(is_g, logits, NEG)
    gmax = jnp.max(gl, axis=-1, keepdims=True)
    gtop = jnp.min(jnp.where(is_g & (gl == gmax), lane, float(LANES)), axis=-1, keepdims=True)
    gprob = 1.0 / jnp.sum(jnp.exp(gl - gmax), axis=-1, keepdims=True)
    e_lo = ROUTE_LANE0 + gtop * EXPERTS_PER_GROUP
    in_grp = (lane >= e_lo) & (lane < e_lo + EXPERTS_PER_GROUP)
    el = jnp.where(in_grp, logits, NEG)
    m1 = jnp.max(el, axis=-1, keepdims=True)
    i1 = jnp.min(jnp.where(in_grp & (el == m1), lane, float(LANES)), axis=-1, keepdims=True)
    el2 = jnp.where(lane == i1, NEG, el)
    m2 = jnp.max(el2, axis=-1, keepdims=True)
    i2 = jnp.min(jnp.where(in_grp & (lane != i1) & (el2 == m2), lane, float(LANES)), axis=-1, keepdims=True)
    r21 = jnp.exp(m2 - m1)
    gate1 = gprob / (1.0 + r21)
    gate2 = gprob * r21 / (1.0 + r21)

    oh1 = lane == i1
    oh2 = lane == i2
    oh = jnp.where(oh1 | oh2, 1.0, 0.0)
    earlier = jnp.dot(tri_ref[...], oh.astype(jnp.bfloat16), preferred_element_type=jnp.float32)
    n_e = jnp.sum(oh, axis=0, keepdims=True)
    chunks_e = jnp.floor((n_e + (CHUNK - 1)) * (1.0 / CHUNK))
    seg0 = jnp.dot(jnp.broadcast_to(chunks_e, (SUBLANES, LANES)).astype(jnp.bfloat16), utri_ref[...],
                   preferred_element_type=jnp.float32)[0:1, :] * CHUNK
    base = earlier + seg0
    lp1 = jnp.sum(jnp.where(oh1, base, 0.0), axis=-1, keepdims=True)
    lp2 = jnp.sum(jnp.where(oh2, base, 0.0), axis=-1, keepdims=True)
    info = jnp.where(lane == 0, lp1, jnp.where(lane == 1, lp2, 0.0))
    rinfo_ref[...] = info
    cnt_ref[...] = jnp.broadcast_to(n_e, cnt_ref.shape)

    def pieces(g):
        hi = g.astype(jnp.bfloat16).astype(jnp.float32)
        mid = (g - hi).astype(jnp.bfloat16).astype(jnp.float32)
        return hi, mid, g - hi - mid

    aux = jnp.zeros((TOK_TILE, LANES), jnp.float32)
    for k, piece in enumerate(pieces(gate1) + pieces(gate2) + (i1,)):
        aux = jnp.where(lane == k, piece, aux)

    info_t = info.T
    prow = lax.broadcasted_iota(jnp.int32, (LOCAL_ROWS, TOK_TILE), 0).astype(jnp.float32)
    sel = jnp.where((prow == info_t[0:1, :]) | (prow == info_t[1:2, :]), 1.0, 0.0).astype(jnp.bfloat16)
    moved = jnp.dot(sel, jnp.concatenate([xn, aux.astype(jnp.bfloat16)], axis=1),
                    preferred_element_type=jnp.float32)
    xs_ref[...] = moved[:, 0:D_MODEL].astype(jnp.bfloat16)
    gloc_ref[...] = moved[:, D_MODEL:]


def _mixer(x2, qp, kv, nab, swb, sinkcol, poolw, pools, wout, g2, rw, rb, tri, utri, *, batch, seq_len):
    T = x2.shape[0]
    nblk = seq_len // TOK_TILE
    n_tok_tiles = T // TOK_TILE

    def cur(b, i):
        return (b * nblk + i, 0)

    def prev(b, i):
        return (b * nblk + jnp.maximum(i - 1, 0), 0)

    def nxt(b, i):
        return (b * nblk + jnp.minimum(i + 1, nblk - 1), 0)

    def resident(a):
        zeros = (0,) * a.ndim
        return pl.BlockSpec(a.shape, lambda b, i: zeros, pipeline_mode=pl.Buffered(1))

    return pl.pallas_call(
        functools.partial(_mixer_kernel, seq_len=seq_len),
        grid=(batch, nblk),
        in_specs=[
            pl.BlockSpec((TOK_TILE, D_MODEL), cur),
            pl.BlockSpec((TOK_TILE, D_QP), cur),
            pl.BlockSpec((TOK_TILE, D_KV), prev),
            pl.BlockSpec((TOK_TILE, D_KV), cur),
            pl.BlockSpec((TOK_TILE, D_KV), nxt),
            resident(nab), resident(swb), resident(sinkcol), resident(poolw), resident(pools),
            resident(wout), resident(g2), resident(rw), resident(rb), resident(tri), resident(utri),
        ],
        out_specs=[
            pl.BlockSpec((TOK_TILE, D_MODEL), cur),
            pl.BlockSpec((LOCAL_ROWS, D_MODEL), cur),
            pl.BlockSpec((LOCAL_ROWS, LANES), cur),
            pl.BlockSpec((TOK_TILE, LANES), cur),
            pl.BlockSpec((SUBLANES, LANES), cur),
        ],
        out_shape=[
            jax.ShapeDtypeStruct((T, D_MODEL), jnp.float32),
            jax.ShapeDtypeStruct((n_tok_tiles * LOCAL_ROWS, D_MODEL), jnp.bfloat16),
            jax.ShapeDtypeStruct((n_tok_tiles * LOCAL_ROWS, LANES), jnp.float32),
            jax.ShapeDtypeStruct((T, LANES), jnp.float32),
            jax.ShapeDtypeStruct((n_tok_tiles * SUBLANES, LANES), jnp.float32),
        ],
        scratch_shapes=[
            pltpu.VMEM((3 * TOK_TILE, KW_COLS), jnp.bfloat16),
            pltpu.VMEM((TOK_TILE + 2 * HALO, D_C), jnp.float32),
            pltpu.VMEM((TOK_TILE, D_MIX), jnp.bfloat16),
        ],
        compiler_params=pltpu.CompilerParams(
            dimension_semantics=("arbitrary", "arbitrary"), vmem_limit_bytes=VMEM_LIMIT),
        name="mixer",
    )(x2, qp, kv, kv, kv, nab, swb, sinkcol, poolw, pools, wout, g2, rw, rb, tri, utri)


def _chunk_copy(src_hbm, src_chunk, dst, dst_chunk, sem):
    return pltpu.make_async_copy(
        src_hbm.at[pl.ds(pl.multiple_of(src_chunk * CHUNK, CHUNK), CHUNK)],
        dst.at[pl.ds(dst_chunk * CHUNK, CHUNK)],
        sem)


def _expert_kernel(te_ref, nsub_ref, csrc_ref, xs_hbm, gl_hbm, wg_ref, wu_ref, wd_ref, y_ref,
                   xbuf, gbuf, wgb, wub, wdb, sem):
    j = pl.program_id(0)
    nt = pl.num_programs(0)
    slot = j % 2

    def start_gather(tile, s, h):
        for c in range(SUB_CHUNKS):
            src = csrc_ref[tile * TILE_CHUNKS + h * SUB_CHUNKS + c]
            _chunk_copy(xs_hbm, src, xbuf.at[s, h], c, sem.at[0, s, h]).start()
            _chunk_copy(gl_hbm, src, gbuf.at[s, h], c, sem.at[1, s, h]).start()

    for h in range(EXP_SUBS):
        @pl.when((j == 0) & (h < nsub_ref[0]))
        def _():
            start_gather(0, 0, h)

        @pl.when(h < nsub_ref[jnp.minimum(j + 1, nt - 1)] * (j + 1 < nt).astype(jnp.int32))
        def _():
            start_gather(j + 1, 1 - slot, h)

    @pl.when(nsub_ref[j] > 0)
    def _():
        wgb[...] = wg_ref[0, 0].astype(jnp.bfloat16)
        wub[...] = wu_ref[0, 0].astype(jnp.bfloat16)
        wdb[...] = wd_ref[0, 0].astype(jnp.bfloat16)

    for h in range(EXP_SUBS):
        rows = pl.ds(h * EXP_SUB, EXP_SUB)

        @pl.when(h < nsub_ref[j])
        def _():
            pltpu.make_async_copy(xs_hbm.at[pl.ds(0, EXP_SUB)], xbuf.at[slot, h], sem.at[0, slot, h]).wait()
            pltpu.make_async_copy(gl_hbm.at[pl.ds(0, EXP_SUB)], gbuf.at[slot, h], sem.at[1, slot, h]).wait()
            xs = xbuf[slot, h]
            gp = gbuf[slot, h]
            first = gp[:, 6:7] == (te_ref[j] + ROUTE_LANE0).astype(jnp.float32)
            gate_w = jnp.where(first, gp[:, 0:1] + gp[:, 1:2] + gp[:, 2:3], gp[:, 3:4] + gp[:, 4:5] + gp[:, 5:6])
            gate = jnp.dot(xs, wgb[...], preferred_element_type=jnp.float32)
            up = jnp.dot(xs, wub[...], preferred_element_type=jnp.float32)
            act = (gate * (1.0 / (1.0 + jnp.exp(-gate))) * up * gate_w).astype(jnp.bfloat16)
            y = jnp.dot(act, wdb[...], preferred_element_type=jnp.float32)
            y_ref[rows, :] = y.astype(jnp.bfloat16)

        @pl.when(h >= nsub_ref[j])
        def _():
            y_ref[rows, :] = jnp.zeros((EXP_SUB, D_MODEL), jnp.bfloat16)


def _experts(layer, tile_expert, n_sub, chunk_src, xs_local, gate_local, wg, wu, wd):
    n_tiles = tile_expert.shape[0]

    def w_map(j, te, nu, cs):
        return (layer, te[j], 0, 0)

    return pl.pallas_call(
        _expert_kernel,
        grid_spec=pltpu.PrefetchScalarGridSpec(
            num_scalar_prefetch=3,
            grid=(n_tiles,),
            in_specs=[
                pl.BlockSpec(memory_space=pl.ANY),
                pl.BlockSpec(memory_space=pl.ANY),
                pl.BlockSpec((1, 1, D_MODEL, D_EXPERT), w_map),
                pl.BlockSpec((1, 1, D_MODEL, D_EXPERT), w_map),
                pl.BlockSpec((1, 1, D_EXPERT, D_MODEL), w_map),
            ],
            out_specs=pl.BlockSpec((EXP_TILE, D_MODEL), lambda j, te, nu, cs: (j, 0)),
            scratch_shapes=[
                pltpu.VMEM((2, EXP_SUBS, EXP_SUB, D_MODEL), jnp.bfloat16),
                pltpu.VMEM((2, EXP_SUBS, EXP_SUB, LANES), jnp.float32),
                pltpu.VMEM((D_MODEL, D_EXPERT), jnp.bfloat16),
                pltpu.VMEM((D_MODEL, D_EXPERT), jnp.bfloat16),
                pltpu.VMEM((D_EXPERT, D_MODEL), jnp.bfloat16),
                pltpu.SemaphoreType.DMA((2, 2, EXP_SUBS)),
            ],
        ),
        out_shape=jax.ShapeDtypeStruct((n_tiles * EXP_TILE, D_MODEL), jnp.bfloat16),
        compiler_params=pltpu.CompilerParams(
            dimension_semantics=("arbitrary",), vmem_limit_bytes=VMEM_LIMIT),
        name="experts",
    )(tile_expert, n_sub, chunk_src, xs_local, gate_local, wg, wu, wd)


def _combine_kernel(ctab_ref, xmid_ref, rinfo_ref, g_ref, ys_hbm, out_ref, ybuf, sem, *, final_norm):
    i = pl.program_id(0)
    nt = pl.num_programs(0)
    slot = i % 2

    def start_gather(tile, s):
        for c in range(LOCAL_CHUNKS):
            _chunk_copy(ys_hbm, ctab_ref[tile * LOCAL_CHUNKS + c], ybuf.at[s], c, sem.at[s]).start()

    @pl.when(i == 0)
    def _():
        start_gather(0, 0)

    @pl.when(i + 1 < nt)
    def _():
        start_gather(i + 1, 1 - slot)

    pltpu.make_async_copy(ys_hbm.at[pl.ds(0, LOCAL_ROWS)], ybuf.at[slot], sem.at[slot]).wait()

    info = rinfo_ref[...]
    pcol = lax.broadcasted_iota(jnp.int32, (TOK_TILE, LOCAL_ROWS), 1).astype(jnp.float32)
    pick = jnp.where((pcol == info[:, 0:1]) | (pcol == info[:, 1:2]), 1.0, 0.0).astype(jnp.bfloat16)
    out = xmid_ref[...] + jnp.dot(pick, ybuf[slot], preferred_element_type=jnp.float32)
    if final_norm:
        out = _rmsnorm_f32(out, g_ref[...])
    out_ref[...] = out


def _combine(chunk_tab, xmid, rinfo, g, ys, *, final_norm):
    T = xmid.shape[0]
    return pl.pallas_call(
        functools.partial(_combine_kernel, final_norm=final_norm),
        grid_spec=pltpu.PrefetchScalarGridSpec(
            num_scalar_prefetch=1,
            grid=(T // TOK_TILE,),
            in_specs=[
                pl.BlockSpec((TOK_TILE, D_MODEL), lambda i, ct: (i, 0)),
                pl.BlockSpec((TOK_TILE, LANES), lambda i, ct: (i, 0)),
                pl.BlockSpec((1, D_MODEL), lambda i, ct: (0, 0)),
                pl.BlockSpec(memory_space=pl.ANY),
            ],
            out_specs=pl.BlockSpec((TOK_TILE, D_MODEL), lambda i, ct: (i, 0)),
            scratch_shapes=[
                pltpu.VMEM((2, LOCAL_ROWS, D_MODEL), jnp.bfloat16),
                pltpu.SemaphoreType.DMA((2,)),
            ],
        ),
        out_shape=jax.ShapeDtypeStruct((T, D_MODEL), jnp.float32),
        compiler_params=pltpu.CompilerParams(
            dimension_semantics=("arbitrary",), vmem_limit_bytes=VMEM_LIMIT),
        name="combine",
    )(chunk_tab, xmid, rinfo, g, ys)


def _pair_heads(a, axis):
    shape = a.shape
    split = shape[:axis] + (SWA_KV_HEADS, SWA_REP, HEAD_DIM) + shape[axis + 1:]
    return jnp.swapaxes(a.reshape(split), axis, axis + 1).reshape(shape)


def _in_proj_weight(w):
    off_bq = 3 * D_A
    return jnp.concatenate(
        [w[:, 0:D_A], _pair_heads(w[:, off_bq:off_bq + D_B], 1), w[:, D_A:off_bq], w[:, off_bq + D_B:]],
        axis=1).astype(jnp.bfloat16)


def _out_proj_weight(w):
    return jnp.concatenate(
        [w[0:D_A], _pair_heads(w[D_A:D_A + D_B], 0), w[D_A + D_B:]], axis=0).astype(jnp.bfloat16)


def _na_bias_table(rel_bias):
    c = np.arange(GRID_W)[:, None]
    cp = np.arange(GRID_W)[None, :]
    cs = np.clip(c - NA_COLS // 2, 0, GRID_W - NA_COLS)
    valid = (cp >= cs) & (cp < cs + NA_COLS)
    d = np.arange(2 * NA_COLS - 1)[:, None, None]
    onehot = ((cp - c + (NA_COLS - 1))[None] == d) & valid[None]
    full = jnp.einsum("hrd,dcm->hrcm", rel_bias.astype(jnp.float32), jnp.asarray(onehot, jnp.float32),
                      precision=lax.Precision.HIGHEST)
    full = jnp.where(jnp.asarray(valid)[None, None], full, NEG)
    variants = []
    for k in range(NA_ROWS):
        win = full[:, NA_ROWS - 1 - k:2 * NA_ROWS - 1 - k]
        variants.append(win.transpose(0, 2, 1, 3).reshape(NA_HEADS * GRID_W, NA_ROWS * GRID_W))
    return jnp.stack(variants)


def _swa_bias_table():
    slopes = (2.0 ** (-8.0 * np.arange(1, SWA_Q_HEADS + 1) / SWA_Q_HEADS)).astype(np.float32)
    qi = np.arange(SWA_BLOCK)[:, None]
    ki = np.arange(3 * SWA_BLOCK)[None, :]
    dist = np.abs(ki - qi - SWA_BLOCK).astype(np.float32)
    tab = np.where(dist <= SWA_WINDOW, -slopes[:, None, None] * dist[None], np.float32(NEG))
    tab = tab.reshape(SWA_Q_HEADS * SWA_BLOCK, 3 * SWA_BLOCK).astype(np.float32)
    first, last = tab.copy(), tab.copy()
    first[:, :SWA_BLOCK] = NEG
    last[:, 2 * SWA_BLOCK:] = NEG
    return jnp.asarray(np.stack([first, tab, last]))


def _block_diag(pool_w):
    n = pool_w.shape[0]
    out = jnp.zeros((n * POOL_GROUP_DIM, n * POOL_GROUP_DIM), pool_w.dtype)
    for gi in range(n):
        sl = slice(gi * POOL_GROUP_DIM, (gi + 1) * POOL_GROUP_DIM)
        out = out.at[sl, sl].set(pool_w[gi])
    return out


def _router_weights(rg_w, rg_b, re_w, re_b):
    w = jnp.zeros((D_MODEL, LANES), jnp.float32)
    w = w.at[:, 0:N_GROUPS].set(rg_w).at[:, ROUTE_LANE0:ROUTE_LANE0 + N_EXPERTS].set(re_w)
    bias = jnp.zeros((1, LANES), jnp.float32)
    bias = bias.at[0, 0:N_GROUPS].set(rg_b).at[0, ROUTE_LANE0:ROUTE_LANE0 + N_EXPERTS].set(re_b)
    return w.astype(jnp.bfloat16), bias


def _dispatch_tables(cnt, n_tiles):
    nb = cnt.shape[0] // SUBLANES
    n = cnt.reshape(nb, SUBLANES, LANES)[:, 0, ROUTE_LANE0:ROUTE_LANE0 + N_EXPERTS].astype(jnp.int32)
    g = (n + (CHUNK - 1)) // CHUNK
    l_end = jnp.cumsum(g, axis=1)
    l_off = l_end - g
    c_end = jnp.cumsum(g, axis=0)
    c_off = c_end - g
    tot = c_end[-1]
    tiles = (tot + (TILE_CHUNKS - 1)) // TILE_CHUNKS
    t_end = jnp.cumsum(tiles)
    t_off = t_end - tiles
    n_used = t_end[-1:]

    experts = jnp.arange(N_EXPERTS, dtype=jnp.int32)
    tile_ids = jnp.arange(n_tiles, dtype=jnp.int32)
    tile_expert = jnp.minimum(jnp.sum((t_end[None, :] <= tile_ids[:, None]).astype(jnp.int32), axis=1),
                              N_EXPERTS - 1)
    oh_te = (tile_expert[:, None] == experts[None, :]).astype(jnp.int32)
    left = jnp.sum(oh_te * (tot + t_off * TILE_CHUNKS)[None, :], axis=1) - tile_ids * TILE_CHUNKS
    n_sub = jnp.clip((left + (SUB_CHUNKS - 1)) // SUB_CHUNKS, 0, EXP_SUBS)

    q = jnp.arange(n_tiles * TILE_CHUNKS, dtype=jnp.int32)
    tile_q = q // TILE_CHUNKS
    oh_e = (jnp.repeat(tile_expert, TILE_CHUNKS)[:, None] == experts[None, :]).astype(jnp.int32)
    ro = q - jnp.sum(oh_e * t_off[None, :], axis=1) * TILE_CHUNKS
    valid = (ro < jnp.sum(oh_e * tot[None, :], axis=1)) & (tile_q < n_used[0])
    c_end_q = jnp.sum(oh_e[None, :, :] * c_end[:, None, :], axis=2)
    c_off_q = jnp.sum(oh_e[None, :, :] * c_off[:, None, :], axis=2)
    l_off_q = jnp.sum(oh_e[None, :, :] * l_off[:, None, :], axis=2)
    b_q = jnp.minimum(jnp.sum((c_end_q <= ro[None, :]).astype(jnp.int32), axis=0), nb - 1)
    oh_b = (jnp.arange(nb, dtype=jnp.int32)[:, None] == b_q[None, :]).astype(jnp.int32)
    src = b_q * LOCAL_CHUNKS + jnp.sum(oh_b * (l_off_q + ro[None, :] - c_off_q), axis=0)
    chunk_src = jnp.where(valid, src, LOCAL_CHUNKS - 1)

    c = jnp.arange(LOCAL_CHUNKS, dtype=jnp.int32)
    e_c = jnp.minimum(jnp.sum((l_end[:, None, :] <= c[None, :, None]).astype(jnp.int32), axis=2),
                      N_EXPERTS - 1)
    oh_ec = (e_c[:, :, None] == experts[None, None, :]).astype(jnp.int32)
    pos = (jnp.sum(oh_ec * (t_off[None, None, :] * TILE_CHUNKS + c_off[:, None, :] - l_off[:, None, :]), axis=2)
           + c[None, :])
    chunk_tab = jnp.where(c[None, :] < l_end[:, -1:], pos, 0).reshape(-1)
    return tile_expert, n_sub, chunk_src, chunk_tab


def kernel(x, norm1_g, w_in, nat_bias, swa_sink, pool_w, pool_scale, w_out, norm2_g, router_g_w,
           router_g_b, router_e_w, router_e_b, expert_w_gate, expert_w_up, expert_w_down, final_g):
    batch, seq_len, _ = x.shape
    depth = w_in.shape[0]
    T = batch * seq_len
    assert seq_len % TOK_TILE == 0 and TOK_TILE % SWA_BLOCK == 0 and TOK_TILE % GRID_W == 0
    max_chunks = (2 * T) // CHUNK + (T // TOK_TILE) * N_EXPERTS
    n_tiles = max_chunks // TILE_CHUNKS + N_EXPERTS

    swb = _swa_bias_table()
    tri = jnp.asarray(np.tril(np.ones((TOK_TILE, TOK_TILE), np.float32), -1)).astype(jnp.bfloat16)
    utri = jnp.asarray(np.triu(np.ones((LANES, LANES), np.float32), 1)).astype(jnp.bfloat16)

    x2 = x.reshape(T, D_MODEL)
    for l in range(depth):
        w_in_l = _in_proj_weight(w_in[l])
        w_out_l = _out_proj_weight(w_out[l])
        nab = _na_bias_table(nat_bias[l])
        sinkcol = jnp.broadcast_to(
            jnp.repeat(swa_sink[l].astype(jnp.float32), SWA_BLOCK)[:, None], (SWA_Q_HEADS * SWA_BLOCK, LANES))
        poolw = _block_diag(pool_w[l]).astype(jnp.bfloat16)
        pools = pool_scale[l].reshape(1, D_C).astype(jnp.float32)
        rw, rb = _router_weights(router_g_w[l], router_g_b[l], router_e_w[l], router_e_b[l])

        qp, kv = _norm_proj(x2, norm1_g[l].reshape(1, D_MODEL), w_in_l)
        xmid, xs_local, gate_local, rinfo, cnt = _mixer(
            x2, qp, kv, nab, swb, sinkcol, poolw, pools, w_out_l, norm2_g[l].reshape(1, D_MODEL),
            rw, rb, tri, utri, batch=batch, seq_len=seq_len)
        tile_expert, n_sub, chunk_src, chunk_tab = _dispatch_tables(cnt, n_tiles)
        ys = _experts(l, tile_expert, n_sub, chunk_src, xs_local, gate_local,
                      expert_w_gate, expert_w_up, expert_w_down)
        x2 = _combine(chunk_tab, xmid, rinfo, final_g.reshape(1, D_MODEL), ys, final_norm=(l == depth - 1))
    return x2.reshape(batch, seq_len, D_MODEL)
```

```python
import functools

import jax
import jax.numpy as jnp
import numpy as np
from jax import lax
from jax.experimental import pallas as pl
from jax.experimental.pallas import tpu as pltpu

D_MODEL = 1024
GRID_W = 64
HEAD_DIM = 64
NA_HEADS = 4
NA_ROWS = 8
NA_COLS = 16
SWA_Q_HEADS = 8
SWA_KV_HEADS = 2
SWA_REP = SWA_Q_HEADS // SWA_KV_HEADS
SWA_WINDOW = 128
SWA_BLOCK = 128
POOL_WINDOWS = (2, 4, 8, 16)
POOL_GROUP_DIM = 64
D_A = NA_HEADS * HEAD_DIM
D_B = SWA_Q_HEADS * HEAD_DIM
D_BKV = SWA_KV_HEADS * HEAD_DIM
D_C = len(POOL_WINDOWS) * POOL_GROUP_DIM
D_MIX = D_A + D_B + D_C
D_QP = D_A + D_B
D_KV = 2 * D_A + 2 * D_BKV + D_C
D_IN = D_QP + D_KV
N_GROUPS = 4
EXPERTS_PER_GROUP = 8
N_EXPERTS = N_GROUPS * EXPERTS_PER_GROUP
D_EXPERT = 256
RMS_EPS = 1e-6
NEG = -1e30

LANES = 128
SUBLANES = 8
ROW_CHUNKS = D_MODEL // LANES

TOK_TILE = 512
EXP_SUB = 256
EXP_SUBS = 2
EXP_TILE = EXP_SUB * EXP_SUBS
CHUNK = 16
LOCAL_CHUNKS = (2 * TOK_TILE + N_EXPERTS * (CHUNK - 1)) // CHUNK + 2
LOCAL_ROWS = LOCAL_CHUNKS * CHUNK
SUB_CHUNKS = EXP_SUB // CHUNK
TILE_CHUNKS = EXP_TILE // CHUNK
ROUTE_LANE0 = 8
NA_ROWS_PER_STEP = 4
HALO = 8
VMEM_LIMIT = 56 * 1024 * 1024


def _rmsnorm_f32(x, g):
    return x * lax.rsqrt(jnp.mean(x * x, axis=-1, keepdims=True) + RMS_EPS) * g


def _norm_proj_kernel(x_ref, g_ref, w_ref, qp_ref, kv_ref):
    xn = _rmsnorm_f32(x_ref[...], g_ref[...]).astype(jnp.bfloat16)
    proj = jnp.dot(xn, w_ref[...], preferred_element_type=jnp.float32)
    qp_ref[...] = proj[:, :D_QP].astype(jnp.bfloat16)
    kv_ref[...] = proj[:, D_QP:].astype(jnp.bfloat16)


def _norm_proj(x2, g, w):
    T = x2.shape[0]
    return pl.pallas_call(
        _norm_proj_kernel,
        grid=(T // TOK_TILE,),
        in_specs=[
            pl.BlockSpec((TOK_TILE, D_MODEL), lambda i: (i, 0)),
            pl.BlockSpec((1, D_MODEL), lambda i: (0, 0)),
            pl.BlockSpec((D_MODEL, D_IN), lambda i: (0, 0)),
        ],
        out_specs=[
            pl.BlockSpec((TOK_TILE, D_QP), lambda i: (i, 0)),
            pl.BlockSpec((TOK_TILE, D_KV), lambda i: (i, 0)),
        ],
        out_shape=[
            jax.ShapeDtypeStruct((T, D_QP), jnp.bfloat16),
            jax.ShapeDtypeStruct((T, D_KV), jnp.bfloat16),
        ],
        compiler_params=pltpu.CompilerParams(
            dimension_semantics=("arbitrary",), vmem_limit_bytes=VMEM_LIMIT),
        name="norm_proj",
    )(x2, g, w)


KW_AK, KW_AV, KW_BK, KW_BV = 0, D_A, 2 * D_A, 2 * D_A + D_BKV
KW_COLS = 2 * D_A + 2 * D_BKV
KV_CU = KW_COLS


def _mixer_kernel(x_ref, qp_ref, kvp_ref, kvc_ref, kvn_ref, nab_ref, swb_ref, sink_ref,
                  poolw_ref, pools_ref, wout_ref, g2_ref, rw_ref, rb_ref, tri_ref, utri_ref,
                  xmid_ref, xs_ref, gloc_ref, rinfo_ref, cnt_ref,
                  kwin, uwin, mix, *, seq_len):
    b = pl.program_id(0)
    i = pl.program_id(1)
    nblk = pl.num_programs(1)
    rows_per_tile = TOK_TILE // GRID_W
    grid_rows = seq_len // GRID_W

    kwin[0:TOK_TILE, :] = kvp_ref[:, 0:KW_COLS]
    kwin[TOK_TILE:2 * TOK_TILE, :] = kvc_ref[:, 0:KW_COLS]
    kwin[2 * TOK_TILE:3 * TOK_TILE, :] = kvn_ref[:, 0:KW_COLS]

    lane_a = lax.broadcasted_iota(jnp.int32, (GRID_W, D_A), 1) // HEAD_DIM

    def na_row(rr):
        r = i * rows_per_tile + rr
        rs = jnp.clip(r - NA_ROWS // 2, 0, grid_rows - NA_ROWS)
        variant = r - rs
        start = pl.multiple_of((rs - i * rows_per_tile + rows_per_tile) * GRID_W, GRID_W)
        q0 = pl.multiple_of(rr * GRID_W, GRID_W)
        q = qp_ref[pl.ds(q0, GRID_W), 0:D_A] * jnp.bfloat16(HEAD_DIM ** -0.5)
        zero = jnp.zeros_like(q)
        qs = jnp.concatenate([jnp.where(lane_a == h, q, zero) for h in range(NA_HEADS)], axis=0)
        kw = kwin[pl.ds(start, NA_ROWS * GRID_W), KW_AK:KW_AK + D_A]
        vw = kwin[pl.ds(start, NA_ROWS * GRID_W), KW_AV:KW_AV + D_A]
        s = lax.dot_general(qs, kw, (((1,), (1,)), ((), ())), preferred_element_type=jnp.float32)
        s = s + nab_ref[variant]
        m = jnp.max(s, axis=-1, keepdims=True)
        p = jnp.exp(s - m)
        l = jnp.sum(p, axis=-1, keepdims=True)
        pv = jnp.dot(p.astype(jnp.bfloat16), vw, preferred_element_type=jnp.float32)
        pv = pv * (1.0 / l)
        o = jnp.zeros((GRID_W, D_A), jnp.float32)
        for h in range(NA_HEADS):
            o = o + jnp.where(lane_a == h, pv[h * GRID_W:(h + 1) * GRID_W, :], 0.0)
        mix[pl.ds(q0, GRID_W), 0:D_A] = o.astype(jnp.bfloat16)

    def na_step(it, c):
        for k in range(NA_ROWS_PER_STEP):
            na_row(it * NA_ROWS_PER_STEP + k)
        return c

    lax.fori_loop(0, rows_per_tile // NA_ROWS_PER_STEP, na_step, 0)

    lane_b = lax.broadcasted_iota(jnp.int32, (SWA_BLOCK, LANES), 1) // HEAD_DIM
    blocks_per_tile = TOK_TILE // SWA_BLOCK
    nblocks = seq_len // SWA_BLOCK

    ones_v = jnp.ones((3 * SWA_BLOCK, LANES), jnp.bfloat16)

    def swa_step(sb, c):
        n = i * blocks_per_tile + sb
        variant = jnp.where(n == 0, 0, jnp.where(n == nblocks - 1, 2, 1))
        q0 = pl.multiple_of(sb * SWA_BLOCK, SWA_BLOCK)
        k0 = pl.multiple_of(TOK_TILE - SWA_BLOCK + sb * SWA_BLOCK, SWA_BLOCK)
        kw = kwin[pl.ds(k0, 3 * SWA_BLOCK), KW_BK:KW_BK + D_BKV]
        vaug = jnp.concatenate([kwin[pl.ds(k0, 3 * SWA_BLOCK), KW_BV:KW_BV + D_BKV], ones_v], axis=1)
        outs = []
        for g in range(SWA_KV_HEADS):
            pieces = []
            for t in range(SWA_REP):
                qt = qp_ref[pl.ds(q0, SWA_BLOCK), D_A + t * LANES:D_A + (t + 1) * LANES]
                qt = qt * jnp.bfloat16(HEAD_DIM ** -0.5)
                pieces.append(jnp.where(lane_b == g, qt, jnp.zeros_like(qt)))
            qs = jnp.concatenate(pieces, axis=0)
            r0 = g * SWA_REP * SWA_BLOCK
            s = lax.dot_general(qs, kw, (((1,), (1,)), ((), ())), preferred_element_type=jnp.float32)
            s = s + swb_ref[variant, r0:r0 + SWA_REP * SWA_BLOCK, :]
            sink = sink_ref[r0:r0 + SWA_REP * SWA_BLOCK, :]
            m = jnp.broadcast_to(jnp.max(s, axis=-1, keepdims=True), sink.shape)
            m = jnp.maximum(m, sink)
            p = jnp.exp(s - jnp.concatenate([m, m, m], axis=1)).astype(jnp.bfloat16)
            pv = jnp.dot(p, vaug, preferred_element_type=jnp.float32)
            l = pv[:, LANES:2 * LANES] + jnp.exp(sink - m)
            outs.append(pv[:, 0:LANES] * (1.0 / l))
        for t in range(SWA_REP):
            o0 = outs[0][t * SWA_BLOCK:(t + 1) * SWA_BLOCK, :]
            o1 = outs[1][t * SWA_BLOCK:(t + 1) * SWA_BLOCK, :]
            ot = jnp.where(lane_b == 0, o0, o1)
            mix[pl.ds(q0, SWA_BLOCK), D_A + t * LANES:D_A + (t + 1) * LANES] = ot.astype(jnp.bfloat16)
        return c

    lax.fori_loop(0, blocks_per_tile, swa_step, 0)

    u = kvc_ref[:, KV_CU:KV_CU + D_C].astype(jnp.float32)
    prev_ok = (i > 0).astype(jnp.float32)
    next_ok = (i < nblk - 1).astype(jnp.float32)
    uwin[0:HALO, :] = kvp_ref[TOK_TILE - HALO:TOK_TILE, KV_CU:KV_CU + D_C].astype(jnp.float32) * prev_ok
    uwin[HALO:HALO + TOK_TILE, :] = u
    uwin[HALO + TOK_TILE:2 * HALO + TOK_TILE, :] = kvn_ref[0:HALO, KV_CU:KV_CU + D_C].astype(jnp.float32) * next_ok
    n_ext = TOK_TILE + 2 * HALO
    a2 = uwin[0:n_ext - 1, :] + uwin[1:n_ext, :]
    a4 = a2[0:n_ext - 3, :] + a2[2:n_ext - 1, :]
    a8 = a4[0:n_ext - 7, :] + a4[4:n_ext - 3, :]
    a16 = a8[0:n_ext - 15, :] + a8[8:n_ext - 7, :]
    w2 = a2[7:7 + TOK_TILE, :]
    w4 = a4[6:6 + TOK_TILE, :]
    w8 = a8[4:4 + TOK_TILE, :]
    w16 = a16[0:TOK_TILE, :]
    lane_c = lax.broadcasted_iota(jnp.int32, (TOK_TILE, D_C), 1) // POOL_GROUP_DIM
    pooled = jnp.where(lane_c == 0, w2, jnp.where(lane_c == 1, w4, jnp.where(lane_c == 2, w8, w16)))
    half = jnp.where(lane_c == 0, 1, jnp.where(lane_c == 1, 2, jnp.where(lane_c == 2, 4, 8)))
    pos = i * TOK_TILE + lax.broadcasted_iota(jnp.int32, (TOK_TILE, D_C), 0)
    cnt = (jnp.minimum(pos + half, seq_len) - jnp.maximum(pos - half, 0)).astype(jnp.float32)
    d = (pooled / cnt - u).astype(jnp.bfloat16)
    oc = jnp.dot(d, poolw_ref[...], preferred_element_type=jnp.float32) * pools_ref[...]
    mix[:, D_A + D_B:D_MIX] = oc.astype(jnp.bfloat16)

    xm = x_ref[...] + jnp.dot(mix[...], wout_ref[...], preferred_element_type=jnp.float32)
    xmid_ref[...] = xm

    xn = _rmsnorm_f32(xm, g2_ref[...]).astype(jnp.bfloat16)
    logits = jnp.dot(xn, rw_ref[...], preferred_element_type=jnp.float32) + rb_ref[...]
    lane = lax.broadcasted_iota(jnp.int32, (TOK_TILE, LANES), 1).astype(jnp.float32)
    is_g = lane < N_GROUPS
    gl = jnp.where(is_g, logits, NEG)
    gmax = jnp.max(gl, axis=-1, keepdims=True)
    gtop = jnp.min(jnp.where(is_g & (gl == gmax), lane, float(LANES)), axis=-1, keepdims=True)
    gprob = 1.0 / jnp.sum(jnp.exp(gl - gmax), axis=-1, keepdims=True)
    e_lo = ROUTE_LANE0 + gtop * EXPERTS_PER_GROUP
    in_grp = (lane >= e_lo) & (lane < e_lo + EXPERTS_PER_GROUP)
    el = jnp.where(in_grp, logits, NEG)
    m1 = jnp.max(el, axis=-1, keepdims=True)
    i1 = jnp.min(jnp.where(in_grp & (el == m1), lane, float(LANES)), axis=-1, keepdims=True)
    el2 = jnp.where(lane == i1, NEG, el)
    m2 = jnp.max(el2, axis=-1, keepdims=True)
    i2 = jnp.min(jnp.where(in_grp & (lane != i1) & (el2 == m2), lane, float(LANES)), axis=-1, keepdims=True)
    r21 = jnp.exp(m2 - m1)
    gate1 = gprob / (1.0 + r21)
    gate2 = gprob * r21 / (1.0 + r21)

    oh1 = lane == i1
    oh2 = lane == i2
    oh = jnp.where(oh1 | oh2, 1.0, 0.0)
    earlier = jnp.dot(tri_ref[...], oh.astype(jnp.bfloat16), preferred_element_type=jnp.float32)
    n_e = jnp.sum(oh, axis=0, keepdims=True)
    chunks_e = jnp.floor((n_e + (CHUNK - 1)) * (1.0 / CHUNK))
    seg0 = jnp.dot(jnp.broadcast_to(chunks_e, (SUBLANES, LANES)).astype(jnp.bfloat16), utri_ref[...],
                   preferred_element_type=jnp.float32)[0:1, :] * CHUNK
    base = earlier + seg0
    lp1 = jnp.sum(jnp.where(oh1, base, 0.0), axis=-1, keepdims=True)
    lp2 = jnp.sum(jnp.where(oh2, base, 0.0), axis=-1, keepdims=True)
    info = jnp.where(lane == 0, lp1, jnp.where(lane == 1, lp2, 0.0))
    rinfo_ref[...] = info
    cnt_ref[...] = jnp.broadcast_to(n_e, cnt_ref.shape)

    def pieces(g):
        hi = g.astype(jnp.bfloat16).astype(jnp.float32)
        mid = (g - hi).astype(jnp.bfloat16).astype(jnp.float32)
        return hi, mid, g - hi - mid

    aux = jnp.zeros((TOK_TILE, LANES), jnp.float32)
    for k, piece in enumerate(pieces(gate1) + pieces(gate2) + (i1,)):
        aux = jnp.where(lane == k, piece, aux)

    info_t = info.T
    prow = lax.broadcasted_iota(jnp.int32, (LOCAL_ROWS, TOK_TILE), 0).astype(jnp.float32)
    sel = jnp.where((prow == info_t[0:1, :]) | (prow == info_t[1:2, :]), 1.0, 0.0).astype(jnp.bfloat16)
    moved = jnp.dot(sel, jnp.concatenate([xn, aux.astype(jnp.bfloat16)], axis=1),
                    preferred_element_type=jnp.float32)
    xs_ref[...] = moved[:, 0:D_MODEL].astype(jnp.bfloat16)
    gloc_ref[...] = moved[:, D_MODEL:]


def _mixer(x2, qp, kv, nab, swb, sinkcol, poolw, pools, wout, g2, rw, rb, tri, utri, *, batch, seq_len):
    T = x2.shape[0]
    nblk = seq_len // TOK_TILE
    n_tok_tiles = T // TOK_TILE

    def cur(b, i):
        return (b * nblk + i, 0)

    def prev(b, i):
        return (b * nblk + jnp.maximum(i - 1, 0), 0)

    def nxt(b, i):
        return (b * nblk + jnp.minimum(i + 1, nblk - 1), 0)

    def resident(a):
        zeros = (0,) * a.ndim
        return pl.BlockSpec(a.shape, lambda b, i: zeros, pipeline_mode=pl.Buffered(1))

    return pl.pallas_call(
        functools.partial(_mixer_kernel, seq_len=seq_len),
        grid=(batch, nblk),
        in_specs=[
            pl.BlockSpec((TOK_TILE, D_MODEL), cur),
            pl.BlockSpec((TOK_TILE, D_QP), cur),
            pl.BlockSpec((TOK_TILE, D_KV), prev),
            pl.BlockSpec((TOK_TILE, D_KV), cur),
            pl.BlockSpec((TOK_TILE, D_KV), nxt),
            resident(nab), resident(swb), resident(sinkcol), resident(poolw), resident(pools),
            resident(wout), resident(g2), resident(rw), resident(rb), resident(tri), resident(utri),
        ],
        out_specs=[
            pl.BlockSpec((TOK_TILE, D_MODEL), cur),
            pl.BlockSpec((LOCAL_ROWS, D_MODEL), cur),
            pl.BlockSpec((LOCAL_ROWS, LANES), cur),
            pl.BlockSpec((TOK_TILE, LANES), cur),
            pl.BlockSpec((SUBLANES, LANES), cur),
        ],
        out_shape=[
            jax.ShapeDtypeStruct((T, D_MODEL), jnp.float32),
            jax.ShapeDtypeStruct((n_tok_tiles * LOCAL_ROWS, D_MODEL), jnp.bfloat16),
            jax.ShapeDtypeStruct((n_tok_tiles * LOCAL_ROWS, LANES), jnp.float32),
            jax.ShapeDtypeStruct((T, LANES), jnp.float32),
            jax.ShapeDtypeStruct((n_tok_tiles * SUBLANES, LANES), jnp.float32),
        ],
        scratch_shapes=[
            pltpu.VMEM((3 * TOK_TILE, KW_COLS), jnp.bfloat16),
            pltpu.VMEM((TOK_TILE + 2 * HALO, D_C), jnp.float32),
            pltpu.VMEM((TOK_TILE, D_MIX), jnp.bfloat16),
        ],
        compiler_params=pltpu.CompilerParams(
            dimension_semantics=("arbitrary", "arbitrary"), vmem_limit_bytes=VMEM_LIMIT),
        name="mixer",
    )(x2, qp, kv, kv, kv, nab, swb, sinkcol, poolw, pools, wout, g2, rw, rb, tri, utri)


def _chunk_copy(src_hbm, src_chunk, dst, dst_chunk, sem):
    return pltpu.make_async_copy(
        src_hbm.at[pl.ds(pl.multiple_of(src_chunk * CHUNK, CHUNK), CHUNK)],
        dst.at[pl.ds(dst_chunk * CHUNK, CHUNK)],
        sem)


def _expert_kernel(te_ref, nsub_ref, first_ref, wslot_ref, nexte_ref, csrc_ref,
                   xs_hbm, gl_hbm, wg_hbm, wu_hbm, wd_hbm, y_ref,
                   xbuf, gbuf, wgf, wuf, wdf, wgb, wub, wdb, sem, wsem, *, layer):
    j = pl.program_id(0)
    nt = pl.num_programs(0)
    slot = j % 2

    def start_gather(tile, s, h):
        for c in range(SUB_CHUNKS):
            src = csrc_ref[tile * TILE_CHUNKS + h * SUB_CHUNKS + c]
            _chunk_copy(xs_hbm, src, xbuf.at[s, h], c, sem.at[0, s, h]).start()
            _chunk_copy(gl_hbm, src, gbuf.at[s, h], c, sem.at[1, s, h]).start()

    def weight_copies(expert, ws):
        return [pltpu.make_async_copy(w_hbm.at[layer, expert], wbuf.at[ws], wsem.at[k, ws])
                for k, (w_hbm, wbuf) in enumerate(((wg_hbm, wgf), (wu_hbm, wuf), (wd_hbm, wdf)))]

    for h in range(EXP_SUBS):
        @pl.when((j == 0) & (h < nsub_ref[0]))
        def _():
            start_gather(0, 0, h)

        @pl.when(h < nsub_ref[jnp.minimum(j + 1, nt - 1)] * (j + 1 < nt).astype(jnp.int32))
        def _():
            start_gather(j + 1, 1 - slot, h)

    ws = wslot_ref[j]

    @pl.when(j == 0)
    def _():
        for cp in weight_copies(te_ref[0], 0):
            cp.start()

    @pl.when(first_ref[j] > 0)
    def _():
        for cp in weight_copies(te_ref[j], ws):
            cp.wait()

        @pl.when(nexte_ref[j] >= 0)
        def _():
            for cp in weight_copies(nexte_ref[j], 1 - ws):
                cp.start()

        wgb[...] = wgf[ws].astype(jnp.bfloat16)
        wub[...] = wuf[ws].astype(jnp.bfloat16)
        wdb[...] = wdf[ws].astype(jnp.bfloat16)

    for h in range(EXP_SUBS):
        rows = pl.ds(h * EXP_SUB, EXP_SUB)

        @pl.when(h < nsub_ref[j])
        def _():
            pltpu.make_async_copy(xs_hbm.at[pl.ds(0, EXP_SUB)], xbuf.at[slot, h], sem.at[0, slot, h]).wait()
            pltpu.make_async_copy(gl_hbm.at[pl.ds(0, EXP_SUB)], gbuf.at[slot, h], sem.at[1, slot, h]).wait()
            xs = xbuf[slot, h]
            gp = gbuf[slot, h]
            first = gp[:, 6:7] == (te_ref[j] + ROUTE_LANE0).astype(jnp.float32)
            gate_w = jnp.where(first, gp[:, 0:1] + gp[:, 1:2] + gp[:, 2:3], gp[:, 3:4] + gp[:, 4:5] + gp[:, 5:6])
            gate = jnp.dot(xs, wgb[...], preferred_element_type=jnp.float32)
            up = jnp.dot(xs, wub[...], preferred_element_type=jnp.float32)
            act = (gate * (1.0 / (1.0 + jnp.exp(-gate))) * up * gate_w).astype(jnp.bfloat16)
            y = jnp.dot(act, wdb[...], preferred_element_type=jnp.float32)
            y_ref[rows, :] = y.astype(jnp.bfloat16)

        @pl.when(h >= nsub_ref[j])
        def _():
            y_ref[rows, :] = jnp.zeros((EXP_SUB, D_MODEL), jnp.bfloat16)


def _experts(layer, tile_tables, chunk_src, xs_local, gate_local, wg, wu, wd):
    tile_expert, n_sub, first, wslot, next_expert = tile_tables
    n_tiles = tile_expert.shape[0]
    any_space = pl.BlockSpec(memory_space=pl.ANY)
    return pl.pallas_call(
        functools.partial(_expert_kernel, layer=layer),
        grid_spec=pltpu.PrefetchScalarGridSpec(
            num_scalar_prefetch=6,
            grid=(n_tiles,),
            in_specs=[any_space] * 5,
            out_specs=pl.BlockSpec((EXP_TILE, D_MODEL), lambda j, *tables: (j, 0)),
            scratch_shapes=[
                pltpu.VMEM((2, EXP_SUBS, EXP_SUB, D_MODEL), jnp.bfloat16),
                pltpu.VMEM((2, EXP_SUBS, EXP_SUB, LANES), jnp.float32),
                pltpu.VMEM((2, D_MODEL, D_EXPERT), jnp.float32),
                pltpu.VMEM((2, D_MODEL, D_EXPERT), jnp.float32),
                pltpu.VMEM((2, D_EXPERT, D_MODEL), jnp.float32),
                pltpu.VMEM((D_MODEL, D_EXPERT), jnp.bfloat16),
                pltpu.VMEM((D_MODEL, D_EXPERT), jnp.bfloat16),
                pltpu.VMEM((D_EXPERT, D_MODEL), jnp.bfloat16),
                pltpu.SemaphoreType.DMA((2, 2, EXP_SUBS)),
                pltpu.SemaphoreType.DMA((3, 2)),
            ],
        ),
        out_shape=jax.ShapeDtypeStruct((n_tiles * EXP_TILE, D_MODEL), jnp.bfloat16),
        compiler_params=pltpu.CompilerParams(
            dimension_semantics=("arbitrary",), vmem_limit_bytes=VMEM_LIMIT),
        name="experts",
    )(tile_expert, n_sub, first, wslot, next_expert, chunk_src, xs_local, gate_local, wg, wu, wd)


def _combine_kernel(ctab_ref, xmid_ref, rinfo_ref, g_ref, ys_hbm, out_ref, ybuf, sem, *, final_norm):
    i = pl.program_id(0)
    nt = pl.num_programs(0)
    slot = i % 2

    def start_gather(tile, s):
        for c in range(LOCAL_CHUNKS):
            _chunk_copy(ys_hbm, ctab_ref[tile * LOCAL_CHUNKS + c], ybuf.at[s], c, sem.at[s]).start()

    @pl.when(i == 0)
    def _():
        start_gather(0, 0)

    @pl.when(i + 1 < nt)
    def _():
        start_gather(i + 1, 1 - slot)

    pltpu.make_async_copy(ys_hbm.at[pl.ds(0, LOCAL_ROWS)], ybuf.at[slot], sem.at[slot]).wait()

    info = rinfo_ref[...]
    pcol = lax.broadcasted_iota(jnp.int32, (TOK_TILE, LOCAL_ROWS), 1).astype(jnp.float32)
    pick = jnp.where((pcol == info[:, 0:1]) | (pcol == info[:, 1:2]), 1.0, 0.0).astype(jnp.bfloat16)
    out = xmid_ref[...] + jnp.dot(pick, ybuf[slot], preferred_element_type=jnp.float32)
    if final_norm:
        out = _rmsnorm_f32(out, g_ref[...])
    out_ref[...] = out


def _combine(chunk_tab, xmid, rinfo, g, ys, *, final_norm):
    T = xmid.shape[0]
    return pl.pallas_call(
        functools.partial(_combine_kernel, final_norm=final_norm),
        grid_spec=pltpu.PrefetchScalarGridSpec(
            num_scalar_prefetch=1,
            grid=(T // TOK_TILE,),
            in_specs=[
                pl.BlockSpec((TOK_TILE, D_MODEL), lambda i, ct: (i, 0)),
                pl.BlockSpec((TOK_TILE, LANES), lambda i, ct: (i, 0)),
                pl.BlockSpec((1, D_MODEL), lambda i, ct: (0, 0)),
                pl.BlockSpec(memory_space=pl.ANY),
            ],
            out_specs=pl.BlockSpec((TOK_TILE, D_MODEL), lambda i, ct: (i, 0)),
            scratch_shapes=[
                pltpu.VMEM((2, LOCAL_ROWS, D_MODEL), jnp.bfloat16),
                pltpu.SemaphoreType.DMA((2,)),
            ],
        ),
        out_shape=jax.ShapeDtypeStruct((T, D_MODEL), jnp.float32),
        compiler_params=pltpu.CompilerParams(
            dimension_semantics=("arbitrary",), vmem_limit_bytes=VMEM_LIMIT),
        name="combine",
    )(chunk_tab, xmid, rinfo, g, ys)


def _pair_heads(a, axis):
    shape = a.shape
    split = shape[:axis] + (SWA_KV_HEADS, SWA_REP, HEAD_DIM) + shape[axis + 1:]
    return jnp.swapaxes(a.reshape(split), axis, axis + 1).reshape(shape)


def _in_proj_weight(w):
    off_bq = 3 * D_A
    return jnp.concatenate(
        [w[:, 0:D_A], _pair_heads(w[:, off_bq:off_bq + D_B], 1), w[:, D_A:off_bq], w[:, off_bq + D_B:]],
        axis=1).astype(jnp.bfloat16)


def _out_proj_weight(w):
    return jnp.concatenate(
        [w[0:D_A], _pair_heads(w[D_A:D_A + D_B], 0), w[D_A + D_B:]], axis=0).astype(jnp.bfloat16)


def _na_bias_table(rel_bias):
    c = np.arange(GRID_W)[:, None]
    cp = np.arange(GRID_W)[None, :]
    cs = np.clip(c - NA_COLS // 2, 0, GRID_W - NA_COLS)
    valid = (cp >= cs) & (cp < cs + NA_COLS)
    d = np.arange(2 * NA_COLS - 1)[:, None, None]
    onehot = ((cp - c + (NA_COLS - 1))[None] == d) & valid[None]
    full = jnp.einsum("hrd,dcm->hrcm", rel_bias.astype(jnp.float32), jnp.asarray(onehot, jnp.float32),
                      precision=lax.Precision.HIGHEST)
    full = jnp.where(jnp.asarray(valid)[None, None], full, NEG)
    variants = []
    for k in range(NA_ROWS):
        win = full[:, NA_ROWS - 1 - k:2 * NA_ROWS - 1 - k]
        variants.append(win.transpose(0, 2, 1, 3).reshape(NA_HEADS * GRID_W, NA_ROWS * GRID_W))
    return jnp.stack(variants)


def _swa_bias_table():
    slopes = (2.0 ** (-8.0 * np.arange(1, SWA_Q_HEADS + 1) / SWA_Q_HEADS)).astype(np.float32)
    qi = np.arange(SWA_BLOCK)[:, None]
    ki = np.arange(3 * SWA_BLOCK)[None, :]
    dist = np.abs(ki - qi - SWA_BLOCK).astype(np.float32)
    tab = np.where(dist <= SWA_WINDOW, -slopes[:, None, None] * dist[None], np.float32(NEG))
    tab = tab.reshape(SWA_Q_HEADS * SWA_BLOCK, 3 * SWA_BLOCK).astype(np.float32)
    first, last = tab.copy(), tab.copy()
    first[:, :SWA_BLOCK] = NEG
    last[:, 2 * SWA_BLOCK:] = NEG
    return jnp.asarray(np.stack([first, tab, last]))


def _block_diag(pool_w):
    n = pool_w.shape[0]
    out = jnp.zeros((n * POOL_GROUP_DIM, n * POOL_GROUP_DIM), pool_w.dtype)
    for gi in range(n):
        sl = slice(gi * POOL_GROUP_DIM, (gi + 1) * POOL_GROUP_DIM)
        out = out.at[sl, sl].set(pool_w[gi])
    return out


def _router_weights(rg_w, rg_b, re_w, re_b):
    w = jnp.zeros((D_MODEL, LANES), jnp.float32)
    w = w.at[:, 0:N_GROUPS].set(rg_w).at[:, ROUTE_LANE0:ROUTE_LANE0 + N_EXPERTS].set(re_w)
    bias = jnp.zeros((1, LANES), jnp.float32)
    bias = bias.at[0, 0:N_GROUPS].set(rg_b).at[0, ROUTE_LANE0:ROUTE_LANE0 + N_EXPERTS].set(re_b)
    return w.astype(jnp.bfloat16), bias


def _dispatch_tables(cnt, n_tiles):
    nb = cnt.shape[0] // SUBLANES
    n = cnt.reshape(nb, SUBLANES, LANES)[:, 0, ROUTE_LANE0:ROUTE_LANE0 + N_EXPERTS].astype(jnp.int32)
    g = (n + (CHUNK - 1)) // CHUNK
    l_end = jnp.cumsum(g, axis=1)
    l_off = l_end - g
    c_end = jnp.cumsum(g, axis=0)
    c_off = c_end - g
    tot = c_end[-1]
    tiles = (tot + (TILE_CHUNKS - 1)) // TILE_CHUNKS
    t_end = jnp.cumsum(tiles)
    t_off = t_end - tiles
    n_used = t_end[-1:]

    experts = jnp.arange(N_EXPERTS, dtype=jnp.int32)
    tile_ids = jnp.arange(n_tiles, dtype=jnp.int32)
    tile_expert = jnp.minimum(jnp.sum((t_end[None, :] <= tile_ids[:, None]).astype(jnp.int32), axis=1),
                              N_EXPERTS - 1)
    oh_te = (tile_expert[:, None] == experts[None, :]).astype(jnp.int32)
    left = jnp.sum(oh_te * (tot + t_off * TILE_CHUNKS)[None, :], axis=1) - tile_ids * TILE_CHUNKS
    n_sub = jnp.clip((left + (SUB_CHUNKS - 1)) // SUB_CHUNKS, 0, EXP_SUBS)
    has_rows = tiles > 0
    first = ((tile_ids == jnp.sum(oh_te * t_off[None, :], axis=1)) & (n_sub > 0)).astype(jnp.int32)
    wslot = jnp.sum(oh_te * ((jnp.cumsum(has_rows.astype(jnp.int32)) - 1) % 2)[None, :], axis=1)
    later = (experts[None, :] > experts[:, None]) & has_rows[None, :]
    nxt = jnp.min(jnp.where(later, experts[None, :], N_EXPERTS), axis=1)
    next_expert = jnp.sum(oh_te * jnp.where(nxt < N_EXPERTS, nxt, -1)[None, :], axis=1)
    tile_tables = (tile_expert, n_sub, first, wslot, next_expert)

    q = jnp.arange(n_tiles * TILE_CHUNKS, dtype=jnp.int32)
    tile_q = q // TILE_CHUNKS
    oh_e = (jnp.repeat(tile_expert, TILE_CHUNKS)[:, None] == experts[None, :]).astype(jnp.int32)
    ro = q - jnp.sum(oh_e * t_off[None, :], axis=1) * TILE_CHUNKS
    valid = (ro < jnp.sum(oh_e * tot[None, :], axis=1)) & (tile_q < n_used[0])
    c_end_q = jnp.sum(oh_e[None, :, :] * c_end[:, None, :], axis=2)
    c_off_q = jnp.sum(oh_e[None, :, :] * c_off[:, None, :], axis=2)
    l_off_q = jnp.sum(oh_e[None, :, :] * l_off[:, None, :], axis=2)
    b_q = jnp.minimum(jnp.sum((c_end_q <= ro[None, :]).astype(jnp.int32), axis=0), nb - 1)
    oh_b = (jnp.arange(nb, dtype=jnp.int32)[:, None] == b_q[None, :]).astype(jnp.int32)
    src = b_q * LOCAL_CHUNKS + jnp.sum(oh_b * (l_off_q + ro[None, :] - c_off_q), axis=0)
    chunk_src = jnp.where(valid, src, LOCAL_CHUNKS - 1)

    c = jnp.arange(LOCAL_CHUNKS, dtype=jnp.int32)
    e_c = jnp.minimum(jnp.sum((l_end[:, None, :] <= c[None, :, None]).astype(jnp.int32), axis=2),
                      N_EXPERTS - 1)
    oh_ec = (e_c[:, :, None] == experts[None, None, :]).astype(jnp.int32)
    pos = (jnp.sum(oh_ec * (t_off[None, None, :] * TILE_CHUNKS + c_off[:, None, :] - l_off[:, None, :]), axis=2)
           + c[None, :])
    chunk_tab = jnp.where(c[None, :] < l_end[:, -1:], pos, 0).reshape(-1)
    return tile_tables, chunk_src, chunk_tab


def kernel(x, norm1_g, w_in, nat_bias, swa_sink, pool_w, pool_scale, w_out, norm2_g, router_g_w,
           router_g_b, router_e_w, router_e_b, expert_w_gate, expert_w_up, expert_w_down, final_g):
    batch, seq_len, _ = x.shape
    depth = w_in.shape[0]
    T = batch * seq_len
    assert seq_len % TOK_TILE == 0 and TOK_TILE % SWA_BLOCK == 0 and TOK_TILE % GRID_W == 0
    max_chunks = (2 * T) // CHUNK + (T // TOK_TILE) * N_EXPERTS
    n_tiles = max_chunks // TILE_CHUNKS + N_EXPERTS

    swb = _swa_bias_table()
    tri = jnp.asarray(np.tril(np.ones((TOK_TILE, TOK_TILE), np.float32), -1)).astype(jnp.bfloat16)
    utri = jnp.asarray(np.triu(np.ones((LANES, LANES), np.float32), 1)).astype(jnp.bfloat16)

    x2 = x.reshape(T, D_MODEL)
    for l in range(depth):
        w_in_l = _in_proj_weight(w_in[l])
        w_out_l = _out_proj_weight(w_out[l])
        nab = _na_bias_table(nat_bias[l])
        sinkcol = jnp.broadcast_to(
            jnp.repeat(swa_sink[l].astype(jnp.float32), SWA_BLOCK)[:, None], (SWA_Q_HEADS * SWA_BLOCK, LANES))
        poolw = _block_diag(pool_w[l]).astype(jnp.bfloat16)
        pools = pool_scale[l].reshape(1, D_C).astype(jnp.float32)
        rw, rb = _router_weights(router_g_w[l], router_g_b[l], router_e_w[l], router_e_b[l])

        qp, kv = _norm_proj(x2, norm1_g[l].reshape(1, D_MODEL), w_in_l)
        xmid, xs_local, gate_local, rinfo, cnt = _mixer(
            x2, qp, kv, nab, swb, sinkcol, poolw, pools, w_out_l, norm2_g[l].reshape(1, D_MODEL),
            rw, rb, tri, utri, batch=batch, seq_len=seq_len)
        tile_tables, chunk_src, chunk_tab = _dispatch_tables(cnt, n_tiles)
        ys = _experts(l, tile_tables, chunk_src, xs_local, gate_local,
                      expert_w_gate, expert_w_up, expert_w_down)
        x2 = _combine(chunk_tab, xmid, rinfo, final_g.reshape(1, D_MODEL), ys, final_norm=(l == depth - 1))
    return x2.reshape(batch, seq_len, D_MODEL)
```

```python
import functools

import jax
import jax.numpy as jnp
import numpy as np
from jax import lax
from jax.experimental import pallas as pl
from jax.experimental.pallas import tpu as pltpu

D_MODEL = 1024
GRID_W = 64
HEAD_DIM = 64
NA_HEADS = 4
NA_ROWS = 8
NA_COLS = 16
SWA_Q_HEADS = 8
SWA_KV_HEADS = 2
SWA_REP = SWA_Q_HEADS // SWA_KV_HEADS
SWA_WINDOW = 128
SWA_BLOCK = 128
POOL_WINDOWS = (2, 4, 8, 16)
POOL_GROUP_DIM = 64
D_A = NA_HEADS * HEAD_DIM
D_B = SWA_Q_HEADS * HEAD_DIM
D_BKV = SWA_KV_HEADS * HEAD_DIM
D_C = len(POOL_WINDOWS) * POOL_GROUP_DIM
D_MIX = D_A + D_B + D_C
D_QP = D_A + D_B
D_KV = 2 * D_A + 2 * D_BKV + D_C
D_IN = D_QP + D_KV
N_GROUPS = 4
EXPERTS_PER_GROUP = 8
N_EXPERTS = N_GROUPS * EXPERTS_PER_GROUP
D_EXPERT = 256
RMS_EPS = 1e-6
NEG = -1e30

LANES = 128
SUBLANES = 8
ROW_CHUNKS = D_MODEL // LANES

TOK_TILE = 512
EXP_SUB = 256
EXP_SUBS = 2
EXP_TILE = EXP_SUB * EXP_SUBS
CHUNK = 16
LOCAL_CHUNKS = (2 * TOK_TILE + N_EXPERTS * (CHUNK - 1)) // CHUNK + 2
LOCAL_ROWS = LOCAL_CHUNKS * CHUNK
SUB_CHUNKS = EXP_SUB // CHUNK
TILE_CHUNKS = EXP_TILE // CHUNK
ROUTE_LANE0 = 8
NA_ROWS_PER_STEP = 8
SWA_BLOCKS_PER_STEP = 2
HALO = 8
VMEM_LIMIT = 56 * 1024 * 1024


def _rmsnorm_f32(x, g):
    return x * lax.rsqrt(jnp.mean(x * x, axis=-1, keepdims=True) + RMS_EPS) * g


def _norm_proj_kernel(x_ref, g_ref, w_ref, qp_ref, kv_ref):
    xn = _rmsnorm_f32(x_ref[...], g_ref[...]).astype(jnp.bfloat16)
    proj = jnp.dot(xn, w_ref[...], preferred_element_type=jnp.float32)
    qp_ref[...] = proj[:, :D_QP].astype(jnp.bfloat16)
    kv_ref[...] = proj[:, D_QP:].astype(jnp.bfloat16)


def _norm_proj(x2, g, w):
    T = x2.shape[0]
    return pl.pallas_call(
        _norm_proj_kernel,
        grid=(T // TOK_TILE,),
        in_specs=[
            pl.BlockSpec((TOK_TILE, D_MODEL), lambda i: (i, 0)),
            pl.BlockSpec((1, D_MODEL), lambda i: (0, 0)),
            pl.BlockSpec((D_MODEL, D_IN), lambda i: (0, 0)),
        ],
        out_specs=[
            pl.BlockSpec((TOK_TILE, D_QP), lambda i: (i, 0)),
            pl.BlockSpec((TOK_TILE, D_KV), lambda i: (i, 0)),
        ],
        out_shape=[
            jax.ShapeDtypeStruct((T, D_QP), jnp.bfloat16),
            jax.ShapeDtypeStruct((T, D_KV), jnp.bfloat16),
        ],
        compiler_params=pltpu.CompilerParams(
            dimension_semantics=("arbitrary",), vmem_limit_bytes=VMEM_LIMIT),
        name="norm_proj",
    )(x2, g, w)


KW_AK, KW_AV, KW_BK, KW_BV = 0, D_A, 2 * D_A, 2 * D_A + D_BKV
KW_COLS = 2 * D_A + 2 * D_BKV
KV_CU = KW_COLS


def _mixer_kernel(x_ref, qp_ref, kvp_ref, kvc_ref, kvn_ref, nab_ref, swb_ref, sink_ref,
                  poolw_ref, pools_ref, wout_ref, g2_ref, rw_ref, rb_ref, tri_ref, utri_ref,
                  xmid_ref, xs_ref, gloc_ref, rinfo_ref, cnt_ref,
                  kwin, uwin, mix, *, seq_len):
    b = pl.program_id(0)
    i = pl.program_id(1)
    nblk = pl.num_programs(1)
    rows_per_tile = TOK_TILE // GRID_W
    grid_rows = seq_len // GRID_W

    kwin[0:TOK_TILE, :] = kvp_ref[:, 0:KW_COLS]
    kwin[TOK_TILE:2 * TOK_TILE, :] = kvc_ref[:, 0:KW_COLS]
    kwin[2 * TOK_TILE:3 * TOK_TILE, :] = kvn_ref[:, 0:KW_COLS]

    lane_a = lax.broadcasted_iota(jnp.int32, (GRID_W, D_A), 1) // HEAD_DIM

    def na_row(rr):
        r = i * rows_per_tile + rr
        rs = jnp.clip(r - NA_ROWS // 2, 0, grid_rows - NA_ROWS)
        variant = r - rs
        start = pl.multiple_of((rs - i * rows_per_tile + rows_per_tile) * GRID_W, GRID_W)
        q0 = pl.multiple_of(rr * GRID_W, GRID_W)
        q = qp_ref[pl.ds(q0, GRID_W), 0:D_A] * jnp.bfloat16(HEAD_DIM ** -0.5)
        zero = jnp.zeros_like(q)
        qs = jnp.concatenate([jnp.where(lane_a == h, q, zero) for h in range(NA_HEADS)], axis=0)
        kw = kwin[pl.ds(start, NA_ROWS * GRID_W), KW_AK:KW_AK + D_A]
        vw = kwin[pl.ds(start, NA_ROWS * GRID_W), KW_AV:KW_AV + D_A]
        s = lax.dot_general(qs, kw, (((1,), (1,)), ((), ())), preferred_element_type=jnp.float32)
        s = s + nab_ref[variant]
        m = jnp.max(s, axis=-1, keepdims=True)
        p = jnp.exp(s - m)
        l = jnp.sum(p, axis=-1, keepdims=True)
        pv = jnp.dot(p.astype(jnp.bfloat16), vw, preferred_element_type=jnp.float32)
        pv = pv * (1.0 / l)
        o = jnp.zeros((GRID_W, D_A), jnp.float32)
        for h in range(NA_HEADS):
            o = o + jnp.where(lane_a == h, pv[h * GRID_W:(h + 1) * GRID_W, :], 0.0)
        mix[pl.ds(q0, GRID_W), 0:D_A] = o.astype(jnp.bfloat16)

    def na_step(it, c):
        for k in range(NA_ROWS_PER_STEP):
            na_row(it * NA_ROWS_PER_STEP + k)
        return c

    lax.fori_loop(0, rows_per_tile // NA_ROWS_PER_STEP, na_step, 0)

    lane_b = lax.broadcasted_iota(jnp.int32, (SWA_BLOCK, LANES), 1) // HEAD_DIM
    blocks_per_tile = TOK_TILE // SWA_BLOCK
    nblocks = seq_len // SWA_BLOCK

    ones_v = jnp.ones((3 * SWA_BLOCK, LANES), jnp.bfloat16)

    def swa_step(sb, c):
        n = i * blocks_per_tile + sb
        variant = jnp.where(n == 0, 0, jnp.where(n == nblocks - 1, 2, 1))
        q0 = pl.multiple_of(sb * SWA_BLOCK, SWA_BLOCK)
        k0 = pl.multiple_of(TOK_TILE - SWA_BLOCK + sb * SWA_BLOCK, SWA_BLOCK)
        kw = kwin[pl.ds(k0, 3 * SWA_BLOCK), KW_BK:KW_BK + D_BKV]
        vaug = jnp.concatenate([kwin[pl.ds(k0, 3 * SWA_BLOCK), KW_BV:KW_BV + D_BKV], ones_v], axis=1)
        outs = []
        for g in range(SWA_KV_HEADS):
            pieces = []
            for t in range(SWA_REP):
                qt = qp_ref[pl.ds(q0, SWA_BLOCK), D_A + t * LANES:D_A + (t + 1) * LANES]
                qt = qt * jnp.bfloat16(HEAD_DIM ** -0.5)
                pieces.append(jnp.where(lane_b == g, qt, jnp.zeros_like(qt)))
            qs = jnp.concatenate(pieces, axis=0)
            r0 = g * SWA_REP * SWA_BLOCK
            s = lax.dot_general(qs, kw, (((1,), (1,)), ((), ())), preferred_element_type=jnp.float32)
            s = s + swb_ref[variant, r0:r0 + SWA_REP * SWA_BLOCK, :]
            sink = sink_ref[r0:r0 + SWA_REP * SWA_BLOCK, :]
            m = jnp.broadcast_to(jnp.max(s, axis=-1, keepdims=True), sink.shape)
            m = jnp.maximum(m, sink)
            p = jnp.exp(s - jnp.concatenate([m, m, m], axis=1)).astype(jnp.bfloat16)
            pv = jnp.dot(p, vaug, preferred_element_type=jnp.float32)
            l = pv[:, LANES:2 * LANES] + jnp.exp(sink - m)
            outs.append(pv[:, 0:LANES] * (1.0 / l))
        for t in range(SWA_REP):
            o0 = outs[0][t * SWA_BLOCK:(t + 1) * SWA_BLOCK, :]
            o1 = outs[1][t * SWA_BLOCK:(t + 1) * SWA_BLOCK, :]
            ot = jnp.where(lane_b == 0, o0, o1)
            mix[pl.ds(q0, SWA_BLOCK), D_A + t * LANES:D_A + (t + 1) * LANES] = ot.astype(jnp.bfloat16)
        return c

    def swa_pair(it, c):
        for k in range(SWA_BLOCKS_PER_STEP):
            swa_step(it * SWA_BLOCKS_PER_STEP + k, c)
        return c

    lax.fori_loop(0, blocks_per_tile // SWA_BLOCKS_PER_STEP, swa_pair, 0)

    u = kvc_ref[:, KV_CU:KV_CU + D_C].astype(jnp.float32)
    prev_ok = (i > 0).astype(jnp.float32)
    next_ok = (i < nblk - 1).astype(jnp.float32)
    uwin[0:HALO, :] = kvp_ref[TOK_TILE - HALO:TOK_TILE, KV_CU:KV_CU + D_C].astype(jnp.float32) * prev_ok
    uwin[HALO:HALO + TOK_TILE, :] = u
    uwin[HALO + TOK_TILE:2 * HALO + TOK_TILE, :] = kvn_ref[0:HALO, KV_CU:KV_CU + D_C].astype(jnp.float32) * next_ok
    n_ext = TOK_TILE + 2 * HALO
    a2 = uwin[0:n_ext - 1, :] + uwin[1:n_ext, :]
    a4 = a2[0:n_ext - 3, :] + a2[2:n_ext - 1, :]
    a8 = a4[0:n_ext - 7, :] + a4[4:n_ext - 3, :]
    a16 = a8[0:n_ext - 15, :] + a8[8:n_ext - 7, :]
    w2 = a2[7:7 + TOK_TILE, :]
    w4 = a4[6:6 + TOK_TILE, :]
    w8 = a8[4:4 + TOK_TILE, :]
    w16 = a16[0:TOK_TILE, :]
    lane_c = lax.broadcasted_iota(jnp.int32, (TOK_TILE, D_C), 1) // POOL_GROUP_DIM
    pooled = jnp.where(lane_c == 0, w2, jnp.where(lane_c == 1, w4, jnp.where(lane_c == 2, w8, w16)))
    half = jnp.where(lane_c == 0, 1, jnp.where(lane_c == 1, 2, jnp.where(lane_c == 2, 4, 8)))
    pos = i * TOK_TILE + lax.broadcasted_iota(jnp.int32, (TOK_TILE, D_C), 0)
    cnt = (jnp.minimum(pos + half, seq_len) - jnp.maximum(pos - half, 0)).astype(jnp.float32)
    d = (pooled / cnt - u).astype(jnp.bfloat16)
    oc = jnp.dot(d, poolw_ref[...], preferred_element_type=jnp.float32) * pools_ref[...]
    mix[:, D_A + D_B:D_MIX] = oc.astype(jnp.bfloat16)

    xm = x_ref[...] + jnp.dot(mix[...], wout_ref[...], preferred_element_type=jnp.float32)
    xmid_ref[...] = xm

    xn = _rmsnorm_f32(xm, g2_ref[...]).astype(jnp.bfloat16)
    logits = jnp.dot(xn, rw_ref[...], preferred_element_type=jnp.float32) + rb_ref[...]
    lane = lax.broadcasted_iota(jnp.int32, (TOK_TILE, LANES), 1).astype(jnp.float32)
    is_g = lane < N_GROUPS
    gl = jnp.where(is_g, logits, NEG)
    gmax = jnp.max(gl, axis=-1, keepdims=True)
    gtop = jnp.min(jnp.where(is_g & (gl == gmax), lane, float(LANES)), axis=-1, keepdims=True)
    gprob = 1.0 / jnp.sum(jnp.exp(gl - gmax), axis=-1, keepdims=True)
    e_lo = ROUTE_LANE0 + gtop * EXPERTS_PER_GROUP
    in_grp = (lane >= e_lo) & (lane < e_lo + EXPERTS_PER_GROUP)
    el = jnp.where(in_grp, logits, NEG)
    m1 = jnp.max(el, axis=-1, keepdims=True)
    i1 = jnp.min(jnp.where(in_grp & (el == m1), lane, float(LANES)), axis=-1, keepdims=True)
    el2 = jnp.where(lane == i1, NEG, el)
    m2 = jnp.max(el2, axis=-1, keepdims=True)
    i2 = jnp.min(jnp.where(in_grp & (lane != i1) & (el2 == m2), lane, float(LANES)), axis=-1, keepdims=True)
    r21 = jnp.exp(m2 - m1)
    gate1 = gprob / (1.0 + r21)
    gate2 = gprob * r21 / (1.0 + r21)

    oh1 = lane == i1
    oh2 = lane == i2
    oh = jnp.where(oh1 | oh2, 1.0, 0.0)
    earlier = jnp.dot(tri_ref[...], oh.astype(jnp.bfloat16), preferred_element_type=jnp.float32)
    n_e = jnp.sum(oh, axis=0, keepdims=True)
    chunks_e = jnp.floor((n_e + (CHUNK - 1)) * (1.0 / CHUNK))
    seg0 = jnp.dot(jnp.broadcast_to(chunks_e, (SUBLANES, LANES)).astype(jnp.bfloat16), utri_ref[...],
                   preferred_element_type=jnp.float32)[0:1, :] * CHUNK
    base = earlier + seg0
    lp1 = jnp.sum(jnp.where(oh1, base, 0.0), axis=-1, keepdims=True)
    lp2 = jnp.sum(jnp.where(oh2, base, 0.0), axis=-1, keepdims=True)
    info = jnp.where(lane == 0, lp1, jnp.where(lane == 1, lp2, 0.0))
    rinfo_ref[...] = info
    cnt_ref[...] = jnp.broadcast_to(n_e, cnt_ref.shape)

    def pieces(g):
        hi = g.astype(jnp.bfloat16).astype(jnp.float32)
        mid = (g - hi).astype(jnp.bfloat16).astype(jnp.float32)
        return hi, mid, g - hi - mid

    aux = jnp.zeros((TOK_TILE, LANES), jnp.float32)
    for k, piece in enumerate(pieces(gate1) + pieces(gate2) + (i1,)):
        aux = jnp.where(lane == k, piece, aux)

    info_t = info.T
    prow = lax.broadcasted_iota(jnp.int32, (LOCAL_ROWS, TOK_TILE), 0).astype(jnp.float32)
    sel = jnp.where((prow == info_t[0:1, :]) | (prow == info_t[1:2, :]), 1.0, 0.0).astype(jnp.bfloat16)
    moved = jnp.dot(sel, jnp.concatenate([xn, aux.astype(jnp.bfloat16)], axis=1),
                    preferred_element_type=jnp.float32)
    xs_ref[...] = moved[:, 0:D_MODEL].astype(jnp.bfloat16)
    gloc_ref[...] = moved[:, D_MODEL:]


def _mixer(x2, qp, kv, nab, swb, sinkcol, poolw, pools, wout, g2, rw, rb, tri, utri, *, batch, seq_len):
    T = x2.shape[0]
    nblk = seq_len // TOK_TILE
    n_tok_tiles = T // TOK_TILE

    def cur(b, i):
        return (b * nblk + i, 0)

    def prev(b, i):
        return (b * nblk + jnp.maximum(i - 1, 0), 0)

    def nxt(b, i):
        return (b * nblk + jnp.minimum(i + 1, nblk - 1), 0)

    def resident(a):
        zeros = (0,) * a.ndim
        return pl.BlockSpec(a.shape, lambda b, i: zeros, pipeline_mode=pl.Buffered(1))

    return pl.pallas_call(
        functools.partial(_mixer_kernel, seq_len=seq_len),
        grid=(batch, nblk),
        in_specs=[
            pl.BlockSpec((TOK_TILE, D_MODEL), cur),
            pl.BlockSpec((TOK_TILE, D_QP), cur),
            pl.BlockSpec((TOK_TILE, D_KV), prev),
            pl.BlockSpec((TOK_TILE, D_KV), cur),
            pl.BlockSpec((TOK_TILE, D_KV), nxt),
            resident(nab), resident(swb), resident(sinkcol), resident(poolw), resident(pools),
            resident(wout), resident(g2), resident(rw), resident(rb), resident(tri), resident(utri),
        ],
        out_specs=[
            pl.BlockSpec((TOK_TILE, D_MODEL), cur),
            pl.BlockSpec((LOCAL_ROWS, D_MODEL), cur),
            pl.BlockSpec((LOCAL_ROWS, LANES), cur),
            pl.BlockSpec((TOK_TILE, LANES), cur),
            pl.BlockSpec((SUBLANES, LANES), cur),
        ],
        out_shape=[
            jax.ShapeDtypeStruct((T, D_MODEL), jnp.float32),
            jax.ShapeDtypeStruct((n_tok_tiles * LOCAL_ROWS, D_MODEL), jnp.bfloat16),
            jax.ShapeDtypeStruct((n_tok_tiles * LOCAL_ROWS, LANES), jnp.float32),
            jax.ShapeDtypeStruct((T, LANES), jnp.float32),
            jax.ShapeDtypeStruct((n_tok_tiles * SUBLANES, LANES), jnp.float32),
        ],
        scratch_shapes=[
            pltpu.VMEM((3 * TOK_TILE, KW_COLS), jnp.bfloat16),
            pltpu.VMEM((TOK_TILE + 2 * HALO, D_C), jnp.float32),
            pltpu.VMEM((TOK_TILE, D_MIX), jnp.bfloat16),
        ],
        compiler_params=pltpu.CompilerParams(
            dimension_semantics=("arbitrary", "arbitrary"), vmem_limit_bytes=VMEM_LIMIT),
        name="mixer",
    )(x2, qp, kv, kv, kv, nab, swb, sinkcol, poolw, pools, wout, g2, rw, rb, tri, utri)


def _chunk_copy(src_hbm, src_chunk, dst, dst_chunk, sem):
    return pltpu.make_async_copy(
        src_hbm.at[pl.ds(pl.multiple_of(src_chunk * CHUNK, CHUNK), CHUNK)],
        dst.at[pl.ds(dst_chunk * CHUNK, CHUNK)],
        sem)


def _expert_kernel(te_ref, nsub_ref, first_ref, wslot_ref, nexte_ref, csrc_ref,
                   xs_hbm, gl_hbm, wg_hbm, wu_hbm, wd_hbm, y_ref,
                   xbuf, gbuf, wgf, wuf, wdf, wgb, wub, wdb, sem, wsem, *, layer):
    j = pl.program_id(0)
    slot = j % 2

    def start_gather(tile, s):
        for c in range(TILE_CHUNKS):
            src = csrc_ref[tile * TILE_CHUNKS + c]
            _chunk_copy(xs_hbm, src, xbuf.at[s], c, sem.at[0, s]).start()
            _chunk_copy(gl_hbm, src, gbuf.at[s], c, sem.at[1, s]).start()

    def wait_gather(s):
        pltpu.make_async_copy(xs_hbm.at[pl.ds(0, EXP_TILE)], xbuf.at[s], sem.at[0, s]).wait()
        pltpu.make_async_copy(gl_hbm.at[pl.ds(0, EXP_TILE)], gbuf.at[s], sem.at[1, s]).wait()

    def weight_copies(expert, ws):
        return [pltpu.make_async_copy(w_hbm.at[layer, expert], wbuf.at[ws], wsem.at[k, ws])
                for k, (w_hbm, wbuf) in enumerate(((wg_hbm, wgf), (wu_hbm, wuf), (wd_hbm, wdf)))]

    ws = wslot_ref[j]

    @pl.when(j == 0)
    def _():
        start_gather(0, 0)
        for cp in weight_copies(te_ref[0], 0):
            cp.start()

    @pl.when(first_ref[j] > 0)
    def _():
        for cp in weight_copies(te_ref[j], ws):
            cp.wait()

        @pl.when(nexte_ref[j] >= 0)
        def _():
            for cp in weight_copies(nexte_ref[j], 1 - ws):
                cp.start()

        wgb[...] = wgf[ws].astype(jnp.bfloat16)
        wub[...] = wuf[ws].astype(jnp.bfloat16)
        wdb[...] = wdf[ws].astype(jnp.bfloat16)

    def gated_mlp(n_rows):
        xs = xbuf[slot, 0:n_rows, :]
        gp = gbuf[slot, 0:n_rows, :]
        first = gp[:, 6:7] == (te_ref[j] + ROUTE_LANE0).astype(jnp.float32)
        gate_w = jnp.where(first, gp[:, 0:1] + gp[:, 1:2] + gp[:, 2:3], gp[:, 3:4] + gp[:, 4:5] + gp[:, 5:6])
        gate = jnp.dot(xs, wgb[...], preferred_element_type=jnp.float32)
        up = jnp.dot(xs, wub[...], preferred_element_type=jnp.float32)
        act = (gate * (1.0 / (1.0 + jnp.exp(-gate))) * up * gate_w).astype(jnp.bfloat16)
        return jnp.dot(act, wdb[...], preferred_element_type=jnp.float32).astype(jnp.bfloat16)

    for k in range(1, EXP_SUBS + 1):
        @pl.when(nsub_ref[j] == k)
        def _():
            wait_gather(slot)
            start_gather(j + 1, 1 - slot)
            y_ref[0:k * EXP_SUB, :] = gated_mlp(k * EXP_SUB)
            if k < EXP_SUBS:
                y_ref[k * EXP_SUB:EXP_TILE, :] = jnp.zeros((EXP_TILE - k * EXP_SUB, D_MODEL), jnp.bfloat16)

    @pl.when(nsub_ref[j] == 0)
    def _():
        @pl.when(nsub_ref[jnp.maximum(j - 1, 0)] > 0)
        def _():
            wait_gather(slot)

        y_ref[...] = jnp.zeros_like(y_ref)


def _experts(layer, tile_tables, chunk_src, xs_local, gate_local, wg, wu, wd):
    tile_expert, n_sub, first, wslot, next_expert = tile_tables
    n_tiles = tile_expert.shape[0]
    any_space = pl.BlockSpec(memory_space=pl.ANY)
    return pl.pallas_call(
        functools.partial(_expert_kernel, layer=layer),
        grid_spec=pltpu.PrefetchScalarGridSpec(
            num_scalar_prefetch=6,
            grid=(n_tiles,),
            in_specs=[any_space] * 5,
            out_specs=pl.BlockSpec((EXP_TILE, D_MODEL), lambda j, *tables: (j, 0)),
            scratch_shapes=[
                pltpu.VMEM((2, EXP_TILE, D_MODEL), jnp.bfloat16),
                pltpu.VMEM((2, EXP_TILE, LANES), jnp.float32),
                pltpu.VMEM((2, D_MODEL, D_EXPERT), jnp.float32),
                pltpu.VMEM((2, D_MODEL, D_EXPERT), jnp.float32),
                pltpu.VMEM((2, D_EXPERT, D_MODEL), jnp.float32),
                pltpu.VMEM((D_MODEL, D_EXPERT), jnp.bfloat16),
                pltpu.VMEM((D_MODEL, D_EXPERT), jnp.bfloat16),
                pltpu.VMEM((D_EXPERT, D_MODEL), jnp.bfloat16),
                pltpu.SemaphoreType.DMA((2, 2)),
                pltpu.SemaphoreType.DMA((3, 2)),
            ],
        ),
        out_shape=jax.ShapeDtypeStruct((n_tiles * EXP_TILE, D_MODEL), jnp.bfloat16),
        compiler_params=pltpu.CompilerParams(
            dimension_semantics=("arbitrary",), vmem_limit_bytes=VMEM_LIMIT),
        name="experts",
    )(tile_expert, n_sub, first, wslot, next_expert, chunk_src, xs_local, gate_local, wg, wu, wd)


def _combine_kernel(ctab_ref, xmid_ref, rinfo_ref, g_ref, ys_hbm, out_ref, ybuf, sem, *, final_norm):
    i = pl.program_id(0)
    nt = pl.num_programs(0)
    slot = i % 2

    def start_gather(tile, s):
        for c in range(LOCAL_CHUNKS):
            _chunk_copy(ys_hbm, ctab_ref[tile * LOCAL_CHUNKS + c], ybuf.at[s], c, sem.at[s]).start()

    def wait_gather(s):
        pltpu.make_async_copy(ys_hbm.at[pl.ds(0, LOCAL_ROWS)], ybuf.at[s], sem.at[s]).wait()

    @pl.when(i == 0)
    def _():
        start_gather(0, 0)

    @pl.when(i + 1 < nt)
    def _():
        start_gather(i + 1, 1 - slot)

    wait_gather(slot)
    info = rinfo_ref[...]
    pcol = lax.broadcasted_iota(jnp.int32, (TOK_TILE, LOCAL_ROWS), 1).astype(jnp.float32)
    pick = jnp.where((pcol == info[:, 0:1]) | (pcol == info[:, 1:2]), 1.0, 0.0).astype(jnp.bfloat16)
    out = xmid_ref[...] + jnp.dot(pick, ybuf[slot], preferred_element_type=jnp.float32)
    if final_norm:
        out = _rmsnorm_f32(out, g_ref[...])
    out_ref[...] = out


def _combine(chunk_tab, xmid, rinfo, g, ys, *, final_norm):
    T = xmid.shape[0]
    return pl.pallas_call(
        functools.partial(_combine_kernel, final_norm=final_norm),
        grid_spec=pltpu.PrefetchScalarGridSpec(
            num_scalar_prefetch=1,
            grid=(T // TOK_TILE,),
            in_specs=[
                pl.BlockSpec((TOK_TILE, D_MODEL), lambda i, ct: (i, 0)),
                pl.BlockSpec((TOK_TILE, LANES), lambda i, ct: (i, 0)),
                pl.BlockSpec((1, D_MODEL), lambda i, ct: (0, 0)),
                pl.BlockSpec(memory_space=pl.ANY),
            ],
            out_specs=pl.BlockSpec((TOK_TILE, D_MODEL), lambda i, ct: (i, 0)),
            scratch_shapes=[
                pltpu.VMEM((2, LOCAL_ROWS, D_MODEL), jnp.bfloat16),
                pltpu.SemaphoreType.DMA((2,)),
            ],
        ),
        out_shape=jax.ShapeDtypeStruct((T, D_MODEL), jnp.float32),
        compiler_params=pltpu.CompilerParams(
            dimension_semantics=("arbitrary",), vmem_limit_bytes=VMEM_LIMIT),
        name="combine",
    )(chunk_tab, xmid, rinfo, g, ys)


def _pair_heads(a, axis):
    shape = a.shape
    split = shape[:axis] + (SWA_KV_HEADS, SWA_REP, HEAD_DIM) + shape[axis + 1:]
    return jnp.swapaxes(a.reshape(split), axis, axis + 1).reshape(shape)


def _in_proj_weight(w):
    off_bq = 3 * D_A
    return jnp.concatenate(
        [w[:, 0:D_A], _pair_heads(w[:, off_bq:off_bq + D_B], 1), w[:, D_A:off_bq], w[:, off_bq + D_B:]],
        axis=1).astype(jnp.bfloat16)


def _out_proj_weight(w):
    return jnp.concatenate(
        [w[0:D_A], _pair_heads(w[D_A:D_A + D_B], 0), w[D_A + D_B:]], axis=0).astype(jnp.bfloat16)


def _na_bias_table(rel_bias):
    c = np.arange(GRID_W)[:, None]
    cp = np.arange(GRID_W)[None, :]
    cs = np.clip(c - NA_COLS // 2, 0, GRID_W - NA_COLS)
    valid = (cp >= cs) & (cp < cs + NA_COLS)
    d = np.arange(2 * NA_COLS - 1)[:, None, None]
    onehot = ((cp - c + (NA_COLS - 1))[None] == d) & valid[None]
    full = jnp.einsum("hrd,dcm->hrcm", rel_bias.astype(jnp.float32), jnp.asarray(onehot, jnp.float32),
                      precision=lax.Precision.HIGHEST)
    full = jnp.where(jnp.asarray(valid)[None, None], full, NEG)
    variants = []
    for k in range(NA_ROWS):
        win = full[:, NA_ROWS - 1 - k:2 * NA_ROWS - 1 - k]
        variants.append(win.transpose(0, 2, 1, 3).reshape(NA_HEADS * GRID_W, NA_ROWS * GRID_W))
    return jnp.stack(variants)


def _swa_bias_table():
    slopes = (2.0 ** (-8.0 * np.arange(1, SWA_Q_HEADS + 1) / SWA_Q_HEADS)).astype(np.float32)
    qi = np.arange(SWA_BLOCK)[:, None]
    ki = np.arange(3 * SWA_BLOCK)[None, :]
    dist = np.abs(ki - qi - SWA_BLOCK).astype(np.float32)
    tab = np.where(dist <= SWA_WINDOW, -slopes[:, None, None] * dist[None], np.float32(NEG))
    tab = tab.reshape(SWA_Q_HEADS * SWA_BLOCK, 3 * SWA_BLOCK).astype(np.float32)
    first, last = tab.copy(), tab.copy()
    first[:, :SWA_BLOCK] = NEG
    last[:, 2 * SWA_BLOCK:] = NEG
    return jnp.asarray(np.stack([first, tab, last]))


def _block_diag(pool_w):
    n = pool_w.shape[0]
    out = jnp.zeros((n * POOL_GROUP_DIM, n * POOL_GROUP_DIM), pool_w.dtype)
    for gi in range(n):
        sl = slice(gi * POOL_GROUP_DIM, (gi + 1) * POOL_GROUP_DIM)
        out = out.at[sl, sl].set(pool_w[gi])
    return out


def _router_weights(rg_w, rg_b, re_w, re_b):
    w = jnp.zeros((D_MODEL, LANES), jnp.float32)
    w = w.at[:, 0:N_GROUPS].set(rg_w).at[:, ROUTE_LANE0:ROUTE_LANE0 + N_EXPERTS].set(re_w)
    bias = jnp.zeros((1, LANES), jnp.float32)
    bias = bias.at[0, 0:N_GROUPS].set(rg_b).at[0, ROUTE_LANE0:ROUTE_LANE0 + N_EXPERTS].set(re_b)
    return w.astype(jnp.bfloat16), bias


def _dispatch_tables(cnt, n_tiles):
    nb = cnt.shape[0] // SUBLANES
    n = cnt.reshape(nb, SUBLANES, LANES)[:, 0, ROUTE_LANE0:ROUTE_LANE0 + N_EXPERTS].astype(jnp.int32)
    g = (n + (CHUNK - 1)) // CHUNK
    l_end = jnp.cumsum(g, axis=1)
    l_off = l_end - g
    c_end = jnp.cumsum(g, axis=0)
    c_off = c_end - g
    tot = c_end[-1]
    tiles = (tot + (TILE_CHUNKS - 1)) // TILE_CHUNKS
    t_end = jnp.cumsum(tiles)
    t_off = t_end - tiles
    n_used = t_end[-1:]

    experts = jnp.arange(N_EXPERTS, dtype=jnp.int32)
    tile_ids = jnp.arange(n_tiles, dtype=jnp.int32)
    tile_expert = jnp.minimum(jnp.sum((t_end[None, :] <= tile_ids[:, None]).astype(jnp.int32), axis=1),
                              N_EXPERTS - 1)
    oh_te = (tile_expert[:, None] == experts[None, :]).astype(jnp.int32)
    left = jnp.sum(oh_te * (tot + t_off * TILE_CHUNKS)[None, :], axis=1) - tile_ids * TILE_CHUNKS
    n_sub = jnp.clip((left + (SUB_CHUNKS - 1)) // SUB_CHUNKS, 0, EXP_SUBS)
    has_rows = tiles > 0
    first = ((tile_ids == jnp.sum(oh_te * t_off[None, :], axis=1)) & (n_sub > 0)).astype(jnp.int32)
    wslot = jnp.sum(oh_te * ((jnp.cumsum(has_rows.astype(jnp.int32)) - 1) % 2)[None, :], axis=1)
    later = (experts[None, :] > experts[:, None]) & has_rows[None, :]
    nxt = jnp.min(jnp.where(later, experts[None, :], N_EXPERTS), axis=1)
    next_expert = jnp.sum(oh_te * jnp.where(nxt < N_EXPERTS, nxt, -1)[None, :], axis=1)
    tile_tables = (tile_expert, n_sub, first, wslot, next_expert)

    q = jnp.arange(n_tiles * TILE_CHUNKS, dtype=jnp.int32)
    tile_q = q // TILE_CHUNKS
    oh_e = (jnp.repeat(tile_expert, TILE_CHUNKS)[:, None] == experts[None, :]).astype(jnp.int32)
    ro = q - jnp.sum(oh_e * t_off[None, :], axis=1) * TILE_CHUNKS
    valid = (ro < jnp.sum(oh_e * tot[None, :], axis=1)) & (tile_q < n_used[0])
    cols = jnp.dot(jnp.concatenate([c_end, c_off, l_off], axis=0).astype(jnp.float32),
                   oh_e.T.astype(jnp.float32), precision=lax.Precision.HIGHEST).astype(jnp.int32)
    c_end_q, c_off_q, l_off_q = cols[0:nb], cols[nb:2 * nb], cols[2 * nb:3 * nb]
    b_q = jnp.minimum(jnp.sum((c_end_q <= ro[None, :]).astype(jnp.int32), axis=0), nb - 1)
    oh_b = (jnp.arange(nb, dtype=jnp.int32)[:, None] == b_q[None, :]).astype(jnp.int32)
    src = b_q * LOCAL_CHUNKS + jnp.sum(oh_b * (l_off_q + ro[None, :] - c_off_q), axis=0)
    chunk_src = jnp.where(valid, src, LOCAL_CHUNKS - 1)

    c = jnp.arange(LOCAL_CHUNKS, dtype=jnp.int32)
    e_c = jnp.minimum(jnp.sum((l_end[:, None, :] <= c[None, :, None]).astype(jnp.int32), axis=2),
                      N_EXPERTS - 1)
    oh_ec = (e_c[:, :, None] == experts[None, None, :]).astype(jnp.int32)
    pos = (jnp.sum(oh_ec * (t_off[None, None, :] * TILE_CHUNKS + c_off[:, None, :] - l_off[:, None, :]), axis=2)
           + c[None, :])
    chunk_tab = jnp.where(c[None, :] < l_end[:, -1:], pos, 0).reshape(-1)
    return tile_tables, chunk_src, chunk_tab


def kernel(x, norm1_g, w_in, nat_bias, swa_sink, pool_w, pool_scale, w_out, norm2_g, router_g_w,
           router_g_b, router_e_w, router_e_b, expert_w_gate, expert_w_up, expert_w_down, final_g):
    batch, seq_len, _ = x.shape
    depth = w_in.shape[0]
    T = batch * seq_len
    assert seq_len % TOK_TILE == 0 and TOK_TILE % SWA_BLOCK == 0 and TOK_TILE % GRID_W == 0
    max_chunks = (2 * T) // CHUNK + (T // TOK_TILE) * N_EXPERTS
    n_tiles = max_chunks // TILE_CHUNKS + N_EXPERTS + 1

    swb = _swa_bias_table()
    tri = jnp.asarray(np.tril(np.ones((TOK_TILE, TOK_TILE), np.float32), -1)).astype(jnp.bfloat16)
    utri = jnp.asarray(np.triu(np.ones((LANES, LANES), np.float32), 1)).astype(jnp.bfloat16)

    x2 = x.reshape(T, D_MODEL)
    for l in range(depth):
        w_in_l = _in_proj_weight(w_in[l])
        w_out_l = _out_proj_weight(w_out[l])
        nab = _na_bias_table(nat_bias[l])
        sinkcol = jnp.broadcast_to(
            jnp.repeat(swa_sink[l].astype(jnp.float32), SWA_BLOCK)[:, None], (SWA_Q_HEADS * SWA_BLOCK, LANES))
        poolw = _block_diag(pool_w[l]).astype(jnp.bfloat16)
        pools = pool_scale[l].reshape(1, D_C).astype(jnp.float32)
        rw, rb = _router_weights(router_g_w[l], router_g_b[l], router_e_w[l], router_e_b[l])

        qp, kv = _norm_proj(x2, norm1_g[l].reshape(1, D_MODEL), w_in_l)
        xmid, xs_local, gate_local, rinfo, cnt = _mixer(
            x2, qp, kv, nab, swb, sinkcol, poolw, pools, w_out_l, norm2_g[l].reshape(1, D_MODEL),
            rw, rb, tri, utri, batch=batch, seq_len=seq_len)
        tile_tables, chunk_src, chunk_tab = _dispatch_tables(cnt, n_tiles)
        ys = _experts(l, tile_tables, chunk_src, xs_local, gate_local,
                      expert_w_gate, expert_w_up, expert_w_down)
        x2 = _combine(chunk_tab, xmid, rinfo, final_g.reshape(1, D_MODEL), ys, final_norm=(l == depth - 1))
    return x2.reshape(batch, seq_len, D_MODEL)
```

```python
import functools

import jax
import jax.numpy as jnp
import numpy as np
from jax import lax
from jax.experimental import pallas as pl
from jax.experimental.pallas import tpu as pltpu

D_MODEL = 1024
GRID_W = 64
HEAD_DIM = 64
NA_HEADS = 4
NA_ROWS = 8
NA_COLS = 16
SWA_Q_HEADS = 8
SWA_KV_HEADS = 2
SWA_REP = SWA_Q_HEADS // SWA_KV_HEADS
SWA_WINDOW = 128
SWA_BLOCK = 128
POOL_WINDOWS = (2, 4, 8, 16)
POOL_GROUP_DIM = 64
D_A = NA_HEADS * HEAD_DIM
D_B = SWA_Q_HEADS * HEAD_DIM
D_BKV = SWA_KV_HEADS * HEAD_DIM
D_C = len(POOL_WINDOWS) * POOL_GROUP_DIM
D_MIX = D_A + D_B + D_C
D_QP = D_A + D_B
D_KV = 2 * D_A + 2 * D_BKV + D_C
D_IN = D_QP + D_KV
N_GROUPS = 4
EXPERTS_PER_GROUP = 8
N_EXPERTS = N_GROUPS * EXPERTS_PER_GROUP
D_EXPERT = 256
RMS_EPS = 1e-6
NEG = -1e30

LANES = 128
SUBLANES = 8
ROW_CHUNKS = D_MODEL // LANES

TOK_TILE = 512
EXP_SUB = 256
EXP_SUBS = 2
EXP_TILE = EXP_SUB * EXP_SUBS
CHUNK = 16
LOCAL_CHUNKS = (2 * TOK_TILE + N_EXPERTS * (CHUNK - 1)) // CHUNK + 2
LOCAL_ROWS = LOCAL_CHUNKS * CHUNK
SUB_CHUNKS = EXP_SUB // CHUNK
TILE_CHUNKS = EXP_TILE // CHUNK
ROUTE_LANE0 = 8
NA_ROWS_PER_STEP = 8
SWA_BLOCKS_PER_STEP = 2
HALO = 8
VMEM_LIMIT = 56 * 1024 * 1024


def _rmsnorm_f32(x, g):
    return x * lax.rsqrt(jnp.mean(x * x, axis=-1, keepdims=True) + RMS_EPS) * g


def _norm_proj_kernel(x_ref, g_ref, w_ref, qp_ref, kv_ref):
    xn = _rmsnorm_f32(x_ref[...], g_ref[...]).astype(jnp.bfloat16)
    proj = jnp.dot(xn, w_ref[...], preferred_element_type=jnp.float32)
    qp_ref[...] = proj[:, :D_QP].astype(jnp.bfloat16)
    kv_ref[...] = proj[:, D_QP:].astype(jnp.bfloat16)


def _norm_proj(x2, g, w):
    T = x2.shape[0]
    return pl.pallas_call(
        _norm_proj_kernel,
        grid=(T // TOK_TILE,),
        in_specs=[
            pl.BlockSpec((TOK_TILE, D_MODEL), lambda i: (i, 0)),
            pl.BlockSpec((1, D_MODEL), lambda i: (0, 0)),
            pl.BlockSpec((D_MODEL, D_IN), lambda i: (0, 0)),
        ],
        out_specs=[
            pl.BlockSpec((TOK_TILE, D_QP), lambda i: (i, 0)),
            pl.BlockSpec((TOK_TILE, D_KV), lambda i: (i, 0)),
        ],
        out_shape=[
            jax.ShapeDtypeStruct((T, D_QP), jnp.bfloat16),
            jax.ShapeDtypeStruct((T, D_KV), jnp.bfloat16),
        ],
        compiler_params=pltpu.CompilerParams(
            dimension_semantics=("arbitrary",), vmem_limit_bytes=VMEM_LIMIT),
        name="norm_proj",
    )(x2, g, w)


KW_AK, KW_AV, KW_BK, KW_BV = 0, D_A, 2 * D_A, 2 * D_A + D_BKV
KW_COLS = 2 * D_A + 2 * D_BKV
KV_CU = KW_COLS


def _mixer_kernel(x_ref, qp_ref, kvp_ref, kvc_ref, kvn_ref, nab_ref, swb_ref, sink_ref,
                  poolw_ref, pools_ref, wout_ref, g2_ref, rw_ref, rb_ref, tri_ref, utri_ref,
                  xmid_ref, xs_ref, gloc_ref, rinfo_ref, cnt_ref,
                  kwin, uwin, mix, *, seq_len):
    b = pl.program_id(0)
    i = pl.program_id(1)
    nblk = pl.num_programs(1)
    rows_per_tile = TOK_TILE // GRID_W
    grid_rows = seq_len // GRID_W

    kwin[0:TOK_TILE, :] = kvp_ref[:, 0:KW_COLS]
    kwin[TOK_TILE:2 * TOK_TILE, :] = kvc_ref[:, 0:KW_COLS]
    kwin[2 * TOK_TILE:3 * TOK_TILE, :] = kvn_ref[:, 0:KW_COLS]

    lane_a = lax.broadcasted_iota(jnp.int32, (GRID_W, D_A), 1) // HEAD_DIM

    def na_row(rr):
        r = i * rows_per_tile + rr
        rs = jnp.clip(r - NA_ROWS // 2, 0, grid_rows - NA_ROWS)
        variant = r - rs
        start = pl.multiple_of((rs - i * rows_per_tile + rows_per_tile) * GRID_W, GRID_W)
        q0 = pl.multiple_of(rr * GRID_W, GRID_W)
        q = qp_ref[pl.ds(q0, GRID_W), 0:D_A] * jnp.bfloat16(HEAD_DIM ** -0.5)
        zero = jnp.zeros_like(q)
        qs = jnp.concatenate([jnp.where(lane_a == h, q, zero) for h in range(NA_HEADS)], axis=0)
        kw = kwin[pl.ds(start, NA_ROWS * GRID_W), KW_AK:KW_AK + D_A]
        vw = kwin[pl.ds(start, NA_ROWS * GRID_W), KW_AV:KW_AV + D_A]
        s = lax.dot_general(qs, kw, (((1,), (1,)), ((), ())), preferred_element_type=jnp.float32)
        s = s + nab_ref[variant]
        m = jnp.max(s, axis=-1, keepdims=True)
        p = jnp.exp(s - m)
        l = jnp.sum(p, axis=-1, keepdims=True)
        pv = jnp.dot(p.astype(jnp.bfloat16), vw, preferred_element_type=jnp.float32)
        pv = pv * (1.0 / l)
        o = jnp.zeros((GRID_W, D_A), jnp.float32)
        for h in range(NA_HEADS):
            o = o + jnp.where(lane_a == h, pv[h * GRID_W:(h + 1) * GRID_W, :], 0.0)
        mix[pl.ds(q0, GRID_W), 0:D_A] = o.astype(jnp.bfloat16)

    def na_step(it, c):
        for k in range(NA_ROWS_PER_STEP):
            na_row(it * NA_ROWS_PER_STEP + k)
        return c

    lax.fori_loop(0, rows_per_tile // NA_ROWS_PER_STEP, na_step, 0)

    lane_b = lax.broadcasted_iota(jnp.int32, (SWA_BLOCK, LANES), 1) // HEAD_DIM
    blocks_per_tile = TOK_TILE // SWA_BLOCK
    nblocks = seq_len // SWA_BLOCK

    ones_v = jnp.ones((3 * SWA_BLOCK, LANES), jnp.bfloat16)

    def swa_step(sb, c):
        n = i * blocks_per_tile + sb
        variant = jnp.where(n == 0, 0, jnp.where(n == nblocks - 1, 2, 1))
        q0 = pl.multiple_of(sb * SWA_BLOCK, SWA_BLOCK)
        k0 = pl.multiple_of(TOK_TILE - SWA_BLOCK + sb * SWA_BLOCK, SWA_BLOCK)
        kw = kwin[pl.ds(k0, 3 * SWA_BLOCK), KW_BK:KW_BK + D_BKV]
        vaug = jnp.concatenate([kwin[pl.ds(k0, 3 * SWA_BLOCK), KW_BV:KW_BV + D_BKV], ones_v], axis=1)
        outs = []
        for g in range(SWA_KV_HEADS):
            pieces = []
            for t in range(SWA_REP):
                qt = qp_ref[pl.ds(q0, SWA_BLOCK), D_A + t * LANES:D_A + (t + 1) * LANES]
                qt = qt * jnp.bfloat16(HEAD_DIM ** -0.5)
                pieces.append(jnp.where(lane_b == g, qt, jnp.zeros_like(qt)))
            qs = jnp.concatenate(pieces, axis=0)
            r0 = g * SWA_REP * SWA_BLOCK
            s = lax.dot_general(qs, kw, (((1,), (1,)), ((), ())), preferred_element_type=jnp.float32)
            s = s + swb_ref[variant, r0:r0 + SWA_REP * SWA_BLOCK, :]
            sink = sink_ref[r0:r0 + SWA_REP * SWA_BLOCK, :]
            m = jnp.broadcast_to(jnp.max(s, axis=-1, keepdims=True), sink.shape)
            m = jnp.maximum(m, sink)
            p = jnp.exp(s - jnp.concatenate([m, m, m], axis=1)).astype(jnp.bfloat16)
            pv = jnp.dot(p, vaug, preferred_element_type=jnp.float32)
            l = pv[:, LANES:2 * LANES] + jnp.exp(sink - m)
            outs.append(pv[:, 0:LANES] * (1.0 / l))
        for t in range(SWA_REP):
            o0 = outs[0][t * SWA_BLOCK:(t + 1) * SWA_BLOCK, :]
            o1 = outs[1][t * SWA_BLOCK:(t + 1) * SWA_BLOCK, :]
            ot = jnp.where(lane_b == 0, o0, o1)
            mix[pl.ds(q0, SWA_BLOCK), D_A + t * LANES:D_A + (t + 1) * LANES] = ot.astype(jnp.bfloat16)
        return c

    def swa_pair(it, c):
        for k in range(SWA_BLOCKS_PER_STEP):
            swa_step(it * SWA_BLOCKS_PER_STEP + k, c)
        return c

    lax.fori_loop(0, blocks_per_tile // SWA_BLOCKS_PER_STEP, swa_pair, 0)

    u = kvc_ref[:, KV_CU:KV_CU + D_C].astype(jnp.float32)
    prev_ok = (i > 0).astype(jnp.float32)
    next_ok = (i < nblk - 1).astype(jnp.float32)
    uwin[0:HALO, :] = kvp_ref[TOK_TILE - HALO:TOK_TILE, KV_CU:KV_CU + D_C].astype(jnp.float32) * prev_ok
    uwin[HALO:HALO + TOK_TILE, :] = u
    uwin[HALO + TOK_TILE:2 * HALO + TOK_TILE, :] = kvn_ref[0:HALO, KV_CU:KV_CU + D_C].astype(jnp.float32) * next_ok
    n_ext = TOK_TILE + 2 * HALO
    a2 = uwin[0:n_ext - 1, :] + uwin[1:n_ext, :]
    a4 = a2[0:n_ext - 3, :] + a2[2:n_ext - 1, :]
    a8 = a4[0:n_ext - 7, :] + a4[4:n_ext - 3, :]
    a16 = a8[0:n_ext - 15, :] + a8[8:n_ext - 7, :]
    w2 = a2[7:7 + TOK_TILE, :]
    w4 = a4[6:6 + TOK_TILE, :]
    w8 = a8[4:4 + TOK_TILE, :]
    w16 = a16[0:TOK_TILE, :]
    lane_c = lax.broadcasted_iota(jnp.int32, (TOK_TILE, D_C), 1) // POOL_GROUP_DIM
    pooled = jnp.where(lane_c == 0, w2, jnp.where(lane_c == 1, w4, jnp.where(lane_c == 2, w8, w16)))
    half = jnp.where(lane_c == 0, 1, jnp.where(lane_c == 1, 2, jnp.where(lane_c == 2, 4, 8)))
    pos = i * TOK_TILE + lax.broadcasted_iota(jnp.int32, (TOK_TILE, D_C), 0)
    cnt = (jnp.minimum(pos + half, seq_len) - jnp.maximum(pos - half, 0)).astype(jnp.float32)
    d = (pooled / cnt - u).astype(jnp.bfloat16)
    oc = jnp.dot(d, poolw_ref[...], preferred_element_type=jnp.float32) * pools_ref[...]
    mix[:, D_A + D_B:D_MIX] = oc.astype(jnp.bfloat16)

    xm = x_ref[...] + jnp.dot(mix[...], wout_ref[...], preferred_element_type=jnp.float32)
    xmid_ref[...] = xm

    xn = _rmsnorm_f32(xm, g2_ref[...]).astype(jnp.bfloat16)
    logits = jnp.dot(xn, rw_ref[...], preferred_element_type=jnp.float32) + rb_ref[...]
    lane = lax.broadcasted_iota(jnp.int32, (TOK_TILE, LANES), 1).astype(jnp.float32)
    is_g = lane < N_GROUPS
    gl = jnp.where(is_g, logits, NEG)
    gmax = jnp.max(gl, axis=-1, keepdims=True)
    gtop = jnp.min(jnp.where(is_g & (gl == gmax), lane, float(LANES)), axis=-1, keepdims=True)
    gprob = 1.0 / jnp.sum(jnp.exp(gl - gmax), axis=-1, keepdims=True)
    e_lo = ROUTE_LANE0 + gtop * EXPERTS_PER_GROUP
    in_grp = (lane >= e_lo) & (lane < e_lo + EXPERTS_PER_GROUP)
    el = jnp.where(in_grp, logits, NEG)
    m1 = jnp.max(el, axis=-1, keepdims=True)
    i1 = jnp.min(jnp.where(in_grp & (el == m1), lane, float(LANES)), axis=-1, keepdims=True)
    el2 = jnp.where(lane == i1, NEG, el)
    m2 = jnp.max(el2, axis=-1, keepdims=True)
    i2 = jnp.min(jnp.where(in_grp & (lane != i1) & (el2 == m2), lane, float(LANES)), axis=-1, keepdims=True)
    r21 = jnp.exp(m2 - m1)
    gate1 = gprob / (1.0 + r21)
    gate2 = gprob * r21 / (1.0 + r21)

    oh1 = lane == i1
    oh2 = lane == i2
    oh = jnp.where(oh1 | oh2, 1.0, 0.0)
    earlier = jnp.dot(tri_ref[...], oh.astype(jnp.bfloat16), preferred_element_type=jnp.float32)
    n_e = jnp.sum(oh, axis=0, keepdims=True)
    chunks_e = jnp.floor((n_e + (CHUNK - 1)) * (1.0 / CHUNK))
    seg0 = jnp.dot(jnp.broadcast_to(chunks_e, (SUBLANES, LANES)).astype(jnp.bfloat16), utri_ref[...],
                   preferred_element_type=jnp.float32)[0:1, :] * CHUNK
    base = earlier + seg0
    lp1 = jnp.sum(jnp.where(oh1, base, 0.0), axis=-1, keepdims=True)
    lp2 = jnp.sum(jnp.where(oh2, base, 0.0), axis=-1, keepdims=True)
    info = jnp.where(lane == 0, lp1, jnp.where(lane == 1, lp2, 0.0))
    rinfo_ref[...] = info
    cnt_ref[...] = jnp.broadcast_to(n_e, cnt_ref.shape)

    def pieces(g):
        hi = g.astype(jnp.bfloat16).astype(jnp.float32)
        mid = (g - hi).astype(jnp.bfloat16).astype(jnp.float32)
        return hi, mid, g - hi - mid

    aux = jnp.zeros((TOK_TILE, LANES), jnp.float32)
    for k, piece in enumerate(pieces(gate1) + pieces(gate2) + (i1,)):
        aux = jnp.where(lane == k, piece, aux)

    info_t = info.T
    prow = lax.broadcasted_iota(jnp.int32, (LOCAL_ROWS, TOK_TILE), 0).astype(jnp.float32)
    sel = jnp.where((prow == info_t[0:1, :]) | (prow == info_t[1:2, :]), 1.0, 0.0).astype(jnp.bfloat16)
    moved = jnp.dot(sel, jnp.concatenate([xn, aux.astype(jnp.bfloat16)], axis=1),
                    preferred_element_type=jnp.float32)
    xs_ref[...] = moved[:, 0:D_MODEL].astype(jnp.bfloat16)
    gloc_ref[...] = moved[:, D_MODEL:]


def _mixer(x2, qp, kv, nab, swb, sinkcol, poolw, pools, wout, g2, rw, rb, tri, utri, *, batch, seq_len):
    T = x2.shape[0]
    nblk = seq_len // TOK_TILE
    n_tok_tiles = T // TOK_TILE

    def cur(b, i):
        return (b * nblk + i, 0)

    def prev(b, i):
        return (b * nblk + jnp.maximum(i - 1, 0), 0)

    def nxt(b, i):
        return (b * nblk + jnp.minimum(i + 1, nblk - 1), 0)

    def resident(a):
        zeros = (0,) * a.ndim
        return pl.BlockSpec(a.shape, lambda b, i: zeros, pipeline_mode=pl.Buffered(1))

    return pl.pallas_call(
        functools.partial(_mixer_kernel, seq_len=seq_len),
        grid=(batch, nblk),
        in_specs=[
            pl.BlockSpec((TOK_TILE, D_MODEL), cur),
            pl.BlockSpec((TOK_TILE, D_QP), cur),
            pl.BlockSpec((TOK_TILE, D_KV), prev),
            pl.BlockSpec((TOK_TILE, D_KV), cur),
            pl.BlockSpec((TOK_TILE, D_KV), nxt),
            resident(nab), resident(swb), resident(sinkcol), resident(poolw), resident(pools),
            resident(wout), resident(g2), resident(rw), resident(rb), resident(tri), resident(utri),
        ],
        out_specs=[
            pl.BlockSpec((TOK_TILE, D_MODEL), cur),
            pl.BlockSpec((LOCAL_ROWS, D_MODEL), cur),
            pl.BlockSpec((LOCAL_ROWS, LANES), cur),
            pl.BlockSpec((TOK_TILE, LANES), cur),
            pl.BlockSpec((SUBLANES, LANES), cur),
        ],
        out_shape=[
            jax.ShapeDtypeStruct((T, D_MODEL), jnp.float32),
            jax.ShapeDtypeStruct((n_tok_tiles * LOCAL_ROWS, D_MODEL), jnp.bfloat16),
            jax.ShapeDtypeStruct((n_tok_tiles * LOCAL_ROWS, LANES), jnp.float32),
            jax.ShapeDtypeStruct((T, LANES), jnp.float32),
            jax.ShapeDtypeStruct((n_tok_tiles * SUBLANES, LANES), jnp.float32),
        ],
        scratch_shapes=[
            pltpu.VMEM((3 * TOK_TILE, KW_COLS), jnp.bfloat16),
            pltpu.VMEM((TOK_TILE + 2 * HALO, D_C), jnp.float32),
            pltpu.VMEM((TOK_TILE, D_MIX), jnp.bfloat16),
        ],
        compiler_params=pltpu.CompilerParams(
            dimension_semantics=("arbitrary", "arbitrary"), vmem_limit_bytes=VMEM_LIMIT),
        name="mixer",
    )(x2, qp, kv, kv, kv, nab, swb, sinkcol, poolw, pools, wout, g2, rw, rb, tri, utri)


def _chunk_copy(src_hbm, src_chunk, dst, dst_chunk, sem):
    return pltpu.make_async_copy(
        src_hbm.at[pl.ds(pl.multiple_of(src_chunk * CHUNK, CHUNK), CHUNK)],
        dst.at[pl.ds(dst_chunk * CHUNK, CHUNK)],
        sem)


def _expert_kernel(te_ref, nsub_ref, first_ref, wslot_ref, nexte_ref, csrc_ref,
                   xs_hbm, gl_hbm, wg_hbm, wu_hbm, wd_hbm, y_ref,
                   xbuf, gbuf, wgf, wuf, wdf, wgb, wub, wdb, sem, wsem, *, layer):
    j = pl.program_id(0)
    nt = pl.num_programs(0)
    slot = j % 2

    def start_gather(tile, s, h):
        for c in range(h * SUB_CHUNKS, (h + 1) * SUB_CHUNKS):
            src = csrc_ref[tile * TILE_CHUNKS + c]
            _chunk_copy(xs_hbm, src, xbuf.at[s], c, sem.at[0, s, h]).start()
            _chunk_copy(gl_hbm, src, gbuf.at[s], c, sem.at[1, s, h]).start()

    def wait_gather(s, h):
        rows = pl.ds(h * EXP_SUB, EXP_SUB)
        pltpu.make_async_copy(xs_hbm.at[pl.ds(0, EXP_SUB)], xbuf.at[s, rows], sem.at[0, s, h]).wait()
        pltpu.make_async_copy(gl_hbm.at[pl.ds(0, EXP_SUB)], gbuf.at[s, rows], sem.at[1, s, h]).wait()

    def weight_copies(expert, ws):
        return [pltpu.make_async_copy(w_hbm.at[layer, expert], wbuf.at[ws], wsem.at[k, ws])
                for k, (w_hbm, wbuf) in enumerate(((wg_hbm, wgf), (wu_hbm, wuf), (wd_hbm, wdf)))]

    for h in range(EXP_SUBS):
        @pl.when((j == 0) & (h < nsub_ref[0]))
        def _():
            start_gather(0, 0, h)

        @pl.when(h < nsub_ref[jnp.minimum(j + 1, nt - 1)] * (j + 1 < nt).astype(jnp.int32))
        def _():
            start_gather(j + 1, 1 - slot, h)

    ws = wslot_ref[j]

    @pl.when(j == 0)
    def _():
        for cp in weight_copies(te_ref[0], 0):
            cp.start()

    @pl.when(first_ref[j] > 0)
    def _():
        for cp in weight_copies(te_ref[j], ws):
            cp.wait()

        @pl.when(nexte_ref[j] >= 0)
        def _():
            for cp in weight_copies(nexte_ref[j], 1 - ws):
                cp.start()

        wgb[...] = wgf[ws].astype(jnp.bfloat16)
        wub[...] = wuf[ws].astype(jnp.bfloat16)
        wdb[...] = wdf[ws].astype(jnp.bfloat16)

    def gated_mlp(n_rows):
        xs = xbuf[slot, 0:n_rows, :]
        gp = gbuf[slot, 0:n_rows, :]
        first = gp[:, 6:7] == (te_ref[j] + ROUTE_LANE0).astype(jnp.float32)
        gate_w = jnp.where(first, gp[:, 0:1] + gp[:, 1:2] + gp[:, 2:3], gp[:, 3:4] + gp[:, 4:5] + gp[:, 5:6])
        gate = jnp.dot(xs, wgb[...], preferred_element_type=jnp.float32)
        up = jnp.dot(xs, wub[...], preferred_element_type=jnp.float32)
        act = (gate * (1.0 / (1.0 + jnp.exp(-gate))) * up * gate_w).astype(jnp.bfloat16)
        return jnp.dot(act, wdb[...], preferred_element_type=jnp.float32).astype(jnp.bfloat16)

    for k in range(1, EXP_SUBS + 1):
        @pl.when(nsub_ref[j] == k)
        def _():
            for h in range(k):
                wait_gather(slot, h)
            y_ref[0:k * EXP_SUB, :] = gated_mlp(k * EXP_SUB)
            if k < EXP_SUBS:
                y_ref[k * EXP_SUB:EXP_TILE, :] = jnp.zeros((EXP_TILE - k * EXP_SUB, D_MODEL), jnp.bfloat16)

    @pl.when(nsub_ref[j] == 0)
    def _():
        y_ref[...] = jnp.zeros_like(y_ref)


def _experts(layer, tile_tables, chunk_src, xs_local, gate_local, wg, wu, wd):
    tile_expert, n_sub, first, wslot, next_expert = tile_tables
    n_tiles = tile_expert.shape[0]
    any_space = pl.BlockSpec(memory_space=pl.ANY)
    return pl.pallas_call(
        functools.partial(_expert_kernel, layer=layer),
        grid_spec=pltpu.PrefetchScalarGridSpec(
            num_scalar_prefetch=6,
            grid=(n_tiles,),
            in_specs=[any_space] * 5,
            out_specs=pl.BlockSpec((EXP_TILE, D_MODEL), lambda j, *tables: (j, 0)),
            scratch_shapes=[
                pltpu.VMEM((2, EXP_TILE, D_MODEL), jnp.bfloat16),
                pltpu.VMEM((2, EXP_TILE, LANES), jnp.float32),
                pltpu.VMEM((2, D_MODEL, D_EXPERT), jnp.float32),
                pltpu.VMEM((2, D_MODEL, D_EXPERT), jnp.float32),
                pltpu.VMEM((2, D_EXPERT, D_MODEL), jnp.float32),
                pltpu.VMEM((D_MODEL, D_EXPERT), jnp.bfloat16),
                pltpu.VMEM((D_MODEL, D_EXPERT), jnp.bfloat16),
                pltpu.VMEM((D_EXPERT, D_MODEL), jnp.bfloat16),
                pltpu.SemaphoreType.DMA((2, 2, EXP_SUBS)),
                pltpu.SemaphoreType.DMA((3, 2)),
            ],
        ),
        out_shape=jax.ShapeDtypeStruct((n_tiles * EXP_TILE, D_MODEL), jnp.bfloat16),
        compiler_params=pltpu.CompilerParams(
            dimension_semantics=("arbitrary",), vmem_limit_bytes=VMEM_LIMIT),
        name="experts",
    )(tile_expert, n_sub, first, wslot, next_expert, chunk_src, xs_local, gate_local, wg, wu, wd)


def _combine_kernel(ctab_ref, xmid_ref, rinfo_ref, g_ref, ys_hbm, out_ref, ybuf, sem, *, final_norm):
    i = pl.program_id(0)
    nt = pl.num_programs(0)
    slot = i % 2

    def start_gather(tile, s):
        for c in range(LOCAL_CHUNKS):
            _chunk_copy(ys_hbm, ctab_ref[tile * LOCAL_CHUNKS + c], ybuf.at[s], c, sem.at[s]).start()

    def wait_gather(s):
        pltpu.make_async_copy(ys_hbm.at[pl.ds(0, LOCAL_ROWS)], ybuf.at[s], sem.at[s]).wait()

    @pl.when(i == 0)
    def _():
        start_gather(0, 0)

    @pl.when(i + 1 < nt)
    def _():
        start_gather(i + 1, 1 - slot)

    wait_gather(slot)
    info = rinfo_ref[...]
    pcol = lax.broadcasted_iota(jnp.int32, (TOK_TILE, LOCAL_ROWS), 1).astype(jnp.float32)
    pick = jnp.where((pcol == info[:, 0:1]) | (pcol == info[:, 1:2]), 1.0, 0.0).astype(jnp.bfloat16)
    out = xmid_ref[...] + jnp.dot(pick, ybuf[slot], preferred_element_type=jnp.float32)
    if final_norm:
        out = _rmsnorm_f32(out, g_ref[...])
    out_ref[...] = out


def _combine(chunk_tab, xmid, rinfo, g, ys, *, final_norm):
    T = xmid.shape[0]
    return pl.pallas_call(
        functools.partial(_combine_kernel, final_norm=final_norm),
        grid_spec=pltpu.PrefetchScalarGridSpec(
            num_scalar_prefetch=1,
            grid=(T // TOK_TILE,),
            in_specs=[
                pl.BlockSpec((TOK_TILE, D_MODEL), lambda i, ct: (i, 0)),
                pl.BlockSpec((TOK_TILE, LANES), lambda i, ct: (i, 0)),
                pl.BlockSpec((1, D_MODEL), lambda i, ct: (0, 0)),
                pl.BlockSpec(memory_space=pl.ANY),
            ],
            out_specs=pl.BlockSpec((TOK_TILE, D_MODEL), lambda i, ct: (i, 0)),
            scratch_shapes=[
                pltpu.VMEM((2, LOCAL_ROWS, D_MODEL), jnp.bfloat16),
                pltpu.SemaphoreType.DMA((2,)),
            ],
        ),
        out_shape=jax.ShapeDtypeStruct((T, D_MODEL), jnp.float32),
        compiler_params=pltpu.CompilerParams(
            dimension_semantics=("arbitrary",), vmem_limit_bytes=VMEM_LIMIT),
        name="combine",
    )(chunk_tab, xmid, rinfo, g, ys)


def _pair_heads(a, axis):
    shape = a.shape
    split = shape[:axis] + (SWA_KV_HEADS, SWA_REP, HEAD_DIM) + shape[axis + 1:]
    return jnp.swapaxes(a.reshape(split), axis, axis + 1).reshape(shape)


def _in_proj_weight(w):
    off_bq = 3 * D_A
    return jnp.concatenate(
        [w[:, 0:D_A], _pair_heads(w[:, off_bq:off_bq + D_B], 1), w[:, D_A:off_bq], w[:, off_bq + D_B:]],
        axis=1).astype(jnp.bfloat16)


def _out_proj_weight(w):
    return jnp.concatenate(
        [w[0:D_A], _pair_heads(w[D_A:D_A + D_B], 0), w[D_A + D_B:]], axis=0).astype(jnp.bfloat16)


def _na_bias_table(rel_bias):
    c = np.arange(GRID_W)[:, None]
    cp = np.arange(GRID_W)[None, :]
    cs = np.clip(c - NA_COLS // 2, 0, GRID_W - NA_COLS)
    valid = (cp >= cs) & (cp < cs + NA_COLS)
    d = np.arange(2 * NA_COLS - 1)[:, None, None]
    onehot = ((cp - c + (NA_COLS - 1))[None] == d) & valid[None]
    full = jnp.einsum("hrd,dcm->hrcm", rel_bias.astype(jnp.float32), jnp.asarray(onehot, jnp.float32),
                      precision=lax.Precision.HIGHEST)
    full = jnp.where(jnp.asarray(valid)[None, None], full, NEG)
    variants = []
    for k in range(NA_ROWS):
        win = full[:, NA_ROWS - 1 - k:2 * NA_ROWS - 1 - k]
        variants.append(win.transpose(0, 2, 1, 3).reshape(NA_HEADS * GRID_W, NA_ROWS * GRID_W))
    return jnp.stack(variants)


def _swa_bias_table():
    slopes = (2.0 ** (-8.0 * np.arange(1, SWA_Q_HEADS + 1) / SWA_Q_HEADS)).astype(np.float32)
    qi = np.arange(SWA_BLOCK)[:, None]
    ki = np.arange(3 * SWA_BLOCK)[None, :]
    dist = np.abs(ki - qi - SWA_BLOCK).astype(np.float32)
    tab = np.where(dist <= SWA_WINDOW, -slopes[:, None, None] * dist[None], np.float32(NEG))
    tab = tab.reshape(SWA_Q_HEADS * SWA_BLOCK, 3 * SWA_BLOCK).astype(np.float32)
    first, last = tab.copy(), tab.copy()
    first[:, :SWA_BLOCK] = NEG
    last[:, 2 * SWA_BLOCK:] = NEG
    return jnp.asarray(np.stack([first, tab, last]))


def _block_diag(pool_w):
    n = pool_w.shape[0]
    out = jnp.zeros((n * POOL_GROUP_DIM, n * POOL_GROUP_DIM), pool_w.dtype)
    for gi in range(n):
        sl = slice(gi * POOL_GROUP_DIM, (gi + 1) * POOL_GROUP_DIM)
        out = out.at[sl, sl].set(pool_w[gi])
    return out


def _router_weights(rg_w, rg_b, re_w, re_b):
    w = jnp.zeros((D_MODEL, LANES), jnp.float32)
    w = w.at[:, 0:N_GROUPS].set(rg_w).at[:, ROUTE_LANE0:ROUTE_LANE0 + N_EXPERTS].set(re_w)
    bias = jnp.zeros((1, LANES), jnp.float32)
    bias = bias.at[0, 0:N_GROUPS].set(rg_b).at[0, ROUTE_LANE0:ROUTE_LANE0 + N_EXPERTS].set(re_b)
    return w.astype(jnp.bfloat16), bias


def _dispatch_tables(cnt, n_tiles):
    nb = cnt.shape[0] // SUBLANES
    n = cnt.reshape(nb, SUBLANES, LANES)[:, 0, ROUTE_LANE0:ROUTE_LANE0 + N_EXPERTS].astype(jnp.int32)
    g = (n + (CHUNK - 1)) // CHUNK
    l_end = jnp.cumsum(g, axis=1)
    l_off = l_end - g
    c_end = jnp.cumsum(g, axis=0)
    c_off = c_end - g
    tot = c_end[-1]
    tiles = (tot + (TILE_CHUNKS - 1)) // TILE_CHUNKS
    t_end = jnp.cumsum(tiles)
    t_off = t_end - tiles
    n_used = t_end[-1:]

    experts = jnp.arange(N_EXPERTS, dtype=jnp.int32)
    tile_ids = jnp.arange(n_tiles, dtype=jnp.int32)
    tile_expert = jnp.minimum(jnp.sum((t_end[None, :] <= tile_ids[:, None]).astype(jnp.int32), axis=1),
                              N_EXPERTS - 1)
    oh_te = (tile_expert[:, None] == experts[None, :]).astype(jnp.int32)
    left = jnp.sum(oh_te * (tot + t_off * TILE_CHUNKS)[None, :], axis=1) - tile_ids * TILE_CHUNKS
    n_sub = jnp.clip((left + (SUB_CHUNKS - 1)) // SUB_CHUNKS, 0, EXP_SUBS)
    has_rows = tiles > 0
    first = ((tile_ids == jnp.sum(oh_te * t_off[None, :], axis=1)) & (n_sub > 0)).astype(jnp.int32)
    wslot = jnp.sum(oh_te * ((jnp.cumsum(has_rows.astype(jnp.int32)) - 1) % 2)[None, :], axis=1)
    later = (experts[None, :] > experts[:, None]) & has_rows[None, :]
    nxt = jnp.min(jnp.where(later, experts[None, :], N_EXPERTS), axis=1)
    next_expert = jnp.sum(oh_te * jnp.where(nxt < N_EXPERTS, nxt, -1)[None, :], axis=1)
    tile_tables = (tile_expert, n_sub, first, wslot, next_expert)

    q = jnp.arange(n_tiles * TILE_CHUNKS, dtype=jnp.int32)
    tile_q = q // TILE_CHUNKS
    oh_e = (jnp.repeat(tile_expert, TILE_CHUNKS)[:, None] == experts[None, :]).astype(jnp.int32)
    ro = q - jnp.sum(oh_e * t_off[None, :], axis=1) * TILE_CHUNKS
    valid = (ro < jnp.sum(oh_e * tot[None, :], axis=1)) & (tile_q < n_used[0])
    cols = jnp.dot(jnp.concatenate([c_end, c_off, l_off], axis=0).astype(jnp.float32),
                   oh_e.T.astype(jnp.float32), precision=lax.Precision.HIGHEST).astype(jnp.int32)
    c_end_q, c_off_q, l_off_q = cols[0:nb], cols[nb:2 * nb], cols[2 * nb:3 * nb]
    b_q = jnp.minimum(jnp.sum((c_end_q <= ro[None, :]).astype(jnp.int32), axis=0), nb - 1)
    oh_b = (jnp.arange(nb, dtype=jnp.int32)[:, None] == b_q[None, :]).astype(jnp.int32)
    src = b_q * LOCAL_CHUNKS + jnp.sum(oh_b * (l_off_q + ro[None, :] - c_off_q), axis=0)
    chunk_src = jnp.where(valid, src, LOCAL_CHUNKS - 1)

    c = jnp.arange(LOCAL_CHUNKS, dtype=jnp.int32)
    e_c = jnp.minimum(jnp.sum((l_end[:, None, :] <= c[None, :, None]).astype(jnp.int32), axis=2),
                      N_EXPERTS - 1)
    oh_ec = (e_c[:, :, None] == experts[None, None, :]).astype(jnp.int32)
    pos = (jnp.sum(oh_ec * (t_off[None, None, :] * TILE_CHUNKS + c_off[:, None, :] - l_off[:, None, :]), axis=2)
           + c[None, :])
    chunk_tab = jnp.where(c[None, :] < l_end[:, -1:], pos, 0).reshape(-1)
    return tile_tables, chunk_src, chunk_tab


def kernel(x, norm1_g, w_in, nat_bias, swa_sink, pool_w, pool_scale, w_out, norm2_g, router_g_w,
           router_g_b, router_e_w, router_e_b, expert_w_gate, expert_w_up, expert_w_down, final_g):
    batch, seq_len, _ = x.shape
    depth = w_in.shape[0]
    T = batch * seq_len
    assert seq_len % TOK_TILE == 0 and TOK_TILE % SWA_BLOCK == 0 and TOK_TILE % GRID_W == 0
    max_chunks = (2 * T) // CHUNK + (T // TOK_TILE) * N_EXPERTS
    n_tiles = max_chunks // TILE_CHUNKS + N_EXPERTS

    swb = _swa_bias_table()
    tri = jnp.asarray(np.tril(np.ones((TOK_TILE, TOK_TILE), np.float32), -1)).astype(jnp.bfloat16)
    utri = jnp.asarray(np.triu(np.ones((LANES, LANES), np.float32), 1)).astype(jnp.bfloat16)

    x2 = x.reshape(T, D_MODEL)
    for l in range(depth):
        w_in_l = _in_proj_weight(w_in[l])
        w_out_l = _out_proj_weight(w_out[l])
        nab = _na_bias_table(nat_bias[l])
        sinkcol = jnp.broadcast_to(
            jnp.repeat(swa_sink[l].astype(jnp.float32), SWA_BLOCK)[:, None], (SWA_Q_HEADS * SWA_BLOCK, LANES))
        poolw = _block_diag(pool_w[l]).astype(jnp.bfloat16)
        pools = pool_scale[l].reshape(1, D_C).astype(jnp.float32)
        rw, rb = _router_weights(router_g_w[l], router_g_b[l], router_e_w[l], router_e_b[l])

        qp, kv = _norm_proj(x2, norm1_g[l].reshape(1, D_MODEL), w_in_l)
        xmid, xs_local, gate_local, rinfo, cnt = _mixer(
            x2, qp, kv, nab, swb, sinkcol, poolw, pools, w_out_l, norm2_g[l].reshape(1, D_MODEL),
            rw, rb, tri, utri, batch=batch, seq_len=seq_len)
        tile_tables, chunk_src, chunk_tab = _dispatch_tables(cnt, n_tiles)
        ys = _experts(l, tile_tables, chunk_src, xs_local, gate_local,
                      expert_w_gate, expert_w_up, expert_w_down)
        x2 = _combine(chunk_tab, xmid, rinfo, final_g.reshape(1, D_MODEL), ys, final_norm=(l == depth - 1))
    return x2.reshape(batch, seq_len, D_MODEL)
```

```python
import functools

import jax
import jax.numpy as jnp
import numpy as np
from jax import lax
from jax.experimental import pallas as pl
from jax.experimental.pallas import tpu as pltpu

D_MODEL = 1024
GRID_W = 64
HEAD_DIM = 64
NA_HEADS = 4
NA_ROWS = 8
NA_COLS = 16
SWA_Q_HEADS = 8
SWA_KV_HEADS = 2
SWA_REP = SWA_Q_HEADS // SWA_KV_HEADS
SWA_WINDOW = 128
SWA_BLOCK = 128
POOL_WINDOWS = (2, 4, 8, 16)
POOL_GROUP_DIM = 64
D_A = NA_HEADS * HEAD_DIM
D_B = SWA_Q_HEADS * HEAD_DIM
D_BKV = SWA_KV_HEADS * HEAD_DIM
D_C = len(POOL_WINDOWS) * POOL_GROUP_DIM
D_MIX = D_A + D_B + D_C
D_QP = D_A + D_B
D_KV = 2 * D_A + 2 * D_BKV + D_C
D_IN = D_QP + D_KV
N_GROUPS = 4
EXPERTS_PER_GROUP = 8
N_EXPERTS = N_GROUPS * EXPERTS_PER_GROUP
D_EXPERT = 256
RMS_EPS = 1e-6
NEG = -1e30

LANES = 128
SUBLANES = 8
ROW_CHUNKS = D_MODEL // LANES

TOK_TILE = 512
EXP_SUB = 256
EXP_SUBS = 2
EXP_TILE = EXP_SUB * EXP_SUBS
CHUNK = 16
LOCAL_CHUNKS = (2 * TOK_TILE + N_EXPERTS * (CHUNK - 1)) // CHUNK + 2
LOCAL_ROWS = LOCAL_CHUNKS * CHUNK
SUB_CHUNKS = EXP_SUB // CHUNK
TILE_CHUNKS = EXP_TILE // CHUNK
ROUTE_LANE0 = 8
D_SLOT = D_MODEL + LANES
NA_ROWS_PER_STEP = 8
SWA_BLOCKS_PER_STEP = 2
HALO = 8
VMEM_LIMIT = 56 * 1024 * 1024


def _rmsnorm_f32(x, g):
    return x * lax.rsqrt(jnp.mean(x * x, axis=-1, keepdims=True) + RMS_EPS) * g


def _norm_proj_kernel(x_ref, g_ref, w_ref, qp_ref, kv_ref):
    xn = _rmsnorm_f32(x_ref[...], g_ref[...]).astype(jnp.bfloat16)
    proj = jnp.dot(xn, w_ref[...], preferred_element_type=jnp.float32)
    qp_ref[...] = proj[:, :D_QP].astype(jnp.bfloat16)
    kv_ref[...] = proj[:, D_QP:].astype(jnp.bfloat16)


def _norm_proj(x2, g, w):
    T = x2.shape[0]
    return pl.pallas_call(
        _norm_proj_kernel,
        grid=(T // TOK_TILE,),
        in_specs=[
            pl.BlockSpec((TOK_TILE, D_MODEL), lambda i: (i, 0)),
            pl.BlockSpec((1, D_MODEL), lambda i: (0, 0)),
            pl.BlockSpec((D_MODEL, D_IN), lambda i: (0, 0)),
        ],
        out_specs=[
            pl.BlockSpec((TOK_TILE, D_QP), lambda i: (i, 0)),
            pl.BlockSpec((TOK_TILE, D_KV), lambda i: (i, 0)),
        ],
        out_shape=[
            jax.ShapeDtypeStruct((T, D_QP), jnp.bfloat16),
            jax.ShapeDtypeStruct((T, D_KV), jnp.bfloat16),
        ],
        compiler_params=pltpu.CompilerParams(
            dimension_semantics=("arbitrary",), vmem_limit_bytes=VMEM_LIMIT),
        name="norm_proj",
    )(x2, g, w)


KW_AK, KW_AV, KW_BK, KW_BV = 0, D_A, 2 * D_A, 2 * D_A + D_BKV
KW_COLS = 2 * D_A + 2 * D_BKV
KV_CU = KW_COLS


def _mixer_kernel(x_ref, qp_ref, kvp_ref, kvc_ref, kvn_ref, nab_ref, swb_ref, sink_ref,
                  poolw_ref, pools_ref, wout_ref, g2_ref, rw_ref, rb_ref, tri_ref, utri_ref,
                  xmid_ref, xs_ref, rinfo_ref, cnt_ref,
                  kwin, uwin, mix, *, seq_len):
    b = pl.program_id(0)
    i = pl.program_id(1)
    nblk = pl.num_programs(1)
    rows_per_tile = TOK_TILE // GRID_W
    grid_rows = seq_len // GRID_W

    kwin[0:TOK_TILE, :] = kvp_ref[:, 0:KW_COLS]
    kwin[TOK_TILE:2 * TOK_TILE, :] = kvc_ref[:, 0:KW_COLS]
    kwin[2 * TOK_TILE:3 * TOK_TILE, :] = kvn_ref[:, 0:KW_COLS]

    lane_a = lax.broadcasted_iota(jnp.int32, (GRID_W, D_A), 1) // HEAD_DIM

    def na_row(rr):
        r = i * rows_per_tile + rr
        rs = jnp.clip(r - NA_ROWS // 2, 0, grid_rows - NA_ROWS)
        variant = r - rs
        start = pl.multiple_of((rs - i * rows_per_tile + rows_per_tile) * GRID_W, GRID_W)
        q0 = pl.multiple_of(rr * GRID_W, GRID_W)
        q = qp_ref[pl.ds(q0, GRID_W), 0:D_A] * jnp.bfloat16(HEAD_DIM ** -0.5)
        zero = jnp.zeros_like(q)
        qs = jnp.concatenate([jnp.where(lane_a == h, q, zero) for h in range(NA_HEADS)], axis=0)
        kw = kwin[pl.ds(start, NA_ROWS * GRID_W), KW_AK:KW_AK + D_A]
        vw = kwin[pl.ds(start, NA_ROWS * GRID_W), KW_AV:KW_AV + D_A]
        s = lax.dot_general(qs, kw, (((1,), (1,)), ((), ())), preferred_element_type=jnp.float32)
        s = s + nab_ref[variant]
        m = jnp.max(s, axis=-1, keepdims=True)
        p = jnp.exp(s - m)
        l = jnp.sum(p, axis=-1, keepdims=True)
        pv = jnp.dot(p.astype(jnp.bfloat16), vw, preferred_element_type=jnp.float32)
        pv = pv * (1.0 / l)
        o = jnp.zeros((GRID_W, D_A), jnp.float32)
        for h in range(NA_HEADS):
            o = o + jnp.where(lane_a == h, pv[h * GRID_W:(h + 1) * GRID_W, :], 0.0)
        mix[pl.ds(q0, GRID_W), 0:D_A] = o.astype(jnp.bfloat16)

    def na_step(it, c):
        for k in range(NA_ROWS_PER_STEP):
            na_row(it * NA_ROWS_PER_STEP + k)
        return c

    lax.fori_loop(0, rows_per_tile // NA_ROWS_PER_STEP, na_step, 0)

    lane_b = lax.broadcasted_iota(jnp.int32, (SWA_BLOCK, LANES), 1) // HEAD_DIM
    blocks_per_tile = TOK_TILE // SWA_BLOCK
    nblocks = seq_len // SWA_BLOCK

    ones_v = jnp.ones((3 * SWA_BLOCK, LANES), jnp.bfloat16)

    def swa_step(sb, c):
        n = i * blocks_per_tile + sb
        variant = jnp.where(n == 0, 0, jnp.where(n == nblocks - 1, 2, 1))
        q0 = pl.multiple_of(sb * SWA_BLOCK, SWA_BLOCK)
        k0 = pl.multiple_of(TOK_TILE - SWA_BLOCK + sb * SWA_BLOCK, SWA_BLOCK)
        kw = kwin[pl.ds(k0, 3 * SWA_BLOCK), KW_BK:KW_BK + D_BKV]
        vaug = jnp.concatenate([kwin[pl.ds(k0, 3 * SWA_BLOCK), KW_BV:KW_BV + D_BKV], ones_v], axis=1)
        outs = []
        for g in range(SWA_KV_HEADS):
            pieces = []
            for t in range(SWA_REP):
                qt = qp_ref[pl.ds(q0, SWA_BLOCK), D_A + t * LANES:D_A + (t + 1) * LANES]
                qt = qt * jnp.bfloat16(HEAD_DIM ** -0.5)
                pieces.append(jnp.where(lane_b == g, qt, jnp.zeros_like(qt)))
            qs = jnp.concatenate(pieces, axis=0)
            r0 = g * SWA_REP * SWA_BLOCK
            s = lax.dot_general(qs, kw, (((1,), (1,)), ((), ())), preferred_element_type=jnp.float32)
            s = s + swb_ref[variant, r0:r0 + SWA_REP * SWA_BLOCK, :]
            sink = sink_ref[r0:r0 + SWA_REP * SWA_BLOCK, :]
            m = jnp.broadcast_to(jnp.max(s, axis=-1, keepdims=True), sink.shape)
            m = jnp.maximum(m, sink)
            p = jnp.exp(s - jnp.concatenate([m, m, m], axis=1)).astype(jnp.bfloat16)
            pv = jnp.dot(p, vaug, preferred_element_type=jnp.float32)
            l = pv[:, LANES:2 * LANES] + jnp.exp(sink - m)
            outs.append(pv[:, 0:LANES] * (1.0 / l))
        for t in range(SWA_REP):
            o0 = outs[0][t * SWA_BLOCK:(t + 1) * SWA_BLOCK, :]
            o1 = outs[1][t * SWA_BLOCK:(t + 1) * SWA_BLOCK, :]
            ot = jnp.where(lane_b == 0, o0, o1)
            mix[pl.ds(q0, SWA_BLOCK), D_A + t * LANES:D_A + (t + 1) * LANES] = ot.astype(jnp.bfloat16)
        return c

    def swa_pair(it, c):
        for k in range(SWA_BLOCKS_PER_STEP):
            swa_step(it * SWA_BLOCKS_PER_STEP + k, c)
        return c

    lax.fori_loop(0, blocks_per_tile // SWA_BLOCKS_PER_STEP, swa_pair, 0)

    u = kvc_ref[:, KV_CU:KV_CU + D_C].astype(jnp.float32)
    prev_ok = (i > 0).astype(jnp.float32)
    next_ok = (i < nblk - 1).astype(jnp.float32)
    uwin[0:HALO, :] = kvp_ref[TOK_TILE - HALO:TOK_TILE, KV_CU:KV_CU + D_C].astype(jnp.float32) * prev_ok
    uwin[HALO:HALO + TOK_TILE, :] = u
    uwin[HALO + TOK_TILE:2 * HALO + TOK_TILE, :] = kvn_ref[0:HALO, KV_CU:KV_CU + D_C].astype(jnp.float32) * next_ok
    n_ext = TOK_TILE + 2 * HALO
    a2 = uwin[0:n_ext - 1, :] + uwin[1:n_ext, :]
    a4 = a2[0:n_ext - 3, :] + a2[2:n_ext - 1, :]
    a8 = a4[0:n_ext - 7, :] + a4[4:n_ext - 3, :]
    a16 = a8[0:n_ext - 15, :] + a8[8:n_ext - 7, :]
    w2 = a2[7:7 + TOK_TILE, :]
    w4 = a4[6:6 + TOK_TILE, :]
    w8 = a8[4:4 + TOK_TILE, :]
    w16 = a16[0:TOK_TILE, :]
    lane_c = lax.broadcasted_iota(jnp.int32, (TOK_TILE, D_C), 1) // POOL_GROUP_DIM
    pooled = jnp.where(lane_c == 0, w2, jnp.where(lane_c == 1, w4, jnp.where(lane_c == 2, w8, w16)))
    half = jnp.where(lane_c == 0, 1, jnp.where(lane_c == 1, 2, jnp.where(lane_c == 2, 4, 8)))
    pos = i * TOK_TILE + lax.broadcasted_iota(jnp.int32, (TOK_TILE, D_C), 0)
    cnt = (jnp.minimum(pos + half, seq_len) - jnp.maximum(pos - half, 0)).astype(jnp.float32)
    d = (pooled / cnt - u).astype(jnp.bfloat16)
    oc = jnp.dot(d, poolw_ref[...], preferred_element_type=jnp.float32) * pools_ref[...]
    mix[:, D_A + D_B:D_MIX] = oc.astype(jnp.bfloat16)

    xm = x_ref[...] + jnp.dot(mix[...], wout_ref[...], preferred_element_type=jnp.float32)
    xmid_ref[...] = xm

    xn = _rmsnorm_f32(xm, g2_ref[...]).astype(jnp.bfloat16)
    logits = jnp.dot(xn, rw_ref[...], preferred_element_type=jnp.float32) + rb_ref[...]
    lane = lax.broadcasted_iota(jnp.int32, (TOK_TILE, LANES), 1).astype(jnp.float32)
    is_g = lane < N_GROUPS
    gl = jnp.where(is_g, logits, NEG)
    gmax = jnp.max(gl, axis=-1, keepdims=True)
    gtop = jnp.min(jnp.where(is_g & (gl == gmax), lane, float(LANES)), axis=-1, keepdims=True)
    gprob = 1.0 / jnp.sum(jnp.exp(gl - gmax), axis=-1, keepdims=True)
    e_lo = ROUTE_LANE0 + gtop * EXPERTS_PER_GROUP
    in_grp = (lane >= e_lo) & (lane < e_lo + EXPERTS_PER_GROUP)
    el = jnp.where(in_grp, logits, NEG)
    m1 = jnp.max(el, axis=-1, keepdims=True)
    i1 = jnp.min(jnp.where(in_grp & (el == m1), lane, float(LANES)), axis=-1, keepdims=True)
    el2 = jnp.where(lane == i1, NEG, el)
    m2 = jnp.max(el2, axis=-1, keepdims=True)
    i2 = jnp.min(jnp.where(in_grp & (lane != i1) & (el2 == m2), lane, float(LANES)), axis=-1, keepdims=True)
    r21 = jnp.exp(m2 - m1)
    gate1 = gprob / (1.0 + r21)
    gate2 = gprob * r21 / (1.0 + r21)

    oh1 = lane == i1
    oh2 = lane == i2
    oh = jnp.where(oh1 | oh2, 1.0, 0.0)
    earlier = jnp.dot(tri_ref[...], oh.astype(jnp.bfloat16), preferred_element_type=jnp.float32)
    n_e = jnp.sum(oh, axis=0, keepdims=True)
    chunks_e = jnp.floor((n_e + (CHUNK - 1)) * (1.0 / CHUNK))
    seg0 = jnp.dot(jnp.broadcast_to(chunks_e, (SUBLANES, LANES)).astype(jnp.bfloat16), utri_ref[...],
                   preferred_element_type=jnp.float32)[0:1, :] * CHUNK
    base = earlier + seg0
    lp1 = jnp.sum(jnp.where(oh1, base, 0.0), axis=-1, keepdims=True)
    lp2 = jnp.sum(jnp.where(oh2, base, 0.0), axis=-1, keepdims=True)
    info = jnp.where(lane == 0, lp1, jnp.where(lane == 1, lp2, 0.0))
    rinfo_ref[...] = info
    cnt_ref[...] = jnp.broadcast_to(n_e, cnt_ref.shape)

    def pieces(g):
        hi = g.astype(jnp.bfloat16).astype(jnp.float32)
        mid = (g - hi).astype(jnp.bfloat16).astype(jnp.float32)
        return hi, mid, g - hi - mid

    aux = jnp.zeros((TOK_TILE, LANES), jnp.float32)
    for k, piece in enumerate(pieces(gate1) + pieces(gate2) + (i1,)):
        aux = jnp.where(lane == k, piece, aux)

    info_t = info.T
    prow = lax.broadcasted_iota(jnp.int32, (LOCAL_ROWS, TOK_TILE), 0).astype(jnp.float32)
    sel = jnp.where((prow == info_t[0:1, :]) | (prow == info_t[1:2, :]), 1.0, 0.0).astype(jnp.bfloat16)
    moved = jnp.dot(sel, jnp.concatenate([xn, aux.astype(jnp.bfloat16)], axis=1),
                    preferred_element_type=jnp.float32)
    xs_ref[...] = moved.astype(jnp.bfloat16)


def _mixer(x2, qp, kv, nab, swb, sinkcol, poolw, pools, wout, g2, rw, rb, tri, utri, *, batch, seq_len):
    T = x2.shape[0]
    nblk = seq_len // TOK_TILE
    n_tok_tiles = T // TOK_TILE

    def cur(b, i):
        return (b * nblk + i, 0)

    def prev(b, i):
        return (b * nblk + jnp.maximum(i - 1, 0), 0)

    def nxt(b, i):
        return (b * nblk + jnp.minimum(i + 1, nblk - 1), 0)

    def resident(a):
        zeros = (0,) * a.ndim
        return pl.BlockSpec(a.shape, lambda b, i: zeros, pipeline_mode=pl.Buffered(1))

    return pl.pallas_call(
        functools.partial(_mixer_kernel, seq_len=seq_len),
        grid=(batch, nblk),
        in_specs=[
            pl.BlockSpec((TOK_TILE, D_MODEL), cur),
            pl.BlockSpec((TOK_TILE, D_QP), cur),
            pl.BlockSpec((TOK_TILE, D_KV), prev),
            pl.BlockSpec((TOK_TILE, D_KV), cur),
            pl.BlockSpec((TOK_TILE, D_KV), nxt),
            resident(nab), resident(swb), resident(sinkcol), resident(poolw), resident(pools),
            resident(wout), resident(g2), resident(rw), resident(rb), resident(tri), resident(utri),
        ],
        out_specs=[
            pl.BlockSpec((TOK_TILE, D_MODEL), cur),
            pl.BlockSpec((LOCAL_ROWS, D_SLOT), cur),
            pl.BlockSpec((TOK_TILE, LANES), cur),
            pl.BlockSpec((SUBLANES, LANES), cur),
        ],
        out_shape=[
            jax.ShapeDtypeStruct((T, D_MODEL), jnp.float32),
            jax.ShapeDtypeStruct((n_tok_tiles * LOCAL_ROWS, D_SLOT), jnp.bfloat16),
            jax.ShapeDtypeStruct((T, LANES), jnp.float32),
            jax.ShapeDtypeStruct((n_tok_tiles * SUBLANES, LANES), jnp.float32),
        ],
        scratch_shapes=[
            pltpu.VMEM((3 * TOK_TILE, KW_COLS), jnp.bfloat16),
            pltpu.VMEM((TOK_TILE + 2 * HALO, D_C), jnp.float32),
            pltpu.VMEM((TOK_TILE, D_MIX), jnp.bfloat16),
        ],
        compiler_params=pltpu.CompilerParams(
            dimension_semantics=("arbitrary", "arbitrary"), vmem_limit_bytes=VMEM_LIMIT),
        name="mixer",
    )(x2, qp, kv, kv, kv, nab, swb, sinkcol, poolw, pools, wout, g2, rw, rb, tri, utri)


def _chunk_copy(src_hbm, src_chunk, dst, dst_chunk, sem):
    return pltpu.make_async_copy(
        src_hbm.at[pl.ds(pl.multiple_of(src_chunk * CHUNK, CHUNK), CHUNK)],
        dst.at[pl.ds(dst_chunk * CHUNK, CHUNK)],
        sem)


def _expert_kernel(te_ref, nsub_ref, first_ref, wslot_ref, nexte_ref, csrc_ref,
                   xs_hbm, wg_hbm, wu_hbm, wd_hbm, y_ref,
                   xbuf, wgf, wuf, wdf, wgb, wub, wdb, sem, wsem, *, layer):
    j = pl.program_id(0)
    nt = pl.num_programs(0)
    slot = j % 2

    def start_gather(tile, s, h):
        for c in range(h * SUB_CHUNKS, (h + 1) * SUB_CHUNKS):
            _chunk_copy(xs_hbm, csrc_ref[tile * TILE_CHUNKS + c], xbuf.at[s], c, sem.at[s, h]).start()

    def wait_gather(s, h):
        rows = pl.ds(h * EXP_SUB, EXP_SUB)
        pltpu.make_async_copy(xs_hbm.at[pl.ds(0, EXP_SUB)], xbuf.at[s, rows], sem.at[s, h]).wait()

    def weight_copies(expert, ws):
        return [pltpu.make_async_copy(w_hbm.at[layer, expert], wbuf.at[ws], wsem.at[k, ws])
                for k, (w_hbm, wbuf) in enumerate(((wg_hbm, wgf), (wu_hbm, wuf), (wd_hbm, wdf)))]

    for h in range(EXP_SUBS):
        @pl.when((j == 0) & (h < nsub_ref[0]))
        def _():
            start_gather(0, 0, h)

        @pl.when(h < nsub_ref[jnp.minimum(j + 1, nt - 1)] * (j + 1 < nt).astype(jnp.int32))
        def _():
            start_gather(j + 1, 1 - slot, h)

    ws = wslot_ref[j]

    @pl.when(j == 0)
    def _():
        for cp in weight_copies(te_ref[0], 0):
            cp.start()

    @pl.when(first_ref[j] > 0)
    def _():
        for cp in weight_copies(te_ref[j], ws):
            cp.wait()

        @pl.when(nexte_ref[j] >= 0)
        def _():
            for cp in weight_copies(nexte_ref[j], 1 - ws):
                cp.start()

        wgb[...] = wgf[ws].astype(jnp.bfloat16)
        wub[...] = wuf[ws].astype(jnp.bfloat16)
        wdb[...] = wdf[ws].astype(jnp.bfloat16)

    def gated_mlp(n_rows):
        xs = xbuf[slot, 0:n_rows, 0:D_MODEL]
        gp = xbuf[slot, 0:n_rows, D_MODEL:D_SLOT].astype(jnp.float32)
        first = gp[:, 6:7] == (te_ref[j] + ROUTE_LANE0).astype(jnp.float32)
        gate_w = jnp.where(first, gp[:, 0:1] + gp[:, 1:2] + gp[:, 2:3], gp[:, 3:4] + gp[:, 4:5] + gp[:, 5:6])
        gate = jnp.dot(xs, wgb[...], preferred_element_type=jnp.float32)
        up = jnp.dot(xs, wub[...], preferred_element_type=jnp.float32)
        act = (gate * (1.0 / (1.0 + jnp.exp(-gate))) * up * gate_w).astype(jnp.bfloat16)
        return jnp.dot(act, wdb[...], preferred_element_type=jnp.float32).astype(jnp.bfloat16)

    for k in range(1, EXP_SUBS + 1):
        @pl.when(nsub_ref[j] == k)
        def _():
            for h in range(k):
                wait_gather(slot, h)
            y_ref[0:k * EXP_SUB, :] = gated_mlp(k * EXP_SUB)
            if k < EXP_SUBS:
                y_ref[k * EXP_SUB:EXP_TILE, :] = jnp.zeros((EXP_TILE - k * EXP_SUB, D_MODEL), jnp.bfloat16)

    @pl.when(nsub_ref[j] == 0)
    def _():
        y_ref[...] = jnp.zeros_like(y_ref)


def _experts(layer, tile_tables, chunk_src, xs_local, wg, wu, wd):
    tile_expert, n_sub, first, wslot, next_expert = tile_tables
    n_tiles = tile_expert.shape[0]
    any_space = pl.BlockSpec(memory_space=pl.ANY)

    return pl.pallas_call(
        functools.partial(_expert_kernel, layer=layer),
        grid_spec=pltpu.PrefetchScalarGridSpec(
            num_scalar_prefetch=6,
            grid=(n_tiles,),
            in_specs=[any_space] * 4,
            out_specs=pl.BlockSpec((EXP_TILE, D_MODEL), lambda j, *tables: (j, 0)),
            scratch_shapes=[
                pltpu.VMEM((2, EXP_TILE, D_SLOT), jnp.bfloat16),
                pltpu.VMEM((2, D_MODEL, D_EXPERT), jnp.float32),
                pltpu.VMEM((2, D_MODEL, D_EXPERT), jnp.float32),
                pltpu.VMEM((2, D_EXPERT, D_MODEL), jnp.float32),
                pltpu.VMEM((D_MODEL, D_EXPERT), jnp.bfloat16),
                pltpu.VMEM((D_MODEL, D_EXPERT), jnp.bfloat16),
                pltpu.VMEM((D_EXPERT, D_MODEL), jnp.bfloat16),
                pltpu.SemaphoreType.DMA((2, EXP_SUBS)),
                pltpu.SemaphoreType.DMA((3, 2)),
            ],
        ),
        out_shape=jax.ShapeDtypeStruct((n_tiles * EXP_TILE, D_MODEL), jnp.bfloat16),
        compiler_params=pltpu.CompilerParams(
            dimension_semantics=("arbitrary",), vmem_limit_bytes=VMEM_LIMIT),
        name="experts",
    )(tile_expert, n_sub, first, wslot, next_expert, chunk_src, xs_local, wg, wu, wd)


def _combine_kernel(ctab_ref, xmid_ref, rinfo_ref, g_ref, ys_hbm, out_ref, ybuf, sem, *, final_norm):
    i = pl.program_id(0)
    nt = pl.num_programs(0)
    slot = i % 2

    def start_gather(tile, s):
        for c in range(LOCAL_CHUNKS):
            _chunk_copy(ys_hbm, ctab_ref[tile * LOCAL_CHUNKS + c], ybuf.at[s], c, sem.at[s]).start()

    def wait_gather(s):
        pltpu.make_async_copy(ys_hbm.at[pl.ds(0, LOCAL_ROWS)], ybuf.at[s], sem.at[s]).wait()

    @pl.when(i == 0)
    def _():
        start_gather(0, 0)

    @pl.when(i + 1 < nt)
    def _():
        start_gather(i + 1, 1 - slot)

    wait_gather(slot)
    info = rinfo_ref[...]
    pcol = lax.broadcasted_iota(jnp.int32, (TOK_TILE, LOCAL_ROWS), 1).astype(jnp.float32)
    pick = jnp.where((pcol == info[:, 0:1]) | (pcol == info[:, 1:2]), 1.0, 0.0).astype(jnp.bfloat16)
    out = xmid_ref[...] + jnp.dot(pick, ybuf[slot], preferred_element_type=jnp.float32)
    if final_norm:
        out = _rmsnorm_f32(out, g_ref[...])
    out_ref[...] = out


def _combine(chunk_tab, xmid, rinfo, g, ys, *, final_norm):
    T = xmid.shape[0]
    return pl.pallas_call(
        functools.partial(_combine_kernel, final_norm=final_norm),
        grid_spec=pltpu.PrefetchScalarGridSpec(
            num_scalar_prefetch=1,
            grid=(T // TOK_TILE,),
            in_specs=[
                pl.BlockSpec((TOK_TILE, D_MODEL), lambda i, ct: (i, 0)),
                pl.BlockSpec((TOK_TILE, LANES), lambda i, ct: (i, 0)),
                pl.BlockSpec((1, D_MODEL), lambda i, ct: (0, 0)),
                pl.BlockSpec(memory_space=pl.ANY),
            ],
            out_specs=pl.BlockSpec((TOK_TILE, D_MODEL), lambda i, ct: (i, 0)),
            scratch_shapes=[
                pltpu.VMEM((2, LOCAL_ROWS, D_MODEL), jnp.bfloat16),
                pltpu.SemaphoreType.DMA((2,)),
            ],
        ),
        out_shape=jax.ShapeDtypeStruct((T, D_MODEL), jnp.float32),
        compiler_params=pltpu.CompilerParams(
            dimension_semantics=("arbitrary",), vmem_limit_bytes=VMEM_LIMIT),
        name="combine",
    )(chunk_tab, xmid, rinfo, g, ys)


def _pair_heads(a, axis):
    shape = a.shape
    split = shape[:axis] + (SWA_KV_HEADS, SWA_REP, HEAD_DIM) + shape[axis + 1:]
    return jnp.swapaxes(a.reshape(split), axis, axis + 1).reshape(shape)


def _in_proj_weight(w):
    off_bq = 3 * D_A
    return jnp.concatenate(
        [w[:, 0:D_A], _pair_heads(w[:, off_bq:off_bq + D_B], 1), w[:, D_A:off_bq], w[:, off_bq + D_B:]],
        axis=1).astype(jnp.bfloat16)


def _out_proj_weight(w):
    return jnp.concatenate(
        [w[0:D_A], _pair_heads(w[D_A:D_A + D_B], 0), w[D_A + D_B:]], axis=0).astype(jnp.bfloat16)


def _na_bias_table(rel_bias):
    c = np.arange(GRID_W)[:, None]
    cp = np.arange(GRID_W)[None, :]
    cs = np.clip(c - NA_COLS // 2, 0, GRID_W - NA_COLS)
    valid = (cp >= cs) & (cp < cs + NA_COLS)
    d = np.arange(2 * NA_COLS - 1)[:, None, None]
    onehot = ((cp - c + (NA_COLS - 1))[None] == d) & valid[None]
    full = jnp.einsum("hrd,dcm->hrcm", rel_bias.astype(jnp.float32), jnp.asarray(onehot, jnp.float32),
                      precision=lax.Precision.HIGHEST)
    full = jnp.where(jnp.asarray(valid)[None, None], full, NEG)
    variants = []
    for k in range(NA_ROWS):
        win = full[:, NA_ROWS - 1 - k:2 * NA_ROWS - 1 - k]
        variants.append(win.transpose(0, 2, 1, 3).reshape(NA_HEADS * GRID_W, NA_ROWS * GRID_W))
    return jnp.stack(variants)


def _swa_bias_table():
    slopes = (2.0 ** (-8.0 * np.arange(1, SWA_Q_HEADS + 1) / SWA_Q_HEADS)).astype(np.float32)
    qi = np.arange(SWA_BLOCK)[:, None]
    ki = np.arange(3 * SWA_BLOCK)[None, :]
    dist = np.abs(ki - qi - SWA_BLOCK).astype(np.float32)
    tab = np.where(dist <= SWA_WINDOW, -slopes[:, None, None] * dist[None], np.float32(NEG))
    tab = tab.reshape(SWA_Q_HEADS * SWA_BLOCK, 3 * SWA_BLOCK).astype(np.float32)
    first, last = tab.copy(), tab.copy()
    first[:, :SWA_BLOCK] = NEG
    last[:, 2 * SWA_BLOCK:] = NEG
    return jnp.asarray(np.stack([first, tab, last]))


def _block_diag(pool_w):
    n = pool_w.shape[0]
    out = jnp.zeros((n * POOL_GROUP_DIM, n * POOL_GROUP_DIM), pool_w.dtype)
    for gi in range(n):
        sl = slice(gi * POOL_GROUP_DIM, (gi + 1) * POOL_GROUP_DIM)
        out = out.at[sl, sl].set(pool_w[gi])
    return out


def _router_weights(rg_w, rg_b, re_w, re_b):
    w = jnp.zeros((D_MODEL, LANES), jnp.float32)
    w = w.at[:, 0:N_GROUPS].set(rg_w).at[:, ROUTE_LANE0:ROUTE_LANE0 + N_EXPERTS].set(re_w)
    bias = jnp.zeros((1, LANES), jnp.float32)
    bias = bias.at[0, 0:N_GROUPS].set(rg_b).at[0, ROUTE_LANE0:ROUTE_LANE0 + N_EXPERTS].set(re_b)
    return w.astype(jnp.bfloat16), bias


def _dispatch_tables(cnt, n_tiles):
    nb = cnt.shape[0] // SUBLANES
    n = cnt.reshape(nb, SUBLANES, LANES)[:, 0, ROUTE_LANE0:ROUTE_LANE0 + N_EXPERTS].astype(jnp.int32)
    g = (n + (CHUNK - 1)) // CHUNK
    l_end = jnp.cumsum(g, axis=1)
    l_off = l_end - g
    c_end = jnp.cumsum(g, axis=0)
    c_off = c_end - g
    tot = c_end[-1]
    tiles = (tot + (TILE_CHUNKS - 1)) // TILE_CHUNKS
    t_end = jnp.cumsum(tiles)
    t_off = t_end - tiles
    n_used = t_end[-1:]

    experts = jnp.arange(N_EXPERTS, dtype=jnp.int32)
    tile_ids = jnp.arange(n_tiles, dtype=jnp.int32)
    tile_expert = jnp.minimum(jnp.sum((t_end[None, :] <= tile_ids[:, None]).astype(jnp.int32), axis=1),
                              N_EXPERTS - 1)
    oh_te = (tile_expert[:, None] == experts[None, :]).astype(jnp.int32)
    left = jnp.sum(oh_te * (tot + t_off * TILE_CHUNKS)[None, :], axis=1) - tile_ids * TILE_CHUNKS
    n_sub = jnp.clip((left + (SUB_CHUNKS - 1)) // SUB_CHUNKS, 0, EXP_SUBS)
    has_rows = tiles > 0
    first = ((tile_ids == jnp.sum(oh_te * t_off[None, :], axis=1)) & (n_sub > 0)).astype(jnp.int32)
    wslot = jnp.sum(oh_te * ((jnp.cumsum(has_rows.astype(jnp.int32)) - 1) % 2)[None, :], axis=1)
    later = (experts[None, :] > experts[:, None]) & has_rows[None, :]
    nxt = jnp.min(jnp.where(later, experts[None, :], N_EXPERTS), axis=1)
    next_expert = jnp.sum(oh_te * jnp.where(nxt < N_EXPERTS, nxt, -1)[None, :], axis=1)
    tile_tables = (tile_expert, n_sub, first, wslot, next_expert)

    q = jnp.arange(n_tiles * TILE_CHUNKS, dtype=jnp.int32)
    tile_q = q // TILE_CHUNKS
    oh_e = (jnp.repeat(tile_expert, TILE_CHUNKS)[:, None] == experts[None, :]).astype(jnp.int32)
    ro = q - jnp.sum(oh_e * t_off[None, :], axis=1) * TILE_CHUNKS
    valid = (ro < jnp.sum(oh_e * tot[None, :], axis=1)) & (tile_q < n_used[0])
    cols = jnp.dot(jnp.concatenate([c_end, c_off, l_off], axis=0).astype(jnp.float32),
                   oh_e.T.astype(jnp.float32), precision=lax.Precision.HIGHEST).astype(jnp.int32)
    c_end_q, c_off_q, l_off_q = cols[0:nb], cols[nb:2 * nb], cols[2 * nb:3 * nb]
    b_q = jnp.minimum(jnp.sum((c_end_q <= ro[None, :]).astype(jnp.int32), axis=0), nb - 1)
    oh_b = (jnp.arange(nb, dtype=jnp.int32)[:, None] == b_q[None, :]).astype(jnp.int32)
    src = b_q * LOCAL_CHUNKS + jnp.sum(oh_b * (l_off_q + ro[None, :] - c_off_q), axis=0)
    chunk_src = jnp.where(valid, src, LOCAL_CHUNKS - 1)

    c = jnp.arange(LOCAL_CHUNKS, dtype=jnp.int32)
    e_c = jnp.minimum(jnp.sum((l_end[:, None, :] <= c[None, :, None]).astype(jnp.int32), axis=2),
                      N_EXPERTS - 1)
    oh_ec = (e_c[:, :, None] == experts[None, None, :]).astype(jnp.int32)
    pos = (jnp.sum(oh_ec * (t_off[None, None, :] * TILE_CHUNKS + c_off[:, None, :] - l_off[:, None, :]), axis=2)
           + c[None, :])
    chunk_tab = jnp.where(c[None, :] < l_end[:, -1:], pos, 0).reshape(-1)
    return tile_tables, chunk_src, chunk_tab


def kernel(x, norm1_g, w_in, nat_bias, swa_sink, pool_w, pool_scale, w_out, norm2_g, router_g_w,
           router_g_b, router_e_w, router_e_b, expert_w_gate, expert_w_up, expert_w_down, final_g):
    batch, seq_len, _ = x.shape
    depth = w_in.shape[0]
    T = batch * seq_len
    assert seq_len % TOK_TILE == 0 and TOK_TILE % SWA_BLOCK == 0 and TOK_TILE % GRID_W == 0
    max_chunks = (2 * T) // CHUNK + (T // TOK_TILE) * N_EXPERTS
    n_tiles = max_chunks // TILE_CHUNKS + N_EXPERTS

    swb = _swa_bias_table()
    tri = jnp.asarray(np.tril(np.ones((TOK_TILE, TOK_TILE), np.float32), -1)).astype(jnp.bfloat16)
    utri = jnp.asarray(np.triu(np.ones((LANES, LANES), np.float32), 1)).astype(jnp.bfloat16)

    x2 = x.reshape(T, D_MODEL)
    for l in range(depth):
        w_in_l = _in_proj_weight(w_in[l])
        w_out_l = _out_proj_weight(w_out[l])
        nab = _na_bias_table(nat_bias[l])
        sinkcol = jnp.broadcast_to(
            jnp.repeat(swa_sink[l].astype(jnp.float32), SWA_BLOCK)[:, None], (SWA_Q_HEADS * SWA_BLOCK, LANES))
        poolw = _block_diag(pool_w[l]).astype(jnp.bfloat16)
        pools = pool_scale[l].reshape(1, D_C).astype(jnp.float32)
        rw, rb = _router_weights(router_g_w[l], router_g_b[l], router_e_w[l], router_e_b[l])

        qp, kv = _norm_proj(x2, norm1_g[l].reshape(1, D_MODEL), w_in_l)
        xmid, xs_local, rinfo, cnt = _mixer(
            x2, qp, kv, nab, swb, sinkcol, poolw, pools, w_out_l, norm2_g[l].reshape(1, D_MODEL),
            rw, rb, tri, utri, batch=batch, seq_len=seq_len)
        tile_tables, chunk_src, chunk_tab = _dispatch_tables(cnt, n_tiles)
        ys = _experts(l, tile_tables, chunk_src, xs_local,
                      expert_w_gate, expert_w_up, expert_w_down)
        x2 = _combine(chunk_tab, xmid, rinfo, final_g.reshape(1, D_MODEL), ys, final_norm=(l == depth - 1))
    return x2.reshape(batch, seq_len, D_MODEL)
```

```python
import functools

import jax
import jax.numpy as jnp
import numpy as np
from jax import lax
from jax.experimental import pallas as pl
from jax.experimental.pallas import tpu as pltpu

D_MODEL = 1024
GRID_W = 64
HEAD_DIM = 64
NA_HEADS = 4
NA_ROWS = 8
NA_COLS = 16
SWA_Q_HEADS = 8
SWA_KV_HEADS = 2
SWA_REP = SWA_Q_HEADS // SWA_KV_HEADS
SWA_WINDOW = 128
SWA_BLOCK = 128
POOL_WINDOWS = (2, 4, 8, 16)
POOL_GROUP_DIM = 64
D_A = NA_HEADS * HEAD_DIM
D_B = SWA_Q_HEADS * HEAD_DIM
D_BKV = SWA_KV_HEADS * HEAD_DIM
D_C = len(POOL_WINDOWS) * POOL_GROUP_DIM
D_MIX = D_A + D_B + D_C
D_QP = D_A + D_B
D_KV = 2 * D_A + 2 * D_BKV + D_C
D_IN = D_QP + D_KV
N_GROUPS = 4
EXPERTS_PER_GROUP = 8
N_EXPERTS = N_GROUPS * EXPERTS_PER_GROUP
D_EXPERT = 256
RMS_EPS = 1e-6
NEG = -1e30

LANES = 128
SUBLANES = 8
ROW_CHUNKS = D_MODEL // LANES

TOK_TILE = 512
EXP_SUB = 256
EXP_SUBS = 2
EXP_TILE = EXP_SUB * EXP_SUBS
CHUNK = 16
LOCAL_CHUNKS = (2 * TOK_TILE + N_EXPERTS * (CHUNK - 1)) // CHUNK + 2
LOCAL_ROWS = LOCAL_CHUNKS * CHUNK
SUB_CHUNKS = EXP_SUB // CHUNK
TILE_CHUNKS = EXP_TILE // CHUNK
ROUTE_LANE0 = 8
D_SLOT = D_MODEL + LANES
NA_ROWS_PER_STEP = 8
SWA_BLOCKS_PER_STEP = 2
HALO = 8
VMEM_LIMIT = 56 * 1024 * 1024


def _rmsnorm_f32(x, g):
    return x * lax.rsqrt(jnp.mean(x * x, axis=-1, keepdims=True) + RMS_EPS) * g


def _norm_proj_kernel(x_ref, g_ref, w_ref, qp_ref, kv_ref):
    xn = _rmsnorm_f32(x_ref[...], g_ref[...]).astype(jnp.bfloat16)
    proj = jnp.dot(xn, w_ref[...], preferred_element_type=jnp.float32)
    qp_ref[...] = proj[:, :D_QP].astype(jnp.bfloat16)
    kv_ref[...] = proj[:, D_QP:].astype(jnp.bfloat16)


def _norm_proj(layer, x2, g, w):
    T = x2.shape[0]
    return pl.pallas_call(
        _norm_proj_kernel,
        grid=(T // TOK_TILE,),
        in_specs=[
            pl.BlockSpec((TOK_TILE, D_MODEL), lambda i: (i, 0)),
            pl.BlockSpec((None, 1, D_MODEL), lambda i: (layer, 0, 0)),
            pl.BlockSpec((None, D_MODEL, D_IN), lambda i: (layer, 0, 0)),
        ],
        out_specs=[
            pl.BlockSpec((TOK_TILE, D_QP), lambda i: (i, 0)),
            pl.BlockSpec((TOK_TILE, D_KV), lambda i: (i, 0)),
        ],
        out_shape=[
            jax.ShapeDtypeStruct((T, D_QP), jnp.bfloat16),
            jax.ShapeDtypeStruct((T, D_KV), jnp.bfloat16),
        ],
        compiler_params=pltpu.CompilerParams(
            dimension_semantics=("arbitrary",), vmem_limit_bytes=VMEM_LIMIT),
        name="norm_proj",
    )(x2, g, w)


KW_AK, KW_AV, KW_BK, KW_BV = 0, D_A, 2 * D_A, 2 * D_A + D_BKV
KW_COLS = 2 * D_A + 2 * D_BKV
KV_CU = KW_COLS


def _mixer_kernel(x_ref, qp_ref, kvp_ref, kvc_ref, kvn_ref, nab_ref, swb_ref, sink_ref,
                  poolw_ref, pools_ref, wout_ref, g2_ref, rw_ref, rb_ref, tri_ref, utri_ref,
                  xmid_ref, xs_ref, rinfo_ref, cnt_ref,
                  kwin, uwin, mix, *, seq_len):
    b = pl.program_id(0)
    i = pl.program_id(1)
    nblk = pl.num_programs(1)
    rows_per_tile = TOK_TILE // GRID_W
    grid_rows = seq_len // GRID_W

    kwin[0:TOK_TILE, :] = kvp_ref[:, 0:KW_COLS]
    kwin[TOK_TILE:2 * TOK_TILE, :] = kvc_ref[:, 0:KW_COLS]
    kwin[2 * TOK_TILE:3 * TOK_TILE, :] = kvn_ref[:, 0:KW_COLS]

    lane_a = lax.broadcasted_iota(jnp.int32, (GRID_W, D_A), 1) // HEAD_DIM

    def na_row(rr):
        r = i * rows_per_tile + rr
        rs = jnp.clip(r - NA_ROWS // 2, 0, grid_rows - NA_ROWS)
        variant = r - rs
        start = pl.multiple_of((rs - i * rows_per_tile + rows_per_tile) * GRID_W, GRID_W)
        q0 = pl.multiple_of(rr * GRID_W, GRID_W)
        q = qp_ref[pl.ds(q0, GRID_W), 0:D_A] * jnp.bfloat16(HEAD_DIM ** -0.5)
        zero = jnp.zeros_like(q)
        qs = jnp.concatenate([jnp.where(lane_a == h, q, zero) for h in range(NA_HEADS)], axis=0)
        kw = kwin[pl.ds(start, NA_ROWS * GRID_W), KW_AK:KW_AK + D_A]
        vw = kwin[pl.ds(start, NA_ROWS * GRID_W), KW_AV:KW_AV + D_A]
        s = lax.dot_general(qs, kw, (((1,), (1,)), ((), ())), preferred_element_type=jnp.float32)
        s = s + nab_ref[variant]
        m = jnp.max(s, axis=-1, keepdims=True)
        p = jnp.exp(s - m)
        l = jnp.sum(p, axis=-1, keepdims=True)
        pv = jnp.dot(p.astype(jnp.bfloat16), vw, preferred_element_type=jnp.float32)
        pv = pv * (1.0 / l)
        o = jnp.zeros((GRID_W, D_A), jnp.float32)
        for h in range(NA_HEADS):
            o = o + jnp.where(lane_a == h, pv[h * GRID_W:(h + 1) * GRID_W, :], 0.0)
        mix[pl.ds(q0, GRID_W), 0:D_A] = o.astype(jnp.bfloat16)

    def na_step(it, c):
        for k in range(NA_ROWS_PER_STEP):
            na_row(it * NA_ROWS_PER_STEP + k)
        return c

    lax.fori_loop(0, rows_per_tile // NA_ROWS_PER_STEP, na_step, 0)

    lane_b = lax.broadcasted_iota(jnp.int32, (SWA_BLOCK, LANES), 1) // HEAD_DIM
    blocks_per_tile = TOK_TILE // SWA_BLOCK
    nblocks = seq_len // SWA_BLOCK

    ones_v = jnp.ones((3 * SWA_BLOCK, LANES), jnp.bfloat16)

    def swa_step(sb, c):
        n = i * blocks_per_tile + sb
        variant = jnp.where(n == 0, 0, jnp.where(n == nblocks - 1, 2, 1))
        q0 = pl.multiple_of(sb * SWA_BLOCK, SWA_BLOCK)
        k0 = pl.multiple_of(TOK_TILE - SWA_BLOCK + sb * SWA_BLOCK, SWA_BLOCK)
        kw = kwin[pl.ds(k0, 3 * SWA_BLOCK), KW_BK:KW_BK + D_BKV]
        vaug = jnp.concatenate([kwin[pl.ds(k0, 3 * SWA_BLOCK), KW_BV:KW_BV + D_BKV], ones_v], axis=1)
        outs = []
        for g in range(SWA_KV_HEADS):
            pieces = []
            for t in range(SWA_REP):
                qt = qp_ref[pl.ds(q0, SWA_BLOCK), D_A + t * LANES:D_A + (t + 1) * LANES]
                qt = qt * jnp.bfloat16(HEAD_DIM ** -0.5)
                pieces.append(jnp.where(lane_b == g, qt, jnp.zeros_like(qt)))
            qs = jnp.concatenate(pieces, axis=0)
            r0 = g * SWA_REP * SWA_BLOCK
            s = lax.dot_general(qs, kw, (((1,), (1,)), ((), ())), preferred_element_type=jnp.float32)
            s = s + swb_ref[variant, r0:r0 + SWA_REP * SWA_BLOCK, :]
            sink = sink_ref[r0:r0 + SWA_REP * SWA_BLOCK, :]
            m = jnp.broadcast_to(jnp.max(s, axis=-1, keepdims=True), sink.shape)
            m = jnp.maximum(m, sink)
            p = jnp.exp(s - jnp.concatenate([m, m, m], axis=1)).astype(jnp.bfloat16)
            pv = jnp.dot(p, vaug, preferred_element_type=jnp.float32)
            l = pv[:, LANES:2 * LANES] + jnp.exp(sink - m)
            outs.append(pv[:, 0:LANES] * (1.0 / l))
        for t in range(SWA_REP):
            o0 = outs[0][t * SWA_BLOCK:(t + 1) * SWA_BLOCK, :]
            o1 = outs[1][t * SWA_BLOCK:(t + 1) * SWA_BLOCK, :]
            ot = jnp.where(lane_b == 0, o0, o1)
            mix[pl.ds(q0, SWA_BLOCK), D_A + t * LANES:D_A + (t + 1) * LANES] = ot.astype(jnp.bfloat16)
        return c

    def swa_pair(it, c):
        for k in range(SWA_BLOCKS_PER_STEP):
            swa_step(it * SWA_BLOCKS_PER_STEP + k, c)
        return c

    lax.fori_loop(0, blocks_per_tile // SWA_BLOCKS_PER_STEP, swa_pair, 0)

    u = kvc_ref[:, KV_CU:KV_CU + D_C].astype(jnp.float32)
    prev_ok = (i > 0).astype(jnp.float32)
    next_ok = (i < nblk - 1).astype(jnp.float32)
    uwin[0:HALO, :] = kvp_ref[TOK_TILE - HALO:TOK_TILE, KV_CU:KV_CU + D_C].astype(jnp.float32) * prev_ok
    uwin[HALO:HALO + TOK_TILE, :] = u
    uwin[HALO + TOK_TILE:2 * HALO + TOK_TILE, :] = kvn_ref[0:HALO, KV_CU:KV_CU + D_C].astype(jnp.float32) * next_ok
    n_ext = TOK_TILE + 2 * HALO
    a2 = uwin[0:n_ext - 1, :] + uwin[1:n_ext, :]
    a4 = a2[0:n_ext - 3, :] + a2[2:n_ext - 1, :]
    a8 = a4[0:n_ext - 7, :] + a4[4:n_ext - 3, :]
    a16 = a8[0:n_ext - 15, :] + a8[8:n_ext - 7, :]
    w2 = a2[7:7 + TOK_TILE, :]
    w4 = a4[6:6 + TOK_TILE, :]
    w8 = a8[4:4 + TOK_TILE, :]
    w16 = a16[0:TOK_TILE, :]
    lane_c = lax.broadcasted_iota(jnp.int32, (TOK_TILE, D_C), 1) // POOL_GROUP_DIM
    pooled = jnp.where(lane_c == 0, w2, jnp.where(lane_c == 1, w4, jnp.where(lane_c == 2, w8, w16)))
    half = jnp.where(lane_c == 0, 1, jnp.where(lane_c == 1, 2, jnp.where(lane_c == 2, 4, 8)))
    pos = i * TOK_TILE + lax.broadcasted_iota(jnp.int32, (TOK_TILE, D_C), 0)
    cnt = (jnp.minimum(pos + half, seq_len) - jnp.maximum(pos - half, 0)).astype(jnp.float32)
    d = (pooled / cnt - u).astype(jnp.bfloat16)
    oc = jnp.dot(d, poolw_ref[...], preferred_element_type=jnp.float32) * pools_ref[...]
    mix[:, D_A + D_B:D_MIX] = oc.astype(jnp.bfloat16)

    xm = x_ref[...] + jnp.dot(mix[...], wout_ref[...], preferred_element_type=jnp.float32)
    xmid_ref[...] = xm

    xn = _rmsnorm_f32(xm, g2_ref[...]).astype(jnp.bfloat16)
    logits = jnp.dot(xn, rw_ref[...], preferred_element_type=jnp.float32) + rb_ref[...]
    lane = lax.broadcasted_iota(jnp.int32, (TOK_TILE, LANES), 1).astype(jnp.float32)
    is_g = lane < N_GROUPS
    gl = jnp.where(is_g, logits, NEG)
    gmax = jnp.max(gl, axis=-1, keepdims=True)
    gtop = jnp.min(jnp.where(is_g & (gl == gmax), lane, float(LANES)), axis=-1, keepdims=True)
    gprob = 1.0 / jnp.sum(jnp.exp(gl - gmax), axis=-1, keepdims=True)
    e_lo = ROUTE_LANE0 + gtop * EXPERTS_PER_GROUP
    in_grp = (lane >= e_lo) & (lane < e_lo + EXPERTS_PER_GROUP)
    el = jnp.where(in_grp, logits, NEG)
    m1 = jnp.max(el, axis=-1, keepdims=True)
    i1 = jnp.min(jnp.where(in_grp & (el == m1), lane, float(LANES)), axis=-1, keepdims=True)
    el2 = jnp.where(lane == i1, NEG, el)
    m2 = jnp.max(el2, axis=-1, keepdims=True)
    i2 = jnp.min(jnp.where(in_grp & (lane != i1) & (el2 == m2), lane, float(LANES)), axis=-1, keepdims=True)
    r21 = jnp.exp(m2 - m1)
    gate1 = gprob / (1.0 + r21)
    gate2 = gprob * r21 / (1.0 + r21)

    oh1 = lane == i1
    oh2 = lane == i2
    oh = jnp.where(oh1 | oh2, 1.0, 0.0)
    earlier = jnp.dot(tri_ref[...], oh.astype(jnp.bfloat16), preferred_element_type=jnp.float32)
    n_e = jnp.sum(oh, axis=0, keepdims=True)
    chunks_e = jnp.floor((n_e + (CHUNK - 1)) * (1.0 / CHUNK))
    seg0 = jnp.dot(jnp.broadcast_to(chunks_e, (SUBLANES, LANES)).astype(jnp.bfloat16), utri_ref[...],
                   preferred_element_type=jnp.float32)[0:1, :] * CHUNK
    base = earlier + seg0
    lp1 = jnp.sum(jnp.where(oh1, base, 0.0), axis=-1, keepdims=True)
    lp2 = jnp.sum(jnp.where(oh2, base, 0.0), axis=-1, keepdims=True)
    info = jnp.where(lane == 0, lp1, jnp.where(lane == 1, lp2, 0.0))
    rinfo_ref[...] = info
    cnt_ref[...] = jnp.broadcast_to(n_e, cnt_ref.shape)

    def pieces(g):
        hi = g.astype(jnp.bfloat16).astype(jnp.float32)
        mid = (g - hi).astype(jnp.bfloat16).astype(jnp.float32)
        return hi, mid, g - hi - mid

    aux = jnp.zeros((TOK_TILE, LANES), jnp.float32)
    for k, piece in enumerate(pieces(gate1) + pieces(gate2) + (i1,)):
        aux = jnp.where(lane == k, piece, aux)

    info_t = info.T
    prow = lax.broadcasted_iota(jnp.int32, (LOCAL_ROWS, TOK_TILE), 0).astype(jnp.float32)
    sel = jnp.where((prow == info_t[0:1, :]) | (prow == info_t[1:2, :]), 1.0, 0.0).astype(jnp.bfloat16)
    moved = jnp.dot(sel, jnp.concatenate([xn, aux.astype(jnp.bfloat16)], axis=1),
                    preferred_element_type=jnp.float32)
    xs_ref[...] = moved.astype(jnp.bfloat16)


def _mixer(layer, x2, qp, kv, layer_params, shared_tables, *, batch, seq_len):
    nab, sinkcol, poolw, pools, wout, g2, rw, rb = layer_params
    swb, tri, utri = shared_tables
    T = x2.shape[0]
    nblk = seq_len // TOK_TILE
    n_tok_tiles = T // TOK_TILE

    def cur(b, i):
        return (b * nblk + i, 0)

    def prev(b, i):
        return (b * nblk + jnp.maximum(i - 1, 0), 0)

    def nxt(b, i):
        return (b * nblk + jnp.minimum(i + 1, nblk - 1), 0)

    def resident(a):
        zeros = (0,) * a.ndim
        return pl.BlockSpec(a.shape, lambda b, i: zeros, pipeline_mode=pl.Buffered(1))

    def resident_layer(a):
        index = (layer,) + (0,) * (a.ndim - 1)
        return pl.BlockSpec((None,) + a.shape[1:], lambda b, i: index, pipeline_mode=pl.Buffered(1))

    return pl.pallas_call(
        functools.partial(_mixer_kernel, seq_len=seq_len),
        grid=(batch, nblk),
        in_specs=[
            pl.BlockSpec((TOK_TILE, D_MODEL), cur),
            pl.BlockSpec((TOK_TILE, D_QP), cur),
            pl.BlockSpec((TOK_TILE, D_KV), prev),
            pl.BlockSpec((TOK_TILE, D_KV), cur),
            pl.BlockSpec((TOK_TILE, D_KV), nxt),
            resident_layer(nab), resident(swb), resident_layer(sinkcol), resident_layer(poolw),
            resident_layer(pools), resident_layer(wout), resident_layer(g2), resident_layer(rw),
            resident_layer(rb), resident(tri), resident(utri),
        ],
        out_specs=[
            pl.BlockSpec((TOK_TILE, D_MODEL), cur),
            pl.BlockSpec((LOCAL_ROWS, D_SLOT), cur),
            pl.BlockSpec((TOK_TILE, LANES), cur),
            pl.BlockSpec((SUBLANES, LANES), cur),
        ],
        out_shape=[
            jax.ShapeDtypeStruct((T, D_MODEL), jnp.float32),
            jax.ShapeDtypeStruct((n_tok_tiles * LOCAL_ROWS, D_SLOT), jnp.bfloat16),
            jax.ShapeDtypeStruct((T, LANES), jnp.float32),
            jax.ShapeDtypeStruct((n_tok_tiles * SUBLANES, LANES), jnp.float32),
        ],
        scratch_shapes=[
            pltpu.VMEM((3 * TOK_TILE, KW_COLS), jnp.bfloat16),
            pltpu.VMEM((TOK_TILE + 2 * HALO, D_C), jnp.float32),
            pltpu.VMEM((TOK_TILE, D_MIX), jnp.bfloat16),
        ],
        compiler_params=pltpu.CompilerParams(
            dimension_semantics=("arbitrary", "arbitrary"), vmem_limit_bytes=VMEM_LIMIT),
        name="mixer",
    )(x2, qp, kv, kv, kv, nab, swb, sinkcol, poolw, pools, wout, g2, rw, rb, tri, utri)


def _chunk_copy(src_hbm, src_chunk, dst, dst_chunk, sem):
    return pltpu.make_async_copy(
        src_hbm.at[pl.ds(pl.multiple_of(src_chunk * CHUNK, CHUNK), CHUNK)],
        dst.at[pl.ds(dst_chunk * CHUNK, CHUNK)],
        sem)


def _expert_kernel(te_ref, nsub_ref, first_ref, wslot_ref, nexte_ref, csrc_ref,
                   xs_hbm, wg_hbm, wu_hbm, wd_hbm, y_ref,
                   xbuf, wgf, wuf, wdf, wgb, wub, wdb, sem, wsem, *, layer):
    j = pl.program_id(0)
    nt = pl.num_programs(0)
    slot = j % 2

    def start_gather(tile, s, h):
        for c in range(h * SUB_CHUNKS, (h + 1) * SUB_CHUNKS):
            _chunk_copy(xs_hbm, csrc_ref[tile * TILE_CHUNKS + c], xbuf.at[s], c, sem.at[s, h]).start()

    def wait_gather(s, h):
        rows = pl.ds(h * EXP_SUB, EXP_SUB)
        pltpu.make_async_copy(xs_hbm.at[pl.ds(0, EXP_SUB)], xbuf.at[s, rows], sem.at[s, h]).wait()

    def weight_copies(expert, ws):
        return [pltpu.make_async_copy(w_hbm.at[layer, expert], wbuf.at[ws], wsem.at[k, ws])
                for k, (w_hbm, wbuf) in enumerate(((wg_hbm, wgf), (wu_hbm, wuf), (wd_hbm, wdf)))]

    for h in range(EXP_SUBS):
        @pl.when((j == 0) & (h < nsub_ref[0]))
        def _():
            start_gather(0, 0, h)

        @pl.when(h < nsub_ref[jnp.minimum(j + 1, nt - 1)] * (j + 1 < nt).astype(jnp.int32))
        def _():
            start_gather(j + 1, 1 - slot, h)

    ws = wslot_ref[j]

    @pl.when(j == 0)
    def _():
        for cp in weight_copies(te_ref[0], 0):
            cp.start()

    @pl.when(first_ref[j] > 0)
    def _():
        for cp in weight_copies(te_ref[j], ws):
            cp.wait()

        @pl.when(nexte_ref[j] >= 0)
        def _():
            for cp in weight_copies(nexte_ref[j], 1 - ws):
                cp.start()

        wgb[...] = wgf[ws].astype(jnp.bfloat16)
        wub[...] = wuf[ws].astype(jnp.bfloat16)
        wdb[...] = wdf[ws].astype(jnp.bfloat16)

    def gated_mlp(n_rows):
        xs = xbuf[slot, 0:n_rows, 0:D_MODEL]
        gp = xbuf[slot, 0:n_rows, D_MODEL:D_SLOT].astype(jnp.float32)
        first = gp[:, 6:7] == (te_ref[j] + ROUTE_LANE0).astype(jnp.float32)
        gate_w = jnp.where(first, gp[:, 0:1] + gp[:, 1:2] + gp[:, 2:3], gp[:, 3:4] + gp[:, 4:5] + gp[:, 5:6])
        gate = jnp.dot(xs, wgb[...], preferred_element_type=jnp.float32)
        up = jnp.dot(xs, wub[...], preferred_element_type=jnp.float32)
        act = (gate * (1.0 / (1.0 + jnp.exp(-gate))) * up * gate_w).astype(jnp.bfloat16)
        return jnp.dot(act, wdb[...], preferred_element_type=jnp.float32).astype(jnp.bfloat16)

    for k in range(1, EXP_SUBS + 1):
        @pl.when(nsub_ref[j] == k)
        def _():
            for h in range(k):
                wait_gather(slot, h)
            y_ref[0:k * EXP_SUB, :] = gated_mlp(k * EXP_SUB)
            if k < EXP_SUBS:
                y_ref[k * EXP_SUB:EXP_TILE, :] = jnp.zeros((EXP_TILE - k * EXP_SUB, D_MODEL), jnp.bfloat16)

    @pl.when(nsub_ref[j] == 0)
    def _():
        y_ref[...] = jnp.zeros_like(y_ref)


def _experts(layer, tile_tables, chunk_src, xs_local, wg, wu, wd):
    tile_expert, n_sub, first, wslot, next_expert = tile_tables
    n_tiles = tile_expert.shape[0]
    any_space = pl.BlockSpec(memory_space=pl.ANY)

    return pl.pallas_call(
        functools.partial(_expert_kernel, layer=layer),
        grid_spec=pltpu.PrefetchScalarGridSpec(
            num_scalar_prefetch=6,
            grid=(n_tiles,),
            in_specs=[any_space] * 4,
            out_specs=pl.BlockSpec((EXP_TILE, D_MODEL), lambda j, *tables: (j, 0)),
            scratch_shapes=[
                pltpu.VMEM((2, EXP_TILE, D_SLOT), jnp.bfloat16),
                pltpu.VMEM((2, D_MODEL, D_EXPERT), jnp.float32),
                pltpu.VMEM((2, D_MODEL, D_EXPERT), jnp.float32),
                pltpu.VMEM((2, D_EXPERT, D_MODEL), jnp.float32),
                pltpu.VMEM((D_MODEL, D_EXPERT), jnp.bfloat16),
                pltpu.VMEM((D_MODEL, D_EXPERT), jnp.bfloat16),
                pltpu.VMEM((D_EXPERT, D_MODEL), jnp.bfloat16),
                pltpu.SemaphoreType.DMA((2, EXP_SUBS)),
                pltpu.SemaphoreType.DMA((3, 2)),
            ],
        ),
        out_shape=jax.ShapeDtypeStruct((n_tiles * EXP_TILE, D_MODEL), jnp.bfloat16),
        compiler_params=pltpu.CompilerParams(
            dimension_semantics=("arbitrary",), vmem_limit_bytes=VMEM_LIMIT),
        name="experts",
    )(tile_expert, n_sub, first, wslot, next_expert, chunk_src, xs_local, wg, wu, wd)


def _combine_kernel(ctab_ref, xmid_ref, rinfo_ref, g_ref, ys_hbm, out_ref, ybuf, sem, *, final_norm):
    i = pl.program_id(0)
    nt = pl.num_programs(0)
    slot = i % 2

    def start_gather(tile, s):
        for c in range(LOCAL_CHUNKS):
            _chunk_copy(ys_hbm, ctab_ref[tile * LOCAL_CHUNKS + c], ybuf.at[s], c, sem.at[s]).start()

    def wait_gather(s):
        pltpu.make_async_copy(ys_hbm.at[pl.ds(0, LOCAL_ROWS)], ybuf.at[s], sem.at[s]).wait()

    @pl.when(i == 0)
    def _():
        start_gather(0, 0)

    @pl.when(i + 1 < nt)
    def _():
        start_gather(i + 1, 1 - slot)

    wait_gather(slot)
    info = rinfo_ref[...]
    pcol = lax.broadcasted_iota(jnp.int32, (TOK_TILE, LOCAL_ROWS), 1).astype(jnp.float32)
    pick = jnp.where((pcol == info[:, 0:1]) | (pcol == info[:, 1:2]), 1.0, 0.0).astype(jnp.bfloat16)
    out = xmid_ref[...] + jnp.dot(pick, ybuf[slot], preferred_element_type=jnp.float32)
    if final_norm:
        out = _rmsnorm_f32(out, g_ref[...])
    out_ref[...] = out


def _combine(chunk_tab, xmid, rinfo, g, ys, *, final_norm):
    T = xmid.shape[0]
    return pl.pallas_call(
        functools.partial(_combine_kernel, final_norm=final_norm),
        grid_spec=pltpu.PrefetchScalarGridSpec(
            num_scalar_prefetch=1,
            grid=(T // TOK_TILE,),
            in_specs=[
                pl.BlockSpec((TOK_TILE, D_MODEL), lambda i, ct: (i, 0)),
                pl.BlockSpec((TOK_TILE, LANES), lambda i, ct: (i, 0)),
                pl.BlockSpec((1, D_MODEL), lambda i, ct: (0, 0)),
                pl.BlockSpec(memory_space=pl.ANY),
            ],
            out_specs=pl.BlockSpec((TOK_TILE, D_MODEL), lambda i, ct: (i, 0)),
            scratch_shapes=[
                pltpu.VMEM((2, LOCAL_ROWS, D_MODEL), jnp.bfloat16),
                pltpu.SemaphoreType.DMA((2,)),
            ],
        ),
        out_shape=jax.ShapeDtypeStruct((T, D_MODEL), jnp.float32),
        compiler_params=pltpu.CompilerParams(
            dimension_semantics=("arbitrary",), vmem_limit_bytes=VMEM_LIMIT),
        name="combine",
    )(chunk_tab, xmid, rinfo, g, ys)


def _pair_heads(a, axis):
    shape = a.shape
    split = shape[:axis] + (SWA_KV_HEADS, SWA_REP, HEAD_DIM) + shape[axis + 1:]
    return jnp.swapaxes(a.reshape(split), axis, axis + 1).reshape(shape)


def _in_proj_weight(w):
    off_bq = 3 * D_A
    return jnp.concatenate(
        [w[..., 0:D_A], _pair_heads(w[..., off_bq:off_bq + D_B], w.ndim - 1), w[..., D_A:off_bq],
         w[..., off_bq + D_B:]], axis=-1).astype(jnp.bfloat16)


def _out_proj_weight(w):
    return jnp.concatenate(
        [w[:, 0:D_A], _pair_heads(w[:, D_A:D_A + D_B], 1), w[:, D_A + D_B:]], axis=1).astype(jnp.bfloat16)


def _na_bias_table(rel_bias):
    c = np.arange(GRID_W)[:, None]
    cp = np.arange(GRID_W)[None, :]
    cs = np.clip(c - NA_COLS // 2, 0, GRID_W - NA_COLS)
    valid = (cp >= cs) & (cp < cs + NA_COLS)
    d = np.arange(2 * NA_COLS - 1)[:, None, None]
    col_sel = ((cp - c + (NA_COLS - 1))[None] == d) & valid[None]
    k = np.arange(NA_ROWS)[:, None, None]
    j = np.arange(NA_ROWS)[None, :, None]
    r = np.arange(2 * NA_ROWS - 1)[None, None, :]
    row_sel = r == j - k + (NA_ROWS - 1)
    tab = jnp.einsum("kjr,lhrd,dcm->lkhcjm", jnp.asarray(row_sel, jnp.float32), rel_bias.astype(jnp.float32),
                     jnp.asarray(col_sel, jnp.float32), precision=lax.Precision.HIGHEST)
    tab = jnp.where(jnp.asarray(valid)[None, None, None, :, None, :], tab, NEG)
    return tab.reshape(rel_bias.shape[0], NA_ROWS, NA_HEADS * GRID_W, NA_ROWS * GRID_W)


def _swa_bias_table():
    slopes = (2.0 ** (-8.0 * np.arange(1, SWA_Q_HEADS + 1) / SWA_Q_HEADS)).astype(np.float32)
    qi = np.arange(SWA_BLOCK)[:, None]
    ki = np.arange(3 * SWA_BLOCK)[None, :]
    dist = np.abs(ki - qi - SWA_BLOCK).astype(np.float32)
    tab = np.where(dist <= SWA_WINDOW, -slopes[:, None, None] * dist[None], np.float32(NEG))
    tab = tab.reshape(SWA_Q_HEADS * SWA_BLOCK, 3 * SWA_BLOCK).astype(np.float32)
    first, last = tab.copy(), tab.copy()
    first[:, :SWA_BLOCK] = NEG
    last[:, 2 * SWA_BLOCK:] = NEG
    return jnp.asarray(np.stack([first, tab, last]))


def _block_diag(pool_w):
    depth, n = pool_w.shape[0:2]
    eye = jnp.asarray(np.eye(n, dtype=np.float32))
    out = pool_w[:, :, :, None, :] * eye[None, :, None, :, None]
    return out.reshape(depth, n * POOL_GROUP_DIM, n * POOL_GROUP_DIM)


def _router_weights(rg_w, rg_b, re_w, re_b):
    def lanes(g, e):
        gap = jnp.zeros(g.shape[:-1] + (ROUTE_LANE0 - N_GROUPS,), jnp.float32)
        tail = jnp.zeros(g.shape[:-1] + (LANES - ROUTE_LANE0 - N_EXPERTS,), jnp.float32)
        return jnp.concatenate([g.astype(jnp.float32), gap, e.astype(jnp.float32), tail], axis=-1)

    return lanes(rg_w, re_w).astype(jnp.bfloat16), lanes(rg_b, re_b)[:, None, :]


def _dispatch_tables(cnt, n_tiles):
    nb = cnt.shape[0] // SUBLANES
    n = cnt.reshape(nb, SUBLANES, LANES)[:, 0, ROUTE_LANE0:ROUTE_LANE0 + N_EXPERTS].astype(jnp.int32)
    g = (n + (CHUNK - 1)) // CHUNK
    l_end = jnp.cumsum(g, axis=1)
    l_off = l_end - g
    c_end = jnp.cumsum(g, axis=0)
    c_off = c_end - g
    tot = c_end[-1]
    tiles = (tot + (TILE_CHUNKS - 1)) // TILE_CHUNKS
    t_end = jnp.cumsum(tiles)
    t_off = t_end - tiles
    n_used = t_end[-1:]

    experts = jnp.arange(N_EXPERTS, dtype=jnp.int32)
    tile_ids = jnp.arange(n_tiles, dtype=jnp.int32)
    tile_expert = jnp.minimum(jnp.sum((t_end[None, :] <= tile_ids[:, None]).astype(jnp.int32), axis=1),
                              N_EXPERTS - 1)
    oh_te = (tile_expert[:, None] == experts[None, :]).astype(jnp.int32)
    left = jnp.sum(oh_te * (tot + t_off * TILE_CHUNKS)[None, :], axis=1) - tile_ids * TILE_CHUNKS
    n_sub = jnp.clip((left + (SUB_CHUNKS - 1)) // SUB_CHUNKS, 0, EXP_SUBS)
    has_rows = tiles > 0
    first = ((tile_ids == jnp.sum(oh_te * t_off[None, :], axis=1)) & (n_sub > 0)).astype(jnp.int32)
    wslot = jnp.sum(oh_te * ((jnp.cumsum(has_rows.astype(jnp.int32)) - 1) % 2)[None, :], axis=1)
    later = (experts[None, :] > experts[:, None]) & has_rows[None, :]
    nxt = jnp.min(jnp.where(later, experts[None, :], N_EXPERTS), axis=1)
    next_expert = jnp.sum(oh_te * jnp.where(nxt < N_EXPERTS, nxt, -1)[None, :], axis=1)
    tile_tables = (tile_expert, n_sub, first, wslot, next_expert)

    q = jnp.arange(n_tiles * TILE_CHUNKS, dtype=jnp.int32)
    tile_q = q // TILE_CHUNKS
    oh_e = (jnp.repeat(tile_expert, TILE_CHUNKS)[:, None] == experts[None, :]).astype(jnp.int32)
    ro = q - jnp.sum(oh_e * t_off[None, :], axis=1) * TILE_CHUNKS
    valid = (ro < jnp.sum(oh_e * tot[None, :], axis=1)) & (tile_q < n_used[0])
    cols = jnp.dot(jnp.concatenate([c_end, c_off, l_off], axis=0).astype(jnp.float32),
                   oh_e.T.astype(jnp.float32), precision=lax.Precision.HIGHEST).astype(jnp.int32)
    c_end_q, c_off_q, l_off_q = cols[0:nb], cols[nb:2 * nb], cols[2 * nb:3 * nb]
    b_q = jnp.minimum(jnp.sum((c_end_q <= ro[None, :]).astype(jnp.int32), axis=0), nb - 1)
    oh_b = (jnp.arange(nb, dtype=jnp.int32)[:, None] == b_q[None, :]).astype(jnp.int32)
    src = b_q * LOCAL_CHUNKS + jnp.sum(oh_b * (l_off_q + ro[None, :] - c_off_q), axis=0)
    chunk_src = jnp.where(valid, src, LOCAL_CHUNKS - 1)

    c = jnp.arange(LOCAL_CHUNKS, dtype=jnp.int32)
    e_c = jnp.minimum(jnp.sum((l_end[:, None, :] <= c[None, :, None]).astype(jnp.int32), axis=2),
                      N_EXPERTS - 1)
    oh_ec = (e_c[:, :, None] == experts[None, None, :]).astype(jnp.int32)
    pos = (jnp.sum(oh_ec * (t_off[None, None, :] * TILE_CHUNKS + c_off[:, None, :] - l_off[:, None, :]), axis=2)
           + c[None, :])
    chunk_tab = jnp.where(c[None, :] < l_end[:, -1:], pos, 0).reshape(-1)
    return tile_tables, chunk_src, chunk_tab


def kernel(x, norm1_g, w_in, nat_bias, swa_sink, pool_w, pool_scale, w_out, norm2_g, router_g_w,
           router_g_b, router_e_w, router_e_b, expert_w_gate, expert_w_up, expert_w_down, final_g):
    batch, seq_len, _ = x.shape
    depth = w_in.shape[0]
    T = batch * seq_len
    assert seq_len % TOK_TILE == 0 and TOK_TILE % SWA_BLOCK == 0 and TOK_TILE % GRID_W == 0
    max_chunks = (2 * T) // CHUNK + (T // TOK_TILE) * N_EXPERTS
    n_tiles = max_chunks // TILE_CHUNKS + N_EXPERTS

    swb = _swa_bias_table()
    tri = jnp.asarray(np.tril(np.ones((TOK_TILE, TOK_TILE), np.float32), -1)).astype(jnp.bfloat16)
    utri = jnp.asarray(np.triu(np.ones((LANES, LANES), np.float32), 1)).astype(jnp.bfloat16)

    w_in_k = _in_proj_weight(w_in)
    w_out_k = _out_proj_weight(w_out)
    nab = _na_bias_table(nat_bias)
    sinkcol = jnp.broadcast_to(swa_sink.astype(jnp.float32)[:, :, None, None],
                               (depth, SWA_Q_HEADS, SWA_BLOCK, LANES)).reshape(depth, SWA_Q_HEADS * SWA_BLOCK, LANES)
    poolw = _block_diag(pool_w).astype(jnp.bfloat16)
    pools = pool_scale.reshape(depth, 1, D_C).astype(jnp.float32)
    rw, rb = _router_weights(router_g_w, router_g_b, router_e_w, router_e_b)
    g1 = norm1_g.reshape(depth, 1, D_MODEL)
    g2 = norm2_g.reshape(depth, 1, D_MODEL)

    x2 = x.reshape(T, D_MODEL)
    for l in range(depth):
        qp, kv = _norm_proj(l, x2, g1, w_in_k)
        xmid, xs_local, rinfo, cnt = _mixer(
            l, x2, qp, kv, (nab, sinkcol, poolw, pools, w_out_k, g2, rw, rb), (swb, tri, utri),
            batch=batch, seq_len=seq_len)
        tile_tables, chunk_src, chunk_tab = _dispatch_tables(cnt, n_tiles)
        ys = _experts(l, tile_tables, chunk_src, xs_local,
                      expert_w_gate, expert_w_up, expert_w_down)
        x2 = _combine(chunk_tab, xmid, rinfo, final_g.reshape(1, D_MODEL), ys, final_norm=(l == depth - 1))
    return x2.reshape(batch, seq_len, D_MODEL)
```

```python
import functools

import jax
import jax.numpy as jnp
import numpy as np
from jax import lax
from jax.experimental import pallas as pl
from jax.experimental.pallas import tpu as pltpu

D_MODEL = 1024
GRID_W = 64
HEAD_DIM = 64
NA_HEADS = 4
NA_ROWS = 8
NA_COLS = 16
SWA_Q_HEADS = 8
SWA_KV_HEADS = 2
SWA_REP = SWA_Q_HEADS // SWA_KV_HEADS
SWA_WINDOW = 128
SWA_BLOCK = 128
POOL_WINDOWS = (2, 4, 8, 16)
POOL_GROUP_DIM = 64
D_A = NA_HEADS * HEAD_DIM
D_B = SWA_Q_HEADS * HEAD_DIM
D_BKV = SWA_KV_HEADS * HEAD_DIM
D_C = len(POOL_WINDOWS) * POOL_GROUP_DIM
D_MIX = D_A + D_B + D_C
D_QP = D_A + D_B
D_KV = 2 * D_A + 2 * D_BKV + D_C
D_IN = D_QP + D_KV
N_GROUPS = 4
EXPERTS_PER_GROUP = 8
N_EXPERTS = N_GROUPS * EXPERTS_PER_GROUP
D_EXPERT = 256
RMS_EPS = 1e-6
NEG = -1e30

LANES = 128
SUBLANES = 8
ROW_CHUNKS = D_MODEL // LANES

TOK_TILE = 512
EXP_SUB = 256
EXP_SUBS = 2
EXP_TILE = EXP_SUB * EXP_SUBS
CHUNK = 16
LOCAL_CHUNKS = (2 * TOK_TILE + N_EXPERTS * (CHUNK - 1)) // CHUNK + 2
LOCAL_ROWS = LOCAL_CHUNKS * CHUNK
SUB_CHUNKS = EXP_SUB // CHUNK
TILE_CHUNKS = EXP_TILE // CHUNK
ROUTE_LANE0 = 8
D_SLOT = D_MODEL + LANES
NA_ROWS_PER_STEP = 8
SWA_BLOCKS_PER_STEP = 2
HALO = 8
VMEM_LIMIT = 56 * 1024 * 1024


def _rmsnorm_f32(x, g):
    return x * lax.rsqrt(jnp.mean(x * x, axis=-1, keepdims=True) + RMS_EPS) * g


def _norm_proj_kernel(x_ref, g_ref, w_ref, qp_ref, kv_ref):
    xn = _rmsnorm_f32(x_ref[...], g_ref[...]).astype(jnp.bfloat16)
    proj = jnp.dot(xn, w_ref[...], preferred_element_type=jnp.float32)
    qp_ref[...] = proj[:, :D_QP].astype(jnp.bfloat16)
    kv_ref[...] = proj[:, D_QP:].astype(jnp.bfloat16)


def _norm_proj(layer, x2, g, w):
    T = x2.shape[0]
    return pl.pallas_call(
        _norm_proj_kernel,
        grid=(T // TOK_TILE,),
        in_specs=[
            pl.BlockSpec((TOK_TILE, D_MODEL), lambda i: (i, 0)),
            pl.BlockSpec((None, 1, D_MODEL), lambda i: (layer, 0, 0)),
            pl.BlockSpec((None, D_MODEL, D_IN), lambda i: (layer, 0, 0)),
        ],
        out_specs=[
            pl.BlockSpec((TOK_TILE, D_QP), lambda i: (i, 0)),
            pl.BlockSpec((TOK_TILE, D_KV), lambda i: (i, 0)),
        ],
        out_shape=[
            jax.ShapeDtypeStruct((T, D_QP), jnp.bfloat16),
            jax.ShapeDtypeStruct((T, D_KV), jnp.bfloat16),
        ],
        compiler_params=pltpu.CompilerParams(
            dimension_semantics=("arbitrary",), vmem_limit_bytes=VMEM_LIMIT),
        name="norm_proj",
    )(x2, g, w)


KW_AK, KW_AV, KW_BK, KW_BV = 0, D_A, 2 * D_A, 2 * D_A + D_BKV
KW_COLS = 2 * D_A + 2 * D_BKV
KV_CU = KW_COLS


def _mixer_kernel(x_ref, qp_ref, kvp_ref, kvc_ref, kvn_ref, nab_ref, swb_ref, sink_ref,
                  poolw_ref, pools_ref, wout_ref, g2_ref, rw_ref, rb_ref, tri_ref, utri_ref,
                  xmid_ref, xs_ref, rinfo_ref, cnt_ref,
                  kwin, uwin, mix, xn_scr, logit_scr, *, seq_len):
    t = pl.program_id(0)
    nblk = seq_len // TOK_TILE
    i = jnp.minimum(t, pl.num_programs(0) - 2) % nblk
    rows_per_tile = TOK_TILE // GRID_W
    grid_rows = seq_len // GRID_W

    @pl.when(t == 0)
    def _():
        xn_scr[...] = jnp.zeros_like(xn_scr)
        logit_scr[...] = jnp.zeros_like(logit_scr)

    kwin[0:TOK_TILE, :] = kvp_ref[:, 0:KW_COLS]
    kwin[TOK_TILE:2 * TOK_TILE, :] = kvc_ref[:, 0:KW_COLS]
    kwin[2 * TOK_TILE:3 * TOK_TILE, :] = kvn_ref[:, 0:KW_COLS]

    lane_a = lax.broadcasted_iota(jnp.int32, (GRID_W, D_A), 1) // HEAD_DIM

    def na_row(rr):
        r = i * rows_per_tile + rr
        rs = jnp.clip(r - NA_ROWS // 2, 0, grid_rows - NA_ROWS)
        variant = r - rs
        start = pl.multiple_of((rs - i * rows_per_tile + rows_per_tile) * GRID_W, GRID_W)
        q0 = pl.multiple_of(rr * GRID_W, GRID_W)
        q = qp_ref[pl.ds(q0, GRID_W), 0:D_A] * jnp.bfloat16(HEAD_DIM ** -0.5)
        zero = jnp.zeros_like(q)
        qs = jnp.concatenate([jnp.where(lane_a == h, q, zero) for h in range(NA_HEADS)], axis=0)
        kw = kwin[pl.ds(start, NA_ROWS * GRID_W), KW_AK:KW_AK + D_A]
        vw = kwin[pl.ds(start, NA_ROWS * GRID_W), KW_AV:KW_AV + D_A]
        s = lax.dot_general(qs, kw, (((1,), (1,)), ((), ())), preferred_element_type=jnp.float32)
        s = s + nab_ref[variant]
        m = jnp.max(s, axis=-1, keepdims=True)
        p = jnp.exp(s - m)
        l = jnp.sum(p, axis=-1, keepdims=True)
        pv = jnp.dot(p.astype(jnp.bfloat16), vw, preferred_element_type=jnp.float32)
        pv = pv * (1.0 / l)
        o = jnp.zeros((GRID_W, D_A), jnp.float32)
        for h in range(NA_HEADS):
            o = o + jnp.where(lane_a == h, pv[h * GRID_W:(h + 1) * GRID_W, :], 0.0)
        mix[pl.ds(q0, GRID_W), 0:D_A] = o.astype(jnp.bfloat16)

    def na_step(it, c):
        for k in range(NA_ROWS_PER_STEP):
            na_row(it * NA_ROWS_PER_STEP + k)
        return c

    lax.fori_loop(0, rows_per_tile // NA_ROWS_PER_STEP, na_step, 0)

    lane_b = lax.broadcasted_iota(jnp.int32, (SWA_BLOCK, LANES), 1) // HEAD_DIM
    blocks_per_tile = TOK_TILE // SWA_BLOCK
    nblocks = seq_len // SWA_BLOCK

    ones_v = jnp.ones((3 * SWA_BLOCK, LANES), jnp.bfloat16)

    def swa_step(sb, c):
        n = i * blocks_per_tile + sb
        variant = jnp.where(n == 0, 0, jnp.where(n == nblocks - 1, 2, 1))
        q0 = pl.multiple_of(sb * SWA_BLOCK, SWA_BLOCK)
        k0 = pl.multiple_of(TOK_TILE - SWA_BLOCK + sb * SWA_BLOCK, SWA_BLOCK)
        kw = kwin[pl.ds(k0, 3 * SWA_BLOCK), KW_BK:KW_BK + D_BKV]
        vaug = jnp.concatenate([kwin[pl.ds(k0, 3 * SWA_BLOCK), KW_BV:KW_BV + D_BKV], ones_v], axis=1)
        outs = []
        for g in range(SWA_KV_HEADS):
            pieces = []
            for t in range(SWA_REP):
                qt = qp_ref[pl.ds(q0, SWA_BLOCK), D_A + t * LANES:D_A + (t + 1) * LANES]
                qt = qt * jnp.bfloat16(HEAD_DIM ** -0.5)
                pieces.append(jnp.where(lane_b == g, qt, jnp.zeros_like(qt)))
            qs = jnp.concatenate(pieces, axis=0)
            r0 = g * SWA_REP * SWA_BLOCK
            s = lax.dot_general(qs, kw, (((1,), (1,)), ((), ())), preferred_element_type=jnp.float32)
            s = s + swb_ref[variant, r0:r0 + SWA_REP * SWA_BLOCK, :]
            sink = sink_ref[r0:r0 + SWA_REP * SWA_BLOCK, :]
            m = jnp.broadcast_to(jnp.max(s, axis=-1, keepdims=True), sink.shape)
            m = jnp.maximum(m, sink)
            p = jnp.exp(s - jnp.concatenate([m, m, m], axis=1)).astype(jnp.bfloat16)
            pv = jnp.dot(p, vaug, preferred_element_type=jnp.float32)
            l = pv[:, LANES:2 * LANES] + jnp.exp(sink - m)
            outs.append(pv[:, 0:LANES] * (1.0 / l))
        for t in range(SWA_REP):
            o0 = outs[0][t * SWA_BLOCK:(t + 1) * SWA_BLOCK, :]
            o1 = outs[1][t * SWA_BLOCK:(t + 1) * SWA_BLOCK, :]
            ot = jnp.where(lane_b == 0, o0, o1)
            mix[pl.ds(q0, SWA_BLOCK), D_A + t * LANES:D_A + (t + 1) * LANES] = ot.astype(jnp.bfloat16)
        return c

    def swa_pair(it, c):
        for k in range(SWA_BLOCKS_PER_STEP):
            swa_step(it * SWA_BLOCKS_PER_STEP + k, c)
        return c

    lax.fori_loop(0, blocks_per_tile // SWA_BLOCKS_PER_STEP, swa_pair, 0)

    u = kvc_ref[:, KV_CU:KV_CU + D_C].astype(jnp.float32)
    prev_ok = (i > 0).astype(jnp.float32)
    next_ok = (i < nblk - 1).astype(jnp.float32)
    uwin[0:HALO, :] = kvp_ref[TOK_TILE - HALO:TOK_TILE, KV_CU:KV_CU + D_C].astype(jnp.float32) * prev_ok
    uwin[HALO:HALO + TOK_TILE, :] = u
    uwin[HALO + TOK_TILE:2 * HALO + TOK_TILE, :] = kvn_ref[0:HALO, KV_CU:KV_CU + D_C].astype(jnp.float32) * next_ok
    n_ext = TOK_TILE + 2 * HALO
    a2 = uwin[0:n_ext - 1, :] + uwin[1:n_ext, :]
    a4 = a2[0:n_ext - 3, :] + a2[2:n_ext - 1, :]
    a8 = a4[0:n_ext - 7, :] + a4[4:n_ext - 3, :]
    a16 = a8[0:n_ext - 15, :] + a8[8:n_ext - 7, :]
    w2 = a2[7:7 + TOK_TILE, :]
    w4 = a4[6:6 + TOK_TILE, :]
    w8 = a8[4:4 + TOK_TILE, :]
    w16 = a16[0:TOK_TILE, :]
    lane_c = lax.broadcasted_iota(jnp.int32, (TOK_TILE, D_C), 1) // POOL_GROUP_DIM
    pooled = jnp.where(lane_c == 0, w2, jnp.where(lane_c == 1, w4, jnp.where(lane_c == 2, w8, w16)))
    half = jnp.where(lane_c == 0, 1, jnp.where(lane_c == 1, 2, jnp.where(lane_c == 2, 4, 8)))
    pos = i * TOK_TILE + lax.broadcasted_iota(jnp.int32, (TOK_TILE, D_C), 0)
    cnt = (jnp.minimum(pos + half, seq_len) - jnp.maximum(pos - half, 0)).astype(jnp.float32)
    d = (pooled / cnt - u).astype(jnp.bfloat16)
    oc = jnp.dot(d, poolw_ref[...], preferred_element_type=jnp.float32) * pools_ref[...]
    mix[:, D_A + D_B:D_MIX] = oc.astype(jnp.bfloat16)

    xn = xn_scr[...]
    logits = logit_scr[...]

    xm = x_ref[...] + jnp.dot(mix[...], wout_ref[...], preferred_element_type=jnp.float32)
    xmid_ref[...] = xm
    xn_new = _rmsnorm_f32(xm, g2_ref[...]).astype(jnp.bfloat16)
    xn_scr[...] = xn_new
    logit_scr[...] = jnp.dot(xn_new, rw_ref[...], preferred_element_type=jnp.float32) + rb_ref[...]

    lane = lax.broadcasted_iota(jnp.int32, (TOK_TILE, LANES), 1).astype(jnp.float32)
    is_g = lane < N_GROUPS
    gl = jnp.where(is_g, logits, NEG)
    gmax = jnp.max(gl, axis=-1, keepdims=True)
    gtop = jnp.min(jnp.where(is_g & (gl == gmax), lane, float(LANES)), axis=-1, keepdims=True)
    gprob = 1.0 / jnp.sum(jnp.exp(gl - gmax), axis=-1, keepdims=True)
    e_lo = ROUTE_LANE0 + gtop * EXPERTS_PER_GROUP
    in_grp = (lane >= e_lo) & (lane < e_lo + EXPERTS_PER_GROUP)
    el = jnp.where(in_grp, logits, NEG)
    m1 = jnp.max(el, axis=-1, keepdims=True)
    i1 = jnp.min(jnp.where(in_grp & (el == m1), lane, float(LANES)), axis=-1, keepdims=True)
    el2 = jnp.where(lane == i1, NEG, el)
    m2 = jnp.max(el2, axis=-1, keepdims=True)
    i2 = jnp.min(jnp.where(in_grp & (lane != i1) & (el2 == m2), lane, float(LANES)), axis=-1, keepdims=True)
    r21 = jnp.exp(m2 - m1)
    gate1 = gprob / (1.0 + r21)
    gate2 = gprob * r21 / (1.0 + r21)

    oh1 = lane == i1
    oh2 = lane == i2
    oh = jnp.where(oh1 | oh2, 1.0, 0.0)
    earlier = jnp.dot(tri_ref[...], oh.astype(jnp.bfloat16), preferred_element_type=jnp.float32)
    n_e = jnp.sum(oh, axis=0, keepdims=True)
    chunks_e = jnp.floor((n_e + (CHUNK - 1)) * (1.0 / CHUNK))
    seg0 = jnp.dot(jnp.broadcast_to(chunks_e, (SUBLANES, LANES)).astype(jnp.bfloat16), utri_ref[...],
                   preferred_element_type=jnp.float32)[0:1, :] * CHUNK
    base = earlier + seg0
    lp1 = jnp.sum(jnp.where(oh1, base, 0.0), axis=-1, keepdims=True)
    lp2 = jnp.sum(jnp.where(oh2, base, 0.0), axis=-1, keepdims=True)
    info = jnp.where(lane == 0, lp1, jnp.where(lane == 1, lp2, 0.0))
    rinfo_ref[...] = info
    cnt_ref[...] = jnp.broadcast_to(n_e, cnt_ref.shape)

    def pieces(g):
        hi = g.astype(jnp.bfloat16).astype(jnp.float32)
        mid = (g - hi).astype(jnp.bfloat16).astype(jnp.float32)
        return hi, mid, g - hi - mid

    aux = jnp.zeros((TOK_TILE, LANES), jnp.float32)
    for k, piece in enumerate(pieces(gate1) + pieces(gate2) + (i1,)):
        aux = jnp.where(lane == k, piece, aux)

    info_t = info.T
    prow = lax.broadcasted_iota(jnp.int32, (LOCAL_ROWS, TOK_TILE), 0).astype(jnp.float32)
    sel = jnp.where((prow == info_t[0:1, :]) | (prow == info_t[1:2, :]), 1.0, 0.0).astype(jnp.bfloat16)
    moved = jnp.dot(sel, jnp.concatenate([xn, aux.astype(jnp.bfloat16)], axis=1),
                    preferred_element_type=jnp.float32)
    xs_ref[...] = moved.astype(jnp.bfloat16)


def _mixer(layer, x2, qp, kv, layer_params, shared_tables, *, batch, seq_len):
    nab, sinkcol, poolw, pools, wout, g2, rw, rb = layer_params
    swb, tri, utri = shared_tables
    T = x2.shape[0]
    nblk = seq_len // TOK_TILE
    n_tok_tiles = T // TOK_TILE

    def mixed(t):
        return jnp.minimum(t, n_tok_tiles - 1)

    def cur(t):
        return (mixed(t), 0)

    def prev(t):
        return (jnp.maximum(mixed(t) - 1, (mixed(t) // nblk) * nblk), 0)

    def nxt(t):
        return (jnp.minimum(mixed(t) + 1, (mixed(t) // nblk) * nblk + nblk - 1), 0)

    def dispatched(t):
        return (jnp.maximum(t - 1, 0), 0)

    def resident(a):
        zeros = (0,) * a.ndim
        return pl.BlockSpec(a.shape, lambda t: zeros, pipeline_mode=pl.Buffered(1))

    def resident_layer(a):
        index = (layer,) + (0,) * (a.ndim - 1)
        return pl.BlockSpec((None,) + a.shape[1:], lambda t: index, pipeline_mode=pl.Buffered(1))

    return pl.pallas_call(
        functools.partial(_mixer_kernel, seq_len=seq_len),
        grid=(n_tok_tiles + 1,),
        in_specs=[
            pl.BlockSpec((TOK_TILE, D_MODEL), cur),
            pl.BlockSpec((TOK_TILE, D_QP), cur),
            pl.BlockSpec((TOK_TILE, D_KV), prev),
            pl.BlockSpec((TOK_TILE, D_KV), cur),
            pl.BlockSpec((TOK_TILE, D_KV), nxt),
            resident_layer(nab), resident(swb), resident_layer(sinkcol), resident_layer(poolw),
            resident_layer(pools), resident_layer(wout), resident_layer(g2), resident_layer(rw),
            resident_layer(rb), resident(tri), resident(utri),
        ],
        out_specs=[
            pl.BlockSpec((TOK_TILE, D_MODEL), cur),
            pl.BlockSpec((LOCAL_ROWS, D_SLOT), dispatched),
            pl.BlockSpec((TOK_TILE, LANES), dispatched),
            pl.BlockSpec((SUBLANES, LANES), dispatched),
        ],
        out_shape=[
            jax.ShapeDtypeStruct((T, D_MODEL), jnp.float32),
            jax.ShapeDtypeStruct((n_tok_tiles * LOCAL_ROWS, D_SLOT), jnp.bfloat16),
            jax.ShapeDtypeStruct((T, LANES), jnp.float32),
            jax.ShapeDtypeStruct((n_tok_tiles * SUBLANES, LANES), jnp.float32),
        ],
        scratch_shapes=[
            pltpu.VMEM((3 * TOK_TILE, KW_COLS), jnp.bfloat16),
            pltpu.VMEM((TOK_TILE + 2 * HALO, D_C), jnp.float32),
            pltpu.VMEM((TOK_TILE, D_MIX), jnp.bfloat16),
            pltpu.VMEM((TOK_TILE, D_MODEL), jnp.bfloat16),
            pltpu.VMEM((TOK_TILE, LANES), jnp.float32),
        ],
        compiler_params=pltpu.CompilerParams(
            dimension_semantics=("arbitrary",), vmem_limit_bytes=VMEM_LIMIT),
        name="mixer",
    )(x2, qp, kv, kv, kv, nab, swb, sinkcol, poolw, pools, wout, g2, rw, rb, tri, utri)


def _chunk_copy(src_hbm, src_chunk, dst, dst_chunk, sem):
    return pltpu.make_async_copy(
        src_hbm.at[pl.ds(pl.multiple_of(src_chunk * CHUNK, CHUNK), CHUNK)],
        dst.at[pl.ds(dst_chunk * CHUNK, CHUNK)],
        sem)


def _expert_kernel(te_ref, nsub_ref, first_ref, wslot_ref, nexte_ref, csrc_ref,
                   xs_hbm, wg_hbm, wu_hbm, wd_hbm, y_ref,
                   xbuf, wgf, wuf, wdf, wgb, wub, wdb, sem, wsem, *, layer):
    j = pl.program_id(0)
    nt = pl.num_programs(0)
    slot = j % 2

    def start_gather(tile, s, h):
        for c in range(h * SUB_CHUNKS, (h + 1) * SUB_CHUNKS):
            _chunk_copy(xs_hbm, csrc_ref[tile * TILE_CHUNKS + c], xbuf.at[s], c, sem.at[s, h]).start()

    def wait_gather(s, h):
        rows = pl.ds(h * EXP_SUB, EXP_SUB)
        pltpu.make_async_copy(xs_hbm.at[pl.ds(0, EXP_SUB)], xbuf.at[s, rows], sem.at[s, h]).wait()

    def weight_copies(expert, ws):
        return [pltpu.make_async_copy(w_hbm.at[layer, expert], wbuf.at[ws], wsem.at[k, ws])
                for k, (w_hbm, wbuf) in enumerate(((wg_hbm, wgf), (wu_hbm, wuf), (wd_hbm, wdf)))]

    for h in range(EXP_SUBS):
        @pl.when((j == 0) & (h < nsub_ref[0]))
        def _():
            start_gather(0, 0, h)

        @pl.when(h < nsub_ref[jnp.minimum(j + 1, nt - 1)] * (j + 1 < nt).astype(jnp.int32))
        def _():
            start_gather(j + 1, 1 - slot, h)

    ws = wslot_ref[j]

    @pl.when(j == 0)
    def _():
        for cp in weight_copies(te_ref[0], 0):
            cp.start()

    @pl.when(first_ref[j] > 0)
    def _():
        for cp in weight_copies(te_ref[j], ws):
            cp.wait()

        @pl.when(nexte_ref[j] >= 0)
        def _():
            for cp in weight_copies(nexte_ref[j], 1 - ws):
                cp.start()

        wgb[...] = wgf[ws].astype(jnp.bfloat16)
        wub[...] = wuf[ws].astype(jnp.bfloat16)
        wdb[...] = wdf[ws].astype(jnp.bfloat16)

    def gated_mlp(n_rows):
        xs = xbuf[slot, 0:n_rows, 0:D_MODEL]
        gp = xbuf[slot, 0:n_rows, D_MODEL:D_SLOT].astype(jnp.float32)
        first = gp[:, 6:7] == (te_ref[j] + ROUTE_LANE0).astype(jnp.float32)
        gate_w = jnp.where(first, gp[:, 0:1] + gp[:, 1:2] + gp[:, 2:3], gp[:, 3:4] + gp[:, 4:5] + gp[:, 5:6])
        gate = jnp.dot(xs, wgb[...], preferred_element_type=jnp.float32)
        up = jnp.dot(xs, wub[...], preferred_element_type=jnp.float32)
        act = (gate * (1.0 / (1.0 + jnp.exp(-gate))) * up * gate_w).astype(jnp.bfloat16)
        return jnp.dot(act, wdb[...], preferred_element_type=jnp.float32).astype(jnp.bfloat16)

    for k in range(1, EXP_SUBS + 1):
        @pl.when(nsub_ref[j] == k)
        def _():
            for h in range(k):
                wait_gather(slot, h)
            y_ref[0:k * EXP_SUB, :] = gated_mlp(k * EXP_SUB)
            if k < EXP_SUBS:
                y_ref[k * EXP_SUB:EXP_TILE, :] = jnp.zeros((EXP_TILE - k * EXP_SUB, D_MODEL), jnp.bfloat16)

    @pl.when(nsub_ref[j] == 0)
    def _():
        y_ref[...] = jnp.zeros_like(y_ref)


def _experts(layer, tile_tables, chunk_src, xs_local, wg, wu, wd):
    tile_expert, n_sub, first, wslot, next_expert = tile_tables
    n_tiles = tile_expert.shape[0]
    any_space = pl.BlockSpec(memory_space=pl.ANY)

    return pl.pallas_call(
        functools.partial(_expert_kernel, layer=layer),
        grid_spec=pltpu.PrefetchScalarGridSpec(
            num_scalar_prefetch=6,
            grid=(n_tiles,),
            in_specs=[any_space] * 4,
            out_specs=pl.BlockSpec((EXP_TILE, D_MODEL), lambda j, *tables: (j, 0)),
            scratch_shapes=[
                pltpu.VMEM((2, EXP_TILE, D_SLOT), jnp.bfloat16),
                pltpu.VMEM((2, D_MODEL, D_EXPERT), jnp.float32),
                pltpu.VMEM((2, D_MODEL, D_EXPERT), jnp.float32),
                pltpu.VMEM((2, D_EXPERT, D_MODEL), jnp.float32),
                pltpu.VMEM((D_MODEL, D_EXPERT), jnp.bfloat16),
                pltpu.VMEM((D_MODEL, D_EXPERT), jnp.bfloat16),
                pltpu.VMEM((D_EXPERT, D_MODEL), jnp.bfloat16),
                pltpu.SemaphoreType.DMA((2, EXP_SUBS)),
                pltpu.SemaphoreType.DMA((3, 2)),
            ],
        ),
        out_shape=jax.ShapeDtypeStruct((n_tiles * EXP_TILE, D_MODEL), jnp.bfloat16),
        compiler_params=pltpu.CompilerParams(
            dimension_semantics=("arbitrary",), vmem_limit_bytes=VMEM_LIMIT),
        name="experts",
    )(tile_expert, n_sub, first, wslot, next_expert, chunk_src, xs_local, wg, wu, wd)


def _combine_kernel(ctab_ref, xmid_ref, rinfo_ref, g_ref, ys_hbm, out_ref, ybuf, sem, *, final_norm):
    i = pl.program_id(0)
    nt = pl.num_programs(0)
    slot = i % 2

    def start_gather(tile, s):
        for c in range(LOCAL_CHUNKS):
            _chunk_copy(ys_hbm, ctab_ref[tile * LOCAL_CHUNKS + c], ybuf.at[s], c, sem.at[s]).start()

    def wait_gather(s):
        pltpu.make_async_copy(ys_hbm.at[pl.ds(0, LOCAL_ROWS)], ybuf.at[s], sem.at[s]).wait()

    @pl.when(i == 0)
    def _():
        start_gather(0, 0)

    @pl.when(i + 1 < nt)
    def _():
        start_gather(i + 1, 1 - slot)

    wait_gather(slot)
    info = rinfo_ref[...]
    pcol = lax.broadcasted_iota(jnp.int32, (TOK_TILE, LOCAL_ROWS), 1).astype(jnp.float32)
    pick = jnp.where((pcol == info[:, 0:1]) | (pcol == info[:, 1:2]), 1.0, 0.0).astype(jnp.bfloat16)
    out = xmid_ref[...] + jnp.dot(pick, ybuf[slot], preferred_element_type=jnp.float32)
    if final_norm:
        out = _rmsnorm_f32(out, g_ref[...])
    out_ref[...] = out


def _combine(chunk_tab, xmid, rinfo, g, ys, *, final_norm):
    T = xmid.shape[0]
    return pl.pallas_call(
        functools.partial(_combine_kernel, final_norm=final_norm),
        grid_spec=pltpu.PrefetchScalarGridSpec(
            num_scalar_prefetch=1,
            grid=(T // TOK_TILE,),
            in_specs=[
                pl.BlockSpec((TOK_TILE, D_MODEL), lambda i, ct: (i, 0)),
                pl.BlockSpec((TOK_TILE, LANES), lambda i, ct: (i, 0)),
                pl.BlockSpec((1, D_MODEL), lambda i, ct: (0, 0)),
                pl.BlockSpec(memory_space=pl.ANY),
            ],
            out_specs=pl.BlockSpec((TOK_TILE, D_MODEL), lambda i, ct: (i, 0)),
            scratch_shapes=[
                pltpu.VMEM((2, LOCAL_ROWS, D_MODEL), jnp.bfloat16),
                pltpu.SemaphoreType.DMA((2,)),
            ],
        ),
        out_shape=jax.ShapeDtypeStruct((T, D_MODEL), jnp.float32),
        compiler_params=pltpu.CompilerParams(
            dimension_semantics=("arbitrary",), vmem_limit_bytes=VMEM_LIMIT),
        name="combine",
    )(chunk_tab, xmid, rinfo, g, ys)


def _pair_heads(a, axis):
    shape = a.shape
    split = shape[:axis] + (SWA_KV_HEADS, SWA_REP, HEAD_DIM) + shape[axis + 1:]
    return jnp.swapaxes(a.reshape(split), axis, axis + 1).reshape(shape)


def _in_proj_weight(w):
    off_bq = 3 * D_A
    return jnp.concatenate(
        [w[..., 0:D_A], _pair_heads(w[..., off_bq:off_bq + D_B], w.ndim - 1), w[..., D_A:off_bq],
         w[..., off_bq + D_B:]], axis=-1).astype(jnp.bfloat16)


def _out_proj_weight(w):
    return jnp.concatenate(
        [w[:, 0:D_A], _pair_heads(w[:, D_A:D_A + D_B], 1), w[:, D_A + D_B:]], axis=1).astype(jnp.bfloat16)


def _na_bias_table(rel_bias):
    c = np.arange(GRID_W)[:, None]
    cp = np.arange(GRID_W)[None, :]
    cs = np.clip(c - NA_COLS // 2, 0, GRID_W - NA_COLS)
    valid = (cp >= cs) & (cp < cs + NA_COLS)
    d = np.arange(2 * NA_COLS - 1)[:, None, None]
    col_sel = ((cp - c + (NA_COLS - 1))[None] == d) & valid[None]
    k = np.arange(NA_ROWS)[:, None, None]
    j = np.arange(NA_ROWS)[None, :, None]
    r = np.arange(2 * NA_ROWS - 1)[None, None, :]
    row_sel = r == j - k + (NA_ROWS - 1)
    tab = jnp.einsum("kjr,lhrd,dcm->lkhcjm", jnp.asarray(row_sel, jnp.float32), rel_bias.astype(jnp.float32),
                     jnp.asarray(col_sel, jnp.float32), precision=lax.Precision.HIGHEST)
    tab = jnp.where(jnp.asarray(valid)[None, None, None, :, None, :], tab, NEG)
    return tab.reshape(rel_bias.shape[0], NA_ROWS, NA_HEADS * GRID_W, NA_ROWS * GRID_W)


def _swa_bias_table():
    slopes = (2.0 ** (-8.0 * np.arange(1, SWA_Q_HEADS + 1) / SWA_Q_HEADS)).astype(np.float32)
    qi = np.arange(SWA_BLOCK)[:, None]
    ki = np.arange(3 * SWA_BLOCK)[None, :]
    dist = np.abs(ki - qi - SWA_BLOCK).astype(np.float32)
    tab = np.where(dist <= SWA_WINDOW, -slopes[:, None, None] * dist[None], np.float32(NEG))
    tab = tab.reshape(SWA_Q_HEADS * SWA_BLOCK, 3 * SWA_BLOCK).astype(np.float32)
    first, last = tab.copy(), tab.copy()
    first[:, :SWA_BLOCK] = NEG
    last[:, 2 * SWA_BLOCK:] = NEG
    return jnp.asarray(np.stack([first, tab, last]))


def _block_diag(pool_w):
    depth, n = pool_w.shape[0:2]
    eye = jnp.asarray(np.eye(n, dtype=np.float32))
    out = pool_w[:, :, :, None, :] * eye[None, :, None, :, None]
    return out.reshape(depth, n * POOL_GROUP_DIM, n * POOL_GROUP_DIM)


def _router_weights(rg_w, rg_b, re_w, re_b):
    def lanes(g, e):
        gap = jnp.zeros(g.shape[:-1] + (ROUTE_LANE0 - N_GROUPS,), jnp.float32)
        tail = jnp.zeros(g.shape[:-1] + (LANES - ROUTE_LANE0 - N_EXPERTS,), jnp.float32)
        return jnp.concatenate([g.astype(jnp.float32), gap, e.astype(jnp.float32), tail], axis=-1)

    return lanes(rg_w, re_w).astype(jnp.bfloat16), lanes(rg_b, re_b)[:, None, :]


def _dispatch_tables(cnt, n_tiles):
    nb = cnt.shape[0] // SUBLANES
    n = cnt.reshape(nb, SUBLANES, LANES)[:, 0, ROUTE_LANE0:ROUTE_LANE0 + N_EXPERTS].astype(jnp.int32)
    g = (n + (CHUNK - 1)) // CHUNK
    l_end = jnp.cumsum(g, axis=1)
    l_off = l_end - g
    c_end = jnp.cumsum(g, axis=0)
    c_off = c_end - g
    tot = c_end[-1]
    tiles = (tot + (TILE_CHUNKS - 1)) // TILE_CHUNKS
    t_end = jnp.cumsum(tiles)
    t_off = t_end - tiles
    n_used = t_end[-1:]

    experts = jnp.arange(N_EXPERTS, dtype=jnp.int32)
    tile_ids = jnp.arange(n_tiles, dtype=jnp.int32)
    tile_expert = jnp.minimum(jnp.sum((t_end[None, :] <= tile_ids[:, None]).astype(jnp.int32), axis=1),
                              N_EXPERTS - 1)
    oh_te = (tile_expert[:, None] == experts[None, :]).astype(jnp.int32)
    left = jnp.sum(oh_te * (tot + t_off * TILE_CHUNKS)[None, :], axis=1) - tile_ids * TILE_CHUNKS
    n_sub = jnp.clip((left + (SUB_CHUNKS - 1)) // SUB_CHUNKS, 0, EXP_SUBS)
    has_rows = tiles > 0
    first = ((tile_ids == jnp.sum(oh_te * t_off[None, :], axis=1)) & (n_sub > 0)).astype(jnp.int32)
    wslot = jnp.sum(oh_te * ((jnp.cumsum(has_rows.astype(jnp.int32)) - 1) % 2)[None, :], axis=1)
    later = (experts[None, :] > experts[:, None]) & has_rows[None, :]
    nxt = jnp.min(jnp.where(later, experts[None, :], N_EXPERTS), axis=1)
    next_expert = jnp.sum(oh_te * jnp.where(nxt < N_EXPERTS, nxt, -1)[None, :], axis=1)
    tile_tables = (tile_expert, n_sub, first, wslot, next_expert)

    q = jnp.arange(n_tiles * TILE_CHUNKS, dtype=jnp.int32)
    tile_q = q // TILE_CHUNKS
    oh_e = (jnp.repeat(tile_expert, TILE_CHUNKS)[:, None] == experts[None, :]).astype(jnp.int32)
    ro = q - jnp.sum(oh_e * t_off[None, :], axis=1) * TILE_CHUNKS
    valid = (ro < jnp.sum(oh_e * tot[None, :], axis=1)) & (tile_q < n_used[0])
    cols = jnp.dot(jnp.concatenate([c_end, c_off, l_off], axis=0).astype(jnp.float32),
                   oh_e.T.astype(jnp.float32), precision=lax.Precision.HIGHEST).astype(jnp.int32)
    c_end_q, c_off_q, l_off_q = cols[0:nb], cols[nb:2 * nb], cols[2 * nb:3 * nb]
    b_q = jnp.minimum(jnp.sum((c_end_q <= ro[None, :]).astype(jnp.int32), axis=0), nb - 1)
    oh_b = (jnp.arange(nb, dtype=jnp.int32)[:, None] == b_q[None, :]).astype(jnp.int32)
    src = b_q * LOCAL_CHUNKS + jnp.sum(oh_b * (l_off_q + ro[None, :] - c_off_q), axis=0)
    chunk_src = jnp.where(valid, src, LOCAL_CHUNKS - 1)

    c = jnp.arange(LOCAL_CHUNKS, dtype=jnp.int32)
    e_c = jnp.minimum(jnp.sum((l_end[:, None, :] <= c[None, :, None]).astype(jnp.int32), axis=2),
                      N_EXPERTS - 1)
    oh_ec = (e_c[:, :, None] == experts[None, None, :]).astype(jnp.int32)
    pos = (jnp.sum(oh_ec * (t_off[None, None, :] * TILE_CHUNKS + c_off[:, None, :] - l_off[:, None, :]), axis=2)
           + c[None, :])
    chunk_tab = jnp.where(c[None, :] < l_end[:, -1:], pos, 0).reshape(-1)
    return tile_tables, chunk_src, chunk_tab


def kernel(x, norm1_g, w_in, nat_bias, swa_sink, pool_w, pool_scale, w_out, norm2_g, router_g_w,
           router_g_b, router_e_w, router_e_b, expert_w_gate, expert_w_up, expert_w_down, final_g):
    batch, seq_len, _ = x.shape
    depth = w_in.shape[0]
    T = batch * seq_len
    assert seq_len % TOK_TILE == 0 and TOK_TILE % SWA_BLOCK == 0 and TOK_TILE % GRID_W == 0
    max_chunks = (2 * T) // CHUNK + (T // TOK_TILE) * N_EXPERTS
    n_tiles = max_chunks // TILE_CHUNKS + N_EXPERTS

    swb = _swa_bias_table()
    tri = jnp.asarray(np.tril(np.ones((TOK_TILE, TOK_TILE), np.float32), -1)).astype(jnp.bfloat16)
    utri = jnp.asarray(np.triu(np.ones((LANES, LANES), np.float32), 1)).astype(jnp.bfloat16)

    w_in_k = _in_proj_weight(w_in)
    w_out_k = _out_proj_weight(w_out)
    nab = _na_bias_table(nat_bias)
    sinkcol = jnp.broadcast_to(swa_sink.astype(jnp.float32)[:, :, None, None],
                               (depth, SWA_Q_HEADS, SWA_BLOCK, LANES)).reshape(depth, SWA_Q_HEADS * SWA_BLOCK, LANES)
    poolw = _block_diag(pool_w).astype(jnp.bfloat16)
    pools = pool_scale.reshape(depth, 1, D_C).astype(jnp.float32)
    rw, rb = _router_weights(router_g_w, router_g_b, router_e_w, router_e_b)
    g1 = norm1_g.reshape(depth, 1, D_MODEL)
    g2 = norm2_g.reshape(depth, 1, D_MODEL)

    x2 = x.reshape(T, D_MODEL)
    for l in range(depth):
        qp, kv = _norm_proj(l, x2, g1, w_in_k)
        xmid, xs_local, rinfo, cnt = _mixer(
            l, x2, qp, kv, (nab, sinkcol, poolw, pools, w_out_k, g2, rw, rb), (swb, tri, utri),
            batch=batch, seq_len=seq_len)
        tile_tables, chunk_src, chunk_tab = _dispatch_tables(cnt, n_tiles)
        ys = _experts(l, tile_tables, chunk_src, xs_local,
                      expert_w_gate, expert_w_up, expert_w_down)
        x2 = _combine(chunk_tab, xmid, rinfo, final_g.reshape(1, D_MODEL), ys, final_norm=(l == depth - 1))
    return x2.reshape(batch, seq_len, D_MODEL)
```

```python
import functools

import jax
import jax.numpy as jnp
import numpy as np
from jax import lax
from jax.experimental import pallas as pl
from jax.experimental.pallas import tpu as pltpu

D_MODEL = 1024
GRID_W = 64
HEAD_DIM = 64
NA_HEADS = 4
NA_ROWS = 8
NA_COLS = 16
SWA_Q_HEADS = 8
SWA_KV_HEADS = 2
SWA_REP = SWA_Q_HEADS // SWA_KV_HEADS
SWA_WINDOW = 128
SWA_BLOCK = 128
POOL_WINDOWS = (2, 4, 8, 16)
POOL_GROUP_DIM = 64
D_A = NA_HEADS * HEAD_DIM
D_B = SWA_Q_HEADS * HEAD_DIM
D_BKV = SWA_KV_HEADS * HEAD_DIM
D_C = len(POOL_WINDOWS) * POOL_GROUP_DIM
D_MIX = D_A + D_B + D_C
D_QP = D_A + D_B
D_KV = 2 * D_A + 2 * D_BKV + D_C
D_IN = D_QP + D_KV
N_GROUPS = 4
EXPERTS_PER_GROUP = 8
N_EXPERTS = N_GROUPS * EXPERTS_PER_GROUP
D_EXPERT = 256
RMS_EPS = 1e-6
NEG = -1e30

LANES = 128
SUBLANES = 8
ROW_CHUNKS = D_MODEL // LANES

TOK_TILE = 512
EXP_SUB = 256
EXP_SUBS = 4
EXP_TILE = EXP_SUB * EXP_SUBS
CHUNK = 16
LOCAL_CHUNKS = (2 * TOK_TILE + N_EXPERTS * (CHUNK - 1)) // CHUNK + 2
LOCAL_ROWS = LOCAL_CHUNKS * CHUNK
SUB_CHUNKS = EXP_SUB // CHUNK
TILE_CHUNKS = EXP_TILE // CHUNK
ROUTE_LANE0 = 8
D_SLOT = D_MODEL + LANES
NA_ROWS_PER_STEP = 8
SWA_BLOCKS_PER_STEP = 2
HALO = 8
VMEM_LIMIT = 56 * 1024 * 1024


def _rmsnorm_f32(x, g):
    return x * lax.rsqrt(jnp.mean(x * x, axis=-1, keepdims=True) + RMS_EPS) * g


def _norm_proj_kernel(x_ref, g_ref, w_ref, qp_ref, kv_ref):
    xn = _rmsnorm_f32(x_ref[...], g_ref[...]).astype(jnp.bfloat16)
    proj = jnp.dot(xn, w_ref[...], preferred_element_type=jnp.float32)
    qp_ref[...] = proj[:, :D_QP].astype(jnp.bfloat16)
    kv_ref[...] = proj[:, D_QP:].astype(jnp.bfloat16)


def _norm_proj(layer, x2, g, w):
    T = x2.shape[0]
    return pl.pallas_call(
        _norm_proj_kernel,
        grid=(T // TOK_TILE,),
        in_specs=[
            pl.BlockSpec((TOK_TILE, D_MODEL), lambda i: (i, 0)),
            pl.BlockSpec((None, 1, D_MODEL), lambda i: (layer, 0, 0)),
            pl.BlockSpec((None, D_MODEL, D_IN), lambda i: (layer, 0, 0)),
        ],
        out_specs=[
            pl.BlockSpec((TOK_TILE, D_QP), lambda i: (i, 0)),
            pl.BlockSpec((TOK_TILE, D_KV), lambda i: (i, 0)),
        ],
        out_shape=[
            jax.ShapeDtypeStruct((T, D_QP), jnp.bfloat16),
            jax.ShapeDtypeStruct((T, D_KV), jnp.bfloat16),
        ],
        compiler_params=pltpu.CompilerParams(
            dimension_semantics=("arbitrary",), vmem_limit_bytes=VMEM_LIMIT),
        name="norm_proj",
    )(x2, g, w)


KW_AK, KW_AV, KW_BK, KW_BV = 0, D_A, 2 * D_A, 2 * D_A + D_BKV
KW_COLS = 2 * D_A + 2 * D_BKV
KV_CU = KW_COLS


def _mixer_kernel(x_ref, qp_ref, kvp_ref, kvc_ref, kvn_ref, nab_ref, swb_ref, sink_ref,
                  poolw_ref, pools_ref, wout_ref, g2_ref, rw_ref, rb_ref, tri_ref, utri_ref,
                  xmid_ref, xs_ref, rinfo_ref, cnt_ref,
                  kwin, uwin, mix, xn_scr, logit_scr, *, seq_len):
    t = pl.program_id(0)
    nblk = seq_len // TOK_TILE
    i = jnp.minimum(t, pl.num_programs(0) - 2) % nblk
    rows_per_tile = TOK_TILE // GRID_W
    grid_rows = seq_len // GRID_W

    @pl.when(t == 0)
    def _():
        xn_scr[...] = jnp.zeros_like(xn_scr)
        logit_scr[...] = jnp.zeros_like(logit_scr)

    kwin[0:TOK_TILE, :] = kvp_ref[:, 0:KW_COLS]
    kwin[TOK_TILE:2 * TOK_TILE, :] = kvc_ref[:, 0:KW_COLS]
    kwin[2 * TOK_TILE:3 * TOK_TILE, :] = kvn_ref[:, 0:KW_COLS]

    lane_a = lax.broadcasted_iota(jnp.int32, (GRID_W, D_A), 1) // HEAD_DIM

    def na_row(rr):
        r = i * rows_per_tile + rr
        rs = jnp.clip(r - NA_ROWS // 2, 0, grid_rows - NA_ROWS)
        variant = r - rs
        start = pl.multiple_of((rs - i * rows_per_tile + rows_per_tile) * GRID_W, GRID_W)
        q0 = pl.multiple_of(rr * GRID_W, GRID_W)
        q = qp_ref[pl.ds(q0, GRID_W), 0:D_A] * jnp.bfloat16(HEAD_DIM ** -0.5)
        zero = jnp.zeros_like(q)
        qs = jnp.concatenate([jnp.where(lane_a == h, q, zero) for h in range(NA_HEADS)], axis=0)
        kw = kwin[pl.ds(start, NA_ROWS * GRID_W), KW_AK:KW_AK + D_A]
        vw = kwin[pl.ds(start, NA_ROWS * GRID_W), KW_AV:KW_AV + D_A]
        s = lax.dot_general(qs, kw, (((1,), (1,)), ((), ())), preferred_element_type=jnp.float32)
        s = s + nab_ref[variant]
        m = jnp.max(s, axis=-1, keepdims=True)
        p = jnp.exp(s - m)
        l = jnp.sum(p, axis=-1, keepdims=True)
        pv = jnp.dot(p.astype(jnp.bfloat16), vw, preferred_element_type=jnp.float32)
        pv = pv * (1.0 / l)
        o = jnp.zeros((GRID_W, D_A), jnp.float32)
        for h in range(NA_HEADS):
            o = o + jnp.where(lane_a == h, pv[h * GRID_W:(h + 1) * GRID_W, :], 0.0)
        mix[pl.ds(q0, GRID_W), 0:D_A] = o.astype(jnp.bfloat16)

    def na_step(it, c):
        for k in range(NA_ROWS_PER_STEP):
            na_row(it * NA_ROWS_PER_STEP + k)
        return c

    lax.fori_loop(0, rows_per_tile // NA_ROWS_PER_STEP, na_step, 0)

    lane_b = lax.broadcasted_iota(jnp.int32, (SWA_BLOCK, LANES), 1) // HEAD_DIM
    blocks_per_tile = TOK_TILE // SWA_BLOCK
    nblocks = seq_len // SWA_BLOCK

    ones_v = jnp.ones((3 * SWA_BLOCK, LANES), jnp.bfloat16)

    def swa_step(sb, c):
        n = i * blocks_per_tile + sb
        variant = jnp.where(n == 0, 0, jnp.where(n == nblocks - 1, 2, 1))
        q0 = pl.multiple_of(sb * SWA_BLOCK, SWA_BLOCK)
        k0 = pl.multiple_of(TOK_TILE - SWA_BLOCK + sb * SWA_BLOCK, SWA_BLOCK)
        kw = kwin[pl.ds(k0, 3 * SWA_BLOCK), KW_BK:KW_BK + D_BKV]
        vaug = jnp.concatenate([kwin[pl.ds(k0, 3 * SWA_BLOCK), KW_BV:KW_BV + D_BKV], ones_v], axis=1)
        outs = []
        for g in range(SWA_KV_HEADS):
            pieces = []
            for t in range(SWA_REP):
                qt = qp_ref[pl.ds(q0, SWA_BLOCK), D_A + t * LANES:D_A + (t + 1) * LANES]
                qt = qt * jnp.bfloat16(HEAD_DIM ** -0.5)
                pieces.append(jnp.where(lane_b == g, qt, jnp.zeros_like(qt)))
            qs = jnp.concatenate(pieces, axis=0)
            r0 = g * SWA_REP * SWA_BLOCK
            s = lax.dot_general(qs, kw, (((1,), (1,)), ((), ())), preferred_element_type=jnp.float32)
            s = s + swb_ref[variant, r0:r0 + SWA_REP * SWA_BLOCK, :]
            sink = sink_ref[r0:r0 + SWA_REP * SWA_BLOCK, :]
            m = jnp.broadcast_to(jnp.max(s, axis=-1, keepdims=True), sink.shape)
            m = jnp.maximum(m, sink)
            p = jnp.exp(s - jnp.concatenate([m, m, m], axis=1)).astype(jnp.bfloat16)
            pv = jnp.dot(p, vaug, preferred_element_type=jnp.float32)
            l = pv[:, LANES:2 * LANES] + jnp.exp(sink - m)
            outs.append(pv[:, 0:LANES] * (1.0 / l))
        for t in range(SWA_REP):
            o0 = outs[0][t * SWA_BLOCK:(t + 1) * SWA_BLOCK, :]
            o1 = outs[1][t * SWA_BLOCK:(t + 1) * SWA_BLOCK, :]
            ot = jnp.where(lane_b == 0, o0, o1)
            mix[pl.ds(q0, SWA_BLOCK), D_A + t * LANES:D_A + (t + 1) * LANES] = ot.astype(jnp.bfloat16)
        return c

    def swa_pair(it, c):
        for k in range(SWA_BLOCKS_PER_STEP):
            swa_step(it * SWA_BLOCKS_PER_STEP + k, c)
        return c

    lax.fori_loop(0, blocks_per_tile // SWA_BLOCKS_PER_STEP, swa_pair, 0)

    u = kvc_ref[:, KV_CU:KV_CU + D_C].astype(jnp.float32)
    prev_ok = (i > 0).astype(jnp.float32)
    next_ok = (i < nblk - 1).astype(jnp.float32)
    uwin[0:HALO, :] = kvp_ref[TOK_TILE - HALO:TOK_TILE, KV_CU:KV_CU + D_C].astype(jnp.float32) * prev_ok
    uwin[HALO:HALO + TOK_TILE, :] = u
    uwin[HALO + TOK_TILE:2 * HALO + TOK_TILE, :] = kvn_ref[0:HALO, KV_CU:KV_CU + D_C].astype(jnp.float32) * next_ok
    n_ext = TOK_TILE + 2 * HALO
    a2 = uwin[0:n_ext - 1, :] + uwin[1:n_ext, :]
    a4 = a2[0:n_ext - 3, :] + a2[2:n_ext - 1, :]
    a8 = a4[0:n_ext - 7, :] + a4[4:n_ext - 3, :]
    a16 = a8[0:n_ext - 15, :] + a8[8:n_ext - 7, :]
    w2 = a2[7:7 + TOK_TILE, :]
    w4 = a4[6:6 + TOK_TILE, :]
    w8 = a8[4:4 + TOK_TILE, :]
    w16 = a16[0:TOK_TILE, :]
    lane_c = lax.broadcasted_iota(jnp.int32, (TOK_TILE, D_C), 1) // POOL_GROUP_DIM
    pooled = jnp.where(lane_c == 0, w2, jnp.where(lane_c == 1, w4, jnp.where(lane_c == 2, w8, w16)))
    half = jnp.where(lane_c == 0, 1, jnp.where(lane_c == 1, 2, jnp.where(lane_c == 2, 4, 8)))
    pos = i * TOK_TILE + lax.broadcasted_iota(jnp.int32, (TOK_TILE, D_C), 0)
    cnt = (jnp.minimum(pos + half, seq_len) - jnp.maximum(pos - half, 0)).astype(jnp.float32)
    d = (pooled / cnt - u).astype(jnp.bfloat16)
    oc = jnp.dot(d, poolw_ref[...], preferred_element_type=jnp.float32) * pools_ref[...]
    mix[:, D_A + D_B:D_MIX] = oc.astype(jnp.bfloat16)

    xn = xn_scr[...]
    logits = logit_scr[...]

    xm = x_ref[...] + jnp.dot(mix[...], wout_ref[...], preferred_element_type=jnp.float32)
    xmid_ref[...] = xm
    xn_new = _rmsnorm_f32(xm, g2_ref[...]).astype(jnp.bfloat16)
    xn_scr[...] = xn_new
    logit_scr[...] = jnp.dot(xn_new, rw_ref[...], preferred_element_type=jnp.float32) + rb_ref[...]

    lane = lax.broadcasted_iota(jnp.int32, (TOK_TILE, LANES), 1).astype(jnp.float32)
    is_g = lane < N_GROUPS
    gl = jnp.where(is_g, logits, NEG)
    gmax = jnp.max(gl, axis=-1, keepdims=True)
    gtop = jnp.min(jnp.where(is_g & (gl == gmax), lane, float(LANES)), axis=-1, keepdims=True)
    gprob = 1.0 / jnp.sum(jnp.exp(gl - gmax), axis=-1, keepdims=True)
    e_lo = ROUTE_LANE0 + gtop * EXPERTS_PER_GROUP
    in_grp = (lane >= e_lo) & (lane < e_lo + EXPERTS_PER_GROUP)
    el = jnp.where(in_grp, logits, NEG)
    m1 = jnp.max(el, axis=-1, keepdims=True)
    i1 = jnp.min(jnp.where(in_grp & (el == m1), lane, float(LANES)), axis=-1, keepdims=True)
    el2 = jnp.where(lane == i1, NEG, el)
    m2 = jnp.max(el2, axis=-1, keepdims=True)
    i2 = jnp.min(jnp.where(in_grp & (lane != i1) & (el2 == m2), lane, float(LANES)), axis=-1, keepdims=True)
    r21 = jnp.exp(m2 - m1)
    gate1 = gprob / (1.0 + r21)
    gate2 = gprob * r21 / (1.0 + r21)

    oh1 = lane == i1
    oh2 = lane == i2
    oh = jnp.where(oh1 | oh2, 1.0, 0.0)
    earlier = jnp.dot(tri_ref[...], oh.astype(jnp.bfloat16), preferred_element_type=jnp.float32)
    n_e = jnp.sum(oh, axis=0, keepdims=True)
    chunks_e = jnp.floor((n_e + (CHUNK - 1)) * (1.0 / CHUNK))
    seg0 = jnp.dot(jnp.broadcast_to(chunks_e, (SUBLANES, LANES)).astype(jnp.bfloat16), utri_ref[...],
                   preferred_element_type=jnp.float32)[0:1, :] * CHUNK
    base = earlier + seg0
    lp1 = jnp.sum(jnp.where(oh1, base, 0.0), axis=-1, keepdims=True)
    lp2 = jnp.sum(jnp.where(oh2, base, 0.0), axis=-1, keepdims=True)
    info = jnp.where(lane == 0, lp1, jnp.where(lane == 1, lp2, 0.0))
    rinfo_ref[...] = info
    cnt_ref[...] = jnp.broadcast_to(n_e, cnt_ref.shape)

    def pieces(g):
        hi = g.astype(jnp.bfloat16).astype(jnp.float32)
        mid = (g - hi).astype(jnp.bfloat16).astype(jnp.float32)
        return hi, mid, g - hi - mid

    aux = jnp.zeros((TOK_TILE, LANES), jnp.float32)
    for k, piece in enumerate(pieces(gate1) + pieces(gate2) + (i1,)):
        aux = jnp.where(lane == k, piece, aux)

    info_t = info.T
    prow = lax.broadcasted_iota(jnp.int32, (LOCAL_ROWS, TOK_TILE), 0).astype(jnp.float32)
    sel = jnp.where((prow == info_t[0:1, :]) | (prow == info_t[1:2, :]), 1.0, 0.0).astype(jnp.bfloat16)
    moved = jnp.dot(sel, jnp.concatenate([xn, aux.astype(jnp.bfloat16)], axis=1),
                    preferred_element_type=jnp.float32)
    xs_ref[...] = moved.astype(jnp.bfloat16)


def _mixer(layer, x2, qp, kv, layer_params, shared_tables, *, batch, seq_len):
    nab, sinkcol, poolw, pools, wout, g2, rw, rb = layer_params
    swb, tri, utri = shared_tables
    T = x2.shape[0]
    nblk = seq_len // TOK_TILE
    n_tok_tiles = T // TOK_TILE

    def mixed(t):
        return jnp.minimum(t, n_tok_tiles - 1)

    def cur(t):
        return (mixed(t), 0)

    def prev(t):
        return (jnp.maximum(mixed(t) - 1, (mixed(t) // nblk) * nblk), 0)

    def nxt(t):
        return (jnp.minimum(mixed(t) + 1, (mixed(t) // nblk) * nblk + nblk - 1), 0)

    def dispatched(t):
        return (jnp.maximum(t - 1, 0), 0)

    def resident(a):
        zeros = (0,) * a.ndim
        return pl.BlockSpec(a.shape, lambda t: zeros, pipeline_mode=pl.Buffered(1))

    def resident_layer(a):
        index = (layer,) + (0,) * (a.ndim - 1)
        return pl.BlockSpec((None,) + a.shape[1:], lambda t: index, pipeline_mode=pl.Buffered(1))

    return pl.pallas_call(
        functools.partial(_mixer_kernel, seq_len=seq_len),
        grid=(n_tok_tiles + 1,),
        in_specs=[
            pl.BlockSpec((TOK_TILE, D_MODEL), cur),
            pl.BlockSpec((TOK_TILE, D_QP), cur),
            pl.BlockSpec((TOK_TILE, D_KV), prev),
            pl.BlockSpec((TOK_TILE, D_KV), cur),
            pl.BlockSpec((TOK_TILE, D_KV), nxt),
            resident_layer(nab), resident(swb), resident_layer(sinkcol), resident_layer(poolw),
            resident_layer(pools), resident_layer(wout), resident_layer(g2), resident_layer(rw),
            resident_layer(rb), resident(tri), resident(utri),
        ],
        out_specs=[
            pl.BlockSpec((TOK_TILE, D_MODEL), cur),
            pl.BlockSpec((LOCAL_ROWS, D_SLOT), dispatched),
            pl.BlockSpec((TOK_TILE, LANES), dispatched),
            pl.BlockSpec((SUBLANES, LANES), dispatched),
        ],
        out_shape=[
            jax.ShapeDtypeStruct((T, D_MODEL), jnp.float32),
            jax.ShapeDtypeStruct((n_tok_tiles * LOCAL_ROWS, D_SLOT), jnp.bfloat16),
            jax.ShapeDtypeStruct((T, LANES), jnp.float32),
            jax.ShapeDtypeStruct((n_tok_tiles * SUBLANES, LANES), jnp.float32),
        ],
        scratch_shapes=[
            pltpu.VMEM((3 * TOK_TILE, KW_COLS), jnp.bfloat16),
            pltpu.VMEM((TOK_TILE + 2 * HALO, D_C), jnp.float32),
            pltpu.VMEM((TOK_TILE, D_MIX), jnp.bfloat16),
            pltpu.VMEM((TOK_TILE, D_MODEL), jnp.bfloat16),
            pltpu.VMEM((TOK_TILE, LANES), jnp.float32),
        ],
        compiler_params=pltpu.CompilerParams(
            dimension_semantics=("arbitrary",), vmem_limit_bytes=VMEM_LIMIT),
        name="mixer",
    )(x2, qp, kv, kv, kv, nab, swb, sinkcol, poolw, pools, wout, g2, rw, rb, tri, utri)


def _chunk_copy(src_hbm, src_chunk, dst, dst_chunk, sem):
    return pltpu.make_async_copy(
        src_hbm.at[pl.ds(pl.multiple_of(src_chunk * CHUNK, CHUNK), CHUNK)],
        dst.at[pl.ds(dst_chunk * CHUNK, CHUNK)],
        sem)


def _expert_kernel(te_ref, nsub_ref, first_ref, wslot_ref, nexte_ref, csrc_ref,
                   xs_hbm, wg_hbm, wu_hbm, wd_hbm, y_ref,
                   xbuf, wgf, wuf, wdf, wgb, wub, wdb, sem, wsem, *, layer):
    j = pl.program_id(0)
    nt = pl.num_programs(0)
    slot = j % 2

    def start_gather(tile, s, h):
        for c in range(h * SUB_CHUNKS, (h + 1) * SUB_CHUNKS):
            _chunk_copy(xs_hbm, csrc_ref[tile * TILE_CHUNKS + c], xbuf.at[s], c, sem.at[s, h]).start()

    def wait_gather(s, h):
        rows = pl.ds(h * EXP_SUB, EXP_SUB)
        pltpu.make_async_copy(xs_hbm.at[pl.ds(0, EXP_SUB)], xbuf.at[s, rows], sem.at[s, h]).wait()

    def weight_copies(expert, ws):
        return [pltpu.make_async_copy(w_hbm.at[layer, expert], wbuf.at[ws], wsem.at[k, ws])
                for k, (w_hbm, wbuf) in enumerate(((wg_hbm, wgf), (wu_hbm, wuf), (wd_hbm, wdf)))]

    for h in range(EXP_SUBS):
        @pl.when((j == 0) & (h < nsub_ref[0]))
        def _():
            start_gather(0, 0, h)

        @pl.when(h < nsub_ref[jnp.minimum(j + 1, nt - 1)] * (j + 1 < nt).astype(jnp.int32))
        def _():
            start_gather(j + 1, 1 - slot, h)

    ws = wslot_ref[j]

    @pl.when(j == 0)
    def _():
        for cp in weight_copies(te_ref[0], 0):
            cp.start()

    @pl.when(first_ref[j] > 0)
    def _():
        for cp in weight_copies(te_ref[j], ws):
            cp.wait()

        @pl.when(nexte_ref[j] >= 0)
        def _():
            for cp in weight_copies(nexte_ref[j], 1 - ws):
                cp.start()

        wgb[...] = wgf[ws].astype(jnp.bfloat16)
        wub[...] = wuf[ws].astype(jnp.bfloat16)
        wdb[...] = wdf[ws].astype(jnp.bfloat16)

    def gated_mlp(n_rows):
        xs = xbuf[slot, 0:n_rows, 0:D_MODEL]
        gp = xbuf[slot, 0:n_rows, D_MODEL:D_SLOT].astype(jnp.float32)
        first = gp[:, 6:7] == (te_ref[j] + ROUTE_LANE0).astype(jnp.float32)
        gate_w = jnp.where(first, gp[:, 0:1] + gp[:, 1:2] + gp[:, 2:3], gp[:, 3:4] + gp[:, 4:5] + gp[:, 5:6])
        gate = jnp.dot(xs, wgb[...], preferred_element_type=jnp.float32)
        up = jnp.dot(xs, wub[...], preferred_element_type=jnp.float32)
        act = (gate * (1.0 / (1.0 + jnp.exp(-gate))) * up * gate_w).astype(jnp.bfloat16)
        return jnp.dot(act, wdb[...], preferred_element_type=jnp.float32).astype(jnp.bfloat16)

    for k in range(1, EXP_SUBS + 1):
        @pl.when(nsub_ref[j] == k)
        def _():
            for h in range(k):
                wait_gather(slot, h)
            y_ref[0:k * EXP_SUB, :] = gated_mlp(k * EXP_SUB)
            if k < EXP_SUBS:
                y_ref[k * EXP_SUB:EXP_TILE, :] = jnp.zeros((EXP_TILE - k * EXP_SUB, D_MODEL), jnp.bfloat16)

    @pl.when(nsub_ref[j] == 0)
    def _():
        y_ref[...] = jnp.zeros_like(y_ref)


def _experts(layer, tile_tables, chunk_src, xs_local, wg, wu, wd):
    tile_expert, n_sub, first, wslot, next_expert = tile_tables
    n_tiles = tile_expert.shape[0]
    any_space = pl.BlockSpec(memory_space=pl.ANY)

    return pl.pallas_call(
        functools.partial(_expert_kernel, layer=layer),
        grid_spec=pltpu.PrefetchScalarGridSpec(
            num_scalar_prefetch=6,
            grid=(n_tiles,),
            in_specs=[any_space] * 4,
            out_specs=pl.BlockSpec((EXP_TILE, D_MODEL), lambda j, *tables: (j, 0)),
            scratch_shapes=[
                pltpu.VMEM((2, EXP_TILE, D_SLOT), jnp.bfloat16),
                pltpu.VMEM((2, D_MODEL, D_EXPERT), jnp.float32),
                pltpu.VMEM((2, D_MODEL, D_EXPERT), jnp.float32),
                pltpu.VMEM((2, D_EXPERT, D_MODEL), jnp.float32),
                pltpu.VMEM((D_MODEL, D_EXPERT), jnp.bfloat16),
                pltpu.VMEM((D_MODEL, D_EXPERT), jnp.bfloat16),
                pltpu.VMEM((D_EXPERT, D_MODEL), jnp.bfloat16),
                pltpu.SemaphoreType.DMA((2, EXP_SUBS)),
                pltpu.SemaphoreType.DMA((3, 2)),
            ],
        ),
        out_shape=jax.ShapeDtypeStruct((n_tiles * EXP_TILE, D_MODEL), jnp.bfloat16),
        compiler_params=pltpu.CompilerParams(
            dimension_semantics=("arbitrary",), vmem_limit_bytes=VMEM_LIMIT),
        name="experts",
    )(tile_expert, n_sub, first, wslot, next_expert, chunk_src, xs_local, wg, wu, wd)


def _combine_kernel(ctab_ref, xmid_ref, rinfo_ref, g_ref, ys_hbm, out_ref, ybuf, sem, *, final_norm):
    i = pl.program_id(0)
    nt = pl.num_programs(0)
    slot = i % 2

    def start_gather(tile, s):
        for c in range(LOCAL_CHUNKS):
            _chunk_copy(ys_hbm, ctab_ref[tile * LOCAL_CHUNKS + c], ybuf.at[s], c, sem.at[s]).start()

    def wait_gather(s):
        pltpu.make_async_copy(ys_hbm.at[pl.ds(0, LOCAL_ROWS)], ybuf.at[s], sem.at[s]).wait()

    @pl.when(i == 0)
    def _():
        start_gather(0, 0)

    @pl.when(i + 1 < nt)
    def _():
        start_gather(i + 1, 1 - slot)

    wait_gather(slot)
    info = rinfo_ref[...]
    pcol = lax.broadcasted_iota(jnp.int32, (TOK_TILE, LOCAL_ROWS), 1).astype(jnp.float32)
    pick = jnp.where((pcol == info[:, 0:1]) | (pcol == info[:, 1:2]), 1.0, 0.0).astype(jnp.bfloat16)
    out = xmid_ref[...] + jnp.dot(pick, ybuf[slot], preferred_element_type=jnp.float32)
    if final_norm:
        out = _rmsnorm_f32(out, g_ref[...])
    out_ref[...] = out


def _combine(chunk_tab, xmid, rinfo, g, ys, *, final_norm):
    T = xmid.shape[0]
    return pl.pallas_call(
        functools.partial(_combine_kernel, final_norm=final_norm),
        grid_spec=pltpu.PrefetchScalarGridSpec(
            num_scalar_prefetch=1,
            grid=(T // TOK_TILE,),
            in_specs=[
                pl.BlockSpec((TOK_TILE, D_MODEL), lambda i, ct: (i, 0)),
                pl.BlockSpec((TOK_TILE, LANES), lambda i, ct: (i, 0)),
                pl.BlockSpec((1, D_MODEL), lambda i, ct: (0, 0)),
                pl.BlockSpec(memory_space=pl.ANY),
            ],
            out_specs=pl.BlockSpec((TOK_TILE, D_MODEL), lambda i, ct: (i, 0)),
            scratch_shapes=[
                pltpu.VMEM((2, LOCAL_ROWS, D_MODEL), jnp.bfloat16),
                pltpu.SemaphoreType.DMA((2,)),
            ],
        ),
        out_shape=jax.ShapeDtypeStruct((T, D_MODEL), jnp.float32),
        compiler_params=pltpu.CompilerParams(
            dimension_semantics=("arbitrary",), vmem_limit_bytes=VMEM_LIMIT),
        name="combine",
    )(chunk_tab, xmid, rinfo, g, ys)


def _pair_heads(a, axis):
    shape = a.shape
    split = shape[:axis] + (SWA_KV_HEADS, SWA_REP, HEAD_DIM) + shape[axis + 1:]
    return jnp.swapaxes(a.reshape(split), axis, axis + 1).reshape(shape)


def _in_proj_weight(w):
    off_bq = 3 * D_A
    return jnp.concatenate(
        [w[..., 0:D_A], _pair_heads(w[..., off_bq:off_bq + D_B], w.ndim - 1), w[..., D_A:off_bq],
         w[..., off_bq + D_B:]], axis=-1).astype(jnp.bfloat16)


def _out_proj_weight(w):
    return jnp.concatenate(
        [w[:, 0:D_A], _pair_heads(w[:, D_A:D_A + D_B], 1), w[:, D_A + D_B:]], axis=1).astype(jnp.bfloat16)


def _na_bias_table(rel_bias):
    c = np.arange(GRID_W)[:, None]
    cp = np.arange(GRID_W)[None, :]
    cs = np.clip(c - NA_COLS // 2, 0, GRID_W - NA_COLS)
    valid = (cp >= cs) & (cp < cs + NA_COLS)
    d = np.arange(2 * NA_COLS - 1)[:, None, None]
    col_sel = ((cp - c + (NA_COLS - 1))[None] == d) & valid[None]
    k = np.arange(NA_ROWS)[:, None, None]
    j = np.arange(NA_ROWS)[None, :, None]
    r = np.arange(2 * NA_ROWS - 1)[None, None, :]
    row_sel = r == j - k + (NA_ROWS - 1)
    tab = jnp.einsum("kjr,lhrd,dcm->lkhcjm", jnp.asarray(row_sel, jnp.float32), rel_bias.astype(jnp.float32),
                     jnp.asarray(col_sel, jnp.float32), precision=lax.Precision.HIGHEST)
    tab = jnp.where(jnp.asarray(valid)[None, None, None, :, None, :], tab, NEG)
    return tab.reshape(rel_bias.shape[0], NA_ROWS, NA_HEADS * GRID_W, NA_ROWS * GRID_W)


def _swa_bias_table():
    slopes = (2.0 ** (-8.0 * np.arange(1, SWA_Q_HEADS + 1) / SWA_Q_HEADS)).astype(np.float32)
    qi = np.arange(SWA_BLOCK)[:, None]
    ki = np.arange(3 * SWA_BLOCK)[None, :]
    dist = np.abs(ki - qi - SWA_BLOCK).astype(np.float32)
    tab = np.where(dist <= SWA_WINDOW, -slopes[:, None, None] * dist[None], np.float32(NEG))
    tab = tab.reshape(SWA_Q_HEADS * SWA_BLOCK, 3 * SWA_BLOCK).astype(np.float32)
    first, last = tab.copy(), tab.copy()
    first[:, :SWA_BLOCK] = NEG
    last[:, 2 * SWA_BLOCK:] = NEG
    return jnp.asarray(np.stack([first, tab, last]))


def _block_diag(pool_w):
    depth, n = pool_w.shape[0:2]
    eye = jnp.asarray(np.eye(n, dtype=np.float32))
    out = pool_w[:, :, :, None, :] * eye[None, :, None, :, None]
    return out.reshape(depth, n * POOL_GROUP_DIM, n * POOL_GROUP_DIM)


def _router_weights(rg_w, rg_b, re_w, re_b):
    def lanes(g, e):
        gap = jnp.zeros(g.shape[:-1] + (ROUTE_LANE0 - N_GROUPS,), jnp.float32)
        tail = jnp.zeros(g.shape[:-1] + (LANES - ROUTE_LANE0 - N_EXPERTS,), jnp.float32)
        return jnp.concatenate([g.astype(jnp.float32), gap, e.astype(jnp.float32), tail], axis=-1)

    return lanes(rg_w, re_w).astype(jnp.bfloat16), lanes(rg_b, re_b)[:, None, :]


def _dispatch_tables(cnt, n_tiles):
    nb = cnt.shape[0] // SUBLANES
    n = cnt.reshape(nb, SUBLANES, LANES)[:, 0, ROUTE_LANE0:ROUTE_LANE0 + N_EXPERTS].astype(jnp.int32)
    g = (n + (CHUNK - 1)) // CHUNK
    l_end = jnp.cumsum(g, axis=1)
    l_off = l_end - g
    c_end = jnp.cumsum(g, axis=0)
    c_off = c_end - g
    tot = c_end[-1]
    tiles = (tot + (TILE_CHUNKS - 1)) // TILE_CHUNKS
    t_end = jnp.cumsum(tiles)
    t_off = t_end - tiles
    n_used = t_end[-1:]

    experts = jnp.arange(N_EXPERTS, dtype=jnp.int32)
    tile_ids = jnp.arange(n_tiles, dtype=jnp.int32)
    tile_expert = jnp.minimum(jnp.sum((t_end[None, :] <= tile_ids[:, None]).astype(jnp.int32), axis=1),
                              N_EXPERTS - 1)
    oh_te = (tile_expert[:, None] == experts[None, :]).astype(jnp.int32)
    left = jnp.sum(oh_te * (tot + t_off * TILE_CHUNKS)[None, :], axis=1) - tile_ids * TILE_CHUNKS
    n_sub = jnp.clip((left + (SUB_CHUNKS - 1)) // SUB_CHUNKS, 0, EXP_SUBS)
    has_rows = tiles > 0
    first = ((tile_ids == jnp.sum(oh_te * t_off[None, :], axis=1)) & (n_sub > 0)).astype(jnp.int32)
    wslot = jnp.sum(oh_te * ((jnp.cumsum(has_rows.astype(jnp.int32)) - 1) % 2)[None, :], axis=1)
    later = (experts[None, :] > experts[:, None]) & has_rows[None, :]
    nxt = jnp.min(jnp.where(later, experts[None, :], N_EXPERTS), axis=1)
    next_expert = jnp.sum(oh_te * jnp.where(nxt < N_EXPERTS, nxt, -1)[None, :], axis=1)
    tile_tables = (tile_expert, n_sub, first, wslot, next_expert)

    q = jnp.arange(n_tiles * TILE_CHUNKS, dtype=jnp.int32)
    tile_q = q // TILE_CHUNKS
    oh_e = (jnp.repeat(tile_expert, TILE_CHUNKS)[:, None] == experts[None, :]).astype(jnp.int32)
    ro = q - jnp.sum(oh_e * t_off[None, :], axis=1) * TILE_CHUNKS
    valid = (ro < jnp.sum(oh_e * tot[None, :], axis=1)) & (tile_q < n_used[0])
    cols = jnp.dot(jnp.concatenate([c_end, c_off, l_off], axis=0).astype(jnp.float32),
                   oh_e.T.astype(jnp.float32), precision=lax.Precision.HIGHEST).astype(jnp.int32)
    c_end_q, c_off_q, l_off_q = cols[0:nb], cols[nb:2 * nb], cols[2 * nb:3 * nb]
    b_q = jnp.minimum(jnp.sum((c_end_q <= ro[None, :]).astype(jnp.int32), axis=0), nb - 1)
    oh_b = (jnp.arange(nb, dtype=jnp.int32)[:, None] == b_q[None, :]).astype(jnp.int32)
    src = b_q * LOCAL_CHUNKS + jnp.sum(oh_b * (l_off_q + ro[None, :] - c_off_q), axis=0)
    chunk_src = jnp.where(valid, src, LOCAL_CHUNKS - 1)

    c = jnp.arange(LOCAL_CHUNKS, dtype=jnp.int32)
    e_c = jnp.minimum(jnp.sum((l_end[:, None, :] <= c[None, :, None]).astype(jnp.int32), axis=2),
                      N_EXPERTS - 1)
    oh_ec = (e_c[:, :, None] == experts[None, None, :]).astype(jnp.int32)
    pos = (jnp.sum(oh_ec * (t_off[None, None, :] * TILE_CHUNKS + c_off[:, None, :] - l_off[:, None, :]), axis=2)
           + c[None, :])
    chunk_tab = jnp.where(c[None, :] < l_end[:, -1:], pos, 0).reshape(-1)
    return tile_tables, chunk_src, chunk_tab


def kernel(x, norm1_g, w_in, nat_bias, swa_sink, pool_w, pool_scale, w_out, norm2_g, router_g_w,
           router_g_b, router_e_w, router_e_b, expert_w_gate, expert_w_up, expert_w_down, final_g):
    batch, seq_len, _ = x.shape
    depth = w_in.shape[0]
    T = batch * seq_len
    assert seq_len % TOK_TILE == 0 and TOK_TILE % SWA_BLOCK == 0 and TOK_TILE % GRID_W == 0
    max_chunks = (2 * T) // CHUNK + (T // TOK_TILE) * N_EXPERTS
    n_tiles = max_chunks // TILE_CHUNKS + N_EXPERTS

    swb = _swa_bias_table()
    tri = jnp.asarray(np.tril(np.ones((TOK_TILE, TOK_TILE), np.float32), -1)).astype(jnp.bfloat16)
    utri = jnp.asarray(np.triu(np.ones((LANES, LANES), np.float32), 1)).astype(jnp.bfloat16)

    w_in_k = _in_proj_weight(w_in)
    w_out_k = _out_proj_weight(w_out)
    nab = _na_bias_table(nat_bias)
    sinkcol = jnp.broadcast_to(swa_sink.astype(jnp.float32)[:, :, None, None],
                               (depth, SWA_Q_HEADS, SWA_BLOCK, LANES)).reshape(depth, SWA_Q_HEADS * SWA_BLOCK, LANES)
    poolw = _block_diag(pool_w).astype(jnp.bfloat16)
    pools = pool_scale.reshape(depth, 1, D_C).astype(jnp.float32)
    rw, rb = _router_weights(router_g_w, router_g_b, router_e_w, router_e_b)
    g1 = norm1_g.reshape(depth, 1, D_MODEL)
    g2 = norm2_g.reshape(depth, 1, D_MODEL)

    x2 = x.reshape(T, D_MODEL)
    for l in range(depth):
        qp, kv = _norm_proj(l, x2, g1, w_in_k)
        xmid, xs_local, rinfo, cnt = _mixer(
            l, x2, qp, kv, (nab, sinkcol, poolw, pools, w_out_k, g2, rw, rb), (swb, tri, utri),
            batch=batch, seq_len=seq_len)
        tile_tables, chunk_src, chunk_tab = _dispatch_tables(cnt, n_tiles)
        ys = _experts(l, tile_tables, chunk_src, xs_local,
                      expert_w_gate, expert_w_up, expert_w_down)
        x2 = _combine(chunk_tab, xmid, rinfo, final_g.reshape(1, D_MODEL), ys, final_norm=(l == depth - 1))
    return x2.reshape(batch, seq_len, D_MODEL)
```

```python
import functools

import jax
import jax.numpy as jnp
import numpy as np
from jax import lax
from jax.experimental import pallas as pl
from jax.experimental.pallas import tpu as pltpu

D_MODEL = 1024
GRID_W = 64
HEAD_DIM = 64
NA_HEADS = 4
NA_ROWS = 8
NA_COLS = 16
SWA_Q_HEADS = 8
SWA_KV_HEADS = 2
SWA_REP = SWA_Q_HEADS // SWA_KV_HEADS
SWA_WINDOW = 128
SWA_BLOCK = 128
POOL_WINDOWS = (2, 4, 8, 16)
POOL_GROUP_DIM = 64
D_A = NA_HEADS * HEAD_DIM
D_B = SWA_Q_HEADS * HEAD_DIM
D_BKV = SWA_KV_HEADS * HEAD_DIM
D_C = len(POOL_WINDOWS) * POOL_GROUP_DIM
D_MIX = D_A + D_B + D_C
D_QP = D_A + D_B
D_KV = 2 * D_A + 2 * D_BKV + D_C
D_IN = D_QP + D_KV
N_GROUPS = 4
EXPERTS_PER_GROUP = 8
N_EXPERTS = N_GROUPS * EXPERTS_PER_GROUP
D_EXPERT = 256
RMS_EPS = 1e-6
NEG = -1e30
LOG2E = 1.4426950408889634
QK_SCALE = HEAD_DIM ** -0.5 * LOG2E

LANES = 128
SUBLANES = 8
ROW_CHUNKS = D_MODEL // LANES

TOK_TILE = 512
EXP_SUB = 256
EXP_SUBS = 4
EXP_TILE = EXP_SUB * EXP_SUBS
CHUNK = 16
LOCAL_CHUNKS = (2 * TOK_TILE + N_EXPERTS * (CHUNK - 1)) // CHUNK + 2
LOCAL_ROWS = LOCAL_CHUNKS * CHUNK
SUB_CHUNKS = EXP_SUB // CHUNK
TILE_CHUNKS = EXP_TILE // CHUNK
ROUTE_LANE0 = 8
D_SLOT = D_MODEL + LANES
NA_ROWS_PER_STEP = 8
SWA_BLOCKS_PER_STEP = 4
HALO = 8
VMEM_LIMIT = 56 * 1024 * 1024


def _rmsnorm_f32(x, g):
    return x * lax.rsqrt(jnp.mean(x * x, axis=-1, keepdims=True) + RMS_EPS) * g


def _norm_proj_kernel(x_ref, g_ref, w_ref, qp_ref, kv_ref):
    xn = _rmsnorm_f32(x_ref[...], g_ref[...]).astype(jnp.bfloat16)
    proj = jnp.dot(xn, w_ref[...], preferred_element_type=jnp.float32)
    qp_ref[...] = proj[:, :D_QP].astype(jnp.bfloat16)
    kv_ref[...] = proj[:, D_QP:].astype(jnp.bfloat16)


def _norm_proj(layer, x2, g, w):
    T = x2.shape[0]
    return pl.pallas_call(
        _norm_proj_kernel,
        grid=(T // TOK_TILE,),
        in_specs=[
            pl.BlockSpec((TOK_TILE, D_MODEL), lambda i: (i, 0)),
            pl.BlockSpec((None, 1, D_MODEL), lambda i: (layer, 0, 0)),
            pl.BlockSpec((None, D_MODEL, D_IN), lambda i: (layer, 0, 0)),
        ],
        out_specs=[
            pl.BlockSpec((TOK_TILE, D_QP), lambda i: (i, 0)),
            pl.BlockSpec((TOK_TILE, D_KV), lambda i: (i, 0)),
        ],
        out_shape=[
            jax.ShapeDtypeStruct((T, D_QP), jnp.bfloat16),
            jax.ShapeDtypeStruct((T, D_KV), jnp.bfloat16),
        ],
        compiler_params=pltpu.CompilerParams(
            dimension_semantics=("arbitrary",), vmem_limit_bytes=VMEM_LIMIT),
        name="norm_proj",
    )(x2, g, w)


KW_AK, KW_AV, KW_BK, KW_BV = 0, D_A, 2 * D_A, 2 * D_A + D_BKV
KW_COLS = 2 * D_A + 2 * D_BKV
KV_CU = KW_COLS


def _mixer_kernel(x_ref, qp_ref, kvp_ref, kvc_ref, kvn_ref, nab_ref, swb_ref, sink_ref,
                  poolw_ref, pools_ref, wout_ref, g2_ref, rw_ref, rb_ref, tri_ref, utri_ref,
                  xmid_ref, xs_ref, rinfo_ref, cnt_ref,
                  kwin, uwin, mix, xn_scr, logit_scr, *, seq_len):
    t = pl.program_id(0)
    nblk = seq_len // TOK_TILE
    i = jnp.minimum(t, pl.num_programs(0) - 2) % nblk
    rows_per_tile = TOK_TILE // GRID_W
    grid_rows = seq_len // GRID_W

    @pl.when(t == 0)
    def _():
        xn_scr[...] = jnp.zeros_like(xn_scr)
        logit_scr[...] = jnp.zeros_like(logit_scr)

    kwin[0:TOK_TILE, :] = kvp_ref[:, 0:KW_COLS]
    kwin[TOK_TILE:2 * TOK_TILE, :] = kvc_ref[:, 0:KW_COLS]
    kwin[2 * TOK_TILE:3 * TOK_TILE, :] = kvn_ref[:, 0:KW_COLS]

    lane_a = lax.broadcasted_iota(jnp.int32, (GRID_W, D_A), 1) // HEAD_DIM

    def na_row(rr):
        r = i * rows_per_tile + rr
        rs = jnp.clip(r - NA_ROWS // 2, 0, grid_rows - NA_ROWS)
        variant = r - rs
        start = pl.multiple_of((rs - i * rows_per_tile + rows_per_tile) * GRID_W, GRID_W)
        q0 = pl.multiple_of(rr * GRID_W, GRID_W)
        q = (qp_ref[pl.ds(q0, GRID_W), 0:D_A].astype(jnp.float32) * QK_SCALE).astype(jnp.bfloat16)
        zero = jnp.zeros_like(q)
        qs = jnp.concatenate([jnp.where(lane_a == h, q, zero) for h in range(NA_HEADS)], axis=0)
        kw = kwin[pl.ds(start, NA_ROWS * GRID_W), KW_AK:KW_AK + D_A]
        vw = kwin[pl.ds(start, NA_ROWS * GRID_W), KW_AV:KW_AV + D_A]
        s = lax.dot_general(qs, kw, (((1,), (1,)), ((), ())), preferred_element_type=jnp.float32)
        s = s + nab_ref[variant]
        m = jnp.max(s, axis=-1, keepdims=True)
        p = jnp.exp2(s - m)
        l = jnp.sum(p, axis=-1, keepdims=True)
        pv = jnp.dot(p.astype(jnp.bfloat16), vw, preferred_element_type=jnp.float32)
        pv = pv * (1.0 / l)
        o = jnp.zeros((GRID_W, D_A), jnp.float32)
        for h in range(NA_HEADS):
            o = o + jnp.where(lane_a == h, pv[h * GRID_W:(h + 1) * GRID_W, :], 0.0)
        mix[pl.ds(q0, GRID_W), 0:D_A] = o.astype(jnp.bfloat16)

    def na_step(it, c):
        for k in range(NA_ROWS_PER_STEP):
            na_row(it * NA_ROWS_PER_STEP + k)
        return c

    lax.fori_loop(0, rows_per_tile // NA_ROWS_PER_STEP, na_step, 0)

    lane_b = lax.broadcasted_iota(jnp.int32, (SWA_BLOCK, LANES), 1) // HEAD_DIM
    blocks_per_tile = TOK_TILE // SWA_BLOCK
    nblocks = seq_len // SWA_BLOCK

    ones_v = jnp.ones((3 * SWA_BLOCK, LANES), jnp.bfloat16)

    def swa_step(sb, c):
        n = i * blocks_per_tile + sb
        variant = jnp.where(n == 0, 0, jnp.where(n == nblocks - 1, 2, 1))
        q0 = pl.multiple_of(sb * SWA_BLOCK, SWA_BLOCK)
        k0 = pl.multiple_of(TOK_TILE - SWA_BLOCK + sb * SWA_BLOCK, SWA_BLOCK)
        kw = kwin[pl.ds(k0, 3 * SWA_BLOCK), KW_BK:KW_BK + D_BKV]
        vaug = jnp.concatenate([kwin[pl.ds(k0, 3 * SWA_BLOCK), KW_BV:KW_BV + D_BKV], ones_v], axis=1)
        outs = []
        for g in range(SWA_KV_HEADS):
            pieces = []
            for t in range(SWA_REP):
                qt = qp_ref[pl.ds(q0, SWA_BLOCK), D_A + t * LANES:D_A + (t + 1) * LANES]
                qt = (qt.astype(jnp.float32) * QK_SCALE).astype(jnp.bfloat16)
                pieces.append(jnp.where(lane_b == g, qt, jnp.zeros_like(qt)))
            qs = jnp.concatenate(pieces, axis=0)
            r0 = g * SWA_REP * SWA_BLOCK
            s = lax.dot_general(qs, kw, (((1,), (1,)), ((), ())), preferred_element_type=jnp.float32)
            s = s + swb_ref[variant, r0:r0 + SWA_REP * SWA_BLOCK, :]
            sink = sink_ref[r0:r0 + SWA_REP * SWA_BLOCK, :]
            m = jnp.broadcast_to(jnp.max(s, axis=-1, keepdims=True), sink.shape)
            m = jnp.maximum(m, sink)
            p = jnp.exp2(s - jnp.concatenate([m, m, m], axis=1)).astype(jnp.bfloat16)
            pv = jnp.dot(p, vaug, preferred_element_type=jnp.float32)
            l = pv[:, LANES:2 * LANES] + jnp.exp2(sink - m)
            outs.append(pv[:, 0:LANES] * (1.0 / l))
        for t in range(SWA_REP):
            o0 = outs[0][t * SWA_BLOCK:(t + 1) * SWA_BLOCK, :]
            o1 = outs[1][t * SWA_BLOCK:(t + 1) * SWA_BLOCK, :]
            ot = jnp.where(lane_b == 0, o0, o1)
            mix[pl.ds(q0, SWA_BLOCK), D_A + t * LANES:D_A + (t + 1) * LANES] = ot.astype(jnp.bfloat16)
        return c

    def swa_pair(it, c):
        for k in range(SWA_BLOCKS_PER_STEP):
            swa_step(it * SWA_BLOCKS_PER_STEP + k, c)
        return c

    lax.fori_loop(0, blocks_per_tile // SWA_BLOCKS_PER_STEP, swa_pair, 0)

    u = kvc_ref[:, KV_CU:KV_CU + D_C].astype(jnp.float32)
    prev_ok = (i > 0).astype(jnp.float32)
    next_ok = (i < nblk - 1).astype(jnp.float32)
    uwin[0:HALO, :] = kvp_ref[TOK_TILE - HALO:TOK_TILE, KV_CU:KV_CU + D_C].astype(jnp.float32) * prev_ok
    uwin[HALO:HALO + TOK_TILE, :] = u
    uwin[HALO + TOK_TILE:2 * HALO + TOK_TILE, :] = kvn_ref[0:HALO, KV_CU:KV_CU + D_C].astype(jnp.float32) * next_ok
    n_ext = TOK_TILE + 2 * HALO
    a2 = uwin[0:n_ext - 1, :] + uwin[1:n_ext, :]
    a4 = a2[0:n_ext - 3, :] + a2[2:n_ext - 1, :]
    a8 = a4[0:n_ext - 7, :] + a4[4:n_ext - 3, :]
    a16 = a8[0:n_ext - 15, :] + a8[8:n_ext - 7, :]
    w2 = a2[7:7 + TOK_TILE, :]
    w4 = a4[6:6 + TOK_TILE, :]
    w8 = a8[4:4 + TOK_TILE, :]
    w16 = a16[0:TOK_TILE, :]
    lane_c = lax.broadcasted_iota(jnp.int32, (TOK_TILE, D_C), 1) // POOL_GROUP_DIM
    pooled = jnp.where(lane_c == 0, w2, jnp.where(lane_c == 1, w4, jnp.where(lane_c == 2, w8, w16)))
    half = jnp.where(lane_c == 0, 1, jnp.where(lane_c == 1, 2, jnp.where(lane_c == 2, 4, 8)))
    pos = i * TOK_TILE + lax.broadcasted_iota(jnp.int32, (TOK_TILE, D_C), 0)
    cnt = (jnp.minimum(pos + half, seq_len) - jnp.maximum(pos - half, 0)).astype(jnp.float32)
    d = (pooled / cnt - u).astype(jnp.bfloat16)
    oc = jnp.dot(d, poolw_ref[...], preferred_element_type=jnp.float32) * pools_ref[...]
    mix[:, D_A + D_B:D_MIX] = oc.astype(jnp.bfloat16)

    xn = xn_scr[...]
    logits = logit_scr[...]

    xm = x_ref[...] + jnp.dot(mix[...], wout_ref[...], preferred_element_type=jnp.float32)
    xmid_ref[...] = xm
    xn_new = _rmsnorm_f32(xm, g2_ref[...]).astype(jnp.bfloat16)
    xn_scr[...] = xn_new
    logit_scr[...] = jnp.dot(xn_new, rw_ref[...], preferred_element_type=jnp.float32) + rb_ref[...]

    lane = lax.broadcasted_iota(jnp.int32, (TOK_TILE, LANES), 1).astype(jnp.float32)
    is_g = lane < N_GROUPS
    gl = jnp.where(is_g, logits, NEG)
    gmax = jnp.max(gl, axis=-1, keepdims=True)
    gtop = jnp.min(jnp.where(is_g & (gl == gmax), lane, float(LANES)), axis=-1, keepdims=True)
    gprob = 1.0 / jnp.sum(jnp.exp(gl - gmax), axis=-1, keepdims=True)
    e_lo = ROUTE_LANE0 + gtop * EXPERTS_PER_GROUP
    in_grp = (lane >= e_lo) & (lane < e_lo + EXPERTS_PER_GROUP)
    el = jnp.where(in_grp, logits, NEG)
    m1 = jnp.max(el, axis=-1, keepdims=True)
    i1 = jnp.min(jnp.where(in_grp & (el == m1), lane, float(LANES)), axis=-1, keepdims=True)
    el2 = jnp.where(lane == i1, NEG, el)
    m2 = jnp.max(el2, axis=-1, keepdims=True)
    i2 = jnp.min(jnp.where(in_grp & (lane != i1) & (el2 == m2), lane, float(LANES)), axis=-1, keepdims=True)
    r21 = jnp.exp(m2 - m1)
    gate1 = gprob / (1.0 + r21)
    gate2 = gprob * r21 / (1.0 + r21)

    oh1 = lane == i1
    oh2 = lane == i2
    oh = jnp.where(oh1 | oh2, 1.0, 0.0)
    earlier = jnp.dot(tri_ref[...], oh.astype(jnp.bfloat16), preferred_element_type=jnp.float32)
    n_e = jnp.sum(oh, axis=0, keepdims=True)
    chunks_e = jnp.floor((n_e + (CHUNK - 1)) * (1.0 / CHUNK))
    seg0 = jnp.dot(jnp.broadcast_to(chunks_e, (SUBLANES, LANES)).astype(jnp.bfloat16), utri_ref[...],
                   preferred_element_type=jnp.float32)[0:1, :] * CHUNK
    base = earlier + seg0
    lp1 = jnp.sum(jnp.where(oh1, base, 0.0), axis=-1, keepdims=True)
    lp2 = jnp.sum(jnp.where(oh2, base, 0.0), axis=-1, keepdims=True)
    info = jnp.where(lane == 0, lp1, jnp.where(lane == 1, lp2, 0.0))
    rinfo_ref[...] = info
    cnt_ref[...] = jnp.broadcast_to(n_e, cnt_ref.shape)

    def pieces(g):
        hi = g.astype(jnp.bfloat16).astype(jnp.float32)
        mid = (g - hi).astype(jnp.bfloat16).astype(jnp.float32)
        return hi, mid, g - hi - mid

    aux = jnp.zeros((TOK_TILE, LANES), jnp.float32)
    for k, piece in enumerate(pieces(gate1) + pieces(gate2) + (i1,)):
        aux = jnp.where(lane == k, piece, aux)

    info_t = info.T
    prow = lax.broadcasted_iota(jnp.int32, (LOCAL_ROWS, TOK_TILE), 0).astype(jnp.float32)
    sel = jnp.where((prow == info_t[0:1, :]) | (prow == info_t[1:2, :]), 1.0, 0.0).astype(jnp.bfloat16)
    moved = jnp.dot(sel, jnp.concatenate([xn, aux.astype(jnp.bfloat16)], axis=1),
                    preferred_element_type=jnp.float32)
    xs_ref[...] = moved.astype(jnp.bfloat16)


def _mixer(layer, x2, qp, kv, layer_params, shared_tables, *, batch, seq_len):
    nab, sinkcol, poolw, pools, wout, g2, rw, rb = layer_params
    swb, tri, utri = shared_tables
    T = x2.shape[0]
    nblk = seq_len // TOK_TILE
    n_tok_tiles = T // TOK_TILE

    def mixed(t):
        return jnp.minimum(t, n_tok_tiles - 1)

    def cur(t):
        return (mixed(t), 0)

    def prev(t):
        return (jnp.maximum(mixed(t) - 1, (mixed(t) // nblk) * nblk), 0)

    def nxt(t):
        return (jnp.minimum(mixed(t) + 1, (mixed(t) // nblk) * nblk + nblk - 1), 0)

    def dispatched(t):
        return (jnp.maximum(t - 1, 0), 0)

    def resident(a):
        zeros = (0,) * a.ndim
        return pl.BlockSpec(a.shape, lambda t: zeros, pipeline_mode=pl.Buffered(1))

    def resident_layer(a):
        index = (layer,) + (0,) * (a.ndim - 1)
        return pl.BlockSpec((None,) + a.shape[1:], lambda t: index, pipeline_mode=pl.Buffered(1))

    return pl.pallas_call(
        functools.partial(_mixer_kernel, seq_len=seq_len),
        grid=(n_tok_tiles + 1,),
        in_specs=[
            pl.BlockSpec((TOK_TILE, D_MODEL), cur),
            pl.BlockSpec((TOK_TILE, D_QP), cur),
            pl.BlockSpec((TOK_TILE, D_KV), prev),
            pl.BlockSpec((TOK_TILE, D_KV), cur),
            pl.BlockSpec((TOK_TILE, D_KV), nxt),
            resident_layer(nab), resident(swb), resident_layer(sinkcol), resident_layer(poolw),
            resident_layer(pools), resident_layer(wout), resident_layer(g2), resident_layer(rw),
            resident_layer(rb), resident(tri), resident(utri),
        ],
        out_specs=[
            pl.BlockSpec((TOK_TILE, D_MODEL), cur),
            pl.BlockSpec((LOCAL_ROWS, D_SLOT), dispatched),
            pl.BlockSpec((TOK_TILE, LANES), dispatched),
            pl.BlockSpec((SUBLANES, LANES), dispatched),
        ],
        out_shape=[
            jax.ShapeDtypeStruct((T, D_MODEL), jnp.float32),
            jax.ShapeDtypeStruct((n_tok_tiles * LOCAL_ROWS, D_SLOT), jnp.bfloat16),
            jax.ShapeDtypeStruct((T, LANES), jnp.float32),
            jax.ShapeDtypeStruct((n_tok_tiles * SUBLANES, LANES), jnp.float32),
        ],
        scratch_shapes=[
            pltpu.VMEM((3 * TOK_TILE, KW_COLS), jnp.bfloat16),
            pltpu.VMEM((TOK_TILE + 2 * HALO, D_C), jnp.float32),
            pltpu.VMEM((TOK_TILE, D_MIX), jnp.bfloat16),
            pltpu.VMEM((TOK_TILE, D_MODEL), jnp.bfloat16),
            pltpu.VMEM((TOK_TILE, LANES), jnp.float32),
        ],
        compiler_params=pltpu.CompilerParams(
            dimension_semantics=("arbitrary",), vmem_limit_bytes=VMEM_LIMIT),
        name="mixer",
    )(x2, qp, kv, kv, kv, nab, swb, sinkcol, poolw, pools, wout, g2, rw, rb, tri, utri)


def _chunk_copy(src_hbm, src_chunk, dst, dst_chunk, sem):
    return pltpu.make_async_copy(
        src_hbm.at[pl.ds(pl.multiple_of(src_chunk * CHUNK, CHUNK), CHUNK)],
        dst.at[pl.ds(dst_chunk * CHUNK, CHUNK)],
        sem)


def _expert_kernel(te_ref, nsub_ref, first_ref, wslot_ref, nexte_ref, csrc_ref,
                   xs_hbm, wg_hbm, wu_hbm, wd_hbm, y_ref,
                   xbuf, wgf, wuf, wdf, wgb, wub, wdb, sem, wsem, *, layer):
    j = pl.program_id(0)
    nt = pl.num_programs(0)
    slot = j % 2

    def start_gather(tile, s, h):
        for c in range(h * SUB_CHUNKS, (h + 1) * SUB_CHUNKS):
            _chunk_copy(xs_hbm, csrc_ref[tile * TILE_CHUNKS + c], xbuf.at[s], c, sem.at[s, h]).start()

    def wait_gather(s, h):
        rows = pl.ds(h * EXP_SUB, EXP_SUB)
        pltpu.make_async_copy(xs_hbm.at[pl.ds(0, EXP_SUB)], xbuf.at[s, rows], sem.at[s, h]).wait()

    def weight_copies(expert, ws):
        return [pltpu.make_async_copy(w_hbm.at[layer, expert], wbuf.at[ws], wsem.at[k, ws])
                for k, (w_hbm, wbuf) in enumerate(((wg_hbm, wgf), (wu_hbm, wuf), (wd_hbm, wdf)))]

    for h in range(EXP_SUBS):
        @pl.when((j == 0) & (h < nsub_ref[0]))
        def _():
            start_gather(0, 0, h)

        @pl.when(h < nsub_ref[jnp.minimum(j + 1, nt - 1)] * (j + 1 < nt).astype(jnp.int32))
        def _():
            start_gather(j + 1, 1 - slot, h)

    ws = wslot_ref[j]

    @pl.when(j == 0)
    def _():
        for cp in weight_copies(te_ref[0], 0):
            cp.start()

    @pl.when(first_ref[j] > 0)
    def _():
        for cp in weight_copies(te_ref[j], ws):
            cp.wait()

        @pl.when(nexte_ref[j] >= 0)
        def _():
            for cp in weight_copies(nexte_ref[j], 1 - ws):
                cp.start()

        wgb[...] = wgf[ws].astype(jnp.bfloat16)
        wub[...] = wuf[ws].astype(jnp.bfloat16)
        wdb[...] = wdf[ws].astype(jnp.bfloat16)

    def gated_mlp(n_rows):
        xs = xbuf[slot, 0:n_rows, 0:D_MODEL]
        gp = xbuf[slot, 0:n_rows, D_MODEL:D_SLOT].astype(jnp.float32)
        first = gp[:, 6:7] == (te_ref[j] + ROUTE_LANE0).astype(jnp.float32)
        gate_w = jnp.where(first, gp[:, 0:1] + gp[:, 1:2] + gp[:, 2:3], gp[:, 3:4] + gp[:, 4:5] + gp[:, 5:6])
        gate = jnp.dot(xs, wgb[...], preferred_element_type=jnp.float32)
        up = jnp.dot(xs, wub[...], preferred_element_type=jnp.float32)
        act = (gate * (1.0 / (1.0 + jnp.exp(-gate))) * up * gate_w).astype(jnp.bfloat16)
        return jnp.dot(act, wdb[...], preferred_element_type=jnp.float32).astype(jnp.bfloat16)

    for k in range(1, EXP_SUBS + 1):
        @pl.when(nsub_ref[j] == k)
        def _():
            for h in range(k):
                wait_gather(slot, h)
            y_ref[0:k * EXP_SUB, :] = gated_mlp(k * EXP_SUB)
            if k < EXP_SUBS:
                y_ref[k * EXP_SUB:EXP_TILE, :] = jnp.zeros((EXP_TILE - k * EXP_SUB, D_MODEL), jnp.bfloat16)

    @pl.when(nsub_ref[j] == 0)
    def _():
        y_ref[...] = jnp.zeros_like(y_ref)


def _experts(layer, tile_tables, chunk_src, xs_local, wg, wu, wd):
    tile_expert, n_sub, first, wslot, next_expert = tile_tables
    n_tiles = tile_expert.shape[0]
    any_space = pl.BlockSpec(memory_space=pl.ANY)

    return pl.pallas_call(
        functools.partial(_expert_kernel, layer=layer),
        grid_spec=pltpu.PrefetchScalarGridSpec(
            num_scalar_prefetch=6,
            grid=(n_tiles,),
            in_specs=[any_space] * 4,
            out_specs=pl.BlockSpec((EXP_TILE, D_MODEL), lambda j, *tables: (j, 0)),
            scratch_shapes=[
                pltpu.VMEM((2, EXP_TILE, D_SLOT), jnp.bfloat16),
                pltpu.VMEM((2, D_MODEL, D_EXPERT), jnp.float32),
                pltpu.VMEM((2, D_MODEL, D_EXPERT), jnp.float32),
                pltpu.VMEM((2, D_EXPERT, D_MODEL), jnp.float32),
                pltpu.VMEM((D_MODEL, D_EXPERT), jnp.bfloat16),
                pltpu.VMEM((D_MODEL, D_EXPERT), jnp.bfloat16),
                pltpu.VMEM((D_EXPERT, D_MODEL), jnp.bfloat16),
                pltpu.SemaphoreType.DMA((2, EXP_SUBS)),
                pltpu.SemaphoreType.DMA((3, 2)),
            ],
        ),
        out_shape=jax.ShapeDtypeStruct((n_tiles * EXP_TILE, D_MODEL), jnp.bfloat16),
        compiler_params=pltpu.CompilerParams(
            dimension_semantics=("arbitrary",), vmem_limit_bytes=VMEM_LIMIT),
        name="experts",
    )(tile_expert, n_sub, first, wslot, next_expert, chunk_src, xs_local, wg, wu, wd)


def _combine_kernel(ctab_ref, xmid_ref, rinfo_ref, g_ref, ys_hbm, out_ref, ybuf, sem, *, final_norm):
    i = pl.program_id(0)
    nt = pl.num_programs(0)
    slot = i % 2

    def start_gather(tile, s):
        for c in range(LOCAL_CHUNKS):
            _chunk_copy(ys_hbm, ctab_ref[tile * LOCAL_CHUNKS + c], ybuf.at[s], c, sem.at[s]).start()

    def wait_gather(s):
        pltpu.make_async_copy(ys_hbm.at[pl.ds(0, LOCAL_ROWS)], ybuf.at[s], sem.at[s]).wait()

    @pl.when(i == 0)
    def _():
        start_gather(0, 0)

    @pl.when(i + 1 < nt)
    def _():
        start_gather(i + 1, 1 - slot)

    wait_gather(slot)
    info = rinfo_ref[...]
    pcol = lax.broadcasted_iota(jnp.int32, (TOK_TILE, LOCAL_ROWS), 1).astype(jnp.float32)
    pick = jnp.where((pcol == info[:, 0:1]) | (pcol == info[:, 1:2]), 1.0, 0.0).astype(jnp.bfloat16)
    out = xmid_ref[...] + jnp.dot(pick, ybuf[slot], preferred_element_type=jnp.float32)
    if final_norm:
        out = _rmsnorm_f32(out, g_ref[...])
    out_ref[...] = out


def _combine(chunk_tab, xmid, rinfo, g, ys, *, final_norm):
    T = xmid.shape[0]
    return pl.pallas_call(
        functools.partial(_combine_kernel, final_norm=final_norm),
        grid_spec=pltpu.PrefetchScalarGridSpec(
            num_scalar_prefetch=1,
            grid=(T // TOK_TILE,),
            in_specs=[
                pl.BlockSpec((TOK_TILE, D_MODEL), lambda i, ct: (i, 0)),
                pl.BlockSpec((TOK_TILE, LANES), lambda i, ct: (i, 0)),
                pl.BlockSpec((1, D_MODEL), lambda i, ct: (0, 0)),
                pl.BlockSpec(memory_space=pl.ANY),
            ],
            out_specs=pl.BlockSpec((TOK_TILE, D_MODEL), lambda i, ct: (i, 0)),
            scratch_shapes=[
                pltpu.VMEM((2, LOCAL_ROWS, D_MODEL), jnp.bfloat16),
                pltpu.SemaphoreType.DMA((2,)),
            ],
        ),
        out_shape=jax.ShapeDtypeStruct((T, D_MODEL), jnp.float32),
        compiler_params=pltpu.CompilerParams(
            dimension_semantics=("arbitrary",), vmem_limit_bytes=VMEM_LIMIT),
        name="combine",
    )(chunk_tab, xmid, rinfo, g, ys)


def _pair_heads(a, axis):
    shape = a.shape
    split = shape[:axis] + (SWA_KV_HEADS, SWA_REP, HEAD_DIM) + shape[axis + 1:]
    return jnp.swapaxes(a.reshape(split), axis, axis + 1).reshape(shape)


def _in_proj_weight(w):
    off_bq = 3 * D_A
    return jnp.concatenate(
        [w[..., 0:D_A], _pair_heads(w[..., off_bq:off_bq + D_B], w.ndim - 1), w[..., D_A:off_bq],
         w[..., off_bq + D_B:]], axis=-1).astype(jnp.bfloat16)


def _out_proj_weight(w):
    return jnp.concatenate(
        [w[:, 0:D_A], _pair_heads(w[:, D_A:D_A + D_B], 1), w[:, D_A + D_B:]], axis=1).astype(jnp.bfloat16)


def _na_bias_table(rel_bias):
    c = np.arange(GRID_W)[:, None]
    cp = np.arange(GRID_W)[None, :]
    cs = np.clip(c - NA_COLS // 2, 0, GRID_W - NA_COLS)
    valid = (cp >= cs) & (cp < cs + NA_COLS)
    d = np.arange(2 * NA_COLS - 1)[:, None, None]
    col_sel = ((cp - c + (NA_COLS - 1))[None] == d) & valid[None]
    k = np.arange(NA_ROWS)[:, None, None]
    j = np.arange(NA_ROWS)[None, :, None]
    r = np.arange(2 * NA_ROWS - 1)[None, None, :]
    row_sel = r == j - k + (NA_ROWS - 1)
    tab = jnp.einsum("kjr,lhrd,dcm->lkhcjm", jnp.asarray(row_sel, jnp.float32), rel_bias.astype(jnp.float32),
                     jnp.asarray(col_sel, jnp.float32), precision=lax.Precision.HIGHEST)
    tab = jnp.where(jnp.asarray(valid)[None, None, None, :, None, :], tab * LOG2E, NEG)
    return tab.reshape(rel_bias.shape[0], NA_ROWS, NA_HEADS * GRID_W, NA_ROWS * GRID_W)


def _swa_bias_table():
    slopes = (2.0 ** (-8.0 * np.arange(1, SWA_Q_HEADS + 1) / SWA_Q_HEADS)).astype(np.float32)
    qi = np.arange(SWA_BLOCK)[:, None]
    ki = np.arange(3 * SWA_BLOCK)[None, :]
    dist = np.abs(ki - qi - SWA_BLOCK).astype(np.float32)
    tab = np.where(dist <= SWA_WINDOW, -slopes[:, None, None] * dist[None] * LOG2E, np.float32(NEG))
    tab = tab.reshape(SWA_Q_HEADS * SWA_BLOCK, 3 * SWA_BLOCK).astype(np.float32)
    first, last = tab.copy(), tab.copy()
    first[:, :SWA_BLOCK] = NEG
    last[:, 2 * SWA_BLOCK:] = NEG
    return jnp.asarray(np.stack([first, tab, last]))


def _block_diag(pool_w):
    depth, n = pool_w.shape[0:2]
    eye = jnp.asarray(np.eye(n, dtype=np.float32))
    out = pool_w[:, :, :, None, :] * eye[None, :, None, :, None]
    return out.reshape(depth, n * POOL_GROUP_DIM, n * POOL_GROUP_DIM)


def _router_weights(rg_w, rg_b, re_w, re_b):
    def lanes(g, e):
        gap = jnp.zeros(g.shape[:-1] + (ROUTE_LANE0 - N_GROUPS,), jnp.float32)
        tail = jnp.zeros(g.shape[:-1] + (LANES - ROUTE_LANE0 - N_EXPERTS,), jnp.float32)
        return jnp.concatenate([g.astype(jnp.float32), gap, e.astype(jnp.float32), tail], axis=-1)

    return lanes(rg_w, re_w).astype(jnp.bfloat16), lanes(rg_b, re_b)[:, None, :]


def _dispatch_tables(cnt, n_tiles):
    nb = cnt.shape[0] // SUBLANES
    n = cnt.reshape(nb, SUBLANES, LANES)[:, 0, ROUTE_LANE0:ROUTE_LANE0 + N_EXPERTS].astype(jnp.int32)
    g = (n + (CHUNK - 1)) // CHUNK
    l_end = jnp.cumsum(g, axis=1)
    l_off = l_end - g
    c_end = jnp.cumsum(g, axis=0)
    c_off = c_end - g
    tot = c_end[-1]
    tiles = (tot + (TILE_CHUNKS - 1)) // TILE_CHUNKS
    t_end = jnp.cumsum(tiles)
    t_off = t_end - tiles
    n_used = t_end[-1:]

    experts = jnp.arange(N_EXPERTS, dtype=jnp.int32)
    tile_ids = jnp.arange(n_tiles, dtype=jnp.int32)
    tile_expert = jnp.minimum(jnp.sum((t_end[None, :] <= tile_ids[:, None]).astype(jnp.int32), axis=1),
                              N_EXPERTS - 1)
    oh_te = (tile_expert[:, None] == experts[None, :]).astype(jnp.int32)
    left = jnp.sum(oh_te * (tot + t_off * TILE_CHUNKS)[None, :], axis=1) - tile_ids * TILE_CHUNKS
    n_sub = jnp.clip((left + (SUB_CHUNKS - 1)) // SUB_CHUNKS, 0, EXP_SUBS)
    has_rows = tiles > 0
    first = ((tile_ids == jnp.sum(oh_te * t_off[None, :], axis=1)) & (n_sub > 0)).astype(jnp.int32)
    wslot = jnp.sum(oh_te * ((jnp.cumsum(has_rows.astype(jnp.int32)) - 1) % 2)[None, :], axis=1)
    later = (experts[None, :] > experts[:, None]) & has_rows[None, :]
    nxt = jnp.min(jnp.where(later, experts[None, :], N_EXPERTS), axis=1)
    next_expert = jnp.sum(oh_te * jnp.where(nxt < N_EXPERTS, nxt, -1)[None, :], axis=1)
    tile_tables = (tile_expert, n_sub, first, wslot, next_expert)

    q = jnp.arange(n_tiles * TILE_CHUNKS, dtype=jnp.int32)
    tile_q = q // TILE_CHUNKS
    oh_e = (jnp.repeat(tile_expert, TILE_CHUNKS)[:, None] == experts[None, :]).astype(jnp.int32)
    ro = q - jnp.sum(oh_e * t_off[None, :], axis=1) * TILE_CHUNKS
    valid = (ro < jnp.sum(oh_e * tot[None, :], axis=1)) & (tile_q < n_used[0])
    cols = jnp.dot(jnp.concatenate([c_end, c_off, l_off], axis=0).astype(jnp.float32),
                   oh_e.T.astype(jnp.float32), precision=lax.Precision.HIGHEST).astype(jnp.int32)
    c_end_q, c_off_q, l_off_q = cols[0:nb], cols[nb:2 * nb], cols[2 * nb:3 * nb]
    b_q = jnp.minimum(jnp.sum((c_end_q <= ro[None, :]).astype(jnp.int32), axis=0), nb - 1)
    oh_b = (jnp.arange(nb, dtype=jnp.int32)[:, None] == b_q[None, :]).astype(jnp.int32)
    src = b_q * LOCAL_CHUNKS + jnp.sum(oh_b * (l_off_q + ro[None, :] - c_off_q), axis=0)
    chunk_src = jnp.where(valid, src, LOCAL_CHUNKS - 1)

    c = jnp.arange(LOCAL_CHUNKS, dtype=jnp.int32)
    e_c = jnp.minimum(jnp.sum((l_end[:, None, :] <= c[None, :, None]).astype(jnp.int32), axis=2),
                      N_EXPERTS - 1)
    oh_ec = (e_c[:, :, None] == experts[None, None, :]).astype(jnp.int32)
    pos = (jnp.sum(oh_ec * (t_off[None, None, :] * TILE_CHUNKS + c_off[:, None, :] - l_off[:, None, :]), axis=2)
           + c[None, :])
    chunk_tab = jnp.where(c[None, :] < l_end[:, -1:], pos, 0).reshape(-1)
    return tile_tables, chunk_src, chunk_tab


def kernel(x, norm1_g, w_in, nat_bias, swa_sink, pool_w, pool_scale, w_out, norm2_g, router_g_w,
           router_g_b, router_e_w, router_e_b, expert_w_gate, expert_w_up, expert_w_down, final_g):
    batch, seq_len, _ = x.shape
    depth = w_in.shape[0]
    T = batch * seq_len
    assert seq_len % TOK_TILE == 0 and TOK_TILE % SWA_BLOCK == 0 and TOK_TILE % GRID_W == 0
    max_chunks = (2 * T) // CHUNK + (T // TOK_TILE) * N_EXPERTS
    n_tiles = max_chunks // TILE_CHUNKS + N_EXPERTS

    swb = _swa_bias_table()
    tri = jnp.asarray(np.tril(np.ones((TOK_TILE, TOK_TILE), np.float32), -1)).astype(jnp.bfloat16)
    utri = jnp.asarray(np.triu(np.ones((LANES, LANES), np.float32), 1)).astype(jnp.bfloat16)

    w_in_k = _in_proj_weight(w_in)
    w_out_k = _out_proj_weight(w_out)
    nab = _na_bias_table(nat_bias)
    sinkcol = jnp.broadcast_to((swa_sink.astype(jnp.float32) * LOG2E)[:, :, None, None],
                               (depth, SWA_Q_HEADS, SWA_BLOCK, LANES)).reshape(depth, SWA_Q_HEADS * SWA_BLOCK, LANES)
    poolw = _block_diag(pool_w).astype(jnp.bfloat16)
    pools = pool_scale.reshape(depth, 1, D_C).astype(jnp.float32)
    rw, rb = _router_weights(router_g_w, router_g_b, router_e_w, router_e_b)
    g1 = norm1_g.reshape(depth, 1, D_MODEL)
    g2 = norm2_g.reshape(depth, 1, D_MODEL)

    x2 = x.reshape(T, D_MODEL)
    for l in range(depth):
        qp, kv = _norm_proj(l, x2, g1, w_in_k)
        xmid, xs_local, rinfo, cnt = _mixer(
            l, x2, qp, kv, (nab, sinkcol, poolw, pools, w_out_k, g2, rw, rb), (swb, tri, utri),
            batch=batch, seq_len=seq_len)
        tile_tables, chunk_src, chunk_tab = _dispatch_tables(cnt, n_tiles)
        ys = _experts(l, tile_tables, chunk_src, xs_local,
                      expert_w_gate, expert_w_up, expert_w_down)
        x2 = _combine(chunk_tab, xmid, rinfo, final_g.reshape(1, D_MODEL), ys, final_norm=(l == depth - 1))
    return x2.reshape(batch, seq_len, D_MODEL)
```

```python
import functools

import jax
import jax.numpy as jnp
import numpy as np
from jax import lax
from jax.experimental import pallas as pl
from jax.experimental.pallas import tpu as pltpu

D_MODEL = 1024
GRID_W = 64
HEAD_DIM = 64
NA_HEADS = 4
NA_ROWS = 8
NA_COLS = 16
SWA_Q_HEADS = 8
SWA_KV_HEADS = 2
SWA_REP = SWA_Q_HEADS // SWA_KV_HEADS
SWA_WINDOW = 128
SWA_BLOCK = 128
POOL_WINDOWS = (2, 4, 8, 16)
POOL_GROUP_DIM = 64
D_A = NA_HEADS * HEAD_DIM
D_B = SWA_Q_HEADS * HEAD_DIM
D_BKV = SWA_KV_HEADS * HEAD_DIM
D_C = len(POOL_WINDOWS) * POOL_GROUP_DIM
D_MIX = D_A + D_B + D_C
D_QP = D_A + D_B
D_KV = 2 * D_A + 2 * D_BKV + D_C
D_IN = D_QP + D_KV
N_GROUPS = 4
EXPERTS_PER_GROUP = 8
N_EXPERTS = N_GROUPS * EXPERTS_PER_GROUP
D_EXPERT = 256
RMS_EPS = 1e-6
NEG = -1e30
LOG2E = 1.4426950408889634
QK_SCALE = HEAD_DIM ** -0.5 * LOG2E

LANES = 128
SUBLANES = 8
ROW_CHUNKS = D_MODEL // LANES

TOK_TILE = 512
EXP_SUB = 256
EXP_SUBS = 4
EXP_TILE = EXP_SUB * EXP_SUBS
CHUNK = 16
LOCAL_CHUNKS = (2 * TOK_TILE + N_EXPERTS * (CHUNK - 1)) // CHUNK + 2
LOCAL_ROWS = LOCAL_CHUNKS * CHUNK
SUB_CHUNKS = EXP_SUB // CHUNK
TILE_CHUNKS = EXP_TILE // CHUNK
ROUTE_LANE0 = 8
D_SLOT = D_MODEL + LANES
NA_ROWS_PER_STEP = 8
SWA_BLOCKS_PER_STEP = 4
HALO = 8
VMEM_LIMIT = 56 * 1024 * 1024


def _rmsnorm_f32(x, g):
    return x * lax.rsqrt(jnp.mean(x * x, axis=-1, keepdims=True) + RMS_EPS) * g


def _norm_proj_kernel(x_ref, g_ref, w_ref, qp_ref, kv_ref):
    xn = _rmsnorm_f32(x_ref[...], g_ref[...]).astype(jnp.bfloat16)
    proj = jnp.dot(xn, w_ref[...], preferred_element_type=jnp.float32)
    qp_ref[...] = proj[:, :D_QP].astype(jnp.bfloat16)
    kv_ref[...] = proj[:, D_QP:].astype(jnp.bfloat16)


def _norm_proj(layer, x2, g, w):
    T = x2.shape[0]
    return pl.pallas_call(
        _norm_proj_kernel,
        grid=(T // TOK_TILE,),
        in_specs=[
            pl.BlockSpec((TOK_TILE, D_MODEL), lambda i: (i, 0)),
            pl.BlockSpec((None, 1, D_MODEL), lambda i: (layer, 0, 0)),
            pl.BlockSpec((None, D_MODEL, D_IN), lambda i: (layer, 0, 0)),
        ],
        out_specs=[
            pl.BlockSpec((TOK_TILE, D_QP), lambda i: (i, 0)),
            pl.BlockSpec((TOK_TILE, D_KV), lambda i: (i, 0)),
        ],
        out_shape=[
            jax.ShapeDtypeStruct((T, D_QP), jnp.bfloat16),
            jax.ShapeDtypeStruct((T, D_KV), jnp.bfloat16),
        ],
        compiler_params=pltpu.CompilerParams(
            dimension_semantics=("arbitrary",), vmem_limit_bytes=VMEM_LIMIT),
        name="norm_proj",
    )(x2, g, w)


KW_AK, KW_AV, KW_BK, KW_BV = 0, D_A, 2 * D_A, 2 * D_A + D_BKV
KW_COLS = 2 * D_A + 2 * D_BKV
KV_CU = KW_COLS


def _mixer_kernel(x_ref, qp_ref, kvp_ref, kvc_ref, kvn_ref, nab_ref, swb_ref, sink_ref,
                  poolw_ref, pools_ref, wout_ref, g2_ref, rw_ref, rb_ref, tri_ref, utri_ref,
                  xmid_ref, xs_ref, rinfo_ref, cnt_ref,
                  kwin, uwin, mix, xn_scr, logit_scr, *, seq_len):
    t = pl.program_id(0)
    nblk = seq_len // TOK_TILE
    i = jnp.minimum(t, pl.num_programs(0) - 2) % nblk
    rows_per_tile = TOK_TILE // GRID_W
    grid_rows = seq_len // GRID_W

    @pl.when(t == 0)
    def _():
        xn_scr[...] = jnp.zeros_like(xn_scr)
        logit_scr[...] = jnp.zeros_like(logit_scr)

    kwin[0:TOK_TILE, :] = kvp_ref[:, 0:KW_COLS]
    kwin[TOK_TILE:2 * TOK_TILE, :] = kvc_ref[:, 0:KW_COLS]
    kwin[2 * TOK_TILE:3 * TOK_TILE, :] = kvn_ref[:, 0:KW_COLS]

    lane_a = lax.broadcasted_iota(jnp.int32, (GRID_W, D_A), 1) // HEAD_DIM

    def na_row(rr):
        r = i * rows_per_tile + rr
        rs = jnp.clip(r - NA_ROWS // 2, 0, grid_rows - NA_ROWS)
        variant = r - rs
        start = pl.multiple_of((rs - i * rows_per_tile + rows_per_tile) * GRID_W, GRID_W)
        q0 = pl.multiple_of(rr * GRID_W, GRID_W)
        q = (qp_ref[pl.ds(q0, GRID_W), 0:D_A].astype(jnp.float32) * QK_SCALE).astype(jnp.bfloat16)
        zero = jnp.zeros_like(q)
        qs = jnp.concatenate([jnp.where(lane_a == h, q, zero) for h in range(NA_HEADS)], axis=0)
        kw = kwin[pl.ds(start, NA_ROWS * GRID_W), KW_AK:KW_AK + D_A]
        vw = kwin[pl.ds(start, NA_ROWS * GRID_W), KW_AV:KW_AV + D_A]
        s = lax.dot_general(qs, kw, (((1,), (1,)), ((), ())), preferred_element_type=jnp.float32)
        s = s + nab_ref[variant]
        m = jnp.max(s, axis=-1, keepdims=True)
        p = jnp.exp2(s - m)
        l = jnp.sum(p, axis=-1, keepdims=True)
        pv = jnp.dot(p.astype(jnp.bfloat16), vw, preferred_element_type=jnp.float32)
        pv = pv * (1.0 / l)
        o = jnp.zeros((GRID_W, D_A), jnp.float32)
        for h in range(NA_HEADS):
            o = o + jnp.where(lane_a == h, pv[h * GRID_W:(h + 1) * GRID_W, :], 0.0)
        mix[pl.ds(q0, GRID_W), 0:D_A] = o.astype(jnp.bfloat16)

    def na_step(it, c):
        for k in range(NA_ROWS_PER_STEP):
            na_row(it * NA_ROWS_PER_STEP + k)
        return c

    lax.fori_loop(0, rows_per_tile // NA_ROWS_PER_STEP, na_step, 0)

    lane_b = lax.broadcasted_iota(jnp.int32, (SWA_BLOCK, LANES), 1) // HEAD_DIM
    blocks_per_tile = TOK_TILE // SWA_BLOCK
    nblocks = seq_len // SWA_BLOCK

    ones_v = jnp.ones((3 * SWA_BLOCK, LANES), jnp.bfloat16)

    def swa_step(sb, c):
        n = i * blocks_per_tile + sb
        variant = jnp.where(n == 0, 0, jnp.where(n == nblocks - 1, 2, 1))
        q0 = pl.multiple_of(sb * SWA_BLOCK, SWA_BLOCK)
        k0 = pl.multiple_of(TOK_TILE - SWA_BLOCK + sb * SWA_BLOCK, SWA_BLOCK)
        kw = kwin[pl.ds(k0, 3 * SWA_BLOCK), KW_BK:KW_BK + D_BKV]
        vaug = jnp.concatenate([kwin[pl.ds(k0, 3 * SWA_BLOCK), KW_BV:KW_BV + D_BKV], ones_v], axis=1)
        outs = []
        for g in range(SWA_KV_HEADS):
            pieces = []
            for t in range(SWA_REP):
                qt = qp_ref[pl.ds(q0, SWA_BLOCK), D_A + t * LANES:D_A + (t + 1) * LANES]
                qt = (qt.astype(jnp.float32) * QK_SCALE).astype(jnp.bfloat16)
                pieces.append(jnp.where(lane_b == g, qt, jnp.zeros_like(qt)))
            qs = jnp.concatenate(pieces, axis=0)
            r0 = g * SWA_REP * SWA_BLOCK
            s = lax.dot_general(qs, kw, (((1,), (1,)), ((), ())), preferred_element_type=jnp.float32)
            s = s + swb_ref[variant, r0:r0 + SWA_REP * SWA_BLOCK, :]
            sink = sink_ref[r0:r0 + SWA_REP * SWA_BLOCK, :]
            m = jnp.broadcast_to(jnp.max(s, axis=-1, keepdims=True), sink.shape)
            m = jnp.maximum(m, sink)
            p = jnp.exp2(s - jnp.concatenate([m, m, m], axis=1)).astype(jnp.bfloat16)
            pv = jnp.dot(p, vaug, preferred_element_type=jnp.float32)
            l = pv[:, LANES:2 * LANES] + jnp.exp2(sink - m)
            outs.append(pv[:, 0:LANES] * (1.0 / l))
        for t in range(SWA_REP):
            o0 = outs[0][t * SWA_BLOCK:(t + 1) * SWA_BLOCK, :]
            o1 = outs[1][t * SWA_BLOCK:(t + 1) * SWA_BLOCK, :]
            ot = jnp.where(lane_b == 0, o0, o1)
            mix[pl.ds(q0, SWA_BLOCK), D_A + t * LANES:D_A + (t + 1) * LANES] = ot.astype(jnp.bfloat16)
        return c

    def swa_pair(it, c):
        for k in range(SWA_BLOCKS_PER_STEP):
            swa_step(it * SWA_BLOCKS_PER_STEP + k, c)
        return c

    lax.fori_loop(0, blocks_per_tile // SWA_BLOCKS_PER_STEP, swa_pair, 0)

    u = kvc_ref[:, KV_CU:KV_CU + D_C].astype(jnp.float32)
    prev_ok = (i > 0).astype(jnp.float32)
    next_ok = (i < nblk - 1).astype(jnp.float32)
    uwin[0:HALO, :] = kvp_ref[TOK_TILE - HALO:TOK_TILE, KV_CU:KV_CU + D_C].astype(jnp.float32) * prev_ok
    uwin[HALO:HALO + TOK_TILE, :] = u
    uwin[HALO + TOK_TILE:2 * HALO + TOK_TILE, :] = kvn_ref[0:HALO, KV_CU:KV_CU + D_C].astype(jnp.float32) * next_ok
    n_ext = TOK_TILE + 2 * HALO
    a2 = uwin[0:n_ext - 1, :] + uwin[1:n_ext, :]
    a4 = a2[0:n_ext - 3, :] + a2[2:n_ext - 1, :]
    a8 = a4[0:n_ext - 7, :] + a4[4:n_ext - 3, :]
    a16 = a8[0:n_ext - 15, :] + a8[8:n_ext - 7, :]
    w2 = a2[7:7 + TOK_TILE, :]
    w4 = a4[6:6 + TOK_TILE, :]
    w8 = a8[4:4 + TOK_TILE, :]
    w16 = a16[0:TOK_TILE, :]
    lane_c = lax.broadcasted_iota(jnp.int32, (TOK_TILE, D_C), 1) // POOL_GROUP_DIM
    pooled = jnp.where(lane_c == 0, w2, jnp.where(lane_c == 1, w4, jnp.where(lane_c == 2, w8, w16)))
    half = jnp.where(lane_c == 0, 1, jnp.where(lane_c == 1, 2, jnp.where(lane_c == 2, 4, 8)))
    pos = i * TOK_TILE + lax.broadcasted_iota(jnp.int32, (TOK_TILE, D_C), 0)
    cnt = (jnp.minimum(pos + half, seq_len) - jnp.maximum(pos - half, 0)).astype(jnp.float32)
    d = (pooled / cnt - u).astype(jnp.bfloat16)
    oc = jnp.dot(d, poolw_ref[...], preferred_element_type=jnp.float32) * pools_ref[...]
    mix[:, D_A + D_B:D_MIX] = oc.astype(jnp.bfloat16)

    xn = xn_scr[...]
    logits = logit_scr[...]

    xm = x_ref[...] + jnp.dot(mix[...], wout_ref[...], preferred_element_type=jnp.float32)
    xmid_ref[...] = xm
    xn_new = _rmsnorm_f32(xm, g2_ref[...]).astype(jnp.bfloat16)
    xn_scr[...] = xn_new
    logit_scr[...] = jnp.dot(xn_new, rw_ref[...], preferred_element_type=jnp.float32) + rb_ref[...]

    lane = lax.broadcasted_iota(jnp.int32, (TOK_TILE, LANES), 1).astype(jnp.float32)
    is_g = lane < N_GROUPS
    gl = jnp.where(is_g, logits, NEG)
    gmax = jnp.max(gl, axis=-1, keepdims=True)
    gtop = jnp.min(jnp.where(is_g & (gl == gmax), lane, float(LANES)), axis=-1, keepdims=True)
    gprob = 1.0 / jnp.sum(jnp.exp(gl - gmax), axis=-1, keepdims=True)
    e_lo = ROUTE_LANE0 + gtop * EXPERTS_PER_GROUP
    in_grp = (lane >= e_lo) & (lane < e_lo + EXPERTS_PER_GROUP)
    el = jnp.where(in_grp, logits, NEG)
    m1 = jnp.max(el, axis=-1, keepdims=True)
    i1 = jnp.min(jnp.where(in_grp & (el == m1), lane, float(LANES)), axis=-1, keepdims=True)
    el2 = jnp.where(lane == i1, NEG, el)
    m2 = jnp.max(el2, axis=-1, keepdims=True)
    i2 = jnp.min(jnp.where(in_grp & (lane != i1) & (el2 == m2), lane, float(LANES)), axis=-1, keepdims=True)
    r21 = jnp.exp(m2 - m1)
    gate1 = gprob / (1.0 + r21)
    gate2 = gprob * r21 / (1.0 + r21)

    oh1 = lane == i1
    oh2 = lane == i2
    oh = jnp.where(oh1 | oh2, 1.0, 0.0)
    earlier = jnp.dot(tri_ref[...], oh.astype(jnp.bfloat16), preferred_element_type=jnp.float32)
    n_e = jnp.sum(oh, axis=0, keepdims=True)
    chunks_e = jnp.floor((n_e + (CHUNK - 1)) * (1.0 / CHUNK))
    seg0 = jnp.dot(jnp.broadcast_to(chunks_e, (SUBLANES, LANES)).astype(jnp.bfloat16), utri_ref[...],
                   preferred_element_type=jnp.float32)[0:1, :] * CHUNK
    base = earlier + seg0
    lp1 = jnp.sum(jnp.where(oh1, base, 0.0), axis=-1, keepdims=True)
    lp2 = jnp.sum(jnp.where(oh2, base, 0.0), axis=-1, keepdims=True)
    info = jnp.where(lane == 0, lp1, jnp.where(lane == 1, lp2, 0.0))
    rinfo_ref[...] = info
    cnt_ref[...] = jnp.broadcast_to(n_e, cnt_ref.shape)

    def pieces(g):
        hi = g.astype(jnp.bfloat16).astype(jnp.float32)
        mid = (g - hi).astype(jnp.bfloat16).astype(jnp.float32)
        return hi, mid, g - hi - mid

    aux = jnp.zeros((TOK_TILE, LANES), jnp.float32)
    for k, piece in enumerate(pieces(gate1) + pieces(gate2) + (i1,)):
        aux = jnp.where(lane == k, piece, aux)

    info_t = info.T
    prow = lax.broadcasted_iota(jnp.int32, (LOCAL_ROWS, TOK_TILE), 0).astype(jnp.float32)
    sel = jnp.where((prow == info_t[0:1, :]) | (prow == info_t[1:2, :]), 1.0, 0.0).astype(jnp.bfloat16)
    moved = jnp.dot(sel, jnp.concatenate([xn, aux.astype(jnp.bfloat16)], axis=1),
                    preferred_element_type=jnp.float32)
    xs_ref[...] = moved.astype(jnp.bfloat16)


def _mixer(layer, x2, qp, kv, layer_params, shared_tables, *, batch, seq_len):
    nab, sinkcol, poolw, pools, wout, g2, rw, rb = layer_params
    swb, tri, utri = shared_tables
    T = x2.shape[0]
    nblk = seq_len // TOK_TILE
    n_tok_tiles = T // TOK_TILE

    def mixed(t):
        return jnp.minimum(t, n_tok_tiles - 1)

    def cur(t):
        return (mixed(t), 0)

    def prev(t):
        return (jnp.maximum(mixed(t) - 1, (mixed(t) // nblk) * nblk), 0)

    def nxt(t):
        return (jnp.minimum(mixed(t) + 1, (mixed(t) // nblk) * nblk + nblk - 1), 0)

    def dispatched(t):
        return (jnp.maximum(t - 1, 0), 0)

    def resident(a):
        zeros = (0,) * a.ndim
        return pl.BlockSpec(a.shape, lambda t: zeros, pipeline_mode=pl.Buffered(1))

    def resident_layer(a):
        index = (layer,) + (0,) * (a.ndim - 1)
        return pl.BlockSpec((None,) + a.shape[1:], lambda t: index, pipeline_mode=pl.Buffered(1))

    return pl.pallas_call(
        functools.partial(_mixer_kernel, seq_len=seq_len),
        grid=(n_tok_tiles + 1,),
        in_specs=[
            pl.BlockSpec((TOK_TILE, D_MODEL), cur),
            pl.BlockSpec((TOK_TILE, D_QP), cur),
            pl.BlockSpec((TOK_TILE, D_KV), prev),
            pl.BlockSpec((TOK_TILE, D_KV), cur),
            pl.BlockSpec((TOK_TILE, D_KV), nxt),
            resident_layer(nab), resident(swb), resident_layer(sinkcol), resident_layer(poolw),
            resident_layer(pools), resident_layer(wout), resident_layer(g2), resident_layer(rw),
            resident_layer(rb), resident(tri), resident(utri),
        ],
        out_specs=[
            pl.BlockSpec((TOK_TILE, D_MODEL), cur),
            pl.BlockSpec((LOCAL_ROWS, D_SLOT), dispatched),
            pl.BlockSpec((TOK_TILE, LANES), dispatched),
            pl.BlockSpec((SUBLANES, LANES), dispatched),
        ],
        out_shape=[
            jax.ShapeDtypeStruct((T, D_MODEL), jnp.float32),
            jax.ShapeDtypeStruct((n_tok_tiles * LOCAL_ROWS, D_SLOT), jnp.bfloat16),
            jax.ShapeDtypeStruct((T, LANES), jnp.float32),
            jax.ShapeDtypeStruct((n_tok_tiles * SUBLANES, LANES), jnp.float32),
        ],
        scratch_shapes=[
            pltpu.VMEM((3 * TOK_TILE, KW_COLS), jnp.bfloat16),
            pltpu.VMEM((TOK_TILE + 2 * HALO, D_C), jnp.float32),
            pltpu.VMEM((TOK_TILE, D_MIX), jnp.bfloat16),
            pltpu.VMEM((TOK_TILE, D_MODEL), jnp.bfloat16),
            pltpu.VMEM((TOK_TILE, LANES), jnp.float32),
        ],
        compiler_params=pltpu.CompilerParams(
            dimension_semantics=("arbitrary",), vmem_limit_bytes=VMEM_LIMIT),
        name="mixer",
    )(x2, qp, kv, kv, kv, nab, swb, sinkcol, poolw, pools, wout, g2, rw, rb, tri, utri)


def _chunk_copy(src_hbm, src_chunk, dst, dst_chunk, sem):
    return pltpu.make_async_copy(
        src_hbm.at[pl.ds(pl.multiple_of(src_chunk * CHUNK, CHUNK), CHUNK)],
        dst.at[pl.ds(dst_chunk * CHUNK, CHUNK)],
        sem)


def _expert_kernel(te_ref, nsub_ref, first_ref, wslot_ref, nexte_ref, csrc_ref,
                   xs_hbm, wg_hbm, wu_hbm, wd_hbm, y_ref,
                   xbuf, wgf, wuf, wdf, wgb, wub, wdb, sem, wsem, *, layer):
    j = pl.program_id(0)
    nt = pl.num_programs(0)
    slot = j % 2

    def start_gather(tile, s, h):
        for c in range(h * SUB_CHUNKS, (h + 1) * SUB_CHUNKS):
            _chunk_copy(xs_hbm, csrc_ref[tile * TILE_CHUNKS + c], xbuf.at[s], c, sem.at[s, h]).start()

    def wait_gather(s, h):
        rows = pl.ds(h * EXP_SUB, EXP_SUB)
        pltpu.make_async_copy(xs_hbm.at[pl.ds(0, EXP_SUB)], xbuf.at[s, rows], sem.at[s, h]).wait()

    def weight_copies(expert, ws):
        return [pltpu.make_async_copy(w_hbm.at[layer, expert], wbuf.at[ws], wsem.at[k, ws])
                for k, (w_hbm, wbuf) in enumerate(((wg_hbm, wgf), (wu_hbm, wuf), (wd_hbm, wdf)))]

    for h in range(EXP_SUBS):
        @pl.when((j == 0) & (h < nsub_ref[0]))
        def _():
            start_gather(0, 0, h)

        @pl.when(h < nsub_ref[jnp.minimum(j + 1, nt - 1)] * (j + 1 < nt).astype(jnp.int32))
        def _():
            start_gather(j + 1, 1 - slot, h)

    ws = wslot_ref[j]

    @pl.when(j == 0)
    def _():
        for cp in weight_copies(te_ref[0], 0):
            cp.start()

    @pl.when(first_ref[j] > 0)
    def _():
        for cp in weight_copies(te_ref[j], ws):
            cp.wait()

        @pl.when(nexte_ref[j] >= 0)
        def _():
            for cp in weight_copies(nexte_ref[j], 1 - ws):
                cp.start()

        wgb[...] = wgf[ws].astype(jnp.bfloat16)
        wub[...] = wuf[ws].astype(jnp.bfloat16)
        wdb[...] = wdf[ws].astype(jnp.bfloat16)

    def gated_mlp(n_rows):
        xs = xbuf[slot, 0:n_rows, 0:D_MODEL]
        gp = xbuf[slot, 0:n_rows, D_MODEL:D_SLOT].astype(jnp.float32)
        first = gp[:, 6:7] == (te_ref[j] + ROUTE_LANE0).astype(jnp.float32)
        gate_w = jnp.where(first, gp[:, 0:1] + gp[:, 1:2] + gp[:, 2:3], gp[:, 3:4] + gp[:, 4:5] + gp[:, 5:6])
        gate = jnp.dot(xs, wgb[...], preferred_element_type=jnp.float32)
        up = jnp.dot(xs, wub[...], preferred_element_type=jnp.float32)
        act = (gate * (1.0 / (1.0 + jnp.exp(-gate))) * up * gate_w).astype(jnp.bfloat16)
        return jnp.dot(act, wdb[...], preferred_element_type=jnp.float32).astype(jnp.bfloat16)

    for k in range(1, EXP_SUBS + 1):
        @pl.when(nsub_ref[j] == k)
        def _():
            for h in range(k):
                wait_gather(slot, h)
            y_ref[0:k * EXP_SUB, :] = gated_mlp(k * EXP_SUB)
            if k < EXP_SUBS:
                y_ref[k * EXP_SUB:EXP_TILE, :] = jnp.zeros((EXP_TILE - k * EXP_SUB, D_MODEL), jnp.bfloat16)

    @pl.when(nsub_ref[j] == 0)
    def _():
        y_ref[...] = jnp.zeros_like(y_ref)


def _experts(layer, tile_tables, chunk_src, xs_local, wg, wu, wd):
    tile_expert, n_sub, first, wslot, next_expert = tile_tables
    n_tiles = tile_expert.shape[0]
    any_space = pl.BlockSpec(memory_space=pl.ANY)

    return pl.pallas_call(
        functools.partial(_expert_kernel, layer=layer),
        grid_spec=pltpu.PrefetchScalarGridSpec(
            num_scalar_prefetch=6,
            grid=(n_tiles,),
            in_specs=[any_space] * 4,
            out_specs=pl.BlockSpec((EXP_TILE, D_MODEL), lambda j, *tables: (j, 0)),
            scratch_shapes=[
                pltpu.VMEM((2, EXP_TILE, D_SLOT), jnp.bfloat16),
                pltpu.VMEM((2, D_MODEL, D_EXPERT), jnp.float32),
                pltpu.VMEM((2, D_MODEL, D_EXPERT), jnp.float32),
                pltpu.VMEM((2, D_EXPERT, D_MODEL), jnp.float32),
                pltpu.VMEM((D_MODEL, D_EXPERT), jnp.bfloat16),
                pltpu.VMEM((D_MODEL, D_EXPERT), jnp.bfloat16),
                pltpu.VMEM((D_EXPERT, D_MODEL), jnp.bfloat16),
                pltpu.SemaphoreType.DMA((2, EXP_SUBS)),
                pltpu.SemaphoreType.DMA((3, 2)),
            ],
        ),
        out_shape=jax.ShapeDtypeStruct((n_tiles * EXP_TILE, D_MODEL), jnp.bfloat16),
        compiler_params=pltpu.CompilerParams(
            dimension_semantics=("arbitrary",), vmem_limit_bytes=VMEM_LIMIT),
        name="experts",
    )(tile_expert, n_sub, first, wslot, next_expert, chunk_src, xs_local, wg, wu, wd)


def _combine_tile(ctab_ref, xmid_ref, rinfo_ref, ys_hbm, ybuf, sem):
    i = pl.program_id(0)
    nt = pl.num_programs(0)
    slot = i % 2

    def start_gather(tile, s):
        for c in range(LOCAL_CHUNKS):
            _chunk_copy(ys_hbm, ctab_ref[tile * LOCAL_CHUNKS + c], ybuf.at[s], c, sem.at[s]).start()

    def wait_gather(s):
        pltpu.make_async_copy(ys_hbm.at[pl.ds(0, LOCAL_ROWS)], ybuf.at[s], sem.at[s]).wait()

    @pl.when(i == 0)
    def _():
        start_gather(0, 0)

    @pl.when(i + 1 < nt)
    def _():
        start_gather(i + 1, 1 - slot)

    wait_gather(slot)
    info = rinfo_ref[...]
    pcol = lax.broadcasted_iota(jnp.int32, (TOK_TILE, LOCAL_ROWS), 1).astype(jnp.float32)
    pick = jnp.where((pcol == info[:, 0:1]) | (pcol == info[:, 1:2]), 1.0, 0.0).astype(jnp.bfloat16)
    return xmid_ref[...] + jnp.dot(pick, ybuf[slot], preferred_element_type=jnp.float32)


def _combine_final_kernel(ctab_ref, xmid_ref, rinfo_ref, g_ref, ys_hbm, out_ref, ybuf, sem):
    x = _combine_tile(ctab_ref, xmid_ref, rinfo_ref, ys_hbm, ybuf, sem)
    out_ref[...] = _rmsnorm_f32(x, g_ref[...])


def _combine_proj_kernel(ctab_ref, xmid_ref, rinfo_ref, g_ref, w_ref, ys_hbm, x_ref, qp_ref, kv_ref, ybuf, sem):
    x = _combine_tile(ctab_ref, xmid_ref, rinfo_ref, ys_hbm, ybuf, sem)
    x_ref[...] = x
    xn = _rmsnorm_f32(x, g_ref[...]).astype(jnp.bfloat16)
    proj = jnp.dot(xn, w_ref[...], preferred_element_type=jnp.float32)
    qp_ref[...] = proj[:, :D_QP].astype(jnp.bfloat16)
    kv_ref[...] = proj[:, D_QP:].astype(jnp.bfloat16)


def _combine_proj(next_layer, chunk_tab, xmid, rinfo, g1, w_in_k, ys):
    T = xmid.shape[0]

    def tile(i, ct):
        return (i, 0)

    return pl.pallas_call(
        _combine_proj_kernel,
        grid_spec=pltpu.PrefetchScalarGridSpec(
            num_scalar_prefetch=1,
            grid=(T // TOK_TILE,),
            in_specs=[
                pl.BlockSpec((TOK_TILE, D_MODEL), tile),
                pl.BlockSpec((TOK_TILE, LANES), tile),
                pl.BlockSpec((None, 1, D_MODEL), lambda i, ct: (next_layer, 0, 0)),
                pl.BlockSpec((None, D_MODEL, D_IN), lambda i, ct: (next_layer, 0, 0),
                             pipeline_mode=pl.Buffered(1)),
                pl.BlockSpec(memory_space=pl.ANY),
            ],
            out_specs=[
                pl.BlockSpec((TOK_TILE, D_MODEL), tile),
                pl.BlockSpec((TOK_TILE, D_QP), tile),
                pl.BlockSpec((TOK_TILE, D_KV), tile),
            ],
            scratch_shapes=[
                pltpu.VMEM((2, LOCAL_ROWS, D_MODEL), jnp.bfloat16),
                pltpu.SemaphoreType.DMA((2,)),
            ],
        ),
        out_shape=[
            jax.ShapeDtypeStruct((T, D_MODEL), jnp.float32),
            jax.ShapeDtypeStruct((T, D_QP), jnp.bfloat16),
            jax.ShapeDtypeStruct((T, D_KV), jnp.bfloat16),
        ],
        compiler_params=pltpu.CompilerParams(
            dimension_semantics=("arbitrary",), vmem_limit_bytes=VMEM_LIMIT),
        name="combine_proj",
    )(chunk_tab, xmid, rinfo, g1, w_in_k, ys)


def _combine_final(chunk_tab, xmid, rinfo, g, ys):
    T = xmid.shape[0]
    return pl.pallas_call(
        _combine_final_kernel,
        grid_spec=pltpu.PrefetchScalarGridSpec(
            num_scalar_prefetch=1,
            grid=(T // TOK_TILE,),
            in_specs=[
                pl.BlockSpec((TOK_TILE, D_MODEL), lambda i, ct: (i, 0)),
                pl.BlockSpec((TOK_TILE, LANES), lambda i, ct: (i, 0)),
                pl.BlockSpec((1, D_MODEL), lambda i, ct: (0, 0)),
                pl.BlockSpec(memory_space=pl.ANY),
            ],
            out_specs=pl.BlockSpec((TOK_TILE, D_MODEL), lambda i, ct: (i, 0)),
            scratch_shapes=[
                pltpu.VMEM((2, LOCAL_ROWS, D_MODEL), jnp.bfloat16),
                pltpu.SemaphoreType.DMA((2,)),
            ],
        ),
        out_shape=jax.ShapeDtypeStruct((T, D_MODEL), jnp.float32),
        compiler_params=pltpu.CompilerParams(
            dimension_semantics=("arbitrary",), vmem_limit_bytes=VMEM_LIMIT),
        name="combine",
    )(chunk_tab, xmid, rinfo, g, ys)


def _pair_heads(a, axis):
    shape = a.shape
    split = shape[:axis] + (SWA_KV_HEADS, SWA_REP, HEAD_DIM) + shape[axis + 1:]
    return jnp.swapaxes(a.reshape(split), axis, axis + 1).reshape(shape)


def _in_proj_weight(w):
    off_bq = 3 * D_A
    return jnp.concatenate(
        [w[..., 0:D_A], _pair_heads(w[..., off_bq:off_bq + D_B], w.ndim - 1), w[..., D_A:off_bq],
         w[..., off_bq + D_B:]], axis=-1).astype(jnp.bfloat16)


def _out_proj_weight(w):
    return jnp.concatenate(
        [w[:, 0:D_A], _pair_heads(w[:, D_A:D_A + D_B], 1), w[:, D_A + D_B:]], axis=1).astype(jnp.bfloat16)


def _na_bias_table(rel_bias):
    c = np.arange(GRID_W)[:, None]
    cp = np.arange(GRID_W)[None, :]
    cs = np.clip(c - NA_COLS // 2, 0, GRID_W - NA_COLS)
    valid = (cp >= cs) & (cp < cs + NA_COLS)
    d = np.arange(2 * NA_COLS - 1)[:, None, None]
    col_sel = ((cp - c + (NA_COLS - 1))[None] == d) & valid[None]
    k = np.arange(NA_ROWS)[:, None, None]
    j = np.arange(NA_ROWS)[None, :, None]
    r = np.arange(2 * NA_ROWS - 1)[None, None, :]
    row_sel = r == j - k + (NA_ROWS - 1)
    tab = jnp.einsum("kjr,lhrd,dcm->lkhcjm", jnp.asarray(row_sel, jnp.float32), rel_bias.astype(jnp.float32),
                     jnp.asarray(col_sel, jnp.float32), precision=lax.Precision.HIGHEST)
    tab = jnp.where(jnp.asarray(valid)[None, None, None, :, None, :], tab * LOG2E, NEG)
    return tab.reshape(rel_bias.shape[0], NA_ROWS, NA_HEADS * GRID_W, NA_ROWS * GRID_W)


def _swa_bias_table():
    slopes = (2.0 ** (-8.0 * np.arange(1, SWA_Q_HEADS + 1) / SWA_Q_HEADS)).astype(np.float32)
    qi = np.arange(SWA_BLOCK)[:, None]
    ki = np.arange(3 * SWA_BLOCK)[None, :]
    dist = np.abs(ki - qi - SWA_BLOCK).astype(np.float32)
    tab = np.where(dist <= SWA_WINDOW, -slopes[:, None, None] * dist[None] * LOG2E, np.float32(NEG))
    tab = tab.reshape(SWA_Q_HEADS * SWA_BLOCK, 3 * SWA_BLOCK).astype(np.float32)
    first, last = tab.copy(), tab.copy()
    first[:, :SWA_BLOCK] = NEG
    last[:, 2 * SWA_BLOCK:] = NEG
    return jnp.asarray(np.stack([first, tab, last]))


def _block_diag(pool_w):
    depth, n = pool_w.shape[0:2]
    eye = jnp.asarray(np.eye(n, dtype=np.float32))
    out = pool_w[:, :, :, None, :] * eye[None, :, None, :, None]
    return out.reshape(depth, n * POOL_GROUP_DIM, n * POOL_GROUP_DIM)


def _router_weights(rg_w, rg_b, re_w, re_b):
    def lanes(g, e):
        gap = jnp.zeros(g.shape[:-1] + (ROUTE_LANE0 - N_GROUPS,), jnp.float32)
        tail = jnp.zeros(g.shape[:-1] + (LANES - ROUTE_LANE0 - N_EXPERTS,), jnp.float32)
        return jnp.concatenate([g.astype(jnp.float32), gap, e.astype(jnp.float32), tail], axis=-1)

    return lanes(rg_w, re_w).astype(jnp.bfloat16), lanes(rg_b, re_b)[:, None, :]


def _dispatch_tables(cnt, n_tiles):
    nb = cnt.shape[0] // SUBLANES
    n = cnt.reshape(nb, SUBLANES, LANES)[:, 0, ROUTE_LANE0:ROUTE_LANE0 + N_EXPERTS].astype(jnp.int32)
    g = (n + (CHUNK - 1)) // CHUNK
    l_end = jnp.cumsum(g, axis=1)
    l_off = l_end - g
    c_end = jnp.cumsum(g, axis=0)
    c_off = c_end - g
    tot = c_end[-1]
    tiles = (tot + (TILE_CHUNKS - 1)) // TILE_CHUNKS
    t_end = jnp.cumsum(tiles)
    t_off = t_end - tiles
    n_used = t_end[-1:]

    experts = jnp.arange(N_EXPERTS, dtype=jnp.int32)
    tile_ids = jnp.arange(n_tiles, dtype=jnp.int32)
    tile_expert = jnp.minimum(jnp.sum((t_end[None, :] <= tile_ids[:, None]).astype(jnp.int32), axis=1),
                              N_EXPERTS - 1)
    oh_te = (tile_expert[:, None] == experts[None, :]).astype(jnp.int32)
    left = jnp.sum(oh_te * (tot + t_off * TILE_CHUNKS)[None, :], axis=1) - tile_ids * TILE_CHUNKS
    n_sub = jnp.clip((left + (SUB_CHUNKS - 1)) // SUB_CHUNKS, 0, EXP_SUBS)
    has_rows = tiles > 0
    first = ((tile_ids == jnp.sum(oh_te * t_off[None, :], axis=1)) & (n_sub > 0)).astype(jnp.int32)
    wslot = jnp.sum(oh_te * ((jnp.cumsum(has_rows.astype(jnp.int32)) - 1) % 2)[None, :], axis=1)
    later = (experts[None, :] > experts[:, None]) & has_rows[None, :]
    nxt = jnp.min(jnp.where(later, experts[None, :], N_EXPERTS), axis=1)
    next_expert = jnp.sum(oh_te * jnp.where(nxt < N_EXPERTS, nxt, -1)[None, :], axis=1)
    tile_tables = (tile_expert, n_sub, first, wslot, next_expert)

    q = jnp.arange(n_tiles * TILE_CHUNKS, dtype=jnp.int32)
    tile_q = q // TILE_CHUNKS
    oh_e = (jnp.repeat(tile_expert, TILE_CHUNKS)[:, None] == experts[None, :]).astype(jnp.int32)
    ro = q - jnp.sum(oh_e * t_off[None, :], axis=1) * TILE_CHUNKS
    valid = (ro < jnp.sum(oh_e * tot[None, :], axis=1)) & (tile_q < n_used[0])
    cols = jnp.dot(jnp.concatenate([c_end, c_off, l_off], axis=0).astype(jnp.float32),
                   oh_e.T.astype(jnp.float32), precision=lax.Precision.HIGHEST).astype(jnp.int32)
    c_end_q, c_off_q, l_off_q = cols[0:nb], cols[nb:2 * nb], cols[2 * nb:3 * nb]
    b_q = jnp.minimum(jnp.sum((c_end_q <= ro[None, :]).astype(jnp.int32), axis=0), nb - 1)
    oh_b = (jnp.arange(nb, dtype=jnp.int32)[:, None] == b_q[None, :]).astype(jnp.int32)
    src = b_q * LOCAL_CHUNKS + jnp.sum(oh_b * (l_off_q + ro[None, :] - c_off_q), axis=0)
    chunk_src = jnp.where(valid, src, LOCAL_CHUNKS - 1)

    c = jnp.arange(LOCAL_CHUNKS, dtype=jnp.int32)
    e_c = jnp.minimum(jnp.sum((l_end[:, None, :] <= c[None, :, None]).astype(jnp.int32), axis=2),
                      N_EXPERTS - 1)
    oh_ec = (e_c[:, :, None] == experts[None, None, :]).astype(jnp.int32)
    pos = (jnp.sum(oh_ec * (t_off[None, None, :] * TILE_CHUNKS + c_off[:, None, :] - l_off[:, None, :]), axis=2)
           + c[None, :])
    chunk_tab = jnp.where(c[None, :] < l_end[:, -1:], pos, 0).reshape(-1)
    return tile_tables, chunk_src, chunk_tab


def kernel(x, norm1_g, w_in, nat_bias, swa_sink, pool_w, pool_scale, w_out, norm2_g, router_g_w,
           router_g_b, router_e_w, router_e_b, expert_w_gate, expert_w_up, expert_w_down, final_g):
    batch, seq_len, _ = x.shape
    depth = w_in.shape[0]
    T = batch * seq_len
    assert seq_len % TOK_TILE == 0 and TOK_TILE % SWA_BLOCK == 0 and TOK_TILE % GRID_W == 0
    max_chunks = (2 * T) // CHUNK + (T // TOK_TILE) * N_EXPERTS
    n_tiles = max_chunks // TILE_CHUNKS + N_EXPERTS

    swb = _swa_bias_table()
    tri = jnp.asarray(np.tril(np.ones((TOK_TILE, TOK_TILE), np.float32), -1)).astype(jnp.bfloat16)
    utri = jnp.asarray(np.triu(np.ones((LANES, LANES), np.float32), 1)).astype(jnp.bfloat16)

    w_in_k = _in_proj_weight(w_in)
    w_out_k = _out_proj_weight(w_out)
    nab = _na_bias_table(nat_bias)
    sinkcol = jnp.broadcast_to((swa_sink.astype(jnp.float32) * LOG2E)[:, :, None, None],
                               (depth, SWA_Q_HEADS, SWA_BLOCK, LANES)).reshape(depth, SWA_Q_HEADS * SWA_BLOCK, LANES)
    poolw = _block_diag(pool_w).astype(jnp.bfloat16)
    pools = pool_scale.reshape(depth, 1, D_C).astype(jnp.float32)
    rw, rb = _router_weights(router_g_w, router_g_b, router_e_w, router_e_b)
    g1 = norm1_g.reshape(depth, 1, D_MODEL)
    g2 = norm2_g.reshape(depth, 1, D_MODEL)

    x2 = x.reshape(T, D_MODEL)
    qp, kv = _norm_proj(0, x2, g1, w_in_k)
    for l in range(depth):
        xmid, xs_local, rinfo, cnt = _mixer(
            l, x2, qp, kv, (nab, sinkcol, poolw, pools, w_out_k, g2, rw, rb), (swb, tri, utri),
            batch=batch, seq_len=seq_len)
        tile_tables, chunk_src, chunk_tab = _dispatch_tables(cnt, n_tiles)
        ys = _experts(l, tile_tables, chunk_src, xs_local,
                      expert_w_gate, expert_w_up, expert_w_down)
        if l + 1 < depth:
            x2, qp, kv = _combine_proj(l + 1, chunk_tab, xmid, rinfo, g1, w_in_k, ys)
        else:
            x2 = _combine_final(chunk_tab, xmid, rinfo, final_g.reshape(1, D_MODEL), ys)
    return x2.reshape(batch, seq_len, D_MODEL)
```

```python
import functools

import jax
import jax.numpy as jnp
import numpy as np
from jax import lax
from jax.experimental import pallas as pl
from jax.experimental.pallas import tpu as pltpu

D_MODEL = 1024
GRID_W = 64
HEAD_DIM = 64
NA_HEADS = 4
NA_ROWS = 8
NA_COLS = 16
SWA_Q_HEADS = 8
SWA_KV_HEADS = 2
SWA_REP = SWA_Q_HEADS // SWA_KV_HEADS
SWA_WINDOW = 128
SWA_BLOCK = 128
POOL_WINDOWS = (2, 4, 8, 16)
POOL_GROUP_DIM = 64
D_A = NA_HEADS * HEAD_DIM
D_B = SWA_Q_HEADS * HEAD_DIM
D_BKV = SWA_KV_HEADS * HEAD_DIM
D_C = len(POOL_WINDOWS) * POOL_GROUP_DIM
D_MIX = D_A + D_B + D_C
D_QP = D_A + D_B
D_KV = 2 * D_A + 2 * D_BKV + D_C
D_IN = D_QP + D_KV
N_GROUPS = 4
EXPERTS_PER_GROUP = 8
N_EXPERTS = N_GROUPS * EXPERTS_PER_GROUP
D_EXPERT = 256
RMS_EPS = 1e-6
NEG = -1e30
LOG2E = 1.4426950408889634
QK_SCALE = HEAD_DIM ** -0.5 * LOG2E

LANES = 128
SUBLANES = 8
ROW_CHUNKS = D_MODEL // LANES

TOK_TILE = 512
EXP_SUB = 256
EXP_SUBS = 4
EXP_TILE = EXP_SUB * EXP_SUBS
CHUNK = 16
LOCAL_CHUNKS = (2 * TOK_TILE + N_EXPERTS * (CHUNK - 1)) // CHUNK + 2
LOCAL_ROWS = LOCAL_CHUNKS * CHUNK
SUB_CHUNKS = EXP_SUB // CHUNK
TILE_CHUNKS = EXP_TILE // CHUNK
GATHER_AHEAD = 2
GATHER_SLOTS = GATHER_AHEAD + 1
ROUTE_LANE0 = 8
D_SLOT = D_MODEL + LANES
GATE1_LANES = (0, 1, 2)
GATE2_LANES = (3, 4, 5)
FIRST_EXPERT_LANE = 6
NA_ROWS_PER_STEP = 8
SWA_BLOCKS_PER_STEP = 4
HALO = 8
VMEM_LIMIT = 56 * 1024 * 1024


def _rmsnorm_f32(x, g):
    return x * lax.rsqrt(jnp.mean(x * x, axis=-1, keepdims=True) + RMS_EPS) * g


def _norm_proj_kernel(x_ref, g_ref, w_ref, qp_ref, kv_ref):
    xn = _rmsnorm_f32(x_ref[...], g_ref[...]).astype(jnp.bfloat16)
    proj = jnp.dot(xn, w_ref[...], preferred_element_type=jnp.float32)
    qp_ref[...] = proj[:, :D_QP].astype(jnp.bfloat16)
    kv_ref[...] = proj[:, D_QP:].astype(jnp.bfloat16)


def _norm_proj(layer, x2, g, w):
    T = x2.shape[0]
    return pl.pallas_call(
        _norm_proj_kernel,
        grid=(T // TOK_TILE,),
        in_specs=[
            pl.BlockSpec((TOK_TILE, D_MODEL), lambda i: (i, 0)),
            pl.BlockSpec((None, 1, D_MODEL), lambda i: (layer, 0, 0)),
            pl.BlockSpec((None, D_MODEL, D_IN), lambda i: (layer, 0, 0)),
        ],
        out_specs=[
            pl.BlockSpec((TOK_TILE, D_QP), lambda i: (i, 0)),
            pl.BlockSpec((TOK_TILE, D_KV), lambda i: (i, 0)),
        ],
        out_shape=[
            jax.ShapeDtypeStruct((T, D_QP), jnp.bfloat16),
            jax.ShapeDtypeStruct((T, D_KV), jnp.bfloat16),
        ],
        compiler_params=pltpu.CompilerParams(
            dimension_semantics=("arbitrary",), vmem_limit_bytes=VMEM_LIMIT),
        name="norm_proj",
    )(x2, g, w)


KW_AK, KW_AV, KW_BK, KW_BV = 0, D_A, 2 * D_A, 2 * D_A + D_BKV
KW_COLS = 2 * D_A + 2 * D_BKV
KV_CU = KW_COLS


def _mixer_kernel(x_ref, qp_ref, kvp_ref, kvc_ref, kvn_ref, nab_ref, swb_ref, sink_ref,
                  poolw_ref, pools_ref, wout_ref, g2_ref, rw_ref, rb_ref, tri_ref, utri_ref,
                  xmid_ref, xs_ref, rinfo_ref, cnt_ref,
                  kwin, uwin, mix, xn_scr, logit_scr, *, seq_len):
    t = pl.program_id(0)
    nblk = seq_len // TOK_TILE
    i = jnp.minimum(t, pl.num_programs(0) - 2) % nblk
    rows_per_tile = TOK_TILE // GRID_W
    grid_rows = seq_len // GRID_W

    @pl.when(t == 0)
    def _():
        xn_scr[...] = jnp.zeros_like(xn_scr)
        logit_scr[...] = jnp.zeros_like(logit_scr)

    kwin[0:TOK_TILE, :] = kvp_ref[:, 0:KW_COLS]
    kwin[TOK_TILE:2 * TOK_TILE, :] = kvc_ref[:, 0:KW_COLS]
    kwin[2 * TOK_TILE:3 * TOK_TILE, :] = kvn_ref[:, 0:KW_COLS]

    lane_a = lax.broadcasted_iota(jnp.int32, (GRID_W, D_A), 1) // HEAD_DIM

    def na_row(rr):
        r = i * rows_per_tile + rr
        rs = jnp.clip(r - NA_ROWS // 2, 0, grid_rows - NA_ROWS)
        variant = r - rs
        start = pl.multiple_of((rs - i * rows_per_tile + rows_per_tile) * GRID_W, GRID_W)
        q0 = pl.multiple_of(rr * GRID_W, GRID_W)
        q = (qp_ref[pl.ds(q0, GRID_W), 0:D_A].astype(jnp.float32) * QK_SCALE).astype(jnp.bfloat16)
        zero = jnp.zeros_like(q)
        qs = jnp.concatenate([jnp.where(lane_a == h, q, zero) for h in range(NA_HEADS)], axis=0)
        kw = kwin[pl.ds(start, NA_ROWS * GRID_W), KW_AK:KW_AK + D_A]
        vw = kwin[pl.ds(start, NA_ROWS * GRID_W), KW_AV:KW_AV + D_A]
        s = lax.dot_general(qs, kw, (((1,), (1,)), ((), ())), preferred_element_type=jnp.float32)
        s = s + nab_ref[variant]
        m = jnp.max(s, axis=-1, keepdims=True)
        p = jnp.exp2(s - m)
        l = jnp.sum(p, axis=-1, keepdims=True)
        pv = jnp.dot(p.astype(jnp.bfloat16), vw, preferred_element_type=jnp.float32)
        pv = pv * (1.0 / l)
        o = jnp.zeros((GRID_W, D_A), jnp.float32)
        for h in range(NA_HEADS):
            o = o + jnp.where(lane_a == h, pv[h * GRID_W:(h + 1) * GRID_W, :], 0.0)
        mix[pl.ds(q0, GRID_W), 0:D_A] = o.astype(jnp.bfloat16)

    def na_step(it, c):
        for k in range(NA_ROWS_PER_STEP):
            na_row(it * NA_ROWS_PER_STEP + k)
        return c

    lax.fori_loop(0, rows_per_tile // NA_ROWS_PER_STEP, na_step, 0)

    lane_b = lax.broadcasted_iota(jnp.int32, (SWA_BLOCK, LANES), 1) // HEAD_DIM
    blocks_per_tile = TOK_TILE // SWA_BLOCK
    nblocks = seq_len // SWA_BLOCK

    ones_v = jnp.ones((3 * SWA_BLOCK, LANES), jnp.bfloat16)

    def swa_step(sb, c):
        n = i * blocks_per_tile + sb
        variant = jnp.where(n == 0, 0, jnp.where(n == nblocks - 1, 2, 1))
        q0 = pl.multiple_of(sb * SWA_BLOCK, SWA_BLOCK)
        k0 = pl.multiple_of(TOK_TILE - SWA_BLOCK + sb * SWA_BLOCK, SWA_BLOCK)
        kw = kwin[pl.ds(k0, 3 * SWA_BLOCK), KW_BK:KW_BK + D_BKV]
        vaug = jnp.concatenate([kwin[pl.ds(k0, 3 * SWA_BLOCK), KW_BV:KW_BV + D_BKV], ones_v], axis=1)
        outs = []
        for g in range(SWA_KV_HEADS):
            pieces = []
            for t in range(SWA_REP):
                qt = qp_ref[pl.ds(q0, SWA_BLOCK), D_A + t * LANES:D_A + (t + 1) * LANES]
                qt = (qt.astype(jnp.float32) * QK_SCALE).astype(jnp.bfloat16)
                pieces.append(jnp.where(lane_b == g, qt, jnp.zeros_like(qt)))
            qs = jnp.concatenate(pieces, axis=0)
            r0 = g * SWA_REP * SWA_BLOCK
            s = lax.dot_general(qs, kw, (((1,), (1,)), ((), ())), preferred_element_type=jnp.float32)
            s = s + swb_ref[variant, r0:r0 + SWA_REP * SWA_BLOCK, :]
            sink = sink_ref[r0:r0 + SWA_REP * SWA_BLOCK, :]
            m = jnp.broadcast_to(jnp.max(s, axis=-1, keepdims=True), sink.shape)
            m = jnp.maximum(m, sink)
            p = jnp.exp2(s - jnp.concatenate([m, m, m], axis=1)).astype(jnp.bfloat16)
            pv = jnp.dot(p, vaug, preferred_element_type=jnp.float32)
            l = pv[:, LANES:2 * LANES] + jnp.exp2(sink - m)
            outs.append(pv[:, 0:LANES] * (1.0 / l))
        for t in range(SWA_REP):
            o0 = outs[0][t * SWA_BLOCK:(t + 1) * SWA_BLOCK, :]
            o1 = outs[1][t * SWA_BLOCK:(t + 1) * SWA_BLOCK, :]
            ot = jnp.where(lane_b == 0, o0, o1)
            mix[pl.ds(q0, SWA_BLOCK), D_A + t * LANES:D_A + (t + 1) * LANES] = ot.astype(jnp.bfloat16)
        return c

    def swa_pair(it, c):
        for k in range(SWA_BLOCKS_PER_STEP):
            swa_step(it * SWA_BLOCKS_PER_STEP + k, c)
        return c

    lax.fori_loop(0, blocks_per_tile // SWA_BLOCKS_PER_STEP, swa_pair, 0)

    u = kvc_ref[:, KV_CU:KV_CU + D_C].astype(jnp.float32)
    prev_ok = (i > 0).astype(jnp.float32)
    next_ok = (i < nblk - 1).astype(jnp.float32)
    uwin[0:HALO, :] = kvp_ref[TOK_TILE - HALO:TOK_TILE, KV_CU:KV_CU + D_C].astype(jnp.float32) * prev_ok
    uwin[HALO:HALO + TOK_TILE, :] = u
    uwin[HALO + TOK_TILE:2 * HALO + TOK_TILE, :] = kvn_ref[0:HALO, KV_CU:KV_CU + D_C].astype(jnp.float32) * next_ok
    n_ext = TOK_TILE + 2 * HALO
    a2 = uwin[0:n_ext - 1, :] + uwin[1:n_ext, :]
    a4 = a2[0:n_ext - 3, :] + a2[2:n_ext - 1, :]
    a8 = a4[0:n_ext - 7, :] + a4[4:n_ext - 3, :]
    a16 = a8[0:n_ext - 15, :] + a8[8:n_ext - 7, :]
    w2 = a2[7:7 + TOK_TILE, :]
    w4 = a4[6:6 + TOK_TILE, :]
    w8 = a8[4:4 + TOK_TILE, :]
    w16 = a16[0:TOK_TILE, :]
    lane_c = lax.broadcasted_iota(jnp.int32, (TOK_TILE, D_C), 1) // POOL_GROUP_DIM
    pooled = jnp.where(lane_c == 0, w2, jnp.where(lane_c == 1, w4, jnp.where(lane_c == 2, w8, w16)))
    half = jnp.where(lane_c == 0, 1, jnp.where(lane_c == 1, 2, jnp.where(lane_c == 2, 4, 8)))
    pos = i * TOK_TILE + lax.broadcasted_iota(jnp.int32, (TOK_TILE, D_C), 0)
    cnt = (jnp.minimum(pos + half, seq_len) - jnp.maximum(pos - half, 0)).astype(jnp.float32)
    d = (pooled / cnt - u).astype(jnp.bfloat16)
    oc = jnp.dot(d, poolw_ref[...], preferred_element_type=jnp.float32) * pools_ref[...]
    mix[:, D_A + D_B:D_MIX] = oc.astype(jnp.bfloat16)

    xn = xn_scr[...]
    logits = logit_scr[...]

    xm = x_ref[...] + jnp.dot(mix[...], wout_ref[...], preferred_element_type=jnp.float32)
    xmid_ref[...] = xm
    xn_new = _rmsnorm_f32(xm, g2_ref[...]).astype(jnp.bfloat16)
    xn_scr[...] = xn_new
    logit_scr[...] = jnp.dot(xn_new, rw_ref[...], preferred_element_type=jnp.float32) + rb_ref[...]

    lane = lax.broadcasted_iota(jnp.int32, (TOK_TILE, LANES), 1).astype(jnp.float32)
    is_g = lane < N_GROUPS
    gl = jnp.where(is_g, logits, NEG)
    gmax = jnp.max(gl, axis=-1, keepdims=True)
    gtop = jnp.min(jnp.where(is_g & (gl == gmax), lane, float(LANES)), axis=-1, keepdims=True)
    gprob = 1.0 / jnp.sum(jnp.exp(gl - gmax), axis=-1, keepdims=True)
    e_lo = ROUTE_LANE0 + gtop * EXPERTS_PER_GROUP
    in_grp = (lane >= e_lo) & (lane < e_lo + EXPERTS_PER_GROUP)
    el = jnp.where(in_grp, logits, NEG)
    m1 = jnp.max(el, axis=-1, keepdims=True)
    i1 = jnp.min(jnp.where(in_grp & (el == m1), lane, float(LANES)), axis=-1, keepdims=True)
    el2 = jnp.where(lane == i1, NEG, el)
    m2 = jnp.max(el2, axis=-1, keepdims=True)
    i2 = jnp.min(jnp.where(in_grp & (lane != i1) & (el2 == m2), lane, float(LANES)), axis=-1, keepdims=True)
    r21 = jnp.exp(m2 - m1)
    gate1 = gprob / (1.0 + r21)
    gate2 = gprob * r21 / (1.0 + r21)

    oh1 = lane == i1
    oh2 = lane == i2
    oh = jnp.where(oh1 | oh2, 1.0, 0.0)
    earlier = jnp.dot(tri_ref[...], oh.astype(jnp.bfloat16), preferred_element_type=jnp.float32)
    n_e = jnp.sum(oh, axis=0, keepdims=True)
    chunks_e = jnp.floor((n_e + (CHUNK - 1)) * (1.0 / CHUNK))
    seg0 = jnp.dot(jnp.broadcast_to(chunks_e, (SUBLANES, LANES)).astype(jnp.bfloat16), utri_ref[...],
                   preferred_element_type=jnp.float32)[0:1, :] * CHUNK
    base = earlier + seg0
    lp1 = jnp.sum(jnp.where(oh1, base, 0.0), axis=-1, keepdims=True)
    lp2 = jnp.sum(jnp.where(oh2, base, 0.0), axis=-1, keepdims=True)
    info = jnp.where(lane == 0, lp1, jnp.where(lane == 1, lp2, 0.0))
    rinfo_ref[...] = info
    cnt_ref[...] = jnp.broadcast_to(n_e, cnt_ref.shape)

    def pieces(g):
        hi = g.astype(jnp.bfloat16).astype(jnp.float32)
        mid = (g - hi).astype(jnp.bfloat16).astype(jnp.float32)
        return hi, mid, g - hi - mid

    aux = jnp.zeros((TOK_TILE, LANES), jnp.float32)
    for k, piece in enumerate(pieces(gate1) + pieces(gate2) + (i1,)):
        aux = jnp.where(lane == k, piece, aux)

    info_t = info.T
    prow = lax.broadcasted_iota(jnp.int32, (LOCAL_ROWS, TOK_TILE), 0).astype(jnp.float32)
    sel = jnp.where((prow == info_t[0:1, :]) | (prow == info_t[1:2, :]), 1.0, 0.0).astype(jnp.bfloat16)
    moved = jnp.dot(sel, jnp.concatenate([xn, aux.astype(jnp.bfloat16)], axis=1),
                    preferred_element_type=jnp.float32)
    xs_ref[...] = moved.astype(jnp.bfloat16)


def _mixer(layer, x2, qp, kv, layer_params, shared_tables, *, seq_len):
    nab, sinkcol, poolw, pools, wout, g2, rw, rb = layer_params
    swb, tri, utri = shared_tables
    T = x2.shape[0]
    nblk = seq_len // TOK_TILE
    n_tok_tiles = T // TOK_TILE

    def mixed(t):
        return jnp.minimum(t, n_tok_tiles - 1)

    def cur(t):
        return (mixed(t), 0)

    def prev(t):
        return (jnp.maximum(mixed(t) - 1, (mixed(t) // nblk) * nblk), 0)

    def nxt(t):
        return (jnp.minimum(mixed(t) + 1, (mixed(t) // nblk) * nblk + nblk - 1), 0)

    def dispatched(t):
        return (jnp.maximum(t - 1, 0), 0)

    def resident(a):
        zeros = (0,) * a.ndim
        return pl.BlockSpec(a.shape, lambda t: zeros, pipeline_mode=pl.Buffered(1))

    def resident_layer(a):
        index = (layer,) + (0,) * (a.ndim - 1)
        return pl.BlockSpec((None,) + a.shape[1:], lambda t: index, pipeline_mode=pl.Buffered(1))

    return pl.pallas_call(
        functools.partial(_mixer_kernel, seq_len=seq_len),
        grid=(n_tok_tiles + 1,),
        in_specs=[
            pl.BlockSpec((TOK_TILE, D_MODEL), cur),
            pl.BlockSpec((TOK_TILE, D_QP), cur),
            pl.BlockSpec((TOK_TILE, D_KV), prev),
            pl.BlockSpec((TOK_TILE, D_KV), cur),
            pl.BlockSpec((TOK_TILE, D_KV), nxt),
            resident_layer(nab), resident(swb), resident_layer(sinkcol), resident_layer(poolw),
            resident_layer(pools), resident_layer(wout), resident_layer(g2), resident_layer(rw),
            resident_layer(rb), resident(tri), resident(utri),
        ],
        out_specs=[
            pl.BlockSpec((TOK_TILE, D_MODEL), cur),
            pl.BlockSpec((LOCAL_ROWS, D_SLOT), dispatched),
            pl.BlockSpec((TOK_TILE, LANES), dispatched),
            pl.BlockSpec((SUBLANES, LANES), dispatched),
        ],
        out_shape=[
            jax.ShapeDtypeStruct((T, D_MODEL), jnp.float32),
            jax.ShapeDtypeStruct((n_tok_tiles * LOCAL_ROWS, D_SLOT), jnp.bfloat16),
            jax.ShapeDtypeStruct((T, LANES), jnp.float32),
            jax.ShapeDtypeStruct((n_tok_tiles * SUBLANES, LANES), jnp.float32),
        ],
        scratch_shapes=[
            pltpu.VMEM((3 * TOK_TILE, KW_COLS), jnp.bfloat16),
            pltpu.VMEM((TOK_TILE + 2 * HALO, D_C), jnp.float32),
            pltpu.VMEM((TOK_TILE, D_MIX), jnp.bfloat16),
            pltpu.VMEM((TOK_TILE, D_MODEL), jnp.bfloat16),
            pltpu.VMEM((TOK_TILE, LANES), jnp.float32),
        ],
        compiler_params=pltpu.CompilerParams(
            dimension_semantics=("arbitrary",), vmem_limit_bytes=VMEM_LIMIT),
        name="mixer",
    )(x2, qp, kv, kv, kv, nab, swb, sinkcol, poolw, pools, wout, g2, rw, rb, tri, utri)


def _chunk_copy(src_hbm, src_chunk, dst, dst_chunk, sem):
    return pltpu.make_async_copy(
        src_hbm.at[pl.ds(pl.multiple_of(src_chunk * CHUNK, CHUNK), CHUNK)],
        dst.at[pl.ds(dst_chunk * CHUNK, CHUNK)],
        sem)


def _expert_kernel(te_ref, nsub_ref, first_ref, wslot_ref, nexte_ref, csrc_ref,
                   xs_hbm, wg_hbm, wu_hbm, wd_hbm, y_ref,
                   xbuf, wgf, wuf, wdf, wgb, wub, wdb, sem, wsem, *, layer):
    j = pl.program_id(0)
    nt = pl.num_programs(0)
    slot = j % GATHER_SLOTS

    def start_gather(tile, s, h):
        for c in range(h * SUB_CHUNKS, (h + 1) * SUB_CHUNKS):
            _chunk_copy(xs_hbm, csrc_ref[tile * TILE_CHUNKS + c], xbuf.at[s], c, sem.at[s, h]).start()

    def wait_gather(s, h):
        rows = pl.ds(h * EXP_SUB, EXP_SUB)
        pltpu.make_async_copy(xs_hbm.at[pl.ds(0, EXP_SUB)], xbuf.at[s, rows], sem.at[s, h]).wait()

    def weight_copies(expert, ws):
        return [pltpu.make_async_copy(w_hbm.at[layer, expert], wbuf.at[ws], wsem.at[k, ws])
                for k, (w_hbm, wbuf) in enumerate(((wg_hbm, wgf), (wu_hbm, wuf), (wd_hbm, wdf)))]

    for h in range(EXP_SUBS):
        for first_tile in range(GATHER_AHEAD):
            @pl.when((j == 0) & (h < nsub_ref[first_tile]))
            def _():
                start_gather(first_tile, first_tile, h)

        ahead = j + GATHER_AHEAD

        @pl.when(h < nsub_ref[jnp.minimum(ahead, nt - 1)] * (ahead < nt).astype(jnp.int32))
        def _():
            start_gather(ahead, ahead % GATHER_SLOTS, h)

    ws = wslot_ref[j]

    @pl.when(j == 0)
    def _():
        for cp in weight_copies(te_ref[0], 0):
            cp.start()

    @pl.when(first_ref[j] > 0)
    def _():
        for cp in weight_copies(te_ref[j], ws):
            cp.wait()

        @pl.when(nexte_ref[j] >= 0)
        def _():
            for cp in weight_copies(nexte_ref[j], 1 - ws):
                cp.start()

        wgb[...] = wgf[ws].astype(jnp.bfloat16)
        wub[...] = wuf[ws].astype(jnp.bfloat16)
        wdb[...] = wdf[ws].astype(jnp.bfloat16)

    def gated_mlp(n_rows):
        xs = xbuf[slot, 0:n_rows, 0:D_MODEL]
        gp = xbuf[slot, 0:n_rows, D_MODEL:D_SLOT].astype(jnp.float32)
        first = gp[:, FIRST_EXPERT_LANE:FIRST_EXPERT_LANE + 1] == (te_ref[j] + ROUTE_LANE0).astype(jnp.float32)

        def gate_of(lanes):
            return sum(gp[:, k:k + 1] for k in lanes)

        gate_w = jnp.where(first, gate_of(GATE1_LANES), gate_of(GATE2_LANES))
        gate = jnp.dot(xs, wgb[...], preferred_element_type=jnp.float32)
        up = jnp.dot(xs, wub[...], preferred_element_type=jnp.float32)
        act = (gate * (1.0 / (1.0 + jnp.exp(-gate))) * up * gate_w).astype(jnp.bfloat16)
        return jnp.dot(act, wdb[...], preferred_element_type=jnp.float32).astype(jnp.bfloat16)

    for k in range(1, EXP_SUBS + 1):
        @pl.when(nsub_ref[j] == k)
        def _():
            for h in range(k):
                wait_gather(slot, h)
            y_ref[0:k * EXP_SUB, :] = gated_mlp(k * EXP_SUB)
            if k < EXP_SUBS:
                y_ref[k * EXP_SUB:EXP_TILE, :] = jnp.zeros((EXP_TILE - k * EXP_SUB, D_MODEL), jnp.bfloat16)

    @pl.when(nsub_ref[j] == 0)
    def _():
        y_ref[...] = jnp.zeros_like(y_ref)


def _experts(layer, tile_tables, chunk_src, xs_local, wg, wu, wd):
    tile_expert, n_sub, first, wslot, next_expert = tile_tables
    n_tiles = tile_expert.shape[0]
    any_space = pl.BlockSpec(memory_space=pl.ANY)

    return pl.pallas_call(
        functools.partial(_expert_kernel, layer=layer),
        grid_spec=pltpu.PrefetchScalarGridSpec(
            num_scalar_prefetch=6,
            grid=(n_tiles,),
            in_specs=[any_space] * 4,
            out_specs=pl.BlockSpec((EXP_TILE, D_MODEL), lambda j, *tables: (j, 0)),
            scratch_shapes=[
                pltpu.VMEM((GATHER_SLOTS, EXP_TILE, D_SLOT), jnp.bfloat16),
                pltpu.VMEM((2, D_MODEL, D_EXPERT), jnp.float32),
                pltpu.VMEM((2, D_MODEL, D_EXPERT), jnp.float32),
                pltpu.VMEM((2, D_EXPERT, D_MODEL), jnp.float32),
                pltpu.VMEM((D_MODEL, D_EXPERT), jnp.bfloat16),
                pltpu.VMEM((D_MODEL, D_EXPERT), jnp.bfloat16),
                pltpu.VMEM((D_EXPERT, D_MODEL), jnp.bfloat16),
                pltpu.SemaphoreType.DMA((GATHER_SLOTS, EXP_SUBS)),
                pltpu.SemaphoreType.DMA((3, 2)),
            ],
        ),
        out_shape=jax.ShapeDtypeStruct((n_tiles * EXP_TILE, D_MODEL), jnp.bfloat16),
        compiler_params=pltpu.CompilerParams(
            dimension_semantics=("arbitrary",), vmem_limit_bytes=VMEM_LIMIT),
        name="experts",
    )(tile_expert, n_sub, first, wslot, next_expert, chunk_src, xs_local, wg, wu, wd)


def _combine_tile(ctab_ref, xmid_ref, rinfo_ref, ys_hbm, ybuf, sem):
    i = pl.program_id(0)
    nt = pl.num_programs(0)
    slot = i % 2

    def start_gather(tile, s):
        for c in range(LOCAL_CHUNKS):
            _chunk_copy(ys_hbm, ctab_ref[tile * LOCAL_CHUNKS + c], ybuf.at[s], c, sem.at[s]).start()

    def wait_gather(s):
        pltpu.make_async_copy(ys_hbm.at[pl.ds(0, LOCAL_ROWS)], ybuf.at[s], sem.at[s]).wait()

    @pl.when(i == 0)
    def _():
        start_gather(0, 0)

    @pl.when(i + 1 < nt)
    def _():
        start_gather(i + 1, 1 - slot)

    wait_gather(slot)
    info = rinfo_ref[...]
    pcol = lax.broadcasted_iota(jnp.int32, (TOK_TILE, LOCAL_ROWS), 1).astype(jnp.float32)
    pick = jnp.where((pcol == info[:, 0:1]) | (pcol == info[:, 1:2]), 1.0, 0.0).astype(jnp.bfloat16)
    return xmid_ref[...] + jnp.dot(pick, ybuf[slot], preferred_element_type=jnp.float32)


def _combine_final_kernel(ctab_ref, xmid_ref, rinfo_ref, g_ref, ys_hbm, out_ref, ybuf, sem):
    x = _combine_tile(ctab_ref, xmid_ref, rinfo_ref, ys_hbm, ybuf, sem)
    out_ref[...] = _rmsnorm_f32(x, g_ref[...])


def _combine_proj_kernel(ctab_ref, xmid_ref, rinfo_ref, g_ref, w_ref, ys_hbm, x_ref, qp_ref, kv_ref, ybuf, sem):
    x = _combine_tile(ctab_ref, xmid_ref, rinfo_ref, ys_hbm, ybuf, sem)
    x_ref[...] = x
    xn = _rmsnorm_f32(x, g_ref[...]).astype(jnp.bfloat16)
    proj = jnp.dot(xn, w_ref[...], preferred_element_type=jnp.float32)
    qp_ref[...] = proj[:, :D_QP].astype(jnp.bfloat16)
    kv_ref[...] = proj[:, D_QP:].astype(jnp.bfloat16)


def _combine_proj(next_layer, chunk_tab, xmid, rinfo, g1, w_in_k, ys):
    T = xmid.shape[0]

    def tile(i, ct):
        return (i, 0)

    return pl.pallas_call(
        _combine_proj_kernel,
        grid_spec=pltpu.PrefetchScalarGridSpec(
            num_scalar_prefetch=1,
            grid=(T // TOK_TILE,),
            in_specs=[
                pl.BlockSpec((TOK_TILE, D_MODEL), tile),
                pl.BlockSpec((TOK_TILE, LANES), tile),
                pl.BlockSpec((None, 1, D_MODEL), lambda i, ct: (next_layer, 0, 0)),
                pl.BlockSpec((None, D_MODEL, D_IN), lambda i, ct: (next_layer, 0, 0),
                             pipeline_mode=pl.Buffered(1)),
                pl.BlockSpec(memory_space=pl.ANY),
            ],
            out_specs=[
                pl.BlockSpec((TOK_TILE, D_MODEL), tile),
                pl.BlockSpec((TOK_TILE, D_QP), tile),
                pl.BlockSpec((TOK_TILE, D_KV), tile),
            ],
            scratch_shapes=[
                pltpu.VMEM((2, LOCAL_ROWS, D_MODEL), jnp.bfloat16),
                pltpu.SemaphoreType.DMA((2,)),
            ],
        ),
        out_shape=[
            jax.ShapeDtypeStruct((T, D_MODEL), jnp.float32),
            jax.ShapeDtypeStruct((T, D_QP), jnp.bfloat16),
            jax.ShapeDtypeStruct((T, D_KV), jnp.bfloat16),
        ],
        compiler_params=pltpu.CompilerParams(
            dimension_semantics=("arbitrary",), vmem_limit_bytes=VMEM_LIMIT),
        name="combine_proj",
    )(chunk_tab, xmid, rinfo, g1, w_in_k, ys)


def _combine_final(chunk_tab, xmid, rinfo, g, ys):
    T = xmid.shape[0]
    return pl.pallas_call(
        _combine_final_kernel,
        grid_spec=pltpu.PrefetchScalarGridSpec(
            num_scalar_prefetch=1,
            grid=(T // TOK_TILE,),
            in_specs=[
                pl.BlockSpec((TOK_TILE, D_MODEL), lambda i, ct: (i, 0)),
                pl.BlockSpec((TOK_TILE, LANES), lambda i, ct: (i, 0)),
                pl.BlockSpec((1, D_MODEL), lambda i, ct: (0, 0)),
                pl.BlockSpec(memory_space=pl.ANY),
            ],
            out_specs=pl.BlockSpec((TOK_TILE, D_MODEL), lambda i, ct: (i, 0)),
            scratch_shapes=[
                pltpu.VMEM((2, LOCAL_ROWS, D_MODEL), jnp.bfloat16),
                pltpu.SemaphoreType.DMA((2,)),
            ],
        ),
        out_shape=jax.ShapeDtypeStruct((T, D_MODEL), jnp.float32),
        compiler_params=pltpu.CompilerParams(
            dimension_semantics=("arbitrary",), vmem_limit_bytes=VMEM_LIMIT),
        name="combine",
    )(chunk_tab, xmid, rinfo, g, ys)


def _pair_heads(a, axis):
    shape = a.shape
    split = shape[:axis] + (SWA_KV_HEADS, SWA_REP, HEAD_DIM) + shape[axis + 1:]
    return jnp.swapaxes(a.reshape(split), axis, axis + 1).reshape(shape)


def _in_proj_weight(w):
    off_bq = 3 * D_A
    return jnp.concatenate(
        [w[..., 0:D_A], _pair_heads(w[..., off_bq:off_bq + D_B], w.ndim - 1), w[..., D_A:off_bq],
         w[..., off_bq + D_B:]], axis=-1).astype(jnp.bfloat16)


def _out_proj_weight(w):
    return jnp.concatenate(
        [w[:, 0:D_A], _pair_heads(w[:, D_A:D_A + D_B], 1), w[:, D_A + D_B:]], axis=1).astype(jnp.bfloat16)


def _na_bias_table(rel_bias):
    c = np.arange(GRID_W)[:, None]
    cp = np.arange(GRID_W)[None, :]
    cs = np.clip(c - NA_COLS // 2, 0, GRID_W - NA_COLS)
    valid = (cp >= cs) & (cp < cs + NA_COLS)
    d = np.arange(2 * NA_COLS - 1)[:, None, None]
    col_sel = ((cp - c + (NA_COLS - 1))[None] == d) & valid[None]
    k = np.arange(NA_ROWS)[:, None, None]
    j = np.arange(NA_ROWS)[None, :, None]
    r = np.arange(2 * NA_ROWS - 1)[None, None, :]
    row_sel = r == j - k + (NA_ROWS - 1)
    tab = jnp.einsum("kjr,lhrd,dcm->lkhcjm", jnp.asarray(row_sel, jnp.float32), rel_bias.astype(jnp.float32),
                     jnp.asarray(col_sel, jnp.float32), precision=lax.Precision.HIGHEST)
    tab = jnp.where(jnp.asarray(valid)[None, None, None, :, None, :], tab * LOG2E, NEG)
    return tab.reshape(rel_bias.shape[0], NA_ROWS, NA_HEADS * GRID_W, NA_ROWS * GRID_W)


def _swa_bias_table():
    slopes = (2.0 ** (-8.0 * np.arange(1, SWA_Q_HEADS + 1) / SWA_Q_HEADS)).astype(np.float32)
    qi = np.arange(SWA_BLOCK)[:, None]
    ki = np.arange(3 * SWA_BLOCK)[None, :]
    dist = np.abs(ki - qi - SWA_BLOCK).astype(np.float32)
    tab = np.where(dist <= SWA_WINDOW, -slopes[:, None, None] * dist[None] * LOG2E, np.float32(NEG))
    tab = tab.reshape(SWA_Q_HEADS * SWA_BLOCK, 3 * SWA_BLOCK).astype(np.float32)
    first, last = tab.copy(), tab.copy()
    first[:, :SWA_BLOCK] = NEG
    last[:, 2 * SWA_BLOCK:] = NEG
    return jnp.asarray(np.stack([first, tab, last]))


def _block_diag(pool_w):
    depth, n = pool_w.shape[0:2]
    eye = jnp.asarray(np.eye(n, dtype=np.float32))
    out = pool_w[:, :, :, None, :] * eye[None, :, None, :, None]
    return out.reshape(depth, n * POOL_GROUP_DIM, n * POOL_GROUP_DIM)


def _router_weights(rg_w, rg_b, re_w, re_b):
    def lanes(g, e):
        gap = jnp.zeros(g.shape[:-1] + (ROUTE_LANE0 - N_GROUPS,), jnp.float32)
        tail = jnp.zeros(g.shape[:-1] + (LANES - ROUTE_LANE0 - N_EXPERTS,), jnp.float32)
        return jnp.concatenate([g.astype(jnp.float32), gap, e.astype(jnp.float32), tail], axis=-1)

    return lanes(rg_w, re_w).astype(jnp.bfloat16), lanes(rg_b, re_b)[:, None, :]


def _dispatch_tables(cnt, n_tiles):
    nb = cnt.shape[0] // SUBLANES
    n = cnt.reshape(nb, SUBLANES, LANES)[:, 0, ROUTE_LANE0:ROUTE_LANE0 + N_EXPERTS].astype(jnp.int32)
    g = (n + (CHUNK - 1)) // CHUNK
    l_end = jnp.cumsum(g, axis=1)
    l_off = l_end - g
    c_end = jnp.cumsum(g, axis=0)
    c_off = c_end - g
    tot = c_end[-1]
    tiles = (tot + (TILE_CHUNKS - 1)) // TILE_CHUNKS
    t_end = jnp.cumsum(tiles)
    t_off = t_end - tiles
    n_used = t_end[-1:]

    experts = jnp.arange(N_EXPERTS, dtype=jnp.int32)
    tile_ids = jnp.arange(n_tiles, dtype=jnp.int32)
    tile_expert = jnp.minimum(jnp.sum((t_end[None, :] <= tile_ids[:, None]).astype(jnp.int32), axis=1),
                              N_EXPERTS - 1)
    oh_te = (tile_expert[:, None] == experts[None, :]).astype(jnp.int32)
    left = jnp.sum(oh_te * (tot + t_off * TILE_CHUNKS)[None, :], axis=1) - tile_ids * TILE_CHUNKS
    n_sub = jnp.clip((left + (SUB_CHUNKS - 1)) // SUB_CHUNKS, 0, EXP_SUBS)
    has_rows = tiles > 0
    first = ((tile_ids == jnp.sum(oh_te * t_off[None, :], axis=1)) & (n_sub > 0)).astype(jnp.int32)
    wslot = jnp.sum(oh_te * ((jnp.cumsum(has_rows.astype(jnp.int32)) - 1) % 2)[None, :], axis=1)
    later = (experts[None, :] > experts[:, None]) & has_rows[None, :]
    nxt = jnp.min(jnp.where(later, experts[None, :], N_EXPERTS), axis=1)
    next_expert = jnp.sum(oh_te * jnp.where(nxt < N_EXPERTS, nxt, -1)[None, :], axis=1)
    tile_tables = (tile_expert, n_sub, first, wslot, next_expert)

    q = jnp.arange(n_tiles * TILE_CHUNKS, dtype=jnp.int32)
    tile_q = q // TILE_CHUNKS
    oh_e = (jnp.repeat(tile_expert, TILE_CHUNKS)[:, None] == experts[None, :]).astype(jnp.int32)
    ro = q - jnp.sum(oh_e * t_off[None, :], axis=1) * TILE_CHUNKS
    valid = (ro < jnp.sum(oh_e * tot[None, :], axis=1)) & (tile_q < n_used[0])
    cols = jnp.dot(jnp.concatenate([c_end, c_off, l_off], axis=0).astype(jnp.float32),
                   oh_e.T.astype(jnp.float32), precision=lax.Precision.HIGHEST).astype(jnp.int32)
    c_end_q, c_off_q, l_off_q = cols[0:nb], cols[nb:2 * nb], cols[2 * nb:3 * nb]
    b_q = jnp.minimum(jnp.sum((c_end_q <= ro[None, :]).astype(jnp.int32), axis=0), nb - 1)
    oh_b = (jnp.arange(nb, dtype=jnp.int32)[:, None] == b_q[None, :]).astype(jnp.int32)
    src = b_q * LOCAL_CHUNKS + jnp.sum(oh_b * (l_off_q + ro[None, :] - c_off_q), axis=0)
    chunk_src = jnp.where(valid, src, LOCAL_CHUNKS - 1)

    c = jnp.arange(LOCAL_CHUNKS, dtype=jnp.int32)
    e_c = jnp.minimum(jnp.sum((l_end[:, None, :] <= c[None, :, None]).astype(jnp.int32), axis=2),
                      N_EXPERTS - 1)
    oh_ec = (e_c[:, :, None] == experts[None, None, :]).astype(jnp.int32)
    pos = (jnp.sum(oh_ec * (t_off[None, None, :] * TILE_CHUNKS + c_off[:, None, :] - l_off[:, None, :]), axis=2)
           + c[None, :])
    chunk_tab = jnp.where(c[None, :] < l_end[:, -1:], pos, 0).reshape(-1)
    return tile_tables, chunk_src, chunk_tab


def kernel(x, norm1_g, w_in, nat_bias, swa_sink, pool_w, pool_scale, w_out, norm2_g, router_g_w,
           router_g_b, router_e_w, router_e_b, expert_w_gate, expert_w_up, expert_w_down, final_g):
    batch, seq_len, _ = x.shape
    depth = w_in.shape[0]
    T = batch * seq_len
    assert seq_len % TOK_TILE == 0 and TOK_TILE % SWA_BLOCK == 0 and TOK_TILE % GRID_W == 0
    max_chunks = (2 * T) // CHUNK + (T // TOK_TILE) * N_EXPERTS
    n_tiles = max_chunks // TILE_CHUNKS + N_EXPERTS

    swb = _swa_bias_table()
    tri = jnp.asarray(np.tril(np.ones((TOK_TILE, TOK_TILE), np.float32), -1)).astype(jnp.bfloat16)
    utri = jnp.asarray(np.triu(np.ones((LANES, LANES), np.float32), 1)).astype(jnp.bfloat16)

    w_in_k = _in_proj_weight(w_in)
    w_out_k = _out_proj_weight(w_out)
    nab = _na_bias_table(nat_bias)
    sinkcol = jnp.broadcast_to((swa_sink.astype(jnp.float32) * LOG2E)[:, :, None, None],
                               (depth, SWA_Q_HEADS, SWA_BLOCK, LANES)).reshape(depth, SWA_Q_HEADS * SWA_BLOCK, LANES)
    poolw = _block_diag(pool_w).astype(jnp.bfloat16)
    pools = pool_scale.reshape(depth, 1, D_C).astype(jnp.float32)
    rw, rb = _router_weights(router_g_w, router_g_b, router_e_w, router_e_b)
    g1 = norm1_g.reshape(depth, 1, D_MODEL)
    g2 = norm2_g.reshape(depth, 1, D_MODEL)

    x2 = x.reshape(T, D_MODEL)
    qp, kv = _norm_proj(0, x2, g1, w_in_k)
    for l in range(depth):
        xmid, xs_local, rinfo, cnt = _mixer(
            l, x2, qp, kv, (nab, sinkcol, poolw, pools, w_out_k, g2, rw, rb), (swb, tri, utri),
            seq_len=seq_len)
        tile_tables, chunk_src, chunk_tab = _dispatch_tables(cnt, n_tiles)
        ys = _experts(l, tile_tables, chunk_src, xs_local,
                      expert_w_gate, expert_w_up, expert_w_down)
        if l + 1 < depth:
            x2, qp, kv = _combine_proj(l + 1, chunk_tab, xmid, rinfo, g1, w_in_k, ys)
        else:
            x2 = _combine_final(chunk_tab, xmid, rinfo, final_g.reshape(1, D_MODEL), ys)
    return x2.reshape(batch, seq_len, D_MODEL)
```

```python
import functools

import jax
import jax.numpy as jnp
import numpy as np
from jax import lax
from jax.experimental import pallas as pl
from jax.experimental.pallas import tpu as pltpu

D_MODEL = 1024
GRID_W = 64
HEAD_DIM = 64
NA_HEADS = 4
NA_ROWS = 8
NA_COLS = 16
SWA_Q_HEADS = 8
SWA_KV_HEADS = 2
SWA_REP = SWA_Q_HEADS // SWA_KV_HEADS
SWA_WINDOW = 128
SWA_BLOCK = 128
POOL_WINDOWS = (2, 4, 8, 16)
POOL_GROUP_DIM = 64
D_A = NA_HEADS * HEAD_DIM
D_B = SWA_Q_HEADS * HEAD_DIM
D_BKV = SWA_KV_HEADS * HEAD_DIM
D_C = len(POOL_WINDOWS) * POOL_GROUP_DIM
D_MIX = D_A + D_B + D_C
D_QP = D_A + D_B
D_KV = 2 * D_A + 2 * D_BKV + D_C
D_IN = D_QP + D_KV
N_GROUPS = 4
EXPERTS_PER_GROUP = 8
N_EXPERTS = N_GROUPS * EXPERTS_PER_GROUP
D_EXPERT = 256
RMS_EPS = 1e-6
NEG = -1e30
LOG2E = 1.4426950408889634
QK_SCALE = HEAD_DIM ** -0.5 * LOG2E

LANES = 128
SUBLANES = 8
ROW_CHUNKS = D_MODEL // LANES

TOK_TILE = 512
EXP_SUB = 256
EXP_SUBS = 4
EXP_TILE = EXP_SUB * EXP_SUBS
CHUNK = 16
LOCAL_CHUNKS = (2 * TOK_TILE + N_EXPERTS * (CHUNK - 1)) // CHUNK + 2
LOCAL_ROWS = LOCAL_CHUNKS * CHUNK
SUB_CHUNKS = EXP_SUB // CHUNK
TILE_CHUNKS = EXP_TILE // CHUNK
GATHER_AHEAD = 2
GATHER_SLOTS = GATHER_AHEAD + 1
ROUTE_LANE0 = 8
D_SLOT = D_MODEL + LANES
GATE1_LANES = (0, 1, 2)
GATE2_LANES = (3, 4, 5)
FIRST_EXPERT_LANE = 6
NA_ROWS_PER_STEP = 8
SWA_BLOCKS_PER_STEP = 4
HALO = 8
VMEM_LIMIT = 56 * 1024 * 1024


def _rmsnorm_f32(x, g):
    return x * lax.rsqrt(jnp.mean(x * x, axis=-1, keepdims=True) + RMS_EPS) * g


def _norm_proj_kernel(x_ref, g_ref, w_ref, qp_ref, kv_ref):
    xn = _rmsnorm_f32(x_ref[...], g_ref[...]).astype(jnp.bfloat16)
    proj = jnp.dot(xn, w_ref[...], preferred_element_type=jnp.float32)
    qp_ref[...] = proj[:, :D_QP].astype(jnp.bfloat16)
    kv_ref[...] = proj[:, D_QP:].astype(jnp.bfloat16)


def _norm_proj(layer, x2, g, w):
    T = x2.shape[0]
    return pl.pallas_call(
        _norm_proj_kernel,
        grid=(T // TOK_TILE,),
        in_specs=[
            pl.BlockSpec((TOK_TILE, D_MODEL), lambda i: (i, 0)),
            pl.BlockSpec((None, 1, D_MODEL), lambda i: (layer, 0, 0)),
            pl.BlockSpec((None, D_MODEL, D_IN), lambda i: (layer, 0, 0)),
        ],
        out_specs=[
            pl.BlockSpec((TOK_TILE, D_QP), lambda i: (i, 0)),
            pl.BlockSpec((TOK_TILE, D_KV), lambda i: (i, 0)),
        ],
        out_shape=[
            jax.ShapeDtypeStruct((T, D_QP), jnp.bfloat16),
            jax.ShapeDtypeStruct((T, D_KV), jnp.bfloat16),
        ],
        compiler_params=pltpu.CompilerParams(
            dimension_semantics=("arbitrary",), vmem_limit_bytes=VMEM_LIMIT),
        name="norm_proj",
    )(x2, g, w)


KW_AK, KW_AV, KW_BK, KW_BV = 0, D_A, 2 * D_A, 2 * D_A + D_BKV
KW_COLS = 2 * D_A + 2 * D_BKV
KV_CU = KW_COLS


def _mixer_kernel(x_ref, qp_ref, kvp_ref, kvc_ref, kvn_ref, nab_ref, swb_ref, sink_ref,
                  poolw_ref, pools_ref, wout_ref, g2_ref, rw_ref, rb_ref, tri_ref, utri_ref,
                  xmid_ref, xs_ref, rinfo_ref, cnt_ref,
                  kwin, uwin, mix, xn_scr, logit_scr, *, seq_len):
    t = pl.program_id(0)
    nblk = seq_len // TOK_TILE
    i = jnp.minimum(t, pl.num_programs(0) - 2) % nblk
    rows_per_tile = TOK_TILE // GRID_W
    grid_rows = seq_len // GRID_W

    @pl.when(t == 0)
    def _():
        xn_scr[...] = jnp.zeros_like(xn_scr)
        logit_scr[...] = jnp.zeros_like(logit_scr)

    kwin[0:TOK_TILE, :] = kvp_ref[:, 0:KW_COLS]
    kwin[TOK_TILE:2 * TOK_TILE, :] = kvc_ref[:, 0:KW_COLS]
    kwin[2 * TOK_TILE:3 * TOK_TILE, :] = kvn_ref[:, 0:KW_COLS]

    lane_a = lax.broadcasted_iota(jnp.int32, (GRID_W, D_A), 1) // HEAD_DIM

    def na_row(rr):
        r = i * rows_per_tile + rr
        rs = jnp.clip(r - NA_ROWS // 2, 0, grid_rows - NA_ROWS)
        variant = r - rs
        start = pl.multiple_of((rs - i * rows_per_tile + rows_per_tile) * GRID_W, GRID_W)
        q0 = pl.multiple_of(rr * GRID_W, GRID_W)
        q = (qp_ref[pl.ds(q0, GRID_W), 0:D_A].astype(jnp.float32) * QK_SCALE).astype(jnp.bfloat16)
        zero = jnp.zeros_like(q)
        qs = jnp.concatenate([jnp.where(lane_a == h, q, zero) for h in range(NA_HEADS)], axis=0)
        kw = kwin[pl.ds(start, NA_ROWS * GRID_W), KW_AK:KW_AK + D_A]
        vw = kwin[pl.ds(start, NA_ROWS * GRID_W), KW_AV:KW_AV + D_A]
        s = lax.dot_general(qs, kw, (((1,), (1,)), ((), ())), preferred_element_type=jnp.float32)
        s = s + nab_ref[variant]
        m = jnp.max(s, axis=-1, keepdims=True)
        p = jnp.exp2(s - m)
        l = jnp.sum(p, axis=-1, keepdims=True)
        pv = jnp.dot(p.astype(jnp.bfloat16), vw, preferred_element_type=jnp.float32)
        pv = pv * (1.0 / l)
        o = jnp.zeros((GRID_W, D_A), jnp.float32)
        for h in range(NA_HEADS):
            o = o + jnp.where(lane_a == h, pv[h * GRID_W:(h + 1) * GRID_W, :], 0.0)
        mix[pl.ds(q0, GRID_W), 0:D_A] = o.astype(jnp.bfloat16)

    def na_step(it, c):
        for k in range(NA_ROWS_PER_STEP):
            na_row(it * NA_ROWS_PER_STEP + k)
        return c

    lax.fori_loop(0, rows_per_tile // NA_ROWS_PER_STEP, na_step, 0)

    lane_b = lax.broadcasted_iota(jnp.int32, (SWA_BLOCK, LANES), 1) // HEAD_DIM
    blocks_per_tile = TOK_TILE // SWA_BLOCK
    nblocks = seq_len // SWA_BLOCK

    ones_v = jnp.ones((3 * SWA_BLOCK, LANES), jnp.bfloat16)

    def swa_step(sb, c):
        n = i * blocks_per_tile + sb
        variant = jnp.where(n == 0, 0, jnp.where(n == nblocks - 1, 2, 1))
        q0 = pl.multiple_of(sb * SWA_BLOCK, SWA_BLOCK)
        k0 = pl.multiple_of(TOK_TILE - SWA_BLOCK + sb * SWA_BLOCK, SWA_BLOCK)
        kw = kwin[pl.ds(k0, 3 * SWA_BLOCK), KW_BK:KW_BK + D_BKV]
        vaug = jnp.concatenate([kwin[pl.ds(k0, 3 * SWA_BLOCK), KW_BV:KW_BV + D_BKV], ones_v], axis=1)
        outs = []
        for g in range(SWA_KV_HEADS):
            pieces = []
            for t in range(SWA_REP):
                qt = qp_ref[pl.ds(q0, SWA_BLOCK), D_A + t * LANES:D_A + (t + 1) * LANES]
                qt = (qt.astype(jnp.float32) * QK_SCALE).astype(jnp.bfloat16)
                pieces.append(jnp.where(lane_b == g, qt, jnp.zeros_like(qt)))
            qs = jnp.concatenate(pieces, axis=0)
            r0 = g * SWA_REP * SWA_BLOCK
            s = lax.dot_general(qs, kw, (((1,), (1,)), ((), ())), preferred_element_type=jnp.float32)
            s = s + swb_ref[variant, r0:r0 + SWA_REP * SWA_BLOCK, :]
            sink = sink_ref[r0:r0 + SWA_REP * SWA_BLOCK, :]
            m = jnp.broadcast_to(jnp.max(s, axis=-1, keepdims=True), sink.shape)
            m = jnp.maximum(m, sink)
            p = jnp.exp2(s - jnp.concatenate([m, m, m], axis=1)).astype(jnp.bfloat16)
            pv = jnp.dot(p, vaug, preferred_element_type=jnp.float32)
            l = pv[:, LANES:2 * LANES] + jnp.exp2(sink - m)
            outs.append(pv[:, 0:LANES] * (1.0 / l))
        for t in range(SWA_REP):
            o0 = outs[0][t * SWA_BLOCK:(t + 1) * SWA_BLOCK, :]
            o1 = outs[1][t * SWA_BLOCK:(t + 1) * SWA_BLOCK, :]
            ot = jnp.where(lane_b == 0, o0, o1)
            mix[pl.ds(q0, SWA_BLOCK), D_A + t * LANES:D_A + (t + 1) * LANES] = ot.astype(jnp.bfloat16)
        return c

    def swa_pair(it, c):
        for k in range(SWA_BLOCKS_PER_STEP):
            swa_step(it * SWA_BLOCKS_PER_STEP + k, c)
        return c

    lax.fori_loop(0, blocks_per_tile // SWA_BLOCKS_PER_STEP, swa_pair, 0)

    u = kvc_ref[:, KV_CU:KV_CU + D_C].astype(jnp.float32)
    prev_ok = (i > 0).astype(jnp.float32)
    next_ok = (i < nblk - 1).astype(jnp.float32)
    uwin[0:HALO, :] = kvp_ref[TOK_TILE - HALO:TOK_TILE, KV_CU:KV_CU + D_C].astype(jnp.float32) * prev_ok
    uwin[HALO:HALO + TOK_TILE, :] = u
    uwin[HALO + TOK_TILE:2 * HALO + TOK_TILE, :] = kvn_ref[0:HALO, KV_CU:KV_CU + D_C].astype(jnp.float32) * next_ok
    n_ext = TOK_TILE + 2 * HALO
    a2 = uwin[0:n_ext - 1, :] + uwin[1:n_ext, :]
    a4 = a2[0:n_ext - 3, :] + a2[2:n_ext - 1, :]
    a8 = a4[0:n_ext - 7, :] + a4[4:n_ext - 3, :]
    a16 = a8[0:n_ext - 15, :] + a8[8:n_ext - 7, :]
    w2 = a2[7:7 + TOK_TILE, :]
    w4 = a4[6:6 + TOK_TILE, :]
    w8 = a8[4:4 + TOK_TILE, :]
    w16 = a16[0:TOK_TILE, :]
    lane_c = lax.broadcasted_iota(jnp.int32, (TOK_TILE, D_C), 1) // POOL_GROUP_DIM
    pooled = jnp.where(lane_c == 0, w2, jnp.where(lane_c == 1, w4, jnp.where(lane_c == 2, w8, w16)))
    half = jnp.where(lane_c == 0, 1, jnp.where(lane_c == 1, 2, jnp.where(lane_c == 2, 4, 8)))
    pos = i * TOK_TILE + lax.broadcasted_iota(jnp.int32, (TOK_TILE, D_C), 0)
    cnt = (jnp.minimum(pos + half, seq_len) - jnp.maximum(pos - half, 0)).astype(jnp.float32)
    d = (pooled / cnt - u).astype(jnp.bfloat16)
    oc = jnp.dot(d, poolw_ref[...], preferred_element_type=jnp.float32) * pools_ref[...]
    mix[:, D_A + D_B:D_MIX] = oc.astype(jnp.bfloat16)

    xn = xn_scr[...]
    logits = logit_scr[...]

    xm = x_ref[...] + jnp.dot(mix[...], wout_ref[...], preferred_element_type=jnp.float32)
    xmid_ref[...] = xm
    xn_new = _rmsnorm_f32(xm, g2_ref[...]).astype(jnp.bfloat16)
    xn_scr[...] = xn_new
    logit_scr[...] = jnp.dot(xn_new, rw_ref[...], preferred_element_type=jnp.float32) + rb_ref[...]

    lane = lax.broadcasted_iota(jnp.int32, (TOK_TILE, LANES), 1).astype(jnp.float32)
    is_g = lane < N_GROUPS
    gl = jnp.where(is_g, logits, NEG)
    gmax = jnp.max(gl, axis=-1, keepdims=True)
    gtop = jnp.min(jnp.where(is_g & (gl == gmax), lane, float(LANES)), axis=-1, keepdims=True)
    gprob = 1.0 / jnp.sum(jnp.exp(gl - gmax), axis=-1, keepdims=True)
    e_lo = ROUTE_LANE0 + gtop * EXPERTS_PER_GROUP
    in_grp = (lane >= e_lo) & (lane < e_lo + EXPERTS_PER_GROUP)
    el = jnp.where(in_grp, logits, NEG)
    m1 = jnp.max(el, axis=-1, keepdims=True)
    i1 = jnp.min(jnp.where(in_grp & (el == m1), lane, float(LANES)), axis=-1, keepdims=True)
    el2 = jnp.where(lane == i1, NEG, el)
    m2 = jnp.max(el2, axis=-1, keepdims=True)
    i2 = jnp.min(jnp.where(in_grp & (lane != i1) & (el2 == m2), lane, float(LANES)), axis=-1, keepdims=True)
    r21 = jnp.exp(m2 - m1)
    gate1 = gprob / (1.0 + r21)
    gate2 = gprob * r21 / (1.0 + r21)

    oh1 = lane == i1
    oh2 = lane == i2
    oh = jnp.where(oh1 | oh2, 1.0, 0.0)
    earlier = jnp.dot(tri_ref[...], oh.astype(jnp.bfloat16), preferred_element_type=jnp.float32)
    n_e = jnp.sum(oh, axis=0, keepdims=True)
    chunks_e = jnp.floor((n_e + (CHUNK - 1)) * (1.0 / CHUNK))
    seg0 = jnp.dot(jnp.broadcast_to(chunks_e, (SUBLANES, LANES)).astype(jnp.bfloat16), utri_ref[...],
                   preferred_element_type=jnp.float32)[0:1, :] * CHUNK
    base = earlier + seg0
    lp1 = jnp.sum(jnp.where(oh1, base, 0.0), axis=-1, keepdims=True)
    lp2 = jnp.sum(jnp.where(oh2, base, 0.0), axis=-1, keepdims=True)
    info = jnp.where(lane == 0, lp1, jnp.where(lane == 1, lp2, 0.0))
    rinfo_ref[...] = info
    cnt_ref[...] = jnp.broadcast_to(n_e, cnt_ref.shape)

    def pieces(g):
        hi = g.astype(jnp.bfloat16).astype(jnp.float32)
        mid = (g - hi).astype(jnp.bfloat16).astype(jnp.float32)
        return hi, mid, g - hi - mid

    aux = jnp.zeros((TOK_TILE, LANES), jnp.float32)
    for k, piece in enumerate(pieces(gate1) + pieces(gate2) + (i1,)):
        aux = jnp.where(lane == k, piece, aux)

    info_t = info.T
    prow = lax.broadcasted_iota(jnp.int32, (LOCAL_ROWS, TOK_TILE), 0).astype(jnp.float32)
    sel = jnp.where((prow == info_t[0:1, :]) | (prow == info_t[1:2, :]), 1.0, 0.0).astype(jnp.bfloat16)
    moved = jnp.dot(sel, jnp.concatenate([xn, aux.astype(jnp.bfloat16)], axis=1),
                    preferred_element_type=jnp.float32)
    xs_ref[...] = moved.astype(jnp.bfloat16)


def _mixer(layer, x2, qp, kv, layer_params, shared_tables, *, seq_len):
    nab, sinkcol, poolw, pools, wout, g2, rw, rb = layer_params
    swb, tri, utri = shared_tables
    T = x2.shape[0]
    nblk = seq_len // TOK_TILE
    n_tok_tiles = T // TOK_TILE

    def mixed(t):
        return jnp.minimum(t, n_tok_tiles - 1)

    def cur(t):
        return (mixed(t), 0)

    def prev(t):
        return (jnp.maximum(mixed(t) - 1, (mixed(t) // nblk) * nblk), 0)

    def nxt(t):
        return (jnp.minimum(mixed(t) + 1, (mixed(t) // nblk) * nblk + nblk - 1), 0)

    def dispatched(t):
        return (jnp.maximum(t - 1, 0), 0)

    def resident(a):
        zeros = (0,) * a.ndim
        return pl.BlockSpec(a.shape, lambda t: zeros, pipeline_mode=pl.Buffered(1))

    def resident_layer(a):
        index = (layer,) + (0,) * (a.ndim - 1)
        return pl.BlockSpec((None,) + a.shape[1:], lambda t: index, pipeline_mode=pl.Buffered(1))

    return pl.pallas_call(
        functools.partial(_mixer_kernel, seq_len=seq_len),
        grid=(n_tok_tiles + 1,),
        in_specs=[
            pl.BlockSpec((TOK_TILE, D_MODEL), cur),
            pl.BlockSpec((TOK_TILE, D_QP), cur),
            pl.BlockSpec((TOK_TILE, D_KV), prev),
            pl.BlockSpec((TOK_TILE, D_KV), cur),
            pl.BlockSpec((TOK_TILE, D_KV), nxt),
            resident_layer(nab), resident(swb), resident_layer(sinkcol), resident_layer(poolw),
            resident_layer(pools), resident_layer(wout), resident_layer(g2), resident_layer(rw),
            resident_layer(rb), resident(tri), resident(utri),
        ],
        out_specs=[
            pl.BlockSpec((TOK_TILE, D_MODEL), cur),
            pl.BlockSpec((LOCAL_ROWS, D_SLOT), dispatched),
            pl.BlockSpec((TOK_TILE, LANES), dispatched),
            pl.BlockSpec((SUBLANES, LANES), dispatched),
        ],
        out_shape=[
            jax.ShapeDtypeStruct((T, D_MODEL), jnp.float32),
            jax.ShapeDtypeStruct((n_tok_tiles * LOCAL_ROWS, D_SLOT), jnp.bfloat16),
            jax.ShapeDtypeStruct((T, LANES), jnp.float32),
            jax.ShapeDtypeStruct((n_tok_tiles * SUBLANES, LANES), jnp.float32),
        ],
        scratch_shapes=[
            pltpu.VMEM((3 * TOK_TILE, KW_COLS), jnp.bfloat16),
            pltpu.VMEM((TOK_TILE + 2 * HALO, D_C), jnp.float32),
            pltpu.VMEM((TOK_TILE, D_MIX), jnp.bfloat16),
            pltpu.VMEM((TOK_TILE, D_MODEL), jnp.bfloat16),
            pltpu.VMEM((TOK_TILE, LANES), jnp.float32),
        ],
        compiler_params=pltpu.CompilerParams(
            dimension_semantics=("arbitrary",), vmem_limit_bytes=VMEM_LIMIT),
        name="mixer",
    )(x2, qp, kv, kv, kv, nab, swb, sinkcol, poolw, pools, wout, g2, rw, rb, tri, utri)


def _chunk_copy(src_hbm, src_chunk, dst, dst_chunk, sem):
    return pltpu.make_async_copy(
        src_hbm.at[pl.ds(pl.multiple_of(src_chunk * CHUNK, CHUNK), CHUNK)],
        dst.at[pl.ds(dst_chunk * CHUNK, CHUNK)],
        sem)


def _expert_kernel(te_ref, nsub_ref, first_ref, wslot_ref, nexte_ref, csrc_ref,
                   xs_hbm, wg_hbm, wu_hbm, wd_hbm, y_ref,
                   xbuf, wgf, wuf, wdf, wgb, wub, wdb, sem, wsem, *, layer):
    j = pl.program_id(0)
    nt = pl.num_programs(0)
    slot = j % GATHER_SLOTS

    def start_gather(tile, s, h):
        for c in range(h * SUB_CHUNKS, (h + 1) * SUB_CHUNKS):
            _chunk_copy(xs_hbm, csrc_ref[tile * TILE_CHUNKS + c], xbuf.at[s], c, sem.at[s, h]).start()

    def wait_gather(s, h):
        rows = pl.ds(h * EXP_SUB, EXP_SUB)
        pltpu.make_async_copy(xs_hbm.at[pl.ds(0, EXP_SUB)], xbuf.at[s, rows], sem.at[s, h]).wait()

    def weight_copies(expert, ws):
        return [pltpu.make_async_copy(w_hbm.at[layer, expert], wbuf.at[ws], wsem.at[k, ws])
                for k, (w_hbm, wbuf) in enumerate(((wg_hbm, wgf), (wu_hbm, wuf), (wd_hbm, wdf)))]

    for h in range(EXP_SUBS):
        for first_tile in range(GATHER_AHEAD):
            @pl.when((j == 0) & (h < nsub_ref[first_tile]))
            def _():
                start_gather(first_tile, first_tile, h)

        ahead = j + GATHER_AHEAD

        @pl.when(h < nsub_ref[jnp.minimum(ahead, nt - 1)] * (ahead < nt).astype(jnp.int32))
        def _():
            start_gather(ahead, ahead % GATHER_SLOTS, h)

    ws = wslot_ref[j]

    @pl.when(j == 0)
    def _():
        for cp in weight_copies(te_ref[0], 0):
            cp.start()

    @pl.when(first_ref[j] > 0)
    def _():
        for cp in weight_copies(te_ref[j], ws):
            cp.wait()

        @pl.when(nexte_ref[j] >= 0)
        def _():
            for cp in weight_copies(nexte_ref[j], 1 - ws):
                cp.start()

        wgb[...] = wgf[ws].astype(jnp.bfloat16)
        wub[...] = wuf[ws].astype(jnp.bfloat16)
        wdb[...] = wdf[ws].astype(jnp.bfloat16)

    def gated_mlp(n_rows):
        xs = xbuf[slot, 0:n_rows, 0:D_MODEL]
        gp = xbuf[slot, 0:n_rows, D_MODEL:D_SLOT].astype(jnp.float32)
        first = gp[:, FIRST_EXPERT_LANE:FIRST_EXPERT_LANE + 1] == (te_ref[j] + ROUTE_LANE0).astype(jnp.float32)

        def gate_of(lanes):
            return sum(gp[:, k:k + 1] for k in lanes)

        gate_w = jnp.where(first, gate_of(GATE1_LANES), gate_of(GATE2_LANES))
        gate = jnp.dot(xs, wgb[...], preferred_element_type=jnp.float32)
        up = jnp.dot(xs, wub[...], preferred_element_type=jnp.float32)
        act = (gate * (1.0 / (1.0 + jnp.exp(-gate))) * up * gate_w).astype(jnp.bfloat16)
        return jnp.dot(act, wdb[...], preferred_element_type=jnp.float32).astype(jnp.bfloat16)

    for k in range(1, EXP_SUBS + 1):
        @pl.when(nsub_ref[j] == k)
        def _():
            for h in range(k):
                wait_gather(slot, h)
            y_ref[0:k * EXP_SUB, :] = gated_mlp(k * EXP_SUB)
            if k < EXP_SUBS:
                y_ref[k * EXP_SUB:EXP_TILE, :] = jnp.zeros((EXP_TILE - k * EXP_SUB, D_MODEL), jnp.bfloat16)

    @pl.when(nsub_ref[j] == 0)
    def _():
        y_ref[...] = jnp.zeros_like(y_ref)


def _experts(layer, tile_tables, chunk_src, xs_local, wg, wu, wd):
    tile_expert, n_sub, first, wslot, next_expert = tile_tables
    n_tiles = tile_expert.shape[0]
    any_space = pl.BlockSpec(memory_space=pl.ANY)

    return pl.pallas_call(
        functools.partial(_expert_kernel, layer=layer),
        grid_spec=pltpu.PrefetchScalarGridSpec(
            num_scalar_prefetch=6,
            grid=(n_tiles,),
            in_specs=[any_space] * 4,
            out_specs=pl.BlockSpec((EXP_TILE, D_MODEL), lambda j, *tables: (j, 0)),
            scratch_shapes=[
                pltpu.VMEM((GATHER_SLOTS, EXP_TILE, D_SLOT), jnp.bfloat16),
                pltpu.VMEM((2, D_MODEL, D_EXPERT), jnp.float32),
                pltpu.VMEM((2, D_MODEL, D_EXPERT), jnp.float32),
                pltpu.VMEM((2, D_EXPERT, D_MODEL), jnp.float32),
                pltpu.VMEM((D_MODEL, D_EXPERT), jnp.bfloat16),
                pltpu.VMEM((D_MODEL, D_EXPERT), jnp.bfloat16),
                pltpu.VMEM((D_EXPERT, D_MODEL), jnp.bfloat16),
                pltpu.SemaphoreType.DMA((GATHER_SLOTS, EXP_SUBS)),
                pltpu.SemaphoreType.DMA((3, 2)),
            ],
        ),
        out_shape=jax.ShapeDtypeStruct((n_tiles * EXP_TILE, D_MODEL), jnp.bfloat16),
        compiler_params=pltpu.CompilerParams(
            dimension_semantics=("arbitrary",), vmem_limit_bytes=VMEM_LIMIT),
        name="experts",
    )(tile_expert, n_sub, first, wslot, next_expert, chunk_src, xs_local, wg, wu, wd)


def _combine_tile(ctab_ref, xmid_ref, rinfo_ref, ys_hbm, ybuf, sem):
    i = pl.program_id(0)
    nt = pl.num_programs(0)
    slot = i % GATHER_SLOTS

    def start_gather(tile, s):
        for c in range(LOCAL_CHUNKS):
            _chunk_copy(ys_hbm, ctab_ref[tile * LOCAL_CHUNKS + c], ybuf.at[s], c, sem.at[s]).start()

    def wait_gather(s):
        pltpu.make_async_copy(ys_hbm.at[pl.ds(0, LOCAL_ROWS)], ybuf.at[s], sem.at[s]).wait()

    @pl.when(i == 0)
    def _():
        for first_tile in range(GATHER_AHEAD):
            start_gather(first_tile, first_tile)

    @pl.when(i + GATHER_AHEAD < nt)
    def _():
        start_gather(i + GATHER_AHEAD, (i + GATHER_AHEAD) % GATHER_SLOTS)

    wait_gather(slot)
    info = rinfo_ref[...]
    pcol = lax.broadcasted_iota(jnp.int32, (TOK_TILE, LOCAL_ROWS), 1).astype(jnp.float32)
    pick = jnp.where((pcol == info[:, 0:1]) | (pcol == info[:, 1:2]), 1.0, 0.0).astype(jnp.bfloat16)
    return xmid_ref[...] + jnp.dot(pick, ybuf[slot], preferred_element_type=jnp.float32)


def _combine_final_kernel(ctab_ref, xmid_ref, rinfo_ref, g_ref, ys_hbm, out_ref, ybuf, sem):
    x = _combine_tile(ctab_ref, xmid_ref, rinfo_ref, ys_hbm, ybuf, sem)
    out_ref[...] = _rmsnorm_f32(x, g_ref[...])


def _combine_proj_kernel(ctab_ref, xmid_ref, rinfo_ref, g_ref, w_ref, ys_hbm, x_ref, qp_ref, kv_ref, ybuf, sem):
    x = _combine_tile(ctab_ref, xmid_ref, rinfo_ref, ys_hbm, ybuf, sem)
    x_ref[...] = x
    xn = _rmsnorm_f32(x, g_ref[...]).astype(jnp.bfloat16)
    proj = jnp.dot(xn, w_ref[...], preferred_element_type=jnp.float32)
    qp_ref[...] = proj[:, :D_QP].astype(jnp.bfloat16)
    kv_ref[...] = proj[:, D_QP:].astype(jnp.bfloat16)


def _combine_proj(next_layer, chunk_tab, xmid, rinfo, g1, w_in_k, ys):
    T = xmid.shape[0]

    def tile(i, ct):
        return (i, 0)

    return pl.pallas_call(
        _combine_proj_kernel,
        grid_spec=pltpu.PrefetchScalarGridSpec(
            num_scalar_prefetch=1,
            grid=(T // TOK_TILE,),
            in_specs=[
                pl.BlockSpec((TOK_TILE, D_MODEL), tile),
                pl.BlockSpec((TOK_TILE, LANES), tile),
                pl.BlockSpec((None, 1, D_MODEL), lambda i, ct: (next_layer, 0, 0)),
                pl.BlockSpec((None, D_MODEL, D_IN), lambda i, ct: (next_layer, 0, 0),
                             pipeline_mode=pl.Buffered(1)),
                pl.BlockSpec(memory_space=pl.ANY),
            ],
            out_specs=[
                pl.BlockSpec((TOK_TILE, D_MODEL), tile),
                pl.BlockSpec((TOK_TILE, D_QP), tile),
                pl.BlockSpec((TOK_TILE, D_KV), tile),
            ],
            scratch_shapes=[
                pltpu.VMEM((GATHER_SLOTS, LOCAL_ROWS, D_MODEL), jnp.bfloat16),
                pltpu.SemaphoreType.DMA((GATHER_SLOTS,)),
            ],
        ),
        out_shape=[
            jax.ShapeDtypeStruct((T, D_MODEL), jnp.float32),
            jax.ShapeDtypeStruct((T, D_QP), jnp.bfloat16),
            jax.ShapeDtypeStruct((T, D_KV), jnp.bfloat16),
        ],
        compiler_params=pltpu.CompilerParams(
            dimension_semantics=("arbitrary",), vmem_limit_bytes=VMEM_LIMIT),
        name="combine_proj",
    )(chunk_tab, xmid, rinfo, g1, w_in_k, ys)


def _combine_final(chunk_tab, xmid, rinfo, g, ys):
    T = xmid.shape[0]
    return pl.pallas_call(
        _combine_final_kernel,
        grid_spec=pltpu.PrefetchScalarGridSpec(
            num_scalar_prefetch=1,
            grid=(T // TOK_TILE,),
            in_specs=[
                pl.BlockSpec((TOK_TILE, D_MODEL), lambda i, ct: (i, 0)),
                pl.BlockSpec((TOK_TILE, LANES), lambda i, ct: (i, 0)),
                pl.BlockSpec((1, D_MODEL), lambda i, ct: (0, 0)),
                pl.BlockSpec(memory_space=pl.ANY),
            ],
            out_specs=pl.BlockSpec((TOK_TILE, D_MODEL), lambda i, ct: (i, 0)),
            scratch_shapes=[
                pltpu.VMEM((GATHER_SLOTS, LOCAL_ROWS, D_MODEL), jnp.bfloat16),
                pltpu.SemaphoreType.DMA((GATHER_SLOTS,)),
            ],
        ),
        out_shape=jax.ShapeDtypeStruct((T, D_MODEL), jnp.float32),
        compiler_params=pltpu.CompilerParams(
            dimension_semantics=("arbitrary",), vmem_limit_bytes=VMEM_LIMIT),
        name="combine",
    )(chunk_tab, xmid, rinfo, g, ys)


def _pair_heads(a, axis):
    shape = a.shape
    split = shape[:axis] + (SWA_KV_HEADS, SWA_REP, HEAD_DIM) + shape[axis + 1:]
    return jnp.swapaxes(a.reshape(split), axis, axis + 1).reshape(shape)


def _in_proj_weight(w):
    off_bq = 3 * D_A
    return jnp.concatenate(
        [w[..., 0:D_A], _pair_heads(w[..., off_bq:off_bq + D_B], w.ndim - 1), w[..., D_A:off_bq],
         w[..., off_bq + D_B:]], axis=-1).astype(jnp.bfloat16)


def _out_proj_weight(w):
    return jnp.concatenate(
        [w[:, 0:D_A], _pair_heads(w[:, D_A:D_A + D_B], 1), w[:, D_A + D_B:]], axis=1).astype(jnp.bfloat16)


def _na_bias_table(rel_bias):
    c = np.arange(GRID_W)[:, None]
    cp = np.arange(GRID_W)[None, :]
    cs = np.clip(c - NA_COLS // 2, 0, GRID_W - NA_COLS)
    valid = (cp >= cs) & (cp < cs + NA_COLS)
    d = np.arange(2 * NA_COLS - 1)[:, None, None]
    col_sel = ((cp - c + (NA_COLS - 1))[None] == d) & valid[None]
    k = np.arange(NA_ROWS)[:, None, None]
    j = np.arange(NA_ROWS)[None, :, None]
    r = np.arange(2 * NA_ROWS - 1)[None, None, :]
    row_sel = r == j - k + (NA_ROWS - 1)
    tab = jnp.einsum("kjr,lhrd,dcm->lkhcjm", jnp.asarray(row_sel, jnp.float32), rel_bias.astype(jnp.float32),
                     jnp.asarray(col_sel, jnp.float32), precision=lax.Precision.HIGHEST)
    tab = jnp.where(jnp.asarray(valid)[None, None, None, :, None, :], tab * LOG2E, NEG)
    return tab.reshape(rel_bias.shape[0], NA_ROWS, NA_HEADS * GRID_W, NA_ROWS * GRID_W)


def _swa_bias_table():
    slopes = (2.0 ** (-8.0 * np.arange(1, SWA_Q_HEADS + 1) / SWA_Q_HEADS)).astype(np.float32)
    qi = np.arange(SWA_BLOCK)[:, None]
    ki = np.arange(3 * SWA_BLOCK)[None, :]
    dist = np.abs(ki - qi - SWA_BLOCK).astype(np.float32)
    tab = np.where(dist <= SWA_WINDOW, -slopes[:, None, None] * dist[None] * LOG2E, np.float32(NEG))
    tab = tab.reshape(SWA_Q_HEADS * SWA_BLOCK, 3 * SWA_BLOCK).astype(np.float32)
    first, last = tab.copy(), tab.copy()
    first[:, :SWA_BLOCK] = NEG
    last[:, 2 * SWA_BLOCK:] = NEG
    return jnp.asarray(np.stack([first, tab, last]))


def _block_diag(pool_w):
    depth, n = pool_w.shape[0:2]
    eye = jnp.asarray(np.eye(n, dtype=np.float32))
    out = pool_w[:, :, :, None, :] * eye[None, :, None, :, None]
    return out.reshape(depth, n * POOL_GROUP_DIM, n * POOL_GROUP_DIM)


def _router_weights(rg_w, rg_b, re_w, re_b):
    def lanes(g, e):
        gap = jnp.zeros(g.shape[:-1] + (ROUTE_LANE0 - N_GROUPS,), jnp.float32)
        tail = jnp.zeros(g.shape[:-1] + (LANES - ROUTE_LANE0 - N_EXPERTS,), jnp.float32)
        return jnp.concatenate([g.astype(jnp.float32), gap, e.astype(jnp.float32), tail], axis=-1)

    return lanes(rg_w, re_w).astype(jnp.bfloat16), lanes(rg_b, re_b)[:, None, :]


def _dispatch_tables(cnt, n_tiles):
    nb = cnt.shape[0] // SUBLANES
    n = cnt.reshape(nb, SUBLANES, LANES)[:, 0, ROUTE_LANE0:ROUTE_LANE0 + N_EXPERTS].astype(jnp.int32)
    g = (n + (CHUNK - 1)) // CHUNK
    l_end = jnp.cumsum(g, axis=1)
    l_off = l_end - g
    c_end = jnp.cumsum(g, axis=0)
    c_off = c_end - g
    tot = c_end[-1]
    tiles = (tot + (TILE_CHUNKS - 1)) // TILE_CHUNKS
    t_end = jnp.cumsum(tiles)
    t_off = t_end - tiles
    n_used = t_end[-1:]

    experts = jnp.arange(N_EXPERTS, dtype=jnp.int32)
    tile_ids = jnp.arange(n_tiles, dtype=jnp.int32)
    tile_expert = jnp.minimum(jnp.sum((t_end[None, :] <= tile_ids[:, None]).astype(jnp.int32), axis=1),
                              N_EXPERTS - 1)
    oh_te = (tile_expert[:, None] == experts[None, :]).astype(jnp.int32)
    left = jnp.sum(oh_te * (tot + t_off * TILE_CHUNKS)[None, :], axis=1) - tile_ids * TILE_CHUNKS
    n_sub = jnp.clip((left + (SUB_CHUNKS - 1)) // SUB_CHUNKS, 0, EXP_SUBS)
    has_rows = tiles > 0
    first = ((tile_ids == jnp.sum(oh_te * t_off[None, :], axis=1)) & (n_sub > 0)).astype(jnp.int32)
    wslot = jnp.sum(oh_te * ((jnp.cumsum(has_rows.astype(jnp.int32)) - 1) % 2)[None, :], axis=1)
    later = (experts[None, :] > experts[:, None]) & has_rows[None, :]
    nxt = jnp.min(jnp.where(later, experts[None, :], N_EXPERTS), axis=1)
    next_expert = jnp.sum(oh_te * jnp.where(nxt < N_EXPERTS, nxt, -1)[None, :], axis=1)
    tile_tables = (tile_expert, n_sub, first, wslot, next_expert)

    q = jnp.arange(n_tiles * TILE_CHUNKS, dtype=jnp.int32)
    tile_q = q // TILE_CHUNKS
    oh_e = (jnp.repeat(tile_expert, TILE_CHUNKS)[:, None] == experts[None, :]).astype(jnp.int32)
    ro = q - jnp.sum(oh_e * t_off[None, :], axis=1) * TILE_CHUNKS
    valid = (ro < jnp.sum(oh_e * tot[None, :], axis=1)) & (tile_q < n_used[0])
    cols = jnp.dot(jnp.concatenate([c_end, c_off, l_off], axis=0).astype(jnp.float32),
                   oh_e.T.astype(jnp.float32), precision=lax.Precision.HIGHEST).astype(jnp.int32)
    c_end_q, c_off_q, l_off_q = cols[0:nb], cols[nb:2 * nb], cols[2 * nb:3 * nb]
    b_q = jnp.minimum(jnp.sum((c_end_q <= ro[None, :]).astype(jnp.int32), axis=0), nb - 1)
    oh_b = (jnp.arange(nb, dtype=jnp.int32)[:, None] == b_q[None, :]).astype(jnp.int32)
    src = b_q * LOCAL_CHUNKS + jnp.sum(oh_b * (l_off_q + ro[None, :] - c_off_q), axis=0)
    chunk_src = jnp.where(valid, src, LOCAL_CHUNKS - 1)

    c = jnp.arange(LOCAL_CHUNKS, dtype=jnp.int32)
    e_c = jnp.minimum(jnp.sum((l_end[:, None, :] <= c[None, :, None]).astype(jnp.int32), axis=2),
                      N_EXPERTS - 1)
    oh_ec = (e_c[:, :, None] == experts[None, None, :]).astype(jnp.int32)
    pos = (jnp.sum(oh_ec * (t_off[None, None, :] * TILE_CHUNKS + c_off[:, None, :] - l_off[:, None, :]), axis=2)
           + c[None, :])
    chunk_tab = jnp.where(c[None, :] < l_end[:, -1:], pos, 0).reshape(-1)
    return tile_tables, chunk_src, chunk_tab


def kernel(x, norm1_g, w_in, nat_bias, swa_sink, pool_w, pool_scale, w_out, norm2_g, router_g_w,
           router_g_b, router_e_w, router_e_b, expert_w_gate, expert_w_up, expert_w_down, final_g):
    batch, seq_len, _ = x.shape
    depth = w_in.shape[0]
    T = batch * seq_len
    assert seq_len % TOK_TILE == 0 and TOK_TILE % SWA_BLOCK == 0 and TOK_TILE % GRID_W == 0
    max_chunks = (2 * T) // CHUNK + (T // TOK_TILE) * N_EXPERTS
    n_tiles = max_chunks // TILE_CHUNKS + N_EXPERTS

    swb = _swa_bias_table()
    tri = jnp.asarray(np.tril(np.ones((TOK_TILE, TOK_TILE), np.float32), -1)).astype(jnp.bfloat16)
    utri = jnp.asarray(np.triu(np.ones((LANES, LANES), np.float32), 1)).astype(jnp.bfloat16)

    w_in_k = _in_proj_weight(w_in)
    w_out_k = _out_proj_weight(w_out)
    nab = _na_bias_table(nat_bias)
    sinkcol = jnp.broadcast_to((swa_sink.astype(jnp.float32) * LOG2E)[:, :, None, None],
                               (depth, SWA_Q_HEADS, SWA_BLOCK, LANES)).reshape(depth, SWA_Q_HEADS * SWA_BLOCK, LANES)
    poolw = _block_diag(pool_w).astype(jnp.bfloat16)
    pools = pool_scale.reshape(depth, 1, D_C).astype(jnp.float32)
    rw, rb = _router_weights(router_g_w, router_g_b, router_e_w, router_e_b)
    g1 = norm1_g.reshape(depth, 1, D_MODEL)
    g2 = norm2_g.reshape(depth, 1, D_MODEL)

    x2 = x.reshape(T, D_MODEL)
    qp, kv = _norm_proj(0, x2, g1, w_in_k)
    for l in range(depth):
        xmid, xs_local, rinfo, cnt = _mixer(
            l, x2, qp, kv, (nab, sinkcol, poolw, pools, w_out_k, g2, rw, rb), (swb, tri, utri),
            seq_len=seq_len)
        tile_tables, chunk_src, chunk_tab = _dispatch_tables(cnt, n_tiles)
        ys = _experts(l, tile_tables, chunk_src, xs_local,
                      expert_w_gate, expert_w_up, expert_w_down)
        if l + 1 < depth:
            x2, qp, kv = _combine_proj(l + 1, chunk_tab, xmid, rinfo, g1, w_in_k, ys)
        else:
            x2 = _combine_final(chunk_tab, xmid, rinfo, final_g.reshape(1, D_MODEL), ys)
    return x2.reshape(batch, seq_len, D_MODEL)
```

```python
import functools

import jax
import jax.numpy as jnp
import numpy as np
from jax import lax
from jax.experimental import pallas as pl
from jax.experimental.pallas import tpu as pltpu

D_MODEL = 1024
GRID_W = 64
HEAD_DIM = 64
NA_HEADS = 4
NA_ROWS = 8
NA_COLS = 16
SWA_Q_HEADS = 8
SWA_KV_HEADS = 2
SWA_REP = SWA_Q_HEADS // SWA_KV_HEADS
SWA_WINDOW = 128
SWA_BLOCK = 128
POOL_WINDOWS = (2, 4, 8, 16)
POOL_GROUP_DIM = 64
D_A = NA_HEADS * HEAD_DIM
D_B = SWA_Q_HEADS * HEAD_DIM
D_BKV = SWA_KV_HEADS * HEAD_DIM
D_C = len(POOL_WINDOWS) * POOL_GROUP_DIM
D_MIX = D_A + D_B + D_C
D_QP = D_A + D_B
D_KV = 2 * D_A + 2 * D_BKV + D_C
D_IN = D_QP + D_KV
N_GROUPS = 4
EXPERTS_PER_GROUP = 8
N_EXPERTS = N_GROUPS * EXPERTS_PER_GROUP
D_EXPERT = 256
RMS_EPS = 1e-6
NEG = -1e30
LOG2E = 1.4426950408889634
QK_SCALE = HEAD_DIM ** -0.5 * LOG2E

LANES = 128
SUBLANES = 8
ROW_CHUNKS = D_MODEL // LANES

TOK_TILE = 512
EXP_SUB = 256
EXP_SUBS = 4
EXP_TILE = EXP_SUB * EXP_SUBS
CHUNK = 16
LOCAL_CHUNKS = (2 * TOK_TILE + N_EXPERTS * (CHUNK - 1)) // CHUNK + 2
LOCAL_ROWS = LOCAL_CHUNKS * CHUNK
SUB_CHUNKS = EXP_SUB // CHUNK
TILE_CHUNKS = EXP_TILE // CHUNK
GATHER_AHEAD = 2
GATHER_SLOTS = GATHER_AHEAD + 1
ROUTE_LANE0 = 8
D_SLOT = D_MODEL + LANES
GATE1_LANES = (0, 1, 2)
GATE2_LANES = (3, 4, 5)
FIRST_EXPERT_LANE = 6
NA_ROWS_PER_STEP = 8
SWA_BLOCKS_PER_STEP = 4
HALO = 8
WIN_TILES = 3
VMEM_LIMIT = 56 * 1024 * 1024


def _rmsnorm_f32(x, g):
    return x * lax.rsqrt(jnp.mean(x * x, axis=-1, keepdims=True) + RMS_EPS) * g


def _norm_proj_kernel(x_ref, g_ref, w_ref, qp_ref, kv_ref):
    xn = _rmsnorm_f32(x_ref[...], g_ref[...]).astype(jnp.bfloat16)
    proj = jnp.dot(xn, w_ref[...], preferred_element_type=jnp.float32)
    qp_ref[...] = proj[:, :D_QP].astype(jnp.bfloat16)
    kv_ref[...] = proj[:, D_QP:].astype(jnp.bfloat16)


def _norm_proj(layer, x2, g, w):
    T = x2.shape[0]
    return pl.pallas_call(
        _norm_proj_kernel,
        grid=(T // TOK_TILE,),
        in_specs=[
            pl.BlockSpec((TOK_TILE, D_MODEL), lambda i: (i, 0)),
            pl.BlockSpec((None, 1, D_MODEL), lambda i: (layer, 0, 0)),
            pl.BlockSpec((None, D_MODEL, D_IN), lambda i: (layer, 0, 0)),
        ],
        out_specs=[
            pl.BlockSpec((TOK_TILE, D_QP), lambda i: (i, 0)),
            pl.BlockSpec((TOK_TILE, D_KV), lambda i: (i, 0)),
        ],
        out_shape=[
            jax.ShapeDtypeStruct((T, D_QP), jnp.bfloat16),
            jax.ShapeDtypeStruct((T, D_KV), jnp.bfloat16),
        ],
        compiler_params=pltpu.CompilerParams(
            dimension_semantics=("arbitrary",), vmem_limit_bytes=VMEM_LIMIT),
        name="norm_proj",
    )(x2, g, w)


KW_AK, KW_AV, KW_BK, KW_BV = 0, D_A, 2 * D_A, 2 * D_A + D_BKV
KW_COLS = 2 * D_A + 2 * D_BKV
KV_CU = KW_COLS


def _mixer_kernel(x_ref, qp_ref, kwin_ref, nab_ref, swb_ref, sink_ref,
                  poolw_ref, pools_ref, wout_ref, g2_ref, rw_ref, rb_ref, tri_ref, utri_ref,
                  xmid_ref, xs_ref, rinfo_ref, cnt_ref,
                  uwin, mix, xn_scr, logit_scr, *, seq_len):
    t = pl.program_id(0)
    nblk = seq_len // TOK_TILE
    i = jnp.minimum(t, pl.num_programs(0) - 2) % nblk
    rows_per_tile = TOK_TILE // GRID_W
    grid_rows = seq_len // GRID_W

    @pl.when(t == 0)
    def _():
        xn_scr[...] = jnp.zeros_like(xn_scr)
        logit_scr[...] = jnp.zeros_like(logit_scr)

    cur_off = pl.multiple_of((i - jnp.clip(i - 1, 0, nblk - WIN_TILES)) * TOK_TILE, TOK_TILE)

    lane_a = lax.broadcasted_iota(jnp.int32, (GRID_W, D_A), 1) // HEAD_DIM

    def na_row(rr):
        r = i * rows_per_tile + rr
        rs = jnp.clip(r - NA_ROWS // 2, 0, grid_rows - NA_ROWS)
        variant = r - rs
        start = pl.multiple_of(cur_off + (rs - i * rows_per_tile) * GRID_W, GRID_W)
        q0 = pl.multiple_of(rr * GRID_W, GRID_W)
        q = (qp_ref[pl.ds(q0, GRID_W), 0:D_A].astype(jnp.float32) * QK_SCALE).astype(jnp.bfloat16)
        zero = jnp.zeros_like(q)
        qs = jnp.concatenate([jnp.where(lane_a == h, q, zero) for h in range(NA_HEADS)], axis=0)
        kw = kwin_ref[pl.ds(start, NA_ROWS * GRID_W), KW_AK:KW_AK + D_A]
        vw = kwin_ref[pl.ds(start, NA_ROWS * GRID_W), KW_AV:KW_AV + D_A]
        s = lax.dot_general(qs, kw, (((1,), (1,)), ((), ())), preferred_element_type=jnp.float32)
        s = s + nab_ref[variant]
        m = jnp.max(s, axis=-1, keepdims=True)
        p = jnp.exp2(s - m)
        l = jnp.sum(p, axis=-1, keepdims=True)
        pv = jnp.dot(p.astype(jnp.bfloat16), vw, preferred_element_type=jnp.float32)
        pv = pv * (1.0 / l)
        o = jnp.zeros((GRID_W, D_A), jnp.float32)
        for h in range(NA_HEADS):
            o = o + jnp.where(lane_a == h, pv[h * GRID_W:(h + 1) * GRID_W, :], 0.0)
        mix[pl.ds(q0, GRID_W), 0:D_A] = o.astype(jnp.bfloat16)

    def na_step(it, c):
        for k in range(NA_ROWS_PER_STEP):
            na_row(it * NA_ROWS_PER_STEP + k)
        return c

    lax.fori_loop(0, rows_per_tile // NA_ROWS_PER_STEP, na_step, 0)

    lane_b = lax.broadcasted_iota(jnp.int32, (SWA_BLOCK, LANES), 1) // HEAD_DIM
    blocks_per_tile = TOK_TILE // SWA_BLOCK
    nblocks = seq_len // SWA_BLOCK

    ones_v = jnp.ones((3 * SWA_BLOCK, LANES), jnp.bfloat16)

    def swa_step(sb, c):
        n = i * blocks_per_tile + sb
        variant = jnp.where(n == 0, 0, jnp.where(n == nblocks - 1, 2, 1))
        q0 = pl.multiple_of(sb * SWA_BLOCK, SWA_BLOCK)
        k0 = pl.multiple_of(cur_off + (sb - variant) * SWA_BLOCK, SWA_BLOCK)
        kw = kwin_ref[pl.ds(k0, 3 * SWA_BLOCK), KW_BK:KW_BK + D_BKV]
        vaug = jnp.concatenate([kwin_ref[pl.ds(k0, 3 * SWA_BLOCK), KW_BV:KW_BV + D_BKV], ones_v], axis=1)
        outs = []
        for g in range(SWA_KV_HEADS):
            pieces = []
            for t in range(SWA_REP):
                qt = qp_ref[pl.ds(q0, SWA_BLOCK), D_A + t * LANES:D_A + (t + 1) * LANES]
                qt = (qt.astype(jnp.float32) * QK_SCALE).astype(jnp.bfloat16)
                pieces.append(jnp.where(lane_b == g, qt, jnp.zeros_like(qt)))
            qs = jnp.concatenate(pieces, axis=0)
            r0 = g * SWA_REP * SWA_BLOCK
            s = lax.dot_general(qs, kw, (((1,), (1,)), ((), ())), preferred_element_type=jnp.float32)
            s = s + swb_ref[variant, r0:r0 + SWA_REP * SWA_BLOCK, :]
            sink = sink_ref[r0:r0 + SWA_REP * SWA_BLOCK, :]
            m = jnp.broadcast_to(jnp.max(s, axis=-1, keepdims=True), sink.shape)
            m = jnp.maximum(m, sink)
            p = jnp.exp2(s - jnp.concatenate([m, m, m], axis=1)).astype(jnp.bfloat16)
            pv = jnp.dot(p, vaug, preferred_element_type=jnp.float32)
            l = pv[:, LANES:2 * LANES] + jnp.exp2(sink - m)
            outs.append(pv[:, 0:LANES] * (1.0 / l))
        for t in range(SWA_REP):
            o0 = outs[0][t * SWA_BLOCK:(t + 1) * SWA_BLOCK, :]
            o1 = outs[1][t * SWA_BLOCK:(t + 1) * SWA_BLOCK, :]
            ot = jnp.where(lane_b == 0, o0, o1)
            mix[pl.ds(q0, SWA_BLOCK), D_A + t * LANES:D_A + (t + 1) * LANES] = ot.astype(jnp.bfloat16)
        return c

    def swa_pair(it, c):
        for k in range(SWA_BLOCKS_PER_STEP):
            swa_step(it * SWA_BLOCKS_PER_STEP + k, c)
        return c

    lax.fori_loop(0, blocks_per_tile // SWA_BLOCKS_PER_STEP, swa_pair, 0)

    u = kwin_ref[pl.ds(cur_off, TOK_TILE), KV_CU:KV_CU + D_C].astype(jnp.float32)
    prev_ok = (i > 0).astype(jnp.float32)
    next_ok = (i < nblk - 1).astype(jnp.float32)
    before = pl.multiple_of(jnp.maximum(cur_off - CHUNK, 0), CHUNK)
    after = pl.multiple_of(jnp.minimum(cur_off + TOK_TILE, WIN_TILES * TOK_TILE - CHUNK), CHUNK)
    u_before = kwin_ref[pl.ds(before, CHUNK), KV_CU:KV_CU + D_C].astype(jnp.float32)
    u_after = kwin_ref[pl.ds(after, CHUNK), KV_CU:KV_CU + D_C].astype(jnp.float32)
    uwin[0:HALO, :] = u_before[CHUNK - HALO:CHUNK, :] * prev_ok
    uwin[HALO:HALO + TOK_TILE, :] = u
    uwin[HALO + TOK_TILE:2 * HALO + TOK_TILE, :] = u_after[0:HALO, :] * next_ok
    n_ext = TOK_TILE + 2 * HALO
    a2 = uwin[0:n_ext - 1, :] + uwin[1:n_ext, :]
    a4 = a2[0:n_ext - 3, :] + a2[2:n_ext - 1, :]
    a8 = a4[0:n_ext - 7, :] + a4[4:n_ext - 3, :]
    a16 = a8[0:n_ext - 15, :] + a8[8:n_ext - 7, :]
    w2 = a2[7:7 + TOK_TILE, :]
    w4 = a4[6:6 + TOK_TILE, :]
    w8 = a8[4:4 + TOK_TILE, :]
    w16 = a16[0:TOK_TILE, :]
    lane_c = lax.broadcasted_iota(jnp.int32, (TOK_TILE, D_C), 1) // POOL_GROUP_DIM
    pooled = jnp.where(lane_c == 0, w2, jnp.where(lane_c == 1, w4, jnp.where(lane_c == 2, w8, w16)))
    half = jnp.where(lane_c == 0, 1, jnp.where(lane_c == 1, 2, jnp.where(lane_c == 2, 4, 8)))
    pos = i * TOK_TILE + lax.broadcasted_iota(jnp.int32, (TOK_TILE, D_C), 0)
    cnt = (jnp.minimum(pos + half, seq_len) - jnp.maximum(pos - half, 0)).astype(jnp.float32)
    d = (pooled / cnt - u).astype(jnp.bfloat16)
    oc = jnp.dot(d, poolw_ref[...], preferred_element_type=jnp.float32) * pools_ref[...]
    mix[:, D_A + D_B:D_MIX] = oc.astype(jnp.bfloat16)

    xn = xn_scr[...]
    logits = logit_scr[...]

    xm = x_ref[...] + jnp.dot(mix[...], wout_ref[...], preferred_element_type=jnp.float32)
    xmid_ref[...] = xm
    xn_new = _rmsnorm_f32(xm, g2_ref[...]).astype(jnp.bfloat16)
    xn_scr[...] = xn_new
    logit_scr[...] = jnp.dot(xn_new, rw_ref[...], preferred_element_type=jnp.float32) + rb_ref[...]

    lane = lax.broadcasted_iota(jnp.int32, (TOK_TILE, LANES), 1).astype(jnp.float32)
    is_g = lane < N_GROUPS
    gl = jnp.where(is_g, logits, NEG)
    gmax = jnp.max(gl, axis=-1, keepdims=True)
    gtop = jnp.min(jnp.where(is_g & (gl == gmax), lane, float(LANES)), axis=-1, keepdims=True)
    gprob = 1.0 / jnp.sum(jnp.exp(gl - gmax), axis=-1, keepdims=True)
    e_lo = ROUTE_LANE0 + gtop * EXPERTS_PER_GROUP
    in_grp = (lane >= e_lo) & (lane < e_lo + EXPERTS_PER_GROUP)
    el = jnp.where(in_grp, logits, NEG)
    m1 = jnp.max(el, axis=-1, keepdims=True)
    i1 = jnp.min(jnp.where(in_grp & (el == m1), lane, float(LANES)), axis=-1, keepdims=True)
    el2 = jnp.where(lane == i1, NEG, el)
    m2 = jnp.max(el2, axis=-1, keepdims=True)
    i2 = jnp.min(jnp.where(in_grp & (lane != i1) & (el2 == m2), lane, float(LANES)), axis=-1, keepdims=True)
    r21 = jnp.exp(m2 - m1)
    gate1 = gprob / (1.0 + r21)
    gate2 = gprob * r21 / (1.0 + r21)

    oh1 = lane == i1
    oh2 = lane == i2
    oh = jnp.where(oh1 | oh2, 1.0, 0.0)
    earlier = jnp.dot(tri_ref[...], oh.astype(jnp.bfloat16), preferred_element_type=jnp.float32)
    n_e = jnp.sum(oh, axis=0, keepdims=True)
    chunks_e = jnp.floor((n_e + (CHUNK - 1)) * (1.0 / CHUNK))
    seg0 = jnp.dot(jnp.broadcast_to(chunks_e, (SUBLANES, LANES)).astype(jnp.bfloat16), utri_ref[...],
                   preferred_element_type=jnp.float32)[0:1, :] * CHUNK
    base = earlier + seg0
    lp1 = jnp.sum(jnp.where(oh1, base, 0.0), axis=-1, keepdims=True)
    lp2 = jnp.sum(jnp.where(oh2, base, 0.0), axis=-1, keepdims=True)
    info = jnp.where(lane == 0, lp1, jnp.where(lane == 1, lp2, 0.0))
    rinfo_ref[...] = info
    cnt_ref[...] = jnp.broadcast_to(n_e, cnt_ref.shape)

    def pieces(g):
        hi = g.astype(jnp.bfloat16).astype(jnp.float32)
        mid = (g - hi).astype(jnp.bfloat16).astype(jnp.float32)
        return hi, mid, g - hi - mid

    aux = jnp.zeros((TOK_TILE, LANES), jnp.float32)
    for k, piece in enumerate(pieces(gate1) + pieces(gate2) + (i1,)):
        aux = jnp.where(lane == k, piece, aux)

    info_t = info.T
    prow = lax.broadcasted_iota(jnp.int32, (LOCAL_ROWS, TOK_TILE), 0).astype(jnp.float32)
    sel = jnp.where((prow == info_t[0:1, :]) | (prow == info_t[1:2, :]), 1.0, 0.0).astype(jnp.bfloat16)
    moved = jnp.dot(sel, jnp.concatenate([xn, aux.astype(jnp.bfloat16)], axis=1),
                    preferred_element_type=jnp.float32)
    xs_ref[...] = moved.astype(jnp.bfloat16)


def _mixer(layer, x2, qp, kv, layer_params, shared_tables, *, seq_len):
    nab, sinkcol, poolw, pools, wout, g2, rw, rb = layer_params
    swb, tri, utri = shared_tables
    T = x2.shape[0]
    nblk = seq_len // TOK_TILE
    n_tok_tiles = T // TOK_TILE

    def mixed(t):
        return jnp.minimum(t, n_tok_tiles - 1)

    def cur(t):
        return (mixed(t), 0)

    def window(t):
        seq0 = (mixed(t) // nblk) * nblk
        return (jnp.clip(mixed(t) - 1, seq0, seq0 + nblk - WIN_TILES) * TOK_TILE, 0)

    def dispatched(t):
        return (jnp.maximum(t - 1, 0), 0)

    def resident(a):
        zeros = (0,) * a.ndim
        return pl.BlockSpec(a.shape, lambda t: zeros, pipeline_mode=pl.Buffered(1))

    def resident_layer(a):
        index = (layer,) + (0,) * (a.ndim - 1)
        return pl.BlockSpec((None,) + a.shape[1:], lambda t: index, pipeline_mode=pl.Buffered(1))

    return pl.pallas_call(
        functools.partial(_mixer_kernel, seq_len=seq_len),
        grid=(n_tok_tiles + 1,),
        in_specs=[
            pl.BlockSpec((TOK_TILE, D_MODEL), cur),
            pl.BlockSpec((TOK_TILE, D_QP), cur),
            pl.BlockSpec((pl.Element(WIN_TILES * TOK_TILE), pl.Element(D_KV)), window),
            resident_layer(nab), resident(swb), resident_layer(sinkcol), resident_layer(poolw),
            resident_layer(pools), resident_layer(wout), resident_layer(g2), resident_layer(rw),
            resident_layer(rb), resident(tri), resident(utri),
        ],
        out_specs=[
            pl.BlockSpec((TOK_TILE, D_MODEL), cur),
            pl.BlockSpec((LOCAL_ROWS, D_SLOT), dispatched),
            pl.BlockSpec((TOK_TILE, LANES), dispatched),
            pl.BlockSpec((SUBLANES, LANES), dispatched),
        ],
        out_shape=[
            jax.ShapeDtypeStruct((T, D_MODEL), jnp.float32),
            jax.ShapeDtypeStruct((n_tok_tiles * LOCAL_ROWS, D_SLOT), jnp.bfloat16),
            jax.ShapeDtypeStruct((T, LANES), jnp.float32),
            jax.ShapeDtypeStruct((n_tok_tiles * SUBLANES, LANES), jnp.float32),
        ],
        scratch_shapes=[
            pltpu.VMEM((TOK_TILE + 2 * HALO, D_C), jnp.float32),
            pltpu.VMEM((TOK_TILE, D_MIX), jnp.bfloat16),
            pltpu.VMEM((TOK_TILE, D_MODEL), jnp.bfloat16),
            pltpu.VMEM((TOK_TILE, LANES), jnp.float32),
        ],
        compiler_params=pltpu.CompilerParams(
            dimension_semantics=("arbitrary",), vmem_limit_bytes=VMEM_LIMIT),
        name="mixer",
    )(x2, qp, kv, nab, swb, sinkcol, poolw, pools, wout, g2, rw, rb, tri, utri)


def _chunk_copy(src_hbm, src_chunk, dst, dst_chunk, sem):
    return pltpu.make_async_copy(
        src_hbm.at[pl.ds(pl.multiple_of(src_chunk * CHUNK, CHUNK), CHUNK)],
        dst.at[pl.ds(dst_chunk * CHUNK, CHUNK)],
        sem)


def _expert_kernel(te_ref, nsub_ref, first_ref, wslot_ref, nexte_ref, csrc_ref,
                   xs_hbm, wg_hbm, wu_hbm, wd_hbm, y_ref,
                   xbuf, wgf, wuf, wdf, wgb, wub, wdb, sem, wsem, *, layer):
    j = pl.program_id(0)
    nt = pl.num_programs(0)
    slot = j % GATHER_SLOTS

    def start_gather(tile, s, h):
        for c in range(h * SUB_CHUNKS, (h + 1) * SUB_CHUNKS):
            _chunk_copy(xs_hbm, csrc_ref[tile * TILE_CHUNKS + c], xbuf.at[s], c, sem.at[s, h]).start()

    def wait_gather(s, h):
        rows = pl.ds(h * EXP_SUB, EXP_SUB)
        pltpu.make_async_copy(xs_hbm.at[pl.ds(0, EXP_SUB)], xbuf.at[s, rows], sem.at[s, h]).wait()

    def weight_copies(expert, ws):
        return [pltpu.make_async_copy(w_hbm.at[layer, expert], wbuf.at[ws], wsem.at[k, ws])
                for k, (w_hbm, wbuf) in enumerate(((wg_hbm, wgf), (wu_hbm, wuf), (wd_hbm, wdf)))]

    for h in range(EXP_SUBS):
        for first_tile in range(GATHER_AHEAD):
            @pl.when((j == 0) & (h < nsub_ref[first_tile]))
            def _():
                start_gather(first_tile, first_tile, h)

        ahead = j + GATHER_AHEAD

        @pl.when(h < nsub_ref[jnp.minimum(ahead, nt - 1)] * (ahead < nt).astype(jnp.int32))
        def _():
            start_gather(ahead, ahead % GATHER_SLOTS, h)

    ws = wslot_ref[j]

    @pl.when(j == 0)
    def _():
        for cp in weight_copies(te_ref[0], 0):
            cp.start()

    @pl.when(first_ref[j] > 0)
    def _():
        for cp in weight_copies(te_ref[j], ws):
            cp.wait()

        @pl.when(nexte_ref[j] >= 0)
        def _():
            for cp in weight_copies(nexte_ref[j], 1 - ws):
                cp.start()

        wgb[...] = wgf[ws].astype(jnp.bfloat16)
        wub[...] = wuf[ws].astype(jnp.bfloat16)
        wdb[...] = wdf[ws].astype(jnp.bfloat16)

    def gated_mlp(n_rows):
        xs = xbuf[slot, 0:n_rows, 0:D_MODEL]
        gp = xbuf[slot, 0:n_rows, D_MODEL:D_SLOT].astype(jnp.float32)
        first = gp[:, FIRST_EXPERT_LANE:FIRST_EXPERT_LANE + 1] == (te_ref[j] + ROUTE_LANE0).astype(jnp.float32)

        def gate_of(lanes):
            return sum(gp[:, k:k + 1] for k in lanes)

        gate_w = jnp.where(first, gate_of(GATE1_LANES), gate_of(GATE2_LANES))
        gate = jnp.dot(xs, wgb[...], preferred_element_type=jnp.float32)
        up = jnp.dot(xs, wub[...], preferred_element_type=jnp.float32)
        act = (gate * (1.0 / (1.0 + jnp.exp(-gate))) * up * gate_w).astype(jnp.bfloat16)
        return jnp.dot(act, wdb[...], preferred_element_type=jnp.float32).astype(jnp.bfloat16)

    for k in range(1, EXP_SUBS + 1):
        @pl.when(nsub_ref[j] == k)
        def _():
            for h in range(k):
                wait_gather(slot, h)
            y_ref[0:k * EXP_SUB, :] = gated_mlp(k * EXP_SUB)
            if k < EXP_SUBS:
                y_ref[k * EXP_SUB:EXP_TILE, :] = jnp.zeros((EXP_TILE - k * EXP_SUB, D_MODEL), jnp.bfloat16)

    @pl.when(nsub_ref[j] == 0)
    def _():
        y_ref[...] = jnp.zeros_like(y_ref)


def _experts(layer, tile_tables, chunk_src, xs_local, wg, wu, wd):
    tile_expert, n_sub, first, wslot, next_expert = tile_tables
    n_tiles = tile_expert.shape[0]
    any_space = pl.BlockSpec(memory_space=pl.ANY)

    return pl.pallas_call(
        functools.partial(_expert_kernel, layer=layer),
        grid_spec=pltpu.PrefetchScalarGridSpec(
            num_scalar_prefetch=6,
            grid=(n_tiles,),
            in_specs=[any_space] * 4,
            out_specs=pl.BlockSpec((EXP_TILE, D_MODEL), lambda j, *tables: (j, 0)),
            scratch_shapes=[
                pltpu.VMEM((GATHER_SLOTS, EXP_TILE, D_SLOT), jnp.bfloat16),
                pltpu.VMEM((2, D_MODEL, D_EXPERT), jnp.float32),
                pltpu.VMEM((2, D_MODEL, D_EXPERT), jnp.float32),
                pltpu.VMEM((2, D_EXPERT, D_MODEL), jnp.float32),
                pltpu.VMEM((D_MODEL, D_EXPERT), jnp.bfloat16),
                pltpu.VMEM((D_MODEL, D_EXPERT), jnp.bfloat16),
                pltpu.VMEM((D_EXPERT, D_MODEL), jnp.bfloat16),
                pltpu.SemaphoreType.DMA((GATHER_SLOTS, EXP_SUBS)),
                pltpu.SemaphoreType.DMA((3, 2)),
            ],
        ),
        out_shape=jax.ShapeDtypeStruct((n_tiles * EXP_TILE, D_MODEL), jnp.bfloat16),
        compiler_params=pltpu.CompilerParams(
            dimension_semantics=("arbitrary",), vmem_limit_bytes=VMEM_LIMIT),
        name="experts",
    )(tile_expert, n_sub, first, wslot, next_expert, chunk_src, xs_local, wg, wu, wd)


def _combine_tile(ctab_ref, xmid_ref, rinfo_ref, ys_hbm, ybuf, sem):
    i = pl.program_id(0)
    nt = pl.num_programs(0)
    slot = i % GATHER_SLOTS

    def start_gather(tile, s):
        for c in range(LOCAL_CHUNKS):
            _chunk_copy(ys_hbm, ctab_ref[tile * LOCAL_CHUNKS + c], ybuf.at[s], c, sem.at[s]).start()

    def wait_gather(s):
        pltpu.make_async_copy(ys_hbm.at[pl.ds(0, LOCAL_ROWS)], ybuf.at[s], sem.at[s]).wait()

    @pl.when(i == 0)
    def _():
        for first_tile in range(GATHER_AHEAD):
            start_gather(first_tile, first_tile)

    @pl.when(i + GATHER_AHEAD < nt)
    def _():
        start_gather(i + GATHER_AHEAD, (i + GATHER_AHEAD) % GATHER_SLOTS)

    wait_gather(slot)
    info = rinfo_ref[...]
    pcol = lax.broadcasted_iota(jnp.int32, (TOK_TILE, LOCAL_ROWS), 1).astype(jnp.float32)
    pick = jnp.where((pcol == info[:, 0:1]) | (pcol == info[:, 1:2]), 1.0, 0.0).astype(jnp.bfloat16)
    return xmid_ref[...] + jnp.dot(pick, ybuf[slot], preferred_element_type=jnp.float32)


def _combine_final_kernel(ctab_ref, xmid_ref, rinfo_ref, g_ref, ys_hbm, out_ref, ybuf, sem):
    x = _combine_tile(ctab_ref, xmid_ref, rinfo_ref, ys_hbm, ybuf, sem)
    out_ref[...] = _rmsnorm_f32(x, g_ref[...])


def _combine_proj_kernel(ctab_ref, xmid_ref, rinfo_ref, g_ref, w_ref, ys_hbm, x_ref, qp_ref, kv_ref, ybuf, sem):
    x = _combine_tile(ctab_ref, xmid_ref, rinfo_ref, ys_hbm, ybuf, sem)
    x_ref[...] = x
    xn = _rmsnorm_f32(x, g_ref[...]).astype(jnp.bfloat16)
    proj = jnp.dot(xn, w_ref[...], preferred_element_type=jnp.float32)
    qp_ref[...] = proj[:, :D_QP].astype(jnp.bfloat16)
    kv_ref[...] = proj[:, D_QP:].astype(jnp.bfloat16)


def _combine_proj(next_layer, chunk_tab, xmid, rinfo, g1, w_in_k, ys):
    T = xmid.shape[0]

    def tile(i, ct):
        return (i, 0)

    return pl.pallas_call(
        _combine_proj_kernel,
        grid_spec=pltpu.PrefetchScalarGridSpec(
            num_scalar_prefetch=1,
            grid=(T // TOK_TILE,),
            in_specs=[
                pl.BlockSpec((TOK_TILE, D_MODEL), tile),
                pl.BlockSpec((TOK_TILE, LANES), tile),
                pl.BlockSpec((None, 1, D_MODEL), lambda i, ct: (next_layer, 0, 0)),
                pl.BlockSpec((None, D_MODEL, D_IN), lambda i, ct: (next_layer, 0, 0),
                             pipeline_mode=pl.Buffered(1)),
                pl.BlockSpec(memory_space=pl.ANY),
            ],
            out_specs=[
                pl.BlockSpec((TOK_TILE, D_MODEL), tile),
                pl.BlockSpec((TOK_TILE, D_QP), tile),
                pl.BlockSpec((TOK_TILE, D_KV), tile),
            ],
            scratch_shapes=[
                pltpu.VMEM((GATHER_SLOTS, LOCAL_ROWS, D_MODEL), jnp.bfloat16),
                pltpu.SemaphoreType.DMA((GATHER_SLOTS,)),
            ],
        ),
        out_shape=[
            jax.ShapeDtypeStruct((T, D_MODEL), jnp.float32),
            jax.ShapeDtypeStruct((T, D_QP), jnp.bfloat16),
            jax.ShapeDtypeStruct((T, D_KV), jnp.bfloat16),
        ],
        compiler_params=pltpu.CompilerParams(
            dimension_semantics=("arbitrary",), vmem_limit_bytes=VMEM_LIMIT),
        name="combine_proj",
    )(chunk_tab, xmid, rinfo, g1, w_in_k, ys)


def _combine_final(chunk_tab, xmid, rinfo, g, ys):
    T = xmid.shape[0]
    return pl.pallas_call(
        _combine_final_kernel,
        grid_spec=pltpu.PrefetchScalarGridSpec(
            num_scalar_prefetch=1,
            grid=(T // TOK_TILE,),
            in_specs=[
                pl.BlockSpec((TOK_TILE, D_MODEL), lambda i, ct: (i, 0)),
                pl.BlockSpec((TOK_TILE, LANES), lambda i, ct: (i, 0)),
                pl.BlockSpec((1, D_MODEL), lambda i, ct: (0, 0)),
                pl.BlockSpec(memory_space=pl.ANY),
            ],
            out_specs=pl.BlockSpec((TOK_TILE, D_MODEL), lambda i, ct: (i, 0)),
            scratch_shapes=[
                pltpu.VMEM((GATHER_SLOTS, LOCAL_ROWS, D_MODEL), jnp.bfloat16),
                pltpu.SemaphoreType.DMA((GATHER_SLOTS,)),
            ],
        ),
        out_shape=jax.ShapeDtypeStruct((T, D_MODEL), jnp.float32),
        compiler_params=pltpu.CompilerParams(
            dimension_semantics=("arbitrary",), vmem_limit_bytes=VMEM_LIMIT),
        name="combine",
    )(chunk_tab, xmid, rinfo, g, ys)


def _pair_heads(a, axis):
    shape = a.shape
    split = shape[:axis] + (SWA_KV_HEADS, SWA_REP, HEAD_DIM) + shape[axis + 1:]
    return jnp.swapaxes(a.reshape(split), axis, axis + 1).reshape(shape)


def _in_proj_weight(w):
    off_bq = 3 * D_A
    return jnp.concatenate(
        [w[..., 0:D_A], _pair_heads(w[..., off_bq:off_bq + D_B], w.ndim - 1), w[..., D_A:off_bq],
         w[..., off_bq + D_B:]], axis=-1).astype(jnp.bfloat16)


def _out_proj_weight(w):
    return jnp.concatenate(
        [w[:, 0:D_A], _pair_heads(w[:, D_A:D_A + D_B], 1), w[:, D_A + D_B:]], axis=1).astype(jnp.bfloat16)


def _na_bias_table(rel_bias):
    c = np.arange(GRID_W)[:, None]
    cp = np.arange(GRID_W)[None, :]
    cs = np.clip(c - NA_COLS // 2, 0, GRID_W - NA_COLS)
    valid = (cp >= cs) & (cp < cs + NA_COLS)
    d = np.arange(2 * NA_COLS - 1)[:, None, None]
    col_sel = ((cp - c + (NA_COLS - 1))[None] == d) & valid[None]
    k = np.arange(NA_ROWS)[:, None, None]
    j = np.arange(NA_ROWS)[None, :, None]
    r = np.arange(2 * NA_ROWS - 1)[None, None, :]
    row_sel = r == j - k + (NA_ROWS - 1)
    tab = jnp.einsum("kjr,lhrd,dcm->lkhcjm", jnp.asarray(row_sel, jnp.float32), rel_bias.astype(jnp.float32),
                     jnp.asarray(col_sel, jnp.float32), precision=lax.Precision.HIGHEST)
    tab = jnp.where(jnp.asarray(valid)[None, None, None, :, None, :], tab * LOG2E, NEG)
    return tab.reshape(rel_bias.shape[0], NA_ROWS, NA_HEADS * GRID_W, NA_ROWS * GRID_W)


def _swa_bias_table():
    slopes = (2.0 ** (-8.0 * np.arange(1, SWA_Q_HEADS + 1) / SWA_Q_HEADS)).astype(np.float32)
    qi = np.arange(SWA_BLOCK)[:, None]
    ki = np.arange(3 * SWA_BLOCK)[None, :]
    variants = []
    for v in range(3):
        dist = np.abs(ki - qi - v * SWA_BLOCK).astype(np.float32)
        tab = np.where(dist <= SWA_WINDOW, -slopes[:, None, None] * dist[None] * LOG2E, np.float32(NEG))
        variants.append(tab.reshape(SWA_Q_HEADS * SWA_BLOCK, 3 * SWA_BLOCK).astype(np.float32))
    return jnp.asarray(np.stack(variants))


def _block_diag(pool_w):
    depth, n = pool_w.shape[0:2]
    eye = jnp.asarray(np.eye(n, dtype=np.float32))
    out = pool_w[:, :, :, None, :] * eye[None, :, None, :, None]
    return out.reshape(depth, n * POOL_GROUP_DIM, n * POOL_GROUP_DIM)


def _router_weights(rg_w, rg_b, re_w, re_b):
    def lanes(g, e):
        gap = jnp.zeros(g.shape[:-1] + (ROUTE_LANE0 - N_GROUPS,), jnp.float32)
        tail = jnp.zeros(g.shape[:-1] + (LANES - ROUTE_LANE0 - N_EXPERTS,), jnp.float32)
        return jnp.concatenate([g.astype(jnp.float32), gap, e.astype(jnp.float32), tail], axis=-1)

    return lanes(rg_w, re_w).astype(jnp.bfloat16), lanes(rg_b, re_b)[:, None, :]


def _dispatch_tables(cnt, n_tiles):
    nb = cnt.shape[0] // SUBLANES
    n = cnt.reshape(nb, SUBLANES, LANES)[:, 0, ROUTE_LANE0:ROUTE_LANE0 + N_EXPERTS].astype(jnp.int32)
    g = (n + (CHUNK - 1)) // CHUNK
    l_end = jnp.cumsum(g, axis=1)
    l_off = l_end - g
    c_end = jnp.cumsum(g, axis=0)
    c_off = c_end - g
    tot = c_end[-1]
    tiles = (tot + (TILE_CHUNKS - 1)) // TILE_CHUNKS
    t_end = jnp.cumsum(tiles)
    t_off = t_end - tiles
    n_used = t_end[-1:]

    experts = jnp.arange(N_EXPERTS, dtype=jnp.int32)
    tile_ids = jnp.arange(n_tiles, dtype=jnp.int32)
    tile_expert = jnp.minimum(jnp.sum((t_end[None, :] <= tile_ids[:, None]).astype(jnp.int32), axis=1),
                              N_EXPERTS - 1)
    oh_te = (tile_expert[:, None] == experts[None, :]).astype(jnp.int32)
    left = jnp.sum(oh_te * (tot + t_off * TILE_CHUNKS)[None, :], axis=1) - tile_ids * TILE_CHUNKS
    n_sub = jnp.clip((left + (SUB_CHUNKS - 1)) // SUB_CHUNKS, 0, EXP_SUBS)
    has_rows = tiles > 0
    first = ((tile_ids == jnp.sum(oh_te * t_off[None, :], axis=1)) & (n_sub > 0)).astype(jnp.int32)
    wslot = jnp.sum(oh_te * ((jnp.cumsum(has_rows.astype(jnp.int32)) - 1) % 2)[None, :], axis=1)
    later = (experts[None, :] > experts[:, None]) & has_rows[None, :]
    nxt = jnp.min(jnp.where(later, experts[None, :], N_EXPERTS), axis=1)
    next_expert = jnp.sum(oh_te * jnp.where(nxt < N_EXPERTS, nxt, -1)[None, :], axis=1)
    tile_tables = (tile_expert, n_sub, first, wslot, next_expert)

    q = jnp.arange(n_tiles * TILE_CHUNKS, dtype=jnp.int32)
    tile_q = q // TILE_CHUNKS
    oh_e = (jnp.repeat(tile_expert, TILE_CHUNKS)[:, None] == experts[None, :]).astype(jnp.int32)
    ro = q - jnp.sum(oh_e * t_off[None, :], axis=1) * TILE_CHUNKS
    valid = (ro < jnp.sum(oh_e * tot[None, :], axis=1)) & (tile_q < n_used[0])
    cols = jnp.dot(jnp.concatenate([c_end, c_off, l_off], axis=0).astype(jnp.float32),
                   oh_e.T.astype(jnp.float32), precision=lax.Precision.HIGHEST).astype(jnp.int32)
    c_end_q, c_off_q, l_off_q = cols[0:nb], cols[nb:2 * nb], cols[2 * nb:3 * nb]
    b_q = jnp.minimum(jnp.sum((c_end_q <= ro[None, :]).astype(jnp.int32), axis=0), nb - 1)
    oh_b = (jnp.arange(nb, dtype=jnp.int32)[:, None] == b_q[None, :]).astype(jnp.int32)
    src = b_q * LOCAL_CHUNKS + jnp.sum(oh_b * (l_off_q + ro[None, :] - c_off_q), axis=0)
    chunk_src = jnp.where(valid, src, LOCAL_CHUNKS - 1)

    c = jnp.arange(LOCAL_CHUNKS, dtype=jnp.int32)
    e_c = jnp.minimum(jnp.sum((l_end[:, None, :] <= c[None, :, None]).astype(jnp.int32), axis=2),
                      N_EXPERTS - 1)
    oh_ec = (e_c[:, :, None] == experts[None, None, :]).astype(jnp.int32)
    pos = (jnp.sum(oh_ec * (t_off[None, None, :] * TILE_CHUNKS + c_off[:, None, :] - l_off[:, None, :]), axis=2)
           + c[None, :])
    chunk_tab = jnp.where(c[None, :] < l_end[:, -1:], pos, 0).reshape(-1)
    return tile_tables, chunk_src, chunk_tab


def kernel(x, norm1_g, w_in, nat_bias, swa_sink, pool_w, pool_scale, w_out, norm2_g, router_g_w,
           router_g_b, router_e_w, router_e_b, expert_w_gate, expert_w_up, expert_w_down, final_g):
    batch, seq_len, _ = x.shape
    depth = w_in.shape[0]
    T = batch * seq_len
    assert seq_len % TOK_TILE == 0 and TOK_TILE % SWA_BLOCK == 0 and TOK_TILE % GRID_W == 0
    max_chunks = (2 * T) // CHUNK + (T // TOK_TILE) * N_EXPERTS
    n_tiles = max_chunks // TILE_CHUNKS + N_EXPERTS

    swb = _swa_bias_table()
    tri = jnp.asarray(np.tril(np.ones((TOK_TILE, TOK_TILE), np.float32), -1)).astype(jnp.bfloat16)
    utri = jnp.asarray(np.triu(np.ones((LANES, LANES), np.float32), 1)).astype(jnp.bfloat16)

    w_in_k = _in_proj_weight(w_in)
    w_out_k = _out_proj_weight(w_out)
    nab = _na_bias_table(nat_bias)
    sinkcol = jnp.broadcast_to((swa_sink.astype(jnp.float32) * LOG2E)[:, :, None, None],
                               (depth, SWA_Q_HEADS, SWA_BLOCK, LANES)).reshape(depth, SWA_Q_HEADS * SWA_BLOCK, LANES)
    poolw = _block_diag(pool_w).astype(jnp.bfloat16)
    pools = pool_scale.reshape(depth, 1, D_C).astype(jnp.float32)
    rw, rb = _router_weights(router_g_w, router_g_b, router_e_w, router_e_b)
    g1 = norm1_g.reshape(depth, 1, D_MODEL)
    g2 = norm2_g.reshape(depth, 1, D_MODEL)

    x2 = x.reshape(T, D_MODEL)
    qp, kv = _norm_proj(0, x2, g1, w_in_k)
    for l in range(depth):
        xmid, xs_local, rinfo, cnt = _mixer(
            l, x2, qp, kv, (nab, sinkcol, poolw, pools, w_out_k, g2, rw, rb), (swb, tri, utri),
            seq_len=seq_len)
        tile_tables, chunk_src, chunk_tab = _dispatch_tables(cnt, n_tiles)
        ys = _experts(l, tile_tables, chunk_src, xs_local,
                      expert_w_gate, expert_w_up, expert_w_down)
        if l + 1 < depth:
            x2, qp, kv = _combine_proj(l + 1, chunk_tab, xmid, rinfo, g1, w_in_k, ys)
        else:
            x2 = _combine_final(chunk_tab, xmid, rinfo, final_g.reshape(1, D_MODEL), ys)
    return x2.reshape(batch, seq_len, D_MODEL)
```

```python
import functools

import jax
import jax.numpy as jnp
import numpy as np
from jax import lax
from jax.experimental import pallas as pl
from jax.experimental.pallas import tpu as pltpu

D_MODEL = 1024
GRID_W = 64
HEAD_DIM = 64
NA_HEADS = 4
NA_ROWS = 8
NA_COLS = 16
SWA_Q_HEADS = 8
SWA_KV_HEADS = 2
SWA_REP = SWA_Q_HEADS // SWA_KV_HEADS
SWA_WINDOW = 128
SWA_BLOCK = 128
POOL_WINDOWS = (2, 4, 8, 16)
POOL_GROUP_DIM = 64
D_A = NA_HEADS * HEAD_DIM
D_B = SWA_Q_HEADS * HEAD_DIM
D_BKV = SWA_KV_HEADS * HEAD_DIM
D_C = len(POOL_WINDOWS) * POOL_GROUP_DIM
D_MIX = D_A + D_B + D_C
D_QP = D_A + D_B
D_KV = 2 * D_A + 2 * D_BKV + D_C
D_IN = D_QP + D_KV
N_GROUPS = 4
EXPERTS_PER_GROUP = 8
N_EXPERTS = N_GROUPS * EXPERTS_PER_GROUP
D_EXPERT = 256
RMS_EPS = 1e-6
NEG = -1e30
LOG2E = 1.4426950408889634
QK_SCALE = HEAD_DIM ** -0.5 * LOG2E

LANES = 128
SUBLANES = 8
ROW_CHUNKS = D_MODEL // LANES

TOK_TILE = 512
EXP_SUB = 256
EXP_SUBS = 4
EXP_TILE = EXP_SUB * EXP_SUBS
CHUNK = 16
LOCAL_CHUNKS = (2 * TOK_TILE + N_EXPERTS * (CHUNK - 1)) // CHUNK + 2
LOCAL_ROWS = LOCAL_CHUNKS * CHUNK
SUB_CHUNKS = EXP_SUB // CHUNK
TILE_CHUNKS = EXP_TILE // CHUNK
GATHER_AHEAD = 3
GATHER_SLOTS = GATHER_AHEAD + 1
ROUTE_LANE0 = 8
D_SLOT = D_MODEL + LANES
GATE1_LANES = (0, 1, 2)
GATE2_LANES = (3, 4, 5)
FIRST_EXPERT_LANE = 6
NA_ROWS_PER_STEP = 8
SWA_BLOCKS_PER_STEP = 4
HALO = 8
WIN_TILES = 3
VMEM_LIMIT = 56 * 1024 * 1024


def _rmsnorm_f32(x, g):
    return x * lax.rsqrt(jnp.mean(x * x, axis=-1, keepdims=True) + RMS_EPS) * g


def _norm_proj_kernel(x_ref, g_ref, w_ref, qp_ref, kv_ref):
    xn = _rmsnorm_f32(x_ref[...], g_ref[...]).astype(jnp.bfloat16)
    proj = jnp.dot(xn, w_ref[...], preferred_element_type=jnp.float32)
    qp_ref[...] = proj[:, :D_QP].astype(jnp.bfloat16)
    kv_ref[...] = proj[:, D_QP:].astype(jnp.bfloat16)


def _norm_proj(layer, x2, g, w):
    T = x2.shape[0]
    return pl.pallas_call(
        _norm_proj_kernel,
        grid=(T // TOK_TILE,),
        in_specs=[
            pl.BlockSpec((TOK_TILE, D_MODEL), lambda i: (i, 0)),
            pl.BlockSpec((None, 1, D_MODEL), lambda i: (layer, 0, 0)),
            pl.BlockSpec((None, D_MODEL, D_IN), lambda i: (layer, 0, 0)),
        ],
        out_specs=[
            pl.BlockSpec((TOK_TILE, D_QP), lambda i: (i, 0)),
            pl.BlockSpec((TOK_TILE, D_KV), lambda i: (i, 0)),
        ],
        out_shape=[
            jax.ShapeDtypeStruct((T, D_QP), jnp.bfloat16),
            jax.ShapeDtypeStruct((T, D_KV), jnp.bfloat16),
        ],
        compiler_params=pltpu.CompilerParams(
            dimension_semantics=("arbitrary",), vmem_limit_bytes=VMEM_LIMIT),
        name="norm_proj",
    )(x2, g, w)


KW_AK, KW_AV, KW_BK, KW_BV = 0, D_A, 2 * D_A, 2 * D_A + D_BKV
KW_COLS = 2 * D_A + 2 * D_BKV
KV_CU = KW_COLS


def _mixer_kernel(x_ref, qp_ref, kwin_ref, nab_ref, swb_ref, sink_ref,
                  poolw_ref, pools_ref, wout_ref, g2_ref, rw_ref, rb_ref, tri_ref, utri_ref,
                  xmid_ref, xs_ref, rinfo_ref, cnt_ref,
                  uwin, mix, xn_scr, logit_scr, *, seq_len):
    t = pl.program_id(0)
    nblk = seq_len // TOK_TILE
    i = jnp.minimum(t, pl.num_programs(0) - 2) % nblk
    rows_per_tile = TOK_TILE // GRID_W
    grid_rows = seq_len // GRID_W

    @pl.when(t == 0)
    def _():
        xn_scr[...] = jnp.zeros_like(xn_scr)
        logit_scr[...] = jnp.zeros_like(logit_scr)

    cur_off = pl.multiple_of((i - jnp.clip(i - 1, 0, nblk - WIN_TILES)) * TOK_TILE, TOK_TILE)

    lane_a = lax.broadcasted_iota(jnp.int32, (GRID_W, D_A), 1) // HEAD_DIM

    def na_row(rr):
        r = i * rows_per_tile + rr
        rs = jnp.clip(r - NA_ROWS // 2, 0, grid_rows - NA_ROWS)
        variant = r - rs
        start = pl.multiple_of(cur_off + (rs - i * rows_per_tile) * GRID_W, GRID_W)
        q0 = pl.multiple_of(rr * GRID_W, GRID_W)
        q = (qp_ref[pl.ds(q0, GRID_W), 0:D_A].astype(jnp.float32) * QK_SCALE).astype(jnp.bfloat16)
        zero = jnp.zeros_like(q)
        qs = jnp.concatenate([jnp.where(lane_a == h, q, zero) for h in range(NA_HEADS)], axis=0)
        kw = kwin_ref[pl.ds(start, NA_ROWS * GRID_W), KW_AK:KW_AK + D_A]
        vw = kwin_ref[pl.ds(start, NA_ROWS * GRID_W), KW_AV:KW_AV + D_A]
        s = lax.dot_general(qs, kw, (((1,), (1,)), ((), ())), preferred_element_type=jnp.float32)
        s = s + nab_ref[variant]
        m = jnp.max(s, axis=-1, keepdims=True)
        p = jnp.exp2(s - m)
        l = jnp.sum(p, axis=-1, keepdims=True)
        pv = jnp.dot(p.astype(jnp.bfloat16), vw, preferred_element_type=jnp.float32)
        pv = pv * (1.0 / l)
        o = jnp.zeros((GRID_W, D_A), jnp.float32)
        for h in range(NA_HEADS):
            o = o + jnp.where(lane_a == h, pv[h * GRID_W:(h + 1) * GRID_W, :], 0.0)
        mix[pl.ds(q0, GRID_W), 0:D_A] = o.astype(jnp.bfloat16)

    def na_step(it, c):
        for k in range(NA_ROWS_PER_STEP):
            na_row(it * NA_ROWS_PER_STEP + k)
        return c

    lax.fori_loop(0, rows_per_tile // NA_ROWS_PER_STEP, na_step, 0)

    lane_b = lax.broadcasted_iota(jnp.int32, (SWA_BLOCK, LANES), 1) // HEAD_DIM
    blocks_per_tile = TOK_TILE // SWA_BLOCK
    nblocks = seq_len // SWA_BLOCK

    ones_v = jnp.ones((3 * SWA_BLOCK, LANES), jnp.bfloat16)

    def swa_step(sb, c):
        n = i * blocks_per_tile + sb
        variant = jnp.where(n == 0, 0, jnp.where(n == nblocks - 1, 2, 1))
        q0 = pl.multiple_of(sb * SWA_BLOCK, SWA_BLOCK)
        k0 = pl.multiple_of(cur_off + (sb - variant) * SWA_BLOCK, SWA_BLOCK)
        kw = kwin_ref[pl.ds(k0, 3 * SWA_BLOCK), KW_BK:KW_BK + D_BKV]
        vaug = jnp.concatenate([kwin_ref[pl.ds(k0, 3 * SWA_BLOCK), KW_BV:KW_BV + D_BKV], ones_v], axis=1)
        outs = []
        for g in range(SWA_KV_HEADS):
            pieces = []
            for t in range(SWA_REP):
                qt = qp_ref[pl.ds(q0, SWA_BLOCK), D_A + t * LANES:D_A + (t + 1) * LANES]
                qt = (qt.astype(jnp.float32) * QK_SCALE).astype(jnp.bfloat16)
                pieces.append(jnp.where(lane_b == g, qt, jnp.zeros_like(qt)))
            qs = jnp.concatenate(pieces, axis=0)
            r0 = g * SWA_REP * SWA_BLOCK
            s = lax.dot_general(qs, kw, (((1,), (1,)), ((), ())), preferred_element_type=jnp.float32)
            s = s + swb_ref[variant, r0:r0 + SWA_REP * SWA_BLOCK, :]
            sink = sink_ref[r0:r0 + SWA_REP * SWA_BLOCK, :]
            m = jnp.broadcast_to(jnp.max(s, axis=-1, keepdims=True), sink.shape)
            m = jnp.maximum(m, sink)
            p = jnp.exp2(s - jnp.concatenate([m, m, m], axis=1)).astype(jnp.bfloat16)
            pv = jnp.dot(p, vaug, preferred_element_type=jnp.float32)
            l = pv[:, LANES:2 * LANES] + jnp.exp2(sink - m)
            outs.append(pv[:, 0:LANES] * (1.0 / l))
        for t in range(SWA_REP):
            o0 = outs[0][t * SWA_BLOCK:(t + 1) * SWA_BLOCK, :]
            o1 = outs[1][t * SWA_BLOCK:(t + 1) * SWA_BLOCK, :]
            ot = jnp.where(lane_b == 0, o0, o1)
            mix[pl.ds(q0, SWA_BLOCK), D_A + t * LANES:D_A + (t + 1) * LANES] = ot.astype(jnp.bfloat16)
        return c

    def swa_pair(it, c):
        for k in range(SWA_BLOCKS_PER_STEP):
            swa_step(it * SWA_BLOCKS_PER_STEP + k, c)
        return c

    lax.fori_loop(0, blocks_per_tile // SWA_BLOCKS_PER_STEP, swa_pair, 0)

    u = kwin_ref[pl.ds(cur_off, TOK_TILE), KV_CU:KV_CU + D_C].astype(jnp.float32)
    prev_ok = (i > 0).astype(jnp.float32)
    next_ok = (i < nblk - 1).astype(jnp.float32)
    before = pl.multiple_of(jnp.maximum(cur_off - CHUNK, 0), CHUNK)
    after = pl.multiple_of(jnp.minimum(cur_off + TOK_TILE, WIN_TILES * TOK_TILE - CHUNK), CHUNK)
    u_before = kwin_ref[pl.ds(before, CHUNK), KV_CU:KV_CU + D_C].astype(jnp.float32)
    u_after = kwin_ref[pl.ds(after, CHUNK), KV_CU:KV_CU + D_C].astype(jnp.float32)
    uwin[0:HALO, :] = u_before[CHUNK - HALO:CHUNK, :] * prev_ok
    uwin[HALO:HALO + TOK_TILE, :] = u
    uwin[HALO + TOK_TILE:2 * HALO + TOK_TILE, :] = u_after[0:HALO, :] * next_ok
    n_ext = TOK_TILE + 2 * HALO
    a2 = uwin[0:n_ext - 1, :] + uwin[1:n_ext, :]
    a4 = a2[0:n_ext - 3, :] + a2[2:n_ext - 1, :]
    a8 = a4[0:n_ext - 7, :] + a4[4:n_ext - 3, :]
    a16 = a8[0:n_ext - 15, :] + a8[8:n_ext - 7, :]
    w2 = a2[7:7 + TOK_TILE, :]
    w4 = a4[6:6 + TOK_TILE, :]
    w8 = a8[4:4 + TOK_TILE, :]
    w16 = a16[0:TOK_TILE, :]
    lane_c = lax.broadcasted_iota(jnp.int32, (TOK_TILE, D_C), 1) // POOL_GROUP_DIM
    pooled = jnp.where(lane_c == 0, w2, jnp.where(lane_c == 1, w4, jnp.where(lane_c == 2, w8, w16)))
    half = jnp.where(lane_c == 0, 1, jnp.where(lane_c == 1, 2, jnp.where(lane_c == 2, 4, 8)))
    pos = i * TOK_TILE + lax.broadcasted_iota(jnp.int32, (TOK_TILE, D_C), 0)
    cnt = (jnp.minimum(pos + half, seq_len) - jnp.maximum(pos - half, 0)).astype(jnp.float32)
    d = (pooled / cnt - u).astype(jnp.bfloat16)
    oc = jnp.dot(d, poolw_ref[...], preferred_element_type=jnp.float32) * pools_ref[...]
    mix[:, D_A + D_B:D_MIX] = oc.astype(jnp.bfloat16)

    xn = xn_scr[...]
    logits = logit_scr[...]

    xm = x_ref[...] + jnp.dot(mix[...], wout_ref[...], preferred_element_type=jnp.float32)
    xmid_ref[...] = xm
    xn_new = _rmsnorm_f32(xm, g2_ref[...]).astype(jnp.bfloat16)
    xn_scr[...] = xn_new
    logit_scr[...] = jnp.dot(xn_new, rw_ref[...], preferred_element_type=jnp.float32) + rb_ref[...]

    lane = lax.broadcasted_iota(jnp.int32, (TOK_TILE, LANES), 1).astype(jnp.float32)
    is_g = lane < N_GROUPS
    gl = jnp.where(is_g, logits, NEG)
    gmax = jnp.max(gl, axis=-1, keepdims=True)
    gtop = jnp.min(jnp.where(is_g & (gl == gmax), lane, float(LANES)), axis=-1, keepdims=True)
    gprob = 1.0 / jnp.sum(jnp.exp(gl - gmax), axis=-1, keepdims=True)
    e_lo = ROUTE_LANE0 + gtop * EXPERTS_PER_GROUP
    in_grp = (lane >= e_lo) & (lane < e_lo + EXPERTS_PER_GROUP)
    el = jnp.where(in_grp, logits, NEG)
    m1 = jnp.max(el, axis=-1, keepdims=True)
    i1 = jnp.min(jnp.where(in_grp & (el == m1), lane, float(LANES)), axis=-1, keepdims=True)
    el2 = jnp.where(lane == i1, NEG, el)
    m2 = jnp.max(el2, axis=-1, keepdims=True)
    i2 = jnp.min(jnp.where(in_grp & (lane != i1) & (el2 == m2), lane, float(LANES)), axis=-1, keepdims=True)
    r21 = jnp.exp(m2 - m1)
    gate1 = gprob / (1.0 + r21)
    gate2 = gprob * r21 / (1.0 + r21)

    oh1 = lane == i1
    oh2 = lane == i2
    oh = jnp.where(oh1 | oh2, 1.0, 0.0)
    earlier = jnp.dot(tri_ref[...], oh.astype(jnp.bfloat16), preferred_element_type=jnp.float32)
    n_e = jnp.sum(oh, axis=0, keepdims=True)
    chunks_e = jnp.floor((n_e + (CHUNK - 1)) * (1.0 / CHUNK))
    seg0 = jnp.dot(jnp.broadcast_to(chunks_e, (SUBLANES, LANES)).astype(jnp.bfloat16), utri_ref[...],
                   preferred_element_type=jnp.float32)[0:1, :] * CHUNK
    base = earlier + seg0
    lp1 = jnp.sum(jnp.where(oh1, base, 0.0), axis=-1, keepdims=True)
    lp2 = jnp.sum(jnp.where(oh2, base, 0.0), axis=-1, keepdims=True)
    info = jnp.where(lane == 0, lp1, jnp.where(lane == 1, lp2, 0.0))
    rinfo_ref[...] = info
    cnt_ref[...] = jnp.broadcast_to(n_e, cnt_ref.shape)

    def pieces(g):
        hi = g.astype(jnp.bfloat16).astype(jnp.float32)
        mid = (g - hi).astype(jnp.bfloat16).astype(jnp.float32)
        return hi, mid, g - hi - mid

    aux = jnp.zeros((TOK_TILE, LANES), jnp.float32)
    for k, piece in enumerate(pieces(gate1) + pieces(gate2) + (i1,)):
        aux = jnp.where(lane == k, piece, aux)

    info_t = info.T
    prow = lax.broadcasted_iota(jnp.int32, (LOCAL_ROWS, TOK_TILE), 0).astype(jnp.float32)
    sel = jnp.where((prow == info_t[0:1, :]) | (prow == info_t[1:2, :]), 1.0, 0.0).astype(jnp.bfloat16)
    moved = jnp.dot(sel, jnp.concatenate([xn, aux.astype(jnp.bfloat16)], axis=1),
                    preferred_element_type=jnp.float32)
    xs_ref[...] = moved.astype(jnp.bfloat16)


def _mixer(layer, x2, qp, kv, layer_params, shared_tables, *, seq_len):
    nab, sinkcol, poolw, pools, wout, g2, rw, rb = layer_params
    swb, tri, utri = shared_tables
    T = x2.shape[0]
    nblk = seq_len // TOK_TILE
    n_tok_tiles = T // TOK_TILE

    def mixed(t):
        return jnp.minimum(t, n_tok_tiles - 1)

    def cur(t):
        return (mixed(t), 0)

    def window(t):
        seq0 = (mixed(t) // nblk) * nblk
        return (jnp.clip(mixed(t) - 1, seq0, seq0 + nblk - WIN_TILES) * TOK_TILE, 0)

    def dispatched(t):
        return (jnp.maximum(t - 1, 0), 0)

    def resident(a):
        zeros = (0,) * a.ndim
        return pl.BlockSpec(a.shape, lambda t: zeros, pipeline_mode=pl.Buffered(1))

    def resident_layer(a):
        index = (layer,) + (0,) * (a.ndim - 1)
        return pl.BlockSpec((None,) + a.shape[1:], lambda t: index, pipeline_mode=pl.Buffered(1))

    return pl.pallas_call(
        functools.partial(_mixer_kernel, seq_len=seq_len),
        grid=(n_tok_tiles + 1,),
        in_specs=[
            pl.BlockSpec((TOK_TILE, D_MODEL), cur),
            pl.BlockSpec((TOK_TILE, D_QP), cur),
            pl.BlockSpec((pl.Element(WIN_TILES * TOK_TILE), pl.Element(D_KV)), window),
            resident_layer(nab), resident(swb), resident_layer(sinkcol), resident_layer(poolw),
            resident_layer(pools), resident_layer(wout), resident_layer(g2), resident_layer(rw),
            resident_layer(rb), resident(tri), resident(utri),
        ],
        out_specs=[
            pl.BlockSpec((TOK_TILE, D_MODEL), cur),
            pl.BlockSpec((LOCAL_ROWS, D_SLOT), dispatched),
            pl.BlockSpec((TOK_TILE, LANES), dispatched),
            pl.BlockSpec((SUBLANES, LANES), dispatched),
        ],
        out_shape=[
            jax.ShapeDtypeStruct((T, D_MODEL), jnp.float32),
            jax.ShapeDtypeStruct((n_tok_tiles * LOCAL_ROWS, D_SLOT), jnp.bfloat16),
            jax.ShapeDtypeStruct((T, LANES), jnp.float32),
            jax.ShapeDtypeStruct((n_tok_tiles * SUBLANES, LANES), jnp.float32),
        ],
        scratch_shapes=[
            pltpu.VMEM((TOK_TILE + 2 * HALO, D_C), jnp.float32),
            pltpu.VMEM((TOK_TILE, D_MIX), jnp.bfloat16),
            pltpu.VMEM((TOK_TILE, D_MODEL), jnp.bfloat16),
            pltpu.VMEM((TOK_TILE, LANES), jnp.float32),
        ],
        compiler_params=pltpu.CompilerParams(
            dimension_semantics=("arbitrary",), vmem_limit_bytes=VMEM_LIMIT),
        name="mixer",
    )(x2, qp, kv, nab, swb, sinkcol, poolw, pools, wout, g2, rw, rb, tri, utri)


def _chunk_copy(src_hbm, src_chunk, dst, dst_chunk, sem):
    return pltpu.make_async_copy(
        src_hbm.at[pl.ds(pl.multiple_of(src_chunk * CHUNK, CHUNK), CHUNK)],
        dst.at[pl.ds(dst_chunk * CHUNK, CHUNK)],
        sem)


def _expert_kernel(te_ref, nsub_ref, first_ref, wslot_ref, nexte_ref, csrc_ref,
                   xs_hbm, wg_hbm, wu_hbm, wd_hbm, y_ref,
                   xbuf, wgf, wuf, wdf, wgb, wub, wdb, sem, wsem, *, layer):
    j = pl.program_id(0)
    nt = pl.num_programs(0)
    slot = j % GATHER_SLOTS

    def start_gather(tile, s, h):
        for c in range(h * SUB_CHUNKS, (h + 1) * SUB_CHUNKS):
            _chunk_copy(xs_hbm, csrc_ref[tile * TILE_CHUNKS + c], xbuf.at[s], c, sem.at[s, h]).start()

    def wait_gather(s, h):
        rows = pl.ds(h * EXP_SUB, EXP_SUB)
        pltpu.make_async_copy(xs_hbm.at[pl.ds(0, EXP_SUB)], xbuf.at[s, rows], sem.at[s, h]).wait()

    def weight_copies(expert, ws):
        return [pltpu.make_async_copy(w_hbm.at[layer, expert], wbuf.at[ws], wsem.at[k, ws])
                for k, (w_hbm, wbuf) in enumerate(((wg_hbm, wgf), (wu_hbm, wuf), (wd_hbm, wdf)))]

    for h in range(EXP_SUBS):
        for first_tile in range(GATHER_AHEAD):
            @pl.when((j == 0) & (h < nsub_ref[first_tile]))
            def _():
                start_gather(first_tile, first_tile, h)

        ahead = j + GATHER_AHEAD

        @pl.when(h < nsub_ref[jnp.minimum(ahead, nt - 1)] * (ahead < nt).astype(jnp.int32))
        def _():
            start_gather(ahead, ahead % GATHER_SLOTS, h)

    ws = wslot_ref[j]

    @pl.when(j == 0)
    def _():
        for cp in weight_copies(te_ref[0], 0):
            cp.start()

    @pl.when(first_ref[j] > 0)
    def _():
        for cp in weight_copies(te_ref[j], ws):
            cp.wait()

        @pl.when(nexte_ref[j] >= 0)
        def _():
            for cp in weight_copies(nexte_ref[j], 1 - ws):
                cp.start()

        wgb[...] = wgf[ws].astype(jnp.bfloat16)
        wub[...] = wuf[ws].astype(jnp.bfloat16)
        wdb[...] = wdf[ws].astype(jnp.bfloat16)

    def gated_mlp(n_rows):
        xs = xbuf[slot, 0:n_rows, 0:D_MODEL]
        gp = xbuf[slot, 0:n_rows, D_MODEL:D_SLOT].astype(jnp.float32)
        first = gp[:, FIRST_EXPERT_LANE:FIRST_EXPERT_LANE + 1] == (te_ref[j] + ROUTE_LANE0).astype(jnp.float32)

        def gate_of(lanes):
            return sum(gp[:, k:k + 1] for k in lanes)

        gate_w = jnp.where(first, gate_of(GATE1_LANES), gate_of(GATE2_LANES))
        gate = jnp.dot(xs, wgb[...], preferred_element_type=jnp.float32)
        up = jnp.dot(xs, wub[...], preferred_element_type=jnp.float32)
        act = (gate * (1.0 / (1.0 + jnp.exp(-gate))) * up * gate_w).astype(jnp.bfloat16)
        return jnp.dot(act, wdb[...], preferred_element_type=jnp.float32).astype(jnp.bfloat16)

    for k in range(1, EXP_SUBS + 1):
        @pl.when(nsub_ref[j] == k)
        def _():
            for h in range(k):
                wait_gather(slot, h)
            y_ref[0:k * EXP_SUB, :] = gated_mlp(k * EXP_SUB)
            if k < EXP_SUBS:
                y_ref[k * EXP_SUB:EXP_TILE, :] = jnp.zeros((EXP_TILE - k * EXP_SUB, D_MODEL), jnp.bfloat16)

    @pl.when(nsub_ref[j] == 0)
    def _():
        y_ref[...] = jnp.zeros_like(y_ref)


def _experts(layer, tile_tables, chunk_src, xs_local, wg, wu, wd):
    tile_expert, n_sub, first, wslot, next_expert = tile_tables
    n_tiles = tile_expert.shape[0]
    any_space = pl.BlockSpec(memory_space=pl.ANY)

    return pl.pallas_call(
        functools.partial(_expert_kernel, layer=layer),
        grid_spec=pltpu.PrefetchScalarGridSpec(
            num_scalar_prefetch=6,
            grid=(n_tiles,),
            in_specs=[any_space] * 4,
            out_specs=pl.BlockSpec((EXP_TILE, D_MODEL), lambda j, *tables: (j, 0)),
            scratch_shapes=[
                pltpu.VMEM((GATHER_SLOTS, EXP_TILE, D_SLOT), jnp.bfloat16),
                pltpu.VMEM((2, D_MODEL, D_EXPERT), jnp.float32),
                pltpu.VMEM((2, D_MODEL, D_EXPERT), jnp.float32),
                pltpu.VMEM((2, D_EXPERT, D_MODEL), jnp.float32),
                pltpu.VMEM((D_MODEL, D_EXPERT), jnp.bfloat16),
                pltpu.VMEM((D_MODEL, D_EXPERT), jnp.bfloat16),
                pltpu.VMEM((D_EXPERT, D_MODEL), jnp.bfloat16),
                pltpu.SemaphoreType.DMA((GATHER_SLOTS, EXP_SUBS)),
                pltpu.SemaphoreType.DMA((3, 2)),
            ],
        ),
        out_shape=jax.ShapeDtypeStruct((n_tiles * EXP_TILE, D_MODEL), jnp.bfloat16),
        compiler_params=pltpu.CompilerParams(
            dimension_semantics=("arbitrary",), vmem_limit_bytes=VMEM_LIMIT),
        name="experts",
    )(tile_expert, n_sub, first, wslot, next_expert, chunk_src, xs_local, wg, wu, wd)


def _combine_tile(ctab_ref, xmid_ref, rinfo_ref, ys_hbm, ybuf, sem):
    i = pl.program_id(0)
    nt = pl.num_programs(0)
    slot = i % GATHER_SLOTS

    def start_gather(tile, s):
        for c in range(LOCAL_CHUNKS):
            _chunk_copy(ys_hbm, ctab_ref[tile * LOCAL_CHUNKS + c], ybuf.at[s], c, sem.at[s]).start()

    def wait_gather(s):
        pltpu.make_async_copy(ys_hbm.at[pl.ds(0, LOCAL_ROWS)], ybuf.at[s], sem.at[s]).wait()

    @pl.when(i == 0)
    def _():
        for first_tile in range(GATHER_AHEAD):
            start_gather(first_tile, first_tile)

    @pl.when(i + GATHER_AHEAD < nt)
    def _():
        start_gather(i + GATHER_AHEAD, (i + GATHER_AHEAD) % GATHER_SLOTS)

    wait_gather(slot)
    info = rinfo_ref[...]
    pcol = lax.broadcasted_iota(jnp.int32, (TOK_TILE, LOCAL_ROWS), 1).astype(jnp.float32)
    pick = jnp.where((pcol == info[:, 0:1]) | (pcol == info[:, 1:2]), 1.0, 0.0).astype(jnp.bfloat16)
    return xmid_ref[...] + jnp.dot(pick, ybuf[slot], preferred_element_type=jnp.float32)


def _combine_final_kernel(ctab_ref, xmid_ref, rinfo_ref, g_ref, ys_hbm, out_ref, ybuf, sem):
    x = _combine_tile(ctab_ref, xmid_ref, rinfo_ref, ys_hbm, ybuf, sem)
    out_ref[...] = _rmsnorm_f32(x, g_ref[...])


def _combine_proj_kernel(ctab_ref, xmid_ref, rinfo_ref, g_ref, w_ref, ys_hbm, x_ref, qp_ref, kv_ref, ybuf, sem):
    x = _combine_tile(ctab_ref, xmid_ref, rinfo_ref, ys_hbm, ybuf, sem)
    x_ref[...] = x
    xn = _rmsnorm_f32(x, g_ref[...]).astype(jnp.bfloat16)
    proj = jnp.dot(xn, w_ref[...], preferred_element_type=jnp.float32)
    qp_ref[...] = proj[:, :D_QP].astype(jnp.bfloat16)
    kv_ref[...] = proj[:, D_QP:].astype(jnp.bfloat16)


def _combine_proj(next_layer, chunk_tab, xmid, rinfo, g1, w_in_k, ys):
    T = xmid.shape[0]

    def tile(i, ct):
        return (i, 0)

    return pl.pallas_call(
        _combine_proj_kernel,
        grid_spec=pltpu.PrefetchScalarGridSpec(
            num_scalar_prefetch=1,
            grid=(T // TOK_TILE,),
            in_specs=[
                pl.BlockSpec((TOK_TILE, D_MODEL), tile),
                pl.BlockSpec((TOK_TILE, LANES), tile),
                pl.BlockSpec((None, 1, D_MODEL), lambda i, ct: (next_layer, 0, 0)),
                pl.BlockSpec((None, D_MODEL, D_IN), lambda i, ct: (next_layer, 0, 0),
                             pipeline_mode=pl.Buffered(1)),
                pl.BlockSpec(memory_space=pl.ANY),
            ],
            out_specs=[
                pl.BlockSpec((TOK_TILE, D_MODEL), tile),
                pl.BlockSpec((TOK_TILE, D_QP), tile),
                pl.BlockSpec((TOK_TILE, D_KV), tile),
            ],
            scratch_shapes=[
                pltpu.VMEM((GATHER_SLOTS, LOCAL_ROWS, D_MODEL), jnp.bfloat16),
                pltpu.SemaphoreType.DMA((GATHER_SLOTS,)),
            ],
        ),
        out_shape=[
            jax.ShapeDtypeStruct((T, D_MODEL), jnp.float32),
            jax.ShapeDtypeStruct((T, D_QP), jnp.bfloat16),
            jax.ShapeDtypeStruct((T, D_KV), jnp.bfloat16),
        ],
        compiler_params=pltpu.CompilerParams(
            dimension_semantics=("arbitrary",), vmem_limit_bytes=VMEM_LIMIT),
        name="combine_proj",
    )(chunk_tab, xmid, rinfo, g1, w_in_k, ys)


def _combine_final(chunk_tab, xmid, rinfo, g, ys):
    T = xmid.shape[0]
    return pl.pallas_call(
        _combine_final_kernel,
        grid_spec=pltpu.PrefetchScalarGridSpec(
            num_scalar_prefetch=1,
            grid=(T // TOK_TILE,),
            in_specs=[
                pl.BlockSpec((TOK_TILE, D_MODEL), lambda i, ct: (i, 0)),
                pl.BlockSpec((TOK_TILE, LANES), lambda i, ct: (i, 0)),
                pl.BlockSpec((1, D_MODEL), lambda i, ct: (0, 0)),
                pl.BlockSpec(memory_space=pl.ANY),
            ],
            out_specs=pl.BlockSpec((TOK_TILE, D_MODEL), lambda i, ct: (i, 0)),
            scratch_shapes=[
                pltpu.VMEM((GATHER_SLOTS, LOCAL_ROWS, D_MODEL), jnp.bfloat16),
                pltpu.SemaphoreType.DMA((GATHER_SLOTS,)),
            ],
        ),
        out_shape=jax.ShapeDtypeStruct((T, D_MODEL), jnp.float32),
        compiler_params=pltpu.CompilerParams(
            dimension_semantics=("arbitrary",), vmem_limit_bytes=VMEM_LIMIT),
        name="combine",
    )(chunk_tab, xmid, rinfo, g, ys)


def _pair_heads(a, axis):
    shape = a.shape
    split = shape[:axis] + (SWA_KV_HEADS, SWA_REP, HEAD_DIM) + shape[axis + 1:]
    return jnp.swapaxes(a.reshape(split), axis, axis + 1).reshape(shape)


def _in_proj_weight(w):
    off_bq = 3 * D_A
    return jnp.concatenate(
        [w[..., 0:D_A], _pair_heads(w[..., off_bq:off_bq + D_B], w.ndim - 1), w[..., D_A:off_bq],
         w[..., off_bq + D_B:]], axis=-1).astype(jnp.bfloat16)


def _out_proj_weight(w):
    return jnp.concatenate(
        [w[:, 0:D_A], _pair_heads(w[:, D_A:D_A + D_B], 1), w[:, D_A + D_B:]], axis=1).astype(jnp.bfloat16)


def _na_bias_table(rel_bias):
    c = np.arange(GRID_W)[:, None]
    cp = np.arange(GRID_W)[None, :]
    cs = np.clip(c - NA_COLS // 2, 0, GRID_W - NA_COLS)
    valid = (cp >= cs) & (cp < cs + NA_COLS)
    d = np.arange(2 * NA_COLS - 1)[:, None, None]
    col_sel = ((cp - c + (NA_COLS - 1))[None] == d) & valid[None]
    k = np.arange(NA_ROWS)[:, None, None]
    j = np.arange(NA_ROWS)[None, :, None]
    r = np.arange(2 * NA_ROWS - 1)[None, None, :]
    row_sel = r == j - k + (NA_ROWS - 1)
    tab = jnp.einsum("kjr,lhrd,dcm->lkhcjm", jnp.asarray(row_sel, jnp.float32), rel_bias.astype(jnp.float32),
                     jnp.asarray(col_sel, jnp.float32), precision=lax.Precision.HIGHEST)
    tab = jnp.where(jnp.asarray(valid)[None, None, None, :, None, :], tab * LOG2E, NEG)
    return tab.reshape(rel_bias.shape[0], NA_ROWS, NA_HEADS * GRID_W, NA_ROWS * GRID_W)


def _swa_bias_table():
    slopes = (2.0 ** (-8.0 * np.arange(1, SWA_Q_HEADS + 1) / SWA_Q_HEADS)).astype(np.float32)
    qi = np.arange(SWA_BLOCK)[:, None]
    ki = np.arange(3 * SWA_BLOCK)[None, :]
    variants = []
    for v in range(3):
        dist = np.abs(ki - qi - v * SWA_BLOCK).astype(np.float32)
        tab = np.where(dist <= SWA_WINDOW, -slopes[:, None, None] * dist[None] * LOG2E, np.float32(NEG))
        variants.append(tab.reshape(SWA_Q_HEADS * SWA_BLOCK, 3 * SWA_BLOCK).astype(np.float32))
    return jnp.asarray(np.stack(variants))


def _block_diag(pool_w):
    depth, n = pool_w.shape[0:2]
    eye = jnp.asarray(np.eye(n, dtype=np.float32))
    out = pool_w[:, :, :, None, :] * eye[None, :, None, :, None]
    return out.reshape(depth, n * POOL_GROUP_DIM, n * POOL_GROUP_DIM)


def _router_weights(rg_w, rg_b, re_w, re_b):
    def lanes(g, e):
        gap = jnp.zeros(g.shape[:-1] + (ROUTE_LANE0 - N_GROUPS,), jnp.float32)
        tail = jnp.zeros(g.shape[:-1] + (LANES - ROUTE_LANE0 - N_EXPERTS,), jnp.float32)
        return jnp.concatenate([g.astype(jnp.float32), gap, e.astype(jnp.float32), tail], axis=-1)

    return lanes(rg_w, re_w).astype(jnp.bfloat16), lanes(rg_b, re_b)[:, None, :]


def _dispatch_tables(cnt, n_tiles):
    nb = cnt.shape[0] // SUBLANES
    n = cnt.reshape(nb, SUBLANES, LANES)[:, 0, ROUTE_LANE0:ROUTE_LANE0 + N_EXPERTS].astype(jnp.int32)
    g = (n + (CHUNK - 1)) // CHUNK
    l_end = jnp.cumsum(g, axis=1)
    l_off = l_end - g
    c_end = jnp.cumsum(g, axis=0)
    c_off = c_end - g
    tot = c_end[-1]
    tiles = (tot + (TILE_CHUNKS - 1)) // TILE_CHUNKS
    t_end = jnp.cumsum(tiles)
    t_off = t_end - tiles
    n_used = t_end[-1:]

    experts = jnp.arange(N_EXPERTS, dtype=jnp.int32)
    tile_ids = jnp.arange(n_tiles, dtype=jnp.int32)
    tile_expert = jnp.minimum(jnp.sum((t_end[None, :] <= tile_ids[:, None]).astype(jnp.int32), axis=1),
                              N_EXPERTS - 1)
    oh_te = (tile_expert[:, None] == experts[None, :]).astype(jnp.int32)
    left = jnp.sum(oh_te * (tot + t_off * TILE_CHUNKS)[None, :], axis=1) - tile_ids * TILE_CHUNKS
    n_sub = jnp.clip((left + (SUB_CHUNKS - 1)) // SUB_CHUNKS, 0, EXP_SUBS)
    has_rows = tiles > 0
    first = ((tile_ids == jnp.sum(oh_te * t_off[None, :], axis=1)) & (n_sub > 0)).astype(jnp.int32)
    wslot = jnp.sum(oh_te * ((jnp.cumsum(has_rows.astype(jnp.int32)) - 1) % 2)[None, :], axis=1)
    later = (experts[None, :] > experts[:, None]) & has_rows[None, :]
    nxt = jnp.min(jnp.where(later, experts[None, :], N_EXPERTS), axis=1)
    next_expert = jnp.sum(oh_te * jnp.where(nxt < N_EXPERTS, nxt, -1)[None, :], axis=1)
    tile_tables = (tile_expert, n_sub, first, wslot, next_expert)

    q = jnp.arange(n_tiles * TILE_CHUNKS, dtype=jnp.int32)
    tile_q = q // TILE_CHUNKS
    oh_e = (jnp.repeat(tile_expert, TILE_CHUNKS)[:, None] == experts[None, :]).astype(jnp.int32)
    ro = q - jnp.sum(oh_e * t_off[None, :], axis=1) * TILE_CHUNKS
    valid = (ro < jnp.sum(oh_e * tot[None, :], axis=1)) & (tile_q < n_used[0])
    cols = jnp.dot(jnp.concatenate([c_end, c_off, l_off], axis=0).astype(jnp.float32),
                   oh_e.T.astype(jnp.float32), precision=lax.Precision.HIGHEST).astype(jnp.int32)
    c_end_q, c_off_q, l_off_q = cols[0:nb], cols[nb:2 * nb], cols[2 * nb:3 * nb]
    b_q = jnp.minimum(jnp.sum((c_end_q <= ro[None, :]).astype(jnp.int32), axis=0), nb - 1)
    oh_b = (jnp.arange(nb, dtype=jnp.int32)[:, None] == b_q[None, :]).astype(jnp.int32)
    src = b_q * LOCAL_CHUNKS + jnp.sum(oh_b * (l_off_q + ro[None, :] - c_off_q), axis=0)
    chunk_src = jnp.where(valid, src, LOCAL_CHUNKS - 1)

    c = jnp.arange(LOCAL_CHUNKS, dtype=jnp.int32)
    e_c = jnp.minimum(jnp.sum((l_end[:, None, :] <= c[None, :, None]).astype(jnp.int32), axis=2),
                      N_EXPERTS - 1)
    oh_ec = (e_c[:, :, None] == experts[None, None, :]).astype(jnp.int32)
    pos = (jnp.sum(oh_ec * (t_off[None, None, :] * TILE_CHUNKS + c_off[:, None, :] - l_off[:, None, :]), axis=2)
           + c[None, :])
    chunk_tab = jnp.where(c[None, :] < l_end[:, -1:], pos, 0).reshape(-1)
    return tile_tables, chunk_src, chunk_tab


def kernel(x, norm1_g, w_in, nat_bias, swa_sink, pool_w, pool_scale, w_out, norm2_g, router_g_w,
           router_g_b, router_e_w, router_e_b, expert_w_gate, expert_w_up, expert_w_down, final_g):
    batch, seq_len, _ = x.shape
    depth = w_in.shape[0]
    T = batch * seq_len
    assert seq_len % TOK_TILE == 0 and TOK_TILE % SWA_BLOCK == 0 and TOK_TILE % GRID_W == 0
    max_chunks = (2 * T) // CHUNK + (T // TOK_TILE) * N_EXPERTS
    n_tiles = max_chunks // TILE_CHUNKS + N_EXPERTS

    swb = _swa_bias_table()
    tri = jnp.asarray(np.tril(np.ones((TOK_TILE, TOK_TILE), np.float32), -1)).astype(jnp.bfloat16)
    utri = jnp.asarray(np.triu(np.ones((LANES, LANES), np.float32), 1)).astype(jnp.bfloat16)

    w_in_k = _in_proj_weight(w_in)
    w_out_k = _out_proj_weight(w_out)
    nab = _na_bias_table(nat_bias)
    sinkcol = jnp.broadcast_to((swa_sink.astype(jnp.float32) * LOG2E)[:, :, None, None],
                               (depth, SWA_Q_HEADS, SWA_BLOCK, LANES)).reshape(depth, SWA_Q_HEADS * SWA_BLOCK, LANES)
    poolw = _block_diag(pool_w).astype(jnp.bfloat16)
    pools = pool_scale.reshape(depth, 1, D_C).astype(jnp.float32)
    rw, rb = _router_weights(router_g_w, router_g_b, router_e_w, router_e_b)
    g1 = norm1_g.reshape(depth, 1, D_MODEL)
    g2 = norm2_g.reshape(depth, 1, D_MODEL)

    x2 = x.reshape(T, D_MODEL)
    qp, kv = _norm_proj(0, x2, g1, w_in_k)
    for l in range(depth):
        xmid, xs_local, rinfo, cnt = _mixer(
            l, x2, qp, kv, (nab, sinkcol, poolw, pools, w_out_k, g2, rw, rb), (swb, tri, utri),
            seq_len=seq_len)
        tile_tables, chunk_src, chunk_tab = _dispatch_tables(cnt, n_tiles)
        ys = _experts(l, tile_tables, chunk_src, xs_local,
                      expert_w_gate, expert_w_up, expert_w_down)
        if l + 1 < depth:
            x2, qp, kv = _combine_proj(l + 1, chunk_tab, xmid, rinfo, g1, w_in_k, ys)
        else:
            x2 = _combine_final(chunk_tab, xmid, rinfo, final_g.reshape(1, D_MODEL), ys)
    return x2.reshape(batch, seq_len, D_MODEL)
```

```python
import functools

import jax
import jax.numpy as jnp
import numpy as np
from jax import lax
from jax.experimental import pallas as pl
from jax.experimental.pallas import tpu as pltpu

D_MODEL = 1024
GRID_W = 64
HEAD_DIM = 64
NA_HEADS = 4
NA_ROWS = 8
NA_COLS = 16
SWA_Q_HEADS = 8
SWA_KV_HEADS = 2
SWA_REP = SWA_Q_HEADS // SWA_KV_HEADS
SWA_WINDOW = 128
SWA_BLOCK = 128
POOL_WINDOWS = (2, 4, 8, 16)
POOL_GROUP_DIM = 64
D_A = NA_HEADS * HEAD_DIM
D_B = SWA_Q_HEADS * HEAD_DIM
D_BKV = SWA_KV_HEADS * HEAD_DIM
D_C = len(POOL_WINDOWS) * POOL_GROUP_DIM
D_MIX = D_A + D_B + D_C
D_QP = D_A + D_B
D_KV = 2 * D_A + 2 * D_BKV + D_C
D_IN = D_QP + D_KV
N_GROUPS = 4
EXPERTS_PER_GROUP = 8
N_EXPERTS = N_GROUPS * EXPERTS_PER_GROUP
D_EXPERT = 256
RMS_EPS = 1e-6
NEG = -1e30
LOG2E = 1.4426950408889634
QK_SCALE = HEAD_DIM ** -0.5 * LOG2E

LANES = 128
SUBLANES = 8
ROW_CHUNKS = D_MODEL // LANES

TOK_TILE = 512
EXP_SUB = 256
EXP_SUBS = 4
EXP_TILE = EXP_SUB * EXP_SUBS
CHUNK = 16
LOCAL_CHUNKS = (2 * TOK_TILE + N_EXPERTS * (CHUNK - 1)) // CHUNK + 2
LOCAL_ROWS = LOCAL_CHUNKS * CHUNK
SUB_CHUNKS = EXP_SUB // CHUNK
TILE_CHUNKS = EXP_TILE // CHUNK
GATHER_AHEAD = 2
GATHER_SLOTS = GATHER_AHEAD + 1
ROUTE_LANE0 = 8
INFO_ROW1, INFO_ROW2, INFO_GATE1, INFO_GATE2 = 0, 1, 2, 3
NA_ROWS_PER_STEP = 8
SWA_BLOCKS_PER_STEP = 4
HALO = 8
WIN_TILES = 3
VMEM_LIMIT = 56 * 1024 * 1024


def _rmsnorm_f32(x, g):
    return x * lax.rsqrt(jnp.mean(x * x, axis=-1, keepdims=True) + RMS_EPS) * g


def _norm_proj_kernel(x_ref, g_ref, w_ref, qp_ref, kv_ref):
    xn = _rmsnorm_f32(x_ref[...], g_ref[...]).astype(jnp.bfloat16)
    proj = jnp.dot(xn, w_ref[...], preferred_element_type=jnp.float32)
    qp_ref[...] = proj[:, :D_QP].astype(jnp.bfloat16)
    kv_ref[...] = proj[:, D_QP:].astype(jnp.bfloat16)


def _norm_proj(layer, x2, g, w):
    T = x2.shape[0]
    return pl.pallas_call(
        _norm_proj_kernel,
        grid=(T // TOK_TILE,),
        in_specs=[
            pl.BlockSpec((TOK_TILE, D_MODEL), lambda i: (i, 0)),
            pl.BlockSpec((None, 1, D_MODEL), lambda i: (layer, 0, 0)),
            pl.BlockSpec((None, D_MODEL, D_IN), lambda i: (layer, 0, 0)),
        ],
        out_specs=[
            pl.BlockSpec((TOK_TILE, D_QP), lambda i: (i, 0)),
            pl.BlockSpec((TOK_TILE, D_KV), lambda i: (i, 0)),
        ],
        out_shape=[
            jax.ShapeDtypeStruct((T, D_QP), jnp.bfloat16),
            jax.ShapeDtypeStruct((T, D_KV), jnp.bfloat16),
        ],
        compiler_params=pltpu.CompilerParams(
            dimension_semantics=("arbitrary",), vmem_limit_bytes=VMEM_LIMIT),
        name="norm_proj",
    )(x2, g, w)


KW_AK, KW_AV, KW_BK, KW_BV = 0, D_A, 2 * D_A, 2 * D_A + D_BKV
KW_COLS = 2 * D_A + 2 * D_BKV
KV_CU = KW_COLS


def _mixer_kernel(x_ref, qp_ref, kwin_ref, nab_ref, swb_ref, sink_ref,
                  poolw_ref, pools_ref, wout_ref, g2_ref, rw_ref, rb_ref, tri_ref, utri_ref,
                  xmid_ref, xs_ref, rinfo_ref, cnt_ref,
                  uwin, mix, xn_scr, logit_scr, *, seq_len):
    t = pl.program_id(0)
    nblk = seq_len // TOK_TILE
    i = jnp.minimum(t, pl.num_programs(0) - 2) % nblk
    rows_per_tile = TOK_TILE // GRID_W
    grid_rows = seq_len // GRID_W

    @pl.when(t == 0)
    def _():
        xn_scr[...] = jnp.zeros_like(xn_scr)
        logit_scr[...] = jnp.zeros_like(logit_scr)

    cur_off = pl.multiple_of((i - jnp.clip(i - 1, 0, nblk - WIN_TILES)) * TOK_TILE, TOK_TILE)

    lane_a = lax.broadcasted_iota(jnp.int32, (GRID_W, D_A), 1) // HEAD_DIM

    def na_row(rr):
        r = i * rows_per_tile + rr
        rs = jnp.clip(r - NA_ROWS // 2, 0, grid_rows - NA_ROWS)
        variant = r - rs
        start = pl.multiple_of(cur_off + (rs - i * rows_per_tile) * GRID_W, GRID_W)
        q0 = pl.multiple_of(rr * GRID_W, GRID_W)
        q = qp_ref[pl.ds(q0, GRID_W), 0:D_A]
        zero = jnp.zeros_like(q)
        qs = jnp.concatenate([jnp.where(lane_a == h, q, zero) for h in range(NA_HEADS)], axis=0)
        kw = kwin_ref[pl.ds(start, NA_ROWS * GRID_W), KW_AK:KW_AK + D_A]
        vw = kwin_ref[pl.ds(start, NA_ROWS * GRID_W), KW_AV:KW_AV + D_A]
        s = lax.dot_general(qs, kw, (((1,), (1,)), ((), ())), preferred_element_type=jnp.float32)
        s = s + nab_ref[variant]
        m = jnp.max(s, axis=-1, keepdims=True)
        p = jnp.exp2(s - m)
        l = jnp.sum(p, axis=-1, keepdims=True)
        pv = jnp.dot(p.astype(jnp.bfloat16), vw, preferred_element_type=jnp.float32)
        pv = pv * (1.0 / l)
        o = jnp.zeros((GRID_W, D_A), jnp.float32)
        for h in range(NA_HEADS):
            o = o + jnp.where(lane_a == h, pv[h * GRID_W:(h + 1) * GRID_W, :], 0.0)
        mix[pl.ds(q0, GRID_W), 0:D_A] = o.astype(jnp.bfloat16)

    def na_step(it, c):
        for k in range(NA_ROWS_PER_STEP):
            na_row(it * NA_ROWS_PER_STEP + k)
        return c

    lax.fori_loop(0, rows_per_tile // NA_ROWS_PER_STEP, na_step, 0)

    lane_b = lax.broadcasted_iota(jnp.int32, (SWA_BLOCK, LANES), 1) // HEAD_DIM
    blocks_per_tile = TOK_TILE // SWA_BLOCK
    nblocks = seq_len // SWA_BLOCK

    ones_v = jnp.ones((3 * SWA_BLOCK, LANES), jnp.bfloat16)

    def swa_step(sb, c):
        n = i * blocks_per_tile + sb
        variant = jnp.where(n == 0, 0, jnp.where(n == nblocks - 1, 2, 1))
        q0 = pl.multiple_of(sb * SWA_BLOCK, SWA_BLOCK)
        k0 = pl.multiple_of(cur_off + (sb - variant) * SWA_BLOCK, SWA_BLOCK)
        kw = kwin_ref[pl.ds(k0, 3 * SWA_BLOCK), KW_BK:KW_BK + D_BKV]
        vaug = jnp.concatenate([kwin_ref[pl.ds(k0, 3 * SWA_BLOCK), KW_BV:KW_BV + D_BKV], ones_v], axis=1)
        outs = []
        for g in range(SWA_KV_HEADS):
            pieces = []
            for t in range(SWA_REP):
                qt = qp_ref[pl.ds(q0, SWA_BLOCK), D_A + t * LANES:D_A + (t + 1) * LANES]
                pieces.append(jnp.where(lane_b == g, qt, jnp.zeros_like(qt)))
            qs = jnp.concatenate(pieces, axis=0)
            r0 = g * SWA_REP * SWA_BLOCK
            s = lax.dot_general(qs, kw, (((1,), (1,)), ((), ())), preferred_element_type=jnp.float32)
            s = s + swb_ref[variant, r0:r0 + SWA_REP * SWA_BLOCK, :]
            sink = sink_ref[r0:r0 + SWA_REP * SWA_BLOCK, :]
            m = jnp.broadcast_to(jnp.max(s, axis=-1, keepdims=True), sink.shape)
            m = jnp.maximum(m, sink)
            p = jnp.exp2(s - jnp.concatenate([m, m, m], axis=1)).astype(jnp.bfloat16)
            pv = jnp.dot(p, vaug, preferred_element_type=jnp.float32)
            l = pv[:, LANES:2 * LANES] + jnp.exp2(sink - m)
            outs.append(pv[:, 0:LANES] * (1.0 / l))
        for t in range(SWA_REP):
            o0 = outs[0][t * SWA_BLOCK:(t + 1) * SWA_BLOCK, :]
            o1 = outs[1][t * SWA_BLOCK:(t + 1) * SWA_BLOCK, :]
            ot = jnp.where(lane_b == 0, o0, o1)
            mix[pl.ds(q0, SWA_BLOCK), D_A + t * LANES:D_A + (t + 1) * LANES] = ot.astype(jnp.bfloat16)
        return c

    def swa_pair(it, c):
        for k in range(SWA_BLOCKS_PER_STEP):
            swa_step(it * SWA_BLOCKS_PER_STEP + k, c)
        return c

    lax.fori_loop(0, blocks_per_tile // SWA_BLOCKS_PER_STEP, swa_pair, 0)

    u = kwin_ref[pl.ds(cur_off, TOK_TILE), KV_CU:KV_CU + D_C].astype(jnp.float32)
    prev_ok = (i > 0).astype(jnp.float32)
    next_ok = (i < nblk - 1).astype(jnp.float32)
    before = pl.multiple_of(jnp.maximum(cur_off - CHUNK, 0), CHUNK)
    after = pl.multiple_of(jnp.minimum(cur_off + TOK_TILE, WIN_TILES * TOK_TILE - CHUNK), CHUNK)
    u_before = kwin_ref[pl.ds(before, CHUNK), KV_CU:KV_CU + D_C].astype(jnp.float32)
    u_after = kwin_ref[pl.ds(after, CHUNK), KV_CU:KV_CU + D_C].astype(jnp.float32)
    uwin[0:HALO, :] = u_before[CHUNK - HALO:CHUNK, :] * prev_ok
    uwin[HALO:HALO + TOK_TILE, :] = u
    uwin[HALO + TOK_TILE:2 * HALO + TOK_TILE, :] = u_after[0:HALO, :] * next_ok
    n_ext = TOK_TILE + 2 * HALO
    a2 = uwin[0:n_ext - 1, :] + uwin[1:n_ext, :]
    a4 = a2[0:n_ext - 3, :] + a2[2:n_ext - 1, :]
    a8 = a4[0:n_ext - 7, :] + a4[4:n_ext - 3, :]
    a16 = a8[0:n_ext - 15, :] + a8[8:n_ext - 7, :]
    w2 = a2[7:7 + TOK_TILE, :]
    w4 = a4[6:6 + TOK_TILE, :]
    w8 = a8[4:4 + TOK_TILE, :]
    w16 = a16[0:TOK_TILE, :]
    lane_c = lax.broadcasted_iota(jnp.int32, (TOK_TILE, D_C), 1) // POOL_GROUP_DIM
    pooled = jnp.where(lane_c == 0, w2, jnp.where(lane_c == 1, w4, jnp.where(lane_c == 2, w8, w16)))
    half = jnp.where(lane_c == 0, 1, jnp.where(lane_c == 1, 2, jnp.where(lane_c == 2, 4, 8)))
    pos = i * TOK_TILE + lax.broadcasted_iota(jnp.int32, (TOK_TILE, D_C), 0)
    cnt = (jnp.minimum(pos + half, seq_len) - jnp.maximum(pos - half, 0)).astype(jnp.float32)
    d = (pooled / cnt - u).astype(jnp.bfloat16)
    oc = jnp.dot(d, poolw_ref[...], preferred_element_type=jnp.float32) * pools_ref[...]
    mix[:, D_A + D_B:D_MIX] = oc.astype(jnp.bfloat16)

    xn = xn_scr[...]
    logits = logit_scr[...]

    xm = x_ref[...] + jnp.dot(mix[...], wout_ref[...], preferred_element_type=jnp.float32)
    xmid_ref[...] = xm
    xn_new = _rmsnorm_f32(xm, g2_ref[...]).astype(jnp.bfloat16)
    xn_scr[...] = xn_new
    logit_scr[...] = jnp.dot(xn_new, rw_ref[...], preferred_element_type=jnp.float32) + rb_ref[...]

    lane = lax.broadcasted_iota(jnp.int32, (TOK_TILE, LANES), 1).astype(jnp.float32)
    is_g = lane < N_GROUPS
    gl = jnp.where(is_g, logits, NEG)
    gmax = jnp.max(gl, axis=-1, keepdims=True)
    gtop = jnp.min(jnp.where(is_g & (gl == gmax), lane, float(LANES)), axis=-1, keepdims=True)
    gprob = 1.0 / jnp.sum(jnp.exp(gl - gmax), axis=-1, keepdims=True)
    e_lo = ROUTE_LANE0 + gtop * EXPERTS_PER_GROUP
    in_grp = (lane >= e_lo) & (lane < e_lo + EXPERTS_PER_GROUP)
    el = jnp.where(in_grp, logits, NEG)
    m1 = jnp.max(el, axis=-1, keepdims=True)
    i1 = jnp.min(jnp.where(in_grp & (el == m1), lane, float(LANES)), axis=-1, keepdims=True)
    el2 = jnp.where(lane == i1, NEG, el)
    m2 = jnp.max(el2, axis=-1, keepdims=True)
    i2 = jnp.min(jnp.where(in_grp & (lane != i1) & (el2 == m2), lane, float(LANES)), axis=-1, keepdims=True)
    r21 = jnp.exp(m2 - m1)
    gate1 = gprob / (1.0 + r21)
    gate2 = gprob * r21 / (1.0 + r21)

    oh1 = lane == i1
    oh2 = lane == i2
    oh = jnp.where(oh1 | oh2, 1.0, 0.0)
    earlier = jnp.dot(tri_ref[...], oh.astype(jnp.bfloat16), preferred_element_type=jnp.float32)
    n_e = jnp.sum(oh, axis=0, keepdims=True)
    chunks_e = jnp.floor((n_e + (CHUNK - 1)) * (1.0 / CHUNK))
    seg0 = jnp.dot(jnp.broadcast_to(chunks_e, (SUBLANES, LANES)).astype(jnp.bfloat16), utri_ref[...],
                   preferred_element_type=jnp.float32)[0:1, :] * CHUNK
    base = earlier + seg0
    lp1 = jnp.sum(jnp.where(oh1, base, 0.0), axis=-1, keepdims=True)
    lp2 = jnp.sum(jnp.where(oh2, base, 0.0), axis=-1, keepdims=True)
    info = jnp.zeros((TOK_TILE, LANES), jnp.float32)
    for k, col in enumerate((lp1, lp2, gate1, gate2)):
        info = jnp.where(lane == k, col, info)
    rinfo_ref[...] = info
    cnt_ref[...] = jnp.broadcast_to(n_e, cnt_ref.shape)

    info_t = info.T
    prow = lax.broadcasted_iota(jnp.int32, (LOCAL_ROWS, TOK_TILE), 0).astype(jnp.float32)
    sel = jnp.where((prow == info_t[INFO_ROW1:INFO_ROW1 + 1, :]) | (prow == info_t[INFO_ROW2:INFO_ROW2 + 1, :]),
                    1.0, 0.0).astype(jnp.bfloat16)
    moved = jnp.dot(sel, xn, preferred_element_type=jnp.float32)
    xs_ref[...] = moved.astype(jnp.bfloat16)


def _mixer(layer, x2, qp, kv, layer_params, shared_tables, *, seq_len):
    nab, sinkcol, poolw, pools, wout, g2, rw, rb = layer_params
    swb, tri, utri = shared_tables
    T = x2.shape[0]
    nblk = seq_len // TOK_TILE
    n_tok_tiles = T // TOK_TILE

    def mixed(t):
        return jnp.minimum(t, n_tok_tiles - 1)

    def cur(t):
        return (mixed(t), 0)

    def window(t):
        seq0 = (mixed(t) // nblk) * nblk
        return (jnp.clip(mixed(t) - 1, seq0, seq0 + nblk - WIN_TILES) * TOK_TILE, 0)

    def dispatched(t):
        return (jnp.maximum(t - 1, 0), 0)

    def resident(a):
        zeros = (0,) * a.ndim
        return pl.BlockSpec(a.shape, lambda t: zeros, pipeline_mode=pl.Buffered(1))

    def resident_layer(a):
        index = (layer,) + (0,) * (a.ndim - 1)
        return pl.BlockSpec((None,) + a.shape[1:], lambda t: index, pipeline_mode=pl.Buffered(1))

    return pl.pallas_call(
        functools.partial(_mixer_kernel, seq_len=seq_len),
        grid=(n_tok_tiles + 1,),
        in_specs=[
            pl.BlockSpec((TOK_TILE, D_MODEL), cur),
            pl.BlockSpec((TOK_TILE, D_QP), cur),
            pl.BlockSpec((pl.Element(WIN_TILES * TOK_TILE), pl.Element(D_KV)), window),
            resident_layer(nab), resident(swb), resident_layer(sinkcol), resident_layer(poolw),
            resident_layer(pools), resident_layer(wout), resident_layer(g2), resident_layer(rw),
            resident_layer(rb), resident(tri), resident(utri),
        ],
        out_specs=[
            pl.BlockSpec((TOK_TILE, D_MODEL), cur),
            pl.BlockSpec((LOCAL_ROWS, D_MODEL), dispatched),
            pl.BlockSpec((TOK_TILE, LANES), dispatched),
            pl.BlockSpec((SUBLANES, LANES), dispatched),
        ],
        out_shape=[
            jax.ShapeDtypeStruct((T, D_MODEL), jnp.float32),
            jax.ShapeDtypeStruct((n_tok_tiles * LOCAL_ROWS, D_MODEL), jnp.bfloat16),
            jax.ShapeDtypeStruct((T, LANES), jnp.float32),
            jax.ShapeDtypeStruct((n_tok_tiles * SUBLANES, LANES), jnp.float32),
        ],
        scratch_shapes=[
            pltpu.VMEM((TOK_TILE + 2 * HALO, D_C), jnp.float32),
            pltpu.VMEM((TOK_TILE, D_MIX), jnp.bfloat16),
            pltpu.VMEM((TOK_TILE, D_MODEL), jnp.bfloat16),
            pltpu.VMEM((TOK_TILE, LANES), jnp.float32),
        ],
        compiler_params=pltpu.CompilerParams(
            dimension_semantics=("arbitrary",), vmem_limit_bytes=VMEM_LIMIT),
        name="mixer",
    )(x2, qp, kv, nab, swb, sinkcol, poolw, pools, wout, g2, rw, rb, tri, utri)


def _chunk_copy(src_hbm, src_chunk, dst, dst_chunk, sem):
    return pltpu.make_async_copy(
        src_hbm.at[pl.ds(pl.multiple_of(src_chunk * CHUNK, CHUNK), CHUNK)],
        dst.at[pl.ds(dst_chunk * CHUNK, CHUNK)],
        sem)


def _expert_kernel(te_ref, nsub_ref, first_ref, wslot_ref, nexte_ref, csrc_ref,
                   xs_hbm, wg_hbm, wu_hbm, wd_hbm, y_ref,
                   xbuf, wgf, wuf, wdf, wgb, wub, wdb, sem, wsem, *, layer):
    j = pl.program_id(0)
    nt = pl.num_programs(0)
    slot = j % GATHER_SLOTS

    def start_gather(tile, s, h):
        for c in range(h * SUB_CHUNKS, (h + 1) * SUB_CHUNKS):
            _chunk_copy(xs_hbm, csrc_ref[tile * TILE_CHUNKS + c], xbuf.at[s], c, sem.at[s, h]).start()

    def wait_gather(s, h):
        rows = pl.ds(h * EXP_SUB, EXP_SUB)
        pltpu.make_async_copy(xs_hbm.at[pl.ds(0, EXP_SUB)], xbuf.at[s, rows], sem.at[s, h]).wait()

    def weight_copies(expert, ws):
        return [pltpu.make_async_copy(w_hbm.at[layer, expert], wbuf.at[ws], wsem.at[k, ws])
                for k, (w_hbm, wbuf) in enumerate(((wg_hbm, wgf), (wu_hbm, wuf), (wd_hbm, wdf)))]

    for h in range(EXP_SUBS):
        for first_tile in range(GATHER_AHEAD):
            @pl.when((j == 0) & (h < nsub_ref[first_tile]))
            def _():
                start_gather(first_tile, first_tile, h)

        ahead = j + GATHER_AHEAD

        @pl.when(h < nsub_ref[jnp.minimum(ahead, nt - 1)] * (ahead < nt).astype(jnp.int32))
        def _():
            start_gather(ahead, ahead % GATHER_SLOTS, h)

    ws = wslot_ref[j]

    @pl.when(j == 0)
    def _():
        for cp in weight_copies(te_ref[0], 0):
            cp.start()

    @pl.when(first_ref[j] > 0)
    def _():
        for cp in weight_copies(te_ref[j], ws):
            cp.wait()

        @pl.when(nexte_ref[j] >= 0)
        def _():
            for cp in weight_copies(nexte_ref[j], 1 - ws):
                cp.start()

        wgb[...] = wgf[ws].astype(jnp.bfloat16)
        wub[...] = wuf[ws].astype(jnp.bfloat16)
        wdb[...] = wdf[ws].astype(jnp.bfloat16)

    def gated_mlp(n_rows):
        xs = xbuf[slot, 0:n_rows, :]
        gate = jnp.dot(xs, wgb[...], preferred_element_type=jnp.float32)
        up = jnp.dot(xs, wub[...], preferred_element_type=jnp.float32)
        act = (gate * (1.0 / (1.0 + jnp.exp(-gate))) * up).astype(jnp.bfloat16)
        return jnp.dot(act, wdb[...], preferred_element_type=jnp.float32).astype(jnp.bfloat16)

    for k in range(1, EXP_SUBS + 1):
        @pl.when(nsub_ref[j] == k)
        def _():
            for h in range(k):
                wait_gather(slot, h)
            y_ref[0:k * EXP_SUB, :] = gated_mlp(k * EXP_SUB)
            if k < EXP_SUBS:
                y_ref[k * EXP_SUB:EXP_TILE, :] = jnp.zeros((EXP_TILE - k * EXP_SUB, D_MODEL), jnp.bfloat16)

    @pl.when(nsub_ref[j] == 0)
    def _():
        y_ref[...] = jnp.zeros_like(y_ref)


def _experts(layer, tile_tables, chunk_src, xs_local, wg, wu, wd):
    tile_expert, n_sub, first, wslot, next_expert = tile_tables
    n_tiles = tile_expert.shape[0]
    any_space = pl.BlockSpec(memory_space=pl.ANY)

    return pl.pallas_call(
        functools.partial(_expert_kernel, layer=layer),
        grid_spec=pltpu.PrefetchScalarGridSpec(
            num_scalar_prefetch=6,
            grid=(n_tiles,),
            in_specs=[any_space] * 4,
            out_specs=pl.BlockSpec((EXP_TILE, D_MODEL), lambda j, *tables: (j, 0)),
            scratch_shapes=[
                pltpu.VMEM((GATHER_SLOTS, EXP_TILE, D_MODEL), jnp.bfloat16),
                pltpu.VMEM((2, D_MODEL, D_EXPERT), jnp.float32),
                pltpu.VMEM((2, D_MODEL, D_EXPERT), jnp.float32),
                pltpu.VMEM((2, D_EXPERT, D_MODEL), jnp.float32),
                pltpu.VMEM((D_MODEL, D_EXPERT), jnp.bfloat16),
                pltpu.VMEM((D_MODEL, D_EXPERT), jnp.bfloat16),
                pltpu.VMEM((D_EXPERT, D_MODEL), jnp.bfloat16),
                pltpu.SemaphoreType.DMA((GATHER_SLOTS, EXP_SUBS)),
                pltpu.SemaphoreType.DMA((3, 2)),
            ],
        ),
        out_shape=jax.ShapeDtypeStruct((n_tiles * EXP_TILE, D_MODEL), jnp.bfloat16),
        compiler_params=pltpu.CompilerParams(
            dimension_semantics=("arbitrary",), vmem_limit_bytes=VMEM_LIMIT),
        name="experts",
    )(tile_expert, n_sub, first, wslot, next_expert, chunk_src, xs_local, wg, wu, wd)


def _combine_tile(ctab_ref, xmid_ref, rinfo_ref, ys_hbm, ybuf, sem):
    i = pl.program_id(0)
    nt = pl.num_programs(0)
    slot = i % GATHER_SLOTS

    def start_gather(tile, s):
        for c in range(LOCAL_CHUNKS):
            _chunk_copy(ys_hbm, ctab_ref[tile * LOCAL_CHUNKS + c], ybuf.at[s], c, sem.at[s]).start()

    def wait_gather(s):
        pltpu.make_async_copy(ys_hbm.at[pl.ds(0, LOCAL_ROWS)], ybuf.at[s], sem.at[s]).wait()

    @pl.when(i == 0)
    def _():
        for first_tile in range(GATHER_AHEAD):
            start_gather(first_tile, first_tile)

    @pl.when(i + GATHER_AHEAD < nt)
    def _():
        start_gather(i + GATHER_AHEAD, (i + GATHER_AHEAD) % GATHER_SLOTS)

    wait_gather(slot)
    info = rinfo_ref[...]
    pcol = lax.broadcasted_iota(jnp.int32, (TOK_TILE, LOCAL_ROWS), 1).astype(jnp.float32)
    pick = (jnp.where(pcol == info[:, INFO_ROW1:INFO_ROW1 + 1], info[:, INFO_GATE1:INFO_GATE1 + 1], 0.0)
            + jnp.where(pcol == info[:, INFO_ROW2:INFO_ROW2 + 1], info[:, INFO_GATE2:INFO_GATE2 + 1], 0.0)
            ).astype(jnp.bfloat16)
    return xmid_ref[...] + jnp.dot(pick, ybuf[slot], preferred_element_type=jnp.float32)


def _combine_final_kernel(ctab_ref, xmid_ref, rinfo_ref, g_ref, ys_hbm, out_ref, ybuf, sem):
    x = _combine_tile(ctab_ref, xmid_ref, rinfo_ref, ys_hbm, ybuf, sem)
    out_ref[...] = _rmsnorm_f32(x, g_ref[...])


def _combine_proj_kernel(ctab_ref, xmid_ref, rinfo_ref, g_ref, w_ref, ys_hbm, x_ref, qp_ref, kv_ref, ybuf, sem):
    x = _combine_tile(ctab_ref, xmid_ref, rinfo_ref, ys_hbm, ybuf, sem)
    x_ref[...] = x
    xn = _rmsnorm_f32(x, g_ref[...]).astype(jnp.bfloat16)
    proj = jnp.dot(xn, w_ref[...], preferred_element_type=jnp.float32)
    qp_ref[...] = proj[:, :D_QP].astype(jnp.bfloat16)
    kv_ref[...] = proj[:, D_QP:].astype(jnp.bfloat16)


def _combine_proj(next_layer, chunk_tab, xmid, rinfo, g1, w_in_k, ys):
    T = xmid.shape[0]

    def tile(i, ct):
        return (i, 0)

    return pl.pallas_call(
        _combine_proj_kernel,
        grid_spec=pltpu.PrefetchScalarGridSpec(
            num_scalar_prefetch=1,
            grid=(T // TOK_TILE,),
            in_specs=[
                pl.BlockSpec((TOK_TILE, D_MODEL), tile),
                pl.BlockSpec((TOK_TILE, LANES), tile),
                pl.BlockSpec((None, 1, D_MODEL), lambda i, ct: (next_layer, 0, 0)),
                pl.BlockSpec((None, D_MODEL, D_IN), lambda i, ct: (next_layer, 0, 0),
                             pipeline_mode=pl.Buffered(1)),
                pl.BlockSpec(memory_space=pl.ANY),
            ],
            out_specs=[
                pl.BlockSpec((TOK_TILE, D_MODEL), tile),
                pl.BlockSpec((TOK_TILE, D_QP), tile),
                pl.BlockSpec((TOK_TILE, D_KV), tile),
            ],
            scratch_shapes=[
                pltpu.VMEM((GATHER_SLOTS, LOCAL_ROWS, D_MODEL), jnp.bfloat16),
                pltpu.SemaphoreType.DMA((GATHER_SLOTS,)),
            ],
        ),
        out_shape=[
            jax.ShapeDtypeStruct((T, D_MODEL), jnp.float32),
            jax.ShapeDtypeStruct((T, D_QP), jnp.bfloat16),
            jax.ShapeDtypeStruct((T, D_KV), jnp.bfloat16),
        ],
        compiler_params=pltpu.CompilerParams(
            dimension_semantics=("arbitrary",), vmem_limit_bytes=VMEM_LIMIT),
        name="combine_proj",
    )(chunk_tab, xmid, rinfo, g1, w_in_k, ys)


def _combine_final(chunk_tab, xmid, rinfo, g, ys):
    T = xmid.shape[0]
    return pl.pallas_call(
        _combine_final_kernel,
        grid_spec=pltpu.PrefetchScalarGridSpec(
            num_scalar_prefetch=1,
            grid=(T // TOK_TILE,),
            in_specs=[
                pl.BlockSpec((TOK_TILE, D_MODEL), lambda i, ct: (i, 0)),
                pl.BlockSpec((TOK_TILE, LANES), lambda i, ct: (i, 0)),
                pl.BlockSpec((1, D_MODEL), lambda i, ct: (0, 0)),
                pl.BlockSpec(memory_space=pl.ANY),
            ],
            out_specs=pl.BlockSpec((TOK_TILE, D_MODEL), lambda i, ct: (i, 0)),
            scratch_shapes=[
                pltpu.VMEM((GATHER_SLOTS, LOCAL_ROWS, D_MODEL), jnp.bfloat16),
                pltpu.SemaphoreType.DMA((GATHER_SLOTS,)),
            ],
        ),
        out_shape=jax.ShapeDtypeStruct((T, D_MODEL), jnp.float32),
        compiler_params=pltpu.CompilerParams(
            dimension_semantics=("arbitrary",), vmem_limit_bytes=VMEM_LIMIT),
        name="combine",
    )(chunk_tab, xmid, rinfo, g, ys)


def _pair_heads(a, axis):
    shape = a.shape
    split = shape[:axis] + (SWA_KV_HEADS, SWA_REP, HEAD_DIM) + shape[axis + 1:]
    return jnp.swapaxes(a.reshape(split), axis, axis + 1).reshape(shape)


def _in_proj_weight(w):
    off_bq = 3 * D_A
    return jnp.concatenate(
        [w[..., 0:D_A] * QK_SCALE, _pair_heads(w[..., off_bq:off_bq + D_B], w.ndim - 1) * QK_SCALE,
         w[..., D_A:off_bq], w[..., off_bq + D_B:]], axis=-1).astype(jnp.bfloat16)


def _out_proj_weight(w):
    return jnp.concatenate(
        [w[:, 0:D_A], _pair_heads(w[:, D_A:D_A + D_B], 1), w[:, D_A + D_B:]], axis=1).astype(jnp.bfloat16)


def _na_bias_table(rel_bias):
    c = np.arange(GRID_W)[:, None]
    cp = np.arange(GRID_W)[None, :]
    cs = np.clip(c - NA_COLS // 2, 0, GRID_W - NA_COLS)
    valid = (cp >= cs) & (cp < cs + NA_COLS)
    d = np.arange(2 * NA_COLS - 1)[:, None, None]
    col_sel = ((cp - c + (NA_COLS - 1))[None] == d) & valid[None]
    k = np.arange(NA_ROWS)[:, None, None]
    j = np.arange(NA_ROWS)[None, :, None]
    r = np.arange(2 * NA_ROWS - 1)[None, None, :]
    row_sel = r == j - k + (NA_ROWS - 1)
    tab = jnp.einsum("kjr,lhrd,dcm->lkhcjm", jnp.asarray(row_sel, jnp.float32), rel_bias.astype(jnp.float32),
                     jnp.asarray(col_sel, jnp.float32), precision=lax.Precision.HIGHEST)
    tab = jnp.where(jnp.asarray(valid)[None, None, None, :, None, :], tab * LOG2E, NEG)
    return tab.reshape(rel_bias.shape[0], NA_ROWS, NA_HEADS * GRID_W, NA_ROWS * GRID_W)


def _swa_bias_table():
    slopes = (2.0 ** (-8.0 * np.arange(1, SWA_Q_HEADS + 1) / SWA_Q_HEADS)).astype(np.float32)
    qi = np.arange(SWA_BLOCK)[:, None]
    ki = np.arange(3 * SWA_BLOCK)[None, :]
    variants = []
    for v in range(3):
        dist = np.abs(ki - qi - v * SWA_BLOCK).astype(np.float32)
        tab = np.where(dist <= SWA_WINDOW, -slopes[:, None, None] * dist[None] * LOG2E, np.float32(NEG))
        variants.append(tab.reshape(SWA_Q_HEADS * SWA_BLOCK, 3 * SWA_BLOCK).astype(np.float32))
    return jnp.asarray(np.stack(variants))


def _block_diag(pool_w):
    depth, n = pool_w.shape[0:2]
    eye = jnp.asarray(np.eye(n, dtype=np.float32))
    out = pool_w[:, :, :, None, :] * eye[None, :, None, :, None]
    return out.reshape(depth, n * POOL_GROUP_DIM, n * POOL_GROUP_DIM)


def _router_weights(rg_w, rg_b, re_w, re_b):
    def lanes(g, e):
        gap = jnp.zeros(g.shape[:-1] + (ROUTE_LANE0 - N_GROUPS,), jnp.float32)
        tail = jnp.zeros(g.shape[:-1] + (LANES - ROUTE_LANE0 - N_EXPERTS,), jnp.float32)
        return jnp.concatenate([g.astype(jnp.float32), gap, e.astype(jnp.float32), tail], axis=-1)

    return lanes(rg_w, re_w).astype(jnp.bfloat16), lanes(rg_b, re_b)[:, None, :]


def _dispatch_tables(cnt, n_tiles):
    nb = cnt.shape[0] // SUBLANES
    n = cnt.reshape(nb, SUBLANES, LANES)[:, 0, ROUTE_LANE0:ROUTE_LANE0 + N_EXPERTS].astype(jnp.int32)
    g = (n + (CHUNK - 1)) // CHUNK
    l_end = jnp.cumsum(g, axis=1)
    l_off = l_end - g
    c_end = jnp.cumsum(g, axis=0)
    c_off = c_end - g
    tot = c_end[-1]
    tiles = (tot + (TILE_CHUNKS - 1)) // TILE_CHUNKS
    t_end = jnp.cumsum(tiles)
    t_off = t_end - tiles
    n_used = t_end[-1:]

    experts = jnp.arange(N_EXPERTS, dtype=jnp.int32)
    tile_ids = jnp.arange(n_tiles, dtype=jnp.int32)
    tile_expert = jnp.minimum(jnp.sum((t_end[None, :] <= tile_ids[:, None]).astype(jnp.int32), axis=1),
                              N_EXPERTS - 1)
    oh_te = (tile_expert[:, None] == experts[None, :]).astype(jnp.int32)
    left = jnp.sum(oh_te * (tot + t_off * TILE_CHUNKS)[None, :], axis=1) - tile_ids * TILE_CHUNKS
    n_sub = jnp.clip((left + (SUB_CHUNKS - 1)) // SUB_CHUNKS, 0, EXP_SUBS)
    has_rows = tiles > 0
    first = ((tile_ids == jnp.sum(oh_te * t_off[None, :], axis=1)) & (n_sub > 0)).astype(jnp.int32)
    wslot = jnp.sum(oh_te * ((jnp.cumsum(has_rows.astype(jnp.int32)) - 1) % 2)[None, :], axis=1)
    later = (experts[None, :] > experts[:, None]) & has_rows[None, :]
    nxt = jnp.min(jnp.where(later, experts[None, :], N_EXPERTS), axis=1)
    next_expert = jnp.sum(oh_te * jnp.where(nxt < N_EXPERTS, nxt, -1)[None, :], axis=1)
    tile_tables = (tile_expert, n_sub, first, wslot, next_expert)

    q = jnp.arange(n_tiles * TILE_CHUNKS, dtype=jnp.int32)
    tile_q = q // TILE_CHUNKS
    oh_e = (jnp.repeat(tile_expert, TILE_CHUNKS)[:, None] == experts[None, :]).astype(jnp.int32)
    ro = q - jnp.sum(oh_e * t_off[None, :], axis=1) * TILE_CHUNKS
    valid = (ro < jnp.sum(oh_e * tot[None, :], axis=1)) & (tile_q < n_used[0])
    cols = jnp.dot(jnp.concatenate([c_end, c_off, l_off], axis=0).astype(jnp.float32),
                   oh_e.T.astype(jnp.float32), precision=lax.Precision.HIGHEST).astype(jnp.int32)
    c_end_q, c_off_q, l_off_q = cols[0:nb], cols[nb:2 * nb], cols[2 * nb:3 * nb]
    b_q = jnp.minimum(jnp.sum((c_end_q <= ro[None, :]).astype(jnp.int32), axis=0), nb - 1)
    oh_b = (jnp.arange(nb, dtype=jnp.int32)[:, None] == b_q[None, :]).astype(jnp.int32)
    src = b_q * LOCAL_CHUNKS + jnp.sum(oh_b * (l_off_q + ro[None, :] - c_off_q), axis=0)
    chunk_src = jnp.where(valid, src, LOCAL_CHUNKS - 1)

    c = jnp.arange(LOCAL_CHUNKS, dtype=jnp.int32)
    e_c = jnp.minimum(jnp.sum((l_end[:, None, :] <= c[None, :, None]).astype(jnp.int32), axis=2),
                      N_EXPERTS - 1)
    oh_ec = (e_c[:, :, None] == experts[None, None, :]).astype(jnp.int32)
    pos = (jnp.sum(oh_ec * (t_off[None, None, :] * TILE_CHUNKS + c_off[:, None, :] - l_off[:, None, :]), axis=2)
           + c[None, :])
    chunk_tab = jnp.where(c[None, :] < l_end[:, -1:], pos, 0).reshape(-1)
    return tile_tables, chunk_src, chunk_tab


def kernel(x, norm1_g, w_in, nat_bias, swa_sink, pool_w, pool_scale, w_out, norm2_g, router_g_w,
           router_g_b, router_e_w, router_e_b, expert_w_gate, expert_w_up, expert_w_down, final_g):
    batch, seq_len, _ = x.shape
    depth = w_in.shape[0]
    T = batch * seq_len
    assert seq_len % TOK_TILE == 0 and TOK_TILE % SWA_BLOCK == 0 and TOK_TILE % GRID_W == 0
    max_chunks = (2 * T) // CHUNK + (T // TOK_TILE) * N_EXPERTS
    n_tiles = max_chunks // TILE_CHUNKS + N_EXPERTS

    swb = _swa_bias_table()
    tri = jnp.asarray(np.tril(np.ones((TOK_TILE, TOK_TILE), np.float32), -1)).astype(jnp.bfloat16)
    utri = jnp.asarray(np.triu(np.ones((LANES, LANES), np.float32), 1)).astype(jnp.bfloat16)

    w_in_k = _in_proj_weight(w_in)
    w_out_k = _out_proj_weight(w_out)
    nab = _na_bias_table(nat_bias)
    sinkcol = jnp.broadcast_to((swa_sink.astype(jnp.float32) * LOG2E)[:, :, None, None],
                               (depth, SWA_Q_HEADS, SWA_BLOCK, LANES)).reshape(depth, SWA_Q_HEADS * SWA_BLOCK, LANES)
    poolw = _block_diag(pool_w).astype(jnp.bfloat16)
    pools = pool_scale.reshape(depth, 1, D_C).astype(jnp.float32)
    rw, rb = _router_weights(router_g_w, router_g_b, router_e_w, router_e_b)
    g1 = norm1_g.reshape(depth, 1, D_MODEL)
    g2 = norm2_g.reshape(depth, 1, D_MODEL)

    x2 = x.reshape(T, D_MODEL)
    qp, kv = _norm_proj(0, x2, g1, w_in_k)
    for l in range(depth):
        xmid, xs_local, rinfo, cnt = _mixer(
            l, x2, qp, kv, (nab, sinkcol, poolw, pools, w_out_k, g2, rw, rb), (swb, tri, utri),
            seq_len=seq_len)
        tile_tables, chunk_src, chunk_tab = _dispatch_tables(cnt, n_tiles)
        ys = _experts(l, tile_tables, chunk_src, xs_local,
                      expert_w_gate, expert_w_up, expert_w_down)
        if l + 1 < depth:
            x2, qp, kv = _combine_proj(l + 1, chunk_tab, xmid, rinfo, g1, w_in_k, ys)
        else:
            x2 = _combine_final(chunk_tab, xmid, rinfo, final_g.reshape(1, D_MODEL), ys)
    return x2.reshape(batch, seq_len, D_MODEL)
```

```python
import functools

import jax
import jax.numpy as jnp
import numpy as np
from jax import lax
from jax.experimental import pallas as pl
from jax.experimental.pallas import tpu as pltpu

D_MODEL = 1024
GRID_W = 64
HEAD_DIM = 64
NA_HEADS = 4
NA_ROWS = 8
NA_COLS = 16
SWA_Q_HEADS = 8
SWA_KV_HEADS = 2
SWA_REP = SWA_Q_HEADS // SWA_KV_HEADS
SWA_WINDOW = 128
SWA_BLOCK = 128
POOL_WINDOWS = (2, 4, 8, 16)
POOL_GROUP_DIM = 64
D_A = NA_HEADS * HEAD_DIM
D_B = SWA_Q_HEADS * HEAD_DIM
D_BKV = SWA_KV_HEADS * HEAD_DIM
D_C = len(POOL_WINDOWS) * POOL_GROUP_DIM
D_MIX = D_A + D_B + D_C
D_QP = D_A + D_B
D_KV = 2 * D_A + 2 * D_BKV + D_C
D_IN = D_QP + D_KV
N_GROUPS = 4
EXPERTS_PER_GROUP = 8
N_EXPERTS = N_GROUPS * EXPERTS_PER_GROUP
D_EXPERT = 256
RMS_EPS = 1e-6
NEG = -1e30
LOG2E = 1.4426950408889634
QK_SCALE = HEAD_DIM ** -0.5 * LOG2E

LANES = 128
SUBLANES = 8
ROW_CHUNKS = D_MODEL // LANES

TOK_TILE = 512
EXP_SUB = 256
EXP_SUBS = 4
EXP_TILE = EXP_SUB * EXP_SUBS
CHUNK = 16
LOCAL_CHUNKS = (2 * TOK_TILE + N_EXPERTS * (CHUNK - 1)) // CHUNK + 2
LOCAL_ROWS = LOCAL_CHUNKS * CHUNK
SUB_CHUNKS = EXP_SUB // CHUNK
TILE_CHUNKS = EXP_TILE // CHUNK
GATHER_AHEAD = 2
GATHER_SLOTS = GATHER_AHEAD + 1
ROUTE_LANE0 = 8
INFO_ROW1, INFO_ROW2, INFO_GATE1, INFO_GATE2 = 0, 1, 2, 3
NA_ROWS_PER_STEP = 8
SWA_BLOCKS_PER_STEP = 4
HALO = 8
WIN_TILES = 3
VMEM_LIMIT = 56 * 1024 * 1024


def _rmsnorm_f32(x, g):
    return x * lax.rsqrt(jnp.mean(x * x, axis=-1, keepdims=True) + RMS_EPS) * g


def _norm_proj_kernel(x_ref, g_ref, w_ref, qp_ref, kv_ref):
    xn = _rmsnorm_f32(x_ref[...], g_ref[...]).astype(jnp.bfloat16)
    proj = jnp.dot(xn, w_ref[...], preferred_element_type=jnp.float32)
    qp_ref[...] = proj[:, :D_QP].astype(jnp.bfloat16)
    kv_ref[...] = proj[:, D_QP:].astype(jnp.bfloat16)


def _norm_proj(layer, x2, g, w):
    T = x2.shape[0]
    return pl.pallas_call(
        _norm_proj_kernel,
        grid=(T // TOK_TILE,),
        in_specs=[
            pl.BlockSpec((TOK_TILE, D_MODEL), lambda i: (i, 0)),
            pl.BlockSpec((None, 1, D_MODEL), lambda i: (layer, 0, 0)),
            pl.BlockSpec((None, D_MODEL, D_IN), lambda i: (layer, 0, 0)),
        ],
        out_specs=[
            pl.BlockSpec((TOK_TILE, D_QP), lambda i: (i, 0)),
            pl.BlockSpec((TOK_TILE, D_KV), lambda i: (i, 0)),
        ],
        out_shape=[
            jax.ShapeDtypeStruct((T, D_QP), jnp.bfloat16),
            jax.ShapeDtypeStruct((T, D_KV), jnp.bfloat16),
        ],
        compiler_params=pltpu.CompilerParams(
            dimension_semantics=("arbitrary",), vmem_limit_bytes=VMEM_LIMIT),
        name="norm_proj",
    )(x2, g, w)


KW_AK, KW_AV, KW_BK, KW_BV = 0, D_A, 2 * D_A, 2 * D_A + D_BKV
KW_COLS = 2 * D_A + 2 * D_BKV
KV_CU = KW_COLS


def _mixer_kernel(x_ref, qp_ref, kwin_ref, nab_ref, swb_ref, sink_ref,
                  poolw_ref, pools_ref, wout_ref, g2_ref, rw_ref, rb_ref, tri_ref, utri_ref,
                  xmid_ref, xs_ref, rinfo_ref, cnt_ref,
                  uwin, mix, xn_scr, logit_scr, *, seq_len):
    t = pl.program_id(0)
    nblk = seq_len // TOK_TILE
    i = jnp.minimum(t, pl.num_programs(0) - 2) % nblk
    rows_per_tile = TOK_TILE // GRID_W
    grid_rows = seq_len // GRID_W

    @pl.when(t == 0)
    def _():
        xn_scr[...] = jnp.zeros_like(xn_scr)
        logit_scr[...] = jnp.zeros_like(logit_scr)

    cur_off = pl.multiple_of((i - jnp.clip(i - 1, 0, nblk - WIN_TILES)) * TOK_TILE, TOK_TILE)

    lane_a = lax.broadcasted_iota(jnp.int32, (GRID_W, D_A), 1) // HEAD_DIM

    def na_row(rr):
        r = i * rows_per_tile + rr
        rs = jnp.clip(r - NA_ROWS // 2, 0, grid_rows - NA_ROWS)
        variant = r - rs
        start = pl.multiple_of(cur_off + (rs - i * rows_per_tile) * GRID_W, GRID_W)
        q0 = pl.multiple_of(rr * GRID_W, GRID_W)
        q = qp_ref[pl.ds(q0, GRID_W), 0:D_A]
        zero = jnp.zeros_like(q)
        qs = jnp.concatenate([jnp.where(lane_a == h, q, zero) for h in range(NA_HEADS)], axis=0)
        kw = kwin_ref[pl.ds(start, NA_ROWS * GRID_W), KW_AK:KW_AK + D_A]
        vw = kwin_ref[pl.ds(start, NA_ROWS * GRID_W), KW_AV:KW_AV + D_A]
        s = lax.dot_general(qs, kw, (((1,), (1,)), ((), ())), preferred_element_type=jnp.float32)
        s = s + nab_ref[variant]
        m = jnp.max(s, axis=-1, keepdims=True)
        p = jnp.exp2(s - m)
        l = jnp.sum(p, axis=-1, keepdims=True)
        pv = jnp.dot(p.astype(jnp.bfloat16), vw, preferred_element_type=jnp.float32)
        pv = pv * (1.0 / l)
        o = jnp.zeros((GRID_W, D_A), jnp.float32)
        for h in range(NA_HEADS):
            o = o + jnp.where(lane_a == h, pv[h * GRID_W:(h + 1) * GRID_W, :], 0.0)
        mix[pl.ds(q0, GRID_W), 0:D_A] = o.astype(jnp.bfloat16)

    def na_step(it, c):
        for k in range(NA_ROWS_PER_STEP):
            na_row(it * NA_ROWS_PER_STEP + k)
        return c

    lax.fori_loop(0, rows_per_tile // NA_ROWS_PER_STEP, na_step, 0)

    lane_b = lax.broadcasted_iota(jnp.int32, (SWA_BLOCK, LANES), 1) // HEAD_DIM
    blocks_per_tile = TOK_TILE // SWA_BLOCK
    nblocks = seq_len // SWA_BLOCK

    ones_v = jnp.ones((3 * SWA_BLOCK, LANES), jnp.bfloat16)

    def swa_step(sb, c):
        n = i * blocks_per_tile + sb
        variant = jnp.where(n == 0, 0, jnp.where(n == nblocks - 1, 2, 1))
        q0 = pl.multiple_of(sb * SWA_BLOCK, SWA_BLOCK)
        k0 = pl.multiple_of(cur_off + (sb - variant) * SWA_BLOCK, SWA_BLOCK)
        kw = kwin_ref[pl.ds(k0, 3 * SWA_BLOCK), KW_BK:KW_BK + D_BKV]
        vaug = jnp.concatenate([kwin_ref[pl.ds(k0, 3 * SWA_BLOCK), KW_BV:KW_BV + D_BKV], ones_v], axis=1)
        outs = []
        for g in range(SWA_KV_HEADS):
            pieces = []
            for t in range(SWA_REP):
                qt = qp_ref[pl.ds(q0, SWA_BLOCK), D_A + t * LANES:D_A + (t + 1) * LANES]
                pieces.append(jnp.where(lane_b == g, qt, jnp.zeros_like(qt)))
            qs = jnp.concatenate(pieces, axis=0)
            r0 = g * SWA_REP * SWA_BLOCK
            s = lax.dot_general(qs, kw, (((1,), (1,)), ((), ())), preferred_element_type=jnp.float32)
            s = s + swb_ref[variant, r0:r0 + SWA_REP * SWA_BLOCK, :]
            sink = sink_ref[r0:r0 + SWA_REP * SWA_BLOCK, :]
            m = jnp.broadcast_to(jnp.max(s, axis=-1, keepdims=True), sink.shape)
            m = jnp.maximum(m, sink)
            p = jnp.exp2(s - jnp.concatenate([m, m, m], axis=1)).astype(jnp.bfloat16)
            pv = jnp.dot(p, vaug, preferred_element_type=jnp.float32)
            l = pv[:, LANES:2 * LANES] + jnp.exp2(sink - m)
            outs.append(pv[:, 0:LANES] * (1.0 / l))
        for t in range(SWA_REP):
            o0 = outs[0][t * SWA_BLOCK:(t + 1) * SWA_BLOCK, :]
            o1 = outs[1][t * SWA_BLOCK:(t + 1) * SWA_BLOCK, :]
            ot = jnp.where(lane_b == 0, o0, o1)
            mix[pl.ds(q0, SWA_BLOCK), D_A + t * LANES:D_A + (t + 1) * LANES] = ot.astype(jnp.bfloat16)
        return c

    def swa_pair(it, c):
        for k in range(SWA_BLOCKS_PER_STEP):
            swa_step(it * SWA_BLOCKS_PER_STEP + k, c)
        return c

    lax.fori_loop(0, blocks_per_tile // SWA_BLOCKS_PER_STEP, swa_pair, 0)

    u = kwin_ref[pl.ds(cur_off, TOK_TILE), KV_CU:KV_CU + D_C].astype(jnp.float32)
    prev_ok = (i > 0).astype(jnp.float32)
    next_ok = (i < nblk - 1).astype(jnp.float32)
    before = pl.multiple_of(jnp.maximum(cur_off - CHUNK, 0), CHUNK)
    after = pl.multiple_of(jnp.minimum(cur_off + TOK_TILE, WIN_TILES * TOK_TILE - CHUNK), CHUNK)
    u_before = kwin_ref[pl.ds(before, CHUNK), KV_CU:KV_CU + D_C].astype(jnp.float32)
    u_after = kwin_ref[pl.ds(after, CHUNK), KV_CU:KV_CU + D_C].astype(jnp.float32)
    uwin[0:HALO, :] = u_before[CHUNK - HALO:CHUNK, :] * prev_ok
    uwin[HALO:HALO + TOK_TILE, :] = u
    uwin[HALO + TOK_TILE:2 * HALO + TOK_TILE, :] = u_after[0:HALO, :] * next_ok
    n_ext = TOK_TILE + 2 * HALO
    a2 = uwin[0:n_ext - 1, :] + uwin[1:n_ext, :]
    a4 = a2[0:n_ext - 3, :] + a2[2:n_ext - 1, :]
    a8 = a4[0:n_ext - 7, :] + a4[4:n_ext - 3, :]
    a16 = a8[0:n_ext - 15, :] + a8[8:n_ext - 7, :]
    w2 = a2[7:7 + TOK_TILE, :]
    w4 = a4[6:6 + TOK_TILE, :]
    w8 = a8[4:4 + TOK_TILE, :]
    w16 = a16[0:TOK_TILE, :]
    lane_c = lax.broadcasted_iota(jnp.int32, (TOK_TILE, D_C), 1) // POOL_GROUP_DIM
    pooled = jnp.where(lane_c == 0, w2, jnp.where(lane_c == 1, w4, jnp.where(lane_c == 2, w8, w16)))
    half = jnp.where(lane_c == 0, 1, jnp.where(lane_c == 1, 2, jnp.where(lane_c == 2, 4, 8)))
    pos = i * TOK_TILE + lax.broadcasted_iota(jnp.int32, (TOK_TILE, D_C), 0)
    cnt = (jnp.minimum(pos + half, seq_len) - jnp.maximum(pos - half, 0)).astype(jnp.float32)
    d = (pooled / cnt - u).astype(jnp.bfloat16)
    oc = jnp.dot(d, poolw_ref[...], preferred_element_type=jnp.float32) * pools_ref[...]
    mix[:, D_A + D_B:D_MIX] = oc.astype(jnp.bfloat16)

    xn = xn_scr[...]
    logits = logit_scr[...]

    xm = x_ref[...] + jnp.dot(mix[...], wout_ref[...], preferred_element_type=jnp.float32)
    xmid_ref[...] = xm
    xn_new = _rmsnorm_f32(xm, g2_ref[...]).astype(jnp.bfloat16)
    xn_scr[...] = xn_new
    logit_scr[...] = jnp.dot(xn_new, rw_ref[...], preferred_element_type=jnp.float32) + rb_ref[...]

    lane = lax.broadcasted_iota(jnp.int32, (TOK_TILE, LANES), 1).astype(jnp.float32)
    is_g = lane < N_GROUPS
    gl = jnp.where(is_g, logits, NEG)
    gmax = jnp.max(gl, axis=-1, keepdims=True)
    gtop = jnp.min(jnp.where(is_g & (gl == gmax), lane, float(LANES)), axis=-1, keepdims=True)
    gprob = 1.0 / jnp.sum(jnp.exp(gl - gmax), axis=-1, keepdims=True)
    e_lo = ROUTE_LANE0 + gtop * EXPERTS_PER_GROUP
    in_grp = (lane >= e_lo) & (lane < e_lo + EXPERTS_PER_GROUP)
    el = jnp.where(in_grp, logits, NEG)
    m1 = jnp.max(el, axis=-1, keepdims=True)
    i1 = jnp.min(jnp.where(in_grp & (el == m1), lane, float(LANES)), axis=-1, keepdims=True)
    el2 = jnp.where(lane == i1, NEG, el)
    m2 = jnp.max(el2, axis=-1, keepdims=True)
    i2 = jnp.min(jnp.where(in_grp & (lane != i1) & (el2 == m2), lane, float(LANES)), axis=-1, keepdims=True)
    r21 = jnp.exp(m2 - m1)
    gate1 = gprob / (1.0 + r21)
    gate2 = gprob * r21 / (1.0 + r21)

    oh1 = lane == i1
    oh2 = lane == i2
    oh = jnp.where(oh1 | oh2, 1.0, 0.0)
    earlier = jnp.dot(tri_ref[...], oh.astype(jnp.bfloat16), preferred_element_type=jnp.float32)
    n_e = jnp.sum(oh, axis=0, keepdims=True)
    chunks_e = jnp.floor((n_e + (CHUNK - 1)) * (1.0 / CHUNK))
    seg0 = jnp.dot(jnp.broadcast_to(chunks_e, (SUBLANES, LANES)).astype(jnp.bfloat16), utri_ref[...],
                   preferred_element_type=jnp.float32)[0:1, :] * CHUNK
    base = earlier + seg0
    lp1 = jnp.sum(jnp.where(oh1, base, 0.0), axis=-1, keepdims=True)
    lp2 = jnp.sum(jnp.where(oh2, base, 0.0), axis=-1, keepdims=True)
    info = jnp.zeros((TOK_TILE, LANES), jnp.float32)
    for k, col in enumerate((lp1, lp2, gate1, gate2)):
        info = jnp.where(lane == k, col, info)
    rinfo_ref[...] = info
    cnt_ref[...] = jnp.broadcast_to(n_e, cnt_ref.shape)

    info_t = info.T
    prow = lax.broadcasted_iota(jnp.int32, (LOCAL_ROWS, TOK_TILE), 0).astype(jnp.float32)
    sel = jnp.where((prow == info_t[INFO_ROW1:INFO_ROW1 + 1, :]) | (prow == info_t[INFO_ROW2:INFO_ROW2 + 1, :]),
                    1.0, 0.0).astype(jnp.bfloat16)
    moved = jnp.dot(sel, xn, preferred_element_type=jnp.float32)
    xs_ref[...] = moved.astype(jnp.bfloat16)


def _mixer(layer, x2, qp, kv, layer_params, shared_tables, *, seq_len):
    nab, sinkcol, poolw, pools, wout, g2, rw, rb = layer_params
    swb, tri, utri = shared_tables
    T = x2.shape[0]
    nblk = seq_len // TOK_TILE
    n_tok_tiles = T // TOK_TILE

    def mixed(t):
        return jnp.minimum(t, n_tok_tiles - 1)

    def cur(t):
        return (mixed(t), 0)

    def window(t):
        seq0 = (mixed(t) // nblk) * nblk
        return (jnp.clip(mixed(t) - 1, seq0, seq0 + nblk - WIN_TILES) * TOK_TILE, 0)

    def dispatched(t):
        return (jnp.maximum(t - 1, 0), 0)

    def resident(a):
        zeros = (0,) * a.ndim
        return pl.BlockSpec(a.shape, lambda t: zeros, pipeline_mode=pl.Buffered(1))

    def resident_layer(a):
        index = (layer,) + (0,) * (a.ndim - 1)
        return pl.BlockSpec((None,) + a.shape[1:], lambda t: index, pipeline_mode=pl.Buffered(1))

    return pl.pallas_call(
        functools.partial(_mixer_kernel, seq_len=seq_len),
        grid=(n_tok_tiles + 1,),
        in_specs=[
            pl.BlockSpec((TOK_TILE, D_MODEL), cur),
            pl.BlockSpec((TOK_TILE, D_QP), cur),
            pl.BlockSpec((pl.Element(WIN_TILES * TOK_TILE), pl.Element(D_KV)), window),
            resident_layer(nab), resident(swb), resident_layer(sinkcol), resident_layer(poolw),
            resident_layer(pools), resident_layer(wout), resident_layer(g2), resident_layer(rw),
            resident_layer(rb), resident(tri), resident(utri),
        ],
        out_specs=[
            pl.BlockSpec((TOK_TILE, D_MODEL), cur),
            pl.BlockSpec((LOCAL_ROWS, D_MODEL), dispatched),
            pl.BlockSpec((TOK_TILE, LANES), dispatched),
            pl.BlockSpec((SUBLANES, LANES), dispatched),
        ],
        out_shape=[
            jax.ShapeDtypeStruct((T, D_MODEL), jnp.float32),
            jax.ShapeDtypeStruct((n_tok_tiles * LOCAL_ROWS, D_MODEL), jnp.bfloat16),
            jax.ShapeDtypeStruct((T, LANES), jnp.float32),
            jax.ShapeDtypeStruct((n_tok_tiles * SUBLANES, LANES), jnp.float32),
        ],
        scratch_shapes=[
            pltpu.VMEM((TOK_TILE + 2 * HALO, D_C), jnp.float32),
            pltpu.VMEM((TOK_TILE, D_MIX), jnp.bfloat16),
            pltpu.VMEM((TOK_TILE, D_MODEL), jnp.bfloat16),
            pltpu.VMEM((TOK_TILE, LANES), jnp.float32),
        ],
        compiler_params=pltpu.CompilerParams(
            dimension_semantics=("arbitrary",), vmem_limit_bytes=VMEM_LIMIT),
        name="mixer",
    )(x2, qp, kv, nab, swb, sinkcol, poolw, pools, wout, g2, rw, rb, tri, utri)


def _chunk_copy(src_hbm, src_chunk, dst, dst_chunk, sem):
    return pltpu.make_async_copy(
        src_hbm.at[pl.ds(pl.multiple_of(src_chunk * CHUNK, CHUNK), CHUNK)],
        dst.at[pl.ds(dst_chunk * CHUNK, CHUNK)],
        sem)


def _expert_kernel(te_ref, nsub_ref, first_ref, wslot_ref, nexte_ref, csrc_ref,
                   xs_hbm, wg_hbm, wu_hbm, wd_hbm, y_ref,
                   xbuf, wgf, wuf, wdf, wgb, wub, wdb, sem, wsem, *, layer):
    j = pl.program_id(0)
    nt = pl.num_programs(0)
    slot = j % GATHER_SLOTS

    def start_gather(tile, s, h):
        for c in range(h * SUB_CHUNKS, (h + 1) * SUB_CHUNKS):
            _chunk_copy(xs_hbm, csrc_ref[tile * TILE_CHUNKS + c], xbuf.at[s], c, sem.at[s, h]).start()

    def wait_gather(s, h):
        rows = pl.ds(h * EXP_SUB, EXP_SUB)
        pltpu.make_async_copy(xs_hbm.at[pl.ds(0, EXP_SUB)], xbuf.at[s, rows], sem.at[s, h]).wait()

    def weight_copies(expert, ws):
        return [pltpu.make_async_copy(w_hbm.at[layer, expert], wbuf.at[ws], wsem.at[k, ws])
                for k, (w_hbm, wbuf) in enumerate(((wg_hbm, wgf), (wu_hbm, wuf), (wd_hbm, wdf)))]

    for h in range(EXP_SUBS):
        for first_tile in range(GATHER_AHEAD):
            @pl.when((j == 0) & (h < nsub_ref[first_tile]))
            def _():
                start_gather(first_tile, first_tile, h)

        ahead = j + GATHER_AHEAD

        @pl.when(h < nsub_ref[jnp.minimum(ahead, nt - 1)] * (ahead < nt).astype(jnp.int32))
        def _():
            start_gather(ahead, ahead % GATHER_SLOTS, h)

    ws = wslot_ref[j]

    @pl.when(j == 0)
    def _():
        for cp in weight_copies(te_ref[0], 0):
            cp.start()

    @pl.when(first_ref[j] > 0)
    def _():
        for cp in weight_copies(te_ref[j], ws):
            cp.wait()

        @pl.when(nexte_ref[j] >= 0)
        def _():
            for cp in weight_copies(nexte_ref[j], 1 - ws):
                cp.start()

        wgb[...] = wgf[ws].astype(jnp.bfloat16)
        wub[...] = wuf[ws].astype(jnp.bfloat16)
        wdb[...] = wdf[ws].astype(jnp.bfloat16)

    def gated_mlp(n_rows):
        xs = xbuf[slot, 0:n_rows, :]
        gate = jnp.dot(xs, wgb[...], preferred_element_type=jnp.float32)
        up = jnp.dot(xs, wub[...], preferred_element_type=jnp.float32)
        act = (gate * (1.0 / (1.0 + jnp.exp(-gate))) * up).astype(jnp.bfloat16)
        return jnp.dot(act, wdb[...], preferred_element_type=jnp.float32).astype(jnp.bfloat16)

    for k in range(1, EXP_SUBS + 1):
        @pl.when(nsub_ref[j] == k)
        def _():
            for h in range(k):
                wait_gather(slot, h)
            y_ref[0:k * EXP_SUB, :] = gated_mlp(k * EXP_SUB)
            if k < EXP_SUBS:
                y_ref[k * EXP_SUB:EXP_TILE, :] = jnp.zeros((EXP_TILE - k * EXP_SUB, D_MODEL), jnp.bfloat16)

    @pl.when(nsub_ref[j] == 0)
    def _():
        y_ref[...] = jnp.zeros_like(y_ref)


def _experts(layer, tile_tables, chunk_src, xs_local, wg, wu, wd):
    tile_expert, n_sub, first, wslot, next_expert = tile_tables
    n_tiles = tile_expert.shape[0]
    any_space = pl.BlockSpec(memory_space=pl.ANY)

    return pl.pallas_call(
        functools.partial(_expert_kernel, layer=layer),
        grid_spec=pltpu.PrefetchScalarGridSpec(
            num_scalar_prefetch=6,
            grid=(n_tiles,),
            in_specs=[any_space] * 4,
            out_specs=pl.BlockSpec((EXP_TILE, D_MODEL), lambda j, *tables: (j, 0)),
            scratch_shapes=[
                pltpu.VMEM((GATHER_SLOTS, EXP_TILE, D_MODEL), jnp.bfloat16),
                pltpu.VMEM((2, D_MODEL, D_EXPERT), jnp.float32),
                pltpu.VMEM((2, D_MODEL, D_EXPERT), jnp.float32),
                pltpu.VMEM((2, D_EXPERT, D_MODEL), jnp.float32),
                pltpu.VMEM((D_MODEL, D_EXPERT), jnp.bfloat16),
                pltpu.VMEM((D_MODEL, D_EXPERT), jnp.bfloat16),
                pltpu.VMEM((D_EXPERT, D_MODEL), jnp.bfloat16),
                pltpu.SemaphoreType.DMA((GATHER_SLOTS, EXP_SUBS)),
                pltpu.SemaphoreType.DMA((3, 2)),
            ],
        ),
        out_shape=jax.ShapeDtypeStruct((n_tiles * EXP_TILE, D_MODEL), jnp.bfloat16),
        compiler_params=pltpu.CompilerParams(
            dimension_semantics=("arbitrary",), vmem_limit_bytes=VMEM_LIMIT),
        name="experts",
    )(tile_expert, n_sub, first, wslot, next_expert, chunk_src, xs_local, wg, wu, wd)


def _combine_tile(ctab_ref, xmid_ref, rinfo_ref, ys_hbm, ybuf, sem):
    i = pl.program_id(0)
    nt = pl.num_programs(0)
    slot = i % GATHER_SLOTS

    def start_gather(tile, s):
        for c in range(LOCAL_CHUNKS):
            _chunk_copy(ys_hbm, ctab_ref[tile * LOCAL_CHUNKS + c], ybuf.at[s], c, sem.at[s]).start()

    def wait_gather(s):
        pltpu.make_async_copy(ys_hbm.at[pl.ds(0, LOCAL_ROWS)], ybuf.at[s], sem.at[s]).wait()

    @pl.when(i == 0)
    def _():
        for first_tile in range(GATHER_AHEAD):
            start_gather(first_tile, first_tile)

    @pl.when(i + GATHER_AHEAD < nt)
    def _():
        start_gather(i + GATHER_AHEAD, (i + GATHER_AHEAD) % GATHER_SLOTS)

    wait_gather(slot)
    info = rinfo_ref[...]
    pcol = lax.broadcasted_iota(jnp.int32, (TOK_TILE, LOCAL_ROWS), 1).astype(jnp.float32)
    pick = jnp.where(pcol == info[:, INFO_ROW1:INFO_ROW1 + 1], info[:, INFO_GATE1:INFO_GATE1 + 1],
                     jnp.where(pcol == info[:, INFO_ROW2:INFO_ROW2 + 1], info[:, INFO_GATE2:INFO_GATE2 + 1], 0.0)
                     ).astype(jnp.bfloat16)
    return xmid_ref[...] + jnp.dot(pick, ybuf[slot], preferred_element_type=jnp.float32)


def _combine_final_kernel(ctab_ref, xmid_ref, rinfo_ref, g_ref, ys_hbm, out_ref, ybuf, sem):
    x = _combine_tile(ctab_ref, xmid_ref, rinfo_ref, ys_hbm, ybuf, sem)
    out_ref[...] = _rmsnorm_f32(x, g_ref[...])


def _combine_proj_kernel(ctab_ref, xmid_ref, rinfo_ref, g_ref, w_ref, ys_hbm, x_ref, qp_ref, kv_ref, ybuf, sem):
    x = _combine_tile(ctab_ref, xmid_ref, rinfo_ref, ys_hbm, ybuf, sem)
    x_ref[...] = x
    xn = _rmsnorm_f32(x, g_ref[...]).astype(jnp.bfloat16)
    proj = jnp.dot(xn, w_ref[...], preferred_element_type=jnp.float32)
    qp_ref[...] = proj[:, :D_QP].astype(jnp.bfloat16)
    kv_ref[...] = proj[:, D_QP:].astype(jnp.bfloat16)


def _combine_proj(next_layer, chunk_tab, xmid, rinfo, g1, w_in_k, ys):
    T = xmid.shape[0]

    def tile(i, ct):
        return (i, 0)

    return pl.pallas_call(
        _combine_proj_kernel,
        grid_spec=pltpu.PrefetchScalarGridSpec(
            num_scalar_prefetch=1,
            grid=(T // TOK_TILE,),
            in_specs=[
                pl.BlockSpec((TOK_TILE, D_MODEL), tile),
                pl.BlockSpec((TOK_TILE, LANES), tile),
                pl.BlockSpec((None, 1, D_MODEL), lambda i, ct: (next_layer, 0, 0)),
                pl.BlockSpec((None, D_MODEL, D_IN), lambda i, ct: (next_layer, 0, 0),
                             pipeline_mode=pl.Buffered(1)),
                pl.BlockSpec(memory_space=pl.ANY),
            ],
            out_specs=[
                pl.BlockSpec((TOK_TILE, D_MODEL), tile),
                pl.BlockSpec((TOK_TILE, D_QP), tile),
                pl.BlockSpec((TOK_TILE, D_KV), tile),
            ],
            scratch_shapes=[
                pltpu.VMEM((GATHER_SLOTS, LOCAL_ROWS, D_MODEL), jnp.bfloat16),
                pltpu.SemaphoreType.DMA((GATHER_SLOTS,)),
            ],
        ),
        out_shape=[
            jax.ShapeDtypeStruct((T, D_MODEL), jnp.float32),
            jax.ShapeDtypeStruct((T, D_QP), jnp.bfloat16),
            jax.ShapeDtypeStruct((T, D_KV), jnp.bfloat16),
        ],
        compiler_params=pltpu.CompilerParams(
            dimension_semantics=("arbitrary",), vmem_limit_bytes=VMEM_LIMIT),
        name="combine_proj",
    )(chunk_tab, xmid, rinfo, g1, w_in_k, ys)


def _combine_final(chunk_tab, xmid, rinfo, g, ys):
    T = xmid.shape[0]
    return pl.pallas_call(
        _combine_final_kernel,
        grid_spec=pltpu.PrefetchScalarGridSpec(
            num_scalar_prefetch=1,
            grid=(T // TOK_TILE,),
            in_specs=[
                pl.BlockSpec((TOK_TILE, D_MODEL), lambda i, ct: (i, 0)),
                pl.BlockSpec((TOK_TILE, LANES), lambda i, ct: (i, 0)),
                pl.BlockSpec((1, D_MODEL), lambda i, ct: (0, 0)),
                pl.BlockSpec(memory_space=pl.ANY),
            ],
            out_specs=pl.BlockSpec((TOK_TILE, D_MODEL), lambda i, ct: (i, 0)),
            scratch_shapes=[
                pltpu.VMEM((GATHER_SLOTS, LOCAL_ROWS, D_MODEL), jnp.bfloat16),
                pltpu.SemaphoreType.DMA((GATHER_SLOTS,)),
            ],
        ),
        out_shape=jax.ShapeDtypeStruct((T, D_MODEL), jnp.float32),
        compiler_params=pltpu.CompilerParams(
            dimension_semantics=("arbitrary",), vmem_limit_bytes=VMEM_LIMIT),
        name="combine",
    )(chunk_tab, xmid, rinfo, g, ys)


def _pair_heads(a, axis):
    shape = a.shape
    split = shape[:axis] + (SWA_KV_HEADS, SWA_REP, HEAD_DIM) + shape[axis + 1:]
    return jnp.swapaxes(a.reshape(split), axis, axis + 1).reshape(shape)


def _in_proj_weight(w):
    off_bq = 3 * D_A
    return jnp.concatenate(
        [w[..., 0:D_A] * QK_SCALE, _pair_heads(w[..., off_bq:off_bq + D_B], w.ndim - 1) * QK_SCALE,
         w[..., D_A:off_bq], w[..., off_bq + D_B:]], axis=-1).astype(jnp.bfloat16)


def _out_proj_weight(w):
    return jnp.concatenate(
        [w[:, 0:D_A], _pair_heads(w[:, D_A:D_A + D_B], 1), w[:, D_A + D_B:]], axis=1).astype(jnp.bfloat16)


def _na_bias_table(rel_bias):
    c = np.arange(GRID_W)[:, None]
    cp = np.arange(GRID_W)[None, :]
    cs = np.clip(c - NA_COLS // 2, 0, GRID_W - NA_COLS)
    valid = (cp >= cs) & (cp < cs + NA_COLS)
    d = np.arange(2 * NA_COLS - 1)[:, None, None]
    col_sel = ((cp - c + (NA_COLS - 1))[None] == d) & valid[None]
    k = np.arange(NA_ROWS)[:, None, None]
    j = np.arange(NA_ROWS)[None, :, None]
    r = np.arange(2 * NA_ROWS - 1)[None, None, :]
    row_sel = r == j - k + (NA_ROWS - 1)
    tab = jnp.einsum("kjr,lhrd,dcm->lkhcjm", jnp.asarray(row_sel, jnp.float32), rel_bias.astype(jnp.float32),
                     jnp.asarray(col_sel, jnp.float32), precision=lax.Precision.HIGHEST)
    tab = jnp.where(jnp.asarray(valid)[None, None, None, :, None, :], tab * LOG2E, NEG)
    return tab.reshape(rel_bias.shape[0], NA_ROWS, NA_HEADS * GRID_W, NA_ROWS * GRID_W)


def _swa_bias_table():
    slopes = (2.0 ** (-8.0 * np.arange(1, SWA_Q_HEADS + 1) / SWA_Q_HEADS)).astype(np.float32)
    qi = np.arange(SWA_BLOCK)[:, None]
    ki = np.arange(3 * SWA_BLOCK)[None, :]
    variants = []
    for v in range(3):
        dist = np.abs(ki - qi - v * SWA_BLOCK).astype(np.float32)
        tab = np.where(dist <= SWA_WINDOW, -slopes[:, None, None] * dist[None] * LOG2E, np.float32(NEG))
        variants.append(tab.reshape(SWA_Q_HEADS * SWA_BLOCK, 3 * SWA_BLOCK).astype(np.float32))
    return jnp.asarray(np.stack(variants))


def _block_diag(pool_w):
    depth, n = pool_w.shape[0:2]
    eye = jnp.asarray(np.eye(n, dtype=np.float32))
    out = pool_w[:, :, :, None, :] * eye[None, :, None, :, None]
    return out.reshape(depth, n * POOL_GROUP_DIM, n * POOL_GROUP_DIM)


def _router_weights(rg_w, rg_b, re_w, re_b):
    def lanes(g, e):
        gap = jnp.zeros(g.shape[:-1] + (ROUTE_LANE0 - N_GROUPS,), jnp.float32)
        tail = jnp.zeros(g.shape[:-1] + (LANES - ROUTE_LANE0 - N_EXPERTS,), jnp.float32)
        return jnp.concatenate([g.astype(jnp.float32), gap, e.astype(jnp.float32), tail], axis=-1)

    return lanes(rg_w, re_w).astype(jnp.bfloat16), lanes(rg_b, re_b)[:, None, :]


def _dispatch_tables(cnt, n_tiles):
    nb = cnt.shape[0] // SUBLANES
    n = cnt.reshape(nb, SUBLANES, LANES)[:, 0, ROUTE_LANE0:ROUTE_LANE0 + N_EXPERTS].astype(jnp.int32)
    g = (n + (CHUNK - 1)) // CHUNK
    l_end = jnp.cumsum(g, axis=1)
    l_off = l_end - g
    c_end = jnp.cumsum(g, axis=0)
    c_off = c_end - g
    tot = c_end[-1]
    tiles = (tot + (TILE_CHUNKS - 1)) // TILE_CHUNKS
    t_end = jnp.cumsum(tiles)
    t_off = t_end - tiles
    n_used = t_end[-1:]

    experts = jnp.arange(N_EXPERTS, dtype=jnp.int32)
    tile_ids = jnp.arange(n_tiles, dtype=jnp.int32)
    tile_expert = jnp.minimum(jnp.sum((t_end[None, :] <= tile_ids[:, None]).astype(jnp.int32), axis=1),
                              N_EXPERTS - 1)
    oh_te = (tile_expert[:, None] == experts[None, :]).astype(jnp.int32)
    left = jnp.sum(oh_te * (tot + t_off * TILE_CHUNKS)[None, :], axis=1) - tile_ids * TILE_CHUNKS
    n_sub = jnp.clip((left + (SUB_CHUNKS - 1)) // SUB_CHUNKS, 0, EXP_SUBS)
    has_rows = tiles > 0
    first = ((tile_ids == jnp.sum(oh_te * t_off[None, :], axis=1)) & (n_sub > 0)).astype(jnp.int32)
    wslot = jnp.sum(oh_te * ((jnp.cumsum(has_rows.astype(jnp.int32)) - 1) % 2)[None, :], axis=1)
    later = (experts[None, :] > experts[:, None]) & has_rows[None, :]
    nxt = jnp.min(jnp.where(later, experts[None, :], N_EXPERTS), axis=1)
    next_expert = jnp.sum(oh_te * jnp.where(nxt < N_EXPERTS, nxt, -1)[None, :], axis=1)
    tile_tables = (tile_expert, n_sub, first, wslot, next_expert)

    q = jnp.arange(n_tiles * TILE_CHUNKS, dtype=jnp.int32)
    tile_q = q // TILE_CHUNKS
    oh_e = (jnp.repeat(tile_expert, TILE_CHUNKS)[:, None] == experts[None, :]).astype(jnp.int32)
    ro = q - jnp.sum(oh_e * t_off[None, :], axis=1) * TILE_CHUNKS
    valid = (ro < jnp.sum(oh_e * tot[None, :], axis=1)) & (tile_q < n_used[0])
    cols = jnp.dot(jnp.concatenate([c_end, c_off, l_off], axis=0).astype(jnp.float32),
                   oh_e.T.astype(jnp.float32), precision=lax.Precision.HIGHEST).astype(jnp.int32)
    c_end_q, c_off_q, l_off_q = cols[0:nb], cols[nb:2 * nb], cols[2 * nb:3 * nb]
    b_q = jnp.minimum(jnp.sum((c_end_q <= ro[None, :]).astype(jnp.int32), axis=0), nb - 1)
    oh_b = (jnp.arange(nb, dtype=jnp.int32)[:, None] == b_q[None, :]).astype(jnp.int32)
    src = b_q * LOCAL_CHUNKS + jnp.sum(oh_b * (l_off_q + ro[None, :] - c_off_q), axis=0)
    chunk_src = jnp.where(valid, src, LOCAL_CHUNKS - 1)

    c = jnp.arange(LOCAL_CHUNKS, dtype=jnp.int32)
    e_c = jnp.minimum(jnp.sum((l_end[:, None, :] <= c[None, :, None]).astype(jnp.int32), axis=2),
                      N_EXPERTS - 1)
    oh_ec = (e_c[:, :, None] == experts[None, None, :]).astype(jnp.int32)
    pos = (jnp.sum(oh_ec * (t_off[None, None, :] * TILE_CHUNKS + c_off[:, None, :] - l_off[:, None, :]), axis=2)
           + c[None, :])
    chunk_tab = jnp.where(c[None, :] < l_end[:, -1:], pos, 0).reshape(-1)
    return tile_tables, chunk_src, chunk_tab


def kernel(x, norm1_g, w_in, nat_bias, swa_sink, pool_w, pool_scale, w_out, norm2_g, router_g_w,
           router_g_b, router_e_w, router_e_b, expert_w_gate, expert_w_up, expert_w_down, final_g):
    batch, seq_len, _ = x.shape
    depth = w_in.shape[0]
    T = batch * seq_len
    assert seq_len % TOK_TILE == 0 and TOK_TILE % SWA_BLOCK == 0 and TOK_TILE % GRID_W == 0
    max_chunks = (2 * T) // CHUNK + (T // TOK_TILE) * N_EXPERTS
    n_tiles = max_chunks // TILE_CHUNKS + N_EXPERTS

    swb = _swa_bias_table()
    tri = jnp.asarray(np.tril(np.ones((TOK_TILE, TOK_TILE), np.float32), -1)).astype(jnp.bfloat16)
    utri = jnp.asarray(np.triu(np.ones((LANES, LANES), np.float32), 1)).astype(jnp.bfloat16)

    w_in_k = _in_proj_weight(w_in)
    w_out_k = _out_proj_weight(w_out)
    nab = _na_bias_table(nat_bias)
    sinkcol = jnp.broadcast_to((swa_sink.astype(jnp.float32) * LOG2E)[:, :, None, None],
                               (depth, SWA_Q_HEADS, SWA_BLOCK, LANES)).reshape(depth, SWA_Q_HEADS * SWA_BLOCK, LANES)
    poolw = _block_diag(pool_w).astype(jnp.bfloat16)
    pools = pool_scale.reshape(depth, 1, D_C).astype(jnp.float32)
    rw, rb = _router_weights(router_g_w, router_g_b, router_e_w, router_e_b)
    g1 = norm1_g.reshape(depth, 1, D_MODEL)
    g2 = norm2_g.reshape(depth, 1, D_MODEL)

    x2 = x.reshape(T, D_MODEL)
    qp, kv = _norm_proj(0, x2, g1, w_in_k)
    for l in range(depth):
        xmid, xs_local, rinfo, cnt = _mixer(
            l, x2, qp, kv, (nab, sinkcol, poolw, pools, w_out_k, g2, rw, rb), (swb, tri, utri),
            seq_len=seq_len)
        tile_tables, chunk_src, chunk_tab = _dispatch_tables(cnt, n_tiles)
        ys = _experts(l, tile_tables, chunk_src, xs_local,
                      expert_w_gate, expert_w_up, expert_w_down)
        if l + 1 < depth:
            x2, qp, kv = _combine_proj(l + 1, chunk_tab, xmid, rinfo, g1, w_in_k, ys)
        else:
            x2 = _combine_final(chunk_tab, xmid, rinfo, final_g.reshape(1, D_MODEL), ys)
    return x2.reshape(batch, seq_len, D_MODEL)
```

```python
import functools

import jax
import jax.numpy as jnp
import numpy as np
from jax import lax
from jax.experimental import pallas as pl
from jax.experimental.pallas import tpu as pltpu

D_MODEL = 1024
GRID_W = 64
HEAD_DIM = 64
NA_HEADS = 4
NA_ROWS = 8
NA_COLS = 16
SWA_Q_HEADS = 8
SWA_KV_HEADS = 2
SWA_REP = SWA_Q_HEADS // SWA_KV_HEADS
SWA_WINDOW = 128
SWA_BLOCK = 128
POOL_WINDOWS = (2, 4, 8, 16)
POOL_GROUP_DIM = 64
D_A = NA_HEADS * HEAD_DIM
D_B = SWA_Q_HEADS * HEAD_DIM
D_BKV = SWA_KV_HEADS * HEAD_DIM
D_C = len(POOL_WINDOWS) * POOL_GROUP_DIM
D_MIX = D_A + D_B + D_C
D_QP = D_A + D_B
D_KV = 2 * D_A + 2 * D_BKV + D_C
D_IN = D_QP + D_KV
N_GROUPS = 4
EXPERTS_PER_GROUP = 8
N_EXPERTS = N_GROUPS * EXPERTS_PER_GROUP
D_EXPERT = 256
RMS_EPS = 1e-6
NEG = -1e30
LOG2E = 1.4426950408889634
QK_SCALE = HEAD_DIM ** -0.5 * LOG2E

LANES = 128
SUBLANES = 8
ROW_CHUNKS = D_MODEL // LANES

TOK_TILE = 512
EXP_SUB = 256
EXP_SUBS = 4
EXP_TILE = EXP_SUB * EXP_SUBS
CHUNK = 16
LOCAL_CHUNKS = (2 * TOK_TILE + N_EXPERTS * (CHUNK - 1)) // CHUNK + 2
LOCAL_ROWS = LOCAL_CHUNKS * CHUNK
SUB_CHUNKS = EXP_SUB // CHUNK
TILE_CHUNKS = EXP_TILE // CHUNK
GATHER_AHEAD = 2
GATHER_SLOTS = GATHER_AHEAD + 1
ROUTE_LANE0 = 8
INFO_ROW1, INFO_ROW2, INFO_GATE1, INFO_GATE2 = 0, 1, 2, 3
NA_ROWS_PER_STEP = 8
SWA_BLOCKS_PER_STEP = 4
HALO = 8
WIN_TILES = 3
VMEM_LIMIT = 56 * 1024 * 1024


def _rmsnorm_f32(x, g):
    return x * lax.rsqrt(jnp.mean(x * x, axis=-1, keepdims=True) + RMS_EPS) * g


def _norm_proj_kernel(x_ref, g_ref, w_ref, qp_ref, kv_ref):
    xn = _rmsnorm_f32(x_ref[...], g_ref[...]).astype(jnp.bfloat16)
    proj = jnp.dot(xn, w_ref[...], preferred_element_type=jnp.float32)
    qp_ref[...] = proj[:, :D_QP].astype(jnp.bfloat16)
    kv_ref[...] = proj[:, D_QP:].astype(jnp.bfloat16)


def _norm_proj(layer, x2, g, w):
    T = x2.shape[0]
    return pl.pallas_call(
        _norm_proj_kernel,
        grid=(T // TOK_TILE,),
        in_specs=[
            pl.BlockSpec((TOK_TILE, D_MODEL), lambda i: (i, 0)),
            pl.BlockSpec((None, 1, D_MODEL), lambda i: (layer, 0, 0)),
            pl.BlockSpec((None, D_MODEL, D_IN), lambda i: (layer, 0, 0)),
        ],
        out_specs=[
            pl.BlockSpec((TOK_TILE, D_QP), lambda i: (i, 0)),
            pl.BlockSpec((TOK_TILE, D_KV), lambda i: (i, 0)),
        ],
        out_shape=[
            jax.ShapeDtypeStruct((T, D_QP), jnp.bfloat16),
            jax.ShapeDtypeStruct((T, D_KV), jnp.bfloat16),
        ],
        compiler_params=pltpu.CompilerParams(
            dimension_semantics=("arbitrary",), vmem_limit_bytes=VMEM_LIMIT),
        name="norm_proj",
    )(x2, g, w)


KW_AK, KW_AV, KW_BK, KW_BV = 0, D_A, 2 * D_A, 2 * D_A + D_BKV
KW_COLS = 2 * D_A + 2 * D_BKV
KV_CU = KW_COLS


def _mixer_kernel(x_ref, qp_ref, kwin_ref, nab_ref, swb_ref, sink_ref,
                  poolw_ref, pools_ref, wout_ref, g2_ref, rw_ref, rb_ref, tri_ref, utri_ref,
                  xmid_ref, xs_ref, rinfo_ref, cnt_ref,
                  uwin, mix, xn_scr, logit_scr, *, seq_len):
    t = pl.program_id(0)
    nblk = seq_len // TOK_TILE
    i = jnp.minimum(t, pl.num_programs(0) - 2) % nblk
    rows_per_tile = TOK_TILE // GRID_W
    grid_rows = seq_len // GRID_W

    @pl.when(t == 0)
    def _():
        xn_scr[...] = jnp.zeros_like(xn_scr)
        logit_scr[...] = jnp.zeros_like(logit_scr)

    cur_off = pl.multiple_of((i - jnp.clip(i - 1, 0, nblk - WIN_TILES)) * TOK_TILE, TOK_TILE)

    lane_a = lax.broadcasted_iota(jnp.int32, (GRID_W, D_A), 1) // HEAD_DIM

    def na_row(rr):
        r = i * rows_per_tile + rr
        rs = jnp.clip(r - NA_ROWS // 2, 0, grid_rows - NA_ROWS)
        variant = r - rs
        start = pl.multiple_of(cur_off + (rs - i * rows_per_tile) * GRID_W, GRID_W)
        q0 = pl.multiple_of(rr * GRID_W, GRID_W)
        q = qp_ref[pl.ds(q0, GRID_W), 0:D_A]
        zero = jnp.zeros_like(q)
        qs = jnp.concatenate([jnp.where(lane_a == h, q, zero) for h in range(NA_HEADS)], axis=0)
        kw = kwin_ref[pl.ds(start, NA_ROWS * GRID_W), KW_AK:KW_AK + D_A]
        vw = kwin_ref[pl.ds(start, NA_ROWS * GRID_W), KW_AV:KW_AV + D_A]
        s = lax.dot_general(qs, kw, (((1,), (1,)), ((), ())), preferred_element_type=jnp.float32)
        s = s + nab_ref[variant]
        m = jnp.max(s, axis=-1, keepdims=True)
        p = jnp.exp2(s - m)
        l = jnp.sum(p, axis=-1, keepdims=True)
        pv = jnp.dot(p.astype(jnp.bfloat16), vw, preferred_element_type=jnp.float32)
        pv = pv * (1.0 / l)
        o = jnp.zeros((GRID_W, D_A), jnp.float32)
        for h in range(NA_HEADS):
            o = o + jnp.where(lane_a == h, pv[h * GRID_W:(h + 1) * GRID_W, :], 0.0)
        mix[pl.ds(q0, GRID_W), 0:D_A] = o.astype(jnp.bfloat16)

    def na_step(it, c):
        for k in range(NA_ROWS_PER_STEP):
            na_row(it * NA_ROWS_PER_STEP + k)
        return c

    lax.fori_loop(0, rows_per_tile // NA_ROWS_PER_STEP, na_step, 0)

    lane_b = lax.broadcasted_iota(jnp.int32, (SWA_BLOCK, LANES), 1) // HEAD_DIM
    blocks_per_tile = TOK_TILE // SWA_BLOCK
    nblocks = seq_len // SWA_BLOCK

    ones_v = jnp.ones((3 * SWA_BLOCK, LANES), jnp.bfloat16)

    def swa_step(sb, c):
        n = i * blocks_per_tile + sb
        variant = jnp.where(n == 0, 0, jnp.where(n == nblocks - 1, 2, 1))
        q0 = pl.multiple_of(sb * SWA_BLOCK, SWA_BLOCK)
        k0 = pl.multiple_of(cur_off + (sb - variant) * SWA_BLOCK, SWA_BLOCK)
        kw = kwin_ref[pl.ds(k0, 3 * SWA_BLOCK), KW_BK:KW_BK + D_BKV]
        vaug = jnp.concatenate([kwin_ref[pl.ds(k0, 3 * SWA_BLOCK), KW_BV:KW_BV + D_BKV], ones_v], axis=1)
        outs = []
        for g in range(SWA_KV_HEADS):
            pieces = []
            for t in range(SWA_REP):
                qt = qp_ref[pl.ds(q0, SWA_BLOCK), D_A + t * LANES:D_A + (t + 1) * LANES]
                pieces.append(jnp.where(lane_b == g, qt, jnp.zeros_like(qt)))
            qs = jnp.concatenate(pieces, axis=0)
            r0 = g * SWA_REP * SWA_BLOCK
            s = lax.dot_general(qs, kw, (((1,), (1,)), ((), ())), preferred_element_type=jnp.float32)
            s = s + swb_ref[variant, r0:r0 + SWA_REP * SWA_BLOCK, :]
            sink = sink_ref[r0:r0 + SWA_REP * SWA_BLOCK, :]
            m = jnp.broadcast_to(jnp.max(s, axis=-1, keepdims=True), sink.shape)
            m = jnp.maximum(m, sink)
            p = jnp.exp2(s - jnp.concatenate([m, m, m], axis=1)).astype(jnp.bfloat16)
            pv = jnp.dot(p, vaug, preferred_element_type=jnp.float32)
            l = pv[:, LANES:2 * LANES] + jnp.exp2(sink - m)
            outs.append(pv[:, 0:LANES] * (1.0 / l))
        for t in range(SWA_REP):
            o0 = outs[0][t * SWA_BLOCK:(t + 1) * SWA_BLOCK, :]
            o1 = outs[1][t * SWA_BLOCK:(t + 1) * SWA_BLOCK, :]
            ot = jnp.where(lane_b == 0, o0, o1)
            mix[pl.ds(q0, SWA_BLOCK), D_A + t * LANES:D_A + (t + 1) * LANES] = ot.astype(jnp.bfloat16)
        return c

    def swa_pair(it, c):
        for k in range(SWA_BLOCKS_PER_STEP):
            swa_step(it * SWA_BLOCKS_PER_STEP + k, c)
        return c

    lax.fori_loop(0, blocks_per_tile // SWA_BLOCKS_PER_STEP, swa_pair, 0)

    u = kwin_ref[pl.ds(cur_off, TOK_TILE), KV_CU:KV_CU + D_C].astype(jnp.float32)
    prev_ok = (i > 0).astype(jnp.float32)
    next_ok = (i < nblk - 1).astype(jnp.float32)
    before = pl.multiple_of(jnp.maximum(cur_off - CHUNK, 0), CHUNK)
    after = pl.multiple_of(jnp.minimum(cur_off + TOK_TILE, WIN_TILES * TOK_TILE - CHUNK), CHUNK)
    u_before = kwin_ref[pl.ds(before, CHUNK), KV_CU:KV_CU + D_C].astype(jnp.float32)
    u_after = kwin_ref[pl.ds(after, CHUNK), KV_CU:KV_CU + D_C].astype(jnp.float32)
    uwin[0:HALO, :] = u_before[CHUNK - HALO:CHUNK, :] * prev_ok
    uwin[HALO:HALO + TOK_TILE, :] = u
    uwin[HALO + TOK_TILE:2 * HALO + TOK_TILE, :] = u_after[0:HALO, :] * next_ok
    n_ext = TOK_TILE + 2 * HALO
    a2 = uwin[0:n_ext - 1, :] + uwin[1:n_ext, :]
    a4 = a2[0:n_ext - 3, :] + a2[2:n_ext - 1, :]
    a8 = a4[0:n_ext - 7, :] + a4[4:n_ext - 3, :]
    a16 = a8[0:n_ext - 15, :] + a8[8:n_ext - 7, :]
    w2 = a2[7:7 + TOK_TILE, :]
    w4 = a4[6:6 + TOK_TILE, :]
    w8 = a8[4:4 + TOK_TILE, :]
    w16 = a16[0:TOK_TILE, :]
    lane_c = lax.broadcasted_iota(jnp.int32, (TOK_TILE, D_C), 1) // POOL_GROUP_DIM
    pooled = jnp.where(lane_c == 0, w2, jnp.where(lane_c == 1, w4, jnp.where(lane_c == 2, w8, w16)))
    half = jnp.where(lane_c == 0, 1, jnp.where(lane_c == 1, 2, jnp.where(lane_c == 2, 4, 8)))
    pos = i * TOK_TILE + lax.broadcasted_iota(jnp.int32, (TOK_TILE, D_C), 0)
    cnt = (jnp.minimum(pos + half, seq_len) - jnp.maximum(pos - half, 0)).astype(jnp.float32)
    d = (pooled / cnt - u).astype(jnp.bfloat16)
    oc = jnp.dot(d, poolw_ref[...], preferred_element_type=jnp.float32) * pools_ref[...]
    mix[:, D_A + D_B:D_MIX] = oc.astype(jnp.bfloat16)

    xn = xn_scr[...]
    logits = logit_scr[...]

    xm = x_ref[...] + jnp.dot(mix[...], wout_ref[...], preferred_element_type=jnp.float32)
    xmid_ref[...] = xm
    xn_new = _rmsnorm_f32(xm, g2_ref[...]).astype(jnp.bfloat16)
    xn_scr[...] = xn_new
    logit_scr[...] = jnp.dot(xn_new, rw_ref[...], preferred_element_type=jnp.float32) + rb_ref[...]

    lane = lax.broadcasted_iota(jnp.int32, (TOK_TILE, LANES), 1).astype(jnp.float32)
    is_g = lane < N_GROUPS
    gl = jnp.where(is_g, logits, NEG)
    gmax = jnp.max(gl, axis=-1, keepdims=True)
    gtop = jnp.min(jnp.where(is_g & (gl == gmax), lane, float(LANES)), axis=-1, keepdims=True)
    gprob = 1.0 / jnp.sum(jnp.exp(gl - gmax), axis=-1, keepdims=True)
    e_lo = ROUTE_LANE0 + gtop * EXPERTS_PER_GROUP
    in_grp = (lane >= e_lo) & (lane < e_lo + EXPERTS_PER_GROUP)
    el = jnp.where(in_grp, logits, NEG)
    m1 = jnp.max(el, axis=-1, keepdims=True)
    i1 = jnp.min(jnp.where(in_grp & (el == m1), lane, float(LANES)), axis=-1, keepdims=True)
    el2 = jnp.where(lane == i1, NEG, el)
    m2 = jnp.max(el2, axis=-1, keepdims=True)
    i2 = jnp.min(jnp.where(in_grp & (lane != i1) & (el2 == m2), lane, float(LANES)), axis=-1, keepdims=True)
    r21 = jnp.exp(m2 - m1)
    gate1 = gprob / (1.0 + r21)
    gate2 = gprob * r21 / (1.0 + r21)

    oh1 = lane == i1
    oh2 = lane == i2
    oh = jnp.where(oh1 | oh2, 1.0, 0.0)
    earlier = jnp.dot(tri_ref[...], oh.astype(jnp.bfloat16), preferred_element_type=jnp.float32)
    n_e = jnp.sum(oh, axis=0, keepdims=True)
    chunks_e = jnp.floor((n_e + (CHUNK - 1)) * (1.0 / CHUNK))
    seg0 = jnp.dot(jnp.broadcast_to(chunks_e, (SUBLANES, LANES)).astype(jnp.bfloat16), utri_ref[...],
                   preferred_element_type=jnp.float32)[0:1, :] * CHUNK
    base = earlier + seg0
    lp1 = jnp.sum(jnp.where(oh1, base, 0.0), axis=-1, keepdims=True)
    lp2 = jnp.sum(jnp.where(oh2, base, 0.0), axis=-1, keepdims=True)
    info = jnp.zeros((TOK_TILE, LANES), jnp.float32)
    for k, col in enumerate((lp1, lp2, gate1, gate2)):
        info = jnp.where(lane == k, col, info)
    rinfo_ref[...] = info
    cnt_ref[...] = jnp.broadcast_to(n_e, cnt_ref.shape)

    info_t = info.T
    prow = lax.broadcasted_iota(jnp.int32, (LOCAL_ROWS, TOK_TILE), 0).astype(jnp.float32)
    sel = jnp.where((prow == info_t[INFO_ROW1:INFO_ROW1 + 1, :]) | (prow == info_t[INFO_ROW2:INFO_ROW2 + 1, :]),
                    1.0, 0.0).astype(jnp.bfloat16)
    moved = jnp.dot(sel, xn, preferred_element_type=jnp.float32)
    xs_ref[...] = moved.astype(jnp.bfloat16)


def _mixer(layer, x2, qp, kv, layer_params, shared_tables, *, seq_len):
    nab, sinkcol, poolw, pools, wout, g2, rw, rb = layer_params
    swb, tri, utri = shared_tables
    T = x2.shape[0]
    nblk = seq_len // TOK_TILE
    n_tok_tiles = T // TOK_TILE

    def mixed(t):
        return jnp.minimum(t, n_tok_tiles - 1)

    def cur(t):
        return (mixed(t), 0)

    def window(t):
        seq0 = (mixed(t) // nblk) * nblk
        return (jnp.clip(mixed(t) - 1, seq0, seq0 + nblk - WIN_TILES) * TOK_TILE, 0)

    def dispatched(t):
        return (jnp.maximum(t - 1, 0), 0)

    def resident(a):
        zeros = (0,) * a.ndim
        return pl.BlockSpec(a.shape, lambda t: zeros, pipeline_mode=pl.Buffered(1))

    def resident_layer(a):
        index = (layer,) + (0,) * (a.ndim - 1)
        return pl.BlockSpec((None,) + a.shape[1:], lambda t: index, pipeline_mode=pl.Buffered(1))

    return pl.pallas_call(
        functools.partial(_mixer_kernel, seq_len=seq_len),
        grid=(n_tok_tiles + 1,),
        in_specs=[
            pl.BlockSpec((TOK_TILE, D_MODEL), cur),
            pl.BlockSpec((TOK_TILE, D_QP), cur),
            pl.BlockSpec((pl.Element(WIN_TILES * TOK_TILE), pl.Element(D_KV)), window),
            resident_layer(nab), resident(swb), resident_layer(sinkcol), resident_layer(poolw),
            resident_layer(pools), resident_layer(wout), resident_layer(g2), resident_layer(rw),
            resident_layer(rb), resident(tri), resident(utri),
        ],
        out_specs=[
            pl.BlockSpec((TOK_TILE, D_MODEL), cur),
            pl.BlockSpec((LOCAL_ROWS, D_MODEL), dispatched),
            pl.BlockSpec((TOK_TILE, LANES), dispatched),
            pl.BlockSpec((SUBLANES, LANES), dispatched),
        ],
        out_shape=[
            jax.ShapeDtypeStruct((T, D_MODEL), jnp.float32),
            jax.ShapeDtypeStruct((n_tok_tiles * LOCAL_ROWS, D_MODEL), jnp.bfloat16),
            jax.ShapeDtypeStruct((T, LANES), jnp.float32),
            jax.ShapeDtypeStruct((n_tok_tiles * SUBLANES, LANES), jnp.float32),
        ],
        scratch_shapes=[
            pltpu.VMEM((TOK_TILE + 2 * HALO, D_C), jnp.float32),
            pltpu.VMEM((TOK_TILE, D_MIX), jnp.bfloat16),
            pltpu.VMEM((TOK_TILE, D_MODEL), jnp.bfloat16),
            pltpu.VMEM((TOK_TILE, LANES), jnp.float32),
        ],
        compiler_params=pltpu.CompilerParams(
            dimension_semantics=("arbitrary",), vmem_limit_bytes=VMEM_LIMIT),
        name="mixer",
    )(x2, qp, kv, nab, swb, sinkcol, poolw, pools, wout, g2, rw, rb, tri, utri)


def _chunk_copy(src_hbm, src_chunk, dst, dst_chunk, sem):
    return pltpu.make_async_copy(
        src_hbm.at[pl.ds(pl.multiple_of(src_chunk * CHUNK, CHUNK), CHUNK)],
        dst.at[pl.ds(dst_chunk * CHUNK, CHUNK)],
        sem)


def _expert_kernel(te_ref, nfull_ref, npart_ref, first_ref, wslot_ref, nexte_ref, csrc_ref,
                   xs_hbm, wg_hbm, wu_hbm, wd_hbm, ys_hbm,
                   xbuf, obuf, wgf, wuf, wdf, wgb, wub, wdb, sem, osem, psem, wsem, *, layer):
    j = pl.program_id(0)
    nt = pl.num_programs(0)
    slot = j % GATHER_SLOTS
    oslot = j % 2

    def nsub(tile):
        return nfull_ref[tile] + (npart_ref[tile] > 0).astype(jnp.int32)

    def scatter_full(tile, os, h):
        return [pltpu.make_async_copy(
            obuf.at[os, pl.ds(c * CHUNK, CHUNK)],
            ys_hbm.at[pl.ds(pl.multiple_of(csrc_ref[tile * TILE_CHUNKS + c] * CHUNK, CHUNK), CHUNK)],
            osem.at[os, h]) for c in range(h * SUB_CHUNKS, (h + 1) * SUB_CHUNKS)]

    def scatter_part(tile, os, c):
        idx = nfull_ref[tile] * SUB_CHUNKS + c
        return pltpu.make_async_copy(
            obuf.at[os, pl.ds(pl.multiple_of(idx * CHUNK, CHUNK), CHUNK)],
            ys_hbm.at[pl.ds(pl.multiple_of(csrc_ref[tile * TILE_CHUNKS + idx] * CHUNK, CHUNK), CHUNK)],
            psem.at[os])

    def start_scatters(tile, os):
        for h in range(EXP_SUBS):
            @pl.when(h < nfull_ref[tile])
            def _():
                for cp in scatter_full(tile, os, h):
                    cp.start()

        @pl.when(npart_ref[tile] > 0)
        def _():
            for c in range(SUB_CHUNKS):
                @pl.when(c < npart_ref[tile])
                def _():
                    scatter_part(tile, os, c).start()

    def wait_scatters(tile, os):
        for h in range(EXP_SUBS):
            @pl.when(h < nfull_ref[tile])
            def _():
                pltpu.make_async_copy(obuf.at[os, pl.ds(0, EXP_SUB)], ys_hbm.at[pl.ds(0, EXP_SUB)],
                                      osem.at[os, h]).wait()

        @pl.when(npart_ref[tile] > 0)
        def _():
            for c in range(SUB_CHUNKS):
                @pl.when(c < npart_ref[tile])
                def _():
                    scatter_part(tile, os, c).wait()

    def start_gather(tile, s, h):
        for c in range(h * SUB_CHUNKS, (h + 1) * SUB_CHUNKS):
            _chunk_copy(xs_hbm, csrc_ref[tile * TILE_CHUNKS + c], xbuf.at[s], c, sem.at[s, h]).start()

    def wait_gather(s, h):
        rows = pl.ds(h * EXP_SUB, EXP_SUB)
        pltpu.make_async_copy(xs_hbm.at[pl.ds(0, EXP_SUB)], xbuf.at[s, rows], sem.at[s, h]).wait()

    def weight_copies(expert, ws):
        return [pltpu.make_async_copy(w_hbm.at[layer, expert], wbuf.at[ws], wsem.at[k, ws])
                for k, (w_hbm, wbuf) in enumerate(((wg_hbm, wgf), (wu_hbm, wuf), (wd_hbm, wdf)))]

    for h in range(EXP_SUBS):
        for first_tile in range(GATHER_AHEAD):
            @pl.when((j == 0) & (h < nsub(first_tile)))
            def _():
                start_gather(first_tile, first_tile, h)

        ahead = j + GATHER_AHEAD

        @pl.when(h < nsub(jnp.minimum(ahead, nt - 1)) * (ahead < nt).astype(jnp.int32))
        def _():
            start_gather(ahead, ahead % GATHER_SLOTS, h)

    ws = wslot_ref[j]

    @pl.when(j == 0)
    def _():
        for cp in weight_copies(te_ref[0], 0):
            cp.start()

    @pl.when(first_ref[j] > 0)
    def _():
        for cp in weight_copies(te_ref[j], ws):
            cp.wait()

        @pl.when(nexte_ref[j] >= 0)
        def _():
            for cp in weight_copies(nexte_ref[j], 1 - ws):
                cp.start()

        wgb[...] = wgf[ws].astype(jnp.bfloat16)
        wub[...] = wuf[ws].astype(jnp.bfloat16)
        wdb[...] = wdf[ws].astype(jnp.bfloat16)

    def gated_mlp(n_rows):
        xs = xbuf[slot, 0:n_rows, :]
        gate = jnp.dot(xs, wgb[...], preferred_element_type=jnp.float32)
        up = jnp.dot(xs, wub[...], preferred_element_type=jnp.float32)
        act = (gate * (1.0 / (1.0 + jnp.exp(-gate))) * up).astype(jnp.bfloat16)
        return jnp.dot(act, wdb[...], preferred_element_type=jnp.float32).astype(jnp.bfloat16)

    @pl.when(j >= 2)
    def _():
        wait_scatters(j - 2, oslot)

    for k in range(1, EXP_SUBS + 1):
        @pl.when(nsub(j) == k)
        def _():
            for h in range(k):
                wait_gather(slot, h)
            obuf[oslot, 0:k * EXP_SUB, :] = gated_mlp(k * EXP_SUB)

    start_scatters(j, oslot)

    @pl.when(j == nt - 1)
    def _():
        wait_scatters(j - 1, 1 - oslot)
        wait_scatters(j, oslot)


def _experts(layer, tile_tables, chunk_src, xs_local, wg, wu, wd):
    tile_expert, n_full, n_part, first, wslot, next_expert = tile_tables
    n_tiles = tile_expert.shape[0]
    any_space = pl.BlockSpec(memory_space=pl.ANY)
    n_prefetch = 7

    return pl.pallas_call(
        functools.partial(_expert_kernel, layer=layer),
        grid_spec=pltpu.PrefetchScalarGridSpec(
            num_scalar_prefetch=n_prefetch,
            grid=(n_tiles,),
            in_specs=[any_space] * 4,
            out_specs=any_space,
            scratch_shapes=[
                pltpu.VMEM((GATHER_SLOTS, EXP_TILE, D_MODEL), jnp.bfloat16),
                pltpu.VMEM((2, EXP_TILE, D_MODEL), jnp.bfloat16),
                pltpu.VMEM((2, D_MODEL, D_EXPERT), jnp.float32),
                pltpu.VMEM((2, D_MODEL, D_EXPERT), jnp.float32),
                pltpu.VMEM((2, D_EXPERT, D_MODEL), jnp.float32),
                pltpu.VMEM((D_MODEL, D_EXPERT), jnp.bfloat16),
                pltpu.VMEM((D_MODEL, D_EXPERT), jnp.bfloat16),
                pltpu.VMEM((D_EXPERT, D_MODEL), jnp.bfloat16),
                pltpu.SemaphoreType.DMA((GATHER_SLOTS, EXP_SUBS)),
                pltpu.SemaphoreType.DMA((2, EXP_SUBS)),
                pltpu.SemaphoreType.DMA((2,)),
                pltpu.SemaphoreType.DMA((3, 2)),
            ],
        ),
        out_shape=jax.ShapeDtypeStruct(xs_local.shape, xs_local.dtype),
        input_output_aliases={n_prefetch: 0},
        compiler_params=pltpu.CompilerParams(
            dimension_semantics=("arbitrary",), vmem_limit_bytes=VMEM_LIMIT),
        name="experts",
    )(tile_expert, n_full, n_part, first, wslot, next_expert, chunk_src, xs_local, wg, wu, wd)


def _combine_tile(xmid_ref, rinfo_ref, ys_ref):
    info = rinfo_ref[...]
    pcol = lax.broadcasted_iota(jnp.int32, (TOK_TILE, LOCAL_ROWS), 1).astype(jnp.float32)
    pick = jnp.where(pcol == info[:, INFO_ROW1:INFO_ROW1 + 1], info[:, INFO_GATE1:INFO_GATE1 + 1],
                     jnp.where(pcol == info[:, INFO_ROW2:INFO_ROW2 + 1], info[:, INFO_GATE2:INFO_GATE2 + 1], 0.0)
                     ).astype(jnp.bfloat16)
    return xmid_ref[...] + jnp.dot(pick, ys_ref[...], preferred_element_type=jnp.float32)


def _combine_final_kernel(xmid_ref, rinfo_ref, ys_ref, g_ref, out_ref):
    x = _combine_tile(xmid_ref, rinfo_ref, ys_ref)
    out_ref[...] = _rmsnorm_f32(x, g_ref[...])


def _combine_proj_kernel(xmid_ref, rinfo_ref, ys_ref, g_ref, w_ref, x_ref, qp_ref, kv_ref):
    x = _combine_tile(xmid_ref, rinfo_ref, ys_ref)
    x_ref[...] = x
    xn = _rmsnorm_f32(x, g_ref[...]).astype(jnp.bfloat16)
    proj = jnp.dot(xn, w_ref[...], preferred_element_type=jnp.float32)
    qp_ref[...] = proj[:, :D_QP].astype(jnp.bfloat16)
    kv_ref[...] = proj[:, D_QP:].astype(jnp.bfloat16)


def _tile_rows(rows, cols):
    return pl.BlockSpec((rows, cols), lambda i: (i, 0))


def _combine_proj(next_layer, xmid, rinfo, ys_local, g1, w_in_k):
    T = xmid.shape[0]
    return pl.pallas_call(
        _combine_proj_kernel,
        grid=(T // TOK_TILE,),
        in_specs=[
            _tile_rows(TOK_TILE, D_MODEL), _tile_rows(TOK_TILE, LANES), _tile_rows(LOCAL_ROWS, D_MODEL),
            pl.BlockSpec((None, 1, D_MODEL), lambda i: (next_layer, 0, 0)),
            pl.BlockSpec((None, D_MODEL, D_IN), lambda i: (next_layer, 0, 0), pipeline_mode=pl.Buffered(1)),
        ],
        out_specs=[_tile_rows(TOK_TILE, D_MODEL), _tile_rows(TOK_TILE, D_QP), _tile_rows(TOK_TILE, D_KV)],
        out_shape=[
            jax.ShapeDtypeStruct((T, D_MODEL), jnp.float32),
            jax.ShapeDtypeStruct((T, D_QP), jnp.bfloat16),
            jax.ShapeDtypeStruct((T, D_KV), jnp.bfloat16),
        ],
        compiler_params=pltpu.CompilerParams(
            dimension_semantics=("arbitrary",), vmem_limit_bytes=VMEM_LIMIT),
        name="combine_proj",
    )(xmid, rinfo, ys_local, g1, w_in_k)


def _combine_final(xmid, rinfo, ys_local, g):
    T = xmid.shape[0]
    return pl.pallas_call(
        _combine_final_kernel,
        grid=(T // TOK_TILE,),
        in_specs=[
            _tile_rows(TOK_TILE, D_MODEL), _tile_rows(TOK_TILE, LANES), _tile_rows(LOCAL_ROWS, D_MODEL),
            pl.BlockSpec((1, D_MODEL), lambda i: (0, 0)),
        ],
        out_specs=_tile_rows(TOK_TILE, D_MODEL),
        out_shape=jax.ShapeDtypeStruct((T, D_MODEL), jnp.float32),
        compiler_params=pltpu.CompilerParams(
            dimension_semantics=("arbitrary",), vmem_limit_bytes=VMEM_LIMIT),
        name="combine",
    )(xmid, rinfo, ys_local, g)


def _pair_heads(a, axis):
    shape = a.shape
    split = shape[:axis] + (SWA_KV_HEADS, SWA_REP, HEAD_DIM) + shape[axis + 1:]
    return jnp.swapaxes(a.reshape(split), axis, axis + 1).reshape(shape)


def _in_proj_weight(w):
    off_bq = 3 * D_A
    return jnp.concatenate(
        [w[..., 0:D_A] * QK_SCALE, _pair_heads(w[..., off_bq:off_bq + D_B], w.ndim - 1) * QK_SCALE,
         w[..., D_A:off_bq], w[..., off_bq + D_B:]], axis=-1).astype(jnp.bfloat16)


def _out_proj_weight(w):
    return jnp.concatenate(
        [w[:, 0:D_A], _pair_heads(w[:, D_A:D_A + D_B], 1), w[:, D_A + D_B:]], axis=1).astype(jnp.bfloat16)


def _na_variants_kernel(full_ref, out_ref):
    for k in range(NA_ROWS):
        lo = (NA_ROWS - 1 - k) * GRID_W
        out_ref[k] = full_ref[:, lo:lo + NA_ROWS * GRID_W]


def _na_bias_table(rel_bias):
    depth = rel_bias.shape[0]
    c = np.arange(GRID_W)[:, None]
    cp = np.arange(GRID_W)[None, :]
    cs = np.clip(c - NA_COLS // 2, 0, GRID_W - NA_COLS)
    valid = (cp >= cs) & (cp < cs + NA_COLS)
    d = np.arange(2 * NA_COLS - 1)[:, None, None]
    col_sel = ((cp - c + (NA_COLS - 1))[None] == d) & valid[None]
    full = jnp.einsum("lhrd,dcm->lhcrm", rel_bias.astype(jnp.float32), jnp.asarray(col_sel, jnp.float32),
                      precision=lax.Precision.HIGHEST)
    full = jnp.where(jnp.asarray(valid)[None, None, :, None, :], full * LOG2E, NEG)
    n_off = 2 * NA_ROWS - 1
    full = full.reshape(depth, NA_HEADS * GRID_W, n_off * GRID_W)
    return pl.pallas_call(
        _na_variants_kernel,
        grid=(depth,),
        in_specs=[pl.BlockSpec((None, NA_HEADS * GRID_W, n_off * GRID_W), lambda l: (l, 0, 0))],
        out_specs=pl.BlockSpec((None, NA_ROWS, NA_HEADS * GRID_W, NA_ROWS * GRID_W), lambda l: (l, 0, 0, 0)),
        out_shape=jax.ShapeDtypeStruct((depth, NA_ROWS, NA_HEADS * GRID_W, NA_ROWS * GRID_W), jnp.float32),
        compiler_params=pltpu.CompilerParams(
            dimension_semantics=("arbitrary",), vmem_limit_bytes=VMEM_LIMIT),
        name="na_bias_variants",
    )(full)


def _swa_bias_table():
    slopes = (2.0 ** (-8.0 * np.arange(1, SWA_Q_HEADS + 1) / SWA_Q_HEADS)).astype(np.float32)
    qi = np.arange(SWA_BLOCK)[:, None]
    ki = np.arange(3 * SWA_BLOCK)[None, :]
    variants = []
    for v in range(3):
        dist = np.abs(ki - qi - v * SWA_BLOCK).astype(np.float32)
        tab = np.where(dist <= SWA_WINDOW, -slopes[:, None, None] * dist[None] * LOG2E, np.float32(NEG))
        variants.append(tab.reshape(SWA_Q_HEADS * SWA_BLOCK, 3 * SWA_BLOCK).astype(np.float32))
    return jnp.asarray(np.stack(variants))


def _block_diag(pool_w):
    depth, n = pool_w.shape[0:2]
    eye = jnp.asarray(np.eye(n, dtype=np.float32))
    out = pool_w[:, :, :, None, :] * eye[None, :, None, :, None]
    return out.reshape(depth, n * POOL_GROUP_DIM, n * POOL_GROUP_DIM)


def _router_weights(rg_w, rg_b, re_w, re_b):
    def lanes(g, e):
        gap = jnp.zeros(g.shape[:-1] + (ROUTE_LANE0 - N_GROUPS,), jnp.float32)
        tail = jnp.zeros(g.shape[:-1] + (LANES - ROUTE_LANE0 - N_EXPERTS,), jnp.float32)
        return jnp.concatenate([g.astype(jnp.float32), gap, e.astype(jnp.float32), tail], axis=-1)

    return lanes(rg_w, re_w).astype(jnp.bfloat16), lanes(rg_b, re_b)[:, None, :]


def _dispatch_tables(cnt, n_tiles):
    nb = cnt.shape[0] // SUBLANES
    n = cnt.reshape(nb, SUBLANES, LANES)[:, 0, ROUTE_LANE0:ROUTE_LANE0 + N_EXPERTS].astype(jnp.int32)
    g = (n + (CHUNK - 1)) // CHUNK
    l_end = jnp.cumsum(g, axis=1)
    l_off = l_end - g
    c_end = jnp.cumsum(g, axis=0)
    c_off = c_end - g
    tot = c_end[-1]
    tiles = (tot + (TILE_CHUNKS - 1)) // TILE_CHUNKS
    t_end = jnp.cumsum(tiles)
    t_off = t_end - tiles
    n_used = t_end[-1:]

    experts = jnp.arange(N_EXPERTS, dtype=jnp.int32)
    tile_ids = jnp.arange(n_tiles, dtype=jnp.int32)
    tile_expert = jnp.minimum(jnp.sum((t_end[None, :] <= tile_ids[:, None]).astype(jnp.int32), axis=1),
                              N_EXPERTS - 1)
    oh_te = (tile_expert[:, None] == experts[None, :]).astype(jnp.int32)
    left = jnp.sum(oh_te * (tot + t_off * TILE_CHUNKS)[None, :], axis=1) - tile_ids * TILE_CHUNKS
    n_valid = jnp.clip(left, 0, TILE_CHUNKS)
    has_rows = tiles > 0
    first = ((tile_ids == jnp.sum(oh_te * t_off[None, :], axis=1)) & (n_valid > 0)).astype(jnp.int32)
    wslot = jnp.sum(oh_te * ((jnp.cumsum(has_rows.astype(jnp.int32)) - 1) % 2)[None, :], axis=1)
    later = (experts[None, :] > experts[:, None]) & has_rows[None, :]
    nxt = jnp.min(jnp.where(later, experts[None, :], N_EXPERTS), axis=1)
    next_expert = jnp.sum(oh_te * jnp.where(nxt < N_EXPERTS, nxt, -1)[None, :], axis=1)
    tile_tables = (tile_expert, n_valid // SUB_CHUNKS, n_valid % SUB_CHUNKS, first, wslot, next_expert)

    q = jnp.arange(n_tiles * TILE_CHUNKS, dtype=jnp.int32)
    tile_q = q // TILE_CHUNKS
    oh_e = (jnp.repeat(tile_expert, TILE_CHUNKS)[:, None] == experts[None, :]).astype(jnp.int32)
    ro = q - jnp.sum(oh_e * t_off[None, :], axis=1) * TILE_CHUNKS
    valid = (ro < jnp.sum(oh_e * tot[None, :], axis=1)) & (tile_q < n_used[0])
    cols = jnp.dot(jnp.concatenate([c_end, c_off, l_off], axis=0).astype(jnp.float32),
                   oh_e.T.astype(jnp.float32), precision=lax.Precision.HIGHEST).astype(jnp.int32)
    c_end_q, c_off_q, l_off_q = cols[0:nb], cols[nb:2 * nb], cols[2 * nb:3 * nb]
    b_q = jnp.minimum(jnp.sum((c_end_q <= ro[None, :]).astype(jnp.int32), axis=0), nb - 1)
    oh_b = (jnp.arange(nb, dtype=jnp.int32)[:, None] == b_q[None, :]).astype(jnp.int32)
    src = b_q * LOCAL_CHUNKS + jnp.sum(oh_b * (l_off_q + ro[None, :] - c_off_q), axis=0)
    chunk_src = jnp.where(valid, src, LOCAL_CHUNKS - 1)

    return tile_tables, chunk_src


def kernel(x, norm1_g, w_in, nat_bias, swa_sink, pool_w, pool_scale, w_out, norm2_g, router_g_w,
           router_g_b, router_e_w, router_e_b, expert_w_gate, expert_w_up, expert_w_down, final_g):
    batch, seq_len, _ = x.shape
    depth = w_in.shape[0]
    T = batch * seq_len
    assert seq_len % TOK_TILE == 0 and TOK_TILE % SWA_BLOCK == 0 and TOK_TILE % GRID_W == 0
    max_chunks = (2 * T) // CHUNK + (T // TOK_TILE) * N_EXPERTS
    n_tiles = max_chunks // TILE_CHUNKS + N_EXPERTS

    swb = _swa_bias_table()
    tri = jnp.asarray(np.tril(np.ones((TOK_TILE, TOK_TILE), np.float32), -1)).astype(jnp.bfloat16)
    utri = jnp.asarray(np.triu(np.ones((LANES, LANES), np.float32), 1)).astype(jnp.bfloat16)

    w_in_k = _in_proj_weight(w_in)
    w_out_k = _out_proj_weight(w_out)
    nab = _na_bias_table(nat_bias)
    sinkcol = jnp.broadcast_to((swa_sink.astype(jnp.float32) * LOG2E)[:, :, None, None],
                               (depth, SWA_Q_HEADS, SWA_BLOCK, LANES)).reshape(depth, SWA_Q_HEADS * SWA_BLOCK, LANES)
    poolw = _block_diag(pool_w).astype(jnp.bfloat16)
    pools = pool_scale.reshape(depth, 1, D_C).astype(jnp.float32)
    rw, rb = _router_weights(router_g_w, router_g_b, router_e_w, router_e_b)
    g1 = norm1_g.reshape(depth, 1, D_MODEL)
    g2 = norm2_g.reshape(depth, 1, D_MODEL)

    x2 = x.reshape(T, D_MODEL)
    qp, kv = _norm_proj(0, x2, g1, w_in_k)
    for l in range(depth):
        xmid, xs_local, rinfo, cnt = _mixer(
            l, x2, qp, kv, (nab, sinkcol, poolw, pools, w_out_k, g2, rw, rb), (swb, tri, utri),
            seq_len=seq_len)
        tile_tables, chunk_src = _dispatch_tables(cnt, n_tiles)
        ys_local = _experts(l, tile_tables, chunk_src, xs_local, expert_w_gate, expert_w_up, expert_w_down)
        if l + 1 < depth:
            x2, qp, kv = _combine_proj(l + 1, xmid, rinfo, ys_local, g1, w_in_k)
        else:
            x2 = _combine_final(xmid, rinfo, ys_local, final_g.reshape(1, D_MODEL))
    return x2.reshape(batch, seq_len, D_MODEL)
```

```python
import functools

import jax
import jax.numpy as jnp
import numpy as np
from jax import lax
from jax.experimental import pallas as pl
from jax.experimental.pallas import tpu as pltpu

D_MODEL = 1024
GRID_W = 64
HEAD_DIM = 64
NA_HEADS = 4
NA_ROWS = 8
NA_COLS = 16
SWA_Q_HEADS = 8
SWA_KV_HEADS = 2
SWA_REP = SWA_Q_HEADS // SWA_KV_HEADS
SWA_WINDOW = 128
SWA_BLOCK = 128
POOL_WINDOWS = (2, 4, 8, 16)
POOL_GROUP_DIM = 64
D_A = NA_HEADS * HEAD_DIM
D_B = SWA_Q_HEADS * HEAD_DIM
D_BKV = SWA_KV_HEADS * HEAD_DIM
D_C = len(POOL_WINDOWS) * POOL_GROUP_DIM
D_MIX = D_A + D_B + D_C
D_QP = D_A + D_B
D_KV = 2 * D_A + 2 * D_BKV + D_C
D_IN = D_QP + D_KV
N_GROUPS = 4
EXPERTS_PER_GROUP = 8
N_EXPERTS = N_GROUPS * EXPERTS_PER_GROUP
D_EXPERT = 256
RMS_EPS = 1e-6
NEG = -1e30
LOG2E = 1.4426950408889634
QK_SCALE = HEAD_DIM ** -0.5 * LOG2E

LANES = 128
SUBLANES = 8

TOK_TILE = 512
EXP_SUB = 256
EXP_SUBS = 4
EXP_TILE = EXP_SUB * EXP_SUBS
CHUNK = 16
LOCAL_CHUNKS = (2 * TOK_TILE + N_EXPERTS * (CHUNK - 1)) // CHUNK + 2
LOCAL_ROWS = LOCAL_CHUNKS * CHUNK
SUB_CHUNKS = EXP_SUB // CHUNK
TILE_CHUNKS = EXP_TILE // CHUNK
GATHER_AHEAD = 2
GATHER_SLOTS = GATHER_AHEAD + 1
ROUTE_LANE0 = 8
INFO_ROW1, INFO_ROW2, INFO_GATE1, INFO_GATE2 = 0, 1, 2, 3
NA_ROWS_PER_STEP = 8
SWA_BLOCKS_PER_STEP = 4
HALO = max(POOL_WINDOWS) // 2
WIN_TILES = 3
VMEM_LIMIT = 56 * 1024 * 1024


def _rmsnorm_f32(x, g):
    return x * lax.rsqrt(jnp.mean(x * x, axis=-1, keepdims=True) + RMS_EPS) * g


def _norm_proj_kernel(x_ref, g_ref, w_ref, qp_ref, kv_ref):
    xn = _rmsnorm_f32(x_ref[...], g_ref[...]).astype(jnp.bfloat16)
    proj = jnp.dot(xn, w_ref[...], preferred_element_type=jnp.float32)
    qp_ref[...] = proj[:, :D_QP].astype(jnp.bfloat16)
    kv_ref[...] = proj[:, D_QP:].astype(jnp.bfloat16)


def _norm_proj(layer, x2, g, w):
    T = x2.shape[0]
    return pl.pallas_call(
        _norm_proj_kernel,
        grid=(T // TOK_TILE,),
        in_specs=[
            pl.BlockSpec((TOK_TILE, D_MODEL), lambda i: (i, 0)),
            pl.BlockSpec((None, 1, D_MODEL), lambda i: (layer, 0, 0)),
            pl.BlockSpec((None, D_MODEL, D_IN), lambda i: (layer, 0, 0)),
        ],
        out_specs=[
            pl.BlockSpec((TOK_TILE, D_QP), lambda i: (i, 0)),
            pl.BlockSpec((TOK_TILE, D_KV), lambda i: (i, 0)),
        ],
        out_shape=[
            jax.ShapeDtypeStruct((T, D_QP), jnp.bfloat16),
            jax.ShapeDtypeStruct((T, D_KV), jnp.bfloat16),
        ],
        compiler_params=pltpu.CompilerParams(
            dimension_semantics=("arbitrary",), vmem_limit_bytes=VMEM_LIMIT),
        name="norm_proj",
    )(x2, g, w)


KW_AK, KW_AV, KW_BK, KW_BV = 0, D_A, 2 * D_A, 2 * D_A + D_BKV
KW_COLS = 2 * D_A + 2 * D_BKV
KV_CU = KW_COLS


def _mixer_kernel(x_ref, qp_ref, kwin_ref, nab_ref, swb_ref, sink_ref,
                  poolw_ref, pools_ref, wout_ref, g2_ref, rw_ref, rb_ref, tri_ref, utri_ref,
                  xmid_ref, xs_ref, rinfo_ref, cnt_ref,
                  uwin, mix, xn_scr, logit_scr, *, seq_len):
    t = pl.program_id(0)
    nblk = seq_len // TOK_TILE
    i = jnp.minimum(t, pl.num_programs(0) - 2) % nblk
    rows_per_tile = TOK_TILE // GRID_W
    grid_rows = seq_len // GRID_W

    @pl.when(t == 0)
    def _():
        xn_scr[...] = jnp.zeros_like(xn_scr)
        logit_scr[...] = jnp.zeros_like(logit_scr)

    cur_off = pl.multiple_of((i - jnp.clip(i - 1, 0, nblk - WIN_TILES)) * TOK_TILE, TOK_TILE)

    lane_a = lax.broadcasted_iota(jnp.int32, (GRID_W, D_A), 1) // HEAD_DIM

    def na_row(rr):
        r = i * rows_per_tile + rr
        rs = jnp.clip(r - NA_ROWS // 2, 0, grid_rows - NA_ROWS)
        variant = r - rs
        start = pl.multiple_of(cur_off + (rs - i * rows_per_tile) * GRID_W, GRID_W)
        q0 = pl.multiple_of(rr * GRID_W, GRID_W)
        q = qp_ref[pl.ds(q0, GRID_W), 0:D_A]
        zero = jnp.zeros_like(q)
        qs = jnp.concatenate([jnp.where(lane_a == h, q, zero) for h in range(NA_HEADS)], axis=0)
        kw = kwin_ref[pl.ds(start, NA_ROWS * GRID_W), KW_AK:KW_AK + D_A]
        vw = kwin_ref[pl.ds(start, NA_ROWS * GRID_W), KW_AV:KW_AV + D_A]
        s = lax.dot_general(qs, kw, (((1,), (1,)), ((), ())), preferred_element_type=jnp.float32)
        s = s + nab_ref[variant]
        m = jnp.max(s, axis=-1, keepdims=True)
        p = jnp.exp2(s - m)
        l = jnp.sum(p, axis=-1, keepdims=True)
        pv = jnp.dot(p.astype(jnp.bfloat16), vw, preferred_element_type=jnp.float32)
        pv = pv * (1.0 / l)
        o = jnp.zeros((GRID_W, D_A), jnp.float32)
        for h in range(NA_HEADS):
            o = o + jnp.where(lane_a == h, pv[h * GRID_W:(h + 1) * GRID_W, :], 0.0)
        mix[pl.ds(q0, GRID_W), 0:D_A] = o.astype(jnp.bfloat16)

    def na_step(it, c):
        for k in range(NA_ROWS_PER_STEP):
            na_row(it * NA_ROWS_PER_STEP + k)
        return c

    lax.fori_loop(0, rows_per_tile // NA_ROWS_PER_STEP, na_step, 0)

    lane_b = lax.broadcasted_iota(jnp.int32, (SWA_BLOCK, LANES), 1) // HEAD_DIM
    blocks_per_tile = TOK_TILE // SWA_BLOCK
    nblocks = seq_len // SWA_BLOCK

    ones_v = jnp.ones((3 * SWA_BLOCK, LANES), jnp.bfloat16)

    def swa_step(sb, c):
        n = i * blocks_per_tile + sb
        variant = jnp.where(n == 0, 0, jnp.where(n == nblocks - 1, 2, 1))
        q0 = pl.multiple_of(sb * SWA_BLOCK, SWA_BLOCK)
        k0 = pl.multiple_of(cur_off + (sb - variant) * SWA_BLOCK, SWA_BLOCK)
        kw = kwin_ref[pl.ds(k0, 3 * SWA_BLOCK), KW_BK:KW_BK + D_BKV]
        vaug = jnp.concatenate([kwin_ref[pl.ds(k0, 3 * SWA_BLOCK), KW_BV:KW_BV + D_BKV], ones_v], axis=1)
        outs = []
        for g in range(SWA_KV_HEADS):
            pieces = []
            for t in range(SWA_REP):
                qt = qp_ref[pl.ds(q0, SWA_BLOCK), D_A + t * LANES:D_A + (t + 1) * LANES]
                pieces.append(jnp.where(lane_b == g, qt, jnp.zeros_like(qt)))
            qs = jnp.concatenate(pieces, axis=0)
            r0 = g * SWA_REP * SWA_BLOCK
            s = lax.dot_general(qs, kw, (((1,), (1,)), ((), ())), preferred_element_type=jnp.float32)
            s = s + swb_ref[variant, r0:r0 + SWA_REP * SWA_BLOCK, :]
            sink = sink_ref[r0:r0 + SWA_REP * SWA_BLOCK, :]
            m = jnp.broadcast_to(jnp.max(s, axis=-1, keepdims=True), sink.shape)
            m = jnp.maximum(m, sink)
            p = jnp.exp2(s - jnp.concatenate([m, m, m], axis=1)).astype(jnp.bfloat16)
            pv = jnp.dot(p, vaug, preferred_element_type=jnp.float32)
            l = pv[:, LANES:2 * LANES] + jnp.exp2(sink - m)
            outs.append(pv[:, 0:LANES] * (1.0 / l))
        for t in range(SWA_REP):
            o0 = outs[0][t * SWA_BLOCK:(t + 1) * SWA_BLOCK, :]
            o1 = outs[1][t * SWA_BLOCK:(t + 1) * SWA_BLOCK, :]
            ot = jnp.where(lane_b == 0, o0, o1)
            mix[pl.ds(q0, SWA_BLOCK), D_A + t * LANES:D_A + (t + 1) * LANES] = ot.astype(jnp.bfloat16)
        return c

    def swa_pair(it, c):
        for k in range(SWA_BLOCKS_PER_STEP):
            swa_step(it * SWA_BLOCKS_PER_STEP + k, c)
        return c

    lax.fori_loop(0, blocks_per_tile // SWA_BLOCKS_PER_STEP, swa_pair, 0)

    u = kwin_ref[pl.ds(cur_off, TOK_TILE), KV_CU:KV_CU + D_C].astype(jnp.float32)
    prev_ok = (i > 0).astype(jnp.float32)
    next_ok = (i < nblk - 1).astype(jnp.float32)
    before = pl.multiple_of(jnp.maximum(cur_off - CHUNK, 0), CHUNK)
    after = pl.multiple_of(jnp.minimum(cur_off + TOK_TILE, WIN_TILES * TOK_TILE - CHUNK), CHUNK)
    u_before = kwin_ref[pl.ds(before, CHUNK), KV_CU:KV_CU + D_C].astype(jnp.float32)
    u_after = kwin_ref[pl.ds(after, CHUNK), KV_CU:KV_CU + D_C].astype(jnp.float32)
    uwin[0:HALO, :] = u_before[CHUNK - HALO:CHUNK, :] * prev_ok
    uwin[HALO:HALO + TOK_TILE, :] = u
    uwin[HALO + TOK_TILE:2 * HALO + TOK_TILE, :] = u_after[0:HALO, :] * next_ok
    n_ext = TOK_TILE + 2 * HALO
    a2 = uwin[0:n_ext - 1, :] + uwin[1:n_ext, :]
    a4 = a2[0:n_ext - 3, :] + a2[2:n_ext - 1, :]
    a8 = a4[0:n_ext - 7, :] + a4[4:n_ext - 3, :]
    a16 = a8[0:n_ext - 15, :] + a8[8:n_ext - 7, :]
    w2 = a2[7:7 + TOK_TILE, :]
    w4 = a4[6:6 + TOK_TILE, :]
    w8 = a8[4:4 + TOK_TILE, :]
    w16 = a16[0:TOK_TILE, :]
    lane_c = lax.broadcasted_iota(jnp.int32, (TOK_TILE, D_C), 1) // POOL_GROUP_DIM
    pooled = jnp.where(lane_c == 0, w2, jnp.where(lane_c == 1, w4, jnp.where(lane_c == 2, w8, w16)))
    half = jnp.where(lane_c == 0, 1, jnp.where(lane_c == 1, 2, jnp.where(lane_c == 2, 4, 8)))
    pos = i * TOK_TILE + lax.broadcasted_iota(jnp.int32, (TOK_TILE, D_C), 0)
    cnt = (jnp.minimum(pos + half, seq_len) - jnp.maximum(pos - half, 0)).astype(jnp.float32)
    d = (pooled / cnt - u).astype(jnp.bfloat16)
    oc = jnp.dot(d, poolw_ref[...], preferred_element_type=jnp.float32) * pools_ref[...]
    mix[:, D_A + D_B:D_MIX] = oc.astype(jnp.bfloat16)

    xn = xn_scr[...]
    logits = logit_scr[...]

    xm = x_ref[...] + jnp.dot(mix[...], wout_ref[...], preferred_element_type=jnp.float32)
    xmid_ref[...] = xm
    xn_new = _rmsnorm_f32(xm, g2_ref[...]).astype(jnp.bfloat16)
    xn_scr[...] = xn_new
    logit_scr[...] = jnp.dot(xn_new, rw_ref[...], preferred_element_type=jnp.float32) + rb_ref[...]

    lane = lax.broadcasted_iota(jnp.int32, (TOK_TILE, LANES), 1).astype(jnp.float32)
    is_g = lane < N_GROUPS
    gl = jnp.where(is_g, logits, NEG)
    gmax = jnp.max(gl, axis=-1, keepdims=True)
    gtop = jnp.min(jnp.where(is_g & (gl == gmax), lane, float(LANES)), axis=-1, keepdims=True)
    gprob = 1.0 / jnp.sum(jnp.exp(gl - gmax), axis=-1, keepdims=True)
    e_lo = ROUTE_LANE0 + gtop * EXPERTS_PER_GROUP
    in_grp = (lane >= e_lo) & (lane < e_lo + EXPERTS_PER_GROUP)
    el = jnp.where(in_grp, logits, NEG)
    m1 = jnp.max(el, axis=-1, keepdims=True)
    i1 = jnp.min(jnp.where(in_grp & (el == m1), lane, float(LANES)), axis=-1, keepdims=True)
    el2 = jnp.where(lane == i1, NEG, el)
    m2 = jnp.max(el2, axis=-1, keepdims=True)
    i2 = jnp.min(jnp.where(in_grp & (lane != i1) & (el2 == m2), lane, float(LANES)), axis=-1, keepdims=True)
    r21 = jnp.exp(m2 - m1)
    gate1 = gprob / (1.0 + r21)
    gate2 = gprob * r21 / (1.0 + r21)

    oh1 = lane == i1
    oh2 = lane == i2
    oh = jnp.where(oh1 | oh2, 1.0, 0.0)
    earlier = jnp.dot(tri_ref[...], oh.astype(jnp.bfloat16), preferred_element_type=jnp.float32)
    n_e = jnp.sum(oh, axis=0, keepdims=True)
    chunks_e = jnp.floor((n_e + (CHUNK - 1)) * (1.0 / CHUNK))
    seg0 = jnp.dot(jnp.broadcast_to(chunks_e, (SUBLANES, LANES)).astype(jnp.bfloat16), utri_ref[...],
                   preferred_element_type=jnp.float32)[0:1, :] * CHUNK
    base = earlier + seg0
    lp1 = jnp.sum(jnp.where(oh1, base, 0.0), axis=-1, keepdims=True)
    lp2 = jnp.sum(jnp.where(oh2, base, 0.0), axis=-1, keepdims=True)
    info = jnp.zeros((TOK_TILE, LANES), jnp.float32)
    for k, col in enumerate((lp1, lp2, gate1, gate2)):
        info = jnp.where(lane == k, col, info)
    rinfo_ref[...] = info
    cnt_ref[...] = jnp.broadcast_to(n_e, cnt_ref.shape)

    info_t = info.T
    prow = lax.broadcasted_iota(jnp.int32, (LOCAL_ROWS, TOK_TILE), 0).astype(jnp.float32)
    sel = jnp.where((prow == info_t[INFO_ROW1:INFO_ROW1 + 1, :]) | (prow == info_t[INFO_ROW2:INFO_ROW2 + 1, :]),
                    1.0, 0.0).astype(jnp.bfloat16)
    moved = jnp.dot(sel, xn, preferred_element_type=jnp.float32)
    xs_ref[...] = moved.astype(jnp.bfloat16)


def _mixer(layer, x2, qp, kv, layer_params, shared_tables, *, seq_len):
    nab, sinkcol, poolw, pools, wout, g2, rw, rb = layer_params
    swb, tri, utri = shared_tables
    T = x2.shape[0]
    nblk = seq_len // TOK_TILE
    n_tok_tiles = T // TOK_TILE

    def mixed(t):
        return jnp.minimum(t, n_tok_tiles - 1)

    def cur(t):
        return (mixed(t), 0)

    def window(t):
        seq0 = (mixed(t) // nblk) * nblk
        return (jnp.clip(mixed(t) - 1, seq0, seq0 + nblk - WIN_TILES) * TOK_TILE, 0)

    def dispatched(t):
        return (jnp.maximum(t - 1, 0), 0)

    def resident(a):
        zeros = (0,) * a.ndim
        return pl.BlockSpec(a.shape, lambda t: zeros, pipeline_mode=pl.Buffered(1))

    def resident_layer(a):
        index = (layer,) + (0,) * (a.ndim - 1)
        return pl.BlockSpec((None,) + a.shape[1:], lambda t: index, pipeline_mode=pl.Buffered(1))

    return pl.pallas_call(
        functools.partial(_mixer_kernel, seq_len=seq_len),
        grid=(n_tok_tiles + 1,),
        in_specs=[
            pl.BlockSpec((TOK_TILE, D_MODEL), cur),
            pl.BlockSpec((TOK_TILE, D_QP), cur),
            pl.BlockSpec((pl.Element(WIN_TILES * TOK_TILE), pl.Element(D_KV)), window),
            resident_layer(nab), resident(swb), resident_layer(sinkcol), resident_layer(poolw),
            resident_layer(pools), resident_layer(wout), resident_layer(g2), resident_layer(rw),
            resident_layer(rb), resident(tri), resident(utri),
        ],
        out_specs=[
            pl.BlockSpec((TOK_TILE, D_MODEL), cur),
            pl.BlockSpec((LOCAL_ROWS, D_MODEL), dispatched),
            pl.BlockSpec((TOK_TILE, LANES), dispatched),
            pl.BlockSpec((SUBLANES, LANES), dispatched),
        ],
        out_shape=[
            jax.ShapeDtypeStruct((T, D_MODEL), jnp.float32),
            jax.ShapeDtypeStruct((n_tok_tiles * LOCAL_ROWS, D_MODEL), jnp.bfloat16),
            jax.ShapeDtypeStruct((T, LANES), jnp.float32),
            jax.ShapeDtypeStruct((n_tok_tiles * SUBLANES, LANES), jnp.float32),
        ],
        scratch_shapes=[
            pltpu.VMEM((TOK_TILE + 2 * HALO, D_C), jnp.float32),
            pltpu.VMEM((TOK_TILE, D_MIX), jnp.bfloat16),
            pltpu.VMEM((TOK_TILE, D_MODEL), jnp.bfloat16),
            pltpu.VMEM((TOK_TILE, LANES), jnp.float32),
        ],
        compiler_params=pltpu.CompilerParams(
            dimension_semantics=("arbitrary",), vmem_limit_bytes=VMEM_LIMIT),
        name="mixer",
    )(x2, qp, kv, nab, swb, sinkcol, poolw, pools, wout, g2, rw, rb, tri, utri)


def _chunk_copy(src_hbm, src_chunk, dst, dst_chunk, sem):
    return pltpu.make_async_copy(
        src_hbm.at[pl.ds(pl.multiple_of(src_chunk * CHUNK, CHUNK), CHUNK)],
        dst.at[pl.ds(dst_chunk * CHUNK, CHUNK)],
        sem)


def _expert_kernel(te_ref, nfull_ref, npart_ref, first_ref, wslot_ref, nexte_ref, csrc_ref,
                   xs_hbm, wg_hbm, wu_hbm, wd_hbm, ys_hbm,
                   xbuf, obuf, wgf, wuf, wdf, wgb, wub, wdb, sem, osem, psem, wsem, *, layer):
    j = pl.program_id(0)
    nt = pl.num_programs(0)
    slot = j % GATHER_SLOTS
    oslot = j % 2

    def nsub(tile):
        return nfull_ref[tile] + (npart_ref[tile] > 0).astype(jnp.int32)

    def scatter_full(tile, os, h):
        return [pltpu.make_async_copy(
            obuf.at[os, pl.ds(c * CHUNK, CHUNK)],
            ys_hbm.at[pl.ds(pl.multiple_of(csrc_ref[tile * TILE_CHUNKS + c] * CHUNK, CHUNK), CHUNK)],
            osem.at[os, h]) for c in range(h * SUB_CHUNKS, (h + 1) * SUB_CHUNKS)]

    def scatter_part(tile, os, c):
        idx = nfull_ref[tile] * SUB_CHUNKS + c
        return pltpu.make_async_copy(
            obuf.at[os, pl.ds(pl.multiple_of(idx * CHUNK, CHUNK), CHUNK)],
            ys_hbm.at[pl.ds(pl.multiple_of(csrc_ref[tile * TILE_CHUNKS + idx] * CHUNK, CHUNK), CHUNK)],
            psem.at[os])

    def start_scatters(tile, os):
        for h in range(EXP_SUBS):
            @pl.when(h < nfull_ref[tile])
            def _():
                for cp in scatter_full(tile, os, h):
                    cp.start()

        @pl.when(npart_ref[tile] > 0)
        def _():
            for c in range(SUB_CHUNKS):
                @pl.when(c < npart_ref[tile])
                def _():
                    scatter_part(tile, os, c).start()

    def wait_scatters(tile, os):
        for h in range(EXP_SUBS):
            @pl.when(h < nfull_ref[tile])
            def _():
                pltpu.make_async_copy(obuf.at[os, pl.ds(0, EXP_SUB)], ys_hbm.at[pl.ds(0, EXP_SUB)],
                                      osem.at[os, h]).wait()

        @pl.when(npart_ref[tile] > 0)
        def _():
            for c in range(SUB_CHUNKS):
                @pl.when(c < npart_ref[tile])
                def _():
                    scatter_part(tile, os, c).wait()

    def start_gather(tile, s, h):
        for c in range(h * SUB_CHUNKS, (h + 1) * SUB_CHUNKS):
            _chunk_copy(xs_hbm, csrc_ref[tile * TILE_CHUNKS + c], xbuf.at[s], c, sem.at[s, h]).start()

    def wait_gather(s, h):
        rows = pl.ds(h * EXP_SUB, EXP_SUB)
        pltpu.make_async_copy(xs_hbm.at[pl.ds(0, EXP_SUB)], xbuf.at[s, rows], sem.at[s, h]).wait()

    def weight_copies(expert, ws):
        return [pltpu.make_async_copy(w_hbm.at[layer, expert], wbuf.at[ws], wsem.at[k, ws])
                for k, (w_hbm, wbuf) in enumerate(((wg_hbm, wgf), (wu_hbm, wuf), (wd_hbm, wdf)))]

    for h in range(EXP_SUBS):
        for first_tile in range(GATHER_AHEAD):
            @pl.when((j == 0) & (h < nsub(first_tile)))
            def _():
                start_gather(first_tile, first_tile, h)

        ahead = j + GATHER_AHEAD

        @pl.when(h < nsub(jnp.minimum(ahead, nt - 1)) * (ahead < nt).astype(jnp.int32))
        def _():
            start_gather(ahead, ahead % GATHER_SLOTS, h)

    ws = wslot_ref[j]

    @pl.when(j == 0)
    def _():
        for cp in weight_copies(te_ref[0], 0):
            cp.start()

    @pl.when(first_ref[j] > 0)
    def _():
        for cp in weight_copies(te_ref[j], ws):
            cp.wait()

        @pl.when(nexte_ref[j] >= 0)
        def _():
            for cp in weight_copies(nexte_ref[j], 1 - ws):
                cp.start()

        wgb[...] = wgf[ws].astype(jnp.bfloat16)
        wub[...] = wuf[ws].astype(jnp.bfloat16)
        wdb[...] = wdf[ws].astype(jnp.bfloat16)

    def gated_mlp(n_rows):
        xs = xbuf[slot, 0:n_rows, :]
        gate = jnp.dot(xs, wgb[...], preferred_element_type=jnp.float32)
        up = jnp.dot(xs, wub[...], preferred_element_type=jnp.float32)
        act = (gate * (1.0 / (1.0 + jnp.exp(-gate))) * up).astype(jnp.bfloat16)
        return jnp.dot(act, wdb[...], preferred_element_type=jnp.float32).astype(jnp.bfloat16)

    @pl.when(j >= 2)
    def _():
        wait_scatters(j - 2, oslot)

    for k in range(1, EXP_SUBS + 1):
        @pl.when(nsub(j) == k)
        def _():
            for h in range(k):
                wait_gather(slot, h)
            obuf[oslot, 0:k * EXP_SUB, :] = gated_mlp(k * EXP_SUB)

    start_scatters(j, oslot)

    @pl.when(j == nt - 1)
    def _():
        wait_scatters(j - 1, 1 - oslot)
        wait_scatters(j, oslot)


def _experts(layer, tile_tables, chunk_src, xs_local, wg, wu, wd):
    tile_expert, n_full, n_part, first, wslot, next_expert = tile_tables
    n_tiles = tile_expert.shape[0]
    any_space = pl.BlockSpec(memory_space=pl.ANY)
    n_prefetch = 7

    return pl.pallas_call(
        functools.partial(_expert_kernel, layer=layer),
        grid_spec=pltpu.PrefetchScalarGridSpec(
            num_scalar_prefetch=n_prefetch,
            grid=(n_tiles,),
            in_specs=[any_space] * 4,
            out_specs=any_space,
            scratch_shapes=[
                pltpu.VMEM((GATHER_SLOTS, EXP_TILE, D_MODEL), jnp.bfloat16),
                pltpu.VMEM((2, EXP_TILE, D_MODEL), jnp.bfloat16),
                pltpu.VMEM((2, D_MODEL, D_EXPERT), jnp.float32),
                pltpu.VMEM((2, D_MODEL, D_EXPERT), jnp.float32),
                pltpu.VMEM((2, D_EXPERT, D_MODEL), jnp.float32),
                pltpu.VMEM((D_MODEL, D_EXPERT), jnp.bfloat16),
                pltpu.VMEM((D_MODEL, D_EXPERT), jnp.bfloat16),
                pltpu.VMEM((D_EXPERT, D_MODEL), jnp.bfloat16),
                pltpu.SemaphoreType.DMA((GATHER_SLOTS, EXP_SUBS)),
                pltpu.SemaphoreType.DMA((2, EXP_SUBS)),
                pltpu.SemaphoreType.DMA((2,)),
                pltpu.SemaphoreType.DMA((3, 2)),
            ],
        ),
        out_shape=jax.ShapeDtypeStruct(xs_local.shape, xs_local.dtype),
        input_output_aliases={n_prefetch: 0},
        compiler_params=pltpu.CompilerParams(
            dimension_semantics=("arbitrary",), vmem_limit_bytes=VMEM_LIMIT),
        name="experts",
    )(tile_expert, n_full, n_part, first, wslot, next_expert, chunk_src, xs_local, wg, wu, wd)


def _combine_tile(xmid_ref, rinfo_ref, ys_ref):
    info = rinfo_ref[...]
    pcol = lax.broadcasted_iota(jnp.int32, (TOK_TILE, LOCAL_ROWS), 1).astype(jnp.float32)
    pick = jnp.where(pcol == info[:, INFO_ROW1:INFO_ROW1 + 1], info[:, INFO_GATE1:INFO_GATE1 + 1],
                     jnp.where(pcol == info[:, INFO_ROW2:INFO_ROW2 + 1], info[:, INFO_GATE2:INFO_GATE2 + 1], 0.0)
                     ).astype(jnp.bfloat16)
    return xmid_ref[...] + jnp.dot(pick, ys_ref[...], preferred_element_type=jnp.float32)


def _combine_final_kernel(xmid_ref, rinfo_ref, ys_ref, g_ref, out_ref):
    x = _combine_tile(xmid_ref, rinfo_ref, ys_ref)
    out_ref[...] = _rmsnorm_f32(x, g_ref[...])


def _combine_proj_kernel(xmid_ref, rinfo_ref, ys_ref, g_ref, w_ref, x_ref, qp_ref, kv_ref):
    x = _combine_tile(xmid_ref, rinfo_ref, ys_ref)
    x_ref[...] = x
    xn = _rmsnorm_f32(x, g_ref[...]).astype(jnp.bfloat16)
    proj = jnp.dot(xn, w_ref[...], preferred_element_type=jnp.float32)
    qp_ref[...] = proj[:, :D_QP].astype(jnp.bfloat16)
    kv_ref[...] = proj[:, D_QP:].astype(jnp.bfloat16)


def _tile_rows(rows, cols):
    return pl.BlockSpec((rows, cols), lambda i: (i, 0))


def _combine_proj(next_layer, xmid, rinfo, ys_local, g1, w_in_k):
    T = xmid.shape[0]
    return pl.pallas_call(
        _combine_proj_kernel,
        grid=(T // TOK_TILE,),
        in_specs=[
            _tile_rows(TOK_TILE, D_MODEL), _tile_rows(TOK_TILE, LANES), _tile_rows(LOCAL_ROWS, D_MODEL),
            pl.BlockSpec((None, 1, D_MODEL), lambda i: (next_layer, 0, 0)),
            pl.BlockSpec((None, D_MODEL, D_IN), lambda i: (next_layer, 0, 0), pipeline_mode=pl.Buffered(1)),
        ],
        out_specs=[_tile_rows(TOK_TILE, D_MODEL), _tile_rows(TOK_TILE, D_QP), _tile_rows(TOK_TILE, D_KV)],
        out_shape=[
            jax.ShapeDtypeStruct((T, D_MODEL), jnp.float32),
            jax.ShapeDtypeStruct((T, D_QP), jnp.bfloat16),
            jax.ShapeDtypeStruct((T, D_KV), jnp.bfloat16),
        ],
        compiler_params=pltpu.CompilerParams(
            dimension_semantics=("arbitrary",), vmem_limit_bytes=VMEM_LIMIT),
        name="combine_proj",
    )(xmid, rinfo, ys_local, g1, w_in_k)


def _combine_final(xmid, rinfo, ys_local, g):
    T = xmid.shape[0]
    return pl.pallas_call(
        _combine_final_kernel,
        grid=(T // TOK_TILE,),
        in_specs=[
            _tile_rows(TOK_TILE, D_MODEL), _tile_rows(TOK_TILE, LANES), _tile_rows(LOCAL_ROWS, D_MODEL),
            pl.BlockSpec((1, D_MODEL), lambda i: (0, 0)),
        ],
        out_specs=_tile_rows(TOK_TILE, D_MODEL),
        out_shape=jax.ShapeDtypeStruct((T, D_MODEL), jnp.float32),
        compiler_params=pltpu.CompilerParams(
            dimension_semantics=("arbitrary",), vmem_limit_bytes=VMEM_LIMIT),
        name="combine",
    )(xmid, rinfo, ys_local, g)


def _pair_heads(a, axis):
    shape = a.shape
    split = shape[:axis] + (SWA_KV_HEADS, SWA_REP, HEAD_DIM) + shape[axis + 1:]
    return jnp.swapaxes(a.reshape(split), axis, axis + 1).reshape(shape)


def _in_proj_weight(w):
    off_bq = 3 * D_A
    return jnp.concatenate(
        [w[..., 0:D_A] * QK_SCALE, _pair_heads(w[..., off_bq:off_bq + D_B], w.ndim - 1) * QK_SCALE,
         w[..., D_A:off_bq], w[..., off_bq + D_B:]], axis=-1).astype(jnp.bfloat16)


def _out_proj_weight(w):
    return jnp.concatenate(
        [w[:, 0:D_A], _pair_heads(w[:, D_A:D_A + D_B], 1), w[:, D_A + D_B:]], axis=1).astype(jnp.bfloat16)


def _na_variants_kernel(full_ref, out_ref):
    for k in range(NA_ROWS):
        lo = (NA_ROWS - 1 - k) * GRID_W
        out_ref[k] = full_ref[:, lo:lo + NA_ROWS * GRID_W]


def _na_bias_table(rel_bias):
    depth = rel_bias.shape[0]
    c = np.arange(GRID_W)[:, None]
    cp = np.arange(GRID_W)[None, :]
    cs = np.clip(c - NA_COLS // 2, 0, GRID_W - NA_COLS)
    valid = (cp >= cs) & (cp < cs + NA_COLS)
    d = np.arange(2 * NA_COLS - 1)[:, None, None]
    col_sel = ((cp - c + (NA_COLS - 1))[None] == d) & valid[None]
    full = jnp.einsum("lhrd,dcm->lhcrm", rel_bias.astype(jnp.float32), jnp.asarray(col_sel, jnp.float32),
                      precision=lax.Precision.HIGHEST)
    full = jnp.where(jnp.asarray(valid)[None, None, :, None, :], full * LOG2E, NEG)
    n_off = 2 * NA_ROWS - 1
    full = full.reshape(depth, NA_HEADS * GRID_W, n_off * GRID_W)
    return pl.pallas_call(
        _na_variants_kernel,
        grid=(depth,),
        in_specs=[pl.BlockSpec((None, NA_HEADS * GRID_W, n_off * GRID_W), lambda l: (l, 0, 0))],
        out_specs=pl.BlockSpec((None, NA_ROWS, NA_HEADS * GRID_W, NA_ROWS * GRID_W), lambda l: (l, 0, 0, 0)),
        out_shape=jax.ShapeDtypeStruct((depth, NA_ROWS, NA_HEADS * GRID_W, NA_ROWS * GRID_W), jnp.float32),
        compiler_params=pltpu.CompilerParams(
            dimension_semantics=("arbitrary",), vmem_limit_bytes=VMEM_LIMIT),
        name="na_bias_variants",
    )(full)


def _swa_bias_table():
    slopes = (2.0 ** (-8.0 * np.arange(1, SWA_Q_HEADS + 1) / SWA_Q_HEADS)).astype(np.float32)
    qi = np.arange(SWA_BLOCK)[:, None]
    ki = np.arange(3 * SWA_BLOCK)[None, :]
    variants = []
    for v in range(3):
        dist = np.abs(ki - qi - v * SWA_BLOCK).astype(np.float32)
        tab = np.where(dist <= SWA_WINDOW, -slopes[:, None, None] * dist[None] * LOG2E, np.float32(NEG))
        variants.append(tab.reshape(SWA_Q_HEADS * SWA_BLOCK, 3 * SWA_BLOCK).astype(np.float32))
    return jnp.asarray(np.stack(variants))


def _block_diag(pool_w):
    depth, n = pool_w.shape[0:2]
    eye = jnp.asarray(np.eye(n, dtype=np.float32))
    out = pool_w[:, :, :, None, :] * eye[None, :, None, :, None]
    return out.reshape(depth, n * POOL_GROUP_DIM, n * POOL_GROUP_DIM)


def _router_weights(rg_w, rg_b, re_w, re_b):
    def lanes(g, e):
        gap = jnp.zeros(g.shape[:-1] + (ROUTE_LANE0 - N_GROUPS,), jnp.float32)
        tail = jnp.zeros(g.shape[:-1] + (LANES - ROUTE_LANE0 - N_EXPERTS,), jnp.float32)
        return jnp.concatenate([g.astype(jnp.float32), gap, e.astype(jnp.float32), tail], axis=-1)

    return lanes(rg_w, re_w).astype(jnp.bfloat16), lanes(rg_b, re_b)[:, None, :]


def _dispatch_tables(cnt, n_tiles):
    nb = cnt.shape[0] // SUBLANES
    n = cnt.reshape(nb, SUBLANES, LANES)[:, 0, ROUTE_LANE0:ROUTE_LANE0 + N_EXPERTS].astype(jnp.int32)
    g = (n + (CHUNK - 1)) // CHUNK
    l_end = jnp.cumsum(g, axis=1)
    l_off = l_end - g
    c_end = jnp.cumsum(g, axis=0)
    c_off = c_end - g
    tot = c_end[-1]
    tiles = (tot + (TILE_CHUNKS - 1)) // TILE_CHUNKS
    t_end = jnp.cumsum(tiles)
    t_off = t_end - tiles
    n_used = t_end[-1:]

    experts = jnp.arange(N_EXPERTS, dtype=jnp.int32)
    tile_ids = jnp.arange(n_tiles, dtype=jnp.int32)
    tile_expert = jnp.minimum(jnp.sum((t_end[None, :] <= tile_ids[:, None]).astype(jnp.int32), axis=1),
                              N_EXPERTS - 1)
    oh_te = (tile_expert[:, None] == experts[None, :]).astype(jnp.int32)
    left = jnp.sum(oh_te * (tot + t_off * TILE_CHUNKS)[None, :], axis=1) - tile_ids * TILE_CHUNKS
    n_valid = jnp.clip(left, 0, TILE_CHUNKS)
    has_rows = tiles > 0
    first = ((tile_ids == jnp.sum(oh_te * t_off[None, :], axis=1)) & (n_valid > 0)).astype(jnp.int32)
    wslot = jnp.sum(oh_te * ((jnp.cumsum(has_rows.astype(jnp.int32)) - 1) % 2)[None, :], axis=1)
    later = (experts[None, :] > experts[:, None]) & has_rows[None, :]
    nxt = jnp.min(jnp.where(later, experts[None, :], N_EXPERTS), axis=1)
    next_expert = jnp.sum(oh_te * jnp.where(nxt < N_EXPERTS, nxt, -1)[None, :], axis=1)
    tile_tables = (tile_expert, n_valid // SUB_CHUNKS, n_valid % SUB_CHUNKS, first, wslot, next_expert)

    q = jnp.arange(n_tiles * TILE_CHUNKS, dtype=jnp.int32)
    tile_q = q // TILE_CHUNKS
    oh_e = (jnp.repeat(tile_expert, TILE_CHUNKS)[:, None] == experts[None, :]).astype(jnp.int32)
    ro = q - jnp.sum(oh_e * t_off[None, :], axis=1) * TILE_CHUNKS
    valid = (ro < jnp.sum(oh_e * tot[None, :], axis=1)) & (tile_q < n_used[0])
    cols = jnp.dot(jnp.concatenate([c_end, c_off, l_off], axis=0).astype(jnp.float32),
                   oh_e.T.astype(jnp.float32), precision=lax.Precision.HIGHEST).astype(jnp.int32)
    c_end_q, c_off_q, l_off_q = cols[0:nb], cols[nb:2 * nb], cols[2 * nb:3 * nb]
    b_q = jnp.minimum(jnp.sum((c_end_q <= ro[None, :]).astype(jnp.int32), axis=0), nb - 1)
    oh_b = (jnp.arange(nb, dtype=jnp.int32)[:, None] == b_q[None, :]).astype(jnp.int32)
    src = b_q * LOCAL_CHUNKS + jnp.sum(oh_b * (l_off_q + ro[None, :] - c_off_q), axis=0)
    chunk_src = jnp.where(valid, src, LOCAL_CHUNKS - 1)

    return tile_tables, chunk_src


def kernel(x, norm1_g, w_in, nat_bias, swa_sink, pool_w, pool_scale, w_out, norm2_g, router_g_w,
           router_g_b, router_e_w, router_e_b, expert_w_gate, expert_w_up, expert_w_down, final_g):
    batch, seq_len, _ = x.shape
    depth = w_in.shape[0]
    T = batch * seq_len
    assert seq_len % TOK_TILE == 0 and TOK_TILE % SWA_BLOCK == 0 and TOK_TILE % GRID_W == 0
    assert seq_len // TOK_TILE >= WIN_TILES and seq_len // SWA_BLOCK >= 2 and POOL_WINDOWS == (2, 4, 8, 16)
    max_chunks = (2 * T) // CHUNK + (T // TOK_TILE) * N_EXPERTS
    n_tiles = max_chunks // TILE_CHUNKS + N_EXPERTS

    swb = _swa_bias_table()
    tri = jnp.asarray(np.tril(np.ones((TOK_TILE, TOK_TILE), np.float32), -1)).astype(jnp.bfloat16)
    utri = jnp.asarray(np.triu(np.ones((LANES, LANES), np.float32), 1)).astype(jnp.bfloat16)

    w_in_k = _in_proj_weight(w_in)
    w_out_k = _out_proj_weight(w_out)
    nab = _na_bias_table(nat_bias)
    sinkcol = jnp.broadcast_to((swa_sink.astype(jnp.float32) * LOG2E)[:, :, None, None],
                               (depth, SWA_Q_HEADS, SWA_BLOCK, LANES)).reshape(depth, SWA_Q_HEADS * SWA_BLOCK, LANES)
    poolw = _block_diag(pool_w).astype(jnp.bfloat16)
    pools = pool_scale.reshape(depth, 1, D_C).astype(jnp.float32)
    rw, rb = _router_weights(router_g_w, router_g_b, router_e_w, router_e_b)
    g1 = norm1_g.reshape(depth, 1, D_MODEL)
    g2 = norm2_g.reshape(depth, 1, D_MODEL)

    x2 = x.reshape(T, D_MODEL)
    qp, kv = _norm_proj(0, x2, g1, w_in_k)
    for l in range(depth):
        xmid, xs_local, rinfo, cnt = _mixer(
            l, x2, qp, kv, (nab, sinkcol, poolw, pools, w_out_k, g2, rw, rb), (swb, tri, utri),
            seq_len=seq_len)
        tile_tables, chunk_src = _dispatch_tables(cnt, n_tiles)
        ys_local = _experts(l, tile_tables, chunk_src, xs_local, expert_w_gate, expert_w_up, expert_w_down)
        if l + 1 < depth:
            x2, qp, kv = _combine_proj(l + 1, xmid, rinfo, ys_local, g1, w_in_k)
        else:
            x2 = _combine_final(xmid, rinfo, ys_local, final_g.reshape(1, D_MODEL))
    return x2.reshape(batch, seq_len, D_MODEL)
```

```python
import functools

import jax
import jax.numpy as jnp
import numpy as np
from jax import lax
from jax.experimental import pallas as pl
from jax.experimental.pallas import tpu as pltpu

D_MODEL = 1024
GRID_W = 64
HEAD_DIM = 64
NA_HEADS = 4
NA_ROWS = 8
NA_COLS = 16
SWA_Q_HEADS = 8
SWA_KV_HEADS = 2
SWA_REP = SWA_Q_HEADS // SWA_KV_HEADS
SWA_WINDOW = 128
SWA_BLOCK = 128
POOL_WINDOWS = (2, 4, 8, 16)
POOL_GROUP_DIM = 64
D_A = NA_HEADS * HEAD_DIM
D_B = SWA_Q_HEADS * HEAD_DIM
D_BKV = SWA_KV_HEADS * HEAD_DIM
D_C = len(POOL_WINDOWS) * POOL_GROUP_DIM
D_MIX = D_A + D_B + D_C
D_QP = D_A + D_B
D_KV = 2 * D_A + 2 * D_BKV + D_C
D_IN = D_QP + D_KV
N_GROUPS = 4
EXPERTS_PER_GROUP = 8
N_EXPERTS = N_GROUPS * EXPERTS_PER_GROUP
D_EXPERT = 256
RMS_EPS = 1e-6
NEG = -1e30
LOG2E = 1.4426950408889634
QK_SCALE = HEAD_DIM ** -0.5 * LOG2E

LANES = 128
SUBLANES = 8

TOK_TILE = 512
EXP_SUB = 256
EXP_SUBS = 4
EXP_TILE = EXP_SUB * EXP_SUBS
CHUNK = 16
LOCAL_CHUNKS = (2 * TOK_TILE + N_EXPERTS * (CHUNK - 1)) // CHUNK + 2
LOCAL_ROWS = LOCAL_CHUNKS * CHUNK
SUB_CHUNKS = EXP_SUB // CHUNK
TILE_CHUNKS = EXP_TILE // CHUNK
GATHER_AHEAD = 2
GATHER_SLOTS = GATHER_AHEAD + 1
ROUTE_LANE0 = 8
INFO_ROW1, INFO_ROW2, INFO_GATE1, INFO_GATE2 = 0, 1, 2, 3
NA_ROWS_PER_STEP = 8
SWA_BLOCKS_PER_STEP = 4
HALO = max(POOL_WINDOWS) // 2
WIN_TILES = 3
VMEM_LIMIT = 56 * 1024 * 1024


def _rmsnorm_f32(x, g):
    return x * lax.rsqrt(jnp.mean(x * x, axis=-1, keepdims=True) + RMS_EPS) * g


def _norm_proj_kernel(x_ref, g_ref, w_ref, qp_ref, kv_ref):
    xn = _rmsnorm_f32(x_ref[...], g_ref[...]).astype(jnp.bfloat16)
    proj = jnp.dot(xn, w_ref[...], preferred_element_type=jnp.float32)
    qp_ref[...] = proj[:, :D_QP].astype(jnp.bfloat16)
    kv_ref[...] = proj[:, D_QP:].astype(jnp.bfloat16)


def _norm_proj(layer, x2, g, w):
    T = x2.shape[0]
    return pl.pallas_call(
        _norm_proj_kernel,
        grid=(T // TOK_TILE,),
        in_specs=[
            pl.BlockSpec((TOK_TILE, D_MODEL), lambda i: (i, 0)),
            pl.BlockSpec((None, 1, D_MODEL), lambda i: (layer, 0, 0)),
            pl.BlockSpec((None, D_MODEL, D_IN), lambda i: (layer, 0, 0)),
        ],
        out_specs=[
            pl.BlockSpec((TOK_TILE, D_QP), lambda i: (i, 0)),
            pl.BlockSpec((TOK_TILE, D_KV), lambda i: (i, 0)),
        ],
        out_shape=[
            jax.ShapeDtypeStruct((T, D_QP), jnp.bfloat16),
            jax.ShapeDtypeStruct((T, D_KV), jnp.bfloat16),
        ],
        compiler_params=pltpu.CompilerParams(
            dimension_semantics=("arbitrary",), vmem_limit_bytes=VMEM_LIMIT),
        name="norm_proj",
    )(x2, g, w)


KW_AK, KW_AV, KW_BK, KW_BV = 0, D_A, 2 * D_A, 2 * D_A + D_BKV
KW_COLS = 2 * D_A + 2 * D_BKV
KV_CU = KW_COLS


def _mixer_kernel(x_ref, qp_ref, kwin_ref, nab_ref, swb_ref, sink_ref,
                  poolw_ref, pools_ref, wout_ref, g2_ref, rw_ref, rb_ref, tri_ref, utri_ref,
                  xmid_ref, xs_ref, rinfo_ref, cnt_ref,
                  uwin, mix, xn_scr, logit_scr, wout_b, *, seq_len):
    t = pl.program_id(0)
    nblk = seq_len // TOK_TILE
    i = jnp.minimum(t, pl.num_programs(0) - 2) % nblk
    rows_per_tile = TOK_TILE // GRID_W
    grid_rows = seq_len // GRID_W

    @pl.when(t == 0)
    def _():
        xn_scr[...] = jnp.zeros_like(xn_scr)
        logit_scr[...] = jnp.zeros_like(logit_scr)
        wout_b[0:D_A, :] = wout_ref[0:D_A, :].astype(jnp.bfloat16)
        for rep in range(SWA_REP):
            for g in range(SWA_KV_HEADS):
                dst = D_A + (rep * SWA_KV_HEADS + g) * HEAD_DIM
                src = D_A + (g * SWA_REP + rep) * HEAD_DIM
                wout_b[dst:dst + HEAD_DIM, :] = wout_ref[src:src + HEAD_DIM, :].astype(jnp.bfloat16)
        wout_b[D_A + D_B:D_MIX, :] = wout_ref[D_A + D_B:D_MIX, :].astype(jnp.bfloat16)

    cur_off = pl.multiple_of((i - jnp.clip(i - 1, 0, nblk - WIN_TILES)) * TOK_TILE, TOK_TILE)

    lane_a = lax.broadcasted_iota(jnp.int32, (GRID_W, D_A), 1) // HEAD_DIM

    def na_row(rr):
        r = i * rows_per_tile + rr
        rs = jnp.clip(r - NA_ROWS // 2, 0, grid_rows - NA_ROWS)
        variant = r - rs
        start = pl.multiple_of(cur_off + (rs - i * rows_per_tile) * GRID_W, GRID_W)
        q0 = pl.multiple_of(rr * GRID_W, GRID_W)
        q = qp_ref[pl.ds(q0, GRID_W), 0:D_A]
        zero = jnp.zeros_like(q)
        qs = jnp.concatenate([jnp.where(lane_a == h, q, zero) for h in range(NA_HEADS)], axis=0)
        kw = kwin_ref[pl.ds(start, NA_ROWS * GRID_W), KW_AK:KW_AK + D_A]
        vw = kwin_ref[pl.ds(start, NA_ROWS * GRID_W), KW_AV:KW_AV + D_A]
        s = lax.dot_general(qs, kw, (((1,), (1,)), ((), ())), preferred_element_type=jnp.float32)
        s = s + nab_ref[variant]
        m = jnp.max(s, axis=-1, keepdims=True)
        p = jnp.exp2(s - m)
        l = jnp.sum(p, axis=-1, keepdims=True)
        pv = jnp.dot(p.astype(jnp.bfloat16), vw, preferred_element_type=jnp.float32)
        pv = pv * (1.0 / l)
        o = jnp.zeros((GRID_W, D_A), jnp.float32)
        for h in range(NA_HEADS):
            o = o + jnp.where(lane_a == h, pv[h * GRID_W:(h + 1) * GRID_W, :], 0.0)
        mix[pl.ds(q0, GRID_W), 0:D_A] = o.astype(jnp.bfloat16)

    def na_step(it, c):
        for k in range(NA_ROWS_PER_STEP):
            na_row(it * NA_ROWS_PER_STEP + k)
        return c

    lax.fori_loop(0, rows_per_tile // NA_ROWS_PER_STEP, na_step, 0)

    lane_b = lax.broadcasted_iota(jnp.int32, (SWA_BLOCK, LANES), 1) // HEAD_DIM
    blocks_per_tile = TOK_TILE // SWA_BLOCK
    nblocks = seq_len // SWA_BLOCK

    ones_v = jnp.ones((3 * SWA_BLOCK, LANES), jnp.bfloat16)

    def swa_step(sb, c):
        n = i * blocks_per_tile + sb
        variant = jnp.where(n == 0, 0, jnp.where(n == nblocks - 1, 2, 1))
        q0 = pl.multiple_of(sb * SWA_BLOCK, SWA_BLOCK)
        k0 = pl.multiple_of(cur_off + (sb - variant) * SWA_BLOCK, SWA_BLOCK)
        kw = kwin_ref[pl.ds(k0, 3 * SWA_BLOCK), KW_BK:KW_BK + D_BKV]
        vaug = jnp.concatenate([kwin_ref[pl.ds(k0, 3 * SWA_BLOCK), KW_BV:KW_BV + D_BKV], ones_v], axis=1)
        outs = []
        for g in range(SWA_KV_HEADS):
            pieces = []
            for t in range(SWA_REP):
                qt = qp_ref[pl.ds(q0, SWA_BLOCK), D_A + t * LANES:D_A + (t + 1) * LANES]
                pieces.append(jnp.where(lane_b == g, qt, jnp.zeros_like(qt)))
            qs = jnp.concatenate(pieces, axis=0)
            r0 = g * SWA_REP * SWA_BLOCK
            s = lax.dot_general(qs, kw, (((1,), (1,)), ((), ())), preferred_element_type=jnp.float32)
            s = s + swb_ref[variant, r0:r0 + SWA_REP * SWA_BLOCK, :]
            sink = sink_ref[r0:r0 + SWA_REP * SWA_BLOCK, :]
            m = jnp.broadcast_to(jnp.max(s, axis=-1, keepdims=True), sink.shape)
            m = jnp.maximum(m, sink)
            p = jnp.exp2(s - jnp.concatenate([m, m, m], axis=1)).astype(jnp.bfloat16)
            pv = jnp.dot(p, vaug, preferred_element_type=jnp.float32)
            l = pv[:, LANES:2 * LANES] + jnp.exp2(sink - m)
            outs.append(pv[:, 0:LANES] * (1.0 / l))
        for t in range(SWA_REP):
            o0 = outs[0][t * SWA_BLOCK:(t + 1) * SWA_BLOCK, :]
            o1 = outs[1][t * SWA_BLOCK:(t + 1) * SWA_BLOCK, :]
            ot = jnp.where(lane_b == 0, o0, o1)
            mix[pl.ds(q0, SWA_BLOCK), D_A + t * LANES:D_A + (t + 1) * LANES] = ot.astype(jnp.bfloat16)
        return c

    def swa_pair(it, c):
        for k in range(SWA_BLOCKS_PER_STEP):
            swa_step(it * SWA_BLOCKS_PER_STEP + k, c)
        return c

    lax.fori_loop(0, blocks_per_tile // SWA_BLOCKS_PER_STEP, swa_pair, 0)

    u = kwin_ref[pl.ds(cur_off, TOK_TILE), KV_CU:KV_CU + D_C].astype(jnp.float32)
    prev_ok = (i > 0).astype(jnp.float32)
    next_ok = (i < nblk - 1).astype(jnp.float32)
    before = pl.multiple_of(jnp.maximum(cur_off - CHUNK, 0), CHUNK)
    after = pl.multiple_of(jnp.minimum(cur_off + TOK_TILE, WIN_TILES * TOK_TILE - CHUNK), CHUNK)
    u_before = kwin_ref[pl.ds(before, CHUNK), KV_CU:KV_CU + D_C].astype(jnp.float32)
    u_after = kwin_ref[pl.ds(after, CHUNK), KV_CU:KV_CU + D_C].astype(jnp.float32)
    uwin[0:HALO, :] = u_before[CHUNK - HALO:CHUNK, :] * prev_ok
    uwin[HALO:HALO + TOK_TILE, :] = u
    uwin[HALO + TOK_TILE:2 * HALO + TOK_TILE, :] = u_after[0:HALO, :] * next_ok
    n_ext = TOK_TILE + 2 * HALO
    a2 = uwin[0:n_ext - 1, :] + uwin[1:n_ext, :]
    a4 = a2[0:n_ext - 3, :] + a2[2:n_ext - 1, :]
    a8 = a4[0:n_ext - 7, :] + a4[4:n_ext - 3, :]
    a16 = a8[0:n_ext - 15, :] + a8[8:n_ext - 7, :]
    w2 = a2[7:7 + TOK_TILE, :]
    w4 = a4[6:6 + TOK_TILE, :]
    w8 = a8[4:4 + TOK_TILE, :]
    w16 = a16[0:TOK_TILE, :]
    lane_c = lax.broadcasted_iota(jnp.int32, (TOK_TILE, D_C), 1) // POOL_GROUP_DIM
    pooled = jnp.where(lane_c == 0, w2, jnp.where(lane_c == 1, w4, jnp.where(lane_c == 2, w8, w16)))
    half = jnp.where(lane_c == 0, 1, jnp.where(lane_c == 1, 2, jnp.where(lane_c == 2, 4, 8)))
    pos = i * TOK_TILE + lax.broadcasted_iota(jnp.int32, (TOK_TILE, D_C), 0)
    cnt = (jnp.minimum(pos + half, seq_len) - jnp.maximum(pos - half, 0)).astype(jnp.float32)
    d = (pooled / cnt - u).astype(jnp.bfloat16)
    oc = jnp.dot(d, poolw_ref[...], preferred_element_type=jnp.float32) * pools_ref[...]
    mix[:, D_A + D_B:D_MIX] = oc.astype(jnp.bfloat16)

    xn = xn_scr[...]
    logits = logit_scr[...]

    xm = x_ref[...] + jnp.dot(mix[...], wout_b[...], preferred_element_type=jnp.float32)
    xmid_ref[...] = xm
    xn_new = _rmsnorm_f32(xm, g2_ref[...]).astype(jnp.bfloat16)
    xn_scr[...] = xn_new
    logit_scr[...] = jnp.dot(xn_new, rw_ref[...], preferred_element_type=jnp.float32) + rb_ref[...]

    lane = lax.broadcasted_iota(jnp.int32, (TOK_TILE, LANES), 1).astype(jnp.float32)
    is_g = lane < N_GROUPS
    gl = jnp.where(is_g, logits, NEG)
    gmax = jnp.max(gl, axis=-1, keepdims=True)
    gtop = jnp.min(jnp.where(is_g & (gl == gmax), lane, float(LANES)), axis=-1, keepdims=True)
    gprob = 1.0 / jnp.sum(jnp.exp(gl - gmax), axis=-1, keepdims=True)
    e_lo = ROUTE_LANE0 + gtop * EXPERTS_PER_GROUP
    in_grp = (lane >= e_lo) & (lane < e_lo + EXPERTS_PER_GROUP)
    el = jnp.where(in_grp, logits, NEG)
    m1 = jnp.max(el, axis=-1, keepdims=True)
    i1 = jnp.min(jnp.where(in_grp & (el == m1), lane, float(LANES)), axis=-1, keepdims=True)
    el2 = jnp.where(lane == i1, NEG, el)
    m2 = jnp.max(el2, axis=-1, keepdims=True)
    i2 = jnp.min(jnp.where(in_grp & (lane != i1) & (el2 == m2), lane, float(LANES)), axis=-1, keepdims=True)
    r21 = jnp.exp(m2 - m1)
    gate1 = gprob / (1.0 + r21)
    gate2 = gprob * r21 / (1.0 + r21)

    oh1 = lane == i1
    oh2 = lane == i2
    oh = jnp.where(oh1 | oh2, 1.0, 0.0)
    earlier = jnp.dot(tri_ref[...], oh.astype(jnp.bfloat16), preferred_element_type=jnp.float32)
    n_e = jnp.sum(oh, axis=0, keepdims=True)
    chunks_e = jnp.floor((n_e + (CHUNK - 1)) * (1.0 / CHUNK))
    seg0 = jnp.dot(jnp.broadcast_to(chunks_e, (SUBLANES, LANES)).astype(jnp.bfloat16), utri_ref[...],
                   preferred_element_type=jnp.float32)[0:1, :] * CHUNK
    base = earlier + seg0
    lp1 = jnp.sum(jnp.where(oh1, base, 0.0), axis=-1, keepdims=True)
    lp2 = jnp.sum(jnp.where(oh2, base, 0.0), axis=-1, keepdims=True)
    info = jnp.zeros((TOK_TILE, LANES), jnp.float32)
    for k, col in enumerate((lp1, lp2, gate1, gate2)):
        info = jnp.where(lane == k, col, info)
    rinfo_ref[...] = info
    cnt_ref[...] = jnp.broadcast_to(n_e, cnt_ref.shape)

    info_t = info.T
    prow = lax.broadcasted_iota(jnp.int32, (LOCAL_ROWS, TOK_TILE), 0).astype(jnp.float32)
    sel = jnp.where((prow == info_t[INFO_ROW1:INFO_ROW1 + 1, :]) | (prow == info_t[INFO_ROW2:INFO_ROW2 + 1, :]),
                    1.0, 0.0).astype(jnp.bfloat16)
    moved = jnp.dot(sel, xn, preferred_element_type=jnp.float32)
    xs_ref[...] = moved.astype(jnp.bfloat16)


def _mixer(layer, x2, qp, kv, layer_params, shared_tables, *, seq_len):
    nab, sinkcol, poolw, pools, wout, g2, rw, rb = layer_params
    swb, tri, utri = shared_tables
    T = x2.shape[0]
    nblk = seq_len // TOK_TILE
    n_tok_tiles = T // TOK_TILE

    def mixed(t):
        return jnp.minimum(t, n_tok_tiles - 1)

    def cur(t):
        return (mixed(t), 0)

    def window(t):
        seq0 = (mixed(t) // nblk) * nblk
        return (jnp.clip(mixed(t) - 1, seq0, seq0 + nblk - WIN_TILES) * TOK_TILE, 0)

    def dispatched(t):
        return (jnp.maximum(t - 1, 0), 0)

    def resident(a):
        zeros = (0,) * a.ndim
        return pl.BlockSpec(a.shape, lambda t: zeros, pipeline_mode=pl.Buffered(1))

    def resident_layer(a):
        index = (layer,) + (0,) * (a.ndim - 1)
        return pl.BlockSpec((None,) + a.shape[1:], lambda t: index, pipeline_mode=pl.Buffered(1))

    return pl.pallas_call(
        functools.partial(_mixer_kernel, seq_len=seq_len),
        grid=(n_tok_tiles + 1,),
        in_specs=[
            pl.BlockSpec((TOK_TILE, D_MODEL), cur),
            pl.BlockSpec((TOK_TILE, D_QP), cur),
            pl.BlockSpec((pl.Element(WIN_TILES * TOK_TILE), pl.Element(D_KV)), window),
            resident_layer(nab), resident(swb), resident_layer(sinkcol), resident_layer(poolw),
            resident_layer(pools), resident_layer(wout), resident_layer(g2), resident_layer(rw),
            resident_layer(rb), resident(tri), resident(utri),
        ],
        out_specs=[
            pl.BlockSpec((TOK_TILE, D_MODEL), cur),
            pl.BlockSpec((LOCAL_ROWS, D_MODEL), dispatched),
            pl.BlockSpec((TOK_TILE, LANES), dispatched),
            pl.BlockSpec((SUBLANES, LANES), dispatched),
        ],
        out_shape=[
            jax.ShapeDtypeStruct((T, D_MODEL), jnp.float32),
            jax.ShapeDtypeStruct((n_tok_tiles * LOCAL_ROWS, D_MODEL), jnp.bfloat16),
            jax.ShapeDtypeStruct((T, LANES), jnp.float32),
            jax.ShapeDtypeStruct((n_tok_tiles * SUBLANES, LANES), jnp.float32),
        ],
        scratch_shapes=[
            pltpu.VMEM((TOK_TILE + 2 * HALO, D_C), jnp.float32),
            pltpu.VMEM((TOK_TILE, D_MIX), jnp.bfloat16),
            pltpu.VMEM((TOK_TILE, D_MODEL), jnp.bfloat16),
            pltpu.VMEM((TOK_TILE, LANES), jnp.float32),
            pltpu.VMEM((D_MIX, D_MODEL), jnp.bfloat16),
        ],
        compiler_params=pltpu.CompilerParams(
            dimension_semantics=("arbitrary",), vmem_limit_bytes=VMEM_LIMIT),
        name="mixer",
    )(x2, qp, kv, nab, swb, sinkcol, poolw, pools, wout, g2, rw, rb, tri, utri)


def _chunk_copy(src_hbm, src_chunk, dst, dst_chunk, sem):
    return pltpu.make_async_copy(
        src_hbm.at[pl.ds(pl.multiple_of(src_chunk * CHUNK, CHUNK), CHUNK)],
        dst.at[pl.ds(dst_chunk * CHUNK, CHUNK)],
        sem)


def _expert_kernel(te_ref, nfull_ref, npart_ref, first_ref, wslot_ref, nexte_ref, csrc_ref,
                   xs_hbm, wg_hbm, wu_hbm, wd_hbm, ys_hbm,
                   xbuf, obuf, wgf, wuf, wdf, wgb, wub, wdb, sem, osem, psem, wsem, *, layer):
    j = pl.program_id(0)
    nt = pl.num_programs(0)
    slot = j % GATHER_SLOTS
    oslot = j % 2

    def nsub(tile):
        return nfull_ref[tile] + (npart_ref[tile] > 0).astype(jnp.int32)

    def scatter_full(tile, os, h):
        return [pltpu.make_async_copy(
            obuf.at[os, pl.ds(c * CHUNK, CHUNK)],
            ys_hbm.at[pl.ds(pl.multiple_of(csrc_ref[tile * TILE_CHUNKS + c] * CHUNK, CHUNK), CHUNK)],
            osem.at[os, h]) for c in range(h * SUB_CHUNKS, (h + 1) * SUB_CHUNKS)]

    def scatter_part(tile, os, c):
        idx = nfull_ref[tile] * SUB_CHUNKS + c
        return pltpu.make_async_copy(
            obuf.at[os, pl.ds(pl.multiple_of(idx * CHUNK, CHUNK), CHUNK)],
            ys_hbm.at[pl.ds(pl.multiple_of(csrc_ref[tile * TILE_CHUNKS + idx] * CHUNK, CHUNK), CHUNK)],
            psem.at[os])

    def start_scatters(tile, os):
        for h in range(EXP_SUBS):
            @pl.when(h < nfull_ref[tile])
            def _():
                for cp in scatter_full(tile, os, h):
                    cp.start()

        @pl.when(npart_ref[tile] > 0)
        def _():
            for c in range(SUB_CHUNKS):
                @pl.when(c < npart_ref[tile])
                def _():
                    scatter_part(tile, os, c).start()

    def wait_scatters(tile, os):
        for h in range(EXP_SUBS):
            @pl.when(h < nfull_ref[tile])
            def _():
                pltpu.make_async_copy(obuf.at[os, pl.ds(0, EXP_SUB)], ys_hbm.at[pl.ds(0, EXP_SUB)],
                                      osem.at[os, h]).wait()

        @pl.when(npart_ref[tile] > 0)
        def _():
            for c in range(SUB_CHUNKS):
                @pl.when(c < npart_ref[tile])
                def _():
                    scatter_part(tile, os, c).wait()

    def start_gather(tile, s, h):
        for c in range(h * SUB_CHUNKS, (h + 1) * SUB_CHUNKS):
            _chunk_copy(xs_hbm, csrc_ref[tile * TILE_CHUNKS + c], xbuf.at[s], c, sem.at[s, h]).start()

    def wait_gather(s, h):
        rows = pl.ds(h * EXP_SUB, EXP_SUB)
        pltpu.make_async_copy(xs_hbm.at[pl.ds(0, EXP_SUB)], xbuf.at[s, rows], sem.at[s, h]).wait()

    def weight_copies(expert, ws):
        return [pltpu.make_async_copy(w_hbm.at[layer, expert], wbuf.at[ws], wsem.at[k, ws])
                for k, (w_hbm, wbuf) in enumerate(((wg_hbm, wgf), (wu_hbm, wuf), (wd_hbm, wdf)))]

    for h in range(EXP_SUBS):
        for first_tile in range(GATHER_AHEAD):
            @pl.when((j == 0) & (h < nsub(first_tile)))
            def _():
                start_gather(first_tile, first_tile, h)

        ahead = j + GATHER_AHEAD

        @pl.when(h < nsub(jnp.minimum(ahead, nt - 1)) * (ahead < nt).astype(jnp.int32))
        def _():
            start_gather(ahead, ahead % GATHER_SLOTS, h)

    ws = wslot_ref[j]

    @pl.when(j == 0)
    def _():
        for cp in weight_copies(te_ref[0], 0):
            cp.start()

    @pl.when(first_ref[j] > 0)
    def _():
        for cp in weight_copies(te_ref[j], ws):
            cp.wait()

        @pl.when(nexte_ref[j] >= 0)
        def _():
            for cp in weight_copies(nexte_ref[j], 1 - ws):
                cp.start()

        wgb[...] = wgf[ws].astype(jnp.bfloat16)
        wub[...] = wuf[ws].astype(jnp.bfloat16)
        wdb[...] = wdf[ws].astype(jnp.bfloat16)

    def gated_mlp(n_rows):
        xs = xbuf[slot, 0:n_rows, :]
        gate = jnp.dot(xs, wgb[...], preferred_element_type=jnp.float32)
        up = jnp.dot(xs, wub[...], preferred_element_type=jnp.float32)
        act = (gate * (1.0 / (1.0 + jnp.exp(-gate))) * up).astype(jnp.bfloat16)
        return jnp.dot(act, wdb[...], preferred_element_type=jnp.float32).astype(jnp.bfloat16)

    @pl.when(j >= 2)
    def _():
        wait_scatters(j - 2, oslot)

    for k in range(1, EXP_SUBS + 1):
        @pl.when(nsub(j) == k)
        def _():
            for h in range(k):
                wait_gather(slot, h)
            obuf[oslot, 0:k * EXP_SUB, :] = gated_mlp(k * EXP_SUB)

    start_scatters(j, oslot)

    @pl.when(j == nt - 1)
    def _():
        wait_scatters(j - 1, 1 - oslot)
        wait_scatters(j, oslot)


def _experts(layer, tile_tables, chunk_src, xs_local, wg, wu, wd):
    tile_expert, n_full, n_part, first, wslot, next_expert = tile_tables
    n_tiles = tile_expert.shape[0]
    any_space = pl.BlockSpec(memory_space=pl.ANY)
    n_prefetch = 7

    return pl.pallas_call(
        functools.partial(_expert_kernel, layer=layer),
        grid_spec=pltpu.PrefetchScalarGridSpec(
            num_scalar_prefetch=n_prefetch,
            grid=(n_tiles,),
            in_specs=[any_space] * 4,
            out_specs=any_space,
            scratch_shapes=[
                pltpu.VMEM((GATHER_SLOTS, EXP_TILE, D_MODEL), jnp.bfloat16),
                pltpu.VMEM((2, EXP_TILE, D_MODEL), jnp.bfloat16),
                pltpu.VMEM((2, D_MODEL, D_EXPERT), jnp.float32),
                pltpu.VMEM((2, D_MODEL, D_EXPERT), jnp.float32),
                pltpu.VMEM((2, D_EXPERT, D_MODEL), jnp.float32),
                pltpu.VMEM((D_MODEL, D_EXPERT), jnp.bfloat16),
                pltpu.VMEM((D_MODEL, D_EXPERT), jnp.bfloat16),
                pltpu.VMEM((D_EXPERT, D_MODEL), jnp.bfloat16),
                pltpu.SemaphoreType.DMA((GATHER_SLOTS, EXP_SUBS)),
                pltpu.SemaphoreType.DMA((2, EXP_SUBS)),
                pltpu.SemaphoreType.DMA((2,)),
                pltpu.SemaphoreType.DMA((3, 2)),
            ],
        ),
        out_shape=jax.ShapeDtypeStruct(xs_local.shape, xs_local.dtype),
        input_output_aliases={n_prefetch: 0},
        compiler_params=pltpu.CompilerParams(
            dimension_semantics=("arbitrary",), vmem_limit_bytes=VMEM_LIMIT),
        name="experts",
    )(tile_expert, n_full, n_part, first, wslot, next_expert, chunk_src, xs_local, wg, wu, wd)


def _combine_tile(xmid_ref, rinfo_ref, ys_ref):
    info = rinfo_ref[...]
    pcol = lax.broadcasted_iota(jnp.int32, (TOK_TILE, LOCAL_ROWS), 1).astype(jnp.float32)
    pick = jnp.where(pcol == info[:, INFO_ROW1:INFO_ROW1 + 1], info[:, INFO_GATE1:INFO_GATE1 + 1],
                     jnp.where(pcol == info[:, INFO_ROW2:INFO_ROW2 + 1], info[:, INFO_GATE2:INFO_GATE2 + 1], 0.0)
                     ).astype(jnp.bfloat16)
    return xmid_ref[...] + jnp.dot(pick, ys_ref[...], preferred_element_type=jnp.float32)


def _combine_final_kernel(xmid_ref, rinfo_ref, ys_ref, g_ref, out_ref):
    x = _combine_tile(xmid_ref, rinfo_ref, ys_ref)
    out_ref[...] = _rmsnorm_f32(x, g_ref[...])


def _combine_proj_kernel(xmid_ref, rinfo_ref, ys_ref, g_ref, w_ref, x_ref, qp_ref, kv_ref):
    x = _combine_tile(xmid_ref, rinfo_ref, ys_ref)
    x_ref[...] = x
    xn = _rmsnorm_f32(x, g_ref[...]).astype(jnp.bfloat16)
    proj = jnp.dot(xn, w_ref[...], preferred_element_type=jnp.float32)
    qp_ref[...] = proj[:, :D_QP].astype(jnp.bfloat16)
    kv_ref[...] = proj[:, D_QP:].astype(jnp.bfloat16)


def _tile_rows(rows, cols):
    return pl.BlockSpec((rows, cols), lambda i: (i, 0))


def _combine_proj(next_layer, xmid, rinfo, ys_local, g1, w_in_k):
    T = xmid.shape[0]
    return pl.pallas_call(
        _combine_proj_kernel,
        grid=(T // TOK_TILE,),
        in_specs=[
            _tile_rows(TOK_TILE, D_MODEL), _tile_rows(TOK_TILE, LANES), _tile_rows(LOCAL_ROWS, D_MODEL),
            pl.BlockSpec((None, 1, D_MODEL), lambda i: (next_layer, 0, 0)),
            pl.BlockSpec((None, D_MODEL, D_IN), lambda i: (next_layer, 0, 0), pipeline_mode=pl.Buffered(1)),
        ],
        out_specs=[_tile_rows(TOK_TILE, D_MODEL), _tile_rows(TOK_TILE, D_QP), _tile_rows(TOK_TILE, D_KV)],
        out_shape=[
            jax.ShapeDtypeStruct((T, D_MODEL), jnp.float32),
            jax.ShapeDtypeStruct((T, D_QP), jnp.bfloat16),
            jax.ShapeDtypeStruct((T, D_KV), jnp.bfloat16),
        ],
        compiler_params=pltpu.CompilerParams(
            dimension_semantics=("arbitrary",), vmem_limit_bytes=VMEM_LIMIT),
        name="combine_proj",
    )(xmid, rinfo, ys_local, g1, w_in_k)


def _combine_final(xmid, rinfo, ys_local, g):
    T = xmid.shape[0]
    return pl.pallas_call(
        _combine_final_kernel,
        grid=(T // TOK_TILE,),
        in_specs=[
            _tile_rows(TOK_TILE, D_MODEL), _tile_rows(TOK_TILE, LANES), _tile_rows(LOCAL_ROWS, D_MODEL),
            pl.BlockSpec((1, D_MODEL), lambda i: (0, 0)),
        ],
        out_specs=_tile_rows(TOK_TILE, D_MODEL),
        out_shape=jax.ShapeDtypeStruct((T, D_MODEL), jnp.float32),
        compiler_params=pltpu.CompilerParams(
            dimension_semantics=("arbitrary",), vmem_limit_bytes=VMEM_LIMIT),
        name="combine",
    )(xmid, rinfo, ys_local, g)


def _pair_heads(a, axis):
    shape = a.shape
    split = shape[:axis] + (SWA_KV_HEADS, SWA_REP, HEAD_DIM) + shape[axis + 1:]
    return jnp.swapaxes(a.reshape(split), axis, axis + 1).reshape(shape)


def _in_proj_weight(w):
    off_bq = 3 * D_A
    return jnp.concatenate(
        [w[..., 0:D_A] * QK_SCALE, _pair_heads(w[..., off_bq:off_bq + D_B], w.ndim - 1) * QK_SCALE,
         w[..., D_A:off_bq], w[..., off_bq + D_B:]], axis=-1).astype(jnp.bfloat16)


def _na_variants_kernel(full_ref, out_ref):
    for k in range(NA_ROWS):
        lo = (NA_ROWS - 1 - k) * GRID_W
        out_ref[k] = full_ref[:, lo:lo + NA_ROWS * GRID_W]


def _na_bias_table(rel_bias):
    depth = rel_bias.shape[0]
    c = np.arange(GRID_W)[:, None]
    cp = np.arange(GRID_W)[None, :]
    cs = np.clip(c - NA_COLS // 2, 0, GRID_W - NA_COLS)
    valid = (cp >= cs) & (cp < cs + NA_COLS)
    d = np.arange(2 * NA_COLS - 1)[:, None, None]
    col_sel = ((cp - c + (NA_COLS - 1))[None] == d) & valid[None]
    full = jnp.einsum("lhrd,dcm->lhcrm", rel_bias.astype(jnp.float32), jnp.asarray(col_sel, jnp.float32),
                      precision=lax.Precision.HIGHEST)
    full = jnp.where(jnp.asarray(valid)[None, None, :, None, :], full * LOG2E, NEG)
    n_off = 2 * NA_ROWS - 1
    full = full.reshape(depth, NA_HEADS * GRID_W, n_off * GRID_W)
    return pl.pallas_call(
        _na_variants_kernel,
        grid=(depth,),
        in_specs=[pl.BlockSpec((None, NA_HEADS * GRID_W, n_off * GRID_W), lambda l: (l, 0, 0))],
        out_specs=pl.BlockSpec((None, NA_ROWS, NA_HEADS * GRID_W, NA_ROWS * GRID_W), lambda l: (l, 0, 0, 0)),
        out_shape=jax.ShapeDtypeStruct((depth, NA_ROWS, NA_HEADS * GRID_W, NA_ROWS * GRID_W), jnp.float32),
        compiler_params=pltpu.CompilerParams(
            dimension_semantics=("arbitrary",), vmem_limit_bytes=VMEM_LIMIT),
        name="na_bias_variants",
    )(full)


def _swa_bias_table():
    slopes = (2.0 ** (-8.0 * np.arange(1, SWA_Q_HEADS + 1) / SWA_Q_HEADS)).astype(np.float32)
    qi = np.arange(SWA_BLOCK)[:, None]
    ki = np.arange(3 * SWA_BLOCK)[None, :]
    variants = []
    for v in range(3):
        dist = np.abs(ki - qi - v * SWA_BLOCK).astype(np.float32)
        tab = np.where(dist <= SWA_WINDOW, -slopes[:, None, None] * dist[None] * LOG2E, np.float32(NEG))
        variants.append(tab.reshape(SWA_Q_HEADS * SWA_BLOCK, 3 * SWA_BLOCK).astype(np.float32))
    return jnp.asarray(np.stack(variants))


def _block_diag(pool_w):
    depth, n = pool_w.shape[0:2]
    eye = jnp.asarray(np.eye(n, dtype=np.float32))
    out = pool_w[:, :, :, None, :] * eye[None, :, None, :, None]
    return out.reshape(depth, n * POOL_GROUP_DIM, n * POOL_GROUP_DIM)


def _router_weights(rg_w, rg_b, re_w, re_b):
    def lanes(g, e):
        gap = jnp.zeros(g.shape[:-1] + (ROUTE_LANE0 - N_GROUPS,), jnp.float32)
        tail = jnp.zeros(g.shape[:-1] + (LANES - ROUTE_LANE0 - N_EXPERTS,), jnp.float32)
        return jnp.concatenate([g.astype(jnp.float32), gap, e.astype(jnp.float32), tail], axis=-1)

    return lanes(rg_w, re_w).astype(jnp.bfloat16), lanes(rg_b, re_b)[:, None, :]


def _dispatch_tables(cnt, n_tiles):
    nb = cnt.shape[0] // SUBLANES
    n = cnt.reshape(nb, SUBLANES, LANES)[:, 0, ROUTE_LANE0:ROUTE_LANE0 + N_EXPERTS].astype(jnp.int32)
    g = (n + (CHUNK - 1)) // CHUNK
    l_end = jnp.cumsum(g, axis=1)
    l_off = l_end - g
    c_end = jnp.cumsum(g, axis=0)
    c_off = c_end - g
    tot = c_end[-1]
    tiles = (tot + (TILE_CHUNKS - 1)) // TILE_CHUNKS
    t_end = jnp.cumsum(tiles)
    t_off = t_end - tiles
    n_used = t_end[-1:]

    experts = jnp.arange(N_EXPERTS, dtype=jnp.int32)
    tile_ids = jnp.arange(n_tiles, dtype=jnp.int32)
    tile_expert = jnp.minimum(jnp.sum((t_end[None, :] <= tile_ids[:, None]).astype(jnp.int32), axis=1),
                              N_EXPERTS - 1)
    oh_te = (tile_expert[:, None] == experts[None, :]).astype(jnp.int32)
    left = jnp.sum(oh_te * (tot + t_off * TILE_CHUNKS)[None, :], axis=1) - tile_ids * TILE_CHUNKS
    n_valid = jnp.clip(left, 0, TILE_CHUNKS)
    has_rows = tiles > 0
    first = ((tile_ids == jnp.sum(oh_te * t_off[None, :], axis=1)) & (n_valid > 0)).astype(jnp.int32)
    wslot = jnp.sum(oh_te * ((jnp.cumsum(has_rows.astype(jnp.int32)) - 1) % 2)[None, :], axis=1)
    later = (experts[None, :] > experts[:, None]) & has_rows[None, :]
    nxt = jnp.min(jnp.where(later, experts[None, :], N_EXPERTS), axis=1)
    next_expert = jnp.sum(oh_te * jnp.where(nxt < N_EXPERTS, nxt, -1)[None, :], axis=1)
    tile_tables = (tile_expert, n_valid // SUB_CHUNKS, n_valid % SUB_CHUNKS, first, wslot, next_expert)

    q = jnp.arange(n_tiles * TILE_CHUNKS, dtype=jnp.int32)
    tile_q = q // TILE_CHUNKS
    oh_e = (jnp.repeat(tile_expert, TILE_CHUNKS)[:, None] == experts[None, :]).astype(jnp.int32)
    ro = q - jnp.sum(oh_e * t_off[None, :], axis=1) * TILE_CHUNKS
    valid = (ro < jnp.sum(oh_e * tot[None, :], axis=1)) & (tile_q < n_used[0])
    cols = jnp.dot(jnp.concatenate([c_end, c_off, l_off], axis=0).astype(jnp.float32),
                   oh_e.T.astype(jnp.float32), precision=lax.Precision.HIGHEST).astype(jnp.int32)
    c_end_q, c_off_q, l_off_q = cols[0:nb], cols[nb:2 * nb], cols[2 * nb:3 * nb]
    b_q = jnp.minimum(jnp.sum((c_end_q <= ro[None, :]).astype(jnp.int32), axis=0), nb - 1)
    oh_b = (jnp.arange(nb, dtype=jnp.int32)[:, None] == b_q[None, :]).astype(jnp.int32)
    src = b_q * LOCAL_CHUNKS + jnp.sum(oh_b * (l_off_q + ro[None, :] - c_off_q), axis=0)
    chunk_src = jnp.where(valid, src, LOCAL_CHUNKS - 1)

    return tile_tables, chunk_src


def kernel(x, norm1_g, w_in, nat_bias, swa_sink, pool_w, pool_scale, w_out, norm2_g, router_g_w,
           router_g_b, router_e_w, router_e_b, expert_w_gate, expert_w_up, expert_w_down, final_g):
    batch, seq_len, _ = x.shape
    depth = w_in.shape[0]
    T = batch * seq_len
    assert seq_len % TOK_TILE == 0 and TOK_TILE % SWA_BLOCK == 0 and TOK_TILE % GRID_W == 0
    assert seq_len // TOK_TILE >= WIN_TILES and seq_len // SWA_BLOCK >= 2 and POOL_WINDOWS == (2, 4, 8, 16)
    max_chunks = (2 * T) // CHUNK + (T // TOK_TILE) * N_EXPERTS
    n_tiles = max_chunks // TILE_CHUNKS + N_EXPERTS

    swb = _swa_bias_table()
    tri = jnp.asarray(np.tril(np.ones((TOK_TILE, TOK_TILE), np.float32), -1)).astype(jnp.bfloat16)
    utri = jnp.asarray(np.triu(np.ones((LANES, LANES), np.float32), 1)).astype(jnp.bfloat16)

    w_in_k = _in_proj_weight(w_in)
    nab = _na_bias_table(nat_bias)
    sinkcol = jnp.broadcast_to((swa_sink.astype(jnp.float32) * LOG2E)[:, :, None, None],
                               (depth, SWA_Q_HEADS, SWA_BLOCK, LANES)).reshape(depth, SWA_Q_HEADS * SWA_BLOCK, LANES)
    poolw = _block_diag(pool_w).astype(jnp.bfloat16)
    pools = pool_scale.reshape(depth, 1, D_C).astype(jnp.float32)
    rw, rb = _router_weights(router_g_w, router_g_b, router_e_w, router_e_b)
    g1 = norm1_g.reshape(depth, 1, D_MODEL)
    g2 = norm2_g.reshape(depth, 1, D_MODEL)

    x2 = x.reshape(T, D_MODEL)
    qp, kv = _norm_proj(0, x2, g1, w_in_k)
    for l in range(depth):
        xmid, xs_local, rinfo, cnt = _mixer(
            l, x2, qp, kv, (nab, sinkcol, poolw, pools, w_out, g2, rw, rb), (swb, tri, utri),
            seq_len=seq_len)
        tile_tables, chunk_src = _dispatch_tables(cnt, n_tiles)
        ys_local = _experts(l, tile_tables, chunk_src, xs_local, expert_w_gate, expert_w_up, expert_w_down)
        if l + 1 < depth:
            x2, qp, kv = _combine_proj(l + 1, xmid, rinfo, ys_local, g1, w_in_k)
        else:
            x2 = _combine_final(xmid, rinfo, ys_local, final_g.reshape(1, D_MODEL))
    return x2.reshape(batch, seq_len, D_MODEL)
```

```python
import functools

import jax
import jax.numpy as jnp
import numpy as np
from jax import lax
from jax.experimental import pallas as pl
from jax.experimental.pallas import tpu as pltpu

D_MODEL = 1024
GRID_W = 64
HEAD_DIM = 64
NA_HEADS = 4
NA_ROWS = 8
NA_COLS = 16
SWA_Q_HEADS = 8
SWA_KV_HEADS = 2
SWA_REP = SWA_Q_HEADS // SWA_KV_HEADS
SWA_WINDOW = 128
SWA_BLOCK = 128
POOL_WINDOWS = (2, 4, 8, 16)
POOL_GROUP_DIM = 64
D_A = NA_HEADS * HEAD_DIM
D_B = SWA_Q_HEADS * HEAD_DIM
D_BKV = SWA_KV_HEADS * HEAD_DIM
D_C = len(POOL_WINDOWS) * POOL_GROUP_DIM
D_MIX = D_A + D_B + D_C
D_QP = D_A + D_B
D_KV = 2 * D_A + 2 * D_BKV + D_C
D_IN = D_QP + D_KV
N_GROUPS = 4
EXPERTS_PER_GROUP = 8
N_EXPERTS = N_GROUPS * EXPERTS_PER_GROUP
D_EXPERT = 256
RMS_EPS = 1e-6
NEG = -1e30
LOG2E = 1.4426950408889634
QK_SCALE = HEAD_DIM ** -0.5 * LOG2E

LANES = 128
SUBLANES = 8

TOK_TILE = 512
PROJ_TILE = 1024
EXP_SUB = 256
EXP_SUBS = 4
EXP_TILE = EXP_SUB * EXP_SUBS
CHUNK = 16
LOCAL_CHUNKS = (2 * TOK_TILE + N_EXPERTS * (CHUNK - 1)) // CHUNK + 2
LOCAL_ROWS = LOCAL_CHUNKS * CHUNK
SUB_CHUNKS = EXP_SUB // CHUNK
TILE_CHUNKS = EXP_TILE // CHUNK
GATHER_AHEAD = 2
GATHER_SLOTS = GATHER_AHEAD + 1
ROUTE_LANE0 = 8
INFO_ROW1, INFO_ROW2, INFO_GATE1, INFO_GATE2 = 0, 1, 2, 3
NA_ROWS_PER_STEP = 8
SWA_BLOCKS_PER_STEP = 4
HALO = max(POOL_WINDOWS) // 2
WIN_TILES = 3
VMEM_LIMIT = 56 * 1024 * 1024


def _rmsnorm_f32(x, g):
    return x * lax.rsqrt(jnp.mean(x * x, axis=-1, keepdims=True) + RMS_EPS) * g


def _norm_proj_kernel(x_ref, g_ref, w_ref, qp_ref, kv_ref):
    xn = _rmsnorm_f32(x_ref[...], g_ref[...]).astype(jnp.bfloat16)
    proj = jnp.dot(xn, w_ref[...], preferred_element_type=jnp.float32)
    qp_ref[...] = proj[:, :D_QP].astype(jnp.bfloat16)
    kv_ref[...] = proj[:, D_QP:].astype(jnp.bfloat16)


def _norm_proj(layer, x2, g, w):
    T = x2.shape[0]
    return pl.pallas_call(
        _norm_proj_kernel,
        grid=(T // PROJ_TILE,),
        in_specs=[
            pl.BlockSpec((PROJ_TILE, D_MODEL), lambda i: (i, 0)),
            pl.BlockSpec((None, 1, D_MODEL), lambda i: (layer, 0, 0)),
            pl.BlockSpec((None, D_MODEL, D_IN), lambda i: (layer, 0, 0), pipeline_mode=pl.Buffered(1)),
        ],
        out_specs=[
            pl.BlockSpec((PROJ_TILE, D_QP), lambda i: (i, 0)),
            pl.BlockSpec((PROJ_TILE, D_KV), lambda i: (i, 0)),
        ],
        out_shape=[
            jax.ShapeDtypeStruct((T, D_QP), jnp.bfloat16),
            jax.ShapeDtypeStruct((T, D_KV), jnp.bfloat16),
        ],
        compiler_params=pltpu.CompilerParams(
            dimension_semantics=("arbitrary",), vmem_limit_bytes=VMEM_LIMIT),
        name="norm_proj",
    )(x2, g, w)


KW_AK, KW_AV, KW_BK, KW_BV = 0, D_A, 2 * D_A, 2 * D_A + D_BKV
KW_COLS = 2 * D_A + 2 * D_BKV
KV_CU = KW_COLS


def _mixer_kernel(x_ref, qp_ref, kwin_ref, nab_ref, swb_ref, sink_ref,
                  poolw_ref, pools_ref, wout_ref, g2_ref, rw_ref, rb_ref, tri_ref, utri_ref,
                  xmid_ref, xs_ref, rinfo_ref, cnt_ref,
                  uwin, mix, xn_scr, logit_scr, *, seq_len):
    t = pl.program_id(0)
    nblk = seq_len // TOK_TILE
    i = jnp.minimum(t, pl.num_programs(0) - 2) % nblk
    rows_per_tile = TOK_TILE // GRID_W
    grid_rows = seq_len // GRID_W

    @pl.when(t == 0)
    def _():
        xn_scr[...] = jnp.zeros_like(xn_scr)
        logit_scr[...] = jnp.zeros_like(logit_scr)

    cur_off = pl.multiple_of((i - jnp.clip(i - 1, 0, nblk - WIN_TILES)) * TOK_TILE, TOK_TILE)

    lane_a = lax.broadcasted_iota(jnp.int32, (GRID_W, D_A), 1) // HEAD_DIM

    def na_row(rr):
        r = i * rows_per_tile + rr
        rs = jnp.clip(r - NA_ROWS // 2, 0, grid_rows - NA_ROWS)
        variant = r - rs
        start = pl.multiple_of(cur_off + (rs - i * rows_per_tile) * GRID_W, GRID_W)
        q0 = pl.multiple_of(rr * GRID_W, GRID_W)
        q = qp_ref[pl.ds(q0, GRID_W), 0:D_A]
        zero = jnp.zeros_like(q)
        qs = jnp.concatenate([jnp.where(lane_a == h, q, zero) for h in range(NA_HEADS)], axis=0)
        kw = kwin_ref[pl.ds(start, NA_ROWS * GRID_W), KW_AK:KW_AK + D_A]
        vw = kwin_ref[pl.ds(start, NA_ROWS * GRID_W), KW_AV:KW_AV + D_A]
        s = lax.dot_general(qs, kw, (((1,), (1,)), ((), ())), preferred_element_type=jnp.float32)
        s = s + nab_ref[variant]
        m = jnp.max(s, axis=-1, keepdims=True)
        p = jnp.exp2(s - m)
        l = jnp.sum(p, axis=-1, keepdims=True)
        pv = jnp.dot(p.astype(jnp.bfloat16), vw, preferred_element_type=jnp.float32)
        pv = pv * (1.0 / l)
        o = jnp.zeros((GRID_W, D_A), jnp.float32)
        for h in range(NA_HEADS):
            o = o + jnp.where(lane_a == h, pv[h * GRID_W:(h + 1) * GRID_W, :], 0.0)
        mix[pl.ds(q0, GRID_W), 0:D_A] = o.astype(jnp.bfloat16)

    def na_step(it, c):
        for k in range(NA_ROWS_PER_STEP):
            na_row(it * NA_ROWS_PER_STEP + k)
        return c

    lax.fori_loop(0, rows_per_tile // NA_ROWS_PER_STEP, na_step, 0)

    lane_b = lax.broadcasted_iota(jnp.int32, (SWA_BLOCK, LANES), 1) // HEAD_DIM
    blocks_per_tile = TOK_TILE // SWA_BLOCK
    nblocks = seq_len // SWA_BLOCK

    ones_v = jnp.ones((3 * SWA_BLOCK, LANES), jnp.bfloat16)

    def swa_step(sb, c):
        n = i * blocks_per_tile + sb
        variant = jnp.where(n == 0, 0, jnp.where(n == nblocks - 1, 2, 1))
        q0 = pl.multiple_of(sb * SWA_BLOCK, SWA_BLOCK)
        k0 = pl.multiple_of(cur_off + (sb - variant) * SWA_BLOCK, SWA_BLOCK)
        kw = kwin_ref[pl.ds(k0, 3 * SWA_BLOCK), KW_BK:KW_BK + D_BKV]
        vaug = jnp.concatenate([kwin_ref[pl.ds(k0, 3 * SWA_BLOCK), KW_BV:KW_BV + D_BKV], ones_v], axis=1)
        outs = []
        for g in range(SWA_KV_HEADS):
            pieces = []
            for t in range(SWA_REP):
                qt = qp_ref[pl.ds(q0, SWA_BLOCK), D_A + t * LANES:D_A + (t + 1) * LANES]
                pieces.append(jnp.where(lane_b == g, qt, jnp.zeros_like(qt)))
            qs = jnp.concatenate(pieces, axis=0)
            r0 = g * SWA_REP * SWA_BLOCK
            s = lax.dot_general(qs, kw, (((1,), (1,)), ((), ())), preferred_element_type=jnp.float32)
            s = s + swb_ref[variant, r0:r0 + SWA_REP * SWA_BLOCK, :]
            sink = sink_ref[r0:r0 + SWA_REP * SWA_BLOCK, :]
            m = jnp.broadcast_to(jnp.max(s, axis=-1, keepdims=True), sink.shape)
            m = jnp.maximum(m, sink)
            p = jnp.exp2(s - jnp.concatenate([m, m, m], axis=1)).astype(jnp.bfloat16)
            pv = jnp.dot(p, vaug, preferred_element_type=jnp.float32)
            l = pv[:, LANES:2 * LANES] + jnp.exp2(sink - m)
            outs.append(pv[:, 0:LANES] * (1.0 / l))
        for t in range(SWA_REP):
            o0 = outs[0][t * SWA_BLOCK:(t + 1) * SWA_BLOCK, :]
            o1 = outs[1][t * SWA_BLOCK:(t + 1) * SWA_BLOCK, :]
            ot = jnp.where(lane_b == 0, o0, o1)
            mix[pl.ds(q0, SWA_BLOCK), D_A + t * LANES:D_A + (t + 1) * LANES] = ot.astype(jnp.bfloat16)
        return c

    def swa_pair(it, c):
        for k in range(SWA_BLOCKS_PER_STEP):
            swa_step(it * SWA_BLOCKS_PER_STEP + k, c)
        return c

    lax.fori_loop(0, blocks_per_tile // SWA_BLOCKS_PER_STEP, swa_pair, 0)

    u = kwin_ref[pl.ds(cur_off, TOK_TILE), KV_CU:KV_CU + D_C].astype(jnp.float32)
    prev_ok = (i > 0).astype(jnp.float32)
    next_ok = (i < nblk - 1).astype(jnp.float32)
    before = pl.multiple_of(jnp.maximum(cur_off - CHUNK, 0), CHUNK)
    after = pl.multiple_of(jnp.minimum(cur_off + TOK_TILE, WIN_TILES * TOK_TILE - CHUNK), CHUNK)
    u_before = kwin_ref[pl.ds(before, CHUNK), KV_CU:KV_CU + D_C].astype(jnp.float32)
    u_after = kwin_ref[pl.ds(after, CHUNK), KV_CU:KV_CU + D_C].astype(jnp.float32)
    uwin[0:HALO, :] = u_before[CHUNK - HALO:CHUNK, :] * prev_ok
    uwin[HALO:HALO + TOK_TILE, :] = u
    uwin[HALO + TOK_TILE:2 * HALO + TOK_TILE, :] = u_after[0:HALO, :] * next_ok
    n_ext = TOK_TILE + 2 * HALO
    a2 = uwin[0:n_ext - 1, :] + uwin[1:n_ext, :]
    a4 = a2[0:n_ext - 3, :] + a2[2:n_ext - 1, :]
    a8 = a4[0:n_ext - 7, :] + a4[4:n_ext - 3, :]
    a16 = a8[0:n_ext - 15, :] + a8[8:n_ext - 7, :]
    w2 = a2[7:7 + TOK_TILE, :]
    w4 = a4[6:6 + TOK_TILE, :]
    w8 = a8[4:4 + TOK_TILE, :]
    w16 = a16[0:TOK_TILE, :]
    lane_c = lax.broadcasted_iota(jnp.int32, (TOK_TILE, D_C), 1) // POOL_GROUP_DIM
    pooled = jnp.where(lane_c == 0, w2, jnp.where(lane_c == 1, w4, jnp.where(lane_c == 2, w8, w16)))
    half = jnp.where(lane_c == 0, 1, jnp.where(lane_c == 1, 2, jnp.where(lane_c == 2, 4, 8)))
    pos = i * TOK_TILE + lax.broadcasted_iota(jnp.int32, (TOK_TILE, D_C), 0)
    cnt = (jnp.minimum(pos + half, seq_len) - jnp.maximum(pos - half, 0)).astype(jnp.float32)
    d = (pooled / cnt - u).astype(jnp.bfloat16)
    oc = jnp.dot(d, poolw_ref[...], preferred_element_type=jnp.float32) * pools_ref[...]
    mix[:, D_A + D_B:D_MIX] = oc.astype(jnp.bfloat16)

    xn = xn_scr[...]
    logits = logit_scr[...]

    xm = x_ref[...] + jnp.dot(mix[...], wout_ref[...], preferred_element_type=jnp.float32)
    xmid_ref[...] = xm
    xn_new = _rmsnorm_f32(xm, g2_ref[...]).astype(jnp.bfloat16)
    xn_scr[...] = xn_new
    logit_scr[...] = jnp.dot(xn_new, rw_ref[...], preferred_element_type=jnp.float32) + rb_ref[...]

    lane = lax.broadcasted_iota(jnp.int32, (TOK_TILE, LANES), 1).astype(jnp.float32)
    is_g = lane < N_GROUPS
    gl = jnp.where(is_g, logits, NEG)
    gmax = jnp.max(gl, axis=-1, keepdims=True)
    gtop = jnp.min(jnp.where(is_g & (gl == gmax), lane, float(LANES)), axis=-1, keepdims=True)
    gprob = 1.0 / jnp.sum(jnp.exp(gl - gmax), axis=-1, keepdims=True)
    e_lo = ROUTE_LANE0 + gtop * EXPERTS_PER_GROUP
    in_grp = (lane >= e_lo) & (lane < e_lo + EXPERTS_PER_GROUP)
    el = jnp.where(in_grp, logits, NEG)
    m1 = jnp.max(el, axis=-1, keepdims=True)
    i1 = jnp.min(jnp.where(in_grp & (el == m1), lane, float(LANES)), axis=-1, keepdims=True)
    el2 = jnp.where(lane == i1, NEG, el)
    m2 = jnp.max(el2, axis=-1, keepdims=True)
    i2 = jnp.min(jnp.where(in_grp & (lane != i1) & (el2 == m2), lane, float(LANES)), axis=-1, keepdims=True)
    r21 = jnp.exp(m2 - m1)
    gate1 = gprob / (1.0 + r21)
    gate2 = gprob * r21 / (1.0 + r21)

    oh1 = lane == i1
    oh2 = lane == i2
    oh = jnp.where(oh1 | oh2, 1.0, 0.0)
    earlier = jnp.dot(tri_ref[...], oh.astype(jnp.bfloat16), preferred_element_type=jnp.float32)
    n_e = jnp.sum(oh, axis=0, keepdims=True)
    chunks_e = jnp.floor((n_e + (CHUNK - 1)) * (1.0 / CHUNK))
    seg0 = jnp.dot(jnp.broadcast_to(chunks_e, (SUBLANES, LANES)).astype(jnp.bfloat16), utri_ref[...],
                   preferred_element_type=jnp.float32)[0:1, :] * CHUNK
    base = earlier + seg0
    lp1 = jnp.sum(jnp.where(oh1, base, 0.0), axis=-1, keepdims=True)
    lp2 = jnp.sum(jnp.where(oh2, base, 0.0), axis=-1, keepdims=True)
    info = jnp.zeros((TOK_TILE, LANES), jnp.float32)
    for k, col in enumerate((lp1, lp2, gate1, gate2)):
        info = jnp.where(lane == k, col, info)
    rinfo_ref[...] = info
    cnt_ref[...] = jnp.broadcast_to(n_e, cnt_ref.shape)

    info_t = info.T
    prow = lax.broadcasted_iota(jnp.int32, (LOCAL_ROWS, TOK_TILE), 0).astype(jnp.float32)
    sel = jnp.where((prow == info_t[INFO_ROW1:INFO_ROW1 + 1, :]) | (prow == info_t[INFO_ROW2:INFO_ROW2 + 1, :]),
                    1.0, 0.0).astype(jnp.bfloat16)
    moved = jnp.dot(sel, xn, preferred_element_type=jnp.float32)
    xs_ref[...] = moved.astype(jnp.bfloat16)


def _mixer(layer, x2, qp, kv, layer_params, shared_tables, *, seq_len):
    nab, sinkcol, poolw, pools, wout, g2, rw, rb = layer_params
    swb, tri, utri = shared_tables
    T = x2.shape[0]
    nblk = seq_len // TOK_TILE
    n_tok_tiles = T // TOK_TILE

    def mixed(t):
        return jnp.minimum(t, n_tok_tiles - 1)

    def cur(t):
        return (mixed(t), 0)

    def window(t):
        seq0 = (mixed(t) // nblk) * nblk
        return (jnp.clip(mixed(t) - 1, seq0, seq0 + nblk - WIN_TILES) * TOK_TILE, 0)

    def dispatched(t):
        return (jnp.maximum(t - 1, 0), 0)

    def resident(a):
        zeros = (0,) * a.ndim
        return pl.BlockSpec(a.shape, lambda t: zeros, pipeline_mode=pl.Buffered(1))

    def resident_layer(a):
        index = (layer,) + (0,) * (a.ndim - 1)
        return pl.BlockSpec((None,) + a.shape[1:], lambda t: index, pipeline_mode=pl.Buffered(1))

    return pl.pallas_call(
        functools.partial(_mixer_kernel, seq_len=seq_len),
        grid=(n_tok_tiles + 1,),
        in_specs=[
            pl.BlockSpec((TOK_TILE, D_MODEL), cur),
            pl.BlockSpec((TOK_TILE, D_QP), cur),
            pl.BlockSpec((pl.Element(WIN_TILES * TOK_TILE), pl.Element(D_KV)), window),
            resident_layer(nab), resident(swb), resident_layer(sinkcol), resident_layer(poolw),
            resident_layer(pools), resident_layer(wout), resident_layer(g2), resident_layer(rw),
            resident_layer(rb), resident(tri), resident(utri),
        ],
        out_specs=[
            pl.BlockSpec((TOK_TILE, D_MODEL), cur),
            pl.BlockSpec((LOCAL_ROWS, D_MODEL), dispatched),
            pl.BlockSpec((TOK_TILE, LANES), dispatched),
            pl.BlockSpec((SUBLANES, LANES), dispatched),
        ],
        out_shape=[
            jax.ShapeDtypeStruct((T, D_MODEL), jnp.float32),
            jax.ShapeDtypeStruct((n_tok_tiles * LOCAL_ROWS, D_MODEL), jnp.bfloat16),
            jax.ShapeDtypeStruct((T, LANES), jnp.float32),
            jax.ShapeDtypeStruct((n_tok_tiles * SUBLANES, LANES), jnp.float32),
        ],
        scratch_shapes=[
            pltpu.VMEM((TOK_TILE + 2 * HALO, D_C), jnp.float32),
            pltpu.VMEM((TOK_TILE, D_MIX), jnp.bfloat16),
            pltpu.VMEM((TOK_TILE, D_MODEL), jnp.bfloat16),
            pltpu.VMEM((TOK_TILE, LANES), jnp.float32),
        ],
        compiler_params=pltpu.CompilerParams(
            dimension_semantics=("arbitrary",), vmem_limit_bytes=VMEM_LIMIT),
        name="mixer",
    )(x2, qp, kv, nab, swb, sinkcol, poolw, pools, wout, g2, rw, rb, tri, utri)


def _chunk_copy(src_hbm, src_chunk, dst, dst_chunk, sem):
    return pltpu.make_async_copy(
        src_hbm.at[pl.ds(pl.multiple_of(src_chunk * CHUNK, CHUNK), CHUNK)],
        dst.at[pl.ds(dst_chunk * CHUNK, CHUNK)],
        sem)


def _expert_kernel(te_ref, nfull_ref, npart_ref, first_ref, wslot_ref, nexte_ref, csrc_ref,
                   xs_hbm, wg_hbm, wu_hbm, wd_hbm, ys_hbm,
                   xbuf, obuf, wgf, wuf, wdf, wgb, wub, wdb, sem, osem, psem, wsem, *, layer):
    j = pl.program_id(0)
    nt = pl.num_programs(0)
    slot = j % GATHER_SLOTS
    oslot = j % 2

    def nsub(tile):
        return nfull_ref[tile] + (npart_ref[tile] > 0).astype(jnp.int32)

    def scatter_full(tile, os, h):
        return [pltpu.make_async_copy(
            obuf.at[os, pl.ds(c * CHUNK, CHUNK)],
            ys_hbm.at[pl.ds(pl.multiple_of(csrc_ref[tile * TILE_CHUNKS + c] * CHUNK, CHUNK), CHUNK)],
            osem.at[os, h]) for c in range(h * SUB_CHUNKS, (h + 1) * SUB_CHUNKS)]

    def scatter_part(tile, os, c):
        idx = nfull_ref[tile] * SUB_CHUNKS + c
        return pltpu.make_async_copy(
            obuf.at[os, pl.ds(pl.multiple_of(idx * CHUNK, CHUNK), CHUNK)],
            ys_hbm.at[pl.ds(pl.multiple_of(csrc_ref[tile * TILE_CHUNKS + idx] * CHUNK, CHUNK), CHUNK)],
            psem.at[os])

    def start_scatters(tile, os):
        for h in range(EXP_SUBS):
            @pl.when(h < nfull_ref[tile])
            def _():
                for cp in scatter_full(tile, os, h):
                    cp.start()

        @pl.when(npart_ref[tile] > 0)
        def _():
            for c in range(SUB_CHUNKS):
                @pl.when(c < npart_ref[tile])
                def _():
                    scatter_part(tile, os, c).start()

    def wait_scatters(tile, os):
        for h in range(EXP_SUBS):
            @pl.when(h < nfull_ref[tile])
            def _():
                pltpu.make_async_copy(obuf.at[os, pl.ds(0, EXP_SUB)], ys_hbm.at[pl.ds(0, EXP_SUB)],
                                      osem.at[os, h]).wait()

        @pl.when(npart_ref[tile] > 0)
        def _():
            for c in range(SUB_CHUNKS):
                @pl.when(c < npart_ref[tile])
                def _():
                    scatter_part(tile, os, c).wait()

    def start_gather(tile, s, h):
        for c in range(h * SUB_CHUNKS, (h + 1) * SUB_CHUNKS):
            _chunk_copy(xs_hbm, csrc_ref[tile * TILE_CHUNKS + c], xbuf.at[s], c, sem.at[s, h]).start()

    def wait_gather(s, h):
        rows = pl.ds(h * EXP_SUB, EXP_SUB)
        pltpu.make_async_copy(xs_hbm.at[pl.ds(0, EXP_SUB)], xbuf.at[s, rows], sem.at[s, h]).wait()

    def weight_copies(expert, ws):
        return [pltpu.make_async_copy(w_hbm.at[layer, expert], wbuf.at[ws], wsem.at[k, ws])
                for k, (w_hbm, wbuf) in enumerate(((wg_hbm, wgf), (wu_hbm, wuf), (wd_hbm, wdf)))]

    for h in range(EXP_SUBS):
        for first_tile in range(GATHER_AHEAD):
            @pl.when((j == 0) & (h < nsub(first_tile)))
            def _():
                start_gather(first_tile, first_tile, h)

        ahead = j + GATHER_AHEAD

        @pl.when(h < nsub(jnp.minimum(ahead, nt - 1)) * (ahead < nt).astype(jnp.int32))
        def _():
            start_gather(ahead, ahead % GATHER_SLOTS, h)

    ws = wslot_ref[j]

    @pl.when(j == 0)
    def _():
        for cp in weight_copies(te_ref[0], 0):
            cp.start()

    @pl.when(first_ref[j] > 0)
    def _():
        for cp in weight_copies(te_ref[j], ws):
            cp.wait()

        @pl.when(nexte_ref[j] >= 0)
        def _():
            for cp in weight_copies(nexte_ref[j], 1 - ws):
                cp.start()

        wgb[...] = wgf[ws].astype(jnp.bfloat16)
        wub[...] = wuf[ws].astype(jnp.bfloat16)
        wdb[...] = wdf[ws].astype(jnp.bfloat16)

    def gated_mlp(n_rows):
        xs = xbuf[slot, 0:n_rows, :]
        gate = jnp.dot(xs, wgb[...], preferred_element_type=jnp.float32)
        up = jnp.dot(xs, wub[...], preferred_element_type=jnp.float32)
        act = (gate * (1.0 / (1.0 + jnp.exp(-gate))) * up).astype(jnp.bfloat16)
        return jnp.dot(act, wdb[...], preferred_element_type=jnp.float32).astype(jnp.bfloat16)

    @pl.when(j >= 2)
    def _():
        wait_scatters(j - 2, oslot)

    for k in range(1, EXP_SUBS + 1):
        @pl.when(nsub(j) == k)
        def _():
            for h in range(k):
                wait_gather(slot, h)
            obuf[oslot, 0:k * EXP_SUB, :] = gated_mlp(k * EXP_SUB)

    start_scatters(j, oslot)

    @pl.when(j == nt - 1)
    def _():
        wait_scatters(j - 1, 1 - oslot)
        wait_scatters(j, oslot)


def _experts(layer, tile_tables, chunk_src, xs_local, wg, wu, wd):
    tile_expert, n_full, n_part, first, wslot, next_expert = tile_tables
    n_tiles = tile_expert.shape[0]
    any_space = pl.BlockSpec(memory_space=pl.ANY)
    n_prefetch = 7

    return pl.pallas_call(
        functools.partial(_expert_kernel, layer=layer),
        grid_spec=pltpu.PrefetchScalarGridSpec(
            num_scalar_prefetch=n_prefetch,
            grid=(n_tiles,),
            in_specs=[any_space] * 4,
            out_specs=any_space,
            scratch_shapes=[
                pltpu.VMEM((GATHER_SLOTS, EXP_TILE, D_MODEL), jnp.bfloat16),
                pltpu.VMEM((2, EXP_TILE, D_MODEL), jnp.bfloat16),
                pltpu.VMEM((2, D_MODEL, D_EXPERT), jnp.float32),
                pltpu.VMEM((2, D_MODEL, D_EXPERT), jnp.float32),
                pltpu.VMEM((2, D_EXPERT, D_MODEL), jnp.float32),
                pltpu.VMEM((D_MODEL, D_EXPERT), jnp.bfloat16),
                pltpu.VMEM((D_MODEL, D_EXPERT), jnp.bfloat16),
                pltpu.VMEM((D_EXPERT, D_MODEL), jnp.bfloat16),
                pltpu.SemaphoreType.DMA((GATHER_SLOTS, EXP_SUBS)),
                pltpu.SemaphoreType.DMA((2, EXP_SUBS)),
                pltpu.SemaphoreType.DMA((2,)),
                pltpu.SemaphoreType.DMA((3, 2)),
            ],
        ),
        out_shape=jax.ShapeDtypeStruct(xs_local.shape, xs_local.dtype),
        input_output_aliases={n_prefetch: 0},
        compiler_params=pltpu.CompilerParams(
            dimension_semantics=("arbitrary",), vmem_limit_bytes=VMEM_LIMIT),
        name="experts",
    )(tile_expert, n_full, n_part, first, wslot, next_expert, chunk_src, xs_local, wg, wu, wd)


def _combine_tile(xmid_ref, rinfo_ref, ys_ref):
    info = rinfo_ref[...]
    pcol = lax.broadcasted_iota(jnp.int32, (TOK_TILE, LOCAL_ROWS), 1).astype(jnp.float32)
    pick = jnp.where(pcol == info[:, INFO_ROW1:INFO_ROW1 + 1], info[:, INFO_GATE1:INFO_GATE1 + 1],
                     jnp.where(pcol == info[:, INFO_ROW2:INFO_ROW2 + 1], info[:, INFO_GATE2:INFO_GATE2 + 1], 0.0)
                     ).astype(jnp.bfloat16)
    return xmid_ref[...] + jnp.dot(pick, ys_ref[...], preferred_element_type=jnp.float32)


def _combine_final_kernel(xmid_ref, rinfo_ref, ys_ref, g_ref, out_ref):
    x = _combine_tile(xmid_ref, rinfo_ref, ys_ref)
    out_ref[...] = _rmsnorm_f32(x, g_ref[...])


def _combine_proj_kernel(xmid_ref, rinfo_ref, ys_ref, g_ref, w_ref, x_ref, qp_ref, kv_ref):
    x = _combine_tile(xmid_ref, rinfo_ref, ys_ref)
    x_ref[...] = x
    xn = _rmsnorm_f32(x, g_ref[...]).astype(jnp.bfloat16)
    proj = jnp.dot(xn, w_ref[...], preferred_element_type=jnp.float32)
    qp_ref[...] = proj[:, :D_QP].astype(jnp.bfloat16)
    kv_ref[...] = proj[:, D_QP:].astype(jnp.bfloat16)


def _tile_rows(rows, cols):
    return pl.BlockSpec((rows, cols), lambda i: (i, 0))


def _combine_proj(next_layer, xmid, rinfo, ys_local, g1, w_in_k):
    T = xmid.shape[0]
    return pl.pallas_call(
        _combine_proj_kernel,
        grid=(T // TOK_TILE,),
        in_specs=[
            _tile_rows(TOK_TILE, D_MODEL), _tile_rows(TOK_TILE, LANES), _tile_rows(LOCAL_ROWS, D_MODEL),
            pl.BlockSpec((None, 1, D_MODEL), lambda i: (next_layer, 0, 0)),
            pl.BlockSpec((None, D_MODEL, D_IN), lambda i: (next_layer, 0, 0), pipeline_mode=pl.Buffered(1)),
        ],
        out_specs=[_tile_rows(TOK_TILE, D_MODEL), _tile_rows(TOK_TILE, D_QP), _tile_rows(TOK_TILE, D_KV)],
        out_shape=[
            jax.ShapeDtypeStruct((T, D_MODEL), jnp.float32),
            jax.ShapeDtypeStruct((T, D_QP), jnp.bfloat16),
            jax.ShapeDtypeStruct((T, D_KV), jnp.bfloat16),
        ],
        compiler_params=pltpu.CompilerParams(
            dimension_semantics=("arbitrary",), vmem_limit_bytes=VMEM_LIMIT),
        name="combine_proj",
    )(xmid, rinfo, ys_local, g1, w_in_k)


def _combine_final(xmid, rinfo, ys_local, g):
    T = xmid.shape[0]
    return pl.pallas_call(
        _combine_final_kernel,
        grid=(T // TOK_TILE,),
        in_specs=[
            _tile_rows(TOK_TILE, D_MODEL), _tile_rows(TOK_TILE, LANES), _tile_rows(LOCAL_ROWS, D_MODEL),
            pl.BlockSpec((1, D_MODEL), lambda i: (0, 0)),
        ],
        out_specs=_tile_rows(TOK_TILE, D_MODEL),
        out_shape=jax.ShapeDtypeStruct((T, D_MODEL), jnp.float32),
        compiler_params=pltpu.CompilerParams(
            dimension_semantics=("arbitrary",), vmem_limit_bytes=VMEM_LIMIT),
        name="combine",
    )(xmid, rinfo, ys_local, g)


def _pair_heads(a, axis):
    shape = a.shape
    split = shape[:axis] + (SWA_KV_HEADS, SWA_REP, HEAD_DIM) + shape[axis + 1:]
    return jnp.swapaxes(a.reshape(split), axis, axis + 1).reshape(shape)


def _in_proj_weight(w):
    off_bq = 3 * D_A
    return jnp.concatenate(
        [w[..., 0:D_A] * QK_SCALE, _pair_heads(w[..., off_bq:off_bq + D_B], w.ndim - 1) * QK_SCALE,
         w[..., D_A:off_bq], w[..., off_bq + D_B:]], axis=-1).astype(jnp.bfloat16)


def _out_proj_weight(w):
    return jnp.concatenate(
        [w[:, 0:D_A], _pair_heads(w[:, D_A:D_A + D_B], 1), w[:, D_A + D_B:]], axis=1).astype(jnp.bfloat16)


def _na_variants_kernel(full_ref, out_ref):
    for k in range(NA_ROWS):
        lo = (NA_ROWS - 1 - k) * GRID_W
        out_ref[k] = full_ref[:, lo:lo + NA_ROWS * GRID_W]


def _na_bias_table(rel_bias):
    depth = rel_bias.shape[0]
    c = np.arange(GRID_W)[:, None]
    cp = np.arange(GRID_W)[None, :]
    cs = np.clip(c - NA_COLS // 2, 0, GRID_W - NA_COLS)
    valid = (cp >= cs) & (cp < cs + NA_COLS)
    d = np.arange(2 * NA_COLS - 1)[:, None, None]
    col_sel = ((cp - c + (NA_COLS - 1))[None] == d) & valid[None]
    full = jnp.einsum("lhrd,dcm->lhcrm", rel_bias.astype(jnp.float32), jnp.asarray(col_sel, jnp.float32),
                      precision=lax.Precision.HIGHEST)
    full = jnp.where(jnp.asarray(valid)[None, None, :, None, :], full * LOG2E, NEG)
    n_off = 2 * NA_ROWS - 1
    full = full.reshape(depth, NA_HEADS * GRID_W, n_off * GRID_W)
    return pl.pallas_call(
        _na_variants_kernel,
        grid=(depth,),
        in_specs=[pl.BlockSpec((None, NA_HEADS * GRID_W, n_off * GRID_W), lambda l: (l, 0, 0))],
        out_specs=pl.BlockSpec((None, NA_ROWS, NA_HEADS * GRID_W, NA_ROWS * GRID_W), lambda l: (l, 0, 0, 0)),
        out_shape=jax.ShapeDtypeStruct((depth, NA_ROWS, NA_HEADS * GRID_W, NA_ROWS * GRID_W), jnp.float32),
        compiler_params=pltpu.CompilerParams(
            dimension_semantics=("arbitrary",), vmem_limit_bytes=VMEM_LIMIT),
        name="na_bias_variants",
    )(full)


def _swa_bias_table():
    slopes = (2.0 ** (-8.0 * np.arange(1, SWA_Q_HEADS + 1) / SWA_Q_HEADS)).astype(np.float32)
    qi = np.arange(SWA_BLOCK)[:, None]
    ki = np.arange(3 * SWA_BLOCK)[None, :]
    variants = []
    for v in range(3):
        dist = np.abs(ki - qi - v * SWA_BLOCK).astype(np.float32)
        tab = np.where(dist <= SWA_WINDOW, -slopes[:, None, None] * dist[None] * LOG2E, np.float32(NEG))
        variants.append(tab.reshape(SWA_Q_HEADS * SWA_BLOCK, 3 * SWA_BLOCK).astype(np.float32))
    return jnp.asarray(np.stack(variants))


def _block_diag(pool_w):
    depth, n = pool_w.shape[0:2]
    eye = jnp.asarray(np.eye(n, dtype=np.float32))
    out = pool_w[:, :, :, None, :] * eye[None, :, None, :, None]
    return out.reshape(depth, n * POOL_GROUP_DIM, n * POOL_GROUP_DIM)


def _router_weights(rg_w, rg_b, re_w, re_b):
    def lanes(g, e):
        gap = jnp.zeros(g.shape[:-1] + (ROUTE_LANE0 - N_GROUPS,), jnp.float32)
        tail = jnp.zeros(g.shape[:-1] + (LANES - ROUTE_LANE0 - N_EXPERTS,), jnp.float32)
        return jnp.concatenate([g.astype(jnp.float32), gap, e.astype(jnp.float32), tail], axis=-1)

    return lanes(rg_w, re_w).astype(jnp.bfloat16), lanes(rg_b, re_b)[:, None, :]


def _dispatch_tables(cnt, n_tiles):
    nb = cnt.shape[0] // SUBLANES
    n = cnt.reshape(nb, SUBLANES, LANES)[:, 0, ROUTE_LANE0:ROUTE_LANE0 + N_EXPERTS].astype(jnp.int32)
    g = (n + (CHUNK - 1)) // CHUNK
    l_end = jnp.cumsum(g, axis=1)
    l_off = l_end - g
    c_end = jnp.cumsum(g, axis=0)
    c_off = c_end - g
    tot = c_end[-1]
    tiles = (tot + (TILE_CHUNKS - 1)) // TILE_CHUNKS
    t_end = jnp.cumsum(tiles)
    t_off = t_end - tiles
    n_used = t_end[-1:]

    experts = jnp.arange(N_EXPERTS, dtype=jnp.int32)
    tile_ids = jnp.arange(n_tiles, dtype=jnp.int32)
    tile_expert = jnp.minimum(jnp.sum((t_end[None, :] <= tile_ids[:, None]).astype(jnp.int32), axis=1),
                              N_EXPERTS - 1)
    oh_te = (tile_expert[:, None] == experts[None, :]).astype(jnp.int32)
    left = jnp.sum(oh_te * (tot + t_off * TILE_CHUNKS)[None, :], axis=1) - tile_ids * TILE_CHUNKS
    n_valid = jnp.clip(left, 0, TILE_CHUNKS)
    has_rows = tiles > 0
    first = ((tile_ids == jnp.sum(oh_te * t_off[None, :], axis=1)) & (n_valid > 0)).astype(jnp.int32)
    wslot = jnp.sum(oh_te * ((jnp.cumsum(has_rows.astype(jnp.int32)) - 1) % 2)[None, :], axis=1)
    later = (experts[None, :] > experts[:, None]) & has_rows[None, :]
    nxt = jnp.min(jnp.where(later, experts[None, :], N_EXPERTS), axis=1)
    next_expert = jnp.sum(oh_te * jnp.where(nxt < N_EXPERTS, nxt, -1)[None, :], axis=1)
    tile_tables = (tile_expert, n_valid // SUB_CHUNKS, n_valid % SUB_CHUNKS, first, wslot, next_expert)

    q = jnp.arange(n_tiles * TILE_CHUNKS, dtype=jnp.int32)
    tile_q = q // TILE_CHUNKS
    oh_e = (jnp.repeat(tile_expert, TILE_CHUNKS)[:, None] == experts[None, :]).astype(jnp.int32)
    ro = q - jnp.sum(oh_e * t_off[None, :], axis=1) * TILE_CHUNKS
    valid = (ro < jnp.sum(oh_e * tot[None, :], axis=1)) & (tile_q < n_used[0])
    cols = jnp.dot(jnp.concatenate([c_end, c_off, l_off], axis=0).astype(jnp.float32),
                   oh_e.T.astype(jnp.float32), precision=lax.Precision.HIGHEST).astype(jnp.int32)
    c_end_q, c_off_q, l_off_q = cols[0:nb], cols[nb:2 * nb], cols[2 * nb:3 * nb]
    b_q = jnp.minimum(jnp.sum((c_end_q <= ro[None, :]).astype(jnp.int32), axis=0), nb - 1)
    oh_b = (jnp.arange(nb, dtype=jnp.int32)[:, None] == b_q[None, :]).astype(jnp.int32)
    src = b_q * LOCAL_CHUNKS + jnp.sum(oh_b * (l_off_q + ro[None, :] - c_off_q), axis=0)
    chunk_src = jnp.where(valid, src, LOCAL_CHUNKS - 1)

    return tile_tables, chunk_src


def kernel(x, norm1_g, w_in, nat_bias, swa_sink, pool_w, pool_scale, w_out, norm2_g, router_g_w,
           router_g_b, router_e_w, router_e_b, expert_w_gate, expert_w_up, expert_w_down, final_g):
    batch, seq_len, _ = x.shape
    depth = w_in.shape[0]
    T = batch * seq_len
    assert seq_len % TOK_TILE == 0 and TOK_TILE % SWA_BLOCK == 0 and TOK_TILE % GRID_W == 0
    assert seq_len // TOK_TILE >= WIN_TILES and seq_len // SWA_BLOCK >= 2 and POOL_WINDOWS == (2, 4, 8, 16)
    max_chunks = (2 * T) // CHUNK + (T // TOK_TILE) * N_EXPERTS
    n_tiles = max_chunks // TILE_CHUNKS + N_EXPERTS

    swb = _swa_bias_table()
    tri = jnp.asarray(np.tril(np.ones((TOK_TILE, TOK_TILE), np.float32), -1)).astype(jnp.bfloat16)
    utri = jnp.asarray(np.triu(np.ones((LANES, LANES), np.float32), 1)).astype(jnp.bfloat16)

    w_in_k = _in_proj_weight(w_in)
    w_out_k = _out_proj_weight(w_out)
    nab = _na_bias_table(nat_bias)
    sinkcol = jnp.broadcast_to((swa_sink.astype(jnp.float32) * LOG2E)[:, :, None, None],
                               (depth, SWA_Q_HEADS, SWA_BLOCK, LANES)).reshape(depth, SWA_Q_HEADS * SWA_BLOCK, LANES)
    poolw = _block_diag(pool_w).astype(jnp.bfloat16)
    pools = pool_scale.reshape(depth, 1, D_C).astype(jnp.float32)
    rw, rb = _router_weights(router_g_w, router_g_b, router_e_w, router_e_b)
    g1 = norm1_g.reshape(depth, 1, D_MODEL)
    g2 = norm2_g.reshape(depth, 1, D_MODEL)

    x2 = x.reshape(T, D_MODEL)
    qp, kv = _norm_proj(0, x2, g1, w_in_k)
    for l in range(depth):
        xmid, xs_local, rinfo, cnt = _mixer(
            l, x2, qp, kv, (nab, sinkcol, poolw, pools, w_out_k, g2, rw, rb), (swb, tri, utri),
            seq_len=seq_len)
        tile_tables, chunk_src = _dispatch_tables(cnt, n_tiles)
        ys_local = _experts(l, tile_tables, chunk_src, xs_local, expert_w_gate, expert_w_up, expert_w_down)
        if l + 1 < depth:
            x2, qp, kv = _combine_proj(l + 1, xmid, rinfo, ys_local, g1, w_in_k)
        else:
            x2 = _combine_final(xmid, rinfo, ys_local, final_g.reshape(1, D_MODEL))
    return x2.reshape(batch, seq_len, D_MODEL)
```

```python
import functools

import jax
import jax.numpy as jnp
import numpy as np
from jax import lax
from jax.experimental import pallas as pl
from jax.experimental.pallas import tpu as pltpu

D_MODEL = 1024
GRID_W = 64
HEAD_DIM = 64
NA_HEADS = 4
NA_ROWS = 8
NA_COLS = 16
SWA_Q_HEADS = 8
SWA_KV_HEADS = 2
SWA_REP = SWA_Q_HEADS // SWA_KV_HEADS
SWA_WINDOW = 128
SWA_BLOCK = 128
POOL_WINDOWS = (2, 4, 8, 16)
POOL_GROUP_DIM = 64
D_A = NA_HEADS * HEAD_DIM
D_B = SWA_Q_HEADS * HEAD_DIM
D_BKV = SWA_KV_HEADS * HEAD_DIM
D_C = len(POOL_WINDOWS) * POOL_GROUP_DIM
D_MIX = D_A + D_B + D_C
D_QP = D_A + D_B
D_KV = 2 * D_A + 2 * D_BKV + D_C
D_IN = D_QP + D_KV
N_GROUPS = 4
EXPERTS_PER_GROUP = 8
N_EXPERTS = N_GROUPS * EXPERTS_PER_GROUP
D_EXPERT = 256
RMS_EPS = 1e-6
NEG = -1e30
LOG2E = 1.4426950408889634
QK_SCALE = HEAD_DIM ** -0.5 * LOG2E

LANES = 128
SUBLANES = 8

TOK_TILE = 512
PROJ_TILE = 1024
EXP_SUB = 256
EXP_SUBS = 4
EXP_TILE = EXP_SUB * EXP_SUBS
CHUNK = 16
LOCAL_CHUNKS = (2 * TOK_TILE + N_EXPERTS * (CHUNK - 1)) // CHUNK + 2
LOCAL_ROWS = LOCAL_CHUNKS * CHUNK
SUB_CHUNKS = EXP_SUB // CHUNK
TILE_CHUNKS = EXP_TILE // CHUNK
GATHER_AHEAD = 2
GATHER_SLOTS = GATHER_AHEAD + 1
ROUTE_LANE0 = 8
INFO_ROW1, INFO_ROW2, INFO_GATE1, INFO_GATE2 = 0, 1, 2, 3
NA_ROWS_PER_STEP = 8
SWA_BLOCKS_PER_STEP = 4
HALO = max(POOL_WINDOWS) // 2
WIN_TILES = 3
VMEM_LIMIT = 56 * 1024 * 1024


def _rmsnorm_f32(x, g):
    return x * lax.rsqrt(jnp.mean(x * x, axis=-1, keepdims=True) + RMS_EPS) * g


def _norm_proj_kernel(x_ref, g_ref, w_ref, qp_ref, kv_ref):
    xn = _rmsnorm_f32(x_ref[...], g_ref[...]).astype(jnp.bfloat16)
    proj = jnp.dot(xn, w_ref[...], preferred_element_type=jnp.float32)
    qp_ref[...] = proj[:, :D_QP].astype(jnp.bfloat16)
    kv_ref[...] = proj[:, D_QP:].astype(jnp.bfloat16)


def _norm_proj(layer, x2, g, w):
    T = x2.shape[0]
    return pl.pallas_call(
        _norm_proj_kernel,
        grid=(T // PROJ_TILE,),
        in_specs=[
            pl.BlockSpec((PROJ_TILE, D_MODEL), lambda i: (i, 0)),
            pl.BlockSpec((None, 1, D_MODEL), lambda i: (layer, 0, 0)),
            pl.BlockSpec((None, D_MODEL, D_IN), lambda i: (layer, 0, 0), pipeline_mode=pl.Buffered(1)),
        ],
        out_specs=[
            pl.BlockSpec((PROJ_TILE, D_QP), lambda i: (i, 0)),
            pl.BlockSpec((PROJ_TILE, D_KV), lambda i: (i, 0)),
        ],
        out_shape=[
            jax.ShapeDtypeStruct((T, D_QP), jnp.bfloat16),
            jax.ShapeDtypeStruct((T, D_KV), jnp.bfloat16),
        ],
        compiler_params=pltpu.CompilerParams(
            dimension_semantics=("arbitrary",), vmem_limit_bytes=VMEM_LIMIT),
        name="norm_proj",
    )(x2, g, w)


KW_AK, KW_AV, KW_BK, KW_BV = 0, D_A, 2 * D_A, 2 * D_A + D_BKV
KW_COLS = 2 * D_A + 2 * D_BKV
KV_CU = KW_COLS


def _mixer_kernel(x_ref, qp_ref, kwin_ref, nab_ref, swb_ref, sink_ref,
                  poolw_ref, pools_ref, wout_ref, g2_ref, rw_ref, rb_ref, tri_ref, utri_ref,
                  xmid_ref, xs_ref, rinfo_ref, cnt_ref,
                  uwin, mix, xn_scr, logit_scr, *, seq_len):
    t = pl.program_id(0)
    nblk = seq_len // TOK_TILE
    i = jnp.minimum(t, pl.num_programs(0) - 2) % nblk
    rows_per_tile = TOK_TILE // GRID_W
    grid_rows = seq_len // GRID_W

    @pl.when(t == 0)
    def _():
        xn_scr[...] = jnp.zeros_like(xn_scr)
        logit_scr[...] = jnp.zeros_like(logit_scr)

    cur_off = pl.multiple_of((i - jnp.clip(i - 1, 0, nblk - WIN_TILES)) * TOK_TILE, TOK_TILE)

    lane_a = lax.broadcasted_iota(jnp.int32, (GRID_W, D_A), 1) // HEAD_DIM

    def na_row(rr):
        r = i * rows_per_tile + rr
        rs = jnp.clip(r - NA_ROWS // 2, 0, grid_rows - NA_ROWS)
        variant = r - rs
        start = pl.multiple_of(cur_off + (rs - i * rows_per_tile) * GRID_W, GRID_W)
        q0 = pl.multiple_of(rr * GRID_W, GRID_W)
        q = qp_ref[pl.ds(q0, GRID_W), 0:D_A]
        zero = jnp.zeros_like(q)
        qs = jnp.concatenate([jnp.where(lane_a == h, q, zero) for h in range(NA_HEADS)], axis=0)
        kw = kwin_ref[pl.ds(start, NA_ROWS * GRID_W), KW_AK:KW_AK + D_A]
        vw = kwin_ref[pl.ds(start, NA_ROWS * GRID_W), KW_AV:KW_AV + D_A]
        s = lax.dot_general(qs, kw, (((1,), (1,)), ((), ())), preferred_element_type=jnp.float32)
        s = s + nab_ref[variant]
        m = jnp.max(s, axis=-1, keepdims=True)
        p = jnp.exp2(s - m)
        l = jnp.sum(p, axis=-1, keepdims=True)
        pv = jnp.dot(p.astype(jnp.bfloat16), vw, preferred_element_type=jnp.float32)
        pv = pv * (1.0 / l)
        o = jnp.zeros((GRID_W, D_A), jnp.float32)
        for h in range(NA_HEADS):
            o = o + jnp.where(lane_a == h, pv[h * GRID_W:(h + 1) * GRID_W, :], 0.0)
        mix[pl.ds(q0, GRID_W), 0:D_A] = o.astype(jnp.bfloat16)

    def na_step(it, c):
        for k in range(NA_ROWS_PER_STEP):
            na_row(it * NA_ROWS_PER_STEP + k)
        return c

    lax.fori_loop(0, rows_per_tile // NA_ROWS_PER_STEP, na_step, 0)

    lane_b = lax.broadcasted_iota(jnp.int32, (SWA_BLOCK, LANES), 1) // HEAD_DIM
    blocks_per_tile = TOK_TILE // SWA_BLOCK
    nblocks = seq_len // SWA_BLOCK

    ones_v = jnp.ones((3 * SWA_BLOCK, LANES), jnp.bfloat16)

    def swa_step(sb, c):
        n = i * blocks_per_tile + sb
        variant = jnp.where(n == 0, 0, jnp.where(n == nblocks - 1, 2, 1))
        q0 = pl.multiple_of(sb * SWA_BLOCK, SWA_BLOCK)
        k0 = pl.multiple_of(cur_off + (sb - variant) * SWA_BLOCK, SWA_BLOCK)
        kw = kwin_ref[pl.ds(k0, 3 * SWA_BLOCK), KW_BK:KW_BK + D_BKV]
        vaug = jnp.concatenate([kwin_ref[pl.ds(k0, 3 * SWA_BLOCK), KW_BV:KW_BV + D_BKV], ones_v], axis=1)
        outs = []
        for g in range(SWA_KV_HEADS):
            pieces = []
            for t in range(SWA_REP):
                qt = qp_ref[pl.ds(q0, SWA_BLOCK), D_A + t * LANES:D_A + (t + 1) * LANES]
                pieces.append(jnp.where(lane_b == g, qt, jnp.zeros_like(qt)))
            qs = jnp.concatenate(pieces, axis=0)
            r0 = g * SWA_REP * SWA_BLOCK
            s = lax.dot_general(qs, kw, (((1,), (1,)), ((), ())), preferred_element_type=jnp.float32)
            s = s + swb_ref[variant, r0:r0 + SWA_REP * SWA_BLOCK, :]
            sink = sink_ref[r0:r0 + SWA_REP * SWA_BLOCK, :]
            m = jnp.broadcast_to(jnp.max(s, axis=-1, keepdims=True), sink.shape)
            m = jnp.maximum(m, sink)
            p = jnp.exp2(s - jnp.concatenate([m, m, m], axis=1)).astype(jnp.bfloat16)
            pv = jnp.dot(p, vaug, preferred_element_type=jnp.float32)
            l = pv[:, LANES:2 * LANES] + jnp.exp2(sink - m)
            outs.append(pv[:, 0:LANES] * (1.0 / l))
        for t in range(SWA_REP):
            o0 = outs[0][t * SWA_BLOCK:(t + 1) * SWA_BLOCK, :]
            o1 = outs[1][t * SWA_BLOCK:(t + 1) * SWA_BLOCK, :]
            ot = jnp.where(lane_b == 0, o0, o1)
            mix[pl.ds(q0, SWA_BLOCK), D_A + t * LANES:D_A + (t + 1) * LANES] = ot.astype(jnp.bfloat16)
        return c

    def swa_pair(it, c):
        for k in range(SWA_BLOCKS_PER_STEP):
            swa_step(it * SWA_BLOCKS_PER_STEP + k, c)
        return c

    lax.fori_loop(0, blocks_per_tile // SWA_BLOCKS_PER_STEP, swa_pair, 0)

    u = kwin_ref[pl.ds(cur_off, TOK_TILE), KV_CU:KV_CU + D_C].astype(jnp.float32)
    prev_ok = (i > 0).astype(jnp.float32)
    next_ok = (i < nblk - 1).astype(jnp.float32)
    before = pl.multiple_of(jnp.maximum(cur_off - CHUNK, 0), CHUNK)
    after = pl.multiple_of(jnp.minimum(cur_off + TOK_TILE, WIN_TILES * TOK_TILE - CHUNK), CHUNK)
    u_before = kwin_ref[pl.ds(before, CHUNK), KV_CU:KV_CU + D_C].astype(jnp.float32)
    u_after = kwin_ref[pl.ds(after, CHUNK), KV_CU:KV_CU + D_C].astype(jnp.float32)
    uwin[0:HALO, :] = u_before[CHUNK - HALO:CHUNK, :] * prev_ok
    uwin[HALO:HALO + TOK_TILE, :] = u
    uwin[HALO + TOK_TILE:2 * HALO + TOK_TILE, :] = u_after[0:HALO, :] * next_ok
    n_ext = TOK_TILE + 2 * HALO
    a2 = uwin[0:n_ext - 1, :] + uwin[1:n_ext, :]
    a4 = a2[0:n_ext - 3, :] + a2[2:n_ext - 1, :]
    a8 = a4[0:n_ext - 7, :] + a4[4:n_ext - 3, :]
    a16 = a8[0:n_ext - 15, :] + a8[8:n_ext - 7, :]
    w2 = a2[7:7 + TOK_TILE, :]
    w4 = a4[6:6 + TOK_TILE, :]
    w8 = a8[4:4 + TOK_TILE, :]
    w16 = a16[0:TOK_TILE, :]
    lane_c = lax.broadcasted_iota(jnp.int32, (TOK_TILE, D_C), 1) // POOL_GROUP_DIM
    pooled = jnp.where(lane_c == 0, w2, jnp.where(lane_c == 1, w4, jnp.where(lane_c == 2, w8, w16)))
    half = jnp.where(lane_c == 0, 1, jnp.where(lane_c == 1, 2, jnp.where(lane_c == 2, 4, 8)))
    pos = i * TOK_TILE + lax.broadcasted_iota(jnp.int32, (TOK_TILE, D_C), 0)
    cnt = (jnp.minimum(pos + half, seq_len) - jnp.maximum(pos - half, 0)).astype(jnp.float32)
    d = (pooled / cnt - u).astype(jnp.bfloat16)
    oc = jnp.dot(d, poolw_ref[...], preferred_element_type=jnp.float32) * pools_ref[...]
    mix[:, D_A + D_B:D_MIX] = oc.astype(jnp.bfloat16)

    xn = xn_scr[...]
    logits = logit_scr[...]

    xm = x_ref[...] + jnp.dot(mix[...], wout_ref[...], preferred_element_type=jnp.float32)
    xmid_ref[...] = xm
    xn_new = _rmsnorm_f32(xm, g2_ref[...]).astype(jnp.bfloat16)
    xn_scr[...] = xn_new
    logit_scr[...] = jnp.dot(xn_new, rw_ref[...], preferred_element_type=jnp.float32) + rb_ref[...]

    lane = lax.broadcasted_iota(jnp.int32, (TOK_TILE, LANES), 1).astype(jnp.float32)
    is_g = lane < N_GROUPS
    gl = jnp.where(is_g, logits, NEG)
    gmax = jnp.max(gl, axis=-1, keepdims=True)
    gtop = jnp.min(jnp.where(is_g & (gl == gmax), lane, float(LANES)), axis=-1, keepdims=True)
    gprob = 1.0 / jnp.sum(jnp.exp(gl - gmax), axis=-1, keepdims=True)
    e_lo = ROUTE_LANE0 + gtop * EXPERTS_PER_GROUP
    in_grp = (lane >= e_lo) & (lane < e_lo + EXPERTS_PER_GROUP)
    el = jnp.where(in_grp, logits, NEG)
    m1 = jnp.max(el, axis=-1, keepdims=True)
    i1 = jnp.min(jnp.where(in_grp & (el == m1), lane, float(LANES)), axis=-1, keepdims=True)
    el2 = jnp.where(lane == i1, NEG, el)
    m2 = jnp.max(el2, axis=-1, keepdims=True)
    i2 = jnp.min(jnp.where(in_grp & (lane != i1) & (el2 == m2), lane, float(LANES)), axis=-1, keepdims=True)
    r21 = jnp.exp(m2 - m1)
    gate1 = gprob / (1.0 + r21)
    gate2 = gprob * r21 / (1.0 + r21)

    oh1 = lane == i1
    oh2 = lane == i2
    oh = jnp.where(oh1 | oh2, 1.0, 0.0)
    earlier = jnp.dot(tri_ref[...], oh.astype(jnp.bfloat16), preferred_element_type=jnp.float32)
    n_e = jnp.sum(oh, axis=0, keepdims=True)
    chunks_e = jnp.floor((n_e + (CHUNK - 1)) * (1.0 / CHUNK))
    seg0 = jnp.dot(jnp.broadcast_to(chunks_e, (SUBLANES, LANES)).astype(jnp.bfloat16), utri_ref[...],
                   preferred_element_type=jnp.float32)[0:1, :] * CHUNK
    base = earlier + seg0
    lp1 = jnp.sum(jnp.where(oh1, base, 0.0), axis=-1, keepdims=True)
    lp2 = jnp.sum(jnp.where(oh2, base, 0.0), axis=-1, keepdims=True)
    info = jnp.zeros((TOK_TILE, LANES), jnp.float32)
    for k, col in enumerate((lp1, lp2, gate1, gate2)):
        info = jnp.where(lane == k, col, info)
    rinfo_ref[...] = info
    cnt_ref[...] = jnp.broadcast_to(n_e, cnt_ref.shape)

    info_t = info.T
    prow = lax.broadcasted_iota(jnp.int32, (LOCAL_ROWS, TOK_TILE), 0).astype(jnp.float32)
    sel = jnp.where((prow == info_t[INFO_ROW1:INFO_ROW1 + 1, :]) | (prow == info_t[INFO_ROW2:INFO_ROW2 + 1, :]),
                    1.0, 0.0).astype(jnp.bfloat16)
    moved = jnp.dot(sel, xn, preferred_element_type=jnp.float32)
    xs_ref[...] = moved.astype(jnp.bfloat16)


def _mixer(layer, x2, qp, kv, layer_params, shared_tables, *, seq_len):
    nab, sinkcol, poolw, pools, wout, g2, rw, rb = layer_params
    swb, tri, utri = shared_tables
    T = x2.shape[0]
    nblk = seq_len // TOK_TILE
    n_tok_tiles = T // TOK_TILE

    def mixed(t):
        return jnp.minimum(t, n_tok_tiles - 1)

    def cur(t):
        return (mixed(t), 0)

    def window(t):
        seq0 = (mixed(t) // nblk) * nblk
        return (jnp.clip(mixed(t) - 1, seq0, seq0 + nblk - WIN_TILES) * TOK_TILE, 0)

    def dispatched(t):
        return (jnp.maximum(t - 1, 0), 0)

    def resident(a):
        zeros = (0,) * a.ndim
        return pl.BlockSpec(a.shape, lambda t: zeros, pipeline_mode=pl.Buffered(1))

    def resident_layer(a):
        index = (layer,) + (0,) * (a.ndim - 1)
        return pl.BlockSpec((None,) + a.shape[1:], lambda t: index, pipeline_mode=pl.Buffered(1))

    return pl.pallas_call(
        functools.partial(_mixer_kernel, seq_len=seq_len),
        grid=(n_tok_tiles + 1,),
        in_specs=[
            pl.BlockSpec((TOK_TILE, D_MODEL), cur),
            pl.BlockSpec((TOK_TILE, D_QP), cur),
            pl.BlockSpec((pl.Element(WIN_TILES * TOK_TILE), pl.Element(D_KV)), window),
            resident_layer(nab), resident(swb), resident_layer(sinkcol), resident_layer(poolw),
            resident_layer(pools), resident_layer(wout), resident_layer(g2), resident_layer(rw),
            resident_layer(rb), resident(tri), resident(utri),
        ],
        out_specs=[
            pl.BlockSpec((TOK_TILE, D_MODEL), cur),
            pl.BlockSpec((LOCAL_ROWS, D_MODEL), dispatched),
            pl.BlockSpec((TOK_TILE, LANES), dispatched),
            pl.BlockSpec((SUBLANES, LANES), dispatched),
        ],
        out_shape=[
            jax.ShapeDtypeStruct((T, D_MODEL), jnp.float32),
            jax.ShapeDtypeStruct((n_tok_tiles * LOCAL_ROWS, D_MODEL), jnp.bfloat16),
            jax.ShapeDtypeStruct((T, LANES), jnp.float32),
            jax.ShapeDtypeStruct((n_tok_tiles * SUBLANES, LANES), jnp.float32),
        ],
        scratch_shapes=[
            pltpu.VMEM((TOK_TILE + 2 * HALO, D_C), jnp.float32),
            pltpu.VMEM((TOK_TILE, D_MIX), jnp.bfloat16),
            pltpu.VMEM((TOK_TILE, D_MODEL), jnp.bfloat16),
            pltpu.VMEM((TOK_TILE, LANES), jnp.float32),
        ],
        compiler_params=pltpu.CompilerParams(
            dimension_semantics=("arbitrary",), vmem_limit_bytes=VMEM_LIMIT),
        name="mixer",
    )(x2, qp, kv, nab, swb, sinkcol, poolw, pools, wout, g2, rw, rb, tri, utri)


def _chunk_copy(src_hbm, src_chunk, dst, dst_chunk, sem):
    return pltpu.make_async_copy(
        src_hbm.at[pl.ds(pl.multiple_of(src_chunk * CHUNK, CHUNK), CHUNK)],
        dst.at[pl.ds(dst_chunk * CHUNK, CHUNK)],
        sem)


def _expert_kernel(te_ref, nfull_ref, npart_ref, first_ref, wslot_ref, nexte_ref, csrc_ref,
                   xs_hbm, wg_hbm, wu_hbm, wd_hbm, ys_hbm,
                   xbuf, obuf, wgf, wuf, wdf, wgb, wub, wdb, sem, osem, psem, wsem, *, layer):
    j = pl.program_id(0)
    nt = pl.num_programs(0)
    slot = j % GATHER_SLOTS
    oslot = j % 2

    def nsub(tile):
        return nfull_ref[tile] + (npart_ref[tile] > 0).astype(jnp.int32)

    def scatter_full(tile, os, h):
        return [pltpu.make_async_copy(
            obuf.at[os, pl.ds(c * CHUNK, CHUNK)],
            ys_hbm.at[pl.ds(pl.multiple_of(csrc_ref[tile * TILE_CHUNKS + c] * CHUNK, CHUNK), CHUNK)],
            osem.at[os, h]) for c in range(h * SUB_CHUNKS, (h + 1) * SUB_CHUNKS)]

    def scatter_part(tile, os, c):
        idx = nfull_ref[tile] * SUB_CHUNKS + c
        return pltpu.make_async_copy(
            obuf.at[os, pl.ds(pl.multiple_of(idx * CHUNK, CHUNK), CHUNK)],
            ys_hbm.at[pl.ds(pl.multiple_of(csrc_ref[tile * TILE_CHUNKS + idx] * CHUNK, CHUNK), CHUNK)],
            psem.at[os])

    def start_scatters(tile, os):
        for h in range(EXP_SUBS):
            @pl.when(h < nfull_ref[tile])
            def _():
                for n, cp in enumerate(scatter_full(tile, os, h)):
                    cp.start(priority=n % 2)

        @pl.when(npart_ref[tile] > 0)
        def _():
            for c in range(SUB_CHUNKS):
                @pl.when(c < npart_ref[tile])
                def _():
                    scatter_part(tile, os, c).start()

    def wait_scatters(tile, os):
        for h in range(EXP_SUBS):
            @pl.when(h < nfull_ref[tile])
            def _():
                pltpu.make_async_copy(obuf.at[os, pl.ds(0, EXP_SUB)], ys_hbm.at[pl.ds(0, EXP_SUB)],
                                      osem.at[os, h]).wait()

        @pl.when(npart_ref[tile] > 0)
        def _():
            for c in range(SUB_CHUNKS):
                @pl.when(c < npart_ref[tile])
                def _():
                    scatter_part(tile, os, c).wait()

    def start_gather(tile, s, h):
        for c in range(h * SUB_CHUNKS, (h + 1) * SUB_CHUNKS):
            _chunk_copy(xs_hbm, csrc_ref[tile * TILE_CHUNKS + c], xbuf.at[s], c, sem.at[s, h]).start(
                priority=c % 2)

    def wait_gather(s, h):
        rows = pl.ds(h * EXP_SUB, EXP_SUB)
        pltpu.make_async_copy(xs_hbm.at[pl.ds(0, EXP_SUB)], xbuf.at[s, rows], sem.at[s, h]).wait()

    def weight_copies(expert, ws):
        return [pltpu.make_async_copy(w_hbm.at[layer, expert], wbuf.at[ws], wsem.at[k, ws])
                for k, (w_hbm, wbuf) in enumerate(((wg_hbm, wgf), (wu_hbm, wuf), (wd_hbm, wdf)))]

    for h in range(EXP_SUBS):
        for first_tile in range(GATHER_AHEAD):
            @pl.when((j == 0) & (h < nsub(first_tile)))
            def _():
                start_gather(first_tile, first_tile, h)

        ahead = j + GATHER_AHEAD

        @pl.when(h < nsub(jnp.minimum(ahead, nt - 1)) * (ahead < nt).astype(jnp.int32))
        def _():
            start_gather(ahead, ahead % GATHER_SLOTS, h)

    ws = wslot_ref[j]

    @pl.when(j == 0)
    def _():
        for cp in weight_copies(te_ref[0], 0):
            cp.start()

    @pl.when(first_ref[j] > 0)
    def _():
        for cp in weight_copies(te_ref[j], ws):
            cp.wait()

        @pl.when(nexte_ref[j] >= 0)
        def _():
            for cp in weight_copies(nexte_ref[j], 1 - ws):
                cp.start()

        wgb[...] = wgf[ws].astype(jnp.bfloat16)
        wub[...] = wuf[ws].astype(jnp.bfloat16)
        wdb[...] = wdf[ws].astype(jnp.bfloat16)

    def gated_mlp(n_rows):
        xs = xbuf[slot, 0:n_rows, :]
        gate = jnp.dot(xs, wgb[...], preferred_element_type=jnp.float32)
        up = jnp.dot(xs, wub[...], preferred_element_type=jnp.float32)
        act = (gate * (1.0 / (1.0 + jnp.exp(-gate))) * up).astype(jnp.bfloat16)
        return jnp.dot(act, wdb[...], preferred_element_type=jnp.float32).astype(jnp.bfloat16)

    @pl.when(j >= 2)
    def _():
        wait_scatters(j - 2, oslot)

    for k in range(1, EXP_SUBS + 1):
        @pl.when(nsub(j) == k)
        def _():
            for h in range(k):
                wait_gather(slot, h)
            obuf[oslot, 0:k * EXP_SUB, :] = gated_mlp(k * EXP_SUB)

    start_scatters(j, oslot)

    @pl.when(j == nt - 1)
    def _():
        wait_scatters(j - 1, 1 - oslot)
        wait_scatters(j, oslot)


def _experts(layer, tile_tables, chunk_src, xs_local, wg, wu, wd):
    tile_expert, n_full, n_part, first, wslot, next_expert = tile_tables
    n_tiles = tile_expert.shape[0]
    any_space = pl.BlockSpec(memory_space=pl.ANY)
    n_prefetch = 7

    return pl.pallas_call(
        functools.partial(_expert_kernel, layer=layer),
        grid_spec=pltpu.PrefetchScalarGridSpec(
            num_scalar_prefetch=n_prefetch,
            grid=(n_tiles,),
            in_specs=[any_space] * 4,
            out_specs=any_space,
            scratch_shapes=[
                pltpu.VMEM((GATHER_SLOTS, EXP_TILE, D_MODEL), jnp.bfloat16),
                pltpu.VMEM((2, EXP_TILE, D_MODEL), jnp.bfloat16),
                pltpu.VMEM((2, D_MODEL, D_EXPERT), jnp.float32),
                pltpu.VMEM((2, D_MODEL, D_EXPERT), jnp.float32),
                pltpu.VMEM((2, D_EXPERT, D_MODEL), jnp.float32),
                pltpu.VMEM((D_MODEL, D_EXPERT), jnp.bfloat16),
                pltpu.VMEM((D_MODEL, D_EXPERT), jnp.bfloat16),
                pltpu.VMEM((D_EXPERT, D_MODEL), jnp.bfloat16),
                pltpu.SemaphoreType.DMA((GATHER_SLOTS, EXP_SUBS)),
                pltpu.SemaphoreType.DMA((2, EXP_SUBS)),
                pltpu.SemaphoreType.DMA((2,)),
                pltpu.SemaphoreType.DMA((3, 2)),
            ],
        ),
        out_shape=jax.ShapeDtypeStruct(xs_local.shape, xs_local.dtype),
        input_output_aliases={n_prefetch: 0},
        compiler_params=pltpu.CompilerParams(
            dimension_semantics=("arbitrary",), vmem_limit_bytes=VMEM_LIMIT),
        name="experts",
    )(tile_expert, n_full, n_part, first, wslot, next_expert, chunk_src, xs_local, wg, wu, wd)


def _combine_tile(xmid_ref, rinfo_ref, ys_ref):
    info = rinfo_ref[...]
    pcol = lax.broadcasted_iota(jnp.int32, (TOK_TILE, LOCAL_ROWS), 1).astype(jnp.float32)
    pick = jnp.where(pcol == info[:, INFO_ROW1:INFO_ROW1 + 1], info[:, INFO_GATE1:INFO_GATE1 + 1],
                     jnp.where(pcol == info[:, INFO_ROW2:INFO_ROW2 + 1], info[:, INFO_GATE2:INFO_GATE2 + 1], 0.0)
                     ).astype(jnp.bfloat16)
    return xmid_ref[...] + jnp.dot(pick, ys_ref[...], preferred_element_type=jnp.float32)


def _combine_final_kernel(xmid_ref, rinfo_ref, ys_ref, g_ref, out_ref):
    x = _combine_tile(xmid_ref, rinfo_ref, ys_ref)
    out_ref[...] = _rmsnorm_f32(x, g_ref[...])


def _combine_proj_kernel(xmid_ref, rinfo_ref, ys_ref, g_ref, w_ref, x_ref, qp_ref, kv_ref):
    x = _combine_tile(xmid_ref, rinfo_ref, ys_ref)
    x_ref[...] = x
    xn = _rmsnorm_f32(x, g_ref[...]).astype(jnp.bfloat16)
    proj = jnp.dot(xn, w_ref[...], preferred_element_type=jnp.float32)
    qp_ref[...] = proj[:, :D_QP].astype(jnp.bfloat16)
    kv_ref[...] = proj[:, D_QP:].astype(jnp.bfloat16)


def _tile_rows(rows, cols):
    return pl.BlockSpec((rows, cols), lambda i: (i, 0))


def _combine_proj(next_layer, xmid, rinfo, ys_local, g1, w_in_k):
    T = xmid.shape[0]
    return pl.pallas_call(
        _combine_proj_kernel,
        grid=(T // TOK_TILE,),
        in_specs=[
            _tile_rows(TOK_TILE, D_MODEL), _tile_rows(TOK_TILE, LANES), _tile_rows(LOCAL_ROWS, D_MODEL),
            pl.BlockSpec((None, 1, D_MODEL), lambda i: (next_layer, 0, 0)),
            pl.BlockSpec((None, D_MODEL, D_IN), lambda i: (next_layer, 0, 0), pipeline_mode=pl.Buffered(1)),
        ],
        out_specs=[_tile_rows(TOK_TILE, D_MODEL), _tile_rows(TOK_TILE, D_QP), _tile_rows(TOK_TILE, D_KV)],
        out_shape=[
            jax.ShapeDtypeStruct((T, D_MODEL), jnp.float32),
            jax.ShapeDtypeStruct((T, D_QP), jnp.bfloat16),
            jax.ShapeDtypeStruct((T, D_KV), jnp.bfloat16),
        ],
        compiler_params=pltpu.CompilerParams(
            dimension_semantics=("arbitrary",), vmem_limit_bytes=VMEM_LIMIT),
        name="combine_proj",
    )(xmid, rinfo, ys_local, g1, w_in_k)


def _combine_final(xmid, rinfo, ys_local, g):
    T = xmid.shape[0]
    return pl.pallas_call(
        _combine_final_kernel,
        grid=(T // TOK_TILE,),
        in_specs=[
            _tile_rows(TOK_TILE, D_MODEL), _tile_rows(TOK_TILE, LANES), _tile_rows(LOCAL_ROWS, D_MODEL),
            pl.BlockSpec((1, D_MODEL), lambda i: (0, 0)),
        ],
        out_specs=_tile_rows(TOK_TILE, D_MODEL),
        out_shape=jax.ShapeDtypeStruct((T, D_MODEL), jnp.float32),
        compiler_params=pltpu.CompilerParams(
            dimension_semantics=("arbitrary",), vmem_limit_bytes=VMEM_LIMIT),
        name="combine",
    )(xmid, rinfo, ys_local, g)


def _pair_heads(a, axis):
    shape = a.shape
    split = shape[:axis] + (SWA_KV_HEADS, SWA_REP, HEAD_DIM) + shape[axis + 1:]
    return jnp.swapaxes(a.reshape(split), axis, axis + 1).reshape(shape)


def _in_proj_weight(w):
    off_bq = 3 * D_A
    return jnp.concatenate(
        [w[..., 0:D_A] * QK_SCALE, _pair_heads(w[..., off_bq:off_bq + D_B], w.ndim - 1) * QK_SCALE,
         w[..., D_A:off_bq], w[..., off_bq + D_B:]], axis=-1).astype(jnp.bfloat16)


def _out_proj_weight(w):
    return jnp.concatenate(
        [w[:, 0:D_A], _pair_heads(w[:, D_A:D_A + D_B], 1), w[:, D_A + D_B:]], axis=1).astype(jnp.bfloat16)


def _na_variants_kernel(full_ref, out_ref):
    for k in range(NA_ROWS):
        lo = (NA_ROWS - 1 - k) * GRID_W
        out_ref[k] = full_ref[:, lo:lo + NA_ROWS * GRID_W]


def _na_bias_table(rel_bias):
    depth = rel_bias.shape[0]
    c = np.arange(GRID_W)[:, None]
    cp = np.arange(GRID_W)[None, :]
    cs = np.clip(c - NA_COLS // 2, 0, GRID_W - NA_COLS)
    valid = (cp >= cs) & (cp < cs + NA_COLS)
    d = np.arange(2 * NA_COLS - 1)[:, None, None]
    col_sel = ((cp - c + (NA_COLS - 1))[None] == d) & valid[None]
    full = jnp.einsum("lhrd,dcm->lhcrm", rel_bias.astype(jnp.float32), jnp.asarray(col_sel, jnp.float32),
                      precision=lax.Precision.HIGHEST)
    full = jnp.where(jnp.asarray(valid)[None, None, :, None, :], full * LOG2E, NEG)
    n_off = 2 * NA_ROWS - 1
    full = full.reshape(depth, NA_HEADS * GRID_W, n_off * GRID_W)
    return pl.pallas_call(
        _na_variants_kernel,
        grid=(depth,),
        in_specs=[pl.BlockSpec((None, NA_HEADS * GRID_W, n_off * GRID_W), lambda l: (l, 0, 0))],
        out_specs=pl.BlockSpec((None, NA_ROWS, NA_HEADS * GRID_W, NA_ROWS * GRID_W), lambda l: (l, 0, 0, 0)),
        out_shape=jax.ShapeDtypeStruct((depth, NA_ROWS, NA_HEADS * GRID_W, NA_ROWS * GRID_W), jnp.float32),
        compiler_params=pltpu.CompilerParams(
            dimension_semantics=("arbitrary",), vmem_limit_bytes=VMEM_LIMIT),
        name="na_bias_variants",
    )(full)


def _swa_bias_table():
    slopes = (2.0 ** (-8.0 * np.arange(1, SWA_Q_HEADS + 1) / SWA_Q_HEADS)).astype(np.float32)
    qi = np.arange(SWA_BLOCK)[:, None]
    ki = np.arange(3 * SWA_BLOCK)[None, :]
    variants = []
    for v in range(3):
        dist = np.abs(ki - qi - v * SWA_BLOCK).astype(np.float32)
        tab = np.where(dist <= SWA_WINDOW, -slopes[:, None, None] * dist[None] * LOG2E, np.float32(NEG))
        variants.append(tab.reshape(SWA_Q_HEADS * SWA_BLOCK, 3 * SWA_BLOCK).astype(np.float32))
    return jnp.asarray(np.stack(variants))


def _block_diag(pool_w):
    depth, n = pool_w.shape[0:2]
    eye = jnp.asarray(np.eye(n, dtype=np.float32))
    out = pool_w[:, :, :, None, :] * eye[None, :, None, :, None]
    return out.reshape(depth, n * POOL_GROUP_DIM, n * POOL_GROUP_DIM)


def _router_weights(rg_w, rg_b, re_w, re_b):
    def lanes(g, e):
        gap = jnp.zeros(g.shape[:-1] + (ROUTE_LANE0 - N_GROUPS,), jnp.float32)
        tail = jnp.zeros(g.shape[:-1] + (LANES - ROUTE_LANE0 - N_EXPERTS,), jnp.float32)
        return jnp.concatenate([g.astype(jnp.float32), gap, e.astype(jnp.float32), tail], axis=-1)

    return lanes(rg_w, re_w).astype(jnp.bfloat16), lanes(rg_b, re_b)[:, None, :]


def _dispatch_tables(cnt, n_tiles):
    nb = cnt.shape[0] // SUBLANES
    n = cnt.reshape(nb, SUBLANES, LANES)[:, 0, ROUTE_LANE0:ROUTE_LANE0 + N_EXPERTS].astype(jnp.int32)
    g = (n + (CHUNK - 1)) // CHUNK
    l_end = jnp.cumsum(g, axis=1)
    l_off = l_end - g
    c_end = jnp.cumsum(g, axis=0)
    c_off = c_end - g
    tot = c_end[-1]
    tiles = (tot + (TILE_CHUNKS - 1)) // TILE_CHUNKS
    t_end = jnp.cumsum(tiles)
    t_off = t_end - tiles
    n_used = t_end[-1:]

    experts = jnp.arange(N_EXPERTS, dtype=jnp.int32)
    tile_ids = jnp.arange(n_tiles, dtype=jnp.int32)
    tile_expert = jnp.minimum(jnp.sum((t_end[None, :] <= tile_ids[:, None]).astype(jnp.int32), axis=1),
                              N_EXPERTS - 1)
    oh_te = (tile_expert[:, None] == experts[None, :]).astype(jnp.int32)
    left = jnp.sum(oh_te * (tot + t_off * TILE_CHUNKS)[None, :], axis=1) - tile_ids * TILE_CHUNKS
    n_valid = jnp.clip(left, 0, TILE_CHUNKS)
    has_rows = tiles > 0
    first = ((tile_ids == jnp.sum(oh_te * t_off[None, :], axis=1)) & (n_valid > 0)).astype(jnp.int32)
    wslot = jnp.sum(oh_te * ((jnp.cumsum(has_rows.astype(jnp.int32)) - 1) % 2)[None, :], axis=1)
    later = (experts[None, :] > experts[:, None]) & has_rows[None, :]
    nxt = jnp.min(jnp.where(later, experts[None, :], N_EXPERTS), axis=1)
    next_expert = jnp.sum(oh_te * jnp.where(nxt < N_EXPERTS, nxt, -1)[None, :], axis=1)
    tile_tables = (tile_expert, n_valid // SUB_CHUNKS, n_valid % SUB_CHUNKS, first, wslot, next_expert)

    q = jnp.arange(n_tiles * TILE_CHUNKS, dtype=jnp.int32)
    tile_q = q // TILE_CHUNKS
    oh_e = (jnp.repeat(tile_expert, TILE_CHUNKS)[:, None] == experts[None, :]).astype(jnp.int32)
    ro = q - jnp.sum(oh_e * t_off[None, :], axis=1) * TILE_CHUNKS
    valid = (ro < jnp.sum(oh_e * tot[None, :], axis=1)) & (tile_q < n_used[0])
    cols = jnp.dot(jnp.concatenate([c_end, c_off, l_off], axis=0).astype(jnp.float32),
                   oh_e.T.astype(jnp.float32), precision=lax.Precision.HIGHEST).astype(jnp.int32)
    c_end_q, c_off_q, l_off_q = cols[0:nb], cols[nb:2 * nb], cols[2 * nb:3 * nb]
    b_q = jnp.minimum(jnp.sum((c_end_q <= ro[None, :]).astype(jnp.int32), axis=0), nb - 1)
    oh_b = (jnp.arange(nb, dtype=jnp.int32)[:, None] == b_q[None, :]).astype(jnp.int32)
    src = b_q * LOCAL_CHUNKS + jnp.sum(oh_b * (l_off_q + ro[None, :] - c_off_q), axis=0)
    chunk_src = jnp.where(valid, src, LOCAL_CHUNKS - 1)

    return tile_tables, chunk_src


def kernel(x, norm1_g, w_in, nat_bias, swa_sink, pool_w, pool_scale, w_out, norm2_g, router_g_w,
           router_g_b, router_e_w, router_e_b, expert_w_gate, expert_w_up, expert_w_down, final_g):
    batch, seq_len, _ = x.shape
    depth = w_in.shape[0]
    T = batch * seq_len
    assert seq_len % TOK_TILE == 0 and TOK_TILE % SWA_BLOCK == 0 and TOK_TILE % GRID_W == 0
    assert seq_len // TOK_TILE >= WIN_TILES and seq_len // SWA_BLOCK >= 2 and POOL_WINDOWS == (2, 4, 8, 16)
    max_chunks = (2 * T) // CHUNK + (T // TOK_TILE) * N_EXPERTS
    n_tiles = max_chunks // TILE_CHUNKS + N_EXPERTS

    swb = _swa_bias_table()
    tri = jnp.asarray(np.tril(np.ones((TOK_TILE, TOK_TILE), np.float32), -1)).astype(jnp.bfloat16)
    utri = jnp.asarray(np.triu(np.ones((LANES, LANES), np.float32), 1)).astype(jnp.bfloat16)

    w_in_k = _in_proj_weight(w_in)
    w_out_k = _out_proj_weight(w_out)
    nab = _na_bias_table(nat_bias)
    sinkcol = jnp.broadcast_to((swa_sink.astype(jnp.float32) * LOG2E)[:, :, None, None],
                               (depth, SWA_Q_HEADS, SWA_BLOCK, LANES)).reshape(depth, SWA_Q_HEADS * SWA_BLOCK, LANES)
    poolw = _block_diag(pool_w).astype(jnp.bfloat16)
    pools = pool_scale.reshape(depth, 1, D_C).astype(jnp.float32)
    rw, rb = _router_weights(router_g_w, router_g_b, router_e_w, router_e_b)
    g1 = norm1_g.reshape(depth, 1, D_MODEL)
    g2 = norm2_g.reshape(depth, 1, D_MODEL)

    x2 = x.reshape(T, D_MODEL)
    qp, kv = _norm_proj(0, x2, g1, w_in_k)
    for l in range(depth):
        xmid, xs_local, rinfo, cnt = _mixer(
            l, x2, qp, kv, (nab, sinkcol, poolw, pools, w_out_k, g2, rw, rb), (swb, tri, utri),
            seq_len=seq_len)
        tile_tables, chunk_src = _dispatch_tables(cnt, n_tiles)
        ys_local = _experts(l, tile_tables, chunk_src, xs_local, expert_w_gate, expert_w_up, expert_w_down)
        if l + 1 < depth:
            x2, qp, kv = _combine_proj(l + 1, xmid, rinfo, ys_local, g1, w_in_k)
        else:
            x2 = _combine_final(xmid, rinfo, ys_local, final_g.reshape(1, D_MODEL))
    return x2.reshape(batch, seq_len, D_MODEL)
```
